```python
import math
import jax, jax.numpy as jnp
from jax import lax
import numpy as np

D_MODEL = 1024
BATCH = 2
SEQ = 8192
DEPTH = 1
DEC_BATCH = 128
DEC_SEQ = 8
PAST_LEN = 2048
PAGE_SIZE = 128

D_MIX = D_MODEL
POOL_WIDTH = D_MIX // 2
POOL_WINDOWS = (2, 4, 8, 16)
N_POOL_GROUPS = len(POOL_WINDOWS)
POOL_GROUP = POOL_WIDTH // N_POOL_GROUPS
POOL_STATE = max(POOL_WINDOWS) - 1
NSA_WIDTH = D_MIX - POOL_WIDTH
NSA_HEAD_DIM = 64
NSA_HEADS = NSA_WIDTH // NSA_HEAD_DIM
NSA_KV_HEADS = 2
NSA_HPG = NSA_HEADS // NSA_KV_HEADS
CMP_LEN = 32
CMP_STRIDE = 16
CMP_HIDDEN = 256
SEL_LEN = 64
SEL_TOP = 16
WINDOW = 512
Q_BLOCK = 128
KV_WIDTH = 2 * NSA_KV_HEADS * NSA_HEAD_DIM
N_BRANCH = 3
IN_WIDTH = POOL_WIDTH + NSA_WIDTH + N_BRANCH * KV_WIDTH + N_BRANCH * NSA_HEADS
ATTN_SCALE = NSA_HEAD_DIM ** -0.5
FORCED_SCORE = 1e4
NEG = -1e30
MEM_LEN = 256
MEM_HEADS = 4
MEM_HEAD_DIM = D_MODEL // MEM_HEADS
N_EXPERTS = 32
TOP_K = 4
D_FF = D_MODEL
SWIGLU_LIMIT = 7.0
SWIGLU_ALPHA = 1.702
MOE_BLOCK = 128
DN_ALPHA = (2.0 * DEPTH) ** 0.25
DN_BETA = (8.0 * DEPTH) ** -0.25
LN_EPS = 1e-5

kernel_name = "hymba_pool_nsa_moe_deepnorm_step"


def layer_norm(x, g, b):
    xf = x.astype(jnp.float32)
    mu = xf.mean(-1, keepdims=True)
    var = jnp.square(xf - mu).mean(-1, keepdims=True)
    return ((xf - mu) * lax.rsqrt(var + LN_EPS) * g.astype(jnp.float32) + b.astype(jnp.float32)).astype(x.dtype)


def alibi_slopes():
    return jnp.exp2(-8.0 * jnp.arange(1, NSA_HEADS + 1, dtype=jnp.float32) / NSA_HEADS)


def split_mixer_inputs(u):
    B, L = u.shape[:2]
    o1 = POOL_WIDTH
    o2 = o1 + NSA_WIDTH
    o3 = o2 + KV_WIDTH
    o4 = o3 + KV_WIDTH
    o5 = o4 + KV_WIDTH
    kv_shape = (B, L, 2, NSA_KV_HEADS, NSA_HEAD_DIM)
    u_pool = u[..., :o1]
    q = u[..., o1:o2].reshape(B, L, NSA_HEADS, NSA_HEAD_DIM)
    kv_c = u[..., o2:o3].reshape(kv_shape)
    kv_s = u[..., o3:o4].reshape(kv_shape)
    kv_w = u[..., o4:o5].reshape(kv_shape)
    gates = jax.nn.sigmoid(u[..., o5:]).reshape(B, L, NSA_HEADS, N_BRANCH)
    return u_pool, q, kv_c, kv_s, kv_w, gates


def pool_mix(u_ext, pos, pool_w, pool_scale):
    L = pos.shape[0]
    P = POOL_STATE
    c = jnp.pad(jnp.cumsum(u_ext.astype(jnp.float32), axis=1), ((0, 0), (1, 0), (0, 0)))
    u_new = u_ext[:, P:].astype(jnp.float32)
    outs = []
    for g, w in enumerate(POOL_WINDOWS):
        sl = slice(g * POOL_GROUP, (g + 1) * POOL_GROUP)
        win_sum = c[:, P + 1:P + 1 + L, sl] - c[:, P + 1 - w:P + 1 - w + L, sl]
        cnt = jnp.minimum(pos + 1, w).astype(jnp.float32)[None, :, None]
        d = (win_sum / cnt - u_new[..., sl]).astype(u_ext.dtype)
        outs.append(jnp.einsum('blc,cd->bld', d, pool_w[g]))
    return jnp.concatenate(outs, axis=-1) * pool_scale


def compress_kv(kv, pe, w1, b1, w2, b2):
    B, L = kv.shape[:2]
    nc = (L - CMP_LEN) // CMP_STRIDE + 1
    ch = kv[:, :(nc + 1) * CMP_STRIDE].reshape(B, nc + 1, CMP_STRIDE, 2, NSA_KV_HEADS, NSA_HEAD_DIM)
    pe_t = pe.transpose(1, 0, 2)[:, :, None, :]
    first = ch[:, :-1] + pe_t[:CMP_STRIDE]
    second = ch[:, 1:] + pe_t[CMP_STRIDE:]
    hid = (jnp.einsum('bnlcgd,cldh->bncgh', first, w1[:, :CMP_STRIDE])
           + jnp.einsum('bnlcgd,cldh->bncgh', second, w1[:, CMP_STRIDE:]) + b1[:, None, :])
    out = jnp.einsum('bncgh,chd->bncgd', jax.nn.gelu(hid), w2) + b2[:, None, :]
    c_end = jnp.arange(nc) * CMP_STRIDE + CMP_LEN - 1
    return out, c_end


def sel_blocks(kv):
    B, L = kv.shape[:2]
    n_sel = -(-L // SEL_LEN)
    kv = jnp.pad(kv, ((0, 0), (0, n_sel * SEL_LEN - L), (0, 0), (0, 0), (0, 0)))
    return kv.reshape(B, n_sel, SEL_LEN, 2, NSA_KV_HEADS, NSA_HEAD_DIM).transpose(0, 4, 1, 2, 3, 5)


def nsa_branches(q, q_pos, kv_c, c_end, kvb, kv_w, w_pos, slopes):
    B, Lq = q.shape[:2]
    dt = q.dtype
    qg = q.reshape(B, Lq, NSA_KV_HEADS, NSA_HPG, NSA_HEAD_DIM)
    sl = slopes.reshape(NSA_KV_HEADS, NSA_HPG)
    s_c = jnp.einsum('bqghd,bngd->bghqn', qg, kv_c[:, :, 0]).astype(jnp.float32) * ATTN_SCALE
    m_c = c_end[None, :] <= q_pos[:, None]
    p_c = jax.nn.softmax(jnp.where(m_c, s_c, NEG), axis=-1) * m_c.any(-1)[:, None]
    o_c = jnp.einsum('bghqn,bngd->bqghd', p_c.astype(dt), kv_c[:, :, 1])
    n_c = kv_c.shape[1]
    n_sel = kvb.shape[2]
    c_first = jnp.arange(n_c) * CMP_STRIDE
    b_first = jnp.arange(n_sel) * SEL_LEN
    overlap = ((c_first[:, None] < b_first[None, :] + SEL_LEN) & (c_end[:, None] >= b_first[None, :])).astype(jnp.float32)
    imp = jnp.einsum('bghqn,ns->bgqs', p_c, overlap)
    cur = q_pos // SEL_LEN
    blk = jnp.arange(n_sel)
    forced = (blk[None] == 0) | (blk[None] == cur[:, None]) | (blk[None] == cur[:, None] - 1)
    imp = jnp.where(forced, FORCED_SCORE, imp)
    imp = jnp.where(b_first[None] <= q_pos[:, None], imp, -1.0)
    _, idx = lax.top_k(imp, min(SEL_TOP, n_sel))
    bi = jnp.arange(B)[:, None, None, None]
    gi = jnp.arange(NSA_KV_HEADS)[None, :, None, None]
    sel = kvb[bi, gi, idx]
    k_pos = idx[..., None] * SEL_LEN + jnp.arange(SEL_LEN)
    d_s = (q_pos[None, None, :, None, None] - k_pos)[:, :, None]
    s_s = (jnp.einsum('bqghd,bgqnkd->bghqnk', qg, sel[..., 0, :]).astype(jnp.float32) * ATTN_SCALE
           - sl[None, :, :, None, None, None] * d_s.astype(jnp.float32))
    p_s = jax.nn.softmax(jnp.where(d_s >= 0, s_s, NEG), axis=(-2, -1))
    o_s = jnp.einsum('bghqnk,bgqnkd->bqghd', p_s.astype(dt), sel[..., 1, :])
    d_w = q_pos[:, None] - w_pos[None, :]
    m_w = (d_w >= 0) & (d_w < WINDOW) & (w_pos[None, :] >= 0)
    s_w = (jnp.einsum('bqghd,bkgd->bghqk', qg, kv_w[:, :, 0]).astype(jnp.float32) * ATTN_SCALE
           - sl[None, :, :, None, None] * d_w.astype(jnp.float32))
    p_w = jax.nn.softmax(jnp.where(m_w, s_w, NEG), axis=-1)
    o_w = jnp.einsum('bghqk,bkgd->bqghd', p_w.astype(dt), kv_w[:, :, 1])
    shape = (B, Lq, NSA_HEADS, NSA_HEAD_DIM)
    return o_c.reshape(shape), o_s.reshape(shape), o_w.reshape(shape)


def combine_nsa(o_c, o_s, o_w, gates):
    B, L = gates.shape[:2]
    o = gates[..., 0:1] * o_c + gates[..., 1:2] * o_s + gates[..., 2:3] * o_w
    return o.reshape(B, L, NSA_WIDTH)


def nsa_prompt(q, kv_c, kvb, kv_w, slopes):
    B, S = q.shape[:2]
    kv_c, c_end = kv_c
    kvw_pad = jnp.pad(kv_w, ((0, 0), (WINDOW, 0), (0, 0), (0, 0), (0, 0)))
    nqb = S // Q_BLOCK
    q_blocks = q.reshape(B, nqb, Q_BLOCK, NSA_HEADS, NSA_HEAD_DIM).swapaxes(0, 1)
    starts = jnp.arange(nqb, dtype=jnp.int32) * Q_BLOCK

    def one_block(args):
        qb, st = args
        q_pos = st + jnp.arange(Q_BLOCK, dtype=jnp.int32)
        band = lax.dynamic_slice_in_dim(kvw_pad, st, WINDOW + Q_BLOCK, axis=1)
        w_pos = st - WINDOW + jnp.arange(WINDOW + Q_BLOCK, dtype=jnp.int32)
        return nsa_branches(qb, q_pos, kv_c, c_end, kvb, band, w_pos, slopes)

    o_c, o_s, o_w = lax.map(one_block, (q_blocks, starts))
    shape = (B, S, NSA_HEADS, NSA_HEAD_DIM)
    return o_c.swapaxes(0, 1).reshape(shape), o_s.swapaxes(0, 1).reshape(shape), o_w.swapaxes(0, 1).reshape(shape)


def mem_cross_attn(h, mem_kv, wq, wo):
    B, L, _ = h.shape
    q = (h @ wq).reshape(B, L, MEM_HEADS, MEM_HEAD_DIM)
    s = jnp.einsum('blhd,bmhd->bhlm', q, mem_kv[:, :, 0]).astype(jnp.float32) * (MEM_HEAD_DIM ** -0.5)
    p = jax.nn.softmax(s, axis=-1).astype(h.dtype)
    o = jnp.einsum('bhlm,bmhd->blhd', p, mem_kv[:, :, 1]).reshape(B, L, MEM_HEADS * MEM_HEAD_DIM)
    return o @ wo


def moe(h, router_w, router_b, w_gu, b_gu, w_dn, b_dn):
    B, L, D = h.shape
    x = h.reshape(-1, D)
    T = x.shape[0]
    logits = (x @ router_w + router_b).astype(jnp.float32)
    top_v, top_e = lax.top_k(logits, TOP_K)
    gate = jax.nn.softmax(top_v, axis=-1).astype(x.dtype)
    N = T * TOP_K
    e_flat = top_e.reshape(-1)
    w_flat = gate.reshape(-1)
    tok_flat = jnp.arange(N, dtype=jnp.int32) // TOP_K
    order = jnp.argsort(e_flat)
    e_sorted = e_flat[order]
    counts = jnp.bincount(e_flat, length=N_EXPERTS).astype(jnp.int32)
    padded = (counts + MOE_BLOCK - 1) // MOE_BLOCK * MOE_BLOCK
    p_end = jnp.cumsum(padded)
    p_start = p_end - padded
    r_start = jnp.cumsum(counts) - counts
    dest = p_start[e_sorted] + jnp.arange(N, dtype=jnp.int32) - r_start[e_sorted]
    n_blocks = -(-(N + N_EXPERTS * (MOE_BLOCK - 1)) // MOE_BLOCK)
    R = n_blocks * MOE_BLOCK
    row_tok = jnp.full((R,), T, jnp.int32).at[dest].set(tok_flat[order])
    row_w = jnp.zeros((R,), x.dtype).at[dest].set(w_flat[order])
    blk_e = jnp.clip(jnp.searchsorted(p_end, jnp.arange(n_blocks, dtype=jnp.int32) * MOE_BLOCK, side='right'), 0, N_EXPERTS - 1)
    x_rows = jnp.concatenate([x, jnp.zeros((1, D), x.dtype)], axis=0)[row_tok].reshape(n_blocks, MOE_BLOCK, D)

    def expert_block(args):
        xb, e = args
        gu = xb @ w_gu[e] + b_gu[e]
        g = jnp.minimum(gu[:, :D_FF], SWIGLU_LIMIT)
        u = jnp.clip(gu[:, D_FF:], -SWIGLU_LIMIT, SWIGLU_LIMIT)
        a = g * jax.nn.sigmoid(SWIGLU_ALPHA * g) * (u + 1.0)
        return a @ w_dn[e] + b_dn[e]

    y_rows = lax.map(expert_block, (x_rows, blk_e)).reshape(R, D)
    y = jnp.zeros((T + 1, D), x.dtype).at[row_tok].add(y_rows * row_w[:, None])[:T]
    return y.reshape(B, L, D)


def finish_layer(h, pool_o, nsa_o, mem_kv, w_out, ln1_g, ln1_b, mem_wq, mem_wo, ln2_g, ln2_b,
                 router_w, router_b, w_gu, b_gu, w_dn, b_dn, ln3_g, ln3_b):
    mix = jnp.concatenate([pool_o, nsa_o], axis=-1) @ w_out
    h = layer_norm(DN_ALPHA * h + mix, ln1_g, ln1_b)
    h = layer_norm(DN_ALPHA * h + mem_cross_attn(h, mem_kv, mem_wq, mem_wo), ln2_g, ln2_b)
    h = layer_norm(DN_ALPHA * h + moe(h, router_w, router_b, w_gu, b_gu, w_dn, b_dn), ln3_g, ln3_b)
    return h


def gather_pages(pool, page_table):
    g = pool[page_table]
    return g.reshape((g.shape[0], g.shape[1] * g.shape[2]) + g.shape[3:])


def setup_inputs(seed: int = 0) -> dict:
    key = jax.random.key(seed)
    ks = jax.random.split(key, 40)
    f32 = jnp.float32

    def nrm(k, shape, scale=1.0):
        return jax.random.normal(k, shape, f32) * scale

    n_pages = PAST_LEN // PAGE_SIZE
    n_used = DEC_BATCH * n_pages
    n_phys = n_used + max(1, n_used // 4)
    win_rows = min(WINDOW, PAST_LEN)
    kvh = (2, NSA_KV_HEADS, NSA_HEAD_DIM)
    page_table = jax.random.permutation(ks[0], n_phys)[:n_used].reshape(DEC_BATCH, n_pages).astype(jnp.int32)
    return {
        'x_prompt': nrm(ks[1], (BATCH, SEQ, D_MODEL)),
        'x_sample': nrm(ks[2], (DEC_BATCH, DEC_SEQ, D_MODEL)),
        'cache_cmp_kv': nrm(ks[3], (DEPTH, n_phys, PAGE_SIZE) + kvh),
        'cache_slc_kv': nrm(ks[4], (DEPTH, n_phys, PAGE_SIZE) + kvh),
        'state_win_kv': nrm(ks[5], (DEPTH, DEC_BATCH, win_rows) + kvh),
        'state_pool': nrm(ks[6], (DEPTH, DEC_BATCH, POOL_STATE, POOL_WIDTH)),
        'cache_mem_kv': nrm(ks[7], (DEPTH, DEC_BATCH, MEM_LEN, 2, MEM_HEADS, MEM_HEAD_DIM)),
        'page_table': page_table,
        'mem_prompt': nrm(ks[8], (BATCH, MEM_LEN, D_MODEL)),
        'w_in': nrm(ks[9], (DEPTH, D_MODEL, IN_WIDTH), D_MODEL ** -0.5),
        'pool_w': nrm(ks[10], (DEPTH, N_POOL_GROUPS, POOL_GROUP, POOL_GROUP), POOL_GROUP ** -0.5),
        'pool_scale': 1.0 + nrm(ks[11], (DEPTH, POOL_WIDTH), 0.1),
        'cmp_pe': nrm(ks[12], (DEPTH, 2, CMP_LEN, NSA_HEAD_DIM), 0.5),
        'cmp_w1': nrm(ks[13], (DEPTH, 2, CMP_LEN, NSA_HEAD_DIM, CMP_HIDDEN), (CMP_LEN * NSA_HEAD_DIM) ** -0.5),
        'cmp_b1': nrm(ks[14], (DEPTH, 2, CMP_HIDDEN), 0.02),
        'cmp_w2': nrm(ks[15], (DEPTH, 2, CMP_HIDDEN, NSA_HEAD_DIM), 2.0 * CMP_HIDDEN ** -0.5),
        'cmp_b2': nrm(ks[16], (DEPTH, 2, NSA_HEAD_DIM), 0.02),
        'w_out': nrm(ks[17], (DEPTH, D_MIX, D_MODEL), DN_BETA * D_MIX ** -0.5),
        'ln1_g': 1.0 + nrm(ks[18], (DEPTH, D_MODEL), 0.05),
        'ln1_b': nrm(ks[19], (DEPTH, D_MODEL), 0.02),
        'mem_wq': nrm(ks[20], (DEPTH, D_MODEL, MEM_HEADS * MEM_HEAD_DIM), D_MODEL ** -0.5),
        'mem_wkv': nrm(ks[21], (DEPTH, D_MODEL, 2 * MEM_HEADS * MEM_HEAD_DIM), D_MODEL ** -0.5),
        'mem_wo': nrm(ks[22], (DEPTH, MEM_HEADS * MEM_HEAD_DIM, D_MODEL), DN_BETA * (MEM_HEADS * MEM_HEAD_DIM) ** -0.5),
        'ln2_g': 1.0 + nrm(ks[23], (DEPTH, D_MODEL), 0.05),
        'ln2_b': nrm(ks[24], (DEPTH, D_MODEL), 0.02),
        'router_w': nrm(ks[25], (DEPTH, D_MODEL, N_EXPERTS), D_MODEL ** -0.5),
        'router_b': nrm(ks[26], (DEPTH, N_EXPERTS), 0.01),
        'exp_w_gu': nrm(ks[27], (DEPTH, N_EXPERTS, D_MODEL, 2 * D_FF), D_MODEL ** -0.5),
        'exp_b_gu': nrm(ks[28], (DEPTH, N_EXPERTS, 2 * D_FF), 0.02),
        'exp_w_dn': nrm(ks[29], (DEPTH, N_EXPERTS, D_FF, D_MODEL), DN_BETA * D_FF ** -0.5),
        'exp_b_dn': nrm(ks[30], (DEPTH, N_EXPERTS, D_MODEL), 0.02),
        'ln3_g': 1.0 + nrm(ks[31], (DEPTH, D_MODEL), 0.05),
        'ln3_b': nrm(ks[32], (DEPTH, D_MODEL), 0.02),
    }


def reference(x_prompt, x_sample, cache_cmp_kv, cache_slc_kv, state_win_kv, state_pool, cache_mem_kv, page_table,
              mem_prompt, w_in, pool_w, pool_scale, cmp_pe, cmp_w1, cmp_b1, cmp_w2, cmp_b2, w_out, ln1_g, ln1_b,
              mem_wq, mem_wkv, mem_wo, ln2_g, ln2_b, router_w, router_b, exp_w_gu, exp_b_gu, exp_w_dn, exp_b_dn,
              ln3_g, ln3_b):
    slopes = alibi_slopes()
    hp = x_prompt
    hs = x_sample
    Bp, S = hp.shape[:2]
    Ls = hs.shape[1]
    cmp_p, slc_p, win_p, pool_p_st, mem_p = [], [], [], [], []
    cmp_s, slc_s, win_s, pool_s_st = [], [], [], []
    for l in range(DEPTH):
        cmp_w = (cmp_pe[l], cmp_w1[l], cmp_b1[l], cmp_w2[l], cmp_b2[l])
        tail = (w_out[l], ln1_g[l], ln1_b[l], mem_wq[l], mem_wo[l], ln2_g[l], ln2_b[l], router_w[l], router_b[l],
                exp_w_gu[l], exp_b_gu[l], exp_w_dn[l], exp_b_dn[l], ln3_g[l], ln3_b[l])
        up, qp, kvc_p, kvs_p, kvw_p, gp = split_mixer_inputs(hp @ w_in[l])
        pool_o = pool_mix(jnp.pad(up, ((0, 0), (POOL_STATE, 0), (0, 0))), jnp.arange(S, dtype=jnp.int32),
                          pool_w[l], pool_scale[l])
        o_c, o_s, o_w = nsa_prompt(qp, compress_kv(kvc_p, *cmp_w), sel_blocks(kvs_p), kvw_p, slopes)
        mem_kv_p = (mem_prompt @ mem_wkv[l]).reshape(Bp, MEM_LEN, 2, MEM_HEADS, MEM_HEAD_DIM)
        hp = finish_layer(hp, pool_o, combine_nsa(o_c, o_s, o_w, gp), mem_kv_p, *tail)
        cmp_p.append(kvc_p)
        slc_p.append(kvs_p)
        win_p.append(kvw_p[:, -min(WINDOW, S):])
        pool_p_st.append(up[:, -POOL_STATE:])
        mem_p.append(mem_kv_p)
        us, qs, kvc_s, kvs_s, kvw_s, gs = split_mixer_inputs(hs @ w_in[l])
        pos_s = PAST_LEN + jnp.arange(Ls, dtype=jnp.int32)
        pool_ext = jnp.concatenate([state_pool[l], us], axis=1)
        pool_o_s = pool_mix(pool_ext, pos_s, pool_w[l], pool_scale[l])
        full_c = jnp.concatenate([gather_pages(cache_cmp_kv[l], page_table), kvc_s], axis=1)
        full_s = jnp.concatenate([gather_pages(cache_slc_kv[l], page_table), kvs_s], axis=1)
        kv_cc, c_end = compress_kv(full_c, *cmp_w)
        w_rows = state_win_kv.shape[2]
        w_buf = jnp.concatenate([state_win_kv[l], kvw_s], axis=1)
        w_pos = PAST_LEN - w_rows + jnp.arange(w_rows + Ls, dtype=jnp.int32)
        so_c, so_s, so_w = nsa_branches(qs, pos_s, kv_cc, c_end, sel_blocks(full_s), w_buf, w_pos, slopes)
        hs = finish_layer(hs, pool_o_s, combine_nsa(so_c, so_s, so_w, gs), cache_mem_kv[l], *tail)
        cmp_s.append(kvc_s)
        slc_s.append(kvs_s)
        win_s.append(w_buf[:, -w_rows:])
        pool_s_st.append(pool_ext[:, -POOL_STATE:])
    new_cmp_p = jnp.stack(cmp_p)
    new_slc_p = jnp.stack(slc_p)
    new_win_p = jnp.stack(win_p)
    new_pool_p = jnp.stack(pool_p_st)
    new_mem_p = jnp.stack(mem_p)
    new_cmp_s = jnp.stack(cmp_s)
    new_slc_s = jnp.stack(slc_s)
    new_win_s = jnp.stack(win_s)
    new_pool_s = jnp.stack(pool_s_st)
    return (hp, hs, new_cmp_p, new_slc_p, new_win_p, new_pool_p, new_mem_p, new_cmp_s, new_slc_s, new_win_s, new_pool_s)
```

```python
import functools

import jax
import jax.numpy as jnp
from jax import lax
from jax.experimental import pallas as pl
from jax.experimental.pallas import tpu as pltpu

F32 = jnp.float32
BF16 = jnp.bfloat16
I32 = jnp.int32

D_MODEL = 1024
POOL_WIDTH = 512
POOL_WINDOWS = (2, 4, 8, 16)
POOL_GROUP = 128
POOL_STATE = 15
NSA_WIDTH = 512
DH = 64
NSA_HEADS = 8
KVH = 2
HPG = 4
CMP_LEN = 32
CMP_STRIDE = 16
CMP_HIDDEN = 256
SEL_LEN = 64
SEL_TOP = 16
WINDOW = 512
Q_BLOCK = 128
KV_WIDTH = 256
IN_WIDTH = 1816
ATTN_SCALE = DH ** -0.5
FORCED_SCORE = 1e4
NEG = -1e30
MEM_LEN = 256
MEM_HEADS = 4
MEM_HEAD_DIM = 256
N_EXPERTS = 32
TOP_K = 4
D_FF = 1024
SWIGLU_LIMIT = 7.0
SWIGLU_ALPHA = 1.702
DN_ALPHA = 2.0 ** 0.25
LN_EPS = 1e-5
PAST_LEN = 2048
PAGE_SIZE = 128

LANES = 128
SEL_PAD = 128
KEY_TILE = 512
ROW_TILE = 512
MOE_ROWS = 512
VMEM_LIMIT = 56 * 1024 * 1024

HIGHEST = lax.Precision.HIGHEST


def _dot(a, b):
    return jnp.dot(a, b, preferred_element_type=F32)


def _dot_nt(a, b, precision=None):
    return lax.dot_general(a, b, (((1,), (1,)), ((), ())), preferred_element_type=F32,
                           precision=precision)


def _layer_norm(x, g, b):
    mu = jnp.mean(x, axis=-1, keepdims=True)
    xc = x - mu
    var = jnp.mean(xc * xc, axis=-1, keepdims=True)
    return xc * lax.rsqrt(var + LN_EPS) * g + b


def _params(sem, vmem=VMEM_LIMIT):
    return pltpu.CompilerParams(dimension_semantics=sem, vmem_limit_bytes=vmem)


def _split_store(u, up_ref, q_ref, kvc_ref, kvs_ref, kvw_ref, gate_ref):
    o1 = POOL_WIDTH
    o2 = o1 + NSA_WIDTH
    o3 = o2 + KV_WIDTH
    o4 = o3 + KV_WIDTH
    o5 = o4 + KV_WIDTH
    up_ref[...] = u[:, :o1]
    q_ref[...] = u[:, o1:o2]
    kvc_ref[...] = u[:, o2:o3]
    kvs_ref[...] = u[:, o3:o4]
    kvw_ref[...] = u[:, o4:o5]
    gate_ref[...] = 1.0 / (1.0 + jnp.exp(-u[:, o5:]))


def _inproj_prompt_body(x_ref, w_ref, pw_ref, ps_ref,
                        up_ref, q_ref, kvc_ref, kvs_ref, kvw_ref, gate_ref, pool_ref,
                        ext_ref, *, tm, tiles_per_seq):
    halo = POOL_STATE + 1
    t_in_seq = pl.program_id(0) % tiles_per_seq
    u = _dot(x_ref[...].astype(BF16), w_ref[...])
    _split_store(u, up_ref, q_ref, kvc_ref, kvs_ref, kvw_ref, gate_ref)

    @pl.when(t_in_seq == 0)
    def _():
        ext_ref[0:halo, :] = jnp.zeros((halo, POOL_WIDTH), F32)

    ext_ref[halo:halo + tm, :] = u[:, :POOL_WIDTH]
    pos = t_in_seq * tm + lax.broadcasted_iota(I32, (tm, 1), 0)
    for gi, w in enumerate(POOL_WINDOWS):
        cols = slice(gi * POOL_GROUP, (gi + 1) * POOL_GROUP)
        acc = ext_ref[halo:halo + tm, cols]
        for k in range(1, w):
            acc = acc + ext_ref[halo - k:halo - k + tm, cols]
        cnt = jnp.minimum(pos + 1, w).astype(F32)
        d = acc / cnt - ext_ref[halo:halo + tm, cols]
        o = _dot(d.astype(BF16), pw_ref[gi])
        pool_ref[:, cols] = o * ps_ref[:, cols]
    ext_ref[0:halo, :] = ext_ref[tm:tm + halo, :]


def _inproj_prompt(x2d, w_in_bf, pool_w_bf, pool_scale, seq_len):
    T = x2d.shape[0]
    tm = ROW_TILE
    outs = [POOL_WIDTH, NSA_WIDTH, KV_WIDTH, KV_WIDTH, KV_WIDTH, 3 * NSA_HEADS, POOL_WIDTH]
    row = lambda n: pl.BlockSpec((tm, n), lambda i: (i, 0))
    full = lambda a: pl.BlockSpec(a.shape, lambda i: (0,) * a.ndim)
    return pl.pallas_call(
        functools.partial(_inproj_prompt_body, tm=tm, tiles_per_seq=seq_len // tm),
        grid=(T // tm,),
        in_specs=[row(D_MODEL), full(w_in_bf), full(pool_w_bf), full(pool_scale)],
        out_specs=[row(n) for n in outs],
        out_shape=[jax.ShapeDtypeStruct((T, n), F32) for n in outs],
        scratch_shapes=[pltpu.VMEM((tm + POOL_STATE + 1, POOL_WIDTH), F32)],
        compiler_params=_params(("arbitrary",)),
        name="inproj_prompt",
    )(x2d, w_in_bf, pool_w_bf, pool_scale)


def _inproj_sample_body(x_ref, w_ref, pw_ref, ps_ref, st_ref,
                        up_ref, q_ref, kvc_ref, kvs_ref, kvw_ref, gate_ref, pool_ref,
                        ext_ref, *, nb, ls, pos0):
    halo = POOL_STATE + 1
    tm = nb * ls
    u = _dot(x_ref[...].astype(BF16), w_ref[...])
    _split_store(u, up_ref, q_ref, kvc_ref, kvs_ref, kvw_ref, gate_ref)
    ext_ref[:, 0:halo, :] = st_ref[...]
    ext_ref[:, halo:halo + ls, :] = u[:, :POOL_WIDTH].reshape(nb, ls, POOL_WIDTH)
    pos = pos0 + lax.broadcasted_iota(I32, (1, ls, 1), 1)
    for gi, w in enumerate(POOL_WINDOWS):
        cols = slice(gi * POOL_GROUP, (gi + 1) * POOL_GROUP)
        acc = ext_ref[:, halo:halo + ls, cols]
        for k in range(1, w):
            acc = acc + ext_ref[:, halo - k:halo - k + ls, cols]
        cnt = jnp.minimum(pos + 1, w).astype(F32)
        d = acc / cnt - ext_ref[:, halo:halo + ls, cols]
        o = _dot(d.reshape(tm, POOL_GROUP).astype(BF16), pw_ref[gi])
        pool_ref[:, cols] = o * ps_ref[:, cols]


def _inproj_sample(x2d, w_in_bf, pool_w_bf, pool_scale, state_pad, ls, pos0):
    T = x2d.shape[0]
    nb = ROW_TILE // ls
    tm = nb * ls
    outs = [POOL_WIDTH, NSA_WIDTH, KV_WIDTH, KV_WIDTH, KV_WIDTH, 3 * NSA_HEADS, POOL_WIDTH]
    row = lambda n: pl.BlockSpec((tm, n), lambda i: (i, 0))
    full = lambda a: pl.BlockSpec(a.shape, lambda i: (0,) * a.ndim)
    return pl.pallas_call(
        functools.partial(_inproj_sample_body, nb=nb, ls=ls, pos0=pos0),
        grid=(T // tm,),
        in_specs=[row(D_MODEL), full(w_in_bf), full(pool_w_bf), full(pool_scale),
                  pl.BlockSpec((nb, POOL_STATE + 1, POOL_WIDTH), lambda i: (i, 0, 0))],
        out_specs=[row(n) for n in outs],
        out_shape=[jax.ShapeDtypeStruct((T, n), F32) for n in outs],
        scratch_shapes=[pltpu.VMEM((nb, POOL_STATE + 1 + ls, POOL_WIDTH), F32)],
        compiler_params=_params(("arbitrary",)),
        name="inproj_sample",
    )(x2d, w_in_bf, pool_w_bf, pool_scale, state_pad)


def _matmul_body(x_ref, w_ref, o_ref):
    o_ref[...] = _dot(x_ref[...].astype(BF16), w_ref[...].astype(BF16))


def _matmul(x, w, tn=512):
    M, K = x.shape
    N = w.shape[1]
    return pl.pallas_call(
        _matmul_body,
        grid=(N // tn,),
        in_specs=[pl.BlockSpec((M, K), lambda j: (0, 0)), pl.BlockSpec((K, tn), lambda j: (0, j))],
        out_specs=pl.BlockSpec((M, tn), lambda j: (0, j)),
        out_shape=jax.ShapeDtypeStruct((M, N), F32),
        compiler_params=_params(("arbitrary",)),
        name="mem_kv_proj",
    )(x, w)


def _gelu_tanh(x):
    c = 0.7978845608028654
    return 0.5 * x * (1.0 + jnp.tanh(c * (x + 0.044715 * (x * x * x))))


def _compress(kv_refs, n_chunks, wa_ref, wb_ref, pe_ref, b1_ref, w2_ref, b2_ref):
    outs = []
    for c in range(2):
        acc_a = jnp.zeros((n_chunks, 2 * CMP_HIDDEN), F32)
        acc_b = jnp.zeros((n_chunks, 2 * CMP_HIDDEN), F32)
        for l in range(CMP_STRIDE):
            x = kv_refs[c][pl.ds(l, n_chunks, stride=CMP_STRIDE), :]
            xa = (x + pe_ref[c, l:l + 1, :]).astype(BF16)
            xb = (x + pe_ref[c, CMP_STRIDE + l:CMP_STRIDE + l + 1, :]).astype(BF16)
            acc_a = acc_a + _dot(xa, wa_ref[c, l])
            acc_b = acc_b + _dot(xb, wb_ref[c, l])
        hid = acc_a + pltpu.roll(acc_b, n_chunks - 1, 0) + b1_ref[c]
        outs.append(_dot(_gelu_tanh(hid).astype(BF16), w2_ref[c]) + b2_ref[c])
    return jnp.concatenate(outs, axis=1)


def _compress_prompt_body(kv_ref, wa_ref, wb_ref, pe_ref, b1_ref, w2_ref, b2_ref, o_ref, k_ref, v_ref,
                          *, n_chunks):
    k_ref[...] = kv_ref[0, :, :LANES]
    v_ref[...] = kv_ref[0, :, LANES:]
    o_ref[0] = _compress((k_ref, v_ref), n_chunks, wa_ref, wb_ref, pe_ref, b1_ref, w2_ref, b2_ref)


def _compress_prompt(kvc, cw):
    B, S, _ = kvc.shape
    n_chunks = S // CMP_STRIDE
    full = lambda a: pl.BlockSpec(a.shape, lambda b: (0,) * a.ndim)
    return pl.pallas_call(
        functools.partial(_compress_prompt_body, n_chunks=n_chunks),
        grid=(B,),
        in_specs=[pl.BlockSpec((1, S, KV_WIDTH), lambda b: (b, 0, 0))] + [full(a) for a in cw],
        out_specs=pl.BlockSpec((1, n_chunks, KV_WIDTH), lambda b: (b, 0, 0)),
        out_shape=jax.ShapeDtypeStruct((B, n_chunks, KV_WIDTH), F32),
        scratch_shapes=[pltpu.VMEM((S, LANES), F32)] * 2,
        compiler_params=_params(("arbitrary",)),
        name="compress_prompt",
    )(kvc, *cw)


def _compress_weights(cmp_pe, cmp_w1, cmp_b1, cmp_w2, cmp_b2):
    z1 = jnp.zeros((2, CMP_LEN, DH, CMP_HIDDEN), F32)
    w1 = jnp.concatenate([jnp.concatenate([cmp_w1, z1], axis=-1),
                          jnp.concatenate([z1, cmp_w1], axis=-1)], axis=2)
    wa = w1[:, :CMP_STRIDE].astype(BF16)
    wb = w1[:, CMP_STRIDE:].astype(BF16)
    pe = jnp.concatenate([cmp_pe, cmp_pe], axis=-1)
    b1 = jnp.concatenate([cmp_b1, cmp_b1], axis=-1)[:, None, :]
    z2 = jnp.zeros((2, CMP_HIDDEN, DH), F32)
    w2 = jnp.concatenate([jnp.concatenate([cmp_w2, z2], axis=-1),
                          jnp.concatenate([z2, cmp_w2], axis=-1)], axis=1).astype(BF16)
    b2 = jnp.concatenate([cmp_b2, cmp_b2], axis=-1)[:, None, :]
    return wa, wb, pe, b1, w2, b2


def _dup_halves(x):
    lane = lax.broadcasted_iota(I32, x.shape, 1)
    r = pltpu.roll(x, DH, 1)
    lo = lane < DH
    return jnp.where(lo, x, r), jnp.where(lo, r, x)


def _softmax_rows(s):
    m = jnp.max(s, axis=-1, keepdims=True)
    e = jnp.exp(s - m)
    return e / jnp.sum(e, axis=-1, keepdims=True)


def _nsa_core(qs, gates, pos0, lq,
              kc_k, kc_v, n_cmp,
              kaug, vsel, n_key_tiles,
              kw_k, kw_v, w_pos,
              ovt_ref, eg_ref, vt_ref):
    q_pos = pos0 + lax.broadcasted_iota(I32, (lq, 1), 0)
    lane = lax.broadcasted_iota(I32, (lq, LANES), 1)
    lo_half = lane < DH
    slopes = [2.0 ** (-(h + 1)) for h in range(NSA_HEADS)]

    def head_q(h):
        slab = qs[:, (h // 2) * LANES:(h // 2 + 1) * LANES]
        keep = lo_half if h % 2 == 0 else jnp.logical_not(lo_half)
        return jnp.where(keep, slab, 0.0).astype(BF16)

    def assemble(heads):
        return jnp.concatenate([jnp.where(lo_half, heads[2 * k], heads[2 * k + 1])
                                for k in range(NSA_HEADS // 2)], axis=1)

    qh = [head_q(h) for h in range(NSA_HEADS)]

    c_end = lax.broadcasted_iota(I32, (1, n_cmp), 1) * CMP_STRIDE + (CMP_LEN - 1)
    m_c = c_end <= q_pos
    any_c = (q_pos >= CMP_LEN - 1).astype(F32)
    o_c = [None] * NSA_HEADS
    selbias = [None] * KVH
    blk = lax.broadcasted_iota(I32, (SEL_PAD, LANES), 0)
    for g in range(KVH):
        psum = jnp.zeros((lq, n_cmp), F32)
        for hl in range(HPG):
            h = g * HPG + hl
            s = jnp.where(m_c, _dot_nt(qh[h], kc_k[g][...]), NEG)
            p = _softmax_rows(s) * any_c
            psum = psum + p
            o_c[h] = _dot(p.astype(BF16), kc_v[g][...])
        if lq < LANES:
            psum = jnp.concatenate([psum, jnp.zeros((LANES - lq, n_cmp), F32)], axis=0)
        imp_t = _dot_nt(ovt_ref[...], psum, precision=HIGHEST)
        qp_t = pos0 + lax.broadcasted_iota(I32, (SEL_PAD, LANES), 1)
        cur = qp_t // SEL_LEN
        forced = (blk == 0) | (blk == cur) | (blk == cur - 1)
        valid = blk * SEL_LEN <= qp_t
        v = jnp.where(valid, jnp.where(forced, FORCED_SCORE, imp_t), -1.0)
        vt_ref[...] = v

        def rank_step(j, cnt):
            row = vt_ref[pl.ds(j, 1), :]
            ahead = (row > v) | ((row == v) & (j < blk))
            return cnt + ahead.astype(F32)

        cnt = lax.fori_loop(0, SEL_PAD, rank_step, jnp.zeros((SEL_PAD, LANES), F32))
        sel_t = (cnt < SEL_TOP) & valid
        bias = jnp.where(sel_t, 0.0, NEG).T
        selbias[g] = bias[:lq].astype(BF16)

    o_s = [None] * NSA_HEADS
    for h in range(NSA_HEADS):
        g = h // HPG
        qa = jnp.concatenate([qh[h], selbias[g]], axis=1)
        slope = slopes[h]

        def key_tile(t, carry, g=g, qa=qa, slope=slope):
            m, l, acc = carry
            r0 = pl.multiple_of(t * KEY_TILE, KEY_TILE)
            s = _dot_nt(qa, kaug[g][pl.ds(r0, KEY_TILE), :])
            k_pos = r0 + lax.broadcasted_iota(I32, (1, KEY_TILE), 1)
            d = q_pos - k_pos
            s = jnp.where(d >= 0, s - slope * d.astype(F32), NEG)
            m_new = jnp.maximum(m, jnp.max(s, axis=-1, keepdims=True))
            a = jnp.exp(m - m_new)
            e = jnp.exp(s - m_new)
            l = a * l + jnp.sum(e, axis=-1, keepdims=True)
            acc = a * acc + _dot(e.astype(BF16), vsel[g][pl.ds(r0, KEY_TILE), :])
            return m_new, l, acc

        init = (jnp.full((lq, 1), NEG, F32), jnp.zeros((lq, 1), F32), jnp.zeros((lq, LANES), F32))
        m, l, acc = lax.fori_loop(0, n_key_tiles, key_tile, init)
        o_s[h] = acc / l

    d_w = q_pos - w_pos
    m_w = (d_w >= 0) & (d_w < WINDOW) & (w_pos >= 0)
    d_wf = d_w.astype(F32)
    o_w = [None] * NSA_HEADS
    for h in range(NSA_HEADS):
        g = h // HPG
        s = jnp.where(m_w, _dot_nt(qh[h], kw_k[g]) - slopes[h] * d_wf, NEG)
        o_w[h] = _dot(_softmax_rows(s).astype(BF16), kw_v[g])

    out = (_dot(gates, eg_ref[0]) * assemble(o_c)
           + _dot(gates, eg_ref[1]) * assemble(o_s)
           + _dot(gates, eg_ref[2]) * assemble(o_w))
    return out


def _gate_expand():
    r = jnp.arange(3 * NSA_HEADS)
    c = jnp.arange(NSA_WIDTH)
    return jnp.stack([(r[:, None] == 3 * (c[None, :] // DH) + br).astype(F32) for br in range(3)])


def _overlap_t(n_cmp):
    n = jnp.arange(n_cmp)
    s = jnp.arange(SEL_PAD)
    c_first = n * CMP_STRIDE
    c_end = c_first + CMP_LEN - 1
    b_first = s * SEL_LEN
    return ((c_first[None, :] < b_first[:, None] + SEL_LEN) & (c_end[None, :] >= b_first[:, None])).astype(F32)


def _store_keys(kv_f32, row0, kaug, vsel):
    n = kv_f32.shape[0]
    k0, k1 = _dup_halves(kv_f32[:, :LANES])
    v0, v1 = _dup_halves(kv_f32[:, LANES:])
    pos = row0 + lax.broadcasted_iota(I32, (n, SEL_PAD), 0)
    onehot = (lax.broadcasted_iota(I32, (n, SEL_PAD), 1) == pos // SEL_LEN).astype(BF16)
    rows = pl.ds(row0, n)
    kaug[0][rows, :] = jnp.concatenate([k0.astype(BF16), onehot], axis=1)
    kaug[1][rows, :] = jnp.concatenate([k1.astype(BF16), onehot], axis=1)
    vsel[0][rows, :] = v0.astype(BF16)
    vsel[1][rows, :] = v1.astype(BF16)


def _nsa_prompt_body(q_ref, gate_ref, kc_ref, kvs_ref, w0_ref, w1_ref, w2_ref, w3_ref, w4_ref,
                     ovt_ref, eg_ref, o_ref,
                     kck0, kck1, kcv0, kcv1, ka0, ka1, vs0, vs1, vt_ref, *, seq_len):
    j = pl.program_id(1)
    kc_k, kc_v, kaug, vsel = (kck0, kck1), (kcv0, kcv1), (ka0, ka1), (vs0, vs1)
    n_cmp = kc_ref.shape[1]
    chunk = KEY_TILE

    @pl.when(j == 0)
    def _():
        ck0, ck1 = _dup_halves(kc_ref[0, :, :LANES])
        cv0, cv1 = _dup_halves(kc_ref[0, :, LANES:])
        kck0[...] = ck0.astype(BF16)
        kck1[...] = ck1.astype(BF16)
        kcv0[...] = cv0.astype(BF16)
        kcv1[...] = cv1.astype(BF16)

        def build(i, _):
            r0 = pl.multiple_of(i * chunk, chunk)
            _store_keys(kvs_ref[0, pl.ds(r0, chunk), :], r0, kaug, vsel)
            return 0

        lax.fori_loop(0, seq_len // chunk, build, 0)

    st = j * Q_BLOCK
    band = jnp.concatenate([w0_ref[0], w1_ref[0], w2_ref[0], w3_ref[0], w4_ref[0]], axis=0)
    n_win = band.shape[0]
    kw_k = tuple(a.astype(BF16) for a in _dup_halves(band[:, :LANES]))
    kw_v = tuple(a.astype(BF16) for a in _dup_halves(band[:, LANES:]))
    w_pos = st - WINDOW + lax.broadcasted_iota(I32, (1, n_win), 1)
    n_tiles = (st + Q_BLOCK + KEY_TILE - 1) // KEY_TILE
    o_ref[...] = _nsa_core(q_ref[...] * ATTN_SCALE, gate_ref[...], st, Q_BLOCK,
                           kc_k, kc_v, n_cmp, kaug, vsel, n_tiles, kw_k, kw_v, w_pos,
                           ovt_ref, eg_ref, vt_ref)


def _nsa_prompt(q, gates, kc, kvs, kvw):
    B, S, _ = kvs.shape
    nqb = S // Q_BLOCK
    n_cmp = kc.shape[1]
    ovt = _overlap_t(n_cmp)
    eg = _gate_expand()
    n_band = WINDOW // Q_BLOCK + 1

    def band_spec(i):
        return pl.BlockSpec((1, Q_BLOCK, KV_WIDTH),
                            lambda b, j, i=i: (b, jnp.maximum(j - (n_band - 1) + i, 0), 0))

    full = lambda a: pl.BlockSpec(a.shape, lambda b, j: (0,) * a.ndim)
    return pl.pallas_call(
        functools.partial(_nsa_prompt_body, seq_len=S),
        grid=(B, nqb),
        in_specs=[pl.BlockSpec((Q_BLOCK, NSA_WIDTH), lambda b, j: (b * nqb + j, 0)),
                  pl.BlockSpec((Q_BLOCK, 3 * NSA_HEADS), lambda b, j: (b * nqb + j, 0)),
                  pl.BlockSpec((1, n_cmp, KV_WIDTH), lambda b, j: (b, 0, 0)),
                  pl.BlockSpec((1, S, KV_WIDTH), lambda b, j: (b, 0, 0))]
                 + [band_spec(i) for i in range(n_band)] + [full(ovt), full(eg)],
        out_specs=pl.BlockSpec((Q_BLOCK, NSA_WIDTH), lambda b, j: (b * nqb + j, 0)),
        out_shape=jax.ShapeDtypeStruct((B * S, NSA_WIDTH), F32),
        scratch_shapes=[pltpu.VMEM((n_cmp, LANES), BF16)] * 4
                       + [pltpu.VMEM((S, 2 * LANES), BF16)] * 2
                       + [pltpu.VMEM((S, LANES), BF16)] * 2
                       + [pltpu.VMEM((SEL_PAD, LANES), F32)],
        compiler_params=_params(("arbitrary", "arbitrary")),
        name="nsa_prompt",
    )(q, gates, kc, kvs, *([kvw] * n_band), ovt, eg)


def _nsa_sample_body(pt_ref, q_ref, gate_ref, kvs_new_ref, win_ref, kvw_new_ref, *rest,
                     n_pages, ls, past_len):
    cmp_pages = rest[:n_pages]
    slc_pages = rest[n_pages:2 * n_pages]
    (wa_ref, wb_ref, pe_ref, b1_ref, w2_ref, b2_ref, ovt_ref, eg_ref, o_ref,
     full_k, full_v, kck0, kck1, kcv0, kcv1, ka0, ka1, vs0, vs1, vt_ref) = rest[2 * n_pages:]
    del pt_ref
    kc_k, kc_v, kaug, vsel = (kck0, kck1), (kcv0, kcv1), (ka0, ka1), (vs0, vs1)
    n_cmp = past_len // CMP_STRIDE
    n_keys = kaug[0].shape[0]
    tail = 16

    @pl.when(pl.program_id(0) == 0)
    def _():
        for g in range(KVH):
            kaug[g][...] = jnp.zeros(kaug[g].shape, BF16)
            vsel[g][...] = jnp.zeros(vsel[g].shape, BF16)

    for p in range(n_pages):
        full_k[p * PAGE_SIZE:(p + 1) * PAGE_SIZE, :] = cmp_pages[p][0, :, :LANES]
        full_v[p * PAGE_SIZE:(p + 1) * PAGE_SIZE, :] = cmp_pages[p][0, :, LANES:]
        _store_keys(slc_pages[p][0], p * PAGE_SIZE, kaug, vsel)
    new_rows = jnp.concatenate([kvs_new_ref[0], jnp.zeros((tail - ls, KV_WIDTH), F32)], axis=0)
    _store_keys(new_rows, past_len, kaug, vsel)

    kc = _compress((full_k, full_v), n_cmp, wa_ref, wb_ref, pe_ref, b1_ref, w2_ref, b2_ref)
    ck0, ck1 = _dup_halves(kc[:, :LANES])
    cv0, cv1 = _dup_halves(kc[:, LANES:])
    kck0[...] = ck0.astype(BF16)
    kck1[...] = ck1.astype(BF16)
    kcv0[...] = cv0.astype(BF16)
    kcv1[...] = cv1.astype(BF16)

    w_rows = win_ref.shape[1]
    n_win = w_rows + LANES
    band = jnp.concatenate([win_ref[0], kvw_new_ref[0], jnp.zeros((LANES - ls, KV_WIDTH), F32)], axis=0)
    kw_k = tuple(a.astype(BF16) for a in _dup_halves(band[:, :LANES]))
    kw_v = tuple(a.astype(BF16) for a in _dup_halves(band[:, LANES:]))
    w_pos = past_len - w_rows + lax.broadcasted_iota(I32, (1, n_win), 1)
    o_ref[0] = _nsa_core(q_ref[0] * ATTN_SCALE, gate_ref[0], past_len, ls,
                         kc_k, kc_v, n_cmp, kaug, vsel, n_keys // KEY_TILE, kw_k, kw_v, w_pos,
                         ovt_ref, eg_ref, vt_ref)


def _nsa_sample(q, gates, kvs_new, kvw_new, state_win, cache_cmp, cache_slc, page_table, cw, past_len):
    Bd, ls, _ = q.shape
    n_pages = page_table.shape[1]
    n_cmp = past_len // CMP_STRIDE
    n_keys = -(-(past_len + 16) // KEY_TILE) * KEY_TILE
    ovt = _overlap_t(n_cmp)
    eg = _gate_expand()
    w_rows = state_win.shape[1]

    per_b = lambda a: pl.BlockSpec((1,) + a.shape[1:], lambda b, pt: (b,) + (0,) * (a.ndim - 1))
    full = lambda a: pl.BlockSpec(a.shape, lambda b, pt: (0,) * a.ndim)
    page = lambda p: pl.BlockSpec((1, PAGE_SIZE, KV_WIDTH), lambda b, pt, p=p: (pt[b, p], 0, 0))
    grid_spec = pltpu.PrefetchScalarGridSpec(
        num_scalar_prefetch=1,
        grid=(Bd,),
        in_specs=[per_b(q), per_b(gates), per_b(kvs_new), per_b(state_win), per_b(kvw_new)]
                 + [page(p) for p in range(n_pages)] * 2
                 + [full(a) for a in cw] + [full(ovt), full(eg)],
        out_specs=pl.BlockSpec((1, ls, NSA_WIDTH), lambda b, pt: (b, 0, 0)),
        scratch_shapes=[pltpu.VMEM((past_len, LANES), F32)] * 2
                       + [pltpu.VMEM((n_cmp, LANES), BF16)] * 4
                       + [pltpu.VMEM((n_keys, 2 * LANES), BF16)] * 2
                       + [pltpu.VMEM((n_keys, LANES), BF16)] * 2
                       + [pltpu.VMEM((SEL_PAD, LANES), F32)],
    )
    return pl.pallas_call(
        functools.partial(_nsa_sample_body, n_pages=n_pages, ls=ls, past_len=past_len),
        grid_spec=grid_spec,
        out_shape=jax.ShapeDtypeStruct((Bd, ls, NSA_WIDTH), F32),
        compiler_params=_params(("arbitrary",)),
        name="nsa_sample",
    )(page_table, q, gates, kvs_new, state_win, kvw_new,
      *([cache_cmp] * n_pages), *([cache_slc] * n_pages), *cw, ovt, eg)


def _fin1_body(h_ref, pool_ref, nsa_ref, wo_ref, g_ref, b_ref, wq_ref, h1_ref, qm_ref):
    mix = (_dot(pool_ref[...].astype(BF16), wo_ref[0:POOL_WIDTH, :])
           + _dot(nsa_ref[...].astype(BF16), wo_ref[POOL_WIDTH:, :]))
    h1 = _layer_norm(DN_ALPHA * h_ref[...] + mix, g_ref[...], b_ref[...])
    h1_ref[...] = h1
    qm_ref[...] = _dot(h1.astype(BF16), wq_ref[...])


def _fin1(h, pool_o, nsa_o, w_out_bf, g, b, wq_bf):
    T = h.shape[0]
    tm = ROW_TILE
    row = lambda n: pl.BlockSpec((tm, n), lambda i: (i, 0))
    full = lambda a: pl.BlockSpec(a.shape, lambda i: (0,) * a.ndim)
    return pl.pallas_call(
        _fin1_body,
        grid=(T // tm,),
        in_specs=[row(D_MODEL), row(POOL_WIDTH), row(NSA_WIDTH), full(w_out_bf), full(g), full(b), full(wq_bf)],
        out_specs=[row(D_MODEL), row(D_MODEL)],
        out_shape=[jax.ShapeDtypeStruct((T, D_MODEL), F32)] * 2,
        compiler_params=_params(("arbitrary",)),
        name="out_proj_ln1",
    )(h, pool_o, nsa_o, w_out_bf, g, b, wq_bf)


def _memattn_body(q_ref, kv_ref, o_ref):
    width = MEM_HEADS * MEM_HEAD_DIM
    for h in range(MEM_HEADS):
        cols = slice(h * MEM_HEAD_DIM, (h + 1) * MEM_HEAD_DIM)
        qh = (q_ref[0, :, cols] * (MEM_HEAD_DIM ** -0.5)).astype(BF16)
        kh = kv_ref[0, :, cols].astype(BF16)
        vh = kv_ref[0, :, width + h * MEM_HEAD_DIM:width + (h + 1) * MEM_HEAD_DIM].astype(BF16)
        p = _softmax_rows(_dot_nt(qh, kh))
        o_ref[0, :, cols] = _dot(p.astype(BF16), vh)


def _memattn(qm, mem_kv, tq):
    nb, L, W = qm.shape
    return pl.pallas_call(
        _memattn_body,
        grid=(nb, L // tq),
        in_specs=[pl.BlockSpec((1, tq, W), lambda b, t: (b, t, 0)),
                  pl.BlockSpec((1, MEM_LEN, 2 * W), lambda b, t: (b, 0, 0))],
        out_specs=pl.BlockSpec((1, tq, W), lambda b, t: (b, t, 0)),
        out_shape=jax.ShapeDtypeStruct((nb, L, W), F32),
        compiler_params=_params(("arbitrary", "arbitrary")),
        name="mem_attn",
    )(qm, mem_kv)


def _fin2_body(h1_ref, o_ref, wo_ref, g_ref, b_ref, rw_ref, rb_ref, h2_ref, te_ref, tg_ref):
    tm = h1_ref.shape[0]
    a = _dot(o_ref[...].astype(BF16), wo_ref[...])
    h2 = _layer_norm(DN_ALPHA * h1_ref[...] + a, g_ref[...], b_ref[...])
    h2_ref[...] = h2
    logits = jnp.dot(h2, rw_ref[...], preferred_element_type=F32, precision=HIGHEST) + rb_ref[...]
    e_iota = lax.broadcasted_iota(I32, (tm, N_EXPERTS), 1).astype(F32)
    lane = lax.broadcasted_iota(I32, (tm, LANES), 1)
    te = jnp.zeros((tm, LANES), F32)
    tv = jnp.full((tm, LANES), NEG, F32)
    work = logits
    for k in range(TOP_K):
        m = jnp.max(work, axis=-1, keepdims=True)
        idx = jnp.min(jnp.where(work == m, e_iota, float(N_EXPERTS)), axis=-1, keepdims=True)
        te = jnp.where(lane == k, idx, te)
        tv = jnp.where(lane == k, m, tv)
        work = jnp.where(e_iota == idx, -jnp.inf, work)
    ex = jnp.exp(tv - jnp.max(tv, axis=-1, keepdims=True))
    te_ref[...] = te.astype(I32)
    tg_ref[...] = ex / jnp.sum(ex, axis=-1, keepdims=True)


def _fin2(h1, o, wo_bf, g, b, rw, rb):
    T = h1.shape[0]
    tm = ROW_TILE
    row = lambda n: pl.BlockSpec((tm, n), lambda i: (i, 0))
    full = lambda a: pl.BlockSpec(a.shape, lambda i: (0,) * a.ndim)
    return pl.pallas_call(
        _fin2_body,
        grid=(T // tm,),
        in_specs=[row(D_MODEL), row(D_MODEL), full(wo_bf), full(g), full(b), full(rw), full(rb)],
        out_specs=[row(D_MODEL), row(LANES), row(LANES)],
        out_shape=[jax.ShapeDtypeStruct((T, D_MODEL), F32), jax.ShapeDtypeStruct((T, LANES), I32),
                   jax.ShapeDtypeStruct((T, LANES), F32)],
        compiler_params=_params(("arbitrary",)),
        name="mem_out_ln2_router",
    )(h1, o, wo_bf, g, b, rw, rb)


def _moe_body(be_ref, nu_ref, x_ref, rw_ref, wgu_ref, bgu_ref, wdn_ref, bdn_ref, y_ref, wgu_bf, wdn_bf):
    i = pl.program_id(0)
    e = be_ref[i]
    prev = be_ref[jnp.maximum(i - 1, 0)]

    @pl.when((i == 0) | (e != prev))
    def _():
        wgu_bf[...] = wgu_ref[0].astype(BF16)
        wdn_bf[...] = wdn_ref[0].astype(BF16)

    @pl.when(i < nu_ref[0])
    def _():
        x = x_ref[...].astype(BF16)
        g = _dot(x, wgu_bf[:, :D_FF]) + bgu_ref[0, :, :D_FF]
        u = _dot(x, wgu_bf[:, D_FF:]) + bgu_ref[0, :, D_FF:]
        g = jnp.minimum(g, SWIGLU_LIMIT)
        u = jnp.clip(u, -SWIGLU_LIMIT, SWIGLU_LIMIT)
        a = g * (1.0 / (1.0 + jnp.exp(-SWIGLU_ALPHA * g))) * (u + 1.0)
        y = _dot(a.astype(BF16), wdn_bf[...]) + bdn_ref[0]
        y_ref[...] = y * rw_ref[...]

    @pl.when(i >= nu_ref[0])
    def _():
        y_ref[...] = jnp.zeros(y_ref.shape, F32)


def _moe_gmm(x_rows, row_w, blk_e, n_used, w_gu, b_gu, w_dn, b_dn):
    R = x_rows.shape[0]
    bk = MOE_ROWS
    grid_spec = pltpu.PrefetchScalarGridSpec(
        num_scalar_prefetch=2,
        grid=(R // bk,),
        in_specs=[pl.BlockSpec((bk, D_MODEL), lambda i, be, nu: (i, 0)),
                  pl.BlockSpec((bk, 1), lambda i, be, nu: (i, 0)),
                  pl.BlockSpec((1, D_MODEL, 2 * D_FF), lambda i, be, nu: (be[i], 0, 0)),
                  pl.BlockSpec((1, 1, 2 * D_FF), lambda i, be, nu: (be[i], 0, 0)),
                  pl.BlockSpec((1, D_FF, D_MODEL), lambda i, be, nu: (be[i], 0, 0)),
                  pl.BlockSpec((1, 1, D_MODEL), lambda i, be, nu: (be[i], 0, 0))],
        out_specs=pl.BlockSpec((bk, D_MODEL), lambda i, be, nu: (i, 0)),
        scratch_shapes=[pltpu.VMEM((D_MODEL, 2 * D_FF), BF16), pltpu.VMEM((D_FF, D_MODEL), BF16)],
    )
    return pl.pallas_call(
        _moe_body,
        grid_spec=grid_spec,
        out_shape=jax.ShapeDtypeStruct((R, D_MODEL), F32),
        compiler_params=_params(("arbitrary",)),
        name="moe_experts",
    )(blk_e, n_used, x_rows, row_w, w_gu, b_gu, w_dn, b_dn)


def _moe_routing(top_e, top_g, n_tok):
    bk = MOE_ROWS
    N = n_tok * TOP_K
    e_flat = top_e.reshape(-1)
    g_flat = top_g.reshape(-1)
    order = jnp.argsort(e_flat, stable=True).astype(I32)
    onehot = (e_flat[:, None] == jnp.arange(N_EXPERTS, dtype=I32)[None, :]).astype(I32)
    counts = onehot.sum(axis=0)
    padded = (counts + bk - 1) // bk * bk
    p_end = jnp.cumsum(padded)
    p_start = p_end - padded
    r_start = jnp.cumsum(counts) - counts
    n_blocks = -(-(N + N_EXPERTS * (bk - 1)) // bk)
    R = n_blocks * bk
    blk_e = jnp.clip(jnp.searchsorted(p_end, jnp.arange(n_blocks, dtype=I32) * bk, side='right'),
                     0, N_EXPERTS - 1).astype(I32)
    r = jnp.arange(R, dtype=I32)
    e_r = blk_e[r // bk]
    off = r - p_start[e_r]
    valid = off < counts[e_r]
    src = order[jnp.clip(r_start[e_r] + off, 0, N - 1)]
    row_tok = jnp.where(valid, src // TOP_K, n_tok)
    row_w = jnp.where(valid, g_flat[src], 0.0)
    rank = jnp.take_along_axis(jnp.cumsum(onehot, axis=0), e_flat[:, None], axis=1)[:, 0] - 1
    dest = (p_start[e_flat] + rank).reshape(n_tok, TOP_K)
    n_used = (p_end[-1] // bk).astype(I32).reshape(1)
    return row_tok, row_w, blk_e, n_used, dest


def _fin3_body(h2_ref, y_ref, g_ref, b_ref, o_ref):
    o_ref[...] = _layer_norm(DN_ALPHA * h2_ref[...] + y_ref[...], g_ref[...], b_ref[...])


def _fin3(h2, y, g, b):
    T = h2.shape[0]
    tm = ROW_TILE
    row = lambda n: pl.BlockSpec((tm, n), lambda i: (i, 0))
    full = lambda a: pl.BlockSpec(a.shape, lambda i: (0,) * a.ndim)
    return pl.pallas_call(
        _fin3_body,
        grid=(T // tm,),
        in_specs=[row(D_MODEL), row(D_MODEL), full(g), full(b)],
        out_specs=row(D_MODEL),
        out_shape=jax.ShapeDtypeStruct((T, D_MODEL), F32),
        compiler_params=_params(("arbitrary",)),
        name="ln3",
    )(h2, y, g, b)


def kernel(x_prompt, x_sample, cache_cmp_kv, cache_slc_kv, state_win_kv, state_pool, cache_mem_kv, page_table,
           mem_prompt, w_in, pool_w, pool_scale, cmp_pe, cmp_w1, cmp_b1, cmp_w2, cmp_b2, w_out, ln1_g, ln1_b,
           mem_wq, mem_wkv, mem_wo, ln2_g, ln2_b, router_w, router_b, exp_w_gu, exp_b_gu, exp_w_dn, exp_b_dn,
           ln3_g, ln3_b):
    Bp, S, D = x_prompt.shape
    Bd, Ls, _ = x_sample.shape
    Tp, Ts = Bp * S, Bd * Ls
    l = 0
    w_in_bf = w_in[l].astype(BF16)
    pool_w_bf = pool_w[l].astype(BF16)
    ps = pool_scale[l][None, :]
    cw = _compress_weights(cmp_pe[l], cmp_w1[l], cmp_b1[l], cmp_w2[l], cmp_b2[l])
    w_out_bf = w_out[l].astype(BF16)
    wq_bf = mem_wq[l].astype(BF16)
    wo_bf = mem_wo[l].astype(BF16)
    vec = lambda a: a[l][None, :]

    up, qp, kvc_p, kvs_p, kvw_p, gp, pool_p = _inproj_prompt(x_prompt.reshape(Tp, D), w_in_bf, pool_w_bf, ps, S)
    kc_p = _compress_prompt(kvc_p.reshape(Bp, S, KV_WIDTH), cw)
    nsa_p = _nsa_prompt(qp, gp, kc_p, kvs_p.reshape(Bp, S, KV_WIDTH), kvw_p.reshape(Bp, S, KV_WIDTH))
    mem_kv_p = _matmul(mem_prompt.reshape(Bp * MEM_LEN, D), mem_wkv[l]).reshape(Bp, MEM_LEN, 2 * D)
    h1_p, qm_p = _fin1(x_prompt.reshape(Tp, D), pool_p, nsa_p, w_out_bf, vec(ln1_g), vec(ln1_b), wq_bf)
    om_p = _memattn(qm_p.reshape(Bp, S, D), mem_kv_p, ROW_TILE).reshape(Tp, D)
    h2_p, te_p, tg_p = _fin2(h1_p, om_p, wo_bf, vec(ln2_g), vec(ln2_b), router_w[l], vec(router_b))

    state_pad = jnp.pad(state_pool[l], ((0, 0), (1, 0), (0, 0)))
    us, qs, kvc_s, kvs_s, kvw_s, gs, pool_s = _inproj_sample(
        x_sample.reshape(Ts, D), w_in_bf, pool_w_bf, ps, state_pad, Ls, PAST_LEN)
    n_phys = cache_cmp_kv.shape[1]
    w_rows = state_win_kv.shape[2]
    state_win = state_win_kv[l].reshape(Bd, w_rows, KV_WIDTH)
    nsa_s = _nsa_sample(qs.reshape(Bd, Ls, NSA_WIDTH), gs.reshape(Bd, Ls, 3 * NSA_HEADS),
                        kvs_s.reshape(Bd, Ls, KV_WIDTH), kvw_s.reshape(Bd, Ls, KV_WIDTH), state_win,
                        cache_cmp_kv[l].reshape(n_phys, PAGE_SIZE, KV_WIDTH),
                        cache_slc_kv[l].reshape(n_phys, PAGE_SIZE, KV_WIDTH), page_table, cw, PAST_LEN)
    h1_s, qm_s = _fin1(x_sample.reshape(Ts, D), pool_s, nsa_s.reshape(Ts, NSA_WIDTH), w_out_bf,
                       vec(ln1_g), vec(ln1_b), wq_bf)
    om_s = _memattn(qm_s.reshape(Bd, Ls, D), cache_mem_kv[l].reshape(Bd, MEM_LEN, 2 * D), Ls).reshape(Ts, D)
    h2_s, te_s, tg_s = _fin2(h1_s, om_s, wo_bf, vec(ln2_g), vec(ln2_b), router_w[l], vec(router_b))

    T = Tp + Ts
    h2 = jnp.concatenate([h2_p, h2_s], axis=0)
    top_e = jnp.concatenate([te_p[:, :TOP_K], te_s[:, :TOP_K]], axis=0)
    top_g = jnp.concatenate([tg_p[:, :TOP_K], tg_s[:, :TOP_K]], axis=0)
    row_tok, row_w, blk_e, n_used, dest = _moe_routing(top_e, top_g, T)
    x_rows = jnp.concatenate([h2, jnp.zeros((1, D), F32)], axis=0)[row_tok]
    y_rows = _moe_gmm(x_rows, row_w[:, None], blk_e, n_used, exp_w_gu[l], exp_b_gu[l][:, None, :],
                      exp_w_dn[l], exp_b_dn[l][:, None, :])
    y = y_rows[dest].sum(axis=1)
    out = _fin3(h2, y, vec(ln3_g), vec(ln3_b))
    y_prompt = out[:Tp].reshape(Bp, S, D)
    y_sample = out[Tp:].reshape(Bd, Ls, D)

    kv6 = lambda a, b, n: a.reshape(1, b, n, 2, KVH, DH)
    win_p = kvw_p.reshape(Bp, S, KV_WIDTH)[:, S - min(WINDOW, S):]
    win_s = jnp.concatenate([state_win, kvw_s.reshape(Bd, Ls, KV_WIDTH)], axis=1)[:, -w_rows:]
    pool_state_p = up.reshape(Bp, S, POOL_WIDTH)[:, S - POOL_STATE:]
    pool_state_s = jnp.concatenate([state_pool[l], us.reshape(Bd, Ls, POOL_WIDTH)], axis=1)[:, -POOL_STATE:]
    return (y_prompt, y_sample,
            kv6(kvc_p, Bp, S), kv6(kvs_p, Bp, S), kv6(win_p, Bp, min(WINDOW, S)),
            pool_state_p[None], mem_kv_p.reshape(1, Bp, MEM_LEN, 2, MEM_HEADS, MEM_HEAD_DIM),
            kv6(kvc_s, Bd, Ls), kv6(kvs_s, Bd, Ls), kv6(win_s, Bd, w_rows), pool_state_s[None])
```

```python
import functools

import jax
import jax.numpy as jnp
from jax import lax
from jax.experimental import pallas as pl
from jax.experimental.pallas import tpu as pltpu

F32 = jnp.float32
BF16 = jnp.bfloat16
I32 = jnp.int32

D_MODEL = 1024
POOL_WIDTH = 512
POOL_WINDOWS = (2, 4, 8, 16)
POOL_GROUP = 128
POOL_STATE = 15
NSA_WIDTH = 512
DH = 64
NSA_HEADS = 8
KVH = 2
HPG = 4
CMP_LEN = 32
CMP_STRIDE = 16
CMP_HIDDEN = 256
SEL_LEN = 64
SEL_TOP = 16
WINDOW = 512
Q_BLOCK = 128
KV_WIDTH = 256
IN_WIDTH = 1816
ATTN_SCALE = DH ** -0.5
FORCED_SCORE = 1e4
NEG = -1e30
MEM_LEN = 256
MEM_HEADS = 4
MEM_HEAD_DIM = 256
N_EXPERTS = 32
TOP_K = 4
D_FF = 1024
SWIGLU_LIMIT = 7.0
SWIGLU_ALPHA = 1.702
DN_ALPHA = 2.0 ** 0.25
LN_EPS = 1e-5
PAST_LEN = 2048
PAGE_SIZE = 128

LANES = 128
SEL_PAD = 128
KEY_TILE = 512
ROW_TILE = 512
MOE_ROWS = 512
BF16_ROWS = 16
VMEM_LIMIT = 56 * 1024 * 1024

HIGHEST = lax.Precision.HIGHEST


def _dot(a, b):
    return jnp.dot(a, b, preferred_element_type=F32)


def _dot_nt(a, b, precision=None):
    return lax.dot_general(a, b, (((1,), (1,)), ((), ())), preferred_element_type=F32,
                           precision=precision)


def _layer_norm(x, g, b):
    mu = jnp.mean(x, axis=-1, keepdims=True)
    xc = x - mu
    var = jnp.mean(xc * xc, axis=-1, keepdims=True)
    return xc * lax.rsqrt(var + LN_EPS) * g + b


def _params(sem, vmem=VMEM_LIMIT):
    return pltpu.CompilerParams(dimension_semantics=sem, vmem_limit_bytes=vmem)


def _split_store(u, up_ref, q_ref, kvc_ref, kvs_ref, kvw_ref, gate_ref):
    o1 = POOL_WIDTH
    o2 = o1 + NSA_WIDTH
    o3 = o2 + KV_WIDTH
    o4 = o3 + KV_WIDTH
    o5 = o4 + KV_WIDTH
    up_ref[...] = u[:, :o1]
    q_ref[...] = u[:, o1:o2]
    kvc_ref[...] = u[:, o2:o3]
    kvs_ref[...] = u[:, o3:o4]
    kvw_ref[...] = u[:, o4:o5]
    gate_ref[...] = 1.0 / (1.0 + jnp.exp(-u[:, o5:]))


def _inproj_prompt_body(x_ref, w_ref, pw_ref, ps_ref,
                        up_ref, q_ref, kvc_ref, kvs_ref, kvw_ref, gate_ref, pool_ref,
                        ext_ref, *, tm, tiles_per_seq):
    halo = POOL_STATE + 1
    t_in_seq = pl.program_id(0) % tiles_per_seq
    u = _dot(x_ref[...].astype(BF16), w_ref[...])
    _split_store(u, up_ref, q_ref, kvc_ref, kvs_ref, kvw_ref, gate_ref)

    @pl.when(t_in_seq == 0)
    def _():
        ext_ref[0:halo, :] = jnp.zeros((halo, POOL_WIDTH), F32)

    ext_ref[halo:halo + tm, :] = u[:, :POOL_WIDTH]
    pos = t_in_seq * tm + lax.broadcasted_iota(I32, (tm, 1), 0)
    for gi, w in enumerate(POOL_WINDOWS):
        cols = slice(gi * POOL_GROUP, (gi + 1) * POOL_GROUP)
        acc = ext_ref[halo:halo + tm, cols]
        for k in range(1, w):
            acc = acc + ext_ref[halo - k:halo - k + tm, cols]
        cnt = jnp.minimum(pos + 1, w).astype(F32)
        d = acc / cnt - ext_ref[halo:halo + tm, cols]
        o = _dot(d.astype(BF16), pw_ref[gi])
        pool_ref[:, cols] = o * ps_ref[:, cols]
    ext_ref[0:halo, :] = ext_ref[tm:tm + halo, :]


def _inproj_prompt(x2d, w_in_bf, pool_w_bf, pool_scale, seq_len):
    T = x2d.shape[0]
    tm = ROW_TILE
    outs = [POOL_WIDTH, NSA_WIDTH, KV_WIDTH, KV_WIDTH, KV_WIDTH, 3 * NSA_HEADS, POOL_WIDTH]
    row = lambda n: pl.BlockSpec((tm, n), lambda i: (i, 0))
    full = lambda a: pl.BlockSpec(a.shape, lambda i: (0,) * a.ndim)
    return pl.pallas_call(
        functools.partial(_inproj_prompt_body, tm=tm, tiles_per_seq=seq_len // tm),
        grid=(T // tm,),
        in_specs=[row(D_MODEL), full(w_in_bf), full(pool_w_bf), full(pool_scale)],
        out_specs=[row(n) for n in outs],
        out_shape=[jax.ShapeDtypeStruct((T, n), F32) for n in outs],
        scratch_shapes=[pltpu.VMEM((tm + POOL_STATE + 1, POOL_WIDTH), F32)],
        compiler_params=_params(("arbitrary",)),
        name="inproj_prompt",
    )(x2d, w_in_bf, pool_w_bf, pool_scale)


def _inproj_sample_body(x_ref, w_ref, pw_ref, ps_ref, st_ref,
                        up_ref, q_ref, kvc_ref, kvs_ref, kvw_ref, gate_ref, pool_ref,
                        ext_ref, *, nb, ls, pos0):
    halo = POOL_STATE + 1
    tm = nb * ls
    u = _dot(x_ref[...].astype(BF16), w_ref[...])
    _split_store(u, up_ref, q_ref, kvc_ref, kvs_ref, kvw_ref, gate_ref)
    ext_ref[:, 0:halo, :] = st_ref[...]
    ext_ref[:, halo:halo + ls, :] = u[:, :POOL_WIDTH].reshape(nb, ls, POOL_WIDTH)
    pos = pos0 + lax.broadcasted_iota(I32, (1, ls, 1), 1)
    for gi, w in enumerate(POOL_WINDOWS):
        cols = slice(gi * POOL_GROUP, (gi + 1) * POOL_GROUP)
        acc = ext_ref[:, halo:halo + ls, cols]
        for k in range(1, w):
            acc = acc + ext_ref[:, halo - k:halo - k + ls, cols]
        cnt = jnp.minimum(pos + 1, w).astype(F32)
        d = acc / cnt - ext_ref[:, halo:halo + ls, cols]
        o = _dot(d.reshape(tm, POOL_GROUP).astype(BF16), pw_ref[gi])
        pool_ref[:, cols] = o * ps_ref[:, cols]


def _inproj_sample(x2d, w_in_bf, pool_w_bf, pool_scale, state_pad, ls, pos0):
    T = x2d.shape[0]
    nb = ROW_TILE // ls
    tm = nb * ls
    outs = [POOL_WIDTH, NSA_WIDTH, KV_WIDTH, KV_WIDTH, KV_WIDTH, 3 * NSA_HEADS, POOL_WIDTH]
    row = lambda n: pl.BlockSpec((tm, n), lambda i: (i, 0))
    full = lambda a: pl.BlockSpec(a.shape, lambda i: (0,) * a.ndim)
    return pl.pallas_call(
        functools.partial(_inproj_sample_body, nb=nb, ls=ls, pos0=pos0),
        grid=(T // tm,),
        in_specs=[row(D_MODEL), full(w_in_bf), full(pool_w_bf), full(pool_scale),
                  pl.BlockSpec((nb, POOL_STATE + 1, POOL_WIDTH), lambda i: (i, 0, 0))],
        out_specs=[row(n) for n in outs],
        out_shape=[jax.ShapeDtypeStruct((T, n), F32) for n in outs],
        scratch_shapes=[pltpu.VMEM((nb, POOL_STATE + 1 + ls, POOL_WIDTH), F32)],
        compiler_params=_params(("arbitrary",)),
        name="inproj_sample",
    )(x2d, w_in_bf, pool_w_bf, pool_scale, state_pad)


def _matmul_body(x_ref, w_ref, o_ref):
    o_ref[...] = _dot(x_ref[...].astype(BF16), w_ref[...].astype(BF16))


def _matmul(x, w, tn=512):
    M, K = x.shape
    N = w.shape[1]
    return pl.pallas_call(
        _matmul_body,
        grid=(N // tn,),
        in_specs=[pl.BlockSpec((M, K), lambda j: (0, 0)), pl.BlockSpec((K, tn), lambda j: (0, j))],
        out_specs=pl.BlockSpec((M, tn), lambda j: (0, j)),
        out_shape=jax.ShapeDtypeStruct((M, N), F32),
        compiler_params=_params(("arbitrary",)),
        name="mem_kv_proj",
    )(x, w)


def _gelu_tanh(x):
    c = 0.7978845608028654
    return 0.5 * x * (1.0 + jnp.tanh(c * (x + 0.044715 * (x * x * x))))


def _compress(kv_refs, n_chunks, wa_ref, wb_ref, pe_ref, b1_ref, w2_ref, b2_ref):
    outs = []
    for c in range(2):
        acc_a = jnp.zeros((n_chunks, 2 * CMP_HIDDEN), F32)
        acc_b = jnp.zeros((n_chunks, 2 * CMP_HIDDEN), F32)
        for l in range(0, CMP_STRIDE, 2):
            x = jnp.concatenate([kv_refs[c][pl.ds(l, n_chunks, stride=CMP_STRIDE), :],
                                 kv_refs[c][pl.ds(l + 1, n_chunks, stride=CMP_STRIDE), :]], axis=1)
            xa = (x + pe_ref[c, l // 2:l // 2 + 1, :]).astype(BF16)
            xb = (x + pe_ref[c, (CMP_STRIDE + l) // 2:(CMP_STRIDE + l) // 2 + 1, :]).astype(BF16)
            acc_a = acc_a + _dot(xa, wa_ref[c, l // 2])
            acc_b = acc_b + _dot(xb, wb_ref[c, l // 2])
        hid = acc_a + pltpu.roll(acc_b, n_chunks - 1, 0) + b1_ref[c]
        outs.append(_dot(_gelu_tanh(hid).astype(BF16), w2_ref[c]) + b2_ref[c])
    return jnp.concatenate(outs, axis=1)


def _compress_prompt_body(kv_ref, wa_ref, wb_ref, pe_ref, b1_ref, w2_ref, b2_ref, o_ref, k_ref, v_ref,
                          *, n_chunks):
    k_ref[...] = kv_ref[0, :, :LANES]
    v_ref[...] = kv_ref[0, :, LANES:]
    o_ref[0] = _compress((k_ref, v_ref), n_chunks, wa_ref, wb_ref, pe_ref, b1_ref, w2_ref, b2_ref)


def _compress_prompt(kvc, cw):
    B, S, _ = kvc.shape
    n_chunks = S // CMP_STRIDE
    full = lambda a: pl.BlockSpec(a.shape, lambda b: (0,) * a.ndim)
    return pl.pallas_call(
        functools.partial(_compress_prompt_body, n_chunks=n_chunks),
        grid=(B,),
        in_specs=[pl.BlockSpec((1, S, KV_WIDTH), lambda b: (b, 0, 0))] + [full(a) for a in cw],
        out_specs=pl.BlockSpec((1, n_chunks, KV_WIDTH), lambda b: (b, 0, 0)),
        out_shape=jax.ShapeDtypeStruct((B, n_chunks, KV_WIDTH), F32),
        scratch_shapes=[pltpu.VMEM((S, LANES), F32)] * 2,
        compiler_params=_params(("arbitrary",)),
        name="compress_prompt",
    )(kvc, *cw)


def _compress_weights(cmp_pe, cmp_w1, cmp_b1, cmp_w2, cmp_b2):
    z1 = jnp.zeros((2, CMP_LEN, DH, CMP_HIDDEN), F32)
    w1 = jnp.concatenate([jnp.concatenate([cmp_w1, z1], axis=-1),
                          jnp.concatenate([z1, cmp_w1], axis=-1)], axis=2)
    w1 = w1.reshape(2, CMP_LEN // 2, 2 * LANES, 2 * CMP_HIDDEN).astype(BF16)
    wa = w1[:, :CMP_STRIDE // 2]
    wb = w1[:, CMP_STRIDE // 2:]
    pe = jnp.concatenate([cmp_pe, cmp_pe], axis=-1).reshape(2, CMP_LEN // 2, 2 * LANES)
    b1 = jnp.concatenate([cmp_b1, cmp_b1], axis=-1)[:, None, :]
    z2 = jnp.zeros((2, CMP_HIDDEN, DH), F32)
    w2 = jnp.concatenate([jnp.concatenate([cmp_w2, z2], axis=-1),
                          jnp.concatenate([z2, cmp_w2], axis=-1)], axis=1).astype(BF16)
    b2 = jnp.concatenate([cmp_b2, cmp_b2], axis=-1)[:, None, :]
    return wa, wb, pe, b1, w2, b2


def _softmax_rows(s):
    m = jnp.max(s, axis=-1, keepdims=True)
    e = jnp.exp(s - m)
    return e / jnp.sum(e, axis=-1, keepdims=True)


def _lane_iota(n):
    return lax.broadcasted_iota(I32, (n, LANES), 1)


def _key_alibi_cols(pos, lane):
    hi = jnp.left_shift(jnp.right_shift(pos, 6), 6).astype(F32)
    lo = jnp.bitwise_and(pos, SEL_LEN - 1).astype(F32)
    return jnp.where(lane == DH, hi,
                     jnp.where(lane == DH + 1, lo,
                               jnp.where((lane == DH + 2) | (lane == DH + 3), 1.0, 0.0)))


def _key_alibi_rows(pos, sub):
    hi = jnp.left_shift(jnp.right_shift(pos, 6), 6).astype(F32)
    lo = jnp.bitwise_and(pos, SEL_LEN - 1).astype(F32)
    return jnp.where(sub == 0, hi, jnp.where(sub == 1, lo, jnp.where((sub == 2) | (sub == 3), 1.0, 0.0)))


def _halves(x, zero_hi=False):
    lo = _lane_iota(x.shape[0]) < DH
    r = pltpu.roll(x, DH, 1)
    if zero_hi:
        return jnp.where(lo, x, 0.0), jnp.where(lo, r, 0.0)
    return jnp.where(lo, x, r), jnp.where(lo, r, x)


def _fill_queries(qa_ref, qs, q_pos, lq):
    lane = _lane_iota(lq)
    lo_half = lane < DH
    q_hi = jnp.left_shift(jnp.right_shift(q_pos, 7), 7).astype(F32)
    q_lo = jnp.bitwise_and(q_pos, LANES - 1).astype(F32)
    for h in range(NSA_HEADS):
        g, hl = divmod(h, HPG)
        slope = 2.0 ** (-(h + 1))
        slab = qs[:, (h // 2) * LANES:(h // 2 + 1) * LANES]
        if h % 2:
            slab = pltpu.roll(slab, DH, 1)
        ex = jnp.where((lane == DH) | (lane == DH + 1), slope,
                       jnp.where(lane == DH + 2, -slope * q_hi,
                                 jnp.where(lane == DH + 3, -slope * q_lo, 0.0)))
        qa_ref[g, hl * lq:(hl + 1) * lq, 0:LANES] = jnp.where(lo_half, slab, ex).astype(BF16)


def _stack4(x):
    return jnp.concatenate([x] * HPG, axis=0)


def _cmp_branch(qa_ref, kc_k, kc_v, q_pos4, lq, n_cmp):
    c_end = lax.broadcasted_iota(I32, (1, n_cmp), 1) * CMP_STRIDE + (CMP_LEN - 1)
    m_c = c_end <= q_pos4
    any_c = (q_pos4 >= CMP_LEN - 1).astype(F32)
    outs, psums = [], []
    for g in range(KVH):
        s = jnp.where(m_c, _dot_nt(qa_ref[g, :, 0:LANES], kc_k[g][...]), NEG)
        p = _softmax_rows(s) * any_c
        outs.append(_dot(p.astype(BF16), kc_v[g][...]))
        psums.append(p[0:lq] + p[lq:2 * lq] + p[2 * lq:3 * lq] + p[3 * lq:4 * lq])
    return outs, psums


def _select_blocks(psums, ovt_ref, pos0, lq):
    blk = lax.broadcasted_iota(I32, (SEL_PAD, LANES), 0)
    qp_t = pos0 + lax.broadcasted_iota(I32, (SEL_PAD, LANES), 1)
    cur = jnp.right_shift(qp_t, 6)
    forced = (blk == 0) | (blk == cur) | (blk == cur - 1)
    valid = jnp.left_shift(blk, 6) <= qp_t
    vs = []
    for g in range(KVH):
        ps = psums[g]
        if lq < LANES:
            ps = jnp.concatenate([ps, jnp.zeros((LANES - lq, ps.shape[1]), F32)], axis=0)
        imp_t = _dot_nt(ovt_ref[...], ps, precision=HIGHEST)
        vs.append(jnp.where(valid, jnp.where(forced, FORCED_SCORE, imp_t), -1.0))
    v = jnp.concatenate(vs, axis=1)
    blk_f = lax.broadcasted_iota(I32, (SEL_PAD, KVH * LANES), 0).astype(F32)
    sel = jnp.zeros((SEL_PAD, KVH * LANES), F32)
    for _ in range(SEL_TOP):
        m = jnp.max(v, axis=0, keepdims=True)
        idx = jnp.min(jnp.where(v == m, blk_f, float(SEL_PAD)), axis=0, keepdims=True)
        hit = blk_f == idx
        sel = jnp.where(hit, 1.0, sel)
        v = jnp.where(hit, -jnp.inf, v)
    out = []
    for g in range(KVH):
        keep = (sel[:, g * LANES:(g + 1) * LANES] > 0.5) & valid
        out.append(jnp.where(keep, 0.0, NEG).T[:lq].astype(BF16))
    return out


def _store_selbias(qa_ref, selbias, lq):
    for g in range(KVH):
        for hl in range(HPG):
            qa_ref[g, hl * lq:(hl + 1) * lq, LANES:2 * LANES] = selbias[g]


def _combine(gates, eg_ref, o_c, o_s, o_w, lq):
    lo_half = _lane_iota(lq) < DH

    def assemble(per_group):
        slabs = []
        for k in range(NSA_HEADS // 2):
            g, hl = divmod(2 * k, HPG)
            a = per_group[g][hl * lq:(hl + 1) * lq]
            b = per_group[g][(hl + 1) * lq:(hl + 2) * lq]
            slabs.append(jnp.where(lo_half, a, b))
        return jnp.concatenate(slabs, axis=1)

    return (_dot(gates, eg_ref[0]) * assemble(o_c)
            + _dot(gates, eg_ref[1]) * assemble(o_s)
            + _dot(gates, eg_ref[2]) * assemble(o_w))


def _gate_expand():
    r = jnp.arange(3 * NSA_HEADS)
    c = jnp.arange(NSA_WIDTH)
    return jnp.stack([(r[:, None] == 3 * (c[None, :] // DH) + br).astype(F32) for br in range(3)])


def _overlap_t(n_cmp):
    n = jnp.arange(n_cmp)
    s = jnp.arange(SEL_PAD)
    c_first = n * CMP_STRIDE
    c_end = c_first + CMP_LEN - 1
    b_first = s * SEL_LEN
    return ((c_first[None, :] < b_first[:, None] + SEL_LEN) & (c_end[None, :] >= b_first[:, None])).astype(F32)


def _key_rows(kv_f32, pos):
    n = kv_f32.shape[0]
    lane = _lane_iota(n)
    ex = _key_alibi_cols(pos, lane)
    lo = lane < DH
    return jnp.where(lo, kv_f32, ex), jnp.where(lo, pltpu.roll(kv_f32, DH, 1), ex)


def _block_onehot(pos, n):
    return (lax.broadcasted_iota(I32, (n, SEL_PAD), 1) == jnp.right_shift(pos, 6)).astype(BF16)


def _nsa_prompt_body(q_ref, gate_ref, kc_ref, kvs_ref, w0_ref, w1_ref, w2_ref, w3_ref, w4_ref,
                     ovt_ref, eg_ref, o_ref,
                     kck0, kck1, kcv0, kcv1, ka0, ka1, vs0, vs1, qa_ref, *, seq_len):
    j = pl.program_id(1)
    kc_k, kc_v, kaug, vsel = (kck0, kck1), (kcv0, kcv1), (ka0, ka1), (vs0, vs1)
    n_cmp = kc_ref.shape[1]
    lq = Q_BLOCK
    rows = HPG * lq

    @pl.when(j == 0)
    def _():
        ck0, ck1 = _halves(kc_ref[0, :, :LANES], zero_hi=True)
        cv0, cv1 = _halves(kc_ref[0, :, LANES:])
        kck0[...] = ck0.astype(BF16)
        kck1[...] = ck1.astype(BF16)
        kcv0[...] = cv0.astype(BF16)
        kcv1[...] = cv1.astype(BF16)

        def build(i, _):
            r0 = pl.multiple_of(i * KEY_TILE, KEY_TILE)
            pos = r0 + lax.broadcasted_iota(I32, (KEY_TILE, 1), 0)
            k0, k1 = _key_rows(kvs_ref[0, pl.ds(r0, KEY_TILE), :LANES], pos)
            v0, v1 = _halves(kvs_ref[0, pl.ds(r0, KEY_TILE), LANES:])
            onehot = _block_onehot(pos, KEY_TILE)
            ka0[pl.ds(r0, KEY_TILE), :] = jnp.concatenate([k0.astype(BF16), onehot], axis=1)
            ka1[pl.ds(r0, KEY_TILE), :] = jnp.concatenate([k1.astype(BF16), onehot], axis=1)
            vs0[pl.ds(r0, KEY_TILE), :] = v0.astype(BF16)
            vs1[pl.ds(r0, KEY_TILE), :] = v1.astype(BF16)
            return 0

        lax.fori_loop(0, seq_len // KEY_TILE, build, 0)

    st = j * Q_BLOCK
    q_pos = st + lax.broadcasted_iota(I32, (lq, 1), 0)
    q_pos4 = _stack4(q_pos)
    _fill_queries(qa_ref, q_ref[...] * ATTN_SCALE, q_pos, lq)

    o_c, psums = _cmp_branch(qa_ref, kc_k, kc_v, q_pos4, lq, n_cmp)
    _store_selbias(qa_ref, _select_blocks(psums, ovt_ref, st, lq), lq)

    n_tiles = (st + Q_BLOCK + KEY_TILE - 1) // KEY_TILE
    o_s = []
    for g in range(KVH):
        def tile(t, carry, masked, g=g):
            m, l, acc = carry
            r0 = pl.multiple_of(t * KEY_TILE, KEY_TILE)
            s = _dot_nt(qa_ref[g], kaug[g][pl.ds(r0, KEY_TILE), :])
            if masked:
                k_pos = r0 + lax.broadcasted_iota(I32, (1, KEY_TILE), 1)
                s = jnp.where(q_pos4 >= k_pos, s, NEG)
            m_new = jnp.maximum(m, jnp.max(s, axis=-1, keepdims=True))
            a = jnp.exp(m - m_new)
            e = jnp.exp(s - m_new)
            l = a * l + jnp.sum(e, axis=-1, keepdims=True)
            acc = a * acc + _dot(e.astype(BF16), vsel[g][pl.ds(r0, KEY_TILE), :])
            return m_new, l, acc

        init = (jnp.full((rows, 1), NEG, F32), jnp.zeros((rows, 1), F32), jnp.zeros((rows, LANES), F32))
        carry = lax.fori_loop(0, n_tiles - 1, functools.partial(tile, masked=False), init)
        _, l, acc = tile(n_tiles - 1, carry, True)
        o_s.append(acc / l)

    band = jnp.concatenate([w0_ref[0], w1_ref[0], w2_ref[0], w3_ref[0], w4_ref[0]], axis=0)
    n_win = band.shape[0]
    w_pos_col = st - WINDOW + lax.broadcasted_iota(I32, (n_win, 1), 0)
    w_pos = st - WINDOW + lax.broadcasted_iota(I32, (1, n_win), 1)
    kw_k = _key_rows(band[:, :LANES], jnp.maximum(w_pos_col, 0))
    kw_v = _halves(band[:, LANES:])
    d_w = q_pos4 - w_pos
    m_w = (d_w >= 0) & (d_w < WINDOW) & (w_pos >= 0)
    o_w = []
    for g in range(KVH):
        s = jnp.where(m_w, _dot_nt(qa_ref[g, :, 0:LANES], kw_k[g].astype(BF16)), NEG)
        o_w.append(_dot(_softmax_rows(s).astype(BF16), kw_v[g].astype(BF16)))

    o_ref[...] = _combine(gate_ref[...], eg_ref, o_c, o_s, o_w, lq)


def _nsa_prompt(q, gates, kc, kvs, kvw):
    B, S, _ = kvs.shape
    nqb = S // Q_BLOCK
    n_cmp = kc.shape[1]
    ovt = _overlap_t(n_cmp)
    eg = _gate_expand()
    n_band = WINDOW // Q_BLOCK + 1

    def band_spec(i):
        return pl.BlockSpec((1, Q_BLOCK, KV_WIDTH),
                            lambda b, j, i=i: (b, jnp.maximum(j - (n_band - 1) + i, 0), 0))

    full = lambda a: pl.BlockSpec(a.shape, lambda b, j: (0,) * a.ndim)
    return pl.pallas_call(
        functools.partial(_nsa_prompt_body, seq_len=S),
        grid=(B, nqb),
        in_specs=[pl.BlockSpec((Q_BLOCK, NSA_WIDTH), lambda b, j: (b * nqb + j, 0)),
                  pl.BlockSpec((Q_BLOCK, 3 * NSA_HEADS), lambda b, j: (b * nqb + j, 0)),
                  pl.BlockSpec((1, n_cmp, KV_WIDTH), lambda b, j: (b, 0, 0)),
                  pl.BlockSpec((1, S, KV_WIDTH), lambda b, j: (b, 0, 0))]
                 + [band_spec(i) for i in range(n_band)] + [full(ovt), full(eg)],
        out_specs=pl.BlockSpec((Q_BLOCK, NSA_WIDTH), lambda b, j: (b * nqb + j, 0)),
        out_shape=jax.ShapeDtypeStruct((B * S, NSA_WIDTH), F32),
        scratch_shapes=[pltpu.VMEM((n_cmp, LANES), BF16)] * 4
                       + [pltpu.VMEM((S, 2 * LANES), BF16)] * 2
                       + [pltpu.VMEM((S, LANES), BF16)] * 2
                       + [pltpu.VMEM((KVH, HPG * Q_BLOCK, 2 * LANES), BF16)],
        compiler_params=_params(("arbitrary", "arbitrary")),
        name="nsa_prompt",
    )(q, gates, kc, kvs, *([kvw] * n_band), ovt, eg)


def _nsa_sample_body(pt_ref, q_ref, gate_ref, kvs_new_ref, win_ref, kvw_new_ref, *rest,
                     n_pages, ls, past_len):
    cmp_pages = rest[:n_pages]
    slc_pages = rest[n_pages:2 * n_pages]
    (wa_ref, wb_ref, pe_ref, b1_ref, w2_ref, b2_ref, ovt_ref, eg_ref, o_ref,
     full_k, full_v, kck0, kck1, kcv0, kcv1, kt0, kt1, vt0, vt1, wkt0, wkt1, qa_ref) = rest[2 * n_pages:]
    del pt_ref
    kc_k, kc_v, kaug_t, v_t, wk_t = (kck0, kck1), (kcv0, kcv1), (kt0, kt1), (vt0, vt1), (wkt0, wkt1)
    n_cmp = past_len // CMP_STRIDE
    w_rows = win_ref.shape[4]
    lq = BF16_ROWS
    rows = HPG * lq
    w_start = past_len - w_rows

    @pl.when(pl.program_id(0) == 0)
    def _():
        sub = lax.broadcasted_iota(I32, (DH, past_len), 0)
        pos = lax.broadcasted_iota(I32, (1, past_len), 1)
        ex = _key_alibi_rows(pos, sub).astype(BF16)
        onehot = (lax.broadcasted_iota(I32, (SEL_PAD, past_len), 0) == jnp.right_shift(pos, 6)).astype(BF16)
        subw = lax.broadcasted_iota(I32, (DH, w_rows), 0)
        exw = _key_alibi_rows(w_start + lax.broadcasted_iota(I32, (1, w_rows), 1), subw).astype(BF16)
        for g in range(KVH):
            kaug_t[g][DH:2 * DH, :] = ex
            kaug_t[g][2 * DH:, :] = onehot
            wk_t[g][DH:, :] = exw

    for p in range(n_pages):
        cols = slice(p * PAGE_SIZE, (p + 1) * PAGE_SIZE)
        full_k[cols, :] = cmp_pages[p][0, 0].reshape(2 * DH, PAGE_SIZE).T
        full_v[cols, :] = cmp_pages[p][0, 1].reshape(2 * DH, PAGE_SIZE).T
        for g in range(KVH):
            kaug_t[g][0:DH, cols] = slc_pages[p][0, 0, g].astype(BF16)
            vt = slc_pages[p][0, 1, g].astype(BF16)
            v_t[g][0:DH, cols] = vt
            v_t[g][DH:, cols] = vt
    for g in range(KVH):
        wk_t[g][0:DH, :] = win_ref[0, 0, g].astype(BF16)

    kc = _compress((full_k, full_v), n_cmp, wa_ref, wb_ref, pe_ref, b1_ref, w2_ref, b2_ref)
    ck0, ck1 = _halves(kc[:, :LANES], zero_hi=True)
    cv0, cv1 = _halves(kc[:, LANES:])
    kck0[...] = ck0.astype(BF16)
    kck1[...] = ck1.astype(BF16)
    kcv0[...] = cv0.astype(BF16)
    kcv1[...] = cv1.astype(BF16)

    pad_q = jnp.zeros((lq - ls, NSA_WIDTH), F32)
    q_pos = past_len + lax.broadcasted_iota(I32, (lq, 1), 0)
    q_pos4 = _stack4(q_pos)
    _fill_queries(qa_ref, jnp.concatenate([q_ref[0] * ATTN_SCALE, pad_q], axis=0), q_pos, lq)
    gates = jnp.concatenate([gate_ref[0], jnp.zeros((lq - ls, 3 * NSA_HEADS), F32)], axis=0)

    o_c, psums = _cmp_branch(qa_ref, kc_k, kc_v, q_pos4, lq, n_cmp)
    _store_selbias(qa_ref, _select_blocks(psums, ovt_ref, past_len, lq), lq)

    pad_k = jnp.zeros((LANES - ls, KV_WIDTH), F32)
    new_pos_col = past_len + lax.broadcasted_iota(I32, (LANES, 1), 0)
    new_pos = past_len + lax.broadcasted_iota(I32, (1, LANES), 1)
    new_s = jnp.concatenate([kvs_new_ref[0], pad_k], axis=0)
    new_w = jnp.concatenate([kvw_new_ref[0], pad_k], axis=0)
    ks_new = _key_rows(new_s[:, :LANES], new_pos_col)
    vs_new = _halves(new_s[:, LANES:])
    kw_new = _key_rows(new_w[:, :LANES], new_pos_col)
    vw_new = _halves(new_w[:, LANES:])
    onehot_new = _block_onehot(new_pos_col, LANES)
    causal_new = q_pos4 >= new_pos

    o_s, o_w = [], []
    d_past = q_pos4 - (w_start + lax.broadcasted_iota(I32, (1, w_rows), 1))
    m_past = (d_past >= 0) & (d_past < WINDOW)
    d_new = q_pos4 - new_pos
    m_new = (d_new >= 0) & (d_new < WINDOW)
    for g in range(KVH):
        s_past = _dot(qa_ref[g], kaug_t[g][...])
        k_new = jnp.concatenate([ks_new[g].astype(BF16), onehot_new], axis=1)
        s_new = jnp.where(causal_new, _dot_nt(qa_ref[g], k_new), NEG)
        m = jnp.maximum(jnp.max(s_past, axis=-1, keepdims=True), jnp.max(s_new, axis=-1, keepdims=True))
        e_past = jnp.exp(s_past - m)
        e_new = jnp.exp(s_new - m)
        den = jnp.sum(e_past, axis=-1, keepdims=True) + jnp.sum(e_new, axis=-1, keepdims=True)
        acc = _dot_nt(e_past.astype(BF16), v_t[g][...]) + _dot(e_new.astype(BF16), vs_new[g].astype(BF16))
        o_s.append(acc / den)

        sw_past = jnp.where(m_past, _dot(qa_ref[g, :, 0:LANES], wk_t[g][...]), NEG)
        sw_new = jnp.where(m_new, _dot_nt(qa_ref[g, :, 0:LANES], kw_new[g].astype(BF16)), NEG)
        m = jnp.maximum(jnp.max(sw_past, axis=-1, keepdims=True), jnp.max(sw_new, axis=-1, keepdims=True))
        e_past = jnp.exp(sw_past - m)
        e_new = jnp.exp(sw_new - m)
        den = jnp.sum(e_past, axis=-1, keepdims=True) + jnp.sum(e_new, axis=-1, keepdims=True)
        vw = win_ref[0, 1, g].astype(BF16)
        vw2 = jnp.concatenate([vw, vw], axis=0)
        acc = _dot_nt(e_past.astype(BF16), vw2) + _dot(e_new.astype(BF16), vw_new[g].astype(BF16))
        o_w.append(acc / den)

    o_ref[0] = _combine(gates, eg_ref, o_c, o_s, o_w, lq)[:ls]


def _nsa_sample(q, gates, kvs_new, kvw_new, win_t, cmp_t, slc_t, page_table, cw, past_len):
    Bd, ls, _ = q.shape
    n_pages = page_table.shape[1]
    n_cmp = past_len // CMP_STRIDE
    ovt = _overlap_t(n_cmp)
    eg = _gate_expand()
    w_rows = win_t.shape[4]

    per_b = lambda a: pl.BlockSpec((1,) + a.shape[1:], lambda b, pt: (b,) + (0,) * (a.ndim - 1))
    full = lambda a: pl.BlockSpec(a.shape, lambda b, pt: (0,) * a.ndim)
    page = lambda p: pl.BlockSpec((1, 2, KVH, DH, PAGE_SIZE), lambda b, pt, p=p: (pt[b, p], 0, 0, 0, 0))
    grid_spec = pltpu.PrefetchScalarGridSpec(
        num_scalar_prefetch=1,
        grid=(Bd,),
        in_specs=[per_b(q), per_b(gates), per_b(kvs_new), per_b(win_t), per_b(kvw_new)]
                 + [page(p) for p in range(n_pages)] * 2
                 + [full(a) for a in cw] + [full(ovt), full(eg)],
        out_specs=pl.BlockSpec((1, ls, NSA_WIDTH), lambda b, pt: (b, 0, 0)),
        scratch_shapes=[pltpu.VMEM((past_len, LANES), F32)] * 2
                       + [pltpu.VMEM((n_cmp, LANES), BF16)] * 4
                       + [pltpu.VMEM((2 * LANES, past_len), BF16)] * 2
                       + [pltpu.VMEM((LANES, past_len), BF16)] * 2
                       + [pltpu.VMEM((LANES, w_rows), BF16)] * 2
                       + [pltpu.VMEM((KVH, HPG * BF16_ROWS, 2 * LANES), BF16)],
    )
    return pl.pallas_call(
        functools.partial(_nsa_sample_body, n_pages=n_pages, ls=ls, past_len=past_len),
        grid_spec=grid_spec,
        out_shape=jax.ShapeDtypeStruct((Bd, ls, NSA_WIDTH), F32),
        compiler_params=_params(("arbitrary",)),
        name="nsa_sample",
    )(page_table, q, gates, kvs_new, win_t, kvw_new,
      *([cmp_t] * n_pages), *([slc_t] * n_pages), *cw, ovt, eg)


def _fin1_body(h_ref, pool_ref, nsa_ref, wo_ref, g_ref, b_ref, wq_ref, h1_ref, qm_ref):
    mix = (_dot(pool_ref[...].astype(BF16), wo_ref[0:POOL_WIDTH, :])
           + _dot(nsa_ref[...].astype(BF16), wo_ref[POOL_WIDTH:, :]))
    h1 = _layer_norm(DN_ALPHA * h_ref[...] + mix, g_ref[...], b_ref[...])
    h1_ref[...] = h1
    qm_ref[...] = _dot(h1.astype(BF16), wq_ref[...])


def _fin1(h, pool_o, nsa_o, w_out_bf, g, b, wq_bf):
    T = h.shape[0]
    tm = ROW_TILE
    row = lambda n: pl.BlockSpec((tm, n), lambda i: (i, 0))
    full = lambda a: pl.BlockSpec(a.shape, lambda i: (0,) * a.ndim)
    return pl.pallas_call(
        _fin1_body,
        grid=(T // tm,),
        in_specs=[row(D_MODEL), row(POOL_WIDTH), row(NSA_WIDTH), full(w_out_bf), full(g), full(b), full(wq_bf)],
        out_specs=[row(D_MODEL), row(D_MODEL)],
        out_shape=[jax.ShapeDtypeStruct((T, D_MODEL), F32)] * 2,
        compiler_params=_params(("arbitrary",)),
        name="out_proj_ln1",
    )(h, pool_o, nsa_o, w_out_bf, g, b, wq_bf)


def _memattn_body(q_ref, kv_ref, o_ref):
    for h in range(MEM_HEADS):
        cols = slice(h * MEM_HEAD_DIM, (h + 1) * MEM_HEAD_DIM)
        qh = (q_ref[0, :, cols] * (MEM_HEAD_DIM ** -0.5)).astype(BF16)
        kh = kv_ref[0, :, 0, h, :].astype(BF16)
        vh = kv_ref[0, :, 1, h, :].astype(BF16)
        p = _softmax_rows(_dot_nt(qh, kh))
        o_ref[0, :, cols] = _dot(p.astype(BF16), vh)


def _memattn(qm, mem_kv, tq):
    nb, L, W = qm.shape
    return pl.pallas_call(
        _memattn_body,
        grid=(nb, L // tq),
        in_specs=[pl.BlockSpec((1, tq, W), lambda b, t: (b, t, 0)),
                  pl.BlockSpec((1, MEM_LEN, 2, MEM_HEADS, MEM_HEAD_DIM), lambda b, t: (b, 0, 0, 0, 0))],
        out_specs=pl.BlockSpec((1, tq, W), lambda b, t: (b, t, 0)),
        out_shape=jax.ShapeDtypeStruct((nb, L, W), F32),
        compiler_params=_params(("arbitrary", "arbitrary")),
        name="mem_attn",
    )(qm, mem_kv)


def _fin2_body(h1_ref, o_ref, wo_ref, g_ref, b_ref, rw_ref, rb_ref, h2_ref, te_ref, tg_ref):
    tm = h1_ref.shape[0]
    a = _dot(o_ref[...].astype(BF16), wo_ref[...])
    h2 = _layer_norm(DN_ALPHA * h1_ref[...] + a, g_ref[...], b_ref[...])
    h2_ref[...] = h2
    logits = jnp.dot(h2, rw_ref[...], preferred_element_type=F32, precision=HIGHEST) + rb_ref[...]
    e_iota = lax.broadcasted_iota(I32, (tm, N_EXPERTS), 1).astype(F32)
    lane = lax.broadcasted_iota(I32, (tm, LANES), 1)
    te = jnp.zeros((tm, LANES), F32)
    tv = jnp.full((tm, LANES), NEG, F32)
    work = logits
    for k in range(TOP_K):
        m = jnp.max(work, axis=-1, keepdims=True)
        idx = jnp.min(jnp.where(work == m, e_iota, float(N_EXPERTS)), axis=-1, keepdims=True)
        te = jnp.where(lane == k, idx, te)
        tv = jnp.where(lane == k, m, tv)
        work = jnp.where(e_iota == idx, -jnp.inf, work)
    ex = jnp.exp(tv - jnp.max(tv, axis=-1, keepdims=True))
    te_ref[...] = te.astype(I32)
    tg_ref[...] = ex / jnp.sum(ex, axis=-1, keepdims=True)


def _fin2(h1, o, wo_bf, g, b, rw, rb):
    T = h1.shape[0]
    tm = ROW_TILE
    row = lambda n: pl.BlockSpec((tm, n), lambda i: (i, 0))
    full = lambda a: pl.BlockSpec(a.shape, lambda i: (0,) * a.ndim)
    return pl.pallas_call(
        _fin2_body,
        grid=(T // tm,),
        in_specs=[row(D_MODEL), row(D_MODEL), full(wo_bf), full(g), full(b), full(rw), full(rb)],
        out_specs=[row(D_MODEL), row(LANES), row(LANES)],
        out_shape=[jax.ShapeDtypeStruct((T, D_MODEL), F32), jax.ShapeDtypeStruct((T, LANES), I32),
                   jax.ShapeDtypeStruct((T, LANES), F32)],
        compiler_params=_params(("arbitrary",)),
        name="mem_out_ln2_router",
    )(h1, o, wo_bf, g, b, rw, rb)


def _moe_body(be_ref, nu_ref, x_ref, rw_ref, wgu_ref, bgu_ref, wdn_ref, bdn_ref, y_ref, wgu_bf, wdn_bf):
    i = pl.program_id(0)
    e = be_ref[i]
    prev = be_ref[jnp.maximum(i - 1, 0)]

    @pl.when((i == 0) | (e != prev))
    def _():
        wgu_bf[...] = wgu_ref[0].astype(BF16)
        wdn_bf[...] = wdn_ref[0].astype(BF16)

    @pl.when(i < nu_ref[0])
    def _():
        x = x_ref[...].astype(BF16)
        g = _dot(x, wgu_bf[:, :D_FF]) + bgu_ref[0, :, :D_FF]
        u = _dot(x, wgu_bf[:, D_FF:]) + bgu_ref[0, :, D_FF:]
        g = jnp.minimum(g, SWIGLU_LIMIT)
        u = jnp.clip(u, -SWIGLU_LIMIT, SWIGLU_LIMIT)
        a = g * (1.0 / (1.0 + jnp.exp(-SWIGLU_ALPHA * g))) * (u + 1.0)
        y = _dot(a.astype(BF16), wdn_bf[...]) + bdn_ref[0]
        y_ref[...] = y * rw_ref[...]

    @pl.when(i >= nu_ref[0])
    def _():
        y_ref[...] = jnp.zeros(y_ref.shape, F32)


def _moe_gmm(x_rows, row_w, blk_e, n_used, w_gu, b_gu, w_dn, b_dn):
    R = x_rows.shape[0]
    bk = MOE_ROWS
    grid_spec = pltpu.PrefetchScalarGridSpec(
        num_scalar_prefetch=2,
        grid=(R // bk,),
        in_specs=[pl.BlockSpec((bk, D_MODEL), lambda i, be, nu: (i, 0)),
                  pl.BlockSpec((bk, 1), lambda i, be, nu: (i, 0)),
                  pl.BlockSpec((1, D_MODEL, 2 * D_FF), lambda i, be, nu: (be[i], 0, 0)),
                  pl.BlockSpec((1, 1, 2 * D_FF), lambda i, be, nu: (be[i], 0, 0)),
                  pl.BlockSpec((1, D_FF, D_MODEL), lambda i, be, nu: (be[i], 0, 0)),
                  pl.BlockSpec((1, 1, D_MODEL), lambda i, be, nu: (be[i], 0, 0))],
        out_specs=pl.BlockSpec((bk, D_MODEL), lambda i, be, nu: (i, 0)),
        scratch_shapes=[pltpu.VMEM((D_MODEL, 2 * D_FF), BF16), pltpu.VMEM((D_FF, D_MODEL), BF16)],
    )
    return pl.pallas_call(
        _moe_body,
        grid_spec=grid_spec,
        out_shape=jax.ShapeDtypeStruct((R, D_MODEL), F32),
        compiler_params=_params(("arbitrary",)),
        name="moe_experts",
    )(blk_e, n_used, x_rows, row_w, w_gu, b_gu, w_dn, b_dn)


def _moe_routing(top_e, top_g, n_tok):
    bk = MOE_ROWS
    N = n_tok * TOP_K
    e_flat = top_e.reshape(-1)
    g_flat = top_g.reshape(-1)
    order = jnp.argsort(e_flat, stable=True).astype(I32)
    onehot = (e_flat[:, None] == jnp.arange(N_EXPERTS, dtype=I32)[None, :]).astype(I32)
    counts = onehot.sum(axis=0)
    padded = (counts + bk - 1) // bk * bk
    p_end = jnp.cumsum(padded)
    p_start = p_end - padded
    r_start = jnp.cumsum(counts) - counts
    n_blocks = -(-(N + N_EXPERTS * (bk - 1)) // bk)
    R = n_blocks * bk
    blk_e = jnp.clip(jnp.searchsorted(p_end, jnp.arange(n_blocks, dtype=I32) * bk, side='right'),
                     0, N_EXPERTS - 1).astype(I32)
    r = jnp.arange(R, dtype=I32)
    e_r = blk_e[r // bk]
    off = r - p_start[e_r]
    valid = off < counts[e_r]
    src = order[jnp.clip(r_start[e_r] + off, 0, N - 1)]
    row_tok = jnp.where(valid, src // TOP_K, n_tok)
    row_w = jnp.where(valid, g_flat[src], 0.0)
    rank = jnp.take_along_axis(jnp.cumsum(onehot, axis=0), e_flat[:, None], axis=1)[:, 0] - 1
    dest = (p_start[e_flat] + rank).reshape(n_tok, TOP_K)
    n_used = (p_end[-1] // bk).astype(I32).reshape(1)
    return row_tok, row_w, blk_e, n_used, dest


def _fin3_body(h2_ref, y0_ref, y1_ref, y2_ref, y3_ref, g_ref, b_ref, o_ref):
    y = (y0_ref[...] + y1_ref[...]) + (y2_ref[...] + y3_ref[...])
    o_ref[...] = _layer_norm(DN_ALPHA * h2_ref[...] + y, g_ref[...], b_ref[...])


def _fin3(h2, ys, g, b):
    T = h2.shape[0]
    tm = ROW_TILE
    row = lambda n: pl.BlockSpec((tm, n), lambda i: (i, 0))
    full = lambda a: pl.BlockSpec(a.shape, lambda i: (0,) * a.ndim)
    return pl.pallas_call(
        _fin3_body,
        grid=(T // tm,),
        in_specs=[row(D_MODEL)] * (1 + TOP_K) + [full(g), full(b)],
        out_specs=row(D_MODEL),
        out_shape=jax.ShapeDtypeStruct((T, D_MODEL), F32),
        compiler_params=_params(("arbitrary",)),
        name="combine_ln3",
    )(h2, *ys, g, b)


def kernel(x_prompt, x_sample, cache_cmp_kv, cache_slc_kv, state_win_kv, state_pool, cache_mem_kv, page_table,
           mem_prompt, w_in, pool_w, pool_scale, cmp_pe, cmp_w1, cmp_b1, cmp_w2, cmp_b2, w_out, ln1_g, ln1_b,
           mem_wq, mem_wkv, mem_wo, ln2_g, ln2_b, router_w, router_b, exp_w_gu, exp_b_gu, exp_w_dn, exp_b_dn,
           ln3_g, ln3_b):
    Bp, S, D = x_prompt.shape
    Bd, Ls, _ = x_sample.shape
    Tp, Ts = Bp * S, Bd * Ls
    l = 0
    w_in_bf = w_in[l].astype(BF16)
    pool_w_bf = pool_w[l].astype(BF16)
    ps = pool_scale[l][None, :]
    cw = _compress_weights(cmp_pe[l], cmp_w1[l], cmp_b1[l], cmp_w2[l], cmp_b2[l])
    w_out_bf = w_out[l].astype(BF16)
    wq_bf = mem_wq[l].astype(BF16)
    wo_bf = mem_wo[l].astype(BF16)
    vec = lambda a: a[l][None, :]
    mem_shape = (MEM_LEN, 2, MEM_HEADS, MEM_HEAD_DIM)

    up, qp, kvc_p, kvs_p, kvw_p, gp, pool_p = _inproj_prompt(x_prompt.reshape(Tp, D), w_in_bf, pool_w_bf, ps, S)
    kc_p = _compress_prompt(kvc_p.reshape(Bp, S, KV_WIDTH), cw)
    nsa_p = _nsa_prompt(qp, gp, kc_p, kvs_p.reshape(Bp, S, KV_WIDTH), kvw_p.reshape(Bp, S, KV_WIDTH))
    mem_kv_p = _matmul(mem_prompt.reshape(Bp * MEM_LEN, D), mem_wkv[l]).reshape((Bp,) + mem_shape)
    h1_p, qm_p = _fin1(x_prompt.reshape(Tp, D), pool_p, nsa_p, w_out_bf, vec(ln1_g), vec(ln1_b), wq_bf)
    om_p = _memattn(qm_p.reshape(Bp, S, D), mem_kv_p, ROW_TILE).reshape(Tp, D)
    h2_p, te_p, tg_p = _fin2(h1_p, om_p, wo_bf, vec(ln2_g), vec(ln2_b), router_w[l], vec(router_b))

    state_pad = jnp.pad(state_pool[l], ((0, 0), (1, 0), (0, 0)))
    us, qs, kvc_s, kvs_s, kvw_s, gs, pool_s = _inproj_sample(
        x_sample.reshape(Ts, D), w_in_bf, pool_w_bf, ps, state_pad, Ls, PAST_LEN)
    w_rows = state_win_kv.shape[2]
    feature_major = lambda a: jnp.transpose(a, (0, 2, 3, 4, 1))
    nsa_s = _nsa_sample(qs.reshape(Bd, Ls, NSA_WIDTH), gs.reshape(Bd, Ls, 3 * NSA_HEADS),
                        kvs_s.reshape(Bd, Ls, KV_WIDTH), kvw_s.reshape(Bd, Ls, KV_WIDTH),
                        feature_major(state_win_kv[l]), feature_major(cache_cmp_kv[l]),
                        feature_major(cache_slc_kv[l]), page_table, cw, PAST_LEN)
    h1_s, qm_s = _fin1(x_sample.reshape(Ts, D), pool_s, nsa_s.reshape(Ts, NSA_WIDTH), w_out_bf,
                       vec(ln1_g), vec(ln1_b), wq_bf)
    om_s = _memattn(qm_s.reshape(Bd, Ls, D), cache_mem_kv[l], Ls).reshape(Ts, D)
    h2_s, te_s, tg_s = _fin2(h1_s, om_s, wo_bf, vec(ln2_g), vec(ln2_b), router_w[l], vec(router_b))

    T = Tp + Ts
    h2 = jnp.concatenate([h2_p, h2_s], axis=0)
    top_e = jnp.concatenate([te_p[:, :TOP_K], te_s[:, :TOP_K]], axis=0)
    top_g = jnp.concatenate([tg_p[:, :TOP_K], tg_s[:, :TOP_K]], axis=0)
    row_tok, row_w, blk_e, n_used, dest = _moe_routing(top_e, top_g, T)
    x_rows = jnp.concatenate([h2, jnp.zeros((1, D), F32)], axis=0)[row_tok]
    y_rows = _moe_gmm(x_rows, row_w[:, None], blk_e, n_used, exp_w_gu[l], exp_b_gu[l][:, None, :],
                      exp_w_dn[l], exp_b_dn[l][:, None, :])
    out = _fin3(h2, [y_rows[dest[:, k]] for k in range(TOP_K)], vec(ln3_g), vec(ln3_b))
    y_prompt = out[:Tp].reshape(Bp, S, D)
    y_sample = out[Tp:].reshape(Bd, Ls, D)

    kv6 = lambda a, b, n: a.reshape(1, b, n, 2, KVH, DH)
    win_p = kvw_p.reshape(Bp, S, KV_WIDTH)[:, S - min(WINDOW, S):]
    win_s = jnp.concatenate([state_win_kv[l], kvw_s.reshape(Bd, Ls, 2, KVH, DH)], axis=1)[:, -w_rows:]
    pool_state_p = up.reshape(Bp, S, POOL_WIDTH)[:, S - POOL_STATE:]
    pool_state_s = jnp.concatenate([state_pool[l], us.reshape(Bd, Ls, POOL_WIDTH)], axis=1)[:, -POOL_STATE:]
    return (y_prompt, y_sample,
            kv6(kvc_p, Bp, S), kv6(kvs_p, Bp, S), kv6(win_p, Bp, min(WINDOW, S)),
            pool_state_p[None], mem_kv_p[None],
            kv6(kvc_s, Bd, Ls), kv6(kvs_s, Bd, Ls), win_s[None], pool_state_s[None])
```

```python
import functools

import jax
import jax.numpy as jnp
from jax import lax
from jax.experimental import pallas as pl
from jax.experimental.pallas import tpu as pltpu

F32 = jnp.float32
BF16 = jnp.bfloat16
I32 = jnp.int32

D_MODEL = 1024
POOL_WIDTH = 512
POOL_WINDOWS = (2, 4, 8, 16)
POOL_GROUP = 128
POOL_STATE = 15
NSA_WIDTH = 512
DH = 64
NSA_HEADS = 8
KVH = 2
HPG = 4
CMP_LEN = 32
CMP_STRIDE = 16
CMP_HIDDEN = 256
SEL_LEN = 64
SEL_TOP = 16
WINDOW = 512
Q_BLOCK = 128
KV_WIDTH = 256
IN_WIDTH = 1816
ATTN_SCALE = DH ** -0.5
FORCED_SCORE = 1e4
NEG = -1e30
MEM_LEN = 256
MEM_HEADS = 4
MEM_HEAD_DIM = 256
N_EXPERTS = 32
TOP_K = 4
D_FF = 1024
SWIGLU_LIMIT = 7.0
SWIGLU_ALPHA = 1.702
DN_ALPHA = 2.0 ** 0.25
LN_EPS = 1e-5
PAST_LEN = 2048
PAGE_SIZE = 128

LANES = 128
SEL_PAD = 128
KEY_TILE = 512
ROW_TILE = 512
MOE_ROWS = 512
BF16_ROWS = 16
VMEM_LIMIT = 56 * 1024 * 1024

HIGHEST = lax.Precision.HIGHEST


def _dot(a, b):
    return jnp.dot(a, b, preferred_element_type=F32)


def _dot_nt(a, b, precision=None):
    return lax.dot_general(a, b, (((1,), (1,)), ((), ())), preferred_element_type=F32,
                           precision=precision)


def _layer_norm(x, g, b):
    mu = jnp.mean(x, axis=-1, keepdims=True)
    xc = x - mu
    var = jnp.mean(xc * xc, axis=-1, keepdims=True)
    return xc * lax.rsqrt(var + LN_EPS) * g + b


def _params(sem, vmem=VMEM_LIMIT):
    return pltpu.CompilerParams(dimension_semantics=sem, vmem_limit_bytes=vmem)


def _split_store(u, up_ref, q_ref, kvc_ref, kvs_ref, kvw_ref, gate_ref):
    o1 = POOL_WIDTH
    o2 = o1 + NSA_WIDTH
    o3 = o2 + KV_WIDTH
    o4 = o3 + KV_WIDTH
    o5 = o4 + KV_WIDTH
    up_ref[...] = u[:, :o1]
    q_ref[...] = u[:, o1:o2]
    kvc_ref[...] = u[:, o2:o3]
    kvs_ref[...] = u[:, o3:o4]
    kvw_ref[...] = u[:, o4:o5]
    gate_ref[...] = 1.0 / (1.0 + jnp.exp(-u[:, o5:]))


def _inproj_prompt_body(x_ref, w_ref, pw_ref, ps_ref,
                        up_ref, q_ref, kvc_ref, kvs_ref, kvw_ref, gate_ref, pool_ref,
                        ext_ref, *, tm, tiles_per_seq):
    halo = POOL_STATE + 1
    t_in_seq = pl.program_id(0) % tiles_per_seq
    u = _dot(x_ref[...].astype(BF16), w_ref[...])
    _split_store(u, up_ref, q_ref, kvc_ref, kvs_ref, kvw_ref, gate_ref)

    @pl.when(t_in_seq == 0)
    def _():
        ext_ref[0:halo, :] = jnp.zeros((halo, POOL_WIDTH), F32)

    ext_ref[halo:halo + tm, :] = u[:, :POOL_WIDTH]
    pos = t_in_seq * tm + lax.broadcasted_iota(I32, (tm, 1), 0)
    for gi, w in enumerate(POOL_WINDOWS):
        cols = slice(gi * POOL_GROUP, (gi + 1) * POOL_GROUP)
        acc = ext_ref[halo:halo + tm, cols]
        for k in range(1, w):
            acc = acc + ext_ref[halo - k:halo - k + tm, cols]
        cnt = jnp.minimum(pos + 1, w).astype(F32)
        d = acc / cnt - ext_ref[halo:halo + tm, cols]
        o = _dot(d.astype(BF16), pw_ref[gi])
        pool_ref[:, cols] = o * ps_ref[:, cols]
    ext_ref[0:halo, :] = ext_ref[tm:tm + halo, :]


def _inproj_prompt(x2d, w_in_bf, pool_w_bf, pool_scale, seq_len):
    T = x2d.shape[0]
    tm = ROW_TILE
    outs = [POOL_WIDTH, NSA_WIDTH, KV_WIDTH, KV_WIDTH, KV_WIDTH, 3 * NSA_HEADS, POOL_WIDTH]
    row = lambda n: pl.BlockSpec((tm, n), lambda i: (i, 0))
    full = lambda a: pl.BlockSpec(a.shape, lambda i: (0,) * a.ndim)
    return pl.pallas_call(
        functools.partial(_inproj_prompt_body, tm=tm, tiles_per_seq=seq_len // tm),
        grid=(T // tm,),
        in_specs=[row(D_MODEL), full(w_in_bf), full(pool_w_bf), full(pool_scale)],
        out_specs=[row(n) for n in outs],
        out_shape=[jax.ShapeDtypeStruct((T, n), F32) for n in outs],
        scratch_shapes=[pltpu.VMEM((tm + POOL_STATE + 1, POOL_WIDTH), F32)],
        compiler_params=_params(("arbitrary",)),
        name="inproj_prompt",
    )(x2d, w_in_bf, pool_w_bf, pool_scale)


def _inproj_sample_body(x_ref, w_ref, pw_ref, ps_ref, st_ref,
                        up_ref, q_ref, kvc_ref, kvs_ref, kvw_ref, gate_ref, pool_ref,
                        ext_ref, *, nb, ls, pos0):
    halo = POOL_STATE + 1
    tm = nb * ls
    u = _dot(x_ref[...].astype(BF16), w_ref[...])
    _split_store(u, up_ref, q_ref, kvc_ref, kvs_ref, kvw_ref, gate_ref)
    ext_ref[:, 0:halo, :] = st_ref[...]
    ext_ref[:, halo:halo + ls, :] = u[:, :POOL_WIDTH].reshape(nb, ls, POOL_WIDTH)
    pos = pos0 + lax.broadcasted_iota(I32, (1, ls, 1), 1)
    for gi, w in enumerate(POOL_WINDOWS):
        cols = slice(gi * POOL_GROUP, (gi + 1) * POOL_GROUP)
        acc = ext_ref[:, halo:halo + ls, cols]
        for k in range(1, w):
            acc = acc + ext_ref[:, halo - k:halo - k + ls, cols]
        cnt = jnp.minimum(pos + 1, w).astype(F32)
        d = acc / cnt - ext_ref[:, halo:halo + ls, cols]
        o = _dot(d.reshape(tm, POOL_GROUP).astype(BF16), pw_ref[gi])
        pool_ref[:, cols] = o * ps_ref[:, cols]


def _inproj_sample(x2d, w_in_bf, pool_w_bf, pool_scale, state_pad, ls, pos0):
    T = x2d.shape[0]
    nb = ROW_TILE // ls
    tm = nb * ls
    outs = [POOL_WIDTH, NSA_WIDTH, KV_WIDTH, KV_WIDTH, KV_WIDTH, 3 * NSA_HEADS, POOL_WIDTH]
    row = lambda n: pl.BlockSpec((tm, n), lambda i: (i, 0))
    full = lambda a: pl.BlockSpec(a.shape, lambda i: (0,) * a.ndim)
    return pl.pallas_call(
        functools.partial(_inproj_sample_body, nb=nb, ls=ls, pos0=pos0),
        grid=(T // tm,),
        in_specs=[row(D_MODEL), full(w_in_bf), full(pool_w_bf), full(pool_scale),
                  pl.BlockSpec((nb, POOL_STATE + 1, POOL_WIDTH), lambda i: (i, 0, 0))],
        out_specs=[row(n) for n in outs],
        out_shape=[jax.ShapeDtypeStruct((T, n), F32) for n in outs],
        scratch_shapes=[pltpu.VMEM((nb, POOL_STATE + 1 + ls, POOL_WIDTH), F32)],
        compiler_params=_params(("arbitrary",)),
        name="inproj_sample",
    )(x2d, w_in_bf, pool_w_bf, pool_scale, state_pad)


def _matmul_body(x_ref, w_ref, o_ref):
    o_ref[...] = _dot(x_ref[...].astype(BF16), w_ref[...].astype(BF16))


def _matmul(x, w, tn=512):
    M, K = x.shape
    N = w.shape[1]
    return pl.pallas_call(
        _matmul_body,
        grid=(N // tn,),
        in_specs=[pl.BlockSpec((M, K), lambda j: (0, 0)), pl.BlockSpec((K, tn), lambda j: (0, j))],
        out_specs=pl.BlockSpec((M, tn), lambda j: (0, j)),
        out_shape=jax.ShapeDtypeStruct((M, N), F32),
        compiler_params=_params(("arbitrary",)),
        name="mem_kv_proj",
    )(x, w)


def _gelu_tanh(x):
    c = 0.7978845608028654
    return 0.5 * x * (1.0 + jnp.tanh(c * (x + 0.044715 * (x * x * x))))


def _compress(kv_refs, n_chunks, wa_ref, wb_ref, pe_ref, b1_ref, w2_ref, b2_ref):
    outs = []
    for c in range(2):
        acc_a = jnp.zeros((n_chunks, 2 * CMP_HIDDEN), F32)
        acc_b = jnp.zeros((n_chunks, 2 * CMP_HIDDEN), F32)
        for l in range(0, CMP_STRIDE, 2):
            x = jnp.concatenate([kv_refs[c][pl.ds(l, n_chunks, stride=CMP_STRIDE), :],
                                 kv_refs[c][pl.ds(l + 1, n_chunks, stride=CMP_STRIDE), :]], axis=1)
            xa = (x + pe_ref[c, l // 2:l // 2 + 1, :]).astype(BF16)
            xb = (x + pe_ref[c, (CMP_STRIDE + l) // 2:(CMP_STRIDE + l) // 2 + 1, :]).astype(BF16)
            acc_a = acc_a + _dot(xa, wa_ref[c, l // 2])
            acc_b = acc_b + _dot(xb, wb_ref[c, l // 2])
        hid = acc_a + pltpu.roll(acc_b, n_chunks - 1, 0) + b1_ref[c]
        outs.append(_dot(_gelu_tanh(hid).astype(BF16), w2_ref[c]) + b2_ref[c])
    return jnp.concatenate(outs, axis=1)


def _compress_prompt_body(kv_ref, wa_ref, wb_ref, pe_ref, b1_ref, w2_ref, b2_ref, o_ref, k_ref, v_ref,
                          *, n_chunks):
    k_ref[...] = kv_ref[0, :, :LANES]
    v_ref[...] = kv_ref[0, :, LANES:]
    o_ref[0] = _compress((k_ref, v_ref), n_chunks, wa_ref, wb_ref, pe_ref, b1_ref, w2_ref, b2_ref)


def _compress_prompt(kvc, cw):
    B, S, _ = kvc.shape
    n_chunks = S // CMP_STRIDE
    full = lambda a: pl.BlockSpec(a.shape, lambda b: (0,) * a.ndim)
    return pl.pallas_call(
        functools.partial(_compress_prompt_body, n_chunks=n_chunks),
        grid=(B,),
        in_specs=[pl.BlockSpec((1, S, KV_WIDTH), lambda b: (b, 0, 0))] + [full(a) for a in cw],
        out_specs=pl.BlockSpec((1, n_chunks, KV_WIDTH), lambda b: (b, 0, 0)),
        out_shape=jax.ShapeDtypeStruct((B, n_chunks, KV_WIDTH), F32),
        scratch_shapes=[pltpu.VMEM((S, LANES), F32)] * 2,
        compiler_params=_params(("arbitrary",)),
        name="compress_prompt",
    )(kvc, *cw)


def _compress_weights(cmp_pe, cmp_w1, cmp_b1, cmp_w2, cmp_b2):
    z1 = jnp.zeros((2, CMP_LEN, DH, CMP_HIDDEN), F32)
    w1 = jnp.concatenate([jnp.concatenate([cmp_w1, z1], axis=-1),
                          jnp.concatenate([z1, cmp_w1], axis=-1)], axis=2)
    w1 = w1.reshape(2, CMP_LEN // 2, 2 * LANES, 2 * CMP_HIDDEN).astype(BF16)
    wa = w1[:, :CMP_STRIDE // 2]
    wb = w1[:, CMP_STRIDE // 2:]
    pe = jnp.concatenate([cmp_pe, cmp_pe], axis=-1).reshape(2, CMP_LEN // 2, 2 * LANES)
    b1 = jnp.concatenate([cmp_b1, cmp_b1], axis=-1)[:, None, :]
    z2 = jnp.zeros((2, CMP_HIDDEN, DH), F32)
    w2 = jnp.concatenate([jnp.concatenate([cmp_w2, z2], axis=-1),
                          jnp.concatenate([z2, cmp_w2], axis=-1)], axis=1).astype(BF16)
    b2 = jnp.concatenate([cmp_b2, cmp_b2], axis=-1)[:, None, :]
    return wa, wb, pe, b1, w2, b2


def _softmax_rows(s):
    m = jnp.max(s, axis=-1, keepdims=True)
    e = jnp.exp(s - m)
    return e / jnp.sum(e, axis=-1, keepdims=True)


def _lane_iota(n):
    return lax.broadcasted_iota(I32, (n, LANES), 1)


def _key_alibi_cols(pos, lane):
    hi = jnp.left_shift(jnp.right_shift(pos, 6), 6).astype(F32)
    lo = jnp.bitwise_and(pos, SEL_LEN - 1).astype(F32)
    return jnp.where(lane == DH, hi,
                     jnp.where(lane == DH + 1, lo,
                               jnp.where((lane == DH + 2) | (lane == DH + 3), 1.0, 0.0)))


def _key_alibi_rows(pos, sub):
    hi = jnp.left_shift(jnp.right_shift(pos, 6), 6).astype(F32)
    lo = jnp.bitwise_and(pos, SEL_LEN - 1).astype(F32)
    return jnp.where(sub == 0, hi, jnp.where(sub == 1, lo, jnp.where((sub == 2) | (sub == 3), 1.0, 0.0)))


def _halves(x, zero_hi=False):
    lo = _lane_iota(x.shape[0]) < DH
    r = pltpu.roll(x, DH, 1)
    if zero_hi:
        return jnp.where(lo, x, 0.0), jnp.where(lo, r, 0.0)
    return jnp.where(lo, x, r), jnp.where(lo, r, x)


def _fill_queries(qa_ref, qs, q_pos, lq):
    lane = _lane_iota(lq)
    lo_half = lane < DH
    q_hi = jnp.left_shift(jnp.right_shift(q_pos, 7), 7).astype(F32)
    q_lo = jnp.bitwise_and(q_pos, LANES - 1).astype(F32)
    for h in range(NSA_HEADS):
        g, hl = divmod(h, HPG)
        slope = 2.0 ** (-(h + 1))
        slab = qs[:, (h // 2) * LANES:(h // 2 + 1) * LANES]
        if h % 2:
            slab = pltpu.roll(slab, DH, 1)
        ex = jnp.where((lane == DH) | (lane == DH + 1), slope,
                       jnp.where(lane == DH + 2, -slope * q_hi,
                                 jnp.where(lane == DH + 3, -slope * q_lo, 0.0)))
        qa_ref[g, hl * lq:(hl + 1) * lq, 0:LANES] = jnp.where(lo_half, slab, ex).astype(BF16)


def _stack4(x):
    return jnp.concatenate([x] * HPG, axis=0)


def _cmp_branch(qa_ref, kc_k, kc_v, q_pos4, lq, n_cmp):
    c_end = lax.broadcasted_iota(I32, (1, n_cmp), 1) * CMP_STRIDE + (CMP_LEN - 1)
    m_c = c_end <= q_pos4
    any_c = (q_pos4 >= CMP_LEN - 1).astype(F32)
    outs, psums = [], []
    for g in range(KVH):
        s = jnp.where(m_c, _dot_nt(qa_ref[g, :, 0:LANES], kc_k[g][...]), NEG)
        p = _softmax_rows(s) * any_c
        outs.append(_dot(p.astype(BF16), kc_v[g][...]))
        psums.append(p[0:lq] + p[lq:2 * lq] + p[2 * lq:3 * lq] + p[3 * lq:4 * lq])
    return outs, psums


def _select_blocks(psums, ovt_ref, pos0, lq):
    blk = lax.broadcasted_iota(I32, (SEL_PAD, LANES), 0)
    qp_t = pos0 + lax.broadcasted_iota(I32, (SEL_PAD, LANES), 1)
    cur = jnp.right_shift(qp_t, 6)
    forced = (blk == 0) | (blk == cur) | (blk == cur - 1)
    valid = jnp.left_shift(blk, 6) <= qp_t
    vs = []
    for g in range(KVH):
        ps = psums[g]
        if lq < LANES:
            ps = jnp.concatenate([ps, jnp.zeros((LANES - lq, ps.shape[1]), F32)], axis=0)
        imp_t = _dot_nt(ovt_ref[...], ps, precision=HIGHEST)
        vs.append(jnp.where(valid, jnp.where(forced, FORCED_SCORE, imp_t), -1.0))
    v = jnp.concatenate(vs, axis=1)
    blk_f = lax.broadcasted_iota(I32, (SEL_PAD, KVH * LANES), 0).astype(F32)
    sel = jnp.zeros((SEL_PAD, KVH * LANES), F32)
    for _ in range(SEL_TOP):
        m = jnp.max(v, axis=0, keepdims=True)
        idx = jnp.min(jnp.where(v == m, blk_f, float(SEL_PAD)), axis=0, keepdims=True)
        hit = blk_f == idx
        sel = jnp.where(hit, 1.0, sel)
        v = jnp.where(hit, -jnp.inf, v)
    out = []
    for g in range(KVH):
        keep = (sel[:, g * LANES:(g + 1) * LANES] > 0.5) & valid
        out.append(jnp.where(keep, 0.0, NEG).T[:lq].astype(BF16))
    return out


def _store_selbias(qa_ref, selbias, lq):
    for g in range(KVH):
        for hl in range(HPG):
            qa_ref[g, hl * lq:(hl + 1) * lq, LANES:2 * LANES] = selbias[g]


def _combine(gates, eg_ref, o_c, o_s, o_w, lq):
    lo_half = _lane_iota(lq) < DH

    def assemble(per_group):
        slabs = []
        for k in range(NSA_HEADS // 2):
            g, hl = divmod(2 * k, HPG)
            a = per_group[g][hl * lq:(hl + 1) * lq]
            b = per_group[g][(hl + 1) * lq:(hl + 2) * lq]
            slabs.append(jnp.where(lo_half, a, b))
        return jnp.concatenate(slabs, axis=1)

    return (_dot(gates, eg_ref[0]) * assemble(o_c)
            + _dot(gates, eg_ref[1]) * assemble(o_s)
            + _dot(gates, eg_ref[2]) * assemble(o_w))


def _gate_expand():
    r = jnp.arange(3 * NSA_HEADS)
    c = jnp.arange(NSA_WIDTH)
    return jnp.stack([(r[:, None] == 3 * (c[None, :] // DH) + br).astype(F32) for br in range(3)])


def _overlap_t(n_cmp):
    n = jnp.arange(n_cmp)
    s = jnp.arange(SEL_PAD)
    c_first = n * CMP_STRIDE
    c_end = c_first + CMP_LEN - 1
    b_first = s * SEL_LEN
    return ((c_first[None, :] < b_first[:, None] + SEL_LEN) & (c_end[None, :] >= b_first[:, None])).astype(F32)


def _key_rows(kv_f32, pos):
    n = kv_f32.shape[0]
    lane = _lane_iota(n)
    ex = _key_alibi_cols(pos, lane)
    lo = lane < DH
    return jnp.where(lo, kv_f32, ex), jnp.where(lo, pltpu.roll(kv_f32, DH, 1), ex)


def _block_onehot(pos, n):
    return (lax.broadcasted_iota(I32, (n, SEL_PAD), 1) == jnp.right_shift(pos, 6)).astype(BF16)


def _nsa_prompt_body(q_ref, gate_ref, kc_ref, kvs_ref, w0_ref, w1_ref, w2_ref, w3_ref, w4_ref,
                     ovt_ref, eg_ref, o_ref,
                     kck0, kck1, kcv0, kcv1, ka0, ka1, vs0, vs1, qa_ref, *, seq_len):
    j = pl.program_id(1)
    kc_k, kc_v, kaug, vsel = (kck0, kck1), (kcv0, kcv1), (ka0, ka1), (vs0, vs1)
    n_cmp = kc_ref.shape[1]
    lq = Q_BLOCK
    rows = HPG * lq

    @pl.when(j == 0)
    def _():
        ck0, ck1 = _halves(kc_ref[0, :, :LANES], zero_hi=True)
        cv0, cv1 = _halves(kc_ref[0, :, LANES:])
        kck0[...] = ck0.astype(BF16)
        kck1[...] = ck1.astype(BF16)
        kcv0[...] = cv0.astype(BF16)
        kcv1[...] = cv1.astype(BF16)

        def build(i, _):
            r0 = pl.multiple_of(i * KEY_TILE, KEY_TILE)
            pos = r0 + lax.broadcasted_iota(I32, (KEY_TILE, 1), 0)
            k0, k1 = _key_rows(kvs_ref[0, pl.ds(r0, KEY_TILE), :LANES], pos)
            v0, v1 = _halves(kvs_ref[0, pl.ds(r0, KEY_TILE), LANES:])
            onehot = _block_onehot(pos, KEY_TILE)
            ka0[pl.ds(r0, KEY_TILE), :] = jnp.concatenate([k0.astype(BF16), onehot], axis=1)
            ka1[pl.ds(r0, KEY_TILE), :] = jnp.concatenate([k1.astype(BF16), onehot], axis=1)
            vs0[pl.ds(r0, KEY_TILE), :] = v0.astype(BF16)
            vs1[pl.ds(r0, KEY_TILE), :] = v1.astype(BF16)
            return 0

        lax.fori_loop(0, seq_len // KEY_TILE, build, 0)

    st = j * Q_BLOCK
    q_pos = st + lax.broadcasted_iota(I32, (lq, 1), 0)
    q_pos4 = _stack4(q_pos)
    _fill_queries(qa_ref, q_ref[...] * ATTN_SCALE, q_pos, lq)

    o_c, psums = _cmp_branch(qa_ref, kc_k, kc_v, q_pos4, lq, n_cmp)
    _store_selbias(qa_ref, _select_blocks(psums, ovt_ref, st, lq), lq)

    n_tiles = (st + Q_BLOCK + KEY_TILE - 1) // KEY_TILE
    o_s = []
    for g in range(KVH):
        def tile(t, carry, masked, g=g):
            m, l, acc = carry
            r0 = pl.multiple_of(t * KEY_TILE, KEY_TILE)
            s = _dot_nt(qa_ref[g], kaug[g][pl.ds(r0, KEY_TILE), :])
            if masked:
                k_pos = r0 + lax.broadcasted_iota(I32, (1, KEY_TILE), 1)
                s = jnp.where(q_pos4 >= k_pos, s, NEG)
            m_new = jnp.maximum(m, jnp.max(s, axis=-1, keepdims=True))
            a = jnp.exp(m - m_new)
            e = jnp.exp(s - m_new)
            l = a * l + jnp.sum(e, axis=-1, keepdims=True)
            acc = a * acc + _dot(e.astype(BF16), vsel[g][pl.ds(r0, KEY_TILE), :])
            return m_new, l, acc

        init = (jnp.full((rows, 1), NEG, F32), jnp.zeros((rows, 1), F32), jnp.zeros((rows, LANES), F32))
        carry = lax.fori_loop(0, n_tiles - 1, functools.partial(tile, masked=False), init)
        _, l, acc = tile(n_tiles - 1, carry, True)
        o_s.append(acc / l)

    band = jnp.concatenate([w0_ref[0], w1_ref[0], w2_ref[0], w3_ref[0], w4_ref[0]], axis=0)
    n_win = band.shape[0]
    w_pos_col = st - WINDOW + lax.broadcasted_iota(I32, (n_win, 1), 0)
    w_pos = st - WINDOW + lax.broadcasted_iota(I32, (1, n_win), 1)
    kw_k = _key_rows(band[:, :LANES], jnp.maximum(w_pos_col, 0))
    kw_v = _halves(band[:, LANES:])
    d_w = q_pos4 - w_pos
    m_w = (d_w >= 0) & (d_w < WINDOW) & (w_pos >= 0)
    o_w = []
    for g in range(KVH):
        s = jnp.where(m_w, _dot_nt(qa_ref[g, :, 0:LANES], kw_k[g].astype(BF16)), NEG)
        o_w.append(_dot(_softmax_rows(s).astype(BF16), kw_v[g].astype(BF16)))

    o_ref[...] = _combine(gate_ref[...], eg_ref, o_c, o_s, o_w, lq)


def _nsa_prompt(q, gates, kc, kvs, kvw):
    B, S, _ = kvs.shape
    nqb = S // Q_BLOCK
    n_cmp = kc.shape[1]
    ovt = _overlap_t(n_cmp)
    eg = _gate_expand()
    n_band = WINDOW // Q_BLOCK + 1

    def band_spec(i):
        return pl.BlockSpec((1, Q_BLOCK, KV_WIDTH),
                            lambda b, j, i=i: (b, jnp.maximum(j - (n_band - 1) + i, 0), 0))

    full = lambda a: pl.BlockSpec(a.shape, lambda b, j: (0,) * a.ndim)
    return pl.pallas_call(
        functools.partial(_nsa_prompt_body, seq_len=S),
        grid=(B, nqb),
        in_specs=[pl.BlockSpec((Q_BLOCK, NSA_WIDTH), lambda b, j: (b * nqb + j, 0)),
                  pl.BlockSpec((Q_BLOCK, 3 * NSA_HEADS), lambda b, j: (b * nqb + j, 0)),
                  pl.BlockSpec((1, n_cmp, KV_WIDTH), lambda b, j: (b, 0, 0)),
                  pl.BlockSpec((1, S, KV_WIDTH), lambda b, j: (b, 0, 0))]
                 + [band_spec(i) for i in range(n_band)] + [full(ovt), full(eg)],
        out_specs=pl.BlockSpec((Q_BLOCK, NSA_WIDTH), lambda b, j: (b * nqb + j, 0)),
        out_shape=jax.ShapeDtypeStruct((B * S, NSA_WIDTH), F32),
        scratch_shapes=[pltpu.VMEM((n_cmp, LANES), BF16)] * 4
                       + [pltpu.VMEM((S, 2 * LANES), BF16)] * 2
                       + [pltpu.VMEM((S, LANES), BF16)] * 2
                       + [pltpu.VMEM((KVH, HPG * Q_BLOCK, 2 * LANES), BF16)],
        compiler_params=_params(("arbitrary", "arbitrary")),
        name="nsa_prompt",
    )(q, gates, kc, kvs, *([kvw] * n_band), ovt, eg)


def _nsa_sample_body(pt_ref, q_ref, gate_ref, kvs_new_ref, win_ref, kvw_new_ref, *rest,
                     n_pages, ls, past_len):
    cmp_pages = rest[:n_pages]
    slc_pages = rest[n_pages:2 * n_pages]
    (wa_ref, wb_ref, pe_ref, b1_ref, w2_ref, b2_ref, ovt_ref, eg_ref, o_ref,
     full_k, full_v, kck0, kck1, kcv0, kcv1, kt0, kt1, vt0, vt1, wkt0, wkt1, qa_ref) = rest[2 * n_pages:]
    del pt_ref
    kc_k, kc_v, kaug_t, v_t, wk_t = (kck0, kck1), (kcv0, kcv1), (kt0, kt1), (vt0, vt1), (wkt0, wkt1)
    n_cmp = past_len // CMP_STRIDE
    w_rows = win_ref.shape[4]
    lq = BF16_ROWS
    rows = HPG * lq
    w_start = past_len - w_rows

    @pl.when(pl.program_id(0) == 0)
    def _():
        sub = lax.broadcasted_iota(I32, (DH, past_len), 0)
        pos = lax.broadcasted_iota(I32, (1, past_len), 1)
        ex = _key_alibi_rows(pos, sub).astype(BF16)
        onehot = (lax.broadcasted_iota(I32, (SEL_PAD, past_len), 0) == jnp.right_shift(pos, 6)).astype(BF16)
        subw = lax.broadcasted_iota(I32, (DH, w_rows), 0)
        exw = _key_alibi_rows(w_start + lax.broadcasted_iota(I32, (1, w_rows), 1), subw).astype(BF16)
        for g in range(KVH):
            kaug_t[g][DH:2 * DH, :] = ex
            kaug_t[g][2 * DH:, :] = onehot
            wk_t[g][DH:, :] = exw

    for p in range(n_pages):
        cols = slice(p * PAGE_SIZE, (p + 1) * PAGE_SIZE)
        full_k[cols, :] = cmp_pages[p][0, 0].reshape(2 * DH, PAGE_SIZE).T
        full_v[cols, :] = cmp_pages[p][0, 1].reshape(2 * DH, PAGE_SIZE).T
        for g in range(KVH):
            kaug_t[g][0:DH, cols] = slc_pages[p][0, 0, g].astype(BF16)
            vt = slc_pages[p][0, 1, g].astype(BF16)
            v_t[g][0:DH, cols] = vt
            v_t[g][DH:, cols] = vt
    for g in range(KVH):
        wk_t[g][0:DH, :] = win_ref[0, 0, g].astype(BF16)

    kc = _compress((full_k, full_v), n_cmp, wa_ref, wb_ref, pe_ref, b1_ref, w2_ref, b2_ref)
    ck0, ck1 = _halves(kc[:, :LANES], zero_hi=True)
    cv0, cv1 = _halves(kc[:, LANES:])
    kck0[...] = ck0.astype(BF16)
    kck1[...] = ck1.astype(BF16)
    kcv0[...] = cv0.astype(BF16)
    kcv1[...] = cv1.astype(BF16)

    pad_q = jnp.zeros((lq - ls, NSA_WIDTH), F32)
    q_pos = past_len + lax.broadcasted_iota(I32, (lq, 1), 0)
    q_pos4 = _stack4(q_pos)
    _fill_queries(qa_ref, jnp.concatenate([q_ref[0] * ATTN_SCALE, pad_q], axis=0), q_pos, lq)
    gates = jnp.concatenate([gate_ref[0], jnp.zeros((lq - ls, 3 * NSA_HEADS), F32)], axis=0)

    o_c, psums = _cmp_branch(qa_ref, kc_k, kc_v, q_pos4, lq, n_cmp)
    _store_selbias(qa_ref, _select_blocks(psums, ovt_ref, past_len, lq), lq)

    pad_k = jnp.zeros((LANES - ls, KV_WIDTH), F32)
    new_pos_col = past_len + lax.broadcasted_iota(I32, (LANES, 1), 0)
    new_pos = past_len + lax.broadcasted_iota(I32, (1, LANES), 1)
    new_s = jnp.concatenate([kvs_new_ref[0], pad_k], axis=0)
    new_w = jnp.concatenate([kvw_new_ref[0], pad_k], axis=0)
    ks_new = _key_rows(new_s[:, :LANES], new_pos_col)
    vs_new = _halves(new_s[:, LANES:])
    kw_new = _key_rows(new_w[:, :LANES], new_pos_col)
    vw_new = _halves(new_w[:, LANES:])
    onehot_new = _block_onehot(new_pos_col, LANES)
    causal_new = q_pos4 >= new_pos

    o_s, o_w = [], []
    d_past = q_pos4 - (w_start + lax.broadcasted_iota(I32, (1, w_rows), 1))
    m_past = (d_past >= 0) & (d_past < WINDOW)
    d_new = q_pos4 - new_pos
    m_new = (d_new >= 0) & (d_new < WINDOW)
    for g in range(KVH):
        s_past = _dot(qa_ref[g], kaug_t[g][...])
        k_new = jnp.concatenate([ks_new[g].astype(BF16), onehot_new], axis=1)
        s_new = jnp.where(causal_new, _dot_nt(qa_ref[g], k_new), NEG)
        m = jnp.maximum(jnp.max(s_past, axis=-1, keepdims=True), jnp.max(s_new, axis=-1, keepdims=True))
        e_past = jnp.exp(s_past - m)
        e_new = jnp.exp(s_new - m)
        den = jnp.sum(e_past, axis=-1, keepdims=True) + jnp.sum(e_new, axis=-1, keepdims=True)
        acc = _dot_nt(e_past.astype(BF16), v_t[g][...]) + _dot(e_new.astype(BF16), vs_new[g].astype(BF16))
        o_s.append(acc / den)

        sw_past = jnp.where(m_past, _dot(qa_ref[g, :, 0:LANES], wk_t[g][...]), NEG)
        sw_new = jnp.where(m_new, _dot_nt(qa_ref[g, :, 0:LANES], kw_new[g].astype(BF16)), NEG)
        m = jnp.maximum(jnp.max(sw_past, axis=-1, keepdims=True), jnp.max(sw_new, axis=-1, keepdims=True))
        e_past = jnp.exp(sw_past - m)
        e_new = jnp.exp(sw_new - m)
        den = jnp.sum(e_past, axis=-1, keepdims=True) + jnp.sum(e_new, axis=-1, keepdims=True)
        vw = win_ref[0, 1, g].astype(BF16)
        vw2 = jnp.concatenate([vw, vw], axis=0)
        acc = _dot_nt(e_past.astype(BF16), vw2) + _dot(e_new.astype(BF16), vw_new[g].astype(BF16))
        o_w.append(acc / den)

    o_ref[0] = _combine(gates, eg_ref, o_c, o_s, o_w, lq)[:ls]


def _nsa_sample(q, gates, kvs_new, kvw_new, win_t, cmp_t, slc_t, page_table, cw, past_len):
    Bd, ls, _ = q.shape
    n_pages = page_table.shape[1]
    n_cmp = past_len // CMP_STRIDE
    ovt = _overlap_t(n_cmp)
    eg = _gate_expand()
    w_rows = win_t.shape[4]

    per_b = lambda a: pl.BlockSpec((1,) + a.shape[1:], lambda b, pt: (b,) + (0,) * (a.ndim - 1))
    full = lambda a: pl.BlockSpec(a.shape, lambda b, pt: (0,) * a.ndim)
    page = lambda p: pl.BlockSpec((1, 2, KVH, DH, PAGE_SIZE), lambda b, pt, p=p: (pt[b, p], 0, 0, 0, 0))
    grid_spec = pltpu.PrefetchScalarGridSpec(
        num_scalar_prefetch=1,
        grid=(Bd,),
        in_specs=[per_b(q), per_b(gates), per_b(kvs_new), per_b(win_t), per_b(kvw_new)]
                 + [page(p) for p in range(n_pages)] * 2
                 + [full(a) for a in cw] + [full(ovt), full(eg)],
        out_specs=pl.BlockSpec((1, ls, NSA_WIDTH), lambda b, pt: (b, 0, 0)),
        scratch_shapes=[pltpu.VMEM((past_len, LANES), F32)] * 2
                       + [pltpu.VMEM((n_cmp, LANES), BF16)] * 4
                       + [pltpu.VMEM((2 * LANES, past_len), BF16)] * 2
                       + [pltpu.VMEM((LANES, past_len), BF16)] * 2
                       + [pltpu.VMEM((LANES, w_rows), BF16)] * 2
                       + [pltpu.VMEM((KVH, HPG * BF16_ROWS, 2 * LANES), BF16)],
    )
    return pl.pallas_call(
        functools.partial(_nsa_sample_body, n_pages=n_pages, ls=ls, past_len=past_len),
        grid_spec=grid_spec,
        out_shape=jax.ShapeDtypeStruct((Bd, ls, NSA_WIDTH), F32),
        compiler_params=_params(("arbitrary",)),
        name="nsa_sample",
    )(page_table, q, gates, kvs_new, win_t, kvw_new,
      *([cmp_t] * n_pages), *([slc_t] * n_pages), *cw, ovt, eg)


def _fin1_body(h_ref, pool_ref, nsa_ref, wo_ref, g_ref, b_ref, wq_ref, h1_ref, qm_ref):
    mix = (_dot(pool_ref[...].astype(BF16), wo_ref[0:POOL_WIDTH, :])
           + _dot(nsa_ref[...].astype(BF16), wo_ref[POOL_WIDTH:, :]))
    h1 = _layer_norm(DN_ALPHA * h_ref[...] + mix, g_ref[...], b_ref[...])
    h1_ref[...] = h1
    qm_ref[...] = _dot(h1.astype(BF16), wq_ref[...])


def _fin1(h, pool_o, nsa_o, w_out_bf, g, b, wq_bf):
    T = h.shape[0]
    tm = ROW_TILE
    row = lambda n: pl.BlockSpec((tm, n), lambda i: (i, 0))
    full = lambda a: pl.BlockSpec(a.shape, lambda i: (0,) * a.ndim)
    return pl.pallas_call(
        _fin1_body,
        grid=(T // tm,),
        in_specs=[row(D_MODEL), row(POOL_WIDTH), row(NSA_WIDTH), full(w_out_bf), full(g), full(b), full(wq_bf)],
        out_specs=[row(D_MODEL), row(D_MODEL)],
        out_shape=[jax.ShapeDtypeStruct((T, D_MODEL), F32)] * 2,
        compiler_params=_params(("arbitrary",)),
        name="out_proj_ln1",
    )(h, pool_o, nsa_o, w_out_bf, g, b, wq_bf)


def _memattn_body(q_ref, kv_ref, o_ref):
    for h in range(MEM_HEADS):
        cols = slice(h * MEM_HEAD_DIM, (h + 1) * MEM_HEAD_DIM)
        qh = (q_ref[0, :, cols] * (MEM_HEAD_DIM ** -0.5)).astype(BF16)
        kh = kv_ref[0, :, 0, h, :].astype(BF16)
        vh = kv_ref[0, :, 1, h, :].astype(BF16)
        p = _softmax_rows(_dot_nt(qh, kh))
        o_ref[0, :, cols] = _dot(p.astype(BF16), vh)


def _memattn(qm, mem_kv, tq):
    nb, L, W = qm.shape
    return pl.pallas_call(
        _memattn_body,
        grid=(nb, L // tq),
        in_specs=[pl.BlockSpec((1, tq, W), lambda b, t: (b, t, 0)),
                  pl.BlockSpec((1, MEM_LEN, 2, MEM_HEADS, MEM_HEAD_DIM), lambda b, t: (b, 0, 0, 0, 0))],
        out_specs=pl.BlockSpec((1, tq, W), lambda b, t: (b, t, 0)),
        out_shape=jax.ShapeDtypeStruct((nb, L, W), F32),
        compiler_params=_params(("arbitrary", "arbitrary")),
        name="mem_attn",
    )(qm, mem_kv)


def _fin2_body(cnt0_ref, h1_ref, o_ref, wo_ref, g_ref, b_ref, rw_ref, rb_ref,
               h2_ref, te_ref, tg_ref, cnt_ref, run_ref):
    tm = h1_ref.shape[0]

    @pl.when(pl.program_id(0) == 0)
    def _():
        run_ref[...] = cnt0_ref[...]

    a = _dot(o_ref[...].astype(BF16), wo_ref[...])
    h2 = _layer_norm(DN_ALPHA * h1_ref[...] + a, g_ref[...], b_ref[...])
    h2_ref[...] = h2
    logits = jnp.dot(h2, rw_ref[...], preferred_element_type=F32, precision=HIGHEST) + rb_ref[...]
    e_iota = lax.broadcasted_iota(I32, (tm, N_EXPERTS), 1).astype(F32)
    lane = lax.broadcasted_iota(I32, (tm, LANES), 1)
    te = jnp.zeros((tm, LANES), F32)
    tv = jnp.full((tm, LANES), NEG, F32)
    work = logits
    chosen = []
    for k in range(TOP_K):
        m = jnp.max(work, axis=-1, keepdims=True)
        idx = jnp.min(jnp.where(work == m, e_iota, float(N_EXPERTS)), axis=-1, keepdims=True)
        hit = e_iota == idx
        chosen.append(hit)
        te = jnp.where(lane == k, idx, te)
        tv = jnp.where(lane == k, m, tv)
        work = jnp.where(hit, -jnp.inf, work)
    member = sum(c.astype(F32) for c in chosen)
    earlier = (lax.broadcasted_iota(I32, (tm, tm), 0) > lax.broadcasted_iota(I32, (tm, tm), 1)).astype(BF16)
    before = _dot(earlier, member.astype(BF16)) + run_ref[...]
    for k in range(TOP_K):
        rank = jnp.sum(jnp.where(chosen[k], before, 0.0), axis=-1, keepdims=True)
        te = jnp.where(lane == TOP_K + k, rank, te)
    run_ref[...] = run_ref[...] + jnp.sum(member, axis=0, keepdims=True)
    cnt_ref[...] = run_ref[...]
    ex = jnp.exp(tv - jnp.max(tv, axis=-1, keepdims=True))
    te_ref[...] = te.astype(I32)
    tg_ref[...] = ex / jnp.sum(ex, axis=-1, keepdims=True)


def _fin2(cnt0, h1, o, wo_bf, g, b, rw, rb):
    T = h1.shape[0]
    tm = ROW_TILE
    row = lambda n: pl.BlockSpec((tm, n), lambda i: (i, 0))
    full = lambda a: pl.BlockSpec(a.shape, lambda i: (0,) * a.ndim)
    return pl.pallas_call(
        _fin2_body,
        grid=(T // tm,),
        in_specs=[full(cnt0), row(D_MODEL), row(D_MODEL), full(wo_bf), full(g), full(b), full(rw), full(rb)],
        out_specs=[row(D_MODEL), row(LANES), row(LANES), full(cnt0)],
        out_shape=[jax.ShapeDtypeStruct((T, D_MODEL), F32), jax.ShapeDtypeStruct((T, LANES), I32),
                   jax.ShapeDtypeStruct((T, LANES), F32), jax.ShapeDtypeStruct(cnt0.shape, F32)],
        scratch_shapes=[pltpu.VMEM(cnt0.shape, F32)],
        compiler_params=_params(("arbitrary",)),
        name="mem_out_ln2_router",
    )(cnt0, h1, o, wo_bf, g, b, rw, rb)


def _moe_body(ut_ref, ue_ref, nu_ref, rs_ref, re_ref, x_ref, rw_ref, wgu_ref, bgu_ref, wdn_ref, bdn_ref,
              y_ref, wgu_bf, wdn_bf):
    u = pl.program_id(0)
    bk = x_ref.shape[0]
    e = ue_ref[u]
    tile = ut_ref[u]
    prev = jnp.maximum(u - 1, 0)

    @pl.when((u == 0) | (e != ue_ref[prev]))
    def _():
        wgu_bf[...] = wgu_ref[0].astype(BF16)
        wdn_bf[...] = wdn_ref[0].astype(BF16)

    @pl.when(u < nu_ref[0])
    def _():
        x = x_ref[...].astype(BF16)
        g = _dot(x, wgu_bf[:, :D_FF]) + bgu_ref[0, :, :D_FF]
        v = _dot(x, wgu_bf[:, D_FF:]) + bgu_ref[0, :, D_FF:]
        g = jnp.minimum(g, SWIGLU_LIMIT)
        v = jnp.clip(v, -SWIGLU_LIMIT, SWIGLU_LIMIT)
        a = g * (1.0 / (1.0 + jnp.exp(-SWIGLU_ALPHA * g))) * (v + 1.0)
        y = _dot(a.astype(BF16), wdn_bf[...]) + bdn_ref[0]
        row = tile * bk + lax.broadcasted_iota(I32, (bk, 1), 0)
        mine = (row >= rs_ref[e]) & (row < re_ref[e])
        y = jnp.where(mine, y * rw_ref[...], 0.0)

        @pl.when((u == 0) | (tile != ut_ref[prev]))
        def _():
            y_ref[...] = y

        @pl.when((u > 0) & (tile == ut_ref[prev]))
        def _():
            y_ref[...] = y_ref[...] + y


def _moe_gmm(x_rows, row_w, units, w_gu, b_gu, w_dn, b_dn):
    N = x_rows.shape[0]
    bk = MOE_ROWS
    unit_tile, unit_e, n_units, r_start, r_end = units
    grid_spec = pltpu.PrefetchScalarGridSpec(
        num_scalar_prefetch=5,
        grid=(unit_tile.shape[0],),
        in_specs=[pl.BlockSpec((bk, D_MODEL), lambda u, ut, ue, *_: (ut[u], 0)),
                  pl.BlockSpec((bk, 1), lambda u, ut, ue, *_: (ut[u], 0)),
                  pl.BlockSpec((1, D_MODEL, 2 * D_FF), lambda u, ut, ue, *_: (ue[u], 0, 0)),
                  pl.BlockSpec((1, 1, 2 * D_FF), lambda u, ut, ue, *_: (ue[u], 0, 0)),
                  pl.BlockSpec((1, D_FF, D_MODEL), lambda u, ut, ue, *_: (ue[u], 0, 0)),
                  pl.BlockSpec((1, 1, D_MODEL), lambda u, ut, ue, *_: (ue[u], 0, 0))],
        out_specs=pl.BlockSpec((bk, D_MODEL), lambda u, ut, ue, *_: (ut[u], 0)),
        scratch_shapes=[pltpu.VMEM((D_MODEL, 2 * D_FF), BF16), pltpu.VMEM((D_FF, D_MODEL), BF16)],
    )
    return pl.pallas_call(
        _moe_body,
        grid_spec=grid_spec,
        out_shape=jax.ShapeDtypeStruct((N, D_MODEL), F32),
        compiler_params=_params(("arbitrary",)),
        name="moe_experts",
    )(unit_tile, unit_e, n_units, r_start, r_end, x_rows, row_w, w_gu, b_gu, w_dn, b_dn)


FLAT_BITS = 17


def _moe_routing(te, tg, counts):
    bk = MOE_ROWS
    T = te.shape[0]
    N = T * TOP_K
    assert N % bk == 0 and N <= (1 << FLAT_BITS)
    experts = jnp.arange(N_EXPERTS, dtype=I32)
    top_e = te[:, :TOP_K]
    r_end = jnp.cumsum(counts).astype(I32)
    r_start = r_end - counts
    onehot = top_e[:, :, None] == experts[None, None, :]
    pos = jnp.sum(jnp.where(onehot, r_start[None, None, :], 0), axis=-1) + te[:, TOP_K:2 * TOP_K]
    key = jnp.left_shift(top_e.reshape(-1), FLAT_BITS) + jnp.arange(N, dtype=I32)
    key_s, gate_s = lax.sort((key, tg[:, :TOP_K].reshape(-1)), num_keys=1)
    tok_s = jnp.right_shift(jnp.bitwise_and(key_s, (1 << FLAT_BITS) - 1), 2)
    first = r_start // bk
    last = (r_end - 1) // bk
    n_e = jnp.where(counts > 0, last - first + 1, 0)
    u_end = jnp.cumsum(n_e).astype(I32)
    u_start = u_end - n_e
    n_units = u_end[-1]
    u = jnp.minimum(jnp.arange(N // bk + N_EXPERTS - 1, dtype=I32), n_units - 1)
    unit_e = jnp.sum((u[:, None] >= u_end[None, :]).astype(I32), axis=1)
    unit_tile = u + jnp.sum(jnp.where(unit_e[:, None] == experts[None, :], (first - u_start)[None, :], 0), axis=1)
    return pos, tok_s, gate_s, (unit_tile, unit_e, n_units.reshape(1), r_start, r_end)


def _fin3_body(h2_ref, y0_ref, y1_ref, y2_ref, y3_ref, g_ref, b_ref, o_ref):
    y = (y0_ref[...] + y1_ref[...]) + (y2_ref[...] + y3_ref[...])
    o_ref[...] = _layer_norm(DN_ALPHA * h2_ref[...] + y, g_ref[...], b_ref[...])


def _fin3(h2, ys, g, b):
    T = h2.shape[0]
    tm = ROW_TILE
    row = lambda n: pl.BlockSpec((tm, n), lambda i: (i, 0))
    full = lambda a: pl.BlockSpec(a.shape, lambda i: (0,) * a.ndim)
    return pl.pallas_call(
        _fin3_body,
        grid=(T // tm,),
        in_specs=[row(D_MODEL)] * (1 + TOP_K) + [full(g), full(b)],
        out_specs=row(D_MODEL),
        out_shape=jax.ShapeDtypeStruct((T, D_MODEL), F32),
        compiler_params=_params(("arbitrary",)),
        name="combine_ln3",
    )(h2, *ys, g, b)


def kernel(x_prompt, x_sample, cache_cmp_kv, cache_slc_kv, state_win_kv, state_pool, cache_mem_kv, page_table,
           mem_prompt, w_in, pool_w, pool_scale, cmp_pe, cmp_w1, cmp_b1, cmp_w2, cmp_b2, w_out, ln1_g, ln1_b,
           mem_wq, mem_wkv, mem_wo, ln2_g, ln2_b, router_w, router_b, exp_w_gu, exp_b_gu, exp_w_dn, exp_b_dn,
           ln3_g, ln3_b):
    Bp, S, D = x_prompt.shape
    Bd, Ls, _ = x_sample.shape
    Tp, Ts = Bp * S, Bd * Ls
    l = 0
    w_in_bf = w_in[l].astype(BF16)
    pool_w_bf = pool_w[l].astype(BF16)
    ps = pool_scale[l][None, :]
    cw = _compress_weights(cmp_pe[l], cmp_w1[l], cmp_b1[l], cmp_w2[l], cmp_b2[l])
    w_out_bf = w_out[l].astype(BF16)
    wq_bf = mem_wq[l].astype(BF16)
    wo_bf = mem_wo[l].astype(BF16)
    vec = lambda a: a[l][None, :]
    mem_shape = (MEM_LEN, 2, MEM_HEADS, MEM_HEAD_DIM)

    up, qp, kvc_p, kvs_p, kvw_p, gp, pool_p = _inproj_prompt(x_prompt.reshape(Tp, D), w_in_bf, pool_w_bf, ps, S)
    kc_p = _compress_prompt(kvc_p.reshape(Bp, S, KV_WIDTH), cw)
    nsa_p = _nsa_prompt(qp, gp, kc_p, kvs_p.reshape(Bp, S, KV_WIDTH), kvw_p.reshape(Bp, S, KV_WIDTH))
    mem_kv_p = _matmul(mem_prompt.reshape(Bp * MEM_LEN, D), mem_wkv[l]).reshape((Bp,) + mem_shape)
    h1_p, qm_p = _fin1(x_prompt.reshape(Tp, D), pool_p, nsa_p, w_out_bf, vec(ln1_g), vec(ln1_b), wq_bf)
    om_p = _memattn(qm_p.reshape(Bp, S, D), mem_kv_p, ROW_TILE).reshape(Tp, D)
    h2_p, te_p, tg_p, cnt_p = _fin2(jnp.zeros((1, N_EXPERTS), F32), h1_p, om_p, wo_bf, vec(ln2_g), vec(ln2_b),
                                    router_w[l], vec(router_b))

    state_pad = jnp.pad(state_pool[l], ((0, 0), (1, 0), (0, 0)))
    us, qs, kvc_s, kvs_s, kvw_s, gs, pool_s = _inproj_sample(
        x_sample.reshape(Ts, D), w_in_bf, pool_w_bf, ps, state_pad, Ls, PAST_LEN)
    w_rows = state_win_kv.shape[2]
    feature_major = lambda a: jnp.transpose(a, (0, 2, 3, 4, 1))
    nsa_s = _nsa_sample(qs.reshape(Bd, Ls, NSA_WIDTH), gs.reshape(Bd, Ls, 3 * NSA_HEADS),
                        kvs_s.reshape(Bd, Ls, KV_WIDTH), kvw_s.reshape(Bd, Ls, KV_WIDTH),
                        feature_major(state_win_kv[l]), feature_major(cache_cmp_kv[l]),
                        feature_major(cache_slc_kv[l]), page_table, cw, PAST_LEN)
    h1_s, qm_s = _fin1(x_sample.reshape(Ts, D), pool_s, nsa_s.reshape(Ts, NSA_WIDTH), w_out_bf,
                       vec(ln1_g), vec(ln1_b), wq_bf)
    om_s = _memattn(qm_s.reshape(Bd, Ls, D), cache_mem_kv[l], Ls).reshape(Ts, D)
    h2_s, te_s, tg_s, cnt_s = _fin2(cnt_p, h1_s, om_s, wo_bf, vec(ln2_g), vec(ln2_b), router_w[l], vec(router_b))

    h2 = jnp.concatenate([h2_p, h2_s], axis=0)
    pos, tok_s, gate_s, units = _moe_routing(jnp.concatenate([te_p, te_s], axis=0),
                                             jnp.concatenate([tg_p, tg_s], axis=0), cnt_s[0].astype(I32))
    y_rows = _moe_gmm(h2[tok_s], gate_s[:, None], units, exp_w_gu[l], exp_b_gu[l][:, None, :],
                      exp_w_dn[l], exp_b_dn[l][:, None, :])
    out = _fin3(h2, [y_rows[pos[:, k]] for k in range(TOP_K)], vec(ln3_g), vec(ln3_b))
    y_prompt = out[:Tp].reshape(Bp, S, D)
    y_sample = out[Tp:].reshape(Bd, Ls, D)

    kv6 = lambda a, b, n: a.reshape(1, b, n, 2, KVH, DH)
    win_p = kvw_p.reshape(Bp, S, KV_WIDTH)[:, S - min(WINDOW, S):]
    win_s = jnp.concatenate([state_win_kv[l], kvw_s.reshape(Bd, Ls, 2, KVH, DH)], axis=1)[:, -w_rows:]
    pool_state_p = up.reshape(Bp, S, POOL_WIDTH)[:, S - POOL_STATE:]
    pool_state_s = jnp.concatenate([state_pool[l], us.reshape(Bd, Ls, POOL_WIDTH)], axis=1)[:, -POOL_STATE:]
    return (y_prompt, y_sample,
            kv6(kvc_p, Bp, S), kv6(kvs_p, Bp, S), kv6(win_p, Bp, min(WINDOW, S)),
            pool_state_p[None], mem_kv_p[None],
            kv6(kvc_s, Bd, Ls), kv6(kvs_s, Bd, Ls), win_s[None], pool_state_s[None])
```

```python
import functools

import jax
import jax.numpy as jnp
from jax import lax
from jax.experimental import pallas as pl
from jax.experimental.pallas import tpu as pltpu

F32 = jnp.float32
BF16 = jnp.bfloat16
I32 = jnp.int32

D_MODEL = 1024
POOL_WIDTH = 512
POOL_WINDOWS = (2, 4, 8, 16)
POOL_GROUP = 128
POOL_STATE = 15
NSA_WIDTH = 512
DH = 64
NSA_HEADS = 8
KVH = 2
HPG = 4
CMP_LEN = 32
CMP_STRIDE = 16
CMP_HIDDEN = 256
SEL_LEN = 64
SEL_TOP = 16
WINDOW = 512
Q_BLOCK = 128
KV_WIDTH = 256
IN_WIDTH = 1816
ATTN_SCALE = DH ** -0.5
FORCED_SCORE = 1e4
NEG = -1e30
MEM_LEN = 256
MEM_HEADS = 4
MEM_HEAD_DIM = 256
N_EXPERTS = 32
TOP_K = 4
D_FF = 1024
SWIGLU_LIMIT = 7.0
SWIGLU_ALPHA = 1.702
DN_ALPHA = 2.0 ** 0.25
LN_EPS = 1e-5
PAST_LEN = 2048
PAGE_SIZE = 128

LANES = 128
SEL_PAD = 128
KEY_TILE = 512
ROW_TILE = 512
MOE_ROWS = 512
BF16_ROWS = 16
VMEM_LIMIT = 56 * 1024 * 1024

HIGHEST = lax.Precision.HIGHEST


def _dot(a, b):
    return jnp.dot(a, b, preferred_element_type=F32)


def _dot_nt(a, b, precision=None):
    return lax.dot_general(a, b, (((1,), (1,)), ((), ())), preferred_element_type=F32,
                           precision=precision)


def _layer_norm(x, g, b):
    mu = jnp.mean(x, axis=-1, keepdims=True)
    xc = x - mu
    var = jnp.mean(xc * xc, axis=-1, keepdims=True)
    return xc * lax.rsqrt(var + LN_EPS) * g + b


def _params(sem, vmem=VMEM_LIMIT):
    return pltpu.CompilerParams(dimension_semantics=sem, vmem_limit_bytes=vmem)


def _split_store(u, up_ref, q_ref, kvc_ref, kvs_ref, kvw_ref, gate_ref):
    o1 = POOL_WIDTH
    o2 = o1 + NSA_WIDTH
    o3 = o2 + KV_WIDTH
    o4 = o3 + KV_WIDTH
    o5 = o4 + KV_WIDTH
    up_ref[...] = u[:, :o1]
    q_ref[...] = u[:, o1:o2]
    kvc_ref[...] = u[:, o2:o3]
    kvs_ref[...] = u[:, o3:o4]
    kvw_ref[...] = u[:, o4:o5]
    gate_ref[...] = 1.0 / (1.0 + jnp.exp(-u[:, o5:]))


def _inproj_prompt_body(x_ref, w_ref, pw_ref, ps_ref,
                        up_ref, q_ref, kvc_ref, kvs_ref, kvw_ref, gate_ref, pool_ref,
                        ext_ref, *, tm, tiles_per_seq):
    halo = POOL_STATE + 1
    t_in_seq = pl.program_id(0) % tiles_per_seq
    u = _dot(x_ref[...].astype(BF16), w_ref[...])
    _split_store(u, up_ref, q_ref, kvc_ref, kvs_ref, kvw_ref, gate_ref)

    @pl.when(t_in_seq == 0)
    def _():
        ext_ref[0:halo, :] = jnp.zeros((halo, POOL_WIDTH), F32)

    ext_ref[halo:halo + tm, :] = u[:, :POOL_WIDTH]
    pos = t_in_seq * tm + lax.broadcasted_iota(I32, (tm, 1), 0)
    for gi, w in enumerate(POOL_WINDOWS):
        cols = slice(gi * POOL_GROUP, (gi + 1) * POOL_GROUP)
        acc = ext_ref[halo:halo + tm, cols]
        for k in range(1, w):
            acc = acc + ext_ref[halo - k:halo - k + tm, cols]
        cnt = jnp.minimum(pos + 1, w).astype(F32)
        d = acc / cnt - ext_ref[halo:halo + tm, cols]
        o = _dot(d.astype(BF16), pw_ref[gi])
        pool_ref[:, cols] = o * ps_ref[:, cols]
    ext_ref[0:halo, :] = ext_ref[tm:tm + halo, :]


def _inproj_prompt(x2d, w_in_bf, pool_w_bf, pool_scale, seq_len):
    T = x2d.shape[0]
    tm = ROW_TILE
    outs = [POOL_WIDTH, NSA_WIDTH, KV_WIDTH, KV_WIDTH, KV_WIDTH, 3 * NSA_HEADS, POOL_WIDTH]
    row = lambda n: pl.BlockSpec((tm, n), lambda i: (i, 0))
    full = lambda a: pl.BlockSpec(a.shape, lambda i: (0,) * a.ndim)
    return pl.pallas_call(
        functools.partial(_inproj_prompt_body, tm=tm, tiles_per_seq=seq_len // tm),
        grid=(T // tm,),
        in_specs=[row(D_MODEL), full(w_in_bf), full(pool_w_bf), full(pool_scale)],
        out_specs=[row(n) for n in outs],
        out_shape=[jax.ShapeDtypeStruct((T, n), F32) for n in outs],
        scratch_shapes=[pltpu.VMEM((tm + POOL_STATE + 1, POOL_WIDTH), F32)],
        compiler_params=_params(("arbitrary",)),
        name="inproj_prompt",
    )(x2d, w_in_bf, pool_w_bf, pool_scale)


def _inproj_sample_body(x_ref, w_ref, pw_ref, ps_ref, st_ref,
                        up_ref, q_ref, kvc_ref, kvs_ref, kvw_ref, gate_ref, pool_ref,
                        ext_ref, *, nb, ls, pos0):
    halo = POOL_STATE + 1
    tm = nb * ls
    u = _dot(x_ref[...].astype(BF16), w_ref[...])
    _split_store(u, up_ref, q_ref, kvc_ref, kvs_ref, kvw_ref, gate_ref)
    ext_ref[:, 0:halo, :] = st_ref[...]
    ext_ref[:, halo:halo + ls, :] = u[:, :POOL_WIDTH].reshape(nb, ls, POOL_WIDTH)
    pos = pos0 + lax.broadcasted_iota(I32, (1, ls, 1), 1)
    for gi, w in enumerate(POOL_WINDOWS):
        cols = slice(gi * POOL_GROUP, (gi + 1) * POOL_GROUP)
        acc = ext_ref[:, halo:halo + ls, cols]
        for k in range(1, w):
            acc = acc + ext_ref[:, halo - k:halo - k + ls, cols]
        cnt = jnp.minimum(pos + 1, w).astype(F32)
        d = acc / cnt - ext_ref[:, halo:halo + ls, cols]
        o = _dot(d.reshape(tm, POOL_GROUP).astype(BF16), pw_ref[gi])
        pool_ref[:, cols] = o * ps_ref[:, cols]


def _inproj_sample(x2d, w_in_bf, pool_w_bf, pool_scale, state_pad, ls, pos0):
    T = x2d.shape[0]
    nb = ROW_TILE // ls
    tm = nb * ls
    outs = [POOL_WIDTH, NSA_WIDTH, KV_WIDTH, KV_WIDTH, KV_WIDTH, 3 * NSA_HEADS, POOL_WIDTH]
    row = lambda n: pl.BlockSpec((tm, n), lambda i: (i, 0))
    full = lambda a: pl.BlockSpec(a.shape, lambda i: (0,) * a.ndim)
    return pl.pallas_call(
        functools.partial(_inproj_sample_body, nb=nb, ls=ls, pos0=pos0),
        grid=(T // tm,),
        in_specs=[row(D_MODEL), full(w_in_bf), full(pool_w_bf), full(pool_scale),
                  pl.BlockSpec((nb, POOL_STATE + 1, POOL_WIDTH), lambda i: (i, 0, 0))],
        out_specs=[row(n) for n in outs],
        out_shape=[jax.ShapeDtypeStruct((T, n), F32) for n in outs],
        scratch_shapes=[pltpu.VMEM((nb, POOL_STATE + 1 + ls, POOL_WIDTH), F32)],
        compiler_params=_params(("arbitrary",)),
        name="inproj_sample",
    )(x2d, w_in_bf, pool_w_bf, pool_scale, state_pad)


def _matmul_body(x_ref, w_ref, o_ref):
    o_ref[...] = _dot(x_ref[...].astype(BF16), w_ref[...].astype(BF16))


def _matmul(x, w, tn=512):
    M, K = x.shape
    N = w.shape[1]
    return pl.pallas_call(
        _matmul_body,
        grid=(N // tn,),
        in_specs=[pl.BlockSpec((M, K), lambda j: (0, 0)), pl.BlockSpec((K, tn), lambda j: (0, j))],
        out_specs=pl.BlockSpec((M, tn), lambda j: (0, j)),
        out_shape=jax.ShapeDtypeStruct((M, N), F32),
        compiler_params=_params(("arbitrary",)),
        name="mem_kv_proj",
    )(x, w)


def _gelu_tanh(x):
    c = 0.7978845608028654
    return 0.5 * x * (1.0 + jnp.tanh(c * (x + 0.044715 * (x * x * x))))


def _compress(kv_refs, n_chunks, wa_ref, wb_ref, pe_ref, b1_ref, w2_ref, b2_ref):
    outs = []
    for c in range(2):
        acc_a = jnp.zeros((n_chunks, 2 * CMP_HIDDEN), F32)
        acc_b = jnp.zeros((n_chunks, 2 * CMP_HIDDEN), F32)
        for l in range(0, CMP_STRIDE, 2):
            x = jnp.concatenate([kv_refs[c][pl.ds(l, n_chunks, stride=CMP_STRIDE), :],
                                 kv_refs[c][pl.ds(l + 1, n_chunks, stride=CMP_STRIDE), :]], axis=1)
            xa = (x + pe_ref[c, l // 2:l // 2 + 1, :]).astype(BF16)
            xb = (x + pe_ref[c, (CMP_STRIDE + l) // 2:(CMP_STRIDE + l) // 2 + 1, :]).astype(BF16)
            acc_a = acc_a + _dot(xa, wa_ref[c, l // 2])
            acc_b = acc_b + _dot(xb, wb_ref[c, l // 2])
        hid = acc_a + pltpu.roll(acc_b, n_chunks - 1, 0) + b1_ref[c]
        outs.append(_dot(_gelu_tanh(hid).astype(BF16), w2_ref[c]) + b2_ref[c])
    return jnp.concatenate(outs, axis=1)


def _compress_prompt_body(kv_ref, wa_ref, wb_ref, pe_ref, b1_ref, w2_ref, b2_ref, o_ref, k_ref, v_ref,
                          *, n_chunks):
    k_ref[...] = kv_ref[0, :, :LANES]
    v_ref[...] = kv_ref[0, :, LANES:]
    o_ref[0] = _compress((k_ref, v_ref), n_chunks, wa_ref, wb_ref, pe_ref, b1_ref, w2_ref, b2_ref)


def _compress_prompt(kvc, cw):
    B, S, _ = kvc.shape
    n_chunks = S // CMP_STRIDE
    full = lambda a: pl.BlockSpec(a.shape, lambda b: (0,) * a.ndim)
    return pl.pallas_call(
        functools.partial(_compress_prompt_body, n_chunks=n_chunks),
        grid=(B,),
        in_specs=[pl.BlockSpec((1, S, KV_WIDTH), lambda b: (b, 0, 0))] + [full(a) for a in cw],
        out_specs=pl.BlockSpec((1, n_chunks, KV_WIDTH), lambda b: (b, 0, 0)),
        out_shape=jax.ShapeDtypeStruct((B, n_chunks, KV_WIDTH), F32),
        scratch_shapes=[pltpu.VMEM((S, LANES), F32)] * 2,
        compiler_params=_params(("arbitrary",)),
        name="compress_prompt",
    )(kvc, *cw)


def _compress_weights(cmp_pe, cmp_w1, cmp_b1, cmp_w2, cmp_b2):
    z1 = jnp.zeros((2, CMP_LEN, DH, CMP_HIDDEN), F32)
    w1 = jnp.concatenate([jnp.concatenate([cmp_w1, z1], axis=-1),
                          jnp.concatenate([z1, cmp_w1], axis=-1)], axis=2)
    w1 = w1.reshape(2, CMP_LEN // 2, 2 * LANES, 2 * CMP_HIDDEN).astype(BF16)
    wa = w1[:, :CMP_STRIDE // 2]
    wb = w1[:, CMP_STRIDE // 2:]
    pe = jnp.concatenate([cmp_pe, cmp_pe], axis=-1).reshape(2, CMP_LEN // 2, 2 * LANES)
    b1 = jnp.concatenate([cmp_b1, cmp_b1], axis=-1)[:, None, :]
    z2 = jnp.zeros((2, CMP_HIDDEN, DH), F32)
    w2 = jnp.concatenate([jnp.concatenate([cmp_w2, z2], axis=-1),
                          jnp.concatenate([z2, cmp_w2], axis=-1)], axis=1).astype(BF16)
    b2 = jnp.concatenate([cmp_b2, cmp_b2], axis=-1)[:, None, :]
    return wa, wb, pe, b1, w2, b2


def _softmax_rows(s):
    e = jnp.exp(s - jnp.max(s, axis=-1, keepdims=True))
    return e * (1.0 / jnp.sum(e, axis=-1, keepdims=True))


def _lane_iota(n):
    return lax.broadcasted_iota(I32, (n, LANES), 1)


def _key_alibi_cols(pos, lane):
    hi = jnp.left_shift(jnp.right_shift(pos, 6), 6).astype(F32)
    lo = jnp.bitwise_and(pos, SEL_LEN - 1).astype(F32)
    return jnp.where(lane == DH, hi,
                     jnp.where(lane == DH + 1, lo,
                               jnp.where((lane == DH + 2) | (lane == DH + 3), 1.0, 0.0)))


def _key_alibi_rows(pos, sub):
    hi = jnp.left_shift(jnp.right_shift(pos, 6), 6).astype(F32)
    lo = jnp.bitwise_and(pos, SEL_LEN - 1).astype(F32)
    return jnp.where(sub == 0, hi, jnp.where(sub == 1, lo, jnp.where((sub == 2) | (sub == 3), 1.0, 0.0)))


def _halves(x, zero_hi=False):
    lo = _lane_iota(x.shape[0]) < DH
    r = pltpu.roll(x, DH, 1)
    if zero_hi:
        return jnp.where(lo, x, 0.0), jnp.where(lo, r, 0.0)
    return jnp.where(lo, x, r), jnp.where(lo, r, x)


def _fill_queries(qa_ref, qs, q_pos, lq):
    lane = _lane_iota(lq)
    lo_half = lane < DH
    q_hi = jnp.left_shift(jnp.right_shift(q_pos, 7), 7).astype(F32)
    q_lo = jnp.bitwise_and(q_pos, LANES - 1).astype(F32)
    for h in range(NSA_HEADS):
        g, hl = divmod(h, HPG)
        slope = 2.0 ** (-(h + 1))
        slab = qs[:, (h // 2) * LANES:(h // 2 + 1) * LANES]
        if h % 2:
            slab = pltpu.roll(slab, DH, 1)
        ex = jnp.where((lane == DH) | (lane == DH + 1), slope,
                       jnp.where(lane == DH + 2, -slope * q_hi,
                                 jnp.where(lane == DH + 3, -slope * q_lo,
                                           jnp.where(lane == DH + 4, NEG, 0.0))))
        qa_ref[g, hl * lq:(hl + 1) * lq, 0:LANES] = jnp.where(lo_half, slab, ex).astype(BF16)


def _stack4(x):
    return jnp.concatenate([x] * HPG, axis=0)


def _cmp_branch(qa_ref, kc_k, kc_v, q_pos4, lq, n_cmp):
    c_end = lax.broadcasted_iota(I32, (1, n_cmp), 1) * CMP_STRIDE + (CMP_LEN - 1)
    m_c = c_end <= q_pos4
    any_c = (q_pos4 >= CMP_LEN - 1).astype(F32)
    outs, psums = [], []
    for g in range(KVH):
        s = jnp.where(m_c, _dot_nt(qa_ref[g, :, 0:LANES], kc_k[g][...]), NEG)
        p = _softmax_rows(s) * any_c
        outs.append(_dot(p.astype(BF16), kc_v[g][...]))
        psums.append(p[0:lq] + p[lq:2 * lq] + p[2 * lq:3 * lq] + p[3 * lq:4 * lq])
    return outs, psums


def _select_blocks(psums, ovt_ref, pos0, lq):
    blk = lax.broadcasted_iota(I32, (SEL_PAD, LANES), 0)
    qp_t = pos0 + lax.broadcasted_iota(I32, (SEL_PAD, LANES), 1)
    cur = jnp.right_shift(qp_t, 6)
    forced = (blk == 0) | (blk == cur) | (blk == cur - 1)
    valid = jnp.left_shift(blk, 6) <= qp_t
    vs = []
    for g in range(KVH):
        ps = psums[g]
        if lq < LANES:
            ps = jnp.concatenate([ps, jnp.zeros((LANES - lq, ps.shape[1]), F32)], axis=0)
        imp_t = _dot_nt(ovt_ref[...], ps, precision=HIGHEST)
        vs.append(jnp.where(valid, jnp.where(forced, FORCED_SCORE, imp_t), -1.0))
    v = jnp.concatenate(vs, axis=1)
    blk_f = lax.broadcasted_iota(I32, (SEL_PAD, KVH * LANES), 0).astype(F32)
    sel = jnp.zeros((SEL_PAD, KVH * LANES), F32)
    for _ in range(SEL_TOP):
        m = jnp.max(v, axis=0, keepdims=True)
        idx = jnp.min(jnp.where(v == m, blk_f, float(SEL_PAD)), axis=0, keepdims=True)
        hit = blk_f == idx
        sel = jnp.where(hit, 1.0, sel)
        v = jnp.where(hit, -jnp.inf, v)
    out = []
    for g in range(KVH):
        keep = (sel[:, g * LANES:(g + 1) * LANES] > 0.5) & valid
        out.append(jnp.where(keep, 0.0, NEG).T[:lq].astype(BF16))
    return out


def _store_selbias(qa_ref, selbias, lq):
    for g in range(KVH):
        for hl in range(HPG):
            qa_ref[g, hl * lq:(hl + 1) * lq, LANES:2 * LANES] = selbias[g]


def _combine(gates, eg_ref, o_c, o_s, o_w, lq):
    lo_half = _lane_iota(lq) < DH

    def assemble(per_group):
        slabs = []
        for k in range(NSA_HEADS // 2):
            g, hl = divmod(2 * k, HPG)
            a = per_group[g][hl * lq:(hl + 1) * lq]
            b = per_group[g][(hl + 1) * lq:(hl + 2) * lq]
            slabs.append(jnp.where(lo_half, a, b))
        return jnp.concatenate(slabs, axis=1)

    return (_dot(gates, eg_ref[0]) * assemble(o_c)
            + _dot(gates, eg_ref[1]) * assemble(o_s)
            + _dot(gates, eg_ref[2]) * assemble(o_w))


def _gate_expand():
    r = jnp.arange(3 * NSA_HEADS)
    c = jnp.arange(NSA_WIDTH)
    return jnp.stack([(r[:, None] == 3 * (c[None, :] // DH) + br).astype(F32) for br in range(3)])


def _overlap_t(n_cmp):
    n = jnp.arange(n_cmp)
    s = jnp.arange(SEL_PAD)
    c_first = n * CMP_STRIDE
    c_end = c_first + CMP_LEN - 1
    b_first = s * SEL_LEN
    return ((c_first[None, :] < b_first[:, None] + SEL_LEN) & (c_end[None, :] >= b_first[:, None])).astype(F32)


def _key_rows(kv_f32, pos, invalid=None):
    n = kv_f32.shape[0]
    lane = _lane_iota(n)
    ex = _key_alibi_cols(pos, lane)
    if invalid is not None:
        ex = jnp.where((lane == DH + 4) & invalid, 1.0, ex)
    lo = lane < DH
    return jnp.where(lo, kv_f32, ex), jnp.where(lo, pltpu.roll(kv_f32, DH, 1), ex)


def _block_onehot(pos, n):
    return (lax.broadcasted_iota(I32, (n, SEL_PAD), 1) == jnp.right_shift(pos, 6)).astype(BF16)


def _nsa_prompt_body(q_ref, gate_ref, kc_ref, kvs_ref, w0_ref, w1_ref, w2_ref, w3_ref, w4_ref,
                     ovt_ref, eg_ref, wband_ref, o_ref,
                     kck0, kck1, kcv0, kcv1, ka0, ka1, vs0, vs1, qa_ref, *, seq_len):
    j = pl.program_id(1)
    kc_k, kc_v, kaug, vsel = (kck0, kck1), (kcv0, kcv1), (ka0, ka1), (vs0, vs1)
    n_cmp = kc_ref.shape[1]
    lq = Q_BLOCK
    rows = HPG * lq

    @pl.when(j == 0)
    def _():
        ck0, ck1 = _halves(kc_ref[0, :, :LANES], zero_hi=True)
        cv0, cv1 = _halves(kc_ref[0, :, LANES:])
        kck0[...] = ck0.astype(BF16)
        kck1[...] = ck1.astype(BF16)
        kcv0[...] = cv0.astype(BF16)
        kcv1[...] = cv1.astype(BF16)

        def build(i, _):
            r0 = pl.multiple_of(i * KEY_TILE, KEY_TILE)
            pos = r0 + lax.broadcasted_iota(I32, (KEY_TILE, 1), 0)
            k0, k1 = _key_rows(kvs_ref[0, pl.ds(r0, KEY_TILE), :LANES], pos)
            v0, v1 = _halves(kvs_ref[0, pl.ds(r0, KEY_TILE), LANES:])
            onehot = _block_onehot(pos, KEY_TILE)
            ka0[pl.ds(r0, KEY_TILE), :] = jnp.concatenate([k0.astype(BF16), onehot], axis=1)
            ka1[pl.ds(r0, KEY_TILE), :] = jnp.concatenate([k1.astype(BF16), onehot], axis=1)
            vs0[pl.ds(r0, KEY_TILE), :] = v0.astype(BF16)
            vs1[pl.ds(r0, KEY_TILE), :] = v1.astype(BF16)
            return 0

        lax.fori_loop(0, seq_len // KEY_TILE, build, 0)

    st = j * Q_BLOCK
    q_pos = st + lax.broadcasted_iota(I32, (lq, 1), 0)
    q_pos4 = _stack4(q_pos)
    _fill_queries(qa_ref, q_ref[...] * ATTN_SCALE, q_pos, lq)

    o_c, psums = _cmp_branch(qa_ref, kc_k, kc_v, q_pos4, lq, n_cmp)
    _store_selbias(qa_ref, _select_blocks(psums, ovt_ref, st, lq), lq)

    n_tiles = (st + Q_BLOCK + KEY_TILE - 1) // KEY_TILE
    def tile(t, carry, masked):
        r0 = pl.multiple_of(t * KEY_TILE, KEY_TILE)
        out = []
        for g in range(KVH):
            m, l, acc = carry[g]
            s = _dot_nt(qa_ref[g], kaug[g][pl.ds(r0, KEY_TILE), :])
            if masked:
                k_pos = r0 + lax.broadcasted_iota(I32, (1, KEY_TILE), 1)
                s = jnp.where(q_pos4 >= k_pos, s, NEG)
            m_new = jnp.maximum(m, jnp.max(s, axis=-1, keepdims=True))
            a = jnp.exp(m - m_new)
            e = jnp.exp(s - m_new)
            l = a * l + jnp.sum(e, axis=-1, keepdims=True)
            acc = a * acc + _dot(e.astype(BF16), vsel[g][pl.ds(r0, KEY_TILE), :])
            out.append((m_new, l, acc))
        return tuple(out)

    init = tuple((jnp.full((rows, 1), NEG, F32), jnp.zeros((rows, 1), F32), jnp.zeros((rows, LANES), F32))
                 for _ in range(KVH))
    carry = lax.fori_loop(0, n_tiles - 1, functools.partial(tile, masked=False), init)
    carry = tile(n_tiles - 1, carry, True)
    o_s = [acc * (1.0 / l) for (_, l, acc) in carry]

    band = jnp.concatenate([w0_ref[0], w1_ref[0], w2_ref[0], w3_ref[0], w4_ref[0]], axis=0)
    n_win = band.shape[0]
    w_pos_col = st - WINDOW + lax.broadcasted_iota(I32, (n_win, 1), 0)
    kw_k = _key_rows(band[:, :LANES], jnp.maximum(w_pos_col, 0), invalid=w_pos_col < 0)
    kw_v = _halves(band[:, LANES:])
    band_bias = _stack4(wband_ref[...])
    o_w = []
    for g in range(KVH):
        s = _dot_nt(qa_ref[g, :, 0:LANES], kw_k[g].astype(BF16)) + band_bias
        e = jnp.exp(s - jnp.max(s, axis=-1, keepdims=True))
        o_w.append(_dot(e.astype(BF16), kw_v[g].astype(BF16)) * (1.0 / jnp.sum(e, axis=-1, keepdims=True)))

    o_ref[...] = _combine(gate_ref[...], eg_ref, o_c, o_s, o_w, lq)


def _nsa_prompt(q, gates, kc, kvs, kvw):
    B, S, _ = kvs.shape
    nqb = S // Q_BLOCK
    n_cmp = kc.shape[1]
    ovt = _overlap_t(n_cmp)
    eg = _gate_expand()
    n_band = WINDOW // Q_BLOCK + 1
    d_band = jnp.arange(Q_BLOCK)[:, None] + WINDOW - jnp.arange(n_band * Q_BLOCK)[None, :]
    wband = jnp.where((d_band >= 0) & (d_band < WINDOW), 0.0, NEG).astype(F32)

    def band_spec(i):
        return pl.BlockSpec((1, Q_BLOCK, KV_WIDTH),
                            lambda b, j, i=i: (b, jnp.maximum(j - (n_band - 1) + i, 0), 0))

    full = lambda a: pl.BlockSpec(a.shape, lambda b, j: (0,) * a.ndim)
    return pl.pallas_call(
        functools.partial(_nsa_prompt_body, seq_len=S),
        grid=(B, nqb),
        in_specs=[pl.BlockSpec((Q_BLOCK, NSA_WIDTH), lambda b, j: (b * nqb + j, 0)),
                  pl.BlockSpec((Q_BLOCK, 3 * NSA_HEADS), lambda b, j: (b * nqb + j, 0)),
                  pl.BlockSpec((1, n_cmp, KV_WIDTH), lambda b, j: (b, 0, 0)),
                  pl.BlockSpec((1, S, KV_WIDTH), lambda b, j: (b, 0, 0))]
                 + [band_spec(i) for i in range(n_band)] + [full(ovt), full(eg), full(wband)],
        out_specs=pl.BlockSpec((Q_BLOCK, NSA_WIDTH), lambda b, j: (b * nqb + j, 0)),
        out_shape=jax.ShapeDtypeStruct((B * S, NSA_WIDTH), F32),
        scratch_shapes=[pltpu.VMEM((n_cmp, LANES), BF16)] * 4
                       + [pltpu.VMEM((S, 2 * LANES), BF16)] * 2
                       + [pltpu.VMEM((S, LANES), BF16)] * 2
                       + [pltpu.VMEM((KVH, HPG * Q_BLOCK, 2 * LANES), BF16)],
        compiler_params=_params(("arbitrary", "arbitrary")),
        name="nsa_prompt",
    )(q, gates, kc, kvs, *([kvw] * n_band), ovt, eg, wband)


def _nsa_sample_body(pt_ref, q_ref, gate_ref, kvs_new_ref, win_ref, kvw_new_ref, *rest,
                     n_pages, ls, past_len):
    cmp_pages = rest[:n_pages]
    slc_pages = rest[n_pages:2 * n_pages]
    (wa_ref, wb_ref, pe_ref, b1_ref, w2_ref, b2_ref, ovt_ref, eg_ref, o_ref,
     full_k, full_v, kck0, kck1, kcv0, kcv1, kt0, kt1, vt0, vt1, wkt0, wkt1, qa_ref) = rest[2 * n_pages:]
    del pt_ref
    kc_k, kc_v, kaug_t, v_t, wk_t = (kck0, kck1), (kcv0, kcv1), (kt0, kt1), (vt0, vt1), (wkt0, wkt1)
    n_cmp = past_len // CMP_STRIDE
    w_rows = win_ref.shape[4]
    lq = BF16_ROWS
    rows = HPG * lq
    w_start = past_len - w_rows

    @pl.when(pl.program_id(0) == 0)
    def _():
        sub = lax.broadcasted_iota(I32, (DH, past_len), 0)
        pos = lax.broadcasted_iota(I32, (1, past_len), 1)
        ex = _key_alibi_rows(pos, sub).astype(BF16)
        onehot = (lax.broadcasted_iota(I32, (SEL_PAD, past_len), 0) == jnp.right_shift(pos, 6)).astype(BF16)
        subw = lax.broadcasted_iota(I32, (DH, w_rows), 0)
        exw = _key_alibi_rows(w_start + lax.broadcasted_iota(I32, (1, w_rows), 1), subw).astype(BF16)
        for g in range(KVH):
            kaug_t[g][DH:2 * DH, :] = ex
            kaug_t[g][2 * DH:, :] = onehot
            wk_t[g][DH:, :] = exw

    for p in range(n_pages):
        cols = slice(p * PAGE_SIZE, (p + 1) * PAGE_SIZE)
        full_k[cols, :] = cmp_pages[p][0, 0].reshape(2 * DH, PAGE_SIZE).T
        full_v[cols, :] = cmp_pages[p][0, 1].reshape(2 * DH, PAGE_SIZE).T
        for g in range(KVH):
            kaug_t[g][0:DH, cols] = slc_pages[p][0, 0, g].astype(BF16)
            vt = slc_pages[p][0, 1, g].astype(BF16)
            v_t[g][0:DH, cols] = vt
            v_t[g][DH:, cols] = vt
    for g in range(KVH):
        wk_t[g][0:DH, :] = win_ref[0, 0, g].astype(BF16)

    kc = _compress((full_k, full_v), n_cmp, wa_ref, wb_ref, pe_ref, b1_ref, w2_ref, b2_ref)
    ck0, ck1 = _halves(kc[:, :LANES], zero_hi=True)
    cv0, cv1 = _halves(kc[:, LANES:])
    kck0[...] = ck0.astype(BF16)
    kck1[...] = ck1.astype(BF16)
    kcv0[...] = cv0.astype(BF16)
    kcv1[...] = cv1.astype(BF16)

    pad_q = jnp.zeros((lq - ls, NSA_WIDTH), F32)
    q_pos = past_len + lax.broadcasted_iota(I32, (lq, 1), 0)
    q_pos4 = _stack4(q_pos)
    _fill_queries(qa_ref, jnp.concatenate([q_ref[0] * ATTN_SCALE, pad_q], axis=0), q_pos, lq)
    gates = jnp.concatenate([gate_ref[0], jnp.zeros((lq - ls, 3 * NSA_HEADS), F32)], axis=0)

    o_c, psums = _cmp_branch(qa_ref, kc_k, kc_v, q_pos4, lq, n_cmp)
    _store_selbias(qa_ref, _select_blocks(psums, ovt_ref, past_len, lq), lq)

    pad_k = jnp.zeros((LANES - ls, KV_WIDTH), F32)
    new_pos_col = past_len + lax.broadcasted_iota(I32, (LANES, 1), 0)
    new_pos = past_len + lax.broadcasted_iota(I32, (1, LANES), 1)
    new_s = jnp.concatenate([kvs_new_ref[0], pad_k], axis=0)
    new_w = jnp.concatenate([kvw_new_ref[0], pad_k], axis=0)
    ks_new = _key_rows(new_s[:, :LANES], new_pos_col)
    vs_new = _halves(new_s[:, LANES:])
    kw_new = _key_rows(new_w[:, :LANES], new_pos_col)
    vw_new = _halves(new_w[:, LANES:])
    onehot_new = _block_onehot(new_pos_col, LANES)
    causal_new = q_pos4 >= new_pos

    o_s, o_w = [], []
    d_past = q_pos4 - (w_start + lax.broadcasted_iota(I32, (1, w_rows), 1))
    m_past = (d_past >= 0) & (d_past < WINDOW)
    d_new = q_pos4 - new_pos
    m_new = (d_new >= 0) & (d_new < WINDOW)
    for g in range(KVH):
        s_past = _dot(qa_ref[g], kaug_t[g][...])
        k_new = jnp.concatenate([ks_new[g].astype(BF16), onehot_new], axis=1)
        s_new = jnp.where(causal_new, _dot_nt(qa_ref[g], k_new), NEG)
        m = jnp.maximum(jnp.max(s_past, axis=-1, keepdims=True), jnp.max(s_new, axis=-1, keepdims=True))
        e_past = jnp.exp(s_past - m)
        e_new = jnp.exp(s_new - m)
        den = jnp.sum(e_past, axis=-1, keepdims=True) + jnp.sum(e_new, axis=-1, keepdims=True)
        acc = _dot_nt(e_past.astype(BF16), v_t[g][...]) + _dot(e_new.astype(BF16), vs_new[g].astype(BF16))
        o_s.append(acc / den)

        sw_past = jnp.where(m_past, _dot(qa_ref[g, :, 0:LANES], wk_t[g][...]), NEG)
        sw_new = jnp.where(m_new, _dot_nt(qa_ref[g, :, 0:LANES], kw_new[g].astype(BF16)), NEG)
        m = jnp.maximum(jnp.max(sw_past, axis=-1, keepdims=True), jnp.max(sw_new, axis=-1, keepdims=True))
        e_past = jnp.exp(sw_past - m)
        e_new = jnp.exp(sw_new - m)
        den = jnp.sum(e_past, axis=-1, keepdims=True) + jnp.sum(e_new, axis=-1, keepdims=True)
        vw = win_ref[0, 1, g].astype(BF16)
        vw2 = jnp.concatenate([vw, vw], axis=0)
        acc = _dot_nt(e_past.astype(BF16), vw2) + _dot(e_new.astype(BF16), vw_new[g].astype(BF16))
        o_w.append(acc / den)

    o_ref[0] = _combine(gates, eg_ref, o_c, o_s, o_w, lq)[:ls]


def _nsa_sample(q, gates, kvs_new, kvw_new, win_t, cmp_t, slc_t, page_table, cw, past_len):
    Bd, ls, _ = q.shape
    n_pages = page_table.shape[1]
    n_cmp = past_len // CMP_STRIDE
    ovt = _overlap_t(n_cmp)
    eg = _gate_expand()
    w_rows = win_t.shape[4]

    per_b = lambda a: pl.BlockSpec((1,) + a.shape[1:], lambda b, pt: (b,) + (0,) * (a.ndim - 1))
    full = lambda a: pl.BlockSpec(a.shape, lambda b, pt: (0,) * a.ndim)
    page = lambda p: pl.BlockSpec((1, 2, KVH, DH, PAGE_SIZE), lambda b, pt, p=p: (pt[b, p], 0, 0, 0, 0))
    grid_spec = pltpu.PrefetchScalarGridSpec(
        num_scalar_prefetch=1,
        grid=(Bd,),
        in_specs=[per_b(q), per_b(gates), per_b(kvs_new), per_b(win_t), per_b(kvw_new)]
                 + [page(p) for p in range(n_pages)] * 2
                 + [full(a) for a in cw] + [full(ovt), full(eg)],
        out_specs=pl.BlockSpec((1, ls, NSA_WIDTH), lambda b, pt: (b, 0, 0)),
        scratch_shapes=[pltpu.VMEM((past_len, LANES), F32)] * 2
                       + [pltpu.VMEM((n_cmp, LANES), BF16)] * 4
                       + [pltpu.VMEM((2 * LANES, past_len), BF16)] * 2
                       + [pltpu.VMEM((LANES, past_len), BF16)] * 2
                       + [pltpu.VMEM((LANES, w_rows), BF16)] * 2
                       + [pltpu.VMEM((KVH, HPG * BF16_ROWS, 2 * LANES), BF16)],
    )
    return pl.pallas_call(
        functools.partial(_nsa_sample_body, n_pages=n_pages, ls=ls, past_len=past_len),
        grid_spec=grid_spec,
        out_shape=jax.ShapeDtypeStruct((Bd, ls, NSA_WIDTH), F32),
        compiler_params=_params(("arbitrary",)),
        name="nsa_sample",
    )(page_table, q, gates, kvs_new, win_t, kvw_new,
      *([cmp_t] * n_pages), *([slc_t] * n_pages), *cw, ovt, eg)


def _fin1_body(h_ref, pool_ref, nsa_ref, wo_ref, g_ref, b_ref, wq_ref, h1_ref, qm_ref):
    mix = (_dot(pool_ref[...].astype(BF16), wo_ref[0:POOL_WIDTH, :])
           + _dot(nsa_ref[...].astype(BF16), wo_ref[POOL_WIDTH:, :]))
    h1 = _layer_norm(DN_ALPHA * h_ref[...] + mix, g_ref[...], b_ref[...])
    h1_ref[...] = h1
    qm_ref[...] = _dot(h1.astype(BF16), wq_ref[...])


def _fin1(h, pool_o, nsa_o, w_out_bf, g, b, wq_bf):
    T = h.shape[0]
    tm = ROW_TILE
    row = lambda n: pl.BlockSpec((tm, n), lambda i: (i, 0))
    full = lambda a: pl.BlockSpec(a.shape, lambda i: (0,) * a.ndim)
    return pl.pallas_call(
        _fin1_body,
        grid=(T // tm,),
        in_specs=[row(D_MODEL), row(POOL_WIDTH), row(NSA_WIDTH), full(w_out_bf), full(g), full(b), full(wq_bf)],
        out_specs=[row(D_MODEL), row(D_MODEL)],
        out_shape=[jax.ShapeDtypeStruct((T, D_MODEL), F32)] * 2,
        compiler_params=_params(("arbitrary",)),
        name="out_proj_ln1",
    )(h, pool_o, nsa_o, w_out_bf, g, b, wq_bf)


def _memattn_body(q_ref, kv_ref, o_ref):
    width = MEM_HEADS * MEM_HEAD_DIM
    for h in range(MEM_HEADS):
        cols = slice(h * MEM_HEAD_DIM, (h + 1) * MEM_HEAD_DIM)
        qh = (q_ref[0, :, cols] * (MEM_HEAD_DIM ** -0.5)).astype(BF16)
        kh = kv_ref[0, :, cols].astype(BF16)
        vh = kv_ref[0, :, width + h * MEM_HEAD_DIM:width + (h + 1) * MEM_HEAD_DIM].astype(BF16)
        s = _dot_nt(qh, kh)
        e = jnp.exp(s - jnp.max(s, axis=-1, keepdims=True))
        o_ref[0, :, cols] = _dot(e.astype(BF16), vh) * (1.0 / jnp.sum(e, axis=-1, keepdims=True))


def _memattn(qm, mem_kv, tq):
    nb, L, W = qm.shape
    return pl.pallas_call(
        _memattn_body,
        grid=(nb, L // tq),
        in_specs=[pl.BlockSpec((1, tq, W), lambda b, t: (b, t, 0)),
                  pl.BlockSpec((1, MEM_LEN, 2 * W), lambda b, t: (b, 0, 0))],
        out_specs=pl.BlockSpec((1, tq, W), lambda b, t: (b, t, 0)),
        out_shape=jax.ShapeDtypeStruct((nb, L, W), F32),
        compiler_params=_params(("arbitrary", "arbitrary")),
        name="mem_attn",
    )(qm, mem_kv)


def _memattn_few_body(q_ref, kv_ref, o_ref):
    lq = q_ref.shape[1]
    n_keys = MEM_LEN * MEM_HEADS
    q = q_ref[0] * (MEM_HEAD_DIM ** -0.5)
    qs = jnp.concatenate([q[:, h * MEM_HEAD_DIM:(h + 1) * MEM_HEAD_DIM] for h in range(MEM_HEADS)], axis=0)
    k = kv_ref[0, :, 0, :, :].reshape(n_keys, MEM_HEAD_DIM).astype(BF16)
    v = kv_ref[0, :, 1, :, :].reshape(n_keys, MEM_HEAD_DIM).astype(BF16)
    s = _dot_nt(qs.astype(BF16), k)
    assert lq & (lq - 1) == 0 and MEM_HEADS & (MEM_HEADS - 1) == 0
    col_h = jnp.bitwise_and(lax.broadcasted_iota(I32, s.shape, 1), MEM_HEADS - 1)
    row_h = jnp.right_shift(lax.broadcasted_iota(I32, s.shape, 0), lq.bit_length() - 1)
    s = jnp.where(col_h == row_h, s, NEG)
    e = jnp.exp(s - jnp.max(s, axis=-1, keepdims=True))
    o = _dot(e.astype(BF16), v) * (1.0 / jnp.sum(e, axis=-1, keepdims=True))
    for h in range(MEM_HEADS):
        o_ref[0, :, h * MEM_HEAD_DIM:(h + 1) * MEM_HEAD_DIM] = o[h * lq:(h + 1) * lq]


def _memattn_few(qm, mem_kv):
    nb, lq, W = qm.shape
    return pl.pallas_call(
        _memattn_few_body,
        grid=(nb,),
        in_specs=[pl.BlockSpec((1, lq, W), lambda b: (b, 0, 0)),
                  pl.BlockSpec((1, MEM_LEN, 2, MEM_HEADS, MEM_HEAD_DIM), lambda b: (b, 0, 0, 0, 0))],
        out_specs=pl.BlockSpec((1, lq, W), lambda b: (b, 0, 0)),
        out_shape=jax.ShapeDtypeStruct((nb, lq, W), F32),
        compiler_params=_params(("arbitrary",)),
        name="mem_attn_few",
    )(qm, mem_kv)


def _fin2_body(cnt0_ref, h1_ref, o_ref, wo_ref, g_ref, b_ref, rw_ref, rb_ref,
               h2_ref, te_ref, tg_ref, cnt_ref, run_ref):
    tm = h1_ref.shape[0]

    @pl.when(pl.program_id(0) == 0)
    def _():
        run_ref[...] = cnt0_ref[...]

    a = _dot(o_ref[...].astype(BF16), wo_ref[...])
    h2 = _layer_norm(DN_ALPHA * h1_ref[...] + a, g_ref[...], b_ref[...])
    h2_ref[...] = h2
    logits = jnp.dot(h2, rw_ref[...], preferred_element_type=F32, precision=HIGHEST) + rb_ref[...]
    e_iota = lax.broadcasted_iota(I32, (tm, N_EXPERTS), 1).astype(F32)
    lane = lax.broadcasted_iota(I32, (tm, LANES), 1)
    te = jnp.zeros((tm, LANES), F32)
    tv = jnp.full((tm, LANES), NEG, F32)
    work = logits
    chosen = []
    for k in range(TOP_K):
        m = jnp.max(work, axis=-1, keepdims=True)
        idx = jnp.min(jnp.where(work == m, e_iota, float(N_EXPERTS)), axis=-1, keepdims=True)
        hit = e_iota == idx
        chosen.append(hit)
        te = jnp.where(lane == k, idx, te)
        tv = jnp.where(lane == k, m, tv)
        work = jnp.where(hit, -jnp.inf, work)
    member = sum(c.astype(F32) for c in chosen)
    earlier = (lax.broadcasted_iota(I32, (tm, tm), 0) > lax.broadcasted_iota(I32, (tm, tm), 1)).astype(BF16)
    before = _dot(earlier, member.astype(BF16)) + run_ref[...]
    for k in range(TOP_K):
        rank = jnp.sum(jnp.where(chosen[k], before, 0.0), axis=-1, keepdims=True)
        te = jnp.where(lane == TOP_K + k, rank, te)
    run_ref[...] = run_ref[...] + jnp.sum(member, axis=0, keepdims=True)
    cnt_ref[...] = run_ref[...]
    ex = jnp.exp(tv - jnp.max(tv, axis=-1, keepdims=True))
    te_ref[...] = te.astype(I32)
    tg_ref[...] = ex / jnp.sum(ex, axis=-1, keepdims=True)


def _fin2(cnt0, h1, o, wo_bf, g, b, rw, rb):
    T = h1.shape[0]
    tm = ROW_TILE
    row = lambda n: pl.BlockSpec((tm, n), lambda i: (i, 0))
    full = lambda a: pl.BlockSpec(a.shape, lambda i: (0,) * a.ndim)
    return pl.pallas_call(
        _fin2_body,
        grid=(T // tm,),
        in_specs=[full(cnt0), row(D_MODEL), row(D_MODEL), full(wo_bf), full(g), full(b), full(rw), full(rb)],
        out_specs=[row(D_MODEL), row(LANES), row(LANES), full(cnt0)],
        out_shape=[jax.ShapeDtypeStruct((T, D_MODEL), F32), jax.ShapeDtypeStruct((T, LANES), I32),
                   jax.ShapeDtypeStruct((T, LANES), F32), jax.ShapeDtypeStruct(cnt0.shape, F32)],
        scratch_shapes=[pltpu.VMEM(cnt0.shape, F32)],
        compiler_params=_params(("arbitrary",)),
        name="mem_out_ln2_router",
    )(cnt0, h1, o, wo_bf, g, b, rw, rb)


def _moe_body(ut_ref, ue_ref, nu_ref, rs_ref, re_ref, x_ref, rw_ref, wgu_ref, bgu_ref, wdn_ref, bdn_ref,
              y_ref, wgu_bf, wdn_bf):
    u = pl.program_id(0)
    bk = x_ref.shape[0]
    e = ue_ref[u]
    tile = ut_ref[u]
    prev = jnp.maximum(u - 1, 0)

    @pl.when((u == 0) | (e != ue_ref[prev]))
    def _():
        wgu_bf[...] = wgu_ref[0].astype(BF16)
        wdn_bf[...] = wdn_ref[0].astype(BF16)

    @pl.when(u < nu_ref[0])
    def _():
        x = x_ref[...].astype(BF16)
        g = _dot(x, wgu_bf[:, :D_FF]) + bgu_ref[0, :, :D_FF]
        v = _dot(x, wgu_bf[:, D_FF:]) + bgu_ref[0, :, D_FF:]
        g = jnp.minimum(g, SWIGLU_LIMIT)
        v = jnp.clip(v, -SWIGLU_LIMIT, SWIGLU_LIMIT)
        a = g * (1.0 / (1.0 + jnp.exp(-SWIGLU_ALPHA * g))) * (v + 1.0)
        y = _dot(a.astype(BF16), wdn_bf[...]) + bdn_ref[0]
        row = tile * bk + lax.broadcasted_iota(I32, (bk, 1), 0)
        mine = (row >= rs_ref[e]) & (row < re_ref[e])
        y = jnp.where(mine, y * rw_ref[...], 0.0)

        @pl.when((u == 0) | (tile != ut_ref[prev]))
        def _():
            y_ref[...] = y

        @pl.when((u > 0) & (tile == ut_ref[prev]))
        def _():
            y_ref[...] = y_ref[...] + y


def _moe_gmm(x_rows, row_w, units, w_gu, b_gu, w_dn, b_dn):
    N = x_rows.shape[0]
    bk = MOE_ROWS
    unit_tile, unit_e, n_units, r_start, r_end = units
    grid_spec = pltpu.PrefetchScalarGridSpec(
        num_scalar_prefetch=5,
        grid=(unit_tile.shape[0],),
        in_specs=[pl.BlockSpec((bk, D_MODEL), lambda u, ut, ue, *_: (ut[u], 0)),
                  pl.BlockSpec((bk, 1), lambda u, ut, ue, *_: (ut[u], 0)),
                  pl.BlockSpec((1, D_MODEL, 2 * D_FF), lambda u, ut, ue, *_: (ue[u], 0, 0)),
                  pl.BlockSpec((1, 1, 2 * D_FF), lambda u, ut, ue, *_: (ue[u], 0, 0)),
                  pl.BlockSpec((1, D_FF, D_MODEL), lambda u, ut, ue, *_: (ue[u], 0, 0)),
                  pl.BlockSpec((1, 1, D_MODEL), lambda u, ut, ue, *_: (ue[u], 0, 0))],
        out_specs=pl.BlockSpec((bk, D_MODEL), lambda u, ut, ue, *_: (ut[u], 0)),
        scratch_shapes=[pltpu.VMEM((D_MODEL, 2 * D_FF), BF16), pltpu.VMEM((D_FF, D_MODEL), BF16)],
    )
    return pl.pallas_call(
        _moe_body,
        grid_spec=grid_spec,
        out_shape=jax.ShapeDtypeStruct((N, D_MODEL), F32),
        compiler_params=_params(("arbitrary",)),
        name="moe_experts",
    )(unit_tile, unit_e, n_units, r_start, r_end, x_rows, row_w, w_gu, b_gu, w_dn, b_dn)


FLAT_BITS = 17


def _moe_routing(te, tg, counts):
    bk = MOE_ROWS
    T = te.shape[0]
    N = T * TOP_K
    assert N % bk == 0 and N <= (1 << FLAT_BITS)
    experts = jnp.arange(N_EXPERTS, dtype=I32)
    top_e = te[:, :TOP_K]
    r_end = jnp.cumsum(counts).astype(I32)
    r_start = r_end - counts
    onehot = top_e[:, :, None] == experts[None, None, :]
    pos = jnp.sum(jnp.where(onehot, r_start[None, None, :], 0), axis=-1) + te[:, TOP_K:2 * TOP_K]
    key = jnp.left_shift(top_e.reshape(-1), FLAT_BITS) + jnp.arange(N, dtype=I32)
    key_s, gate_s = lax.sort((key, tg[:, :TOP_K].reshape(-1)), num_keys=1)
    tok_s = jnp.right_shift(jnp.bitwise_and(key_s, (1 << FLAT_BITS) - 1), 2)
    first = r_start // bk
    last = (r_end - 1) // bk
    n_e = jnp.where(counts > 0, last - first + 1, 0)
    u_end = jnp.cumsum(n_e).astype(I32)
    u_start = u_end - n_e
    n_units = u_end[-1]
    u = jnp.minimum(jnp.arange(N // bk + N_EXPERTS - 1, dtype=I32), n_units - 1)
    unit_e = jnp.sum((u[:, None] >= u_end[None, :]).astype(I32), axis=1)
    unit_tile = u + jnp.sum(jnp.where(unit_e[:, None] == experts[None, :], (first - u_start)[None, :], 0), axis=1)
    return pos, tok_s, gate_s, (unit_tile, unit_e, n_units.reshape(1), r_start, r_end)


def _fin3_body(h2_ref, y0_ref, y1_ref, y2_ref, y3_ref, g_ref, b_ref, o_ref):
    y = (y0_ref[...] + y1_ref[...]) + (y2_ref[...] + y3_ref[...])
    o_ref[...] = _layer_norm(DN_ALPHA * h2_ref[...] + y, g_ref[...], b_ref[...])


def _fin3(h2, ys, g, b):
    T = h2.shape[0]
    tm = ROW_TILE
    row = lambda n: pl.BlockSpec((tm, n), lambda i: (i, 0))
    full = lambda a: pl.BlockSpec(a.shape, lambda i: (0,) * a.ndim)
    return pl.pallas_call(
        _fin3_body,
        grid=(T // tm,),
        in_specs=[row(D_MODEL)] * (1 + TOP_K) + [full(g), full(b)],
        out_specs=row(D_MODEL),
        out_shape=jax.ShapeDtypeStruct((T, D_MODEL), F32),
        compiler_params=_params(("arbitrary",)),
        name="combine_ln3",
    )(h2, *ys, g, b)


def kernel(x_prompt, x_sample, cache_cmp_kv, cache_slc_kv, state_win_kv, state_pool, cache_mem_kv, page_table,
           mem_prompt, w_in, pool_w, pool_scale, cmp_pe, cmp_w1, cmp_b1, cmp_w2, cmp_b2, w_out, ln1_g, ln1_b,
           mem_wq, mem_wkv, mem_wo, ln2_g, ln2_b, router_w, router_b, exp_w_gu, exp_b_gu, exp_w_dn, exp_b_dn,
           ln3_g, ln3_b):
    Bp, S, D = x_prompt.shape
    Bd, Ls, _ = x_sample.shape
    Tp, Ts = Bp * S, Bd * Ls
    l = 0
    w_in_bf = w_in[l].astype(BF16)
    pool_w_bf = pool_w[l].astype(BF16)
    ps = pool_scale[l][None, :]
    cw = _compress_weights(cmp_pe[l], cmp_w1[l], cmp_b1[l], cmp_w2[l], cmp_b2[l])
    w_out_bf = w_out[l].astype(BF16)
    wq_bf = mem_wq[l].astype(BF16)
    wo_bf = mem_wo[l].astype(BF16)
    vec = lambda a: a[l][None, :]

    up, qp, kvc_p, kvs_p, kvw_p, gp, pool_p = _inproj_prompt(x_prompt.reshape(Tp, D), w_in_bf, pool_w_bf, ps, S)
    kc_p = _compress_prompt(kvc_p.reshape(Bp, S, KV_WIDTH), cw)
    nsa_p = _nsa_prompt(qp, gp, kc_p, kvs_p.reshape(Bp, S, KV_WIDTH), kvw_p.reshape(Bp, S, KV_WIDTH))
    mem_kv_p = _matmul(mem_prompt.reshape(Bp * MEM_LEN, D), mem_wkv[l]).reshape(Bp, MEM_LEN, 2 * D)
    h1_p, qm_p = _fin1(x_prompt.reshape(Tp, D), pool_p, nsa_p, w_out_bf, vec(ln1_g), vec(ln1_b), wq_bf)
    om_p = _memattn(qm_p.reshape(Bp, S, D), mem_kv_p, ROW_TILE).reshape(Tp, D)
    h2_p, te_p, tg_p, cnt_p = _fin2(jnp.zeros((1, N_EXPERTS), F32), h1_p, om_p, wo_bf, vec(ln2_g), vec(ln2_b),
                                    router_w[l], vec(router_b))

    state_pad = jnp.pad(state_pool[l], ((0, 0), (1, 0), (0, 0)))
    us, qs, kvc_s, kvs_s, kvw_s, gs, pool_s = _inproj_sample(
        x_sample.reshape(Ts, D), w_in_bf, pool_w_bf, ps, state_pad, Ls, PAST_LEN)
    w_rows = state_win_kv.shape[2]
    feature_major = lambda a: jnp.transpose(a, (0, 2, 3, 4, 1))
    nsa_s = _nsa_sample(qs.reshape(Bd, Ls, NSA_WIDTH), gs.reshape(Bd, Ls, 3 * NSA_HEADS),
                        kvs_s.reshape(Bd, Ls, KV_WIDTH), kvw_s.reshape(Bd, Ls, KV_WIDTH),
                        feature_major(state_win_kv[l]), feature_major(cache_cmp_kv[l]),
                        feature_major(cache_slc_kv[l]), page_table, cw, PAST_LEN)
    h1_s, qm_s = _fin1(x_sample.reshape(Ts, D), pool_s, nsa_s.reshape(Ts, NSA_WIDTH), w_out_bf,
                       vec(ln1_g), vec(ln1_b), wq_bf)
    om_s = _memattn_few(qm_s.reshape(Bd, Ls, D), cache_mem_kv[l]).reshape(Ts, D)
    h2_s, te_s, tg_s, cnt_s = _fin2(cnt_p, h1_s, om_s, wo_bf, vec(ln2_g), vec(ln2_b), router_w[l], vec(router_b))

    h2 = jnp.concatenate([h2_p, h2_s], axis=0)
    pos, tok_s, gate_s, units = _moe_routing(jnp.concatenate([te_p, te_s], axis=0),
                                             jnp.concatenate([tg_p, tg_s], axis=0), cnt_s[0].astype(I32))
    y_rows = _moe_gmm(h2[tok_s], gate_s[:, None], units, exp_w_gu[l], exp_b_gu[l][:, None, :],
                      exp_w_dn[l], exp_b_dn[l][:, None, :])
    out = _fin3(h2, [y_rows[pos[:, k]] for k in range(TOP_K)], vec(ln3_g), vec(ln3_b))
    y_prompt = out[:Tp].reshape(Bp, S, D)
    y_sample = out[Tp:].reshape(Bd, Ls, D)

    kv6 = lambda a, b, n: a.reshape(1, b, n, 2, KVH, DH)
    win_p = kvw_p.reshape(Bp, S, KV_WIDTH)[:, S - min(WINDOW, S):]
    win_s = jnp.concatenate([state_win_kv[l], kvw_s.reshape(Bd, Ls, 2, KVH, DH)], axis=1)[:, -w_rows:]
    pool_state_p = up.reshape(Bp, S, POOL_WIDTH)[:, S - POOL_STATE:]
    pool_state_s = jnp.concatenate([state_pool[l], us.reshape(Bd, Ls, POOL_WIDTH)], axis=1)[:, -POOL_STATE:]
    return (y_prompt, y_sample,
            kv6(kvc_p, Bp, S), kv6(kvs_p, Bp, S), kv6(win_p, Bp, min(WINDOW, S)),
            pool_state_p[None], mem_kv_p.reshape(1, Bp, MEM_LEN, 2, MEM_HEADS, MEM_HEAD_DIM),
            kv6(kvc_s, Bd, Ls), kv6(kvs_s, Bd, Ls), win_s[None], pool_state_s[None])
```

```python
import functools

import jax
import jax.numpy as jnp
from jax import lax
from jax.experimental import pallas as pl
from jax.experimental.pallas import tpu as pltpu

F32 = jnp.float32
BF16 = jnp.bfloat16
I32 = jnp.int32

D_MODEL = 1024
POOL_WIDTH = 512
POOL_WINDOWS = (2, 4, 8, 16)
POOL_GROUP = 128
POOL_STATE = 15
NSA_WIDTH = 512
DH = 64
NSA_HEADS = 8
KVH = 2
HPG = 4
CMP_LEN = 32
CMP_STRIDE = 16
CMP_HIDDEN = 256
SEL_LEN = 64
SEL_TOP = 16
WINDOW = 512
Q_BLOCK = 128
KV_WIDTH = 256
IN_WIDTH = 1816
ATTN_SCALE = DH ** -0.5
FORCED_SCORE = 1e4
NEG = -1e30
MEM_LEN = 256
MEM_HEADS = 4
MEM_HEAD_DIM = 256
N_EXPERTS = 32
TOP_K = 4
D_FF = 1024
SWIGLU_LIMIT = 7.0
SWIGLU_ALPHA = 1.702
DN_ALPHA = 2.0 ** 0.25
LN_EPS = 1e-5
PAST_LEN = 2048
PAGE_SIZE = 128

LANES = 128
SEL_PAD = 128
KEY_TILE = 512
ROW_TILE = 512
MOE_ROWS = 512
BF16_ROWS = 16
VMEM_LIMIT = 56 * 1024 * 1024

HIGHEST = lax.Precision.HIGHEST


def _dot(a, b):
    return jnp.dot(a, b, preferred_element_type=F32)


def _dot_nt(a, b, precision=None):
    return lax.dot_general(a, b, (((1,), (1,)), ((), ())), preferred_element_type=F32,
                           precision=precision)


def _layer_norm(x, g, b):
    mu = jnp.mean(x, axis=-1, keepdims=True)
    xc = x - mu
    var = jnp.mean(xc * xc, axis=-1, keepdims=True)
    return xc * lax.rsqrt(var + LN_EPS) * g + b


def _params(sem, vmem=VMEM_LIMIT):
    return pltpu.CompilerParams(dimension_semantics=sem, vmem_limit_bytes=vmem)


def _split_store(u, up_ref, q_ref, kvc_ref, kvs_ref, kvw_ref, gate_ref):
    o1 = POOL_WIDTH
    o2 = o1 + NSA_WIDTH
    o3 = o2 + KV_WIDTH
    o4 = o3 + KV_WIDTH
    o5 = o4 + KV_WIDTH
    up_ref[...] = u[:, :o1]
    q_ref[...] = u[:, o1:o2]
    kvc_ref[...] = u[:, o2:o3]
    kvs_ref[...] = u[:, o3:o4]
    kvw_ref[...] = u[:, o4:o5]
    gate_ref[...] = 1.0 / (1.0 + jnp.exp(-u[:, o5:]))


def _inproj_prompt_body(x_ref, w_ref, pw_ref, ps_ref,
                        up_ref, q_ref, kvc_ref, kvs_ref, kvw_ref, gate_ref, pool_ref,
                        ext_ref, *, tm, tiles_per_seq):
    halo = POOL_STATE + 1
    t_in_seq = pl.program_id(0) % tiles_per_seq
    u = _dot(x_ref[...].astype(BF16), w_ref[...])
    _split_store(u, up_ref, q_ref, kvc_ref, kvs_ref, kvw_ref, gate_ref)

    @pl.when(t_in_seq == 0)
    def _():
        ext_ref[0:halo, :] = jnp.zeros((halo, POOL_WIDTH), F32)

    ext_ref[halo:halo + tm, :] = u[:, :POOL_WIDTH]
    pos = t_in_seq * tm + lax.broadcasted_iota(I32, (tm, 1), 0)
    for gi, w in enumerate(POOL_WINDOWS):
        cols = slice(gi * POOL_GROUP, (gi + 1) * POOL_GROUP)
        acc = ext_ref[halo:halo + tm, cols]
        for k in range(1, w):
            acc = acc + ext_ref[halo - k:halo - k + tm, cols]
        cnt = jnp.minimum(pos + 1, w).astype(F32)
        d = acc / cnt - ext_ref[halo:halo + tm, cols]
        o = _dot(d.astype(BF16), pw_ref[gi])
        pool_ref[:, cols] = o * ps_ref[:, cols]
    ext_ref[0:halo, :] = ext_ref[tm:tm + halo, :]


def _inproj_prompt(x2d, w_in_bf, pool_w_bf, pool_scale, seq_len):
    T = x2d.shape[0]
    tm = ROW_TILE
    outs = [POOL_WIDTH, NSA_WIDTH, KV_WIDTH, KV_WIDTH, KV_WIDTH, 3 * NSA_HEADS, POOL_WIDTH]
    row = lambda n: pl.BlockSpec((tm, n), lambda i: (i, 0))
    full = lambda a: pl.BlockSpec(a.shape, lambda i: (0,) * a.ndim)
    return pl.pallas_call(
        functools.partial(_inproj_prompt_body, tm=tm, tiles_per_seq=seq_len // tm),
        grid=(T // tm,),
        in_specs=[row(D_MODEL), full(w_in_bf), full(pool_w_bf), full(pool_scale)],
        out_specs=[row(n) for n in outs],
        out_shape=[jax.ShapeDtypeStruct((T, n), F32) for n in outs],
        scratch_shapes=[pltpu.VMEM((tm + POOL_STATE + 1, POOL_WIDTH), F32)],
        compiler_params=_params(("arbitrary",)),
        name="inproj_prompt",
    )(x2d, w_in_bf, pool_w_bf, pool_scale)


def _inproj_sample_body(x_ref, w_ref, pw_ref, ps_ref, st_ref,
                        up_ref, q_ref, kvc_ref, kvs_ref, kvw_ref, gate_ref, pool_ref,
                        ext_ref, *, nb, ls, pos0):
    halo = POOL_STATE + 1
    tm = nb * ls
    u = _dot(x_ref[...].astype(BF16), w_ref[...])
    _split_store(u, up_ref, q_ref, kvc_ref, kvs_ref, kvw_ref, gate_ref)
    ext_ref[:, 0:halo, :] = st_ref[...]
    ext_ref[:, halo:halo + ls, :] = u[:, :POOL_WIDTH].reshape(nb, ls, POOL_WIDTH)
    pos = pos0 + lax.broadcasted_iota(I32, (1, ls, 1), 1)
    for gi, w in enumerate(POOL_WINDOWS):
        cols = slice(gi * POOL_GROUP, (gi + 1) * POOL_GROUP)
        acc = ext_ref[:, halo:halo + ls, cols]
        for k in range(1, w):
            acc = acc + ext_ref[:, halo - k:halo - k + ls, cols]
        cnt = jnp.minimum(pos + 1, w).astype(F32)
        d = acc / cnt - ext_ref[:, halo:halo + ls, cols]
        o = _dot(d.reshape(tm, POOL_GROUP).astype(BF16), pw_ref[gi])
        pool_ref[:, cols] = o * ps_ref[:, cols]


def _inproj_sample(x2d, w_in_bf, pool_w_bf, pool_scale, state_pad, ls, pos0):
    T = x2d.shape[0]
    nb = ROW_TILE // ls
    tm = nb * ls
    outs = [POOL_WIDTH, NSA_WIDTH, KV_WIDTH, KV_WIDTH, KV_WIDTH, 3 * NSA_HEADS, POOL_WIDTH]
    row = lambda n: pl.BlockSpec((tm, n), lambda i: (i, 0))
    full = lambda a: pl.BlockSpec(a.shape, lambda i: (0,) * a.ndim)
    return pl.pallas_call(
        functools.partial(_inproj_sample_body, nb=nb, ls=ls, pos0=pos0),
        grid=(T // tm,),
        in_specs=[row(D_MODEL), full(w_in_bf), full(pool_w_bf), full(pool_scale),
                  pl.BlockSpec((nb, POOL_STATE + 1, POOL_WIDTH), lambda i: (i, 0, 0))],
        out_specs=[row(n) for n in outs],
        out_shape=[jax.ShapeDtypeStruct((T, n), F32) for n in outs],
        scratch_shapes=[pltpu.VMEM((nb, POOL_STATE + 1 + ls, POOL_WIDTH), F32)],
        compiler_params=_params(("arbitrary",)),
        name="inproj_sample",
    )(x2d, w_in_bf, pool_w_bf, pool_scale, state_pad)


def _matmul_body(x_ref, w_ref, o_ref):
    o_ref[...] = _dot(x_ref[...].astype(BF16), w_ref[...].astype(BF16))


def _matmul(x, w, tn=512):
    M, K = x.shape
    N = w.shape[1]
    return pl.pallas_call(
        _matmul_body,
        grid=(N // tn,),
        in_specs=[pl.BlockSpec((M, K), lambda j: (0, 0)), pl.BlockSpec((K, tn), lambda j: (0, j))],
        out_specs=pl.BlockSpec((M, tn), lambda j: (0, j)),
        out_shape=jax.ShapeDtypeStruct((M, N), F32),
        compiler_params=_params(("arbitrary",)),
        name="mem_kv_proj",
    )(x, w)


def _gelu_tanh(x):
    c = 0.7978845608028654
    return 0.5 * x * (1.0 + jnp.tanh(c * (x + 0.044715 * (x * x * x))))


def _compress(kv_refs, n_chunks, wa_ref, wb_ref, pe_ref, b1_ref, w2_ref, b2_ref):
    outs = []
    for c in range(2):
        acc_a = jnp.zeros((n_chunks, 2 * CMP_HIDDEN), F32)
        acc_b = jnp.zeros((n_chunks, 2 * CMP_HIDDEN), F32)
        for l in range(0, CMP_STRIDE, 2):
            x = jnp.concatenate([kv_refs[c][pl.ds(l, n_chunks, stride=CMP_STRIDE), :],
                                 kv_refs[c][pl.ds(l + 1, n_chunks, stride=CMP_STRIDE), :]], axis=1)
            xa = (x + pe_ref[c, l // 2:l // 2 + 1, :]).astype(BF16)
            xb = (x + pe_ref[c, (CMP_STRIDE + l) // 2:(CMP_STRIDE + l) // 2 + 1, :]).astype(BF16)
            acc_a = acc_a + _dot(xa, wa_ref[c, l // 2])
            acc_b = acc_b + _dot(xb, wb_ref[c, l // 2])
        hid = acc_a + pltpu.roll(acc_b, n_chunks - 1, 0) + b1_ref[c]
        outs.append(_dot(_gelu_tanh(hid).astype(BF16), w2_ref[c]) + b2_ref[c])
    return jnp.concatenate(outs, axis=1)


def _compress_prompt_body(kv_ref, wa_ref, wb_ref, pe_ref, b1_ref, w2_ref, b2_ref, o_ref, k_ref, v_ref,
                          *, n_chunks):
    k_ref[...] = kv_ref[0, :, :LANES]
    v_ref[...] = kv_ref[0, :, LANES:]
    o_ref[0] = _compress((k_ref, v_ref), n_chunks, wa_ref, wb_ref, pe_ref, b1_ref, w2_ref, b2_ref)


def _compress_prompt(kvc, cw):
    B, S, _ = kvc.shape
    n_chunks = S // CMP_STRIDE
    full = lambda a: pl.BlockSpec(a.shape, lambda b: (0,) * a.ndim)
    return pl.pallas_call(
        functools.partial(_compress_prompt_body, n_chunks=n_chunks),
        grid=(B,),
        in_specs=[pl.BlockSpec((1, S, KV_WIDTH), lambda b: (b, 0, 0))] + [full(a) for a in cw],
        out_specs=pl.BlockSpec((1, n_chunks, KV_WIDTH), lambda b: (b, 0, 0)),
        out_shape=jax.ShapeDtypeStruct((B, n_chunks, KV_WIDTH), F32),
        scratch_shapes=[pltpu.VMEM((S, LANES), F32)] * 2,
        compiler_params=_params(("arbitrary",)),
        name="compress_prompt",
    )(kvc, *cw)


def _compress_weights(cmp_pe, cmp_w1, cmp_b1, cmp_w2, cmp_b2):
    z1 = jnp.zeros((2, CMP_LEN, DH, CMP_HIDDEN), F32)
    w1 = jnp.concatenate([jnp.concatenate([cmp_w1, z1], axis=-1),
                          jnp.concatenate([z1, cmp_w1], axis=-1)], axis=2)
    w1 = w1.reshape(2, CMP_LEN // 2, 2 * LANES, 2 * CMP_HIDDEN).astype(BF16)
    wa = w1[:, :CMP_STRIDE // 2]
    wb = w1[:, CMP_STRIDE // 2:]
    pe = jnp.concatenate([cmp_pe, cmp_pe], axis=-1).reshape(2, CMP_LEN // 2, 2 * LANES)
    b1 = jnp.concatenate([cmp_b1, cmp_b1], axis=-1)[:, None, :]
    z2 = jnp.zeros((2, CMP_HIDDEN, DH), F32)
    w2 = jnp.concatenate([jnp.concatenate([cmp_w2, z2], axis=-1),
                          jnp.concatenate([z2, cmp_w2], axis=-1)], axis=1).astype(BF16)
    b2 = jnp.concatenate([cmp_b2, cmp_b2], axis=-1)[:, None, :]
    return wa, wb, pe, b1, w2, b2


def _softmax_rows(s):
    e = jnp.exp(s - jnp.max(s, axis=-1, keepdims=True))
    return e * (1.0 / jnp.sum(e, axis=-1, keepdims=True))


def _lane_iota(n):
    return lax.broadcasted_iota(I32, (n, LANES), 1)


def _key_alibi_cols(pos, lane):
    hi = jnp.left_shift(jnp.right_shift(pos, 6), 6).astype(F32)
    lo = jnp.bitwise_and(pos, SEL_LEN - 1).astype(F32)
    return jnp.where(lane == DH, hi,
                     jnp.where(lane == DH + 1, lo,
                               jnp.where((lane == DH + 2) | (lane == DH + 3), 1.0, 0.0)))


def _key_alibi_rows(pos, sub):
    hi = jnp.left_shift(jnp.right_shift(pos, 6), 6).astype(F32)
    lo = jnp.bitwise_and(pos, SEL_LEN - 1).astype(F32)
    return jnp.where(sub == 0, hi, jnp.where(sub == 1, lo, jnp.where((sub == 2) | (sub == 3), 1.0, 0.0)))


def _halves(x, zero_hi=False):
    lo = _lane_iota(x.shape[0]) < DH
    r = pltpu.roll(x, DH, 1)
    if zero_hi:
        return jnp.where(lo, x, 0.0), jnp.where(lo, r, 0.0)
    return jnp.where(lo, x, r), jnp.where(lo, r, x)


def _fill_queries(qa_ref, qs, q_pos, lq):
    lane = _lane_iota(lq)
    lo_half = lane < DH
    q_hi = jnp.left_shift(jnp.right_shift(q_pos, 7), 7).astype(F32)
    q_lo = jnp.bitwise_and(q_pos, LANES - 1).astype(F32)
    for h in range(NSA_HEADS):
        g, hl = divmod(h, HPG)
        slope = 2.0 ** (-(h + 1))
        slab = qs[:, (h // 2) * LANES:(h // 2 + 1) * LANES]
        if h % 2:
            slab = pltpu.roll(slab, DH, 1)
        ex = jnp.where((lane == DH) | (lane == DH + 1), slope,
                       jnp.where(lane == DH + 2, -slope * q_hi,
                                 jnp.where(lane == DH + 3, -slope * q_lo,
                                           jnp.where(lane == DH + 4, NEG, 0.0))))
        qa_ref[g, hl * lq:(hl + 1) * lq, 0:LANES] = jnp.where(lo_half, slab, ex).astype(BF16)


def _stack4(x):
    return jnp.concatenate([x] * HPG, axis=0)


def _cmp_branch(qa_ref, kc_k, kc_v, q_pos4, lq, n_cmp):
    c_end = lax.broadcasted_iota(I32, (1, n_cmp), 1) * CMP_STRIDE + (CMP_LEN - 1)
    m_c = c_end <= q_pos4
    any_c = (q_pos4 >= CMP_LEN - 1).astype(F32)
    outs, psums = [], []
    for g in range(KVH):
        s = jnp.where(m_c, _dot_nt(qa_ref[g, :, 0:LANES], kc_k[g][...]), NEG)
        p = _softmax_rows(s) * any_c
        outs.append(_dot(p.astype(BF16), kc_v[g][...]))
        psums.append(p[0:lq] + p[lq:2 * lq] + p[2 * lq:3 * lq] + p[3 * lq:4 * lq])
    return outs, psums


def _split3(x):
    hi = x.astype(BF16)
    r1 = x - hi.astype(F32)
    mid = r1.astype(BF16)
    lo = (r1 - mid.astype(F32)).astype(BF16)
    return hi, mid, lo


def _top_blocks_t(imp_ts, pos0):
    blk = lax.broadcasted_iota(I32, (SEL_PAD, LANES), 0)
    qp_t = pos0 + lax.broadcasted_iota(I32, (SEL_PAD, LANES), 1)
    cur = jnp.right_shift(qp_t, 6)
    forced = (blk == 0) | (blk == cur) | (blk == cur - 1)
    valid = jnp.left_shift(blk, 6) <= qp_t
    v = jnp.concatenate([jnp.where(valid, jnp.where(forced, FORCED_SCORE, t), -1.0) for t in imp_ts], axis=1)
    blk_f = lax.broadcasted_iota(I32, (SEL_PAD, KVH * LANES), 0).astype(F32)
    sel = jnp.zeros((SEL_PAD, KVH * LANES), F32)
    for _ in range(SEL_TOP):
        m = jnp.max(v, axis=0, keepdims=True)
        idx = jnp.min(jnp.where(v == m, blk_f, float(SEL_PAD)), axis=0, keepdims=True)
        hit = blk_f == idx
        sel = jnp.where(hit, 1.0, sel)
        v = jnp.where(hit, -jnp.inf, v)
    return [jnp.where((sel[:, g * LANES:(g + 1) * LANES] > 0.5) & valid, 0.0, NEG) for g in range(KVH)]


def _select_blocks(psums, ovt_ref, pos0, lq):
    imp_ts = []
    for g in range(KVH):
        ps = psums[g]
        if lq < LANES:
            ps = jnp.concatenate([ps, jnp.zeros((LANES - lq, ps.shape[1]), F32)], axis=0)
        imp_ts.append(_dot_nt(ovt_ref[...], ps, precision=HIGHEST))
    return [b.T[:lq].astype(BF16) for b in _top_blocks_t(imp_ts, pos0)]


def _store_selbias(qa_ref, selbias, lq):
    for g in range(KVH):
        for hl in range(HPG):
            qa_ref[g, hl * lq:(hl + 1) * lq, LANES:2 * LANES] = selbias[g]


def _combine(gates, eg_ref, o_c, o_s, o_w, lq):
    lo_half = _lane_iota(lq) < DH

    def assemble(per_group):
        slabs = []
        for k in range(NSA_HEADS // 2):
            g, hl = divmod(2 * k, HPG)
            a = per_group[g][hl * lq:(hl + 1) * lq]
            b = per_group[g][(hl + 1) * lq:(hl + 2) * lq]
            slabs.append(jnp.where(lo_half, a, b))
        return jnp.concatenate(slabs, axis=1)

    return (_dot(gates, eg_ref[0]) * assemble(o_c)
            + _dot(gates, eg_ref[1]) * assemble(o_s)
            + _dot(gates, eg_ref[2]) * assemble(o_w))


def _gate_expand():
    r = jnp.arange(3 * NSA_HEADS)
    c = jnp.arange(NSA_WIDTH)
    return jnp.stack([(r[:, None] == 3 * (c[None, :] // DH) + br).astype(F32) for br in range(3)])


def _overlap_t(n_cmp):
    n = jnp.arange(n_cmp)
    s = jnp.arange(SEL_PAD)
    c_first = n * CMP_STRIDE
    c_end = c_first + CMP_LEN - 1
    b_first = s * SEL_LEN
    return ((c_first[None, :] < b_first[:, None] + SEL_LEN) & (c_end[None, :] >= b_first[:, None])).astype(F32)


def _key_rows(kv_f32, pos, invalid=None):
    n = kv_f32.shape[0]
    lane = _lane_iota(n)
    ex = _key_alibi_cols(pos, lane)
    if invalid is not None:
        ex = jnp.where((lane == DH + 4) & invalid, 1.0, ex)
    lo = lane < DH
    return jnp.where(lo, kv_f32, ex), jnp.where(lo, pltpu.roll(kv_f32, DH, 1), ex)


def _block_onehot(pos, n):
    return (lax.broadcasted_iota(I32, (n, SEL_PAD), 1) == jnp.right_shift(pos, 6)).astype(BF16)


def _tile4(x):
    return jnp.concatenate([x] * HPG, axis=1)


def _nsa_prompt_body(q_ref, gate_ref, kc_ref, kvs_ref, w0_ref, w1_ref, w2_ref, w3_ref, w4_ref,
                     ovt_ref, egt_ref, wband_ref, o_ref,
                     kck0, kck1, kcv0, kcv1, ka0, ka1, vt0, vt1, qa0, qa1,
                     sa_ref, sb_ref, m_ref, l_ref, acc_ref, *, seq_len):
    j = pl.program_id(1)
    kc_k, kc_vt, kaug, v_t, qa_t = (kck0, kck1), (kcv0, kcv1), (ka0, ka1), (vt0, vt1), (qa0, qa1)
    n_cmp = kc_ref.shape[1]
    lq = Q_BLOCK
    cols = HPG * lq

    @pl.when(j == 0)
    def _():
        ck0, ck1 = _halves(kc_ref[0, :, :LANES], zero_hi=True)
        kck0[...] = ck0.astype(BF16)
        kck1[...] = ck1.astype(BF16)
        cvt = kc_ref[0, :, LANES:].T.astype(BF16)
        kcv0[...] = cvt[0:DH]
        kcv1[...] = cvt[DH:]

        def build(i, _):
            r0 = pl.multiple_of(i * KEY_TILE, KEY_TILE)
            pos = r0 + lax.broadcasted_iota(I32, (KEY_TILE, 1), 0)
            k0, k1 = _key_rows(kvs_ref[0, pl.ds(r0, KEY_TILE), :LANES], pos)
            onehot = _block_onehot(pos, KEY_TILE)
            ka0[pl.ds(r0, KEY_TILE), :] = jnp.concatenate([k0.astype(BF16), onehot], axis=1)
            ka1[pl.ds(r0, KEY_TILE), :] = jnp.concatenate([k1.astype(BF16), onehot], axis=1)
            vt = kvs_ref[0, pl.ds(r0, KEY_TILE), LANES:].T.astype(BF16)
            vt0[:, pl.ds(r0, KEY_TILE)] = vt[0:DH]
            vt1[:, pl.ds(r0, KEY_TILE)] = vt[DH:]
            return 0

        lax.fori_loop(0, seq_len // KEY_TILE, build, 0)

    st = j * Q_BLOCK
    q_pos = st + lax.broadcasted_iota(I32, (1, lq), 1)
    q_pos4 = _tile4(q_pos)

    q_t = (q_ref[...] * ATTN_SCALE).T
    sub = lax.broadcasted_iota(I32, (DH, lq), 0)
    q_hi = jnp.left_shift(jnp.right_shift(q_pos, 7), 7).astype(F32)
    q_lo = jnp.bitwise_and(q_pos, LANES - 1).astype(F32)
    for h in range(NSA_HEADS):
        g, hl = divmod(h, HPG)
        slope = 2.0 ** (-(h + 1))
        ex = jnp.where(sub <= 1, slope,
                       jnp.where(sub == 2, -slope * q_hi,
                                 jnp.where(sub == 3, -slope * q_lo, jnp.where(sub == 4, NEG, 0.0))))
        qa_t[g][0:DH, hl * lq:(hl + 1) * lq] = q_t[h * DH:(h + 1) * DH].astype(BF16)
        qa_t[g][DH:2 * DH, hl * lq:(hl + 1) * lq] = ex.astype(BF16)

    c_end = lax.broadcasted_iota(I32, (n_cmp, 1), 0) * CMP_STRIDE + (CMP_LEN - 1)
    m_c = c_end <= q_pos4
    any_c = (q_pos4 >= CMP_LEN - 1).astype(F32)
    ovt_bf = ovt_ref[...].astype(BF16)
    o_c, imp_ts = [], []
    for g in range(KVH):
        s = jnp.where(m_c, _dot(kc_k[g][...], qa_t[g][0:LANES, :]), NEG)
        e = jnp.exp(s - jnp.max(s, axis=0, keepdims=True))
        p = e * (any_c / jnp.sum(e, axis=0, keepdims=True))
        o_c.append(_dot(kc_vt[g][...], p.astype(BF16)))
        psum = p[:, 0:lq] + p[:, lq:2 * lq] + p[:, 2 * lq:3 * lq] + p[:, 3 * lq:4 * lq]
        imp_ts.append(sum(_dot(ovt_bf, t) for t in _split3(psum)))
    bias = _top_blocks_t(imp_ts, st)
    for g in range(KVH):
        for hl in range(HPG):
            qa_t[g][2 * DH:, hl * lq:(hl + 1) * lq] = bias[g].astype(BF16)

    n_tiles = (st + Q_BLOCK + KEY_TILE - 1) // KEY_TILE
    for g in range(KVH):
        m_ref[g] = jnp.full((1, cols), NEG, F32)
        l_ref[g] = jnp.zeros((1, cols), F32)
        acc_ref[g] = jnp.zeros((DH, cols), F32)

    def scores(t, s_ref):
        r0 = pl.multiple_of(t * KEY_TILE, KEY_TILE)
        for g in range(KVH):
            s_ref[g] = _dot(kaug[g][pl.ds(r0, KEY_TILE), :], qa_t[g][...])

    def consume(t, s_ref, masked):
        r0 = pl.multiple_of(t * KEY_TILE, KEY_TILE)
        for g in range(KVH):
            s = s_ref[g]
            if masked:
                k_pos = r0 + lax.broadcasted_iota(I32, (KEY_TILE, 1), 0)
                s = jnp.where(k_pos <= q_pos4, s, NEG)
            m = m_ref[g]
            m_new = jnp.maximum(m, jnp.max(s, axis=0, keepdims=True))
            a = jnp.exp(m - m_new)
            e = jnp.exp(s - m_new)
            m_ref[g] = m_new
            l_ref[g] = a * l_ref[g] + jnp.sum(e, axis=0, keepdims=True)
            acc_ref[g] = a * acc_ref[g] + _dot(v_t[g][:, pl.ds(r0, KEY_TILE)], e.astype(BF16))

    scores(0, sa_ref)
    n_pairs = (n_tiles - 1) // 2

    def pair(u, _):
        scores(2 * u + 1, sb_ref)
        consume(2 * u, sa_ref, False)
        scores(2 * u + 2, sa_ref)
        consume(2 * u + 1, sb_ref, False)
        return 0

    lax.fori_loop(0, n_pairs, pair, 0)
    odd_tail = (n_tiles - 1) - 2 * n_pairs == 1

    @pl.when(odd_tail)
    def _():
        scores(n_tiles - 1, sb_ref)
        consume(n_tiles - 2, sa_ref, False)
        consume(n_tiles - 1, sb_ref, True)

    @pl.when(jnp.logical_not(odd_tail))
    def _():
        consume(n_tiles - 1, sa_ref, True)

    o_s = [acc_ref[g] * (1.0 / l_ref[g]) for g in range(KVH)]

    band = jnp.concatenate([w0_ref[0], w1_ref[0], w2_ref[0], w3_ref[0], w4_ref[0]], axis=0)
    n_win = band.shape[0]
    w_pos_col = st - WINDOW + lax.broadcasted_iota(I32, (n_win, 1), 0)
    kw_k = _key_rows(band[:, :LANES], jnp.maximum(w_pos_col, 0), invalid=w_pos_col < 0)
    vw_t = band[:, LANES:].T.astype(BF16)
    band_bias = _tile4(wband_ref[...])
    o_w = []
    for g in range(KVH):
        s = _dot(kw_k[g].astype(BF16), qa_t[g][0:LANES, :]) + band_bias
        e = jnp.exp(s - jnp.max(s, axis=0, keepdims=True))
        o_w.append(_dot(vw_t[g * DH:(g + 1) * DH], e.astype(BF16)) * (1.0 / jnp.sum(e, axis=0, keepdims=True)))

    def heads(per_group):
        return jnp.concatenate([per_group[h // HPG][:, (h % HPG) * lq:(h % HPG + 1) * lq]
                                for h in range(NSA_HEADS)], axis=0)

    gates = gate_ref[...]
    out_t = (_dot_nt(egt_ref[0], gates) * heads(o_c)
             + _dot_nt(egt_ref[1], gates) * heads(o_s)
             + _dot_nt(egt_ref[2], gates) * heads(o_w))
    o_ref[...] = out_t.T


def _gate_expand_t():
    r = jnp.arange(3 * NSA_HEADS)
    c = jnp.arange(NSA_WIDTH)
    return jnp.stack([(3 * (c[:, None] // DH) + br == r[None, :]).astype(F32) for br in range(3)])


def _nsa_prompt(q, gates, kc, kvs, kvw):
    B, S, _ = kvs.shape
    nqb = S // Q_BLOCK
    n_cmp = kc.shape[1]
    ovt = _overlap_t(n_cmp)
    egt = _gate_expand_t()
    n_band = WINDOW // Q_BLOCK + 1
    d_band = jnp.arange(Q_BLOCK)[None, :] + WINDOW - jnp.arange(n_band * Q_BLOCK)[:, None]
    wband = jnp.where((d_band >= 0) & (d_band < WINDOW), 0.0, NEG).astype(F32)

    def band_spec(i):
        return pl.BlockSpec((1, Q_BLOCK, KV_WIDTH),
                            lambda b, j, i=i: (b, jnp.maximum(j - (n_band - 1) + i, 0), 0))

    full = lambda a: pl.BlockSpec(a.shape, lambda b, j: (0,) * a.ndim)
    return pl.pallas_call(
        functools.partial(_nsa_prompt_body, seq_len=S),
        grid=(B, nqb),
        in_specs=[pl.BlockSpec((Q_BLOCK, NSA_WIDTH), lambda b, j: (b * nqb + j, 0)),
                  pl.BlockSpec((Q_BLOCK, 3 * NSA_HEADS), lambda b, j: (b * nqb + j, 0)),
                  pl.BlockSpec((1, n_cmp, KV_WIDTH), lambda b, j: (b, 0, 0)),
                  pl.BlockSpec((1, S, KV_WIDTH), lambda b, j: (b, 0, 0))]
                 + [band_spec(i) for i in range(n_band)] + [full(ovt), full(egt), full(wband)],
        out_specs=pl.BlockSpec((Q_BLOCK, NSA_WIDTH), lambda b, j: (b * nqb + j, 0)),
        out_shape=jax.ShapeDtypeStruct((B * S, NSA_WIDTH), F32),
        scratch_shapes=[pltpu.VMEM((n_cmp, LANES), BF16)] * 2
                       + [pltpu.VMEM((DH, n_cmp), BF16)] * 2
                       + [pltpu.VMEM((S, 2 * LANES), BF16)] * 2
                       + [pltpu.VMEM((DH, S), BF16)] * 2
                       + [pltpu.VMEM((2 * LANES, HPG * Q_BLOCK), BF16)] * 2
                       + [pltpu.VMEM((KVH, KEY_TILE, HPG * Q_BLOCK), F32)] * 2
                       + [pltpu.VMEM((KVH, 1, HPG * Q_BLOCK), F32)] * 2
                       + [pltpu.VMEM((KVH, DH, HPG * Q_BLOCK), F32)],
        compiler_params=_params(("arbitrary", "arbitrary")),
        name="nsa_prompt",
    )(q, gates, kc, kvs, *([kvw] * n_band), ovt, egt, wband)


def _nsa_sample_body(pt_ref, q_ref, gate_ref, kvs_new_ref, win_ref, kvw_new_ref, *rest,
                     n_pages, ls, past_len):
    cmp_pages = rest[:n_pages]
    slc_pages = rest[n_pages:2 * n_pages]
    (wa_ref, wb_ref, pe_ref, b1_ref, w2_ref, b2_ref, ovt_ref, eg_ref, o_ref,
     full_k, full_v, kck0, kck1, kcv0, kcv1, kt0, kt1, vt0, vt1, wkt0, wkt1, qa_ref) = rest[2 * n_pages:]
    del pt_ref
    kc_k, kc_v, kaug_t, v_t, wk_t = (kck0, kck1), (kcv0, kcv1), (kt0, kt1), (vt0, vt1), (wkt0, wkt1)
    n_cmp = past_len // CMP_STRIDE
    w_rows = win_ref.shape[4]
    lq = BF16_ROWS
    rows = HPG * lq
    w_start = past_len - w_rows

    @pl.when(pl.program_id(0) == 0)
    def _():
        sub = lax.broadcasted_iota(I32, (DH, past_len), 0)
        pos = lax.broadcasted_iota(I32, (1, past_len), 1)
        ex = _key_alibi_rows(pos, sub).astype(BF16)
        onehot = (lax.broadcasted_iota(I32, (SEL_PAD, past_len), 0) == jnp.right_shift(pos, 6)).astype(BF16)
        subw = lax.broadcasted_iota(I32, (DH, w_rows), 0)
        exw = _key_alibi_rows(w_start + lax.broadcasted_iota(I32, (1, w_rows), 1), subw).astype(BF16)
        for g in range(KVH):
            kaug_t[g][DH:2 * DH, :] = ex
            kaug_t[g][2 * DH:, :] = onehot
            wk_t[g][DH:, :] = exw

    for p in range(n_pages):
        cols = slice(p * PAGE_SIZE, (p + 1) * PAGE_SIZE)
        full_k[cols, :] = cmp_pages[p][0, 0].reshape(2 * DH, PAGE_SIZE).T
        full_v[cols, :] = cmp_pages[p][0, 1].reshape(2 * DH, PAGE_SIZE).T
        for g in range(KVH):
            kaug_t[g][0:DH, cols] = slc_pages[p][0, 0, g].astype(BF16)
            vt = slc_pages[p][0, 1, g].astype(BF16)
            v_t[g][0:DH, cols] = vt
            v_t[g][DH:, cols] = vt
    for g in range(KVH):
        wk_t[g][0:DH, :] = win_ref[0, 0, g].astype(BF16)

    kc = _compress((full_k, full_v), n_cmp, wa_ref, wb_ref, pe_ref, b1_ref, w2_ref, b2_ref)
    ck0, ck1 = _halves(kc[:, :LANES], zero_hi=True)
    cv0, cv1 = _halves(kc[:, LANES:])
    kck0[...] = ck0.astype(BF16)
    kck1[...] = ck1.astype(BF16)
    kcv0[...] = cv0.astype(BF16)
    kcv1[...] = cv1.astype(BF16)

    pad_q = jnp.zeros((lq - ls, NSA_WIDTH), F32)
    q_pos = past_len + lax.broadcasted_iota(I32, (lq, 1), 0)
    q_pos4 = _stack4(q_pos)
    _fill_queries(qa_ref, jnp.concatenate([q_ref[0] * ATTN_SCALE, pad_q], axis=0), q_pos, lq)
    gates = jnp.concatenate([gate_ref[0], jnp.zeros((lq - ls, 3 * NSA_HEADS), F32)], axis=0)

    o_c, psums = _cmp_branch(qa_ref, kc_k, kc_v, q_pos4, lq, n_cmp)
    _store_selbias(qa_ref, _select_blocks(psums, ovt_ref, past_len, lq), lq)

    pad_k = jnp.zeros((LANES - ls, KV_WIDTH), F32)
    new_pos_col = past_len + lax.broadcasted_iota(I32, (LANES, 1), 0)
    new_pos = past_len + lax.broadcasted_iota(I32, (1, LANES), 1)
    new_s = jnp.concatenate([kvs_new_ref[0], pad_k], axis=0)
    new_w = jnp.concatenate([kvw_new_ref[0], pad_k], axis=0)
    ks_new = _key_rows(new_s[:, :LANES], new_pos_col)
    vs_new = _halves(new_s[:, LANES:])
    kw_new = _key_rows(new_w[:, :LANES], new_pos_col)
    vw_new = _halves(new_w[:, LANES:])
    onehot_new = _block_onehot(new_pos_col, LANES)
    causal_new = q_pos4 >= new_pos

    o_s, o_w = [], []
    d_past = q_pos4 - (w_start + lax.broadcasted_iota(I32, (1, w_rows), 1))
    m_past = (d_past >= 0) & (d_past < WINDOW)
    d_new = q_pos4 - new_pos
    m_new = (d_new >= 0) & (d_new < WINDOW)
    for g in range(KVH):
        s_past = _dot(qa_ref[g], kaug_t[g][...])
        k_new = jnp.concatenate([ks_new[g].astype(BF16), onehot_new], axis=1)
        s_new = jnp.where(causal_new, _dot_nt(qa_ref[g], k_new), NEG)
        m = jnp.maximum(jnp.max(s_past, axis=-1, keepdims=True), jnp.max(s_new, axis=-1, keepdims=True))
        e_past = jnp.exp(s_past - m)
        e_new = jnp.exp(s_new - m)
        den = jnp.sum(e_past, axis=-1, keepdims=True) + jnp.sum(e_new, axis=-1, keepdims=True)
        acc = _dot_nt(e_past.astype(BF16), v_t[g][...]) + _dot(e_new.astype(BF16), vs_new[g].astype(BF16))
        o_s.append(acc / den)

        sw_past = jnp.where(m_past, _dot(qa_ref[g, :, 0:LANES], wk_t[g][...]), NEG)
        sw_new = jnp.where(m_new, _dot_nt(qa_ref[g, :, 0:LANES], kw_new[g].astype(BF16)), NEG)
        m = jnp.maximum(jnp.max(sw_past, axis=-1, keepdims=True), jnp.max(sw_new, axis=-1, keepdims=True))
        e_past = jnp.exp(sw_past - m)
        e_new = jnp.exp(sw_new - m)
        den = jnp.sum(e_past, axis=-1, keepdims=True) + jnp.sum(e_new, axis=-1, keepdims=True)
        vw = win_ref[0, 1, g].astype(BF16)
        vw2 = jnp.concatenate([vw, vw], axis=0)
        acc = _dot_nt(e_past.astype(BF16), vw2) + _dot(e_new.astype(BF16), vw_new[g].astype(BF16))
        o_w.append(acc / den)

    o_ref[0] = _combine(gates, eg_ref, o_c, o_s, o_w, lq)[:ls]


def _nsa_sample(q, gates, kvs_new, kvw_new, win_t, cmp_t, slc_t, page_table, cw, past_len):
    Bd, ls, _ = q.shape
    n_pages = page_table.shape[1]
    n_cmp = past_len // CMP_STRIDE
    ovt = _overlap_t(n_cmp)
    eg = _gate_expand()
    w_rows = win_t.shape[4]

    per_b = lambda a: pl.BlockSpec((1,) + a.shape[1:], lambda b, pt: (b,) + (0,) * (a.ndim - 1))
    full = lambda a: pl.BlockSpec(a.shape, lambda b, pt: (0,) * a.ndim)
    page = lambda p: pl.BlockSpec((1, 2, KVH, DH, PAGE_SIZE), lambda b, pt, p=p: (pt[b, p], 0, 0, 0, 0))
    grid_spec = pltpu.PrefetchScalarGridSpec(
        num_scalar_prefetch=1,
        grid=(Bd,),
        in_specs=[per_b(q), per_b(gates), per_b(kvs_new), per_b(win_t), per_b(kvw_new)]
                 + [page(p) for p in range(n_pages)] * 2
                 + [full(a) for a in cw] + [full(ovt), full(eg)],
        out_specs=pl.BlockSpec((1, ls, NSA_WIDTH), lambda b, pt: (b, 0, 0)),
        scratch_shapes=[pltpu.VMEM((past_len, LANES), F32)] * 2
                       + [pltpu.VMEM((n_cmp, LANES), BF16)] * 4
                       + [pltpu.VMEM((2 * LANES, past_len), BF16)] * 2
                       + [pltpu.VMEM((LANES, past_len), BF16)] * 2
                       + [pltpu.VMEM((LANES, w_rows), BF16)] * 2
                       + [pltpu.VMEM((KVH, HPG * BF16_ROWS, 2 * LANES), BF16)],
    )
    return pl.pallas_call(
        functools.partial(_nsa_sample_body, n_pages=n_pages, ls=ls, past_len=past_len),
        grid_spec=grid_spec,
        out_shape=jax.ShapeDtypeStruct((Bd, ls, NSA_WIDTH), F32),
        compiler_params=_params(("arbitrary",)),
        name="nsa_sample",
    )(page_table, q, gates, kvs_new, win_t, kvw_new,
      *([cmp_t] * n_pages), *([slc_t] * n_pages), *cw, ovt, eg)


def _fin1_body(h_ref, pool_ref, nsa_ref, wo_ref, g_ref, b_ref, wq_ref, h1_ref, qm_ref):
    mix = (_dot(pool_ref[...].astype(BF16), wo_ref[0:POOL_WIDTH, :])
           + _dot(nsa_ref[...].astype(BF16), wo_ref[POOL_WIDTH:, :]))
    h1 = _layer_norm(DN_ALPHA * h_ref[...] + mix, g_ref[...], b_ref[...])
    h1_ref[...] = h1
    qm_ref[...] = _dot(h1.astype(BF16), wq_ref[...])


def _fin1(h, pool_o, nsa_o, w_out_bf, g, b, wq_bf):
    T = h.shape[0]
    tm = ROW_TILE
    row = lambda n: pl.BlockSpec((tm, n), lambda i: (i, 0))
    full = lambda a: pl.BlockSpec(a.shape, lambda i: (0,) * a.ndim)
    return pl.pallas_call(
        _fin1_body,
        grid=(T // tm,),
        in_specs=[row(D_MODEL), row(POOL_WIDTH), row(NSA_WIDTH), full(w_out_bf), full(g), full(b), full(wq_bf)],
        out_specs=[row(D_MODEL), row(D_MODEL)],
        out_shape=[jax.ShapeDtypeStruct((T, D_MODEL), F32)] * 2,
        compiler_params=_params(("arbitrary",)),
        name="out_proj_ln1",
    )(h, pool_o, nsa_o, w_out_bf, g, b, wq_bf)


def _memattn_body(q_ref, kv_ref, o_ref):
    width = MEM_HEADS * MEM_HEAD_DIM
    for h in range(MEM_HEADS):
        cols = slice(h * MEM_HEAD_DIM, (h + 1) * MEM_HEAD_DIM)
        qh = (q_ref[0, :, cols] * (MEM_HEAD_DIM ** -0.5)).astype(BF16)
        kh = kv_ref[0, :, cols].astype(BF16)
        vh = kv_ref[0, :, width + h * MEM_HEAD_DIM:width + (h + 1) * MEM_HEAD_DIM].astype(BF16)
        s = _dot_nt(qh, kh)
        e = jnp.exp(s - jnp.max(s, axis=-1, keepdims=True))
        o_ref[0, :, cols] = _dot(e.astype(BF16), vh) * (1.0 / jnp.sum(e, axis=-1, keepdims=True))


def _memattn(qm, mem_kv, tq):
    nb, L, W = qm.shape
    return pl.pallas_call(
        _memattn_body,
        grid=(nb, L // tq),
        in_specs=[pl.BlockSpec((1, tq, W), lambda b, t: (b, t, 0)),
                  pl.BlockSpec((1, MEM_LEN, 2 * W), lambda b, t: (b, 0, 0))],
        out_specs=pl.BlockSpec((1, tq, W), lambda b, t: (b, t, 0)),
        out_shape=jax.ShapeDtypeStruct((nb, L, W), F32),
        compiler_params=_params(("arbitrary", "arbitrary")),
        name="mem_attn",
    )(qm, mem_kv)


def _memattn_few_body(q_ref, kv_ref, o_ref):
    lq = q_ref.shape[1]
    n_keys = MEM_LEN * MEM_HEADS
    q = q_ref[0] * (MEM_HEAD_DIM ** -0.5)
    qs = jnp.concatenate([q[:, h * MEM_HEAD_DIM:(h + 1) * MEM_HEAD_DIM] for h in range(MEM_HEADS)], axis=0)
    k = kv_ref[0, :, 0, :, :].reshape(n_keys, MEM_HEAD_DIM).astype(BF16)
    v = kv_ref[0, :, 1, :, :].reshape(n_keys, MEM_HEAD_DIM).astype(BF16)
    s = _dot_nt(qs.astype(BF16), k)
    assert lq & (lq - 1) == 0 and MEM_HEADS & (MEM_HEADS - 1) == 0
    col_h = jnp.bitwise_and(lax.broadcasted_iota(I32, s.shape, 1), MEM_HEADS - 1)
    row_h = jnp.right_shift(lax.broadcasted_iota(I32, s.shape, 0), lq.bit_length() - 1)
    s = jnp.where(col_h == row_h, s, NEG)
    e = jnp.exp(s - jnp.max(s, axis=-1, keepdims=True))
    o = _dot(e.astype(BF16), v) * (1.0 / jnp.sum(e, axis=-1, keepdims=True))
    for h in range(MEM_HEADS):
        o_ref[0, :, h * MEM_HEAD_DIM:(h + 1) * MEM_HEAD_DIM] = o[h * lq:(h + 1) * lq]


def _memattn_few(qm, mem_kv):
    nb, lq, W = qm.shape
    return pl.pallas_call(
        _memattn_few_body,
        grid=(nb,),
        in_specs=[pl.BlockSpec((1, lq, W), lambda b: (b, 0, 0)),
                  pl.BlockSpec((1, MEM_LEN, 2, MEM_HEADS, MEM_HEAD_DIM), lambda b: (b, 0, 0, 0, 0))],
        out_specs=pl.BlockSpec((1, lq, W), lambda b: (b, 0, 0)),
        out_shape=jax.ShapeDtypeStruct((nb, lq, W), F32),
        compiler_params=_params(("arbitrary",)),
        name="mem_attn_few",
    )(qm, mem_kv)


def _fin2_body(cnt0_ref, h1_ref, o_ref, wo_ref, g_ref, b_ref, rw_ref, rb_ref,
               h2_ref, te_ref, tg_ref, cnt_ref, run_ref):
    tm = h1_ref.shape[0]

    @pl.when(pl.program_id(0) == 0)
    def _():
        run_ref[...] = cnt0_ref[...]

    a = _dot(o_ref[...].astype(BF16), wo_ref[...])
    h2 = _layer_norm(DN_ALPHA * h1_ref[...] + a, g_ref[...], b_ref[...])
    h2_ref[...] = h2
    logits = jnp.dot(h2, rw_ref[...], preferred_element_type=F32, precision=HIGHEST) + rb_ref[...]
    e_iota = lax.broadcasted_iota(I32, (tm, N_EXPERTS), 1).astype(F32)
    lane = lax.broadcasted_iota(I32, (tm, LANES), 1)
    te = jnp.zeros((tm, LANES), F32)
    tv = jnp.full((tm, LANES), NEG, F32)
    work = logits
    chosen = []
    for k in range(TOP_K):
        m = jnp.max(work, axis=-1, keepdims=True)
        idx = jnp.min(jnp.where(work == m, e_iota, float(N_EXPERTS)), axis=-1, keepdims=True)
        hit = e_iota == idx
        chosen.append(hit)
        te = jnp.where(lane == k, idx, te)
        tv = jnp.where(lane == k, m, tv)
        work = jnp.where(hit, -jnp.inf, work)
    member = sum(c.astype(F32) for c in chosen)
    earlier = (lax.broadcasted_iota(I32, (tm, tm), 0) > lax.broadcasted_iota(I32, (tm, tm), 1)).astype(BF16)
    before = _dot(earlier, member.astype(BF16)) + run_ref[...]
    for k in range(TOP_K):
        rank = jnp.sum(jnp.where(chosen[k], before, 0.0), axis=-1, keepdims=True)
        te = jnp.where(lane == TOP_K + k, rank, te)
    run_ref[...] = run_ref[...] + jnp.sum(member, axis=0, keepdims=True)
    cnt_ref[...] = run_ref[...]
    ex = jnp.exp(tv - jnp.max(tv, axis=-1, keepdims=True))
    te_ref[...] = te.astype(I32)
    tg_ref[...] = ex / jnp.sum(ex, axis=-1, keepdims=True)


def _fin2(cnt0, h1, o, wo_bf, g, b, rw, rb):
    T = h1.shape[0]
    tm = ROW_TILE
    row = lambda n: pl.BlockSpec((tm, n), lambda i: (i, 0))
    full = lambda a: pl.BlockSpec(a.shape, lambda i: (0,) * a.ndim)
    return pl.pallas_call(
        _fin2_body,
        grid=(T // tm,),
        in_specs=[full(cnt0), row(D_MODEL), row(D_MODEL), full(wo_bf), full(g), full(b), full(rw), full(rb)],
        out_specs=[row(D_MODEL), row(LANES), row(LANES), full(cnt0)],
        out_shape=[jax.ShapeDtypeStruct((T, D_MODEL), F32), jax.ShapeDtypeStruct((T, LANES), I32),
                   jax.ShapeDtypeStruct((T, LANES), F32), jax.ShapeDtypeStruct(cnt0.shape, F32)],
        scratch_shapes=[pltpu.VMEM(cnt0.shape, F32)],
        compiler_params=_params(("arbitrary",)),
        name="mem_out_ln2_router",
    )(cnt0, h1, o, wo_bf, g, b, rw, rb)


def _moe_body(ut_ref, ue_ref, nu_ref, rs_ref, re_ref, x_ref, rw_ref, wgu_ref, bgu_ref, wdn_ref, bdn_ref,
              y_ref, wgu_bf, wdn_bf):
    u = pl.program_id(0)
    bk = x_ref.shape[0]
    e = ue_ref[u]
    tile = ut_ref[u]
    prev = jnp.maximum(u - 1, 0)

    @pl.when((u == 0) | (e != ue_ref[prev]))
    def _():
        wgu_bf[...] = wgu_ref[0].astype(BF16)
        wdn_bf[...] = wdn_ref[0].astype(BF16)

    @pl.when(u < nu_ref[0])
    def _():
        x = x_ref[...].astype(BF16)
        g = _dot(x, wgu_bf[:, :D_FF]) + bgu_ref[0, :, :D_FF]
        v = _dot(x, wgu_bf[:, D_FF:]) + bgu_ref[0, :, D_FF:]
        g = jnp.minimum(g, SWIGLU_LIMIT)
        v = jnp.clip(v, -SWIGLU_LIMIT, SWIGLU_LIMIT)
        a = g * (1.0 / (1.0 + jnp.exp(-SWIGLU_ALPHA * g))) * (v + 1.0)
        y = _dot(a.astype(BF16), wdn_bf[...]) + bdn_ref[0]
        row = tile * bk + lax.broadcasted_iota(I32, (bk, 1), 0)
        mine = (row >= rs_ref[e]) & (row < re_ref[e])
        y = jnp.where(mine, y * rw_ref[...], 0.0)

        @pl.when((u == 0) | (tile != ut_ref[prev]))
        def _():
            y_ref[...] = y

        @pl.when((u > 0) & (tile == ut_ref[prev]))
        def _():
            y_ref[...] = y_ref[...] + y


def _moe_gmm(x_rows, row_w, units, w_gu, b_gu, w_dn, b_dn):
    N = x_rows.shape[0]
    bk = MOE_ROWS
    unit_tile, unit_e, n_units, r_start, r_end = units
    grid_spec = pltpu.PrefetchScalarGridSpec(
        num_scalar_prefetch=5,
        grid=(unit_tile.shape[0],),
        in_specs=[pl.BlockSpec((bk, D_MODEL), lambda u, ut, ue, *_: (ut[u], 0)),
                  pl.BlockSpec((bk, 1), lambda u, ut, ue, *_: (ut[u], 0)),
                  pl.BlockSpec((1, D_MODEL, 2 * D_FF), lambda u, ut, ue, *_: (ue[u], 0, 0)),
                  pl.BlockSpec((1, 1, 2 * D_FF), lambda u, ut, ue, *_: (ue[u], 0, 0)),
                  pl.BlockSpec((1, D_FF, D_MODEL), lambda u, ut, ue, *_: (ue[u], 0, 0)),
                  pl.BlockSpec((1, 1, D_MODEL), lambda u, ut, ue, *_: (ue[u], 0, 0))],
        out_specs=pl.BlockSpec((bk, D_MODEL), lambda u, ut, ue, *_: (ut[u], 0)),
        scratch_shapes=[pltpu.VMEM((D_MODEL, 2 * D_FF), BF16), pltpu.VMEM((D_FF, D_MODEL), BF16)],
    )
    return pl.pallas_call(
        _moe_body,
        grid_spec=grid_spec,
        out_shape=jax.ShapeDtypeStruct((N, D_MODEL), F32),
        compiler_params=_params(("arbitrary",)),
        name="moe_experts",
    )(unit_tile, unit_e, n_units, r_start, r_end, x_rows, row_w, w_gu, b_gu, w_dn, b_dn)


FLAT_BITS = 17


def _moe_routing(te, tg, counts):
    bk = MOE_ROWS
    T = te.shape[0]
    N = T * TOP_K
    assert N % bk == 0 and N <= (1 << FLAT_BITS)
    experts = jnp.arange(N_EXPERTS, dtype=I32)
    top_e = te[:, :TOP_K]
    r_end = jnp.cumsum(counts).astype(I32)
    r_start = r_end - counts
    onehot = top_e[:, :, None] == experts[None, None, :]
    pos = jnp.sum(jnp.where(onehot, r_start[None, None, :], 0), axis=-1) + te[:, TOP_K:2 * TOP_K]
    key = jnp.left_shift(top_e.reshape(-1), FLAT_BITS) + jnp.arange(N, dtype=I32)
    key_s, gate_s = lax.sort((key, tg[:, :TOP_K].reshape(-1)), num_keys=1)
    tok_s = jnp.right_shift(jnp.bitwise_and(key_s, (1 << FLAT_BITS) - 1), 2)
    first = r_start // bk
    last = (r_end - 1) // bk
    n_e = jnp.where(counts > 0, last - first + 1, 0)
    u_end = jnp.cumsum(n_e).astype(I32)
    u_start = u_end - n_e
    n_units = u_end[-1]
    u = jnp.minimum(jnp.arange(N // bk + N_EXPERTS - 1, dtype=I32), n_units - 1)
    unit_e = jnp.sum((u[:, None] >= u_end[None, :]).astype(I32), axis=1)
    unit_tile = u + jnp.sum(jnp.where(unit_e[:, None] == experts[None, :], (first - u_start)[None, :], 0), axis=1)
    return pos, tok_s, gate_s, (unit_tile, unit_e, n_units.reshape(1), r_start, r_end)


def _fin3_body(h2_ref, y0_ref, y1_ref, y2_ref, y3_ref, g_ref, b_ref, o_ref):
    y = (y0_ref[...] + y1_ref[...]) + (y2_ref[...] + y3_ref[...])
    o_ref[...] = _layer_norm(DN_ALPHA * h2_ref[...] + y, g_ref[...], b_ref[...])


def _fin3(h2, ys, g, b):
    T = h2.shape[0]
    tm = ROW_TILE
    row = lambda n: pl.BlockSpec((tm, n), lambda i: (i, 0))
    full = lambda a: pl.BlockSpec(a.shape, lambda i: (0,) * a.ndim)
    return pl.pallas_call(
        _fin3_body,
        grid=(T // tm,),
        in_specs=[row(D_MODEL)] * (1 + TOP_K) + [full(g), full(b)],
        out_specs=row(D_MODEL),
        out_shape=jax.ShapeDtypeStruct((T, D_MODEL), F32),
        compiler_params=_params(("arbitrary",)),
        name="combine_ln3",
    )(h2, *ys, g, b)


def kernel(x_prompt, x_sample, cache_cmp_kv, cache_slc_kv, state_win_kv, state_pool, cache_mem_kv, page_table,
           mem_prompt, w_in, pool_w, pool_scale, cmp_pe, cmp_w1, cmp_b1, cmp_w2, cmp_b2, w_out, ln1_g, ln1_b,
           mem_wq, mem_wkv, mem_wo, ln2_g, ln2_b, router_w, router_b, exp_w_gu, exp_b_gu, exp_w_dn, exp_b_dn,
           ln3_g, ln3_b):
    Bp, S, D = x_prompt.shape
    Bd, Ls, _ = x_sample.shape
    Tp, Ts = Bp * S, Bd * Ls
    l = 0
    w_in_bf = w_in[l].astype(BF16)
    pool_w_bf = pool_w[l].astype(BF16)
    ps = pool_scale[l][None, :]
    cw = _compress_weights(cmp_pe[l], cmp_w1[l], cmp_b1[l], cmp_w2[l], cmp_b2[l])
    w_out_bf = w_out[l].astype(BF16)
    wq_bf = mem_wq[l].astype(BF16)
    wo_bf = mem_wo[l].astype(BF16)
    vec = lambda a: a[l][None, :]

    up, qp, kvc_p, kvs_p, kvw_p, gp, pool_p = _inproj_prompt(x_prompt.reshape(Tp, D), w_in_bf, pool_w_bf, ps, S)
    kc_p = _compress_prompt(kvc_p.reshape(Bp, S, KV_WIDTH), cw)
    nsa_p = _nsa_prompt(qp, gp, kc_p, kvs_p.reshape(Bp, S, KV_WIDTH), kvw_p.reshape(Bp, S, KV_WIDTH))
    mem_kv_p = _matmul(mem_prompt.reshape(Bp * MEM_LEN, D), mem_wkv[l]).reshape(Bp, MEM_LEN, 2 * D)
    h1_p, qm_p = _fin1(x_prompt.reshape(Tp, D), pool_p, nsa_p, w_out_bf, vec(ln1_g), vec(ln1_b), wq_bf)
    om_p = _memattn(qm_p.reshape(Bp, S, D), mem_kv_p, ROW_TILE).reshape(Tp, D)
    h2_p, te_p, tg_p, cnt_p = _fin2(jnp.zeros((1, N_EXPERTS), F32), h1_p, om_p, wo_bf, vec(ln2_g), vec(ln2_b),
                                    router_w[l], vec(router_b))

    state_pad = jnp.pad(state_pool[l], ((0, 0), (1, 0), (0, 0)))
    us, qs, kvc_s, kvs_s, kvw_s, gs, pool_s = _inproj_sample(
        x_sample.reshape(Ts, D), w_in_bf, pool_w_bf, ps, state_pad, Ls, PAST_LEN)
    w_rows = state_win_kv.shape[2]
    feature_major = lambda a: jnp.transpose(a, (0, 2, 3, 4, 1))
    nsa_s = _nsa_sample(qs.reshape(Bd, Ls, NSA_WIDTH), gs.reshape(Bd, Ls, 3 * NSA_HEADS),
                        kvs_s.reshape(Bd, Ls, KV_WIDTH), kvw_s.reshape(Bd, Ls, KV_WIDTH),
                        feature_major(state_win_kv[l]), feature_major(cache_cmp_kv[l]),
                        feature_major(cache_slc_kv[l]), page_table, cw, PAST_LEN)
    h1_s, qm_s = _fin1(x_sample.reshape(Ts, D), pool_s, nsa_s.reshape(Ts, NSA_WIDTH), w_out_bf,
                       vec(ln1_g), vec(ln1_b), wq_bf)
    om_s = _memattn_few(qm_s.reshape(Bd, Ls, D), cache_mem_kv[l]).reshape(Ts, D)
    h2_s, te_s, tg_s, cnt_s = _fin2(cnt_p, h1_s, om_s, wo_bf, vec(ln2_g), vec(ln2_b), router_w[l], vec(router_b))

    h2 = jnp.concatenate([h2_p, h2_s], axis=0)
    pos, tok_s, gate_s, units = _moe_routing(jnp.concatenate([te_p, te_s], axis=0),
                                             jnp.concatenate([tg_p, tg_s], axis=0), cnt_s[0].astype(I32))
    y_rows = _moe_gmm(h2[tok_s], gate_s[:, None], units, exp_w_gu[l], exp_b_gu[l][:, None, :],
                      exp_w_dn[l], exp_b_dn[l][:, None, :])
    out = _fin3(h2, [y_rows[pos[:, k]] for k in range(TOP_K)], vec(ln3_g), vec(ln3_b))
    y_prompt = out[:Tp].reshape(Bp, S, D)
    y_sample = out[Tp:].reshape(Bd, Ls, D)

    kv6 = lambda a, b, n: a.reshape(1, b, n, 2, KVH, DH)
    win_p = kvw_p.reshape(Bp, S, KV_WIDTH)[:, S - min(WINDOW, S):]
    win_s = jnp.concatenate([state_win_kv[l], kvw_s.reshape(Bd, Ls, 2, KVH, DH)], axis=1)[:, -w_rows:]
    pool_state_p = up.reshape(Bp, S, POOL_WIDTH)[:, S - POOL_STATE:]
    pool_state_s = jnp.concatenate([state_pool[l], us.reshape(Bd, Ls, POOL_WIDTH)], axis=1)[:, -POOL_STATE:]
    return (y_prompt, y_sample,
            kv6(kvc_p, Bp, S), kv6(kvs_p, Bp, S), kv6(win_p, Bp, min(WINDOW, S)),
            pool_state_p[None], mem_kv_p.reshape(1, Bp, MEM_LEN, 2, MEM_HEADS, MEM_HEAD_DIM),
            kv6(kvc_s, Bd, Ls), kv6(kvs_s, Bd, Ls), win_s[None], pool_state_s[None])
```

```python
import functools

import jax
import jax.numpy as jnp
from jax import lax
from jax.experimental import pallas as pl
from jax.experimental.pallas import tpu as pltpu

F32 = jnp.float32
BF16 = jnp.bfloat16
I32 = jnp.int32

D_MODEL = 1024
POOL_WIDTH = 512
POOL_WINDOWS = (2, 4, 8, 16)
POOL_GROUP = 128
POOL_STATE = 15
NSA_WIDTH = 512
DH = 64
NSA_HEADS = 8
KVH = 2
HPG = 4
CMP_LEN = 32
CMP_STRIDE = 16
CMP_HIDDEN = 256
SEL_LEN = 64
SEL_TOP = 16
WINDOW = 512
Q_BLOCK = 128
KV_WIDTH = 256
IN_WIDTH = 1816
ATTN_SCALE = DH ** -0.5
FORCED_SCORE = 1e4
NEG = -1e30
MEM_LEN = 256
MEM_HEADS = 4
MEM_HEAD_DIM = 256
N_EXPERTS = 32
TOP_K = 4
D_FF = 1024
SWIGLU_LIMIT = 7.0
SWIGLU_ALPHA = 1.702
DN_ALPHA = 2.0 ** 0.25
LN_EPS = 1e-5
PAST_LEN = 2048
PAGE_SIZE = 128

LANES = 128
SEL_PAD = 128
KEY_TILE = 512
ROW_TILE = 512
MOE_ROWS = 512
BF16_ROWS = 16
VMEM_LIMIT = 56 * 1024 * 1024

HIGHEST = lax.Precision.HIGHEST


def _dot(a, b):
    return jnp.dot(a, b, preferred_element_type=F32)


def _dot_nt(a, b, precision=None):
    return lax.dot_general(a, b, (((1,), (1,)), ((), ())), preferred_element_type=F32,
                           precision=precision)


def _layer_norm(x, g, b):
    mu = jnp.mean(x, axis=-1, keepdims=True)
    xc = x - mu
    var = jnp.mean(xc * xc, axis=-1, keepdims=True)
    return xc * lax.rsqrt(var + LN_EPS) * g + b


def _params(sem, vmem=VMEM_LIMIT):
    return pltpu.CompilerParams(dimension_semantics=sem, vmem_limit_bytes=vmem)


def _split_store(u, up_ref, q_ref, kvc_ref, kvs_ref, kvw_ref, gate_ref):
    o1 = POOL_WIDTH
    o2 = o1 + NSA_WIDTH
    o3 = o2 + KV_WIDTH
    o4 = o3 + KV_WIDTH
    o5 = o4 + KV_WIDTH
    up_ref[...] = u[:, :o1]
    q_ref[...] = u[:, o1:o2]
    kvc_ref[...] = u[:, o2:o3]
    kvs_ref[...] = u[:, o3:o4]
    kvw_ref[...] = u[:, o4:o5]
    gate_ref[...] = 1.0 / (1.0 + jnp.exp(-u[:, o5:]))


def _inproj_prompt_body(x_ref, w_ref, pw_ref, ps_ref,
                        up_ref, q_ref, kvc_ref, kvs_ref, kvw_ref, gate_ref, pool_ref, kvct_ref, kvst_ref,
                        ext_ref, *, tm, tiles_per_seq):
    halo = POOL_STATE + 1
    t_in_seq = pl.program_id(0) % tiles_per_seq
    u = _dot(x_ref[...].astype(BF16), w_ref[...])
    _split_store(u, up_ref, q_ref, kvc_ref, kvs_ref, kvw_ref, gate_ref)
    o2 = POOL_WIDTH + NSA_WIDTH
    kvct_ref[0] = u[:, o2:o2 + KV_WIDTH].T.reshape(2, KVH, DH, tm)
    kvst_ref[0] = u[:, o2 + KV_WIDTH:o2 + 2 * KV_WIDTH].T.reshape(2, KVH, DH, tm)

    @pl.when(t_in_seq == 0)
    def _():
        ext_ref[0:halo, :] = jnp.zeros((halo, POOL_WIDTH), F32)

    ext_ref[halo:halo + tm, :] = u[:, :POOL_WIDTH]
    pos = t_in_seq * tm + lax.broadcasted_iota(I32, (tm, 1), 0)
    for gi, w in enumerate(POOL_WINDOWS):
        cols = slice(gi * POOL_GROUP, (gi + 1) * POOL_GROUP)
        acc = ext_ref[halo:halo + tm, cols]
        for k in range(1, w):
            acc = acc + ext_ref[halo - k:halo - k + tm, cols]
        cnt = jnp.minimum(pos + 1, w).astype(F32)
        d = acc / cnt - ext_ref[halo:halo + tm, cols]
        o = _dot(d.astype(BF16), pw_ref[gi])
        pool_ref[:, cols] = o * ps_ref[:, cols]
    ext_ref[0:halo, :] = ext_ref[tm:tm + halo, :]


def _inproj_prompt(x2d, w_in_bf, pool_w_bf, pool_scale, seq_len):
    T = x2d.shape[0]
    tm = ROW_TILE
    outs = [POOL_WIDTH, NSA_WIDTH, KV_WIDTH, KV_WIDTH, KV_WIDTH, 3 * NSA_HEADS, POOL_WIDTH]
    row = lambda n: pl.BlockSpec((tm, n), lambda i: (i, 0))
    full = lambda a: pl.BlockSpec(a.shape, lambda i: (0,) * a.ndim)
    tps = seq_len // tm
    kvt_spec = pl.BlockSpec((1, 2, KVH, DH, tm), lambda i: (i // tps, 0, 0, 0, i % tps))
    kvt_shape = jax.ShapeDtypeStruct((T // seq_len, 2, KVH, DH, seq_len), F32)
    return pl.pallas_call(
        functools.partial(_inproj_prompt_body, tm=tm, tiles_per_seq=tps),
        grid=(T // tm,),
        in_specs=[row(D_MODEL), full(w_in_bf), full(pool_w_bf), full(pool_scale)],
        out_specs=[row(n) for n in outs] + [kvt_spec] * 2,
        out_shape=[jax.ShapeDtypeStruct((T, n), F32) for n in outs] + [kvt_shape] * 2,
        scratch_shapes=[pltpu.VMEM((tm + POOL_STATE + 1, POOL_WIDTH), F32)],
        compiler_params=_params(("arbitrary",)),
        name="inproj_prompt",
    )(x2d, w_in_bf, pool_w_bf, pool_scale)


def _inproj_sample_body(x_ref, w_ref, pw_ref, ps_ref, st_ref,
                        up_ref, q_ref, kvc_ref, kvs_ref, kvw_ref, gate_ref, pool_ref,
                        ext_ref, *, nb, ls, pos0):
    halo = POOL_STATE + 1
    tm = nb * ls
    u = _dot(x_ref[...].astype(BF16), w_ref[...])
    _split_store(u, up_ref, q_ref, kvc_ref, kvs_ref, kvw_ref, gate_ref)
    ext_ref[:, 0:halo, :] = st_ref[...]
    ext_ref[:, halo:halo + ls, :] = u[:, :POOL_WIDTH].reshape(nb, ls, POOL_WIDTH)
    pos = pos0 + lax.broadcasted_iota(I32, (1, ls, 1), 1)
    for gi, w in enumerate(POOL_WINDOWS):
        cols = slice(gi * POOL_GROUP, (gi + 1) * POOL_GROUP)
        acc = ext_ref[:, halo:halo + ls, cols]
        for k in range(1, w):
            acc = acc + ext_ref[:, halo - k:halo - k + ls, cols]
        cnt = jnp.minimum(pos + 1, w).astype(F32)
        d = acc / cnt - ext_ref[:, halo:halo + ls, cols]
        o = _dot(d.reshape(tm, POOL_GROUP).astype(BF16), pw_ref[gi])
        pool_ref[:, cols] = o * ps_ref[:, cols]


def _inproj_sample(x2d, w_in_bf, pool_w_bf, pool_scale, state_pad, ls, pos0):
    T = x2d.shape[0]
    nb = ROW_TILE // ls
    tm = nb * ls
    outs = [POOL_WIDTH, NSA_WIDTH, KV_WIDTH, KV_WIDTH, KV_WIDTH, 3 * NSA_HEADS, POOL_WIDTH]
    row = lambda n: pl.BlockSpec((tm, n), lambda i: (i, 0))
    full = lambda a: pl.BlockSpec(a.shape, lambda i: (0,) * a.ndim)
    return pl.pallas_call(
        functools.partial(_inproj_sample_body, nb=nb, ls=ls, pos0=pos0),
        grid=(T // tm,),
        in_specs=[row(D_MODEL), full(w_in_bf), full(pool_w_bf), full(pool_scale),
                  pl.BlockSpec((nb, POOL_STATE + 1, POOL_WIDTH), lambda i: (i, 0, 0))],
        out_specs=[row(n) for n in outs],
        out_shape=[jax.ShapeDtypeStruct((T, n), F32) for n in outs],
        scratch_shapes=[pltpu.VMEM((nb, POOL_STATE + 1 + ls, POOL_WIDTH), F32)],
        compiler_params=_params(("arbitrary",)),
        name="inproj_sample",
    )(x2d, w_in_bf, pool_w_bf, pool_scale, state_pad)


def _matmul_body(x_ref, w_ref, o_ref):
    o_ref[...] = _dot(x_ref[...].astype(BF16), w_ref[...].astype(BF16))


def _matmul(x, w, tn=512):
    M, K = x.shape
    N = w.shape[1]
    return pl.pallas_call(
        _matmul_body,
        grid=(N // tn,),
        in_specs=[pl.BlockSpec((M, K), lambda j: (0, 0)), pl.BlockSpec((K, tn), lambda j: (0, j))],
        out_specs=pl.BlockSpec((M, tn), lambda j: (0, j)),
        out_shape=jax.ShapeDtypeStruct((M, N), F32),
        compiler_params=_params(("arbitrary",)),
        name="mem_kv_proj",
    )(x, w)


def _gelu_tanh(x):
    c = 0.7978845608028654
    return 0.5 * x * (1.0 + jnp.tanh(c * (x + 0.044715 * (x * x * x))))


def _compress(kv_refs, n_chunks, w1_ref, pe_ref, b1_ref, w2_ref, b2_ref):
    lo = _lane_iota(n_chunks) < DH
    quads = CMP_STRIDE // 4
    outs = []
    for c in range(2):
        acc_a = jnp.zeros((2 * n_chunks, CMP_HIDDEN), F32)
        acc_b = jnp.zeros((2 * n_chunks, CMP_HIDDEN), F32)
        for i in range(quads):
            x = [kv_refs[c][pl.ds(4 * i + m, n_chunks, stride=CMP_STRIDE), :] for m in range(4)]
            r = [pltpu.roll(v, DH, 1) for v in x]
            x_g0 = jnp.concatenate([jnp.where(lo, x[0], r[1]), jnp.where(lo, x[2], r[3])], axis=1)
            x_g1 = jnp.concatenate([jnp.where(lo, r[0], x[1]), jnp.where(lo, r[2], x[3])], axis=1)
            xq = jnp.concatenate([x_g0, x_g1], axis=0)
            acc_a = acc_a + _dot((xq + pe_ref[c, i:i + 1, :]).astype(BF16), w1_ref[c, i])
            acc_b = acc_b + _dot((xq + pe_ref[c, quads + i:quads + i + 1, :]).astype(BF16), w1_ref[c, quads + i])
        hid = acc_a + pltpu.roll(acc_b, 2 * n_chunks - 1, 0) + b1_ref[c]
        outs.append(_dot(_gelu_tanh(hid).astype(BF16), w2_ref[c]) + b2_ref[c])
    return outs


def _compress_prompt_body(kv_ref, w1_ref, pe_ref, b1_ref, w2_ref, b2_ref, o_ref, k_ref, v_ref, *, n_chunks):
    k_ref[...] = kv_ref[0, :, :LANES]
    v_ref[...] = kv_ref[0, :, LANES:]
    keys, values = _compress((k_ref, v_ref), n_chunks, w1_ref, pe_ref, b1_ref, w2_ref, b2_ref)
    o_ref[0, 0] = keys
    o_ref[0, 1] = values


def _compress_prompt(kvc, cw):
    B, S, _ = kvc.shape
    n_chunks = S // CMP_STRIDE
    full = lambda a: pl.BlockSpec(a.shape, lambda b: (0,) * a.ndim)
    return pl.pallas_call(
        functools.partial(_compress_prompt_body, n_chunks=n_chunks),
        grid=(B,),
        in_specs=[pl.BlockSpec((1, S, KV_WIDTH), lambda b: (b, 0, 0))] + [full(a) for a in cw],
        out_specs=pl.BlockSpec((1, 2, KVH * n_chunks, LANES), lambda b: (b, 0, 0, 0)),
        out_shape=jax.ShapeDtypeStruct((B, 2, KVH * n_chunks, LANES), F32),
        scratch_shapes=[pltpu.VMEM((S, LANES), F32)] * 2,
        compiler_params=_params(("arbitrary",)),
        name="compress_prompt",
    )(kvc, *cw)


def _compress_weights(cmp_pe, cmp_w1, cmp_b1, cmp_w2, cmp_b2):
    nq = CMP_LEN // 4
    w1 = cmp_w1.reshape(2, nq, 4 * DH, CMP_HIDDEN).astype(BF16)
    pe = cmp_pe.reshape(2, nq, 4 * DH)
    b1 = cmp_b1[:, None, :]
    w2 = jnp.stack([jnp.concatenate([cmp_w2[0], jnp.zeros_like(cmp_w2[0])], axis=1),
                    jnp.concatenate([cmp_w2[1], cmp_w2[1]], axis=1)]).astype(BF16)
    b2 = jnp.stack([jnp.concatenate([cmp_b2[0], jnp.zeros_like(cmp_b2[0])]),
                    jnp.concatenate([cmp_b2[1], cmp_b2[1]])])[:, None, :]
    return w1, pe, b1, w2, b2


def _softmax_rows(s):
    e = jnp.exp(s - jnp.max(s, axis=-1, keepdims=True))
    return e * (1.0 / jnp.sum(e, axis=-1, keepdims=True))


def _lane_iota(n):
    return lax.broadcasted_iota(I32, (n, LANES), 1)


def _key_alibi_cols(pos, lane):
    hi = jnp.left_shift(jnp.right_shift(pos, 6), 6).astype(F32)
    lo = jnp.bitwise_and(pos, SEL_LEN - 1).astype(F32)
    return jnp.where(lane == DH, hi,
                     jnp.where(lane == DH + 1, lo,
                               jnp.where((lane == DH + 2) | (lane == DH + 3), 1.0, 0.0)))


def _key_alibi_rows(pos, sub):
    hi = jnp.left_shift(jnp.right_shift(pos, 6), 6).astype(F32)
    lo = jnp.bitwise_and(pos, SEL_LEN - 1).astype(F32)
    return jnp.where(sub == 0, hi, jnp.where(sub == 1, lo, jnp.where((sub == 2) | (sub == 3), 1.0, 0.0)))


def _halves(x, zero_hi=False):
    lo = _lane_iota(x.shape[0]) < DH
    r = pltpu.roll(x, DH, 1)
    if zero_hi:
        return jnp.where(lo, x, 0.0), jnp.where(lo, r, 0.0)
    return jnp.where(lo, x, r), jnp.where(lo, r, x)


def _fill_queries(qa_ref, qs, q_pos, lq):
    lane = _lane_iota(lq)
    lo_half = lane < DH
    q_hi = jnp.left_shift(jnp.right_shift(q_pos, 7), 7).astype(F32)
    q_lo = jnp.bitwise_and(q_pos, LANES - 1).astype(F32)
    for h in range(NSA_HEADS):
        g, hl = divmod(h, HPG)
        slope = 2.0 ** (-(h + 1))
        slab = qs[:, (h // 2) * LANES:(h // 2 + 1) * LANES]
        if h % 2:
            slab = pltpu.roll(slab, DH, 1)
        ex = jnp.where((lane == DH) | (lane == DH + 1), slope,
                       jnp.where(lane == DH + 2, -slope * q_hi,
                                 jnp.where(lane == DH + 3, -slope * q_lo,
                                           jnp.where(lane == DH + 4, NEG, 0.0))))
        qa_ref[g, hl * lq:(hl + 1) * lq, 0:LANES] = jnp.where(lo_half, slab, ex).astype(BF16)


def _stack4(x):
    return jnp.concatenate([x] * HPG, axis=0)


def _cmp_branch(qa_ref, kc_k, kc_v, q_pos4, lq, n_cmp):
    c_end = lax.broadcasted_iota(I32, (1, n_cmp), 1) * CMP_STRIDE + (CMP_LEN - 1)
    m_c = c_end <= q_pos4
    any_c = (q_pos4 >= CMP_LEN - 1).astype(F32)
    outs, psums = [], []
    for g in range(KVH):
        s = jnp.where(m_c, _dot_nt(qa_ref[g, :, 0:LANES], kc_k[g][...]), NEG)
        p = _softmax_rows(s) * any_c
        outs.append(_dot(p.astype(BF16), kc_v[g][...]))
        psums.append(p[0:lq] + p[lq:2 * lq] + p[2 * lq:3 * lq] + p[3 * lq:4 * lq])
    return outs, psums


def _split3(x):
    hi = x.astype(BF16)
    r1 = x - hi.astype(F32)
    mid = r1.astype(BF16)
    lo = (r1 - mid.astype(F32)).astype(BF16)
    return hi, mid, lo


def _top_blocks_t(imp_ts, pos0, n_blk=SEL_PAD):
    blk = lax.broadcasted_iota(I32, (n_blk, LANES), 0)
    qp_t = pos0 + lax.broadcasted_iota(I32, (n_blk, LANES), 1)
    cur = jnp.right_shift(qp_t, 6)
    forced = (blk == 0) | (blk == cur) | (blk == cur - 1)
    valid = jnp.left_shift(blk, 6) <= qp_t
    v = jnp.concatenate([jnp.where(valid, jnp.where(forced, FORCED_SCORE, t), -1.0) for t in imp_ts], axis=1)
    blk_f = lax.broadcasted_iota(I32, (n_blk, KVH * LANES), 0).astype(F32)
    sel = jnp.zeros((n_blk, KVH * LANES), F32)
    for _ in range(SEL_TOP):
        m = jnp.max(v, axis=0, keepdims=True)
        idx = jnp.min(jnp.where(v == m, blk_f, float(n_blk)), axis=0, keepdims=True)
        hit = blk_f == idx
        sel = jnp.where(hit, 1.0, sel)
        v = jnp.where(hit, -jnp.inf, v)
    out = [jnp.where((sel[:, g * LANES:(g + 1) * LANES] > 0.5) & valid, 0.0, NEG) for g in range(KVH)]
    if n_blk < SEL_PAD:
        out = [jnp.concatenate([b, jnp.full((SEL_PAD - n_blk, LANES), NEG, F32)], axis=0) for b in out]
    return out


def _select_blocks(psums, ovt_ref, pos0, lq, n_blk):
    imp_ts = []
    for g in range(KVH):
        ps = psums[g]
        if lq < LANES:
            ps = jnp.concatenate([ps, jnp.zeros((LANES - lq, ps.shape[1]), F32)], axis=0)
        imp_ts.append(_dot_nt(ovt_ref[0:n_blk, :], ps, precision=HIGHEST))
    return [b.T[:lq].astype(BF16) for b in _top_blocks_t(imp_ts, pos0, n_blk)]


def _store_selbias(qa_ref, selbias, lq):
    for g in range(KVH):
        for hl in range(HPG):
            qa_ref[g, hl * lq:(hl + 1) * lq, LANES:2 * LANES] = selbias[g]


def _combine(gates, eg_ref, o_c, o_s, o_w, lq):
    lo_half = _lane_iota(lq) < DH

    def assemble(per_group):
        slabs = []
        for k in range(NSA_HEADS // 2):
            g, hl = divmod(2 * k, HPG)
            a = per_group[g][hl * lq:(hl + 1) * lq]
            b = per_group[g][(hl + 1) * lq:(hl + 2) * lq]
            slabs.append(jnp.where(lo_half, a, b))
        return jnp.concatenate(slabs, axis=1)

    return (_dot(gates, eg_ref[0]) * assemble(o_c)
            + _dot(gates, eg_ref[1]) * assemble(o_s)
            + _dot(gates, eg_ref[2]) * assemble(o_w))


def _gate_expand():
    r = jnp.arange(3 * NSA_HEADS)
    c = jnp.arange(NSA_WIDTH)
    return jnp.stack([(r[:, None] == 3 * (c[None, :] // DH) + br).astype(F32) for br in range(3)])


def _overlap_t(n_cmp):
    n = jnp.arange(n_cmp)
    s = jnp.arange(SEL_PAD)
    c_first = n * CMP_STRIDE
    c_end = c_first + CMP_LEN - 1
    b_first = s * SEL_LEN
    return ((c_first[None, :] < b_first[:, None] + SEL_LEN) & (c_end[None, :] >= b_first[:, None])).astype(F32)


def _key_rows(kv_f32, pos, invalid=None):
    n = kv_f32.shape[0]
    lane = _lane_iota(n)
    ex = _key_alibi_cols(pos, lane)
    if invalid is not None:
        ex = jnp.where((lane == DH + 4) & invalid, 1.0, ex)
    lo = lane < DH
    return jnp.where(lo, kv_f32, ex), jnp.where(lo, pltpu.roll(kv_f32, DH, 1), ex)


def _block_onehot(pos, n):
    return (lax.broadcasted_iota(I32, (n, SEL_PAD), 1) == jnp.right_shift(pos, 6)).astype(BF16)


def _tile4(x):
    return jnp.concatenate([x] * HPG, axis=1)


def _nsa_prompt_body(q_ref, gate_ref, kc_ref, kvs_ref, w0_ref, w1_ref, w2_ref, w3_ref, w4_ref,
                     ovt_ref, egt_ref, wband_ref, o_ref,
                     kck0, kck1, kcv0, kcv1, ka0, ka1, vt0, vt1, qa0, qa1,
                     sa_ref, sb_ref, m_ref, l_ref, acc_ref, *, seq_len):
    j = pl.program_id(1)
    kc_k, kc_vt, kaug, v_t, qa_t = (kck0, kck1), (kcv0, kcv1), (ka0, ka1), (vt0, vt1), (qa0, qa1)
    n_cmp = kc_ref.shape[2] // KVH
    lq = Q_BLOCK
    cols = HPG * lq

    @pl.when(j == 0)
    def _():
        for g in range(KVH):
            kc_k[g][...] = kc_ref[0, 0, g * n_cmp:(g + 1) * n_cmp, :].astype(BF16)
            kc_vt[g][...] = kc_ref[0, 1, g * n_cmp:(g + 1) * n_cmp, :].T[0:DH].astype(BF16)

        def build(i, _):
            r0 = pl.multiple_of(i * KEY_TILE, KEY_TILE)
            pos = r0 + lax.broadcasted_iota(I32, (KEY_TILE, 1), 0)
            k0, k1 = _key_rows(kvs_ref[0, pl.ds(r0, KEY_TILE), :LANES], pos)
            onehot = _block_onehot(pos, KEY_TILE)
            ka0[pl.ds(r0, KEY_TILE), :] = jnp.concatenate([k0.astype(BF16), onehot], axis=1)
            ka1[pl.ds(r0, KEY_TILE), :] = jnp.concatenate([k1.astype(BF16), onehot], axis=1)
            vt = kvs_ref[0, pl.ds(r0, KEY_TILE), LANES:].T.astype(BF16)
            vt0[:, pl.ds(r0, KEY_TILE)] = vt[0:DH]
            vt1[:, pl.ds(r0, KEY_TILE)] = vt[DH:]
            return 0

        lax.fori_loop(0, seq_len // KEY_TILE, build, 0)

    st = j * Q_BLOCK
    q_pos = st + lax.broadcasted_iota(I32, (1, lq), 1)
    q_pos4 = _tile4(q_pos)

    q_t = (q_ref[...] * ATTN_SCALE).T
    sub = lax.broadcasted_iota(I32, (DH, lq), 0)
    q_hi = jnp.left_shift(jnp.right_shift(q_pos, 7), 7).astype(F32)
    q_lo = jnp.bitwise_and(q_pos, LANES - 1).astype(F32)
    for h in range(NSA_HEADS):
        g, hl = divmod(h, HPG)
        slope = 2.0 ** (-(h + 1))
        ex = jnp.where(sub <= 1, slope,
                       jnp.where(sub == 2, -slope * q_hi,
                                 jnp.where(sub == 3, -slope * q_lo, jnp.where(sub == 4, NEG, 0.0))))
        qa_t[g][0:DH, hl * lq:(hl + 1) * lq] = q_t[h * DH:(h + 1) * DH].astype(BF16)
        qa_t[g][DH:2 * DH, hl * lq:(hl + 1) * lq] = ex.astype(BF16)

    c_end = lax.broadcasted_iota(I32, (n_cmp, 1), 0) * CMP_STRIDE + (CMP_LEN - 1)
    m_c = c_end <= q_pos4
    any_c = (q_pos4 >= CMP_LEN - 1).astype(F32)
    ovt_bf = ovt_ref[...].astype(BF16)
    o_c, imp_ts = [], []
    for g in range(KVH):
        s = jnp.where(m_c, _dot(kc_k[g][...], qa_t[g][0:LANES, :]), NEG)
        e = jnp.exp(s - jnp.max(s, axis=0, keepdims=True))
        p = e * (any_c / jnp.sum(e, axis=0, keepdims=True))
        o_c.append(_dot(kc_vt[g][...], p.astype(BF16)))
        psum = p[:, 0:lq] + p[:, lq:2 * lq] + p[:, 2 * lq:3 * lq] + p[:, 3 * lq:4 * lq]
        imp_ts.append(sum(_dot(ovt_bf, t) for t in _split3(psum)))
    bias = _top_blocks_t(imp_ts, st)
    for g in range(KVH):
        for hl in range(HPG):
            qa_t[g][2 * DH:, hl * lq:(hl + 1) * lq] = bias[g].astype(BF16)

    n_tiles = (st + Q_BLOCK + KEY_TILE - 1) // KEY_TILE
    for g in range(KVH):
        m_ref[g] = jnp.full((1, cols), NEG, F32)
        l_ref[g] = jnp.zeros((1, cols), F32)
        acc_ref[g] = jnp.zeros((DH, cols), F32)

    def scores(t, s_ref):
        r0 = pl.multiple_of(t * KEY_TILE, KEY_TILE)
        for g in range(KVH):
            s_ref[g] = _dot(kaug[g][pl.ds(r0, KEY_TILE), :], qa_t[g][...])

    def consume(t, s_ref, masked):
        r0 = pl.multiple_of(t * KEY_TILE, KEY_TILE)
        for g in range(KVH):
            s = s_ref[g]
            if masked:
                k_pos = r0 + lax.broadcasted_iota(I32, (KEY_TILE, 1), 0)
                s = jnp.where(k_pos <= q_pos4, s, NEG)
            m = m_ref[g]
            m_new = jnp.maximum(m, jnp.max(s, axis=0, keepdims=True))
            a = jnp.exp(m - m_new)
            e = jnp.exp(s - m_new)
            m_ref[g] = m_new
            l_ref[g] = a * l_ref[g] + jnp.sum(e, axis=0, keepdims=True)
            acc_ref[g] = a * acc_ref[g] + _dot(v_t[g][:, pl.ds(r0, KEY_TILE)], e.astype(BF16))

    scores(0, sa_ref)
    n_pairs = (n_tiles - 1) // 2

    def pair(u, _):
        scores(2 * u + 1, sb_ref)
        consume(2 * u, sa_ref, False)
        scores(2 * u + 2, sa_ref)
        consume(2 * u + 1, sb_ref, False)
        return 0

    lax.fori_loop(0, n_pairs, pair, 0)
    odd_tail = (n_tiles - 1) - 2 * n_pairs == 1

    @pl.when(odd_tail)
    def _():
        scores(n_tiles - 1, sb_ref)
        consume(n_tiles - 2, sa_ref, False)
        consume(n_tiles - 1, sb_ref, True)

    @pl.when(jnp.logical_not(odd_tail))
    def _():
        consume(n_tiles - 1, sa_ref, True)

    o_s = [acc_ref[g] * (1.0 / l_ref[g]) for g in range(KVH)]

    band = jnp.concatenate([w0_ref[0], w1_ref[0], w2_ref[0], w3_ref[0], w4_ref[0]], axis=0)
    n_win = band.shape[0]
    w_pos_col = st - WINDOW + lax.broadcasted_iota(I32, (n_win, 1), 0)
    kw_k = _key_rows(band[:, :LANES], jnp.maximum(w_pos_col, 0), invalid=w_pos_col < 0)
    vw_t = band[:, LANES:].T.astype(BF16)
    band_bias = _tile4(wband_ref[...])
    o_w = []
    for g in range(KVH):
        s = _dot(kw_k[g].astype(BF16), qa_t[g][0:LANES, :]) + band_bias
        e = jnp.exp(s - jnp.max(s, axis=0, keepdims=True))
        o_w.append(_dot(vw_t[g * DH:(g + 1) * DH], e.astype(BF16)) * (1.0 / jnp.sum(e, axis=0, keepdims=True)))

    def heads(per_group):
        return jnp.concatenate([per_group[h // HPG][:, (h % HPG) * lq:(h % HPG + 1) * lq]
                                for h in range(NSA_HEADS)], axis=0)

    gates = gate_ref[...]
    out_t = (_dot_nt(egt_ref[0], gates) * heads(o_c)
             + _dot_nt(egt_ref[1], gates) * heads(o_s)
             + _dot_nt(egt_ref[2], gates) * heads(o_w))
    o_ref[...] = out_t.T


def _gate_expand_t():
    r = jnp.arange(3 * NSA_HEADS)
    c = jnp.arange(NSA_WIDTH)
    return jnp.stack([(3 * (c[:, None] // DH) + br == r[None, :]).astype(F32) for br in range(3)])


def _nsa_prompt(q, gates, kc, kvs, kvw):
    B, S, _ = kvs.shape
    nqb = S // Q_BLOCK
    n_cmp = kc.shape[2] // KVH
    ovt = _overlap_t(n_cmp)
    egt = _gate_expand_t()
    n_band = WINDOW // Q_BLOCK + 1
    d_band = jnp.arange(Q_BLOCK)[None, :] + WINDOW - jnp.arange(n_band * Q_BLOCK)[:, None]
    wband = jnp.where((d_band >= 0) & (d_band < WINDOW), 0.0, NEG).astype(F32)

    def band_spec(i):
        return pl.BlockSpec((1, Q_BLOCK, KV_WIDTH),
                            lambda b, j, i=i: (b, jnp.maximum(j - (n_band - 1) + i, 0), 0))

    full = lambda a: pl.BlockSpec(a.shape, lambda b, j: (0,) * a.ndim)
    return pl.pallas_call(
        functools.partial(_nsa_prompt_body, seq_len=S),
        grid=(B, nqb),
        in_specs=[pl.BlockSpec((Q_BLOCK, NSA_WIDTH), lambda b, j: (b * nqb + j, 0)),
                  pl.BlockSpec((Q_BLOCK, 3 * NSA_HEADS), lambda b, j: (b * nqb + j, 0)),
                  pl.BlockSpec((1,) + kc.shape[1:], lambda b, j: (b, 0, 0, 0)),
                  pl.BlockSpec((1, S, KV_WIDTH), lambda b, j: (b, 0, 0))]
                 + [band_spec(i) for i in range(n_band)] + [full(ovt), full(egt), full(wband)],
        out_specs=pl.BlockSpec((Q_BLOCK, NSA_WIDTH), lambda b, j: (b * nqb + j, 0)),
        out_shape=jax.ShapeDtypeStruct((B * S, NSA_WIDTH), F32),
        scratch_shapes=[pltpu.VMEM((n_cmp, LANES), BF16)] * 2
                       + [pltpu.VMEM((DH, n_cmp), BF16)] * 2
                       + [pltpu.VMEM((S, 2 * LANES), BF16)] * 2
                       + [pltpu.VMEM((DH, S), BF16)] * 2
                       + [pltpu.VMEM((2 * LANES, HPG * Q_BLOCK), BF16)] * 2
                       + [pltpu.VMEM((KVH, KEY_TILE, HPG * Q_BLOCK), F32)] * 2
                       + [pltpu.VMEM((KVH, 1, HPG * Q_BLOCK), F32)] * 2
                       + [pltpu.VMEM((KVH, DH, HPG * Q_BLOCK), F32)],
        compiler_params=_params(("arbitrary", "arbitrary")),
        name="nsa_prompt",
    )(q, gates, kc, kvs, *([kvw] * n_band), ovt, egt, wband)


def _nsa_sample_body(pt_ref, q_ref, gate_ref, kvs_new_ref, win_ref, kvw_new_ref, *rest,
                     n_pages, ls, past_len):
    cmp_pages = rest[:n_pages]
    slc_pages = rest[n_pages:2 * n_pages]
    (w1_ref, pe_ref, b1_ref, w2_ref, b2_ref, ovt_ref, eg_ref, o_ref,
     full_k, full_v, kck0, kck1, kcv0, kcv1, kt0, kt1, vt0, vt1, wkt0, wkt1, qa_ref) = rest[2 * n_pages:]
    del pt_ref
    kc_k, kc_v, kaug_t, v_t, wk_t = (kck0, kck1), (kcv0, kcv1), (kt0, kt1), (vt0, vt1), (wkt0, wkt1)
    n_cmp = past_len // CMP_STRIDE
    w_rows = win_ref.shape[4]
    lq = BF16_ROWS
    rows = HPG * lq
    w_start = past_len - w_rows

    @pl.when(pl.program_id(0) == 0)
    def _():
        sub = lax.broadcasted_iota(I32, (DH, past_len), 0)
        pos = lax.broadcasted_iota(I32, (1, past_len), 1)
        ex = _key_alibi_rows(pos, sub).astype(BF16)
        onehot = (lax.broadcasted_iota(I32, (SEL_PAD, past_len), 0) == jnp.right_shift(pos, 6)).astype(BF16)
        subw = lax.broadcasted_iota(I32, (DH, w_rows), 0)
        exw = _key_alibi_rows(w_start + lax.broadcasted_iota(I32, (1, w_rows), 1), subw).astype(BF16)
        for g in range(KVH):
            kaug_t[g][DH:2 * DH, :] = ex
            kaug_t[g][2 * DH:, :] = onehot
            wk_t[g][DH:, :] = exw

    for p in range(n_pages):
        cols = slice(p * PAGE_SIZE, (p + 1) * PAGE_SIZE)
        full_k[cols, :] = cmp_pages[p][0, 0].reshape(2 * DH, PAGE_SIZE).T
        full_v[cols, :] = cmp_pages[p][0, 1].reshape(2 * DH, PAGE_SIZE).T
        for g in range(KVH):
            kaug_t[g][0:DH, cols] = slc_pages[p][0, 0, g].astype(BF16)
            vt = slc_pages[p][0, 1, g].astype(BF16)
            v_t[g][0:DH, cols] = vt
            v_t[g][DH:, cols] = vt
    for g in range(KVH):
        wk_t[g][0:DH, :] = win_ref[0, 0, g].astype(BF16)

    keys_c, values_c = _compress((full_k, full_v), n_cmp, w1_ref, pe_ref, b1_ref, w2_ref, b2_ref)
    for g in range(KVH):
        kc_k[g][...] = keys_c[g * n_cmp:(g + 1) * n_cmp].astype(BF16)
        kc_v[g][...] = values_c[g * n_cmp:(g + 1) * n_cmp].astype(BF16)

    pad_q = jnp.zeros((lq - ls, NSA_WIDTH), F32)
    q_pos = past_len + lax.broadcasted_iota(I32, (lq, 1), 0)
    q_pos4 = _stack4(q_pos)
    _fill_queries(qa_ref, jnp.concatenate([q_ref[0] * ATTN_SCALE, pad_q], axis=0), q_pos, lq)
    gates = jnp.concatenate([gate_ref[0], jnp.zeros((lq - ls, 3 * NSA_HEADS), F32)], axis=0)

    o_c, psums = _cmp_branch(qa_ref, kc_k, kc_v, q_pos4, lq, n_cmp)
    n_sel = -(-(past_len + lq) // SEL_LEN)
    n_blk = -(-n_sel // 8) * 8
    _store_selbias(qa_ref, _select_blocks(psums, ovt_ref, past_len, lq, n_blk), lq)

    pad_k = jnp.zeros((LANES - ls, KV_WIDTH), F32)
    new_pos_col = past_len + lax.broadcasted_iota(I32, (LANES, 1), 0)
    new_pos = past_len + lax.broadcasted_iota(I32, (1, LANES), 1)
    new_s = jnp.concatenate([kvs_new_ref[0], pad_k], axis=0)
    new_w = jnp.concatenate([kvw_new_ref[0], pad_k], axis=0)
    ks_new = _key_rows(new_s[:, :LANES], new_pos_col)
    vs_new = _halves(new_s[:, LANES:])
    kw_new = _key_rows(new_w[:, :LANES], new_pos_col)
    vw_new = _halves(new_w[:, LANES:])
    onehot_new = _block_onehot(new_pos_col, LANES)
    causal_new = q_pos4 >= new_pos

    o_s, o_w = [], []
    d_past = q_pos4 - (w_start + lax.broadcasted_iota(I32, (1, w_rows), 1))
    m_past = (d_past >= 0) & (d_past < WINDOW)
    d_new = q_pos4 - new_pos
    m_new = (d_new >= 0) & (d_new < WINDOW)
    for g in range(KVH):
        s_past = _dot(qa_ref[g], kaug_t[g][...])
        k_new = jnp.concatenate([ks_new[g].astype(BF16), onehot_new], axis=1)
        s_new = jnp.where(causal_new, _dot_nt(qa_ref[g], k_new), NEG)
        m = jnp.maximum(jnp.max(s_past, axis=-1, keepdims=True), jnp.max(s_new, axis=-1, keepdims=True))
        e_past = jnp.exp(s_past - m)
        e_new = jnp.exp(s_new - m)
        den = jnp.sum(e_past, axis=-1, keepdims=True) + jnp.sum(e_new, axis=-1, keepdims=True)
        acc = _dot_nt(e_past.astype(BF16), v_t[g][...]) + _dot(e_new.astype(BF16), vs_new[g].astype(BF16))
        o_s.append(acc / den)

        sw_past = jnp.where(m_past, _dot(qa_ref[g, :, 0:LANES], wk_t[g][...]), NEG)
        sw_new = jnp.where(m_new, _dot_nt(qa_ref[g, :, 0:LANES], kw_new[g].astype(BF16)), NEG)
        m = jnp.maximum(jnp.max(sw_past, axis=-1, keepdims=True), jnp.max(sw_new, axis=-1, keepdims=True))
        e_past = jnp.exp(sw_past - m)
        e_new = jnp.exp(sw_new - m)
        den = jnp.sum(e_past, axis=-1, keepdims=True) + jnp.sum(e_new, axis=-1, keepdims=True)
        vw = win_ref[0, 1, g].astype(BF16)
        vw2 = jnp.concatenate([vw, vw], axis=0)
        acc = _dot_nt(e_past.astype(BF16), vw2) + _dot(e_new.astype(BF16), vw_new[g].astype(BF16))
        o_w.append(acc / den)

    o_ref[0] = _combine(gates, eg_ref, o_c, o_s, o_w, lq)[:ls]


def _nsa_sample(q, gates, kvs_new, kvw_new, win_t, cmp_t, slc_t, page_table, cw, past_len):
    Bd, ls, _ = q.shape
    n_pages = page_table.shape[1]
    n_cmp = past_len // CMP_STRIDE
    ovt = _overlap_t(n_cmp)
    eg = _gate_expand()
    w_rows = win_t.shape[4]

    per_b = lambda a: pl.BlockSpec((1,) + a.shape[1:], lambda b, pt: (b,) + (0,) * (a.ndim - 1))
    full = lambda a: pl.BlockSpec(a.shape, lambda b, pt: (0,) * a.ndim)
    page = lambda p: pl.BlockSpec((1, 2, KVH, DH, PAGE_SIZE), lambda b, pt, p=p: (pt[b, p], 0, 0, 0, 0))
    grid_spec = pltpu.PrefetchScalarGridSpec(
        num_scalar_prefetch=1,
        grid=(Bd,),
        in_specs=[per_b(q), per_b(gates), per_b(kvs_new), per_b(win_t), per_b(kvw_new)]
                 + [page(p) for p in range(n_pages)] * 2
                 + [full(a) for a in cw] + [full(ovt), full(eg)],
        out_specs=pl.BlockSpec((1, ls, NSA_WIDTH), lambda b, pt: (b, 0, 0)),
        scratch_shapes=[pltpu.VMEM((past_len, LANES), F32)] * 2
                       + [pltpu.VMEM((n_cmp, LANES), BF16)] * 4
                       + [pltpu.VMEM((2 * LANES, past_len), BF16)] * 2
                       + [pltpu.VMEM((LANES, past_len), BF16)] * 2
                       + [pltpu.VMEM((LANES, w_rows), BF16)] * 2
                       + [pltpu.VMEM((KVH, HPG * BF16_ROWS, 2 * LANES), BF16)],
    )
    return pl.pallas_call(
        functools.partial(_nsa_sample_body, n_pages=n_pages, ls=ls, past_len=past_len),
        grid_spec=grid_spec,
        out_shape=jax.ShapeDtypeStruct((Bd, ls, NSA_WIDTH), F32),
        compiler_params=_params(("arbitrary",)),
        name="nsa_sample",
    )(page_table, q, gates, kvs_new, win_t, kvw_new,
      *([cmp_t] * n_pages), *([slc_t] * n_pages), *cw, ovt, eg)


def _fin1_body(h_ref, pool_ref, nsa_ref, wo_ref, g_ref, b_ref, wq_ref, h1_ref, qm_ref):
    mix = (_dot(pool_ref[...].astype(BF16), wo_ref[0:POOL_WIDTH, :])
           + _dot(nsa_ref[...].astype(BF16), wo_ref[POOL_WIDTH:, :]))
    h1 = _layer_norm(DN_ALPHA * h_ref[...] + mix, g_ref[...], b_ref[...])
    h1_ref[...] = h1
    qm_ref[...] = _dot(h1.astype(BF16), wq_ref[...])


def _fin1(h, pool_o, nsa_o, w_out_bf, g, b, wq_bf):
    T = h.shape[0]
    tm = ROW_TILE
    row = lambda n: pl.BlockSpec((tm, n), lambda i: (i, 0))
    full = lambda a: pl.BlockSpec(a.shape, lambda i: (0,) * a.ndim)
    return pl.pallas_call(
        _fin1_body,
        grid=(T // tm,),
        in_specs=[row(D_MODEL), row(POOL_WIDTH), row(NSA_WIDTH), full(w_out_bf), full(g), full(b), full(wq_bf)],
        out_specs=[row(D_MODEL), row(D_MODEL)],
        out_shape=[jax.ShapeDtypeStruct((T, D_MODEL), F32)] * 2,
        compiler_params=_params(("arbitrary",)),
        name="out_proj_ln1",
    )(h, pool_o, nsa_o, w_out_bf, g, b, wq_bf)


def _memattn_body(q_ref, kv_ref, o_ref):
    width = MEM_HEADS * MEM_HEAD_DIM
    for h in range(MEM_HEADS):
        cols = slice(h * MEM_HEAD_DIM, (h + 1) * MEM_HEAD_DIM)
        qh = (q_ref[0, :, cols] * (MEM_HEAD_DIM ** -0.5)).astype(BF16)
        kh = kv_ref[0, :, cols].astype(BF16)
        vh = kv_ref[0, :, width + h * MEM_HEAD_DIM:width + (h + 1) * MEM_HEAD_DIM].astype(BF16)
        s = _dot_nt(qh, kh)
        e = jnp.exp(s - jnp.max(s, axis=-1, keepdims=True))
        o_ref[0, :, cols] = _dot(e.astype(BF16), vh) * (1.0 / jnp.sum(e, axis=-1, keepdims=True))


def _memattn(qm, mem_kv, tq):
    nb, L, W = qm.shape
    return pl.pallas_call(
        _memattn_body,
        grid=(nb, L // tq),
        in_specs=[pl.BlockSpec((1, tq, W), lambda b, t: (b, t, 0)),
                  pl.BlockSpec((1, MEM_LEN, 2 * W), lambda b, t: (b, 0, 0))],
        out_specs=pl.BlockSpec((1, tq, W), lambda b, t: (b, t, 0)),
        out_shape=jax.ShapeDtypeStruct((nb, L, W), F32),
        compiler_params=_params(("arbitrary", "arbitrary")),
        name="mem_attn",
    )(qm, mem_kv)


def _memattn_few_body(q_ref, kv_ref, o_ref):
    lq = q_ref.shape[1]
    n_keys = MEM_LEN * MEM_HEADS
    q = q_ref[0] * (MEM_HEAD_DIM ** -0.5)
    qs = jnp.concatenate([q[:, h * MEM_HEAD_DIM:(h + 1) * MEM_HEAD_DIM] for h in range(MEM_HEADS)], axis=0)
    k = kv_ref[0, :, 0, :, :].reshape(n_keys, MEM_HEAD_DIM).astype(BF16)
    v = kv_ref[0, :, 1, :, :].reshape(n_keys, MEM_HEAD_DIM).astype(BF16)
    s = _dot_nt(qs.astype(BF16), k)
    assert lq & (lq - 1) == 0 and MEM_HEADS & (MEM_HEADS - 1) == 0
    col_h = jnp.bitwise_and(lax.broadcasted_iota(I32, s.shape, 1), MEM_HEADS - 1)
    row_h = jnp.right_shift(lax.broadcasted_iota(I32, s.shape, 0), lq.bit_length() - 1)
    s = jnp.where(col_h == row_h, s, NEG)
    e = jnp.exp(s - jnp.max(s, axis=-1, keepdims=True))
    o = _dot(e.astype(BF16), v) * (1.0 / jnp.sum(e, axis=-1, keepdims=True))
    for h in range(MEM_HEADS):
        o_ref[0, :, h * MEM_HEAD_DIM:(h + 1) * MEM_HEAD_DIM] = o[h * lq:(h + 1) * lq]


def _memattn_few(qm, mem_kv):
    nb, lq, W = qm.shape
    return pl.pallas_call(
        _memattn_few_body,
        grid=(nb,),
        in_specs=[pl.BlockSpec((1, lq, W), lambda b: (b, 0, 0)),
                  pl.BlockSpec((1, MEM_LEN, 2, MEM_HEADS, MEM_HEAD_DIM), lambda b: (b, 0, 0, 0, 0))],
        out_specs=pl.BlockSpec((1, lq, W), lambda b: (b, 0, 0)),
        out_shape=jax.ShapeDtypeStruct((nb, lq, W), F32),
        compiler_params=_params(("arbitrary",)),
        name="mem_attn_few",
    )(qm, mem_kv)


def _fin2_body(cnt0_ref, h1_ref, o_ref, wo_ref, g_ref, b_ref, rw_ref, rb_ref, *rest):
    h2_ref, te_ref, tg_ref, cnt_ref, run_ref = rest[-5:]
    tm = h1_ref.shape[0]

    @pl.when(pl.program_id(0) == 0)
    def _():
        run_ref[...] = cnt0_ref[...]

    a = _dot(o_ref[...].astype(BF16), wo_ref[...])
    h2 = _layer_norm(DN_ALPHA * h1_ref[...] + a, g_ref[...], b_ref[...])
    h2_ref[...] = h2
    logits = jnp.dot(h2, rw_ref[...], preferred_element_type=F32, precision=HIGHEST) + rb_ref[...]
    e_iota = lax.broadcasted_iota(I32, (tm, N_EXPERTS), 1).astype(F32)
    lane = lax.broadcasted_iota(I32, (tm, LANES), 1)
    te = jnp.zeros((tm, LANES), F32)
    tv = jnp.full((tm, LANES), NEG, F32)
    work = logits
    chosen = []
    for k in range(TOP_K):
        m = jnp.max(work, axis=-1, keepdims=True)
        idx = jnp.min(jnp.where(work == m, e_iota, float(N_EXPERTS)), axis=-1, keepdims=True)
        hit = e_iota == idx
        chosen.append(hit)
        te = jnp.where(lane == k, idx, te)
        tv = jnp.where(lane == k, m, tv)
        work = jnp.where(hit, -jnp.inf, work)
    member = sum(c.astype(F32) for c in chosen)
    earlier = (lax.broadcasted_iota(I32, (tm, tm), 0) > lax.broadcasted_iota(I32, (tm, tm), 1)).astype(BF16)
    before = _dot(earlier, member.astype(BF16)) + run_ref[...]
    for k in range(TOP_K):
        rank = jnp.sum(jnp.where(chosen[k], before, 0.0), axis=-1, keepdims=True)
        te = jnp.where(lane == TOP_K + k, rank, te)
    run_ref[...] = run_ref[...] + jnp.sum(member, axis=0, keepdims=True)
    cnt_ref[...] = run_ref[...]
    ex = jnp.exp(tv - jnp.max(tv, axis=-1, keepdims=True))
    te_ref[...] = te.astype(I32)
    tg_ref[...] = ex / jnp.sum(ex, axis=-1, keepdims=True)


def _fin2(cnt0, h1, o, wo_bf, g, b, rw, rb, total_rows, row_offset=0, into=None):
    T = h1.shape[0]
    tm = ROW_TILE
    blk0 = row_offset // tm
    row = lambda n: pl.BlockSpec((tm, n), lambda i: (i, 0))
    out_row = lambda n: pl.BlockSpec((tm, n), lambda i: (i + blk0, 0))
    full = lambda a: pl.BlockSpec(a.shape, lambda i: (0,) * a.ndim)
    ins = [cnt0, h1, o, wo_bf, g, b, rw, rb]
    in_specs = [full(cnt0), row(D_MODEL), row(D_MODEL), full(wo_bf), full(g), full(b), full(rw), full(rb)]
    aliases = {}
    if into is not None:
        aliases = {len(ins) + k: k for k in range(len(into))}
        in_specs = in_specs + [pl.BlockSpec(memory_space=pl.ANY)] * len(into)
        ins = ins + list(into)
    return pl.pallas_call(
        _fin2_body,
        grid=(T // tm,),
        in_specs=in_specs,
        out_specs=[out_row(D_MODEL), out_row(LANES), out_row(LANES), full(cnt0)],
        out_shape=[jax.ShapeDtypeStruct((total_rows, D_MODEL), F32), jax.ShapeDtypeStruct((total_rows, LANES), I32),
                   jax.ShapeDtypeStruct((total_rows, LANES), F32), jax.ShapeDtypeStruct(cnt0.shape, F32)],
        scratch_shapes=[pltpu.VMEM(cnt0.shape, F32)],
        input_output_aliases=aliases,
        compiler_params=_params(("arbitrary",)),
        name="mem_out_ln2_router",
    )(*ins)


def _moe_body(ut_ref, ue_ref, nu_ref, rs_ref, re_ref, x_ref, rw_ref, wgu_ref, bgu_ref, wdn_ref, bdn_ref,
              y_ref, wgu_bf, wdn_bf):
    u = pl.program_id(0)
    bk = x_ref.shape[0]
    e = ue_ref[u]
    tile = ut_ref[u]
    prev = jnp.maximum(u - 1, 0)

    @pl.when((u == 0) | (e != ue_ref[prev]))
    def _():
        wgu_bf[...] = wgu_ref[0].astype(BF16)
        wdn_bf[...] = wdn_ref[0].astype(BF16)

    @pl.when(u < nu_ref[0])
    def _():
        x = x_ref[...].astype(BF16)
        g = _dot(x, wgu_bf[:, :D_FF]) + bgu_ref[0, :, :D_FF]
        v = _dot(x, wgu_bf[:, D_FF:]) + bgu_ref[0, :, D_FF:]
        g = jnp.minimum(g, SWIGLU_LIMIT)
        v = jnp.clip(v, -SWIGLU_LIMIT, SWIGLU_LIMIT)
        a = g * (1.0 / (1.0 + jnp.exp(-SWIGLU_ALPHA * g))) * (v + 1.0)
        y = _dot(a.astype(BF16), wdn_bf[...]) + bdn_ref[0]
        row = tile * bk + lax.broadcasted_iota(I32, (bk, 1), 0)
        mine = (row >= rs_ref[e]) & (row < re_ref[e])
        y = jnp.where(mine, y * rw_ref[...], 0.0)

        @pl.when((u == 0) | (tile != ut_ref[prev]))
        def _():
            y_ref[...] = y

        @pl.when((u > 0) & (tile == ut_ref[prev]))
        def _():
            y_ref[...] = y_ref[...] + y


def _moe_gmm(x_rows, row_w, units, w_gu, b_gu, w_dn, b_dn):
    N = x_rows.shape[0]
    bk = MOE_ROWS
    unit_tile, unit_e, n_units, r_start, r_end = units
    grid_spec = pltpu.PrefetchScalarGridSpec(
        num_scalar_prefetch=5,
        grid=(unit_tile.shape[0],),
        in_specs=[pl.BlockSpec((bk, D_MODEL), lambda u, ut, ue, *_: (ut[u], 0)),
                  pl.BlockSpec((bk, 1), lambda u, ut, ue, *_: (ut[u], 0)),
                  pl.BlockSpec((1, D_MODEL, 2 * D_FF), lambda u, ut, ue, *_: (ue[u], 0, 0)),
                  pl.BlockSpec((1, 1, 2 * D_FF), lambda u, ut, ue, *_: (ue[u], 0, 0)),
                  pl.BlockSpec((1, D_FF, D_MODEL), lambda u, ut, ue, *_: (ue[u], 0, 0)),
                  pl.BlockSpec((1, 1, D_MODEL), lambda u, ut, ue, *_: (ue[u], 0, 0))],
        out_specs=pl.BlockSpec((bk, D_MODEL), lambda u, ut, ue, *_: (ut[u], 0)),
        scratch_shapes=[pltpu.VMEM((D_MODEL, 2 * D_FF), BF16), pltpu.VMEM((D_FF, D_MODEL), BF16)],
    )
    return pl.pallas_call(
        _moe_body,
        grid_spec=grid_spec,
        out_shape=jax.ShapeDtypeStruct((N, D_MODEL), F32),
        compiler_params=_params(("arbitrary",)),
        name="moe_experts",
    )(unit_tile, unit_e, n_units, r_start, r_end, x_rows, row_w, w_gu, b_gu, w_dn, b_dn)


FLAT_BITS = 17


def _moe_routing(te, tg, counts):
    bk = MOE_ROWS
    T = te.shape[0]
    N = T * TOP_K
    assert N % bk == 0 and N <= (1 << FLAT_BITS)
    experts = jnp.arange(N_EXPERTS, dtype=I32)
    top_e = te[:, :TOP_K]
    r_end = jnp.cumsum(counts).astype(I32)
    r_start = r_end - counts
    onehot = top_e[:, :, None] == experts[None, None, :]
    pos = jnp.sum(jnp.where(onehot, r_start[None, None, :], 0), axis=-1) + te[:, TOP_K:2 * TOP_K]
    key = jnp.left_shift(top_e.reshape(-1), FLAT_BITS) + jnp.arange(N, dtype=I32)
    key_s, gate_s = lax.sort((key, tg[:, :TOP_K].reshape(-1)), num_keys=1)
    tok_s = jnp.right_shift(jnp.bitwise_and(key_s, (1 << FLAT_BITS) - 1), 2)
    first = r_start // bk
    last = (r_end - 1) // bk
    n_e = jnp.where(counts > 0, last - first + 1, 0)
    u_end = jnp.cumsum(n_e).astype(I32)
    u_start = u_end - n_e
    n_units = u_end[-1]
    u = jnp.minimum(jnp.arange(N // bk + N_EXPERTS - 1, dtype=I32), n_units - 1)
    unit_e = jnp.sum((u[:, None] >= u_end[None, :]).astype(I32), axis=1)
    unit_tile = u + jnp.sum(jnp.where(unit_e[:, None] == experts[None, :], (first - u_start)[None, :], 0), axis=1)
    return pos, tok_s, gate_s, (unit_tile, unit_e, n_units.reshape(1), r_start, r_end)


def _fin3_body(h2_ref, y0_ref, y1_ref, y2_ref, y3_ref, g_ref, b_ref, o_ref):
    y = (y0_ref[...] + y1_ref[...]) + (y2_ref[...] + y3_ref[...])
    o_ref[...] = _layer_norm(DN_ALPHA * h2_ref[...] + y, g_ref[...], b_ref[...])


def _fin3(h2, ys, g, b, row_offset):
    T = ys[0].shape[0]
    tm = ROW_TILE
    blk0 = row_offset // tm
    row = lambda n: pl.BlockSpec((tm, n), lambda i: (i, 0))
    full = lambda a: pl.BlockSpec(a.shape, lambda i: (0,) * a.ndim)
    return pl.pallas_call(
        _fin3_body,
        grid=(T // tm,),
        in_specs=[pl.BlockSpec((tm, D_MODEL), lambda i: (i + blk0, 0))] + [row(D_MODEL)] * TOP_K
                 + [full(g), full(b)],
        out_specs=row(D_MODEL),
        out_shape=jax.ShapeDtypeStruct((T, D_MODEL), F32),
        compiler_params=_params(("arbitrary",)),
        name="combine_ln3",
    )(h2, *ys, g, b)


def kernel(x_prompt, x_sample, cache_cmp_kv, cache_slc_kv, state_win_kv, state_pool, cache_mem_kv, page_table,
           mem_prompt, w_in, pool_w, pool_scale, cmp_pe, cmp_w1, cmp_b1, cmp_w2, cmp_b2, w_out, ln1_g, ln1_b,
           mem_wq, mem_wkv, mem_wo, ln2_g, ln2_b, router_w, router_b, exp_w_gu, exp_b_gu, exp_w_dn, exp_b_dn,
           ln3_g, ln3_b):
    Bp, S, D = x_prompt.shape
    Bd, Ls, _ = x_sample.shape
    Tp, Ts = Bp * S, Bd * Ls
    l = 0
    w_in_bf = w_in[l].astype(BF16)
    pool_w_bf = pool_w[l].astype(BF16)
    ps = pool_scale[l][None, :]
    cw = _compress_weights(cmp_pe[l], cmp_w1[l], cmp_b1[l], cmp_w2[l], cmp_b2[l])
    w_out_bf = w_out[l].astype(BF16)
    wq_bf = mem_wq[l].astype(BF16)
    wo_bf = mem_wo[l].astype(BF16)
    vec = lambda a: a[l][None, :]

    up, qp, kvc_p, kvs_p, kvw_p, gp, pool_p, kvc_t, kvs_t = _inproj_prompt(
        x_prompt.reshape(Tp, D), w_in_bf, pool_w_bf, ps, S)
    kc_p = _compress_prompt(kvc_p.reshape(Bp, S, KV_WIDTH), cw)
    nsa_p = _nsa_prompt(qp, gp, kc_p, kvs_p.reshape(Bp, S, KV_WIDTH), kvw_p.reshape(Bp, S, KV_WIDTH))
    mem_kv_p = _matmul(mem_prompt.reshape(Bp * MEM_LEN, D), mem_wkv[l]).reshape(Bp, MEM_LEN, 2 * D)
    h1_p, qm_p = _fin1(x_prompt.reshape(Tp, D), pool_p, nsa_p, w_out_bf, vec(ln1_g), vec(ln1_b), wq_bf)
    om_p = _memattn(qm_p.reshape(Bp, S, D), mem_kv_p, ROW_TILE).reshape(Tp, D)
    T = Tp + Ts
    *routed_p, cnt_p = _fin2(jnp.zeros((1, N_EXPERTS), F32), h1_p, om_p, wo_bf, vec(ln2_g), vec(ln2_b),
                             router_w[l], vec(router_b), total_rows=T)

    state_pad = jnp.pad(state_pool[l], ((0, 0), (1, 0), (0, 0)))
    us, qs, kvc_s, kvs_s, kvw_s, gs, pool_s = _inproj_sample(
        x_sample.reshape(Ts, D), w_in_bf, pool_w_bf, ps, state_pad, Ls, PAST_LEN)
    w_rows = state_win_kv.shape[2]
    feature_major = lambda a: jnp.transpose(a, (0, 2, 3, 4, 1))
    nsa_s = _nsa_sample(qs.reshape(Bd, Ls, NSA_WIDTH), gs.reshape(Bd, Ls, 3 * NSA_HEADS),
                        kvs_s.reshape(Bd, Ls, KV_WIDTH), kvw_s.reshape(Bd, Ls, KV_WIDTH),
                        feature_major(state_win_kv[l]), feature_major(cache_cmp_kv[l]),
                        feature_major(cache_slc_kv[l]), page_table, cw, PAST_LEN)
    h1_s, qm_s = _fin1(x_sample.reshape(Ts, D), pool_s, nsa_s.reshape(Ts, NSA_WIDTH), w_out_bf,
                       vec(ln1_g), vec(ln1_b), wq_bf)
    om_s = _memattn_few(qm_s.reshape(Bd, Ls, D), cache_mem_kv[l]).reshape(Ts, D)
    h2, te, tg, cnt_s = _fin2(cnt_p, h1_s, om_s, wo_bf, vec(ln2_g), vec(ln2_b), router_w[l], vec(router_b),
                              total_rows=T, row_offset=Tp, into=routed_p)

    pos, tok_s, gate_s, units = _moe_routing(te, tg, cnt_s[0].astype(I32))
    y_rows = _moe_gmm(h2[tok_s], gate_s[:, None], units, exp_w_gu[l], exp_b_gu[l][:, None, :],
                      exp_w_dn[l], exp_b_dn[l][:, None, :])
    gathered = lambda lo, hi: [y_rows[pos[lo:hi, k]] for k in range(TOP_K)]
    y_prompt = _fin3(h2, gathered(0, Tp), vec(ln3_g), vec(ln3_b), row_offset=0).reshape(Bp, S, D)
    y_sample = _fin3(h2, gathered(Tp, T), vec(ln3_g), vec(ln3_b), row_offset=Tp).reshape(Bd, Ls, D)

    kv6 = lambda a, b, n: a.reshape(1, b, n, 2, KVH, DH)
    row_major = lambda a: jnp.transpose(a, (0, 4, 1, 2, 3))
    win_p = kvw_p.reshape(Bp, S, KV_WIDTH)[:, S - min(WINDOW, S):]
    win_s = jnp.concatenate([state_win_kv[l], kvw_s.reshape(Bd, Ls, 2, KVH, DH)], axis=1)[:, -w_rows:]
    pool_state_p = up.reshape(Bp, S, POOL_WIDTH)[:, S - POOL_STATE:]
    pool_state_s = jnp.concatenate([state_pool[l], us.reshape(Bd, Ls, POOL_WIDTH)], axis=1)[:, -POOL_STATE:]
    return (y_prompt, y_sample,
            row_major(kvc_t)[None], row_major(kvs_t)[None], kv6(win_p, Bp, min(WINDOW, S)),
            pool_state_p[None], mem_kv_p.reshape(1, Bp, MEM_LEN, 2, MEM_HEADS, MEM_HEAD_DIM),
            kv6(kvc_s, Bd, Ls), kv6(kvs_s, Bd, Ls), win_s[None], pool_state_s[None])
```

```python
import functools

import jax
import jax.numpy as jnp
from jax import lax
from jax.experimental import pallas as pl
from jax.experimental.pallas import tpu as pltpu

F32 = jnp.float32
BF16 = jnp.bfloat16
I32 = jnp.int32

D_MODEL = 1024
POOL_WIDTH = 512
POOL_WINDOWS = (2, 4, 8, 16)
POOL_GROUP = 128
POOL_STATE = 15
NSA_WIDTH = 512
DH = 64
NSA_HEADS = 8
KVH = 2
HPG = 4
CMP_LEN = 32
CMP_STRIDE = 16
CMP_HIDDEN = 256
SEL_LEN = 64
SEL_TOP = 16
WINDOW = 512
Q_BLOCK = 128
KV_WIDTH = 256
IN_WIDTH = 1816
ATTN_SCALE = DH ** -0.5
FORCED_SCORE = 1e4
NEG = -1e30
MEM_LEN = 256
MEM_HEADS = 4
MEM_HEAD_DIM = 256
N_EXPERTS = 32
TOP_K = 4
D_FF = 1024
SWIGLU_LIMIT = 7.0
SWIGLU_ALPHA = 1.702
DN_ALPHA = 2.0 ** 0.25
LN_EPS = 1e-5
PAST_LEN = 2048
PAGE_SIZE = 128

LANES = 128
SEL_PAD = 128
KEY_TILE = 512
ROW_TILE = 512
MOE_ROWS = 512
BF16_ROWS = 16
VMEM_LIMIT = 56 * 1024 * 1024

HIGHEST = lax.Precision.HIGHEST


def _dot(a, b):
    return jnp.dot(a, b, preferred_element_type=F32)


def _dot_nt(a, b, precision=None):
    return lax.dot_general(a, b, (((1,), (1,)), ((), ())), preferred_element_type=F32,
                           precision=precision)


def _layer_norm(x, g, b):
    mu = jnp.mean(x, axis=-1, keepdims=True)
    xc = x - mu
    var = jnp.mean(xc * xc, axis=-1, keepdims=True)
    return xc * lax.rsqrt(var + LN_EPS) * g + b


def _params(sem, vmem=VMEM_LIMIT):
    return pltpu.CompilerParams(dimension_semantics=sem, vmem_limit_bytes=vmem)


def _split_store(u, up_ref, q_ref, kvc_ref, kvs_ref, kvw_ref, gate_ref):
    o1 = POOL_WIDTH
    o2 = o1 + NSA_WIDTH
    o3 = o2 + KV_WIDTH
    o4 = o3 + KV_WIDTH
    o5 = o4 + KV_WIDTH
    up_ref[...] = u[:, :o1]
    q_ref[...] = u[:, o1:o2]
    kvc_ref[...] = u[:, o2:o3]
    kvs_ref[...] = u[:, o3:o4]
    kvw_ref[...] = u[:, o4:o5]
    gate_ref[...] = 1.0 / (1.0 + jnp.exp(-u[:, o5:]))


def _inproj_prompt_body(x_ref, w_ref, pw_ref, ps_ref,
                        up_ref, q_ref, kvc_ref, kvs_ref, kvw_ref, gate_ref, pool_ref, kvct_ref, kvst_ref,
                        ext_ref, *, tm, tiles_per_seq):
    halo = POOL_STATE + 1
    t_in_seq = pl.program_id(0) % tiles_per_seq
    u = _dot(x_ref[...].astype(BF16), w_ref[...])
    _split_store(u, up_ref, q_ref, kvc_ref, kvs_ref, kvw_ref, gate_ref)
    o2 = POOL_WIDTH + NSA_WIDTH
    kvct_ref[0] = u[:, o2:o2 + KV_WIDTH].T.reshape(2, KVH, DH, tm)
    kvst_ref[0] = u[:, o2 + KV_WIDTH:o2 + 2 * KV_WIDTH].T.reshape(2, KVH, DH, tm)

    @pl.when(t_in_seq == 0)
    def _():
        ext_ref[0:halo, :] = jnp.zeros((halo, POOL_WIDTH), F32)

    ext_ref[halo:halo + tm, :] = u[:, :POOL_WIDTH]
    pos = t_in_seq * tm + lax.broadcasted_iota(I32, (tm, 1), 0)
    for gi, w in enumerate(POOL_WINDOWS):
        cols = slice(gi * POOL_GROUP, (gi + 1) * POOL_GROUP)
        acc = ext_ref[halo:halo + tm, cols]
        for k in range(1, w):
            acc = acc + ext_ref[halo - k:halo - k + tm, cols]
        cnt = jnp.minimum(pos + 1, w).astype(F32)
        d = acc / cnt - ext_ref[halo:halo + tm, cols]
        o = _dot(d.astype(BF16), pw_ref[gi])
        pool_ref[:, cols] = (o * ps_ref[:, cols]).astype(pool_ref.dtype)
    ext_ref[0:halo, :] = ext_ref[tm:tm + halo, :]


def _inproj_prompt(x2d, w_in_bf, pool_w_bf, pool_scale, seq_len):
    T = x2d.shape[0]
    tm = ROW_TILE
    outs = [POOL_WIDTH, NSA_WIDTH, KV_WIDTH, KV_WIDTH, KV_WIDTH, 3 * NSA_HEADS, POOL_WIDTH]
    row = lambda n: pl.BlockSpec((tm, n), lambda i: (i, 0))
    full = lambda a: pl.BlockSpec(a.shape, lambda i: (0,) * a.ndim)
    tps = seq_len // tm
    kvt_spec = pl.BlockSpec((1, 2, KVH, DH, tm), lambda i: (i // tps, 0, 0, 0, i % tps))
    kvt_shape = jax.ShapeDtypeStruct((T // seq_len, 2, KVH, DH, seq_len), F32)
    return pl.pallas_call(
        functools.partial(_inproj_prompt_body, tm=tm, tiles_per_seq=tps),
        grid=(T // tm,),
        in_specs=[row(D_MODEL), full(w_in_bf), full(pool_w_bf), full(pool_scale)],
        out_specs=[row(n) for n in outs] + [kvt_spec] * 2,
        out_shape=[jax.ShapeDtypeStruct((T, n), F32) for n in outs[:-1]]
                  + [jax.ShapeDtypeStruct((T, outs[-1]), BF16)] + [kvt_shape] * 2,
        scratch_shapes=[pltpu.VMEM((tm + POOL_STATE + 1, POOL_WIDTH), F32)],
        compiler_params=_params(("arbitrary",)),
        name="inproj_prompt",
    )(x2d, w_in_bf, pool_w_bf, pool_scale)


def _inproj_sample_body(x_ref, w_ref, pw_ref, ps_ref, st_ref,
                        up_ref, q_ref, kvc_ref, kvs_ref, kvw_ref, gate_ref, pool_ref,
                        ext_ref, *, nb, ls, pos0):
    halo = POOL_STATE + 1
    tm = nb * ls
    u = _dot(x_ref[...].astype(BF16), w_ref[...])
    _split_store(u, up_ref, q_ref, kvc_ref, kvs_ref, kvw_ref, gate_ref)
    ext_ref[:, 0:halo, :] = st_ref[...]
    ext_ref[:, halo:halo + ls, :] = u[:, :POOL_WIDTH].reshape(nb, ls, POOL_WIDTH)
    pos = pos0 + lax.broadcasted_iota(I32, (1, ls, 1), 1)
    for gi, w in enumerate(POOL_WINDOWS):
        cols = slice(gi * POOL_GROUP, (gi + 1) * POOL_GROUP)
        acc = ext_ref[:, halo:halo + ls, cols]
        for k in range(1, w):
            acc = acc + ext_ref[:, halo - k:halo - k + ls, cols]
        cnt = jnp.minimum(pos + 1, w).astype(F32)
        d = acc / cnt - ext_ref[:, halo:halo + ls, cols]
        o = _dot(d.reshape(tm, POOL_GROUP).astype(BF16), pw_ref[gi])
        pool_ref[:, cols] = (o * ps_ref[:, cols]).astype(pool_ref.dtype)


def _inproj_sample(x2d, w_in_bf, pool_w_bf, pool_scale, state_pad, ls, pos0):
    T = x2d.shape[0]
    nb = ROW_TILE // ls
    tm = nb * ls
    outs = [POOL_WIDTH, NSA_WIDTH, KV_WIDTH, KV_WIDTH, KV_WIDTH, 3 * NSA_HEADS, POOL_WIDTH]
    row = lambda n: pl.BlockSpec((tm, n), lambda i: (i, 0))
    full = lambda a: pl.BlockSpec(a.shape, lambda i: (0,) * a.ndim)
    return pl.pallas_call(
        functools.partial(_inproj_sample_body, nb=nb, ls=ls, pos0=pos0),
        grid=(T // tm,),
        in_specs=[row(D_MODEL), full(w_in_bf), full(pool_w_bf), full(pool_scale),
                  pl.BlockSpec((nb, POOL_STATE + 1, POOL_WIDTH), lambda i: (i, 0, 0))],
        out_specs=[row(n) for n in outs],
        out_shape=[jax.ShapeDtypeStruct((T, n), F32) for n in outs],
        scratch_shapes=[pltpu.VMEM((nb, POOL_STATE + 1 + ls, POOL_WIDTH), F32)],
        compiler_params=_params(("arbitrary",)),
        name="inproj_sample",
    )(x2d, w_in_bf, pool_w_bf, pool_scale, state_pad)


def _matmul_body(x_ref, w_ref, o_ref):
    o_ref[...] = _dot(x_ref[...].astype(BF16), w_ref[...].astype(BF16))


def _matmul(x, w, tn=512):
    M, K = x.shape
    N = w.shape[1]
    return pl.pallas_call(
        _matmul_body,
        grid=(N // tn,),
        in_specs=[pl.BlockSpec((M, K), lambda j: (0, 0)), pl.BlockSpec((K, tn), lambda j: (0, j))],
        out_specs=pl.BlockSpec((M, tn), lambda j: (0, j)),
        out_shape=jax.ShapeDtypeStruct((M, N), F32),
        compiler_params=_params(("arbitrary",)),
        name="mem_kv_proj",
    )(x, w)


def _gelu_tanh(x):
    c = 0.7978845608028654
    return 0.5 * x * (1.0 + jnp.tanh(c * (x + 0.044715 * (x * x * x))))


def _compress(kv_refs, n_chunks, w1_ref, pe_ref, b1_ref, w2_ref, b2_ref):
    lo = _lane_iota(n_chunks) < DH
    quads = CMP_STRIDE // 4
    outs = []
    for c in range(2):
        acc_a = jnp.zeros((2 * n_chunks, CMP_HIDDEN), F32)
        acc_b = jnp.zeros((2 * n_chunks, CMP_HIDDEN), F32)
        for i in range(quads):
            x = [kv_refs[c][pl.ds(4 * i + m, n_chunks, stride=CMP_STRIDE), :] for m in range(4)]
            r = [pltpu.roll(v, DH, 1) for v in x]
            x_g0 = jnp.concatenate([jnp.where(lo, x[0], r[1]), jnp.where(lo, x[2], r[3])], axis=1)
            x_g1 = jnp.concatenate([jnp.where(lo, r[0], x[1]), jnp.where(lo, r[2], x[3])], axis=1)
            xq = jnp.concatenate([x_g0, x_g1], axis=0)
            acc_a = acc_a + _dot((xq + pe_ref[c, i:i + 1, :]).astype(BF16), w1_ref[c, i])
            acc_b = acc_b + _dot((xq + pe_ref[c, quads + i:quads + i + 1, :]).astype(BF16), w1_ref[c, quads + i])
        hid = acc_a + pltpu.roll(acc_b, 2 * n_chunks - 1, 0) + b1_ref[c]
        outs.append(_dot(_gelu_tanh(hid).astype(BF16), w2_ref[c]) + b2_ref[c])
    return outs


def _compress_prompt_body(kv_ref, w1_ref, pe_ref, b1_ref, w2_ref, b2_ref, o_ref, k_ref, v_ref, *, n_chunks):
    k_ref[...] = kv_ref[0, :, :LANES]
    v_ref[...] = kv_ref[0, :, LANES:]
    keys, values = _compress((k_ref, v_ref), n_chunks, w1_ref, pe_ref, b1_ref, w2_ref, b2_ref)
    o_ref[0, 0] = keys
    o_ref[0, 1] = values


def _compress_prompt(kvc, cw):
    B, S, _ = kvc.shape
    n_chunks = S // CMP_STRIDE
    full = lambda a: pl.BlockSpec(a.shape, lambda b: (0,) * a.ndim)
    return pl.pallas_call(
        functools.partial(_compress_prompt_body, n_chunks=n_chunks),
        grid=(B,),
        in_specs=[pl.BlockSpec((1, S, KV_WIDTH), lambda b: (b, 0, 0))] + [full(a) for a in cw],
        out_specs=pl.BlockSpec((1, 2, KVH * n_chunks, LANES), lambda b: (b, 0, 0, 0)),
        out_shape=jax.ShapeDtypeStruct((B, 2, KVH * n_chunks, LANES), F32),
        scratch_shapes=[pltpu.VMEM((S, LANES), F32)] * 2,
        compiler_params=_params(("arbitrary",)),
        name="compress_prompt",
    )(kvc, *cw)


def _compress_weights(cmp_pe, cmp_w1, cmp_b1, cmp_w2, cmp_b2):
    nq = CMP_LEN // 4
    w1 = cmp_w1.reshape(2, nq, 4 * DH, CMP_HIDDEN).astype(BF16)
    pe = cmp_pe.reshape(2, nq, 4 * DH)
    b1 = cmp_b1[:, None, :]
    w2 = jnp.stack([jnp.concatenate([cmp_w2[0], jnp.zeros_like(cmp_w2[0])], axis=1),
                    jnp.concatenate([cmp_w2[1], cmp_w2[1]], axis=1)]).astype(BF16)
    b2 = jnp.stack([jnp.concatenate([cmp_b2[0], jnp.zeros_like(cmp_b2[0])]),
                    jnp.concatenate([cmp_b2[1], cmp_b2[1]])])[:, None, :]
    return w1, pe, b1, w2, b2


def _softmax_rows(s):
    e = jnp.exp(s - jnp.max(s, axis=-1, keepdims=True))
    return e * (1.0 / jnp.sum(e, axis=-1, keepdims=True))


def _lane_iota(n):
    return lax.broadcasted_iota(I32, (n, LANES), 1)


def _key_alibi_cols(pos, lane):
    hi = jnp.left_shift(jnp.right_shift(pos, 6), 6).astype(F32)
    lo = jnp.bitwise_and(pos, SEL_LEN - 1).astype(F32)
    return jnp.where(lane == DH, hi,
                     jnp.where(lane == DH + 1, lo,
                               jnp.where((lane == DH + 2) | (lane == DH + 3), 1.0, 0.0)))


def _key_alibi_rows(pos, sub):
    hi = jnp.left_shift(jnp.right_shift(pos, 6), 6).astype(F32)
    lo = jnp.bitwise_and(pos, SEL_LEN - 1).astype(F32)
    return jnp.where(sub == 0, hi, jnp.where(sub == 1, lo, jnp.where((sub == 2) | (sub == 3), 1.0, 0.0)))


def _halves(x, zero_hi=False):
    lo = _lane_iota(x.shape[0]) < DH
    r = pltpu.roll(x, DH, 1)
    if zero_hi:
        return jnp.where(lo, x, 0.0), jnp.where(lo, r, 0.0)
    return jnp.where(lo, x, r), jnp.where(lo, r, x)


def _fill_queries(qa_ref, qs, q_pos, lq):
    lane = _lane_iota(lq)
    lo_half = lane < DH
    q_hi = jnp.left_shift(jnp.right_shift(q_pos, 7), 7).astype(F32)
    q_lo = jnp.bitwise_and(q_pos, LANES - 1).astype(F32)
    for h in range(NSA_HEADS):
        g, hl = divmod(h, HPG)
        slope = 2.0 ** (-(h + 1))
        slab = qs[:, (h // 2) * LANES:(h // 2 + 1) * LANES]
        if h % 2:
            slab = pltpu.roll(slab, DH, 1)
        ex = jnp.where((lane == DH) | (lane == DH + 1), slope,
                       jnp.where(lane == DH + 2, -slope * q_hi,
                                 jnp.where(lane == DH + 3, -slope * q_lo,
                                           jnp.where(lane == DH + 4, NEG, 0.0))))
        qa_ref[g, hl * lq:(hl + 1) * lq, 0:LANES] = jnp.where(lo_half, slab, ex).astype(BF16)


def _stack4(x):
    return jnp.concatenate([x] * HPG, axis=0)


def _cmp_branch(qa_ref, kc_k, kc_v, q_pos4, lq, n_cmp):
    c_end = lax.broadcasted_iota(I32, (1, n_cmp), 1) * CMP_STRIDE + (CMP_LEN - 1)
    m_c = c_end <= q_pos4
    any_c = (q_pos4 >= CMP_LEN - 1).astype(F32)
    outs, psums = [], []
    for g in range(KVH):
        s = jnp.where(m_c, _dot_nt(qa_ref[g, :, 0:LANES], kc_k[g][...]), NEG)
        p = _softmax_rows(s) * any_c
        outs.append(_dot(p.astype(BF16), kc_v[g][...]))
        psums.append(p[0:lq] + p[lq:2 * lq] + p[2 * lq:3 * lq] + p[3 * lq:4 * lq])
    return outs, psums


def _split3(x):
    hi = x.astype(BF16)
    r1 = x - hi.astype(F32)
    mid = r1.astype(BF16)
    lo = (r1 - mid.astype(F32)).astype(BF16)
    return hi, mid, lo


def _top_blocks_t(imp_ts, pos0, n_blk=SEL_PAD):
    blk = lax.broadcasted_iota(I32, (n_blk, LANES), 0)
    qp_t = pos0 + lax.broadcasted_iota(I32, (n_blk, LANES), 1)
    cur = jnp.right_shift(qp_t, 6)
    forced = (blk == 0) | (blk == cur) | (blk == cur - 1)
    valid = jnp.left_shift(blk, 6) <= qp_t
    v = jnp.concatenate([jnp.where(valid, jnp.where(forced, FORCED_SCORE, t), -1.0) for t in imp_ts], axis=1)
    blk_f = lax.broadcasted_iota(I32, (n_blk, KVH * LANES), 0).astype(F32)
    sel = jnp.zeros((n_blk, KVH * LANES), F32)
    for _ in range(SEL_TOP):
        m = jnp.max(v, axis=0, keepdims=True)
        idx = jnp.min(jnp.where(v == m, blk_f, float(n_blk)), axis=0, keepdims=True)
        hit = blk_f == idx
        sel = jnp.where(hit, 1.0, sel)
        v = jnp.where(hit, -jnp.inf, v)
    out = [jnp.where((sel[:, g * LANES:(g + 1) * LANES] > 0.5) & valid, 0.0, NEG) for g in range(KVH)]
    if n_blk < SEL_PAD:
        out = [jnp.concatenate([b, jnp.full((SEL_PAD - n_blk, LANES), NEG, F32)], axis=0) for b in out]
    return out


def _select_blocks(psums, ovt_ref, pos0, lq, n_blk):
    imp_ts = []
    for g in range(KVH):
        ps = psums[g]
        if lq < LANES:
            ps = jnp.concatenate([ps, jnp.zeros((LANES - lq, ps.shape[1]), F32)], axis=0)
        imp_ts.append(_dot_nt(ovt_ref[0:n_blk, :], ps, precision=HIGHEST))
    return [b.T[:lq].astype(BF16) for b in _top_blocks_t(imp_ts, pos0, n_blk)]


def _store_selbias(qa_ref, selbias, lq):
    for g in range(KVH):
        for hl in range(HPG):
            qa_ref[g, hl * lq:(hl + 1) * lq, LANES:2 * LANES] = selbias[g]


def _combine(gates, eg_ref, o_c, o_s, o_w, lq):
    lo_half = _lane_iota(lq) < DH

    def assemble(per_group):
        slabs = []
        for k in range(NSA_HEADS // 2):
            g, hl = divmod(2 * k, HPG)
            a = per_group[g][hl * lq:(hl + 1) * lq]
            b = per_group[g][(hl + 1) * lq:(hl + 2) * lq]
            slabs.append(jnp.where(lo_half, a, b))
        return jnp.concatenate(slabs, axis=1)

    return (_dot(gates, eg_ref[0]) * assemble(o_c)
            + _dot(gates, eg_ref[1]) * assemble(o_s)
            + _dot(gates, eg_ref[2]) * assemble(o_w))


def _gate_expand():
    r = jnp.arange(3 * NSA_HEADS)
    c = jnp.arange(NSA_WIDTH)
    return jnp.stack([(r[:, None] == 3 * (c[None, :] // DH) + br).astype(F32) for br in range(3)])


def _overlap_t(n_cmp):
    n = jnp.arange(n_cmp)
    s = jnp.arange(SEL_PAD)
    c_first = n * CMP_STRIDE
    c_end = c_first + CMP_LEN - 1
    b_first = s * SEL_LEN
    return ((c_first[None, :] < b_first[:, None] + SEL_LEN) & (c_end[None, :] >= b_first[:, None])).astype(F32)


def _key_rows(kv_f32, pos, invalid=None):
    n = kv_f32.shape[0]
    lane = _lane_iota(n)
    ex = _key_alibi_cols(pos, lane)
    if invalid is not None:
        ex = jnp.where((lane == DH + 4) & invalid, 1.0, ex)
    lo = lane < DH
    return jnp.where(lo, kv_f32, ex), jnp.where(lo, pltpu.roll(kv_f32, DH, 1), ex)


def _block_onehot(pos, n):
    return (lax.broadcasted_iota(I32, (n, SEL_PAD), 1) == jnp.right_shift(pos, 6)).astype(BF16)


def _tile4(x):
    return jnp.concatenate([x] * HPG, axis=1)


def _nsa_prompt_body(q_ref, gate_ref, kc_ref, kvs_ref, w0_ref, w1_ref, w2_ref, w3_ref, w4_ref,
                     ovt_ref, egt_ref, wband_ref, o_ref,
                     kck0, kck1, kcv0, kcv1, ka0, ka1, vt0, vt1, qa0, qa1,
                     sa_ref, sb_ref, m_ref, l_ref, acc_ref, *, seq_len):
    j = pl.program_id(1)
    kc_k, kc_vt, kaug, v_t, qa_t = (kck0, kck1), (kcv0, kcv1), (ka0, ka1), (vt0, vt1), (qa0, qa1)
    n_cmp = kc_ref.shape[2] // KVH
    lq = Q_BLOCK
    cols = HPG * lq

    @pl.when(j == 0)
    def _():
        for g in range(KVH):
            kc_k[g][...] = kc_ref[0, 0, g * n_cmp:(g + 1) * n_cmp, :].astype(BF16)
            kc_vt[g][...] = kc_ref[0, 1, g * n_cmp:(g + 1) * n_cmp, :].T[0:DH].astype(BF16)

        def build(i, _):
            r0 = pl.multiple_of(i * KEY_TILE, KEY_TILE)
            pos = r0 + lax.broadcasted_iota(I32, (KEY_TILE, 1), 0)
            k0, k1 = _key_rows(kvs_ref[0, pl.ds(r0, KEY_TILE), :LANES], pos)
            onehot = _block_onehot(pos, KEY_TILE)
            ka0[pl.ds(r0, KEY_TILE), :] = jnp.concatenate([k0.astype(BF16), onehot], axis=1)
            ka1[pl.ds(r0, KEY_TILE), :] = jnp.concatenate([k1.astype(BF16), onehot], axis=1)
            vt = kvs_ref[0, pl.ds(r0, KEY_TILE), LANES:].T.astype(BF16)
            vt0[:, pl.ds(r0, KEY_TILE)] = vt[0:DH]
            vt1[:, pl.ds(r0, KEY_TILE)] = vt[DH:]
            return 0

        lax.fori_loop(0, seq_len // KEY_TILE, build, 0)

    st = j * Q_BLOCK
    q_pos = st + lax.broadcasted_iota(I32, (1, lq), 1)
    q_pos4 = _tile4(q_pos)

    q_t = (q_ref[...] * ATTN_SCALE).T
    sub = lax.broadcasted_iota(I32, (DH, lq), 0)
    q_hi = jnp.left_shift(jnp.right_shift(q_pos, 7), 7).astype(F32)
    q_lo = jnp.bitwise_and(q_pos, LANES - 1).astype(F32)
    for h in range(NSA_HEADS):
        g, hl = divmod(h, HPG)
        slope = 2.0 ** (-(h + 1))
        ex = jnp.where(sub <= 1, slope,
                       jnp.where(sub == 2, -slope * q_hi,
                                 jnp.where(sub == 3, -slope * q_lo, jnp.where(sub == 4, NEG, 0.0))))
        qa_t[g][0:DH, hl * lq:(hl + 1) * lq] = q_t[h * DH:(h + 1) * DH].astype(BF16)
        qa_t[g][DH:2 * DH, hl * lq:(hl + 1) * lq] = ex.astype(BF16)

    c_end = lax.broadcasted_iota(I32, (n_cmp, 1), 0) * CMP_STRIDE + (CMP_LEN - 1)
    m_c = c_end <= q_pos4
    any_c = (q_pos4 >= CMP_LEN - 1).astype(F32)
    ovt_bf = ovt_ref[...].astype(BF16)
    o_c, imp_ts = [], []
    for g in range(KVH):
        s = jnp.where(m_c, _dot(kc_k[g][...], qa_t[g][0:LANES, :]), NEG)
        e = jnp.exp(s - jnp.max(s, axis=0, keepdims=True))
        p = e * (any_c / jnp.sum(e, axis=0, keepdims=True))
        o_c.append(_dot(kc_vt[g][...], p.astype(BF16)))
        psum = p[:, 0:lq] + p[:, lq:2 * lq] + p[:, 2 * lq:3 * lq] + p[:, 3 * lq:4 * lq]
        imp_ts.append(sum(_dot(ovt_bf, t) for t in _split3(psum)))
    bias = _top_blocks_t(imp_ts, st)
    for g in range(KVH):
        for hl in range(HPG):
            qa_t[g][2 * DH:, hl * lq:(hl + 1) * lq] = bias[g].astype(BF16)

    n_tiles = (st + Q_BLOCK + KEY_TILE - 1) // KEY_TILE
    for g in range(KVH):
        m_ref[g] = jnp.full((1, cols), NEG, F32)
        l_ref[g] = jnp.zeros((1, cols), F32)
        acc_ref[g] = jnp.zeros((DH, cols), F32)

    def scores(t, s_ref):
        r0 = pl.multiple_of(t * KEY_TILE, KEY_TILE)
        for g in range(KVH):
            s_ref[g] = _dot(kaug[g][pl.ds(r0, KEY_TILE), :], qa_t[g][...])

    def consume(t, s_ref, masked):
        r0 = pl.multiple_of(t * KEY_TILE, KEY_TILE)
        for g in range(KVH):
            s = s_ref[g]
            if masked:
                k_pos = r0 + lax.broadcasted_iota(I32, (KEY_TILE, 1), 0)
                s = jnp.where(k_pos <= q_pos4, s, NEG)
            m = m_ref[g]
            m_new = jnp.maximum(m, jnp.max(s, axis=0, keepdims=True))
            a = jnp.exp(m - m_new)
            e = jnp.exp(s - m_new)
            m_ref[g] = m_new
            l_ref[g] = a * l_ref[g] + jnp.sum(e, axis=0, keepdims=True)
            acc_ref[g] = a * acc_ref[g] + _dot(v_t[g][:, pl.ds(r0, KEY_TILE)], e.astype(BF16))

    scores(0, sa_ref)
    n_pairs = (n_tiles - 1) // 2

    def pair(u, _):
        scores(2 * u + 1, sb_ref)
        consume(2 * u, sa_ref, False)
        scores(2 * u + 2, sa_ref)
        consume(2 * u + 1, sb_ref, False)
        return 0

    lax.fori_loop(0, n_pairs, pair, 0)
    odd_tail = (n_tiles - 1) - 2 * n_pairs == 1

    @pl.when(odd_tail)
    def _():
        scores(n_tiles - 1, sb_ref)
        consume(n_tiles - 2, sa_ref, False)
        consume(n_tiles - 1, sb_ref, True)

    @pl.when(jnp.logical_not(odd_tail))
    def _():
        consume(n_tiles - 1, sa_ref, True)

    o_s = [acc_ref[g] * (1.0 / l_ref[g]) for g in range(KVH)]

    band = jnp.concatenate([w0_ref[0], w1_ref[0], w2_ref[0], w3_ref[0], w4_ref[0]], axis=0)
    n_win = band.shape[0]
    w_pos_col = st - WINDOW + lax.broadcasted_iota(I32, (n_win, 1), 0)
    kw_k = _key_rows(band[:, :LANES], jnp.maximum(w_pos_col, 0), invalid=w_pos_col < 0)
    vw_t = band[:, LANES:].T.astype(BF16)
    band_bias = _tile4(wband_ref[...])
    o_w = []
    for g in range(KVH):
        s = _dot(kw_k[g].astype(BF16), qa_t[g][0:LANES, :]) + band_bias
        e = jnp.exp(s - jnp.max(s, axis=0, keepdims=True))
        o_w.append(_dot(vw_t[g * DH:(g + 1) * DH], e.astype(BF16)) * (1.0 / jnp.sum(e, axis=0, keepdims=True)))

    def heads(per_group):
        return jnp.concatenate([per_group[h // HPG][:, (h % HPG) * lq:(h % HPG + 1) * lq]
                                for h in range(NSA_HEADS)], axis=0)

    gates = gate_ref[...]
    out_t = (_dot_nt(egt_ref[0], gates) * heads(o_c)
             + _dot_nt(egt_ref[1], gates) * heads(o_s)
             + _dot_nt(egt_ref[2], gates) * heads(o_w))
    o_ref[...] = out_t.T.astype(o_ref.dtype)


def _gate_expand_t():
    r = jnp.arange(3 * NSA_HEADS)
    c = jnp.arange(NSA_WIDTH)
    return jnp.stack([(3 * (c[:, None] // DH) + br == r[None, :]).astype(F32) for br in range(3)])


def _nsa_prompt(q, gates, kc, kvs, kvw):
    B, S, _ = kvs.shape
    nqb = S // Q_BLOCK
    n_cmp = kc.shape[2] // KVH
    ovt = _overlap_t(n_cmp)
    egt = _gate_expand_t()
    n_band = WINDOW // Q_BLOCK + 1
    d_band = jnp.arange(Q_BLOCK)[None, :] + WINDOW - jnp.arange(n_band * Q_BLOCK)[:, None]
    wband = jnp.where((d_band >= 0) & (d_band < WINDOW), 0.0, NEG).astype(F32)

    def band_spec(i):
        return pl.BlockSpec((1, Q_BLOCK, KV_WIDTH),
                            lambda b, j, i=i: (b, jnp.maximum(j - (n_band - 1) + i, 0), 0))

    full = lambda a: pl.BlockSpec(a.shape, lambda b, j: (0,) * a.ndim)
    return pl.pallas_call(
        functools.partial(_nsa_prompt_body, seq_len=S),
        grid=(B, nqb),
        in_specs=[pl.BlockSpec((Q_BLOCK, NSA_WIDTH), lambda b, j: (b * nqb + j, 0)),
                  pl.BlockSpec((Q_BLOCK, 3 * NSA_HEADS), lambda b, j: (b * nqb + j, 0)),
                  pl.BlockSpec((1,) + kc.shape[1:], lambda b, j: (b, 0, 0, 0)),
                  pl.BlockSpec((1, S, KV_WIDTH), lambda b, j: (b, 0, 0))]
                 + [band_spec(i) for i in range(n_band)] + [full(ovt), full(egt), full(wband)],
        out_specs=pl.BlockSpec((Q_BLOCK, NSA_WIDTH), lambda b, j: (b * nqb + j, 0)),
        out_shape=jax.ShapeDtypeStruct((B * S, NSA_WIDTH), BF16),
        scratch_shapes=[pltpu.VMEM((n_cmp, LANES), BF16)] * 2
                       + [pltpu.VMEM((DH, n_cmp), BF16)] * 2
                       + [pltpu.VMEM((S, 2 * LANES), BF16)] * 2
                       + [pltpu.VMEM((DH, S), BF16)] * 2
                       + [pltpu.VMEM((2 * LANES, HPG * Q_BLOCK), BF16)] * 2
                       + [pltpu.VMEM((KVH, KEY_TILE, HPG * Q_BLOCK), F32)] * 2
                       + [pltpu.VMEM((KVH, 1, HPG * Q_BLOCK), F32)] * 2
                       + [pltpu.VMEM((KVH, DH, HPG * Q_BLOCK), F32)],
        compiler_params=_params(("arbitrary", "arbitrary")),
        name="nsa_prompt",
    )(q, gates, kc, kvs, *([kvw] * n_band), ovt, egt, wband)


def _nsa_sample_body(pt_ref, q_ref, gate_ref, kvs_new_ref, win_ref, kvw_new_ref, *rest,
                     n_pages, ls, past_len):
    cmp_pages = rest[:n_pages]
    slc_pages = rest[n_pages:2 * n_pages]
    (w1_ref, pe_ref, b1_ref, w2_ref, b2_ref, ovt_ref, eg_ref, o_ref,
     full_k, full_v, kck0, kck1, kcv0, kcv1, kt0, kt1, vt0, vt1, wkt0, wkt1, qa_ref) = rest[2 * n_pages:]
    del pt_ref
    kc_k, kc_v, kaug_t, v_t, wk_t = (kck0, kck1), (kcv0, kcv1), (kt0, kt1), (vt0, vt1), (wkt0, wkt1)
    n_cmp = past_len // CMP_STRIDE
    w_rows = win_ref.shape[4]
    lq = BF16_ROWS
    rows = HPG * lq
    w_start = past_len - w_rows

    @pl.when(pl.program_id(0) == 0)
    def _():
        sub = lax.broadcasted_iota(I32, (DH, past_len), 0)
        pos = lax.broadcasted_iota(I32, (1, past_len), 1)
        ex = _key_alibi_rows(pos, sub).astype(BF16)
        onehot = (lax.broadcasted_iota(I32, (SEL_PAD, past_len), 0) == jnp.right_shift(pos, 6)).astype(BF16)
        subw = lax.broadcasted_iota(I32, (DH, w_rows), 0)
        exw = _key_alibi_rows(w_start + lax.broadcasted_iota(I32, (1, w_rows), 1), subw).astype(BF16)
        for g in range(KVH):
            kaug_t[g][DH:2 * DH, :] = ex
            kaug_t[g][2 * DH:, :] = onehot
            wk_t[g][DH:, :] = exw

    for p in range(n_pages):
        cols = slice(p * PAGE_SIZE, (p + 1) * PAGE_SIZE)
        full_k[cols, :] = cmp_pages[p][0, 0].reshape(2 * DH, PAGE_SIZE).T
        full_v[cols, :] = cmp_pages[p][0, 1].reshape(2 * DH, PAGE_SIZE).T
        for g in range(KVH):
            kaug_t[g][0:DH, cols] = slc_pages[p][0, 0, g].astype(BF16)
            vt = slc_pages[p][0, 1, g].astype(BF16)
            v_t[g][0:DH, cols] = vt
            v_t[g][DH:, cols] = vt
    for g in range(KVH):
        wk_t[g][0:DH, :] = win_ref[0, 0, g].astype(BF16)

    keys_c, values_c = _compress((full_k, full_v), n_cmp, w1_ref, pe_ref, b1_ref, w2_ref, b2_ref)
    for g in range(KVH):
        kc_k[g][...] = keys_c[g * n_cmp:(g + 1) * n_cmp].astype(BF16)
        kc_v[g][...] = values_c[g * n_cmp:(g + 1) * n_cmp].astype(BF16)

    pad_q = jnp.zeros((lq - ls, NSA_WIDTH), F32)
    q_pos = past_len + lax.broadcasted_iota(I32, (lq, 1), 0)
    q_pos4 = _stack4(q_pos)
    _fill_queries(qa_ref, jnp.concatenate([q_ref[0] * ATTN_SCALE, pad_q], axis=0), q_pos, lq)
    gates = jnp.concatenate([gate_ref[0], jnp.zeros((lq - ls, 3 * NSA_HEADS), F32)], axis=0)

    o_c, psums = _cmp_branch(qa_ref, kc_k, kc_v, q_pos4, lq, n_cmp)
    n_sel = -(-(past_len + lq) // SEL_LEN)
    n_blk = -(-n_sel // 8) * 8
    _store_selbias(qa_ref, _select_blocks(psums, ovt_ref, past_len, lq, n_blk), lq)

    pad_k = jnp.zeros((LANES - ls, KV_WIDTH), F32)
    new_pos_col = past_len + lax.broadcasted_iota(I32, (LANES, 1), 0)
    new_pos = past_len + lax.broadcasted_iota(I32, (1, LANES), 1)
    new_s = jnp.concatenate([kvs_new_ref[0], pad_k], axis=0)
    new_w = jnp.concatenate([kvw_new_ref[0], pad_k], axis=0)
    ks_new = _key_rows(new_s[:, :LANES], new_pos_col)
    vs_new = _halves(new_s[:, LANES:])
    kw_new = _key_rows(new_w[:, :LANES], new_pos_col)
    vw_new = _halves(new_w[:, LANES:])
    onehot_new = _block_onehot(new_pos_col, LANES)
    causal_new = q_pos4 >= new_pos

    o_s, o_w = [], []
    d_past = q_pos4 - (w_start + lax.broadcasted_iota(I32, (1, w_rows), 1))
    m_past = (d_past >= 0) & (d_past < WINDOW)
    d_new = q_pos4 - new_pos
    m_new = (d_new >= 0) & (d_new < WINDOW)
    for g in range(KVH):
        s_past = _dot(qa_ref[g], kaug_t[g][...])
        k_new = jnp.concatenate([ks_new[g].astype(BF16), onehot_new], axis=1)
        s_new = jnp.where(causal_new, _dot_nt(qa_ref[g], k_new), NEG)
        m = jnp.maximum(jnp.max(s_past, axis=-1, keepdims=True), jnp.max(s_new, axis=-1, keepdims=True))
        e_past = jnp.exp(s_past - m)
        e_new = jnp.exp(s_new - m)
        den = jnp.sum(e_past, axis=-1, keepdims=True) + jnp.sum(e_new, axis=-1, keepdims=True)
        acc = _dot_nt(e_past.astype(BF16), v_t[g][...]) + _dot(e_new.astype(BF16), vs_new[g].astype(BF16))
        o_s.append(acc / den)

        sw_past = jnp.where(m_past, _dot(qa_ref[g, :, 0:LANES], wk_t[g][...]), NEG)
        sw_new = jnp.where(m_new, _dot_nt(qa_ref[g, :, 0:LANES], kw_new[g].astype(BF16)), NEG)
        m = jnp.maximum(jnp.max(sw_past, axis=-1, keepdims=True), jnp.max(sw_new, axis=-1, keepdims=True))
        e_past = jnp.exp(sw_past - m)
        e_new = jnp.exp(sw_new - m)
        den = jnp.sum(e_past, axis=-1, keepdims=True) + jnp.sum(e_new, axis=-1, keepdims=True)
        vw = win_ref[0, 1, g].astype(BF16)
        vw2 = jnp.concatenate([vw, vw], axis=0)
        acc = _dot_nt(e_past.astype(BF16), vw2) + _dot(e_new.astype(BF16), vw_new[g].astype(BF16))
        o_w.append(acc / den)

    o_ref[0] = _combine(gates, eg_ref, o_c, o_s, o_w, lq)[:ls]


def _nsa_sample(q, gates, kvs_new, kvw_new, win_t, cmp_t, slc_t, page_table, cw, past_len):
    Bd, ls, _ = q.shape
    n_pages = page_table.shape[1]
    n_cmp = past_len // CMP_STRIDE
    ovt = _overlap_t(n_cmp)
    eg = _gate_expand()
    w_rows = win_t.shape[4]

    per_b = lambda a: pl.BlockSpec((1,) + a.shape[1:], lambda b, pt: (b,) + (0,) * (a.ndim - 1))
    full = lambda a: pl.BlockSpec(a.shape, lambda b, pt: (0,) * a.ndim)
    page = lambda p: pl.BlockSpec((1, 2, KVH, DH, PAGE_SIZE), lambda b, pt, p=p: (pt[b, p], 0, 0, 0, 0))
    grid_spec = pltpu.PrefetchScalarGridSpec(
        num_scalar_prefetch=1,
        grid=(Bd,),
        in_specs=[per_b(q), per_b(gates), per_b(kvs_new), per_b(win_t), per_b(kvw_new)]
                 + [page(p) for p in range(n_pages)] * 2
                 + [full(a) for a in cw] + [full(ovt), full(eg)],
        out_specs=pl.BlockSpec((1, ls, NSA_WIDTH), lambda b, pt: (b, 0, 0)),
        scratch_shapes=[pltpu.VMEM((past_len, LANES), F32)] * 2
                       + [pltpu.VMEM((n_cmp, LANES), BF16)] * 4
                       + [pltpu.VMEM((2 * LANES, past_len), BF16)] * 2
                       + [pltpu.VMEM((LANES, past_len), BF16)] * 2
                       + [pltpu.VMEM((LANES, w_rows), BF16)] * 2
                       + [pltpu.VMEM((KVH, HPG * BF16_ROWS, 2 * LANES), BF16)],
    )
    return pl.pallas_call(
        functools.partial(_nsa_sample_body, n_pages=n_pages, ls=ls, past_len=past_len),
        grid_spec=grid_spec,
        out_shape=jax.ShapeDtypeStruct((Bd, ls, NSA_WIDTH), F32),
        compiler_params=_params(("arbitrary",)),
        name="nsa_sample",
    )(page_table, q, gates, kvs_new, win_t, kvw_new,
      *([cmp_t] * n_pages), *([slc_t] * n_pages), *cw, ovt, eg)


def _fin1_body(h_ref, pool_ref, nsa_ref, wo_ref, g_ref, b_ref, wq_ref, h1_ref, qm_ref):
    mix = (_dot(pool_ref[...].astype(BF16), wo_ref[0:POOL_WIDTH, :])
           + _dot(nsa_ref[...].astype(BF16), wo_ref[POOL_WIDTH:, :]))
    h1 = _layer_norm(DN_ALPHA * h_ref[...] + mix, g_ref[...], b_ref[...])
    h1_ref[...] = h1
    qm_ref[...] = (_dot(h1.astype(BF16), wq_ref[...]) * (MEM_HEAD_DIM ** -0.5)).astype(qm_ref.dtype)


def _fin1(h, pool_o, nsa_o, w_out_bf, g, b, wq_bf, q_dtype):
    T = h.shape[0]
    tm = ROW_TILE
    row = lambda n: pl.BlockSpec((tm, n), lambda i: (i, 0))
    full = lambda a: pl.BlockSpec(a.shape, lambda i: (0,) * a.ndim)
    return pl.pallas_call(
        _fin1_body,
        grid=(T // tm,),
        in_specs=[row(D_MODEL), row(POOL_WIDTH), row(NSA_WIDTH), full(w_out_bf), full(g), full(b), full(wq_bf)],
        out_specs=[row(D_MODEL), row(D_MODEL)],
        out_shape=[jax.ShapeDtypeStruct((T, D_MODEL), F32), jax.ShapeDtypeStruct((T, D_MODEL), q_dtype)],
        compiler_params=_params(("arbitrary",)),
        name="out_proj_ln1",
    )(h, pool_o, nsa_o, w_out_bf, g, b, wq_bf)


def _memattn_body(q_ref, kv_ref, o_ref):
    width = MEM_HEADS * MEM_HEAD_DIM
    for h in range(MEM_HEADS):
        cols = slice(h * MEM_HEAD_DIM, (h + 1) * MEM_HEAD_DIM)
        qh = q_ref[0, :, cols].astype(BF16)
        kh = kv_ref[0, :, cols].astype(BF16)
        vh = kv_ref[0, :, width + h * MEM_HEAD_DIM:width + (h + 1) * MEM_HEAD_DIM].astype(BF16)
        s = _dot_nt(qh, kh)
        e = jnp.exp(s - jnp.max(s, axis=-1, keepdims=True))
        o = _dot(e.astype(BF16), vh) * (1.0 / jnp.sum(e, axis=-1, keepdims=True))
        o_ref[0, :, cols] = o.astype(o_ref.dtype)


def _memattn(qm, mem_kv, tq):
    nb, L, W = qm.shape
    return pl.pallas_call(
        _memattn_body,
        grid=(nb, L // tq),
        in_specs=[pl.BlockSpec((1, tq, W), lambda b, t: (b, t, 0)),
                  pl.BlockSpec((1, MEM_LEN, 2 * W), lambda b, t: (b, 0, 0))],
        out_specs=pl.BlockSpec((1, tq, W), lambda b, t: (b, t, 0)),
        out_shape=jax.ShapeDtypeStruct((nb, L, W), BF16),
        compiler_params=_params(("arbitrary", "arbitrary")),
        name="mem_attn",
    )(qm, mem_kv)


def _memattn_few_body(q_ref, kv_ref, o_ref):
    lq = q_ref.shape[1]
    n_keys = MEM_LEN * MEM_HEADS
    q = q_ref[0]
    qs = jnp.concatenate([q[:, h * MEM_HEAD_DIM:(h + 1) * MEM_HEAD_DIM] for h in range(MEM_HEADS)], axis=0)
    k = kv_ref[0, :, 0, :, :].reshape(n_keys, MEM_HEAD_DIM).astype(BF16)
    v = kv_ref[0, :, 1, :, :].reshape(n_keys, MEM_HEAD_DIM).astype(BF16)
    s = _dot_nt(qs.astype(BF16), k)
    assert lq & (lq - 1) == 0 and MEM_HEADS & (MEM_HEADS - 1) == 0
    col_h = jnp.bitwise_and(lax.broadcasted_iota(I32, s.shape, 1), MEM_HEADS - 1)
    row_h = jnp.right_shift(lax.broadcasted_iota(I32, s.shape, 0), lq.bit_length() - 1)
    s = jnp.where(col_h == row_h, s, NEG)
    e = jnp.exp(s - jnp.max(s, axis=-1, keepdims=True))
    o = _dot(e.astype(BF16), v) * (1.0 / jnp.sum(e, axis=-1, keepdims=True))
    for h in range(MEM_HEADS):
        o_ref[0, :, h * MEM_HEAD_DIM:(h + 1) * MEM_HEAD_DIM] = o[h * lq:(h + 1) * lq]


def _memattn_few(qm, mem_kv):
    nb, lq, W = qm.shape
    return pl.pallas_call(
        _memattn_few_body,
        grid=(nb,),
        in_specs=[pl.BlockSpec((1, lq, W), lambda b: (b, 0, 0)),
                  pl.BlockSpec((1, MEM_LEN, 2, MEM_HEADS, MEM_HEAD_DIM), lambda b: (b, 0, 0, 0, 0))],
        out_specs=pl.BlockSpec((1, lq, W), lambda b: (b, 0, 0)),
        out_shape=jax.ShapeDtypeStruct((nb, lq, W), F32),
        compiler_params=_params(("arbitrary",)),
        name="mem_attn_few",
    )(qm, mem_kv)


def _fin2_body(cnt0_ref, h1_ref, o_ref, wo_ref, g_ref, b_ref, rw_ref, rb_ref, *rest):
    h2_ref, h2b_ref, te_ref, tg_ref, cnt_ref, run_ref = rest[-6:]
    tm = h1_ref.shape[0]

    @pl.when(pl.program_id(0) == 0)
    def _():
        run_ref[...] = cnt0_ref[...]

    a = _dot(o_ref[...].astype(BF16), wo_ref[...])
    h2 = _layer_norm(DN_ALPHA * h1_ref[...] + a, g_ref[...], b_ref[...])
    h2_ref[...] = h2
    h2b_ref[...] = h2.astype(BF16)
    logits = jnp.dot(h2, rw_ref[...], preferred_element_type=F32, precision=HIGHEST) + rb_ref[...]
    e_iota = lax.broadcasted_iota(I32, (tm, N_EXPERTS), 1).astype(F32)
    lane = lax.broadcasted_iota(I32, (tm, LANES), 1)
    te = jnp.zeros((tm, LANES), F32)
    tv = jnp.full((tm, LANES), NEG, F32)
    work = logits
    chosen = []
    for k in range(TOP_K):
        m = jnp.max(work, axis=-1, keepdims=True)
        idx = jnp.min(jnp.where(work == m, e_iota, float(N_EXPERTS)), axis=-1, keepdims=True)
        hit = e_iota == idx
        chosen.append(hit)
        te = jnp.where(lane == k, idx, te)
        tv = jnp.where(lane == k, m, tv)
        work = jnp.where(hit, -jnp.inf, work)
    member = sum(c.astype(F32) for c in chosen)
    earlier = (lax.broadcasted_iota(I32, (tm, tm), 0) > lax.broadcasted_iota(I32, (tm, tm), 1)).astype(BF16)
    before = _dot(earlier, member.astype(BF16)) + run_ref[...]
    for k in range(TOP_K):
        rank = jnp.sum(jnp.where(chosen[k], before, 0.0), axis=-1, keepdims=True)
        te = jnp.where(lane == TOP_K + k, rank, te)
    run_ref[...] = run_ref[...] + jnp.sum(member, axis=0, keepdims=True)
    cnt_ref[...] = run_ref[...]
    ex = jnp.exp(tv - jnp.max(tv, axis=-1, keepdims=True))
    te_ref[...] = te.astype(I32)
    tg_ref[...] = ex / jnp.sum(ex, axis=-1, keepdims=True)


def _fin2(cnt0, h1, o, wo_bf, g, b, rw, rb, total_rows, row_offset=0, into=None):
    T = h1.shape[0]
    tm = ROW_TILE
    blk0 = row_offset // tm
    row = lambda n: pl.BlockSpec((tm, n), lambda i: (i, 0))
    out_row = lambda n: pl.BlockSpec((tm, n), lambda i: (i + blk0, 0))
    full = lambda a: pl.BlockSpec(a.shape, lambda i: (0,) * a.ndim)
    ins = [cnt0, h1, o, wo_bf, g, b, rw, rb]
    in_specs = [full(cnt0), row(D_MODEL), row(D_MODEL), full(wo_bf), full(g), full(b), full(rw), full(rb)]
    aliases = {}
    if into is not None:
        aliases = {len(ins) + k: k for k in range(len(into))}
        in_specs = in_specs + [pl.BlockSpec(memory_space=pl.ANY)] * len(into)
        ins = ins + list(into)
    return pl.pallas_call(
        _fin2_body,
        grid=(T // tm,),
        in_specs=in_specs,
        out_specs=[out_row(D_MODEL), out_row(D_MODEL), out_row(LANES), out_row(LANES), full(cnt0)],
        out_shape=[jax.ShapeDtypeStruct((total_rows, D_MODEL), F32), jax.ShapeDtypeStruct((total_rows, D_MODEL), BF16),
                   jax.ShapeDtypeStruct((total_rows, LANES), I32), jax.ShapeDtypeStruct((total_rows, LANES), F32),
                   jax.ShapeDtypeStruct(cnt0.shape, F32)],
        scratch_shapes=[pltpu.VMEM(cnt0.shape, F32)],
        input_output_aliases=aliases,
        compiler_params=_params(("arbitrary",)),
        name="mem_out_ln2_router",
    )(*ins)


def _moe_body(ut_ref, ue_ref, nu_ref, rs_ref, re_ref, x_ref, rw_ref, wgu_ref, bgu_ref, wdn_ref, bdn_ref,
              y_ref, wgu_bf, wdn_bf):
    u = pl.program_id(0)
    bk = x_ref.shape[0]
    e = ue_ref[u]
    tile = ut_ref[u]
    prev = jnp.maximum(u - 1, 0)

    @pl.when((u == 0) | (e != ue_ref[prev]))
    def _():
        wgu_bf[...] = wgu_ref[0].astype(BF16)
        wdn_bf[...] = wdn_ref[0].astype(BF16)

    @pl.when(u < nu_ref[0])
    def _():
        x = x_ref[...].astype(BF16)
        g = _dot(x, wgu_bf[:, :D_FF]) + bgu_ref[0, :, :D_FF]
        v = _dot(x, wgu_bf[:, D_FF:]) + bgu_ref[0, :, D_FF:]
        g = jnp.minimum(g, SWIGLU_LIMIT)
        v = jnp.clip(v, -SWIGLU_LIMIT, SWIGLU_LIMIT)
        a = g * (1.0 / (1.0 + jnp.exp(-SWIGLU_ALPHA * g))) * (v + 1.0)
        y = _dot(a.astype(BF16), wdn_bf[...]) + bdn_ref[0]
        row = tile * bk + lax.broadcasted_iota(I32, (bk, 1), 0)
        mine = (row >= rs_ref[e]) & (row < re_ref[e])
        y = jnp.where(mine, y * rw_ref[...], 0.0)

        @pl.when((u == 0) | (tile != ut_ref[prev]))
        def _():
            y_ref[...] = y

        @pl.when((u > 0) & (tile == ut_ref[prev]))
        def _():
            y_ref[...] = y_ref[...] + y


def _moe_gmm(x_rows, row_w, units, w_gu, b_gu, w_dn, b_dn):
    N = x_rows.shape[0]
    bk = MOE_ROWS
    unit_tile, unit_e, n_units, r_start, r_end = units
    grid_spec = pltpu.PrefetchScalarGridSpec(
        num_scalar_prefetch=5,
        grid=(unit_tile.shape[0],),
        in_specs=[pl.BlockSpec((bk, D_MODEL), lambda u, ut, ue, *_: (ut[u], 0)),
                  pl.BlockSpec((bk, 1), lambda u, ut, ue, *_: (ut[u], 0)),
                  pl.BlockSpec((1, D_MODEL, 2 * D_FF), lambda u, ut, ue, *_: (ue[u], 0, 0)),
                  pl.BlockSpec((1, 1, 2 * D_FF), lambda u, ut, ue, *_: (ue[u], 0, 0)),
                  pl.BlockSpec((1, D_FF, D_MODEL), lambda u, ut, ue, *_: (ue[u], 0, 0)),
                  pl.BlockSpec((1, 1, D_MODEL), lambda u, ut, ue, *_: (ue[u], 0, 0))],
        out_specs=pl.BlockSpec((bk, D_MODEL), lambda u, ut, ue, *_: (ut[u], 0)),
        scratch_shapes=[pltpu.VMEM((D_MODEL, 2 * D_FF), BF16), pltpu.VMEM((D_FF, D_MODEL), BF16)],
    )
    return pl.pallas_call(
        _moe_body,
        grid_spec=grid_spec,
        out_shape=jax.ShapeDtypeStruct((N, D_MODEL), F32),
        compiler_params=_params(("arbitrary",)),
        name="moe_experts",
    )(unit_tile, unit_e, n_units, r_start, r_end, x_rows, row_w, w_gu, b_gu, w_dn, b_dn)


FLAT_BITS = 17


def _moe_routing(te, tg, counts):
    bk = MOE_ROWS
    T = te.shape[0]
    N = T * TOP_K
    assert N % bk == 0 and N <= (1 << FLAT_BITS)
    experts = jnp.arange(N_EXPERTS, dtype=I32)
    top_e = te[:, :TOP_K]
    r_end = jnp.cumsum(counts).astype(I32)
    r_start = r_end - counts
    onehot = top_e[:, :, None] == experts[None, None, :]
    pos = jnp.sum(jnp.where(onehot, r_start[None, None, :], 0), axis=-1) + te[:, TOP_K:2 * TOP_K]
    key = jnp.left_shift(top_e.reshape(-1), FLAT_BITS) + jnp.arange(N, dtype=I32)
    key_s, gate_s = lax.sort((key, tg[:, :TOP_K].reshape(-1)), num_keys=1)
    tok_s = jnp.right_shift(jnp.bitwise_and(key_s, (1 << FLAT_BITS) - 1), 2)
    first = r_start // bk
    last = (r_end - 1) // bk
    n_e = jnp.where(counts > 0, last - first + 1, 0)
    u_end = jnp.cumsum(n_e).astype(I32)
    u_start = u_end - n_e
    n_units = u_end[-1]
    u = jnp.minimum(jnp.arange(N // bk + N_EXPERTS - 1, dtype=I32), n_units - 1)
    unit_e = jnp.sum((u[:, None] >= u_end[None, :]).astype(I32), axis=1)
    unit_tile = u + jnp.sum(jnp.where(unit_e[:, None] == experts[None, :], (first - u_start)[None, :], 0), axis=1)
    return pos, tok_s, gate_s, (unit_tile, unit_e, n_units.reshape(1), r_start, r_end)


def _fin3_body(h2_ref, y0_ref, y1_ref, y2_ref, y3_ref, g_ref, b_ref, o_ref):
    y = (y0_ref[...] + y1_ref[...]) + (y2_ref[...] + y3_ref[...])
    o_ref[...] = _layer_norm(DN_ALPHA * h2_ref[...] + y, g_ref[...], b_ref[...])


def _fin3(h2, ys, g, b, row_offset, T):
    tm = ROW_TILE
    blk0 = row_offset // tm
    row = lambda n: pl.BlockSpec((tm, n), lambda i: (i, 0))
    full = lambda a: pl.BlockSpec(a.shape, lambda i: (0,) * a.ndim)
    return pl.pallas_call(
        _fin3_body,
        grid=(T // tm,),
        in_specs=[pl.BlockSpec((tm, D_MODEL), lambda i: (i + blk0, 0))] * (1 + TOP_K) + [full(g), full(b)],
        out_specs=row(D_MODEL),
        out_shape=jax.ShapeDtypeStruct((T, D_MODEL), F32),
        compiler_params=_params(("arbitrary",)),
        name="combine_ln3",
    )(h2, *ys, g, b)


def kernel(x_prompt, x_sample, cache_cmp_kv, cache_slc_kv, state_win_kv, state_pool, cache_mem_kv, page_table,
           mem_prompt, w_in, pool_w, pool_scale, cmp_pe, cmp_w1, cmp_b1, cmp_w2, cmp_b2, w_out, ln1_g, ln1_b,
           mem_wq, mem_wkv, mem_wo, ln2_g, ln2_b, router_w, router_b, exp_w_gu, exp_b_gu, exp_w_dn, exp_b_dn,
           ln3_g, ln3_b):
    Bp, S, D = x_prompt.shape
    Bd, Ls, _ = x_sample.shape
    Tp, Ts = Bp * S, Bd * Ls
    l = 0
    w_in_bf = w_in[l].astype(BF16)
    pool_w_bf = pool_w[l].astype(BF16)
    ps = pool_scale[l][None, :]
    cw = _compress_weights(cmp_pe[l], cmp_w1[l], cmp_b1[l], cmp_w2[l], cmp_b2[l])
    w_out_bf = w_out[l].astype(BF16)
    wq_bf = mem_wq[l].astype(BF16)
    wo_bf = mem_wo[l].astype(BF16)
    vec = lambda a: a[l][None, :]

    up, qp, kvc_p, kvs_p, kvw_p, gp, pool_p, kvc_t, kvs_t = _inproj_prompt(
        x_prompt.reshape(Tp, D), w_in_bf, pool_w_bf, ps, S)
    kc_p = _compress_prompt(kvc_p.reshape(Bp, S, KV_WIDTH), cw)
    nsa_p = _nsa_prompt(qp, gp, kc_p, kvs_p.reshape(Bp, S, KV_WIDTH), kvw_p.reshape(Bp, S, KV_WIDTH))
    mem_kv_p = _matmul(mem_prompt.reshape(Bp * MEM_LEN, D), mem_wkv[l]).reshape(Bp, MEM_LEN, 2 * D)
    h1_p, qm_p = _fin1(x_prompt.reshape(Tp, D), pool_p, nsa_p, w_out_bf, vec(ln1_g), vec(ln1_b), wq_bf, BF16)
    om_p = _memattn(qm_p.reshape(Bp, S, D), mem_kv_p, ROW_TILE).reshape(Tp, D)
    T = Tp + Ts
    *routed_p, cnt_p = _fin2(jnp.zeros((1, N_EXPERTS), F32), h1_p, om_p, wo_bf, vec(ln2_g), vec(ln2_b),
                             router_w[l], vec(router_b), total_rows=T)

    state_pad = jnp.pad(state_pool[l], ((0, 0), (1, 0), (0, 0)))
    us, qs, kvc_s, kvs_s, kvw_s, gs, pool_s = _inproj_sample(
        x_sample.reshape(Ts, D), w_in_bf, pool_w_bf, ps, state_pad, Ls, PAST_LEN)
    w_rows = state_win_kv.shape[2]
    feature_major = lambda a: jnp.transpose(a, (0, 2, 3, 4, 1))
    nsa_s = _nsa_sample(qs.reshape(Bd, Ls, NSA_WIDTH), gs.reshape(Bd, Ls, 3 * NSA_HEADS),
                        kvs_s.reshape(Bd, Ls, KV_WIDTH), kvw_s.reshape(Bd, Ls, KV_WIDTH),
                        feature_major(state_win_kv[l]), feature_major(cache_cmp_kv[l]),
                        feature_major(cache_slc_kv[l]), page_table, cw, PAST_LEN)
    h1_s, qm_s = _fin1(x_sample.reshape(Ts, D), pool_s, nsa_s.reshape(Ts, NSA_WIDTH), w_out_bf,
                       vec(ln1_g), vec(ln1_b), wq_bf, F32)
    om_s = _memattn_few(qm_s.reshape(Bd, Ls, D), cache_mem_kv[l]).reshape(Ts, D)
    h2, h2_bf, te, tg, cnt_s = _fin2(cnt_p, h1_s, om_s, wo_bf, vec(ln2_g), vec(ln2_b), router_w[l], vec(router_b),
                                     total_rows=T, row_offset=Tp, into=routed_p)

    pos, tok_s, gate_s, units = _moe_routing(te, tg, cnt_s[0].astype(I32))
    y_rows = _moe_gmm(h2_bf[tok_s], gate_s[:, None], units, exp_w_gu[l], exp_b_gu[l][:, None, :],
                      exp_w_dn[l], exp_b_dn[l][:, None, :])
    ys = [y_rows[pos[:, k]] for k in range(TOP_K)]
    y_prompt = _fin3(h2, ys, vec(ln3_g), vec(ln3_b), 0, Tp).reshape(Bp, S, D)
    y_sample = _fin3(h2, ys, vec(ln3_g), vec(ln3_b), Tp, Ts).reshape(Bd, Ls, D)

    kv6 = lambda a, b, n: a.reshape(1, b, n, 2, KVH, DH)
    row_major = lambda a: jnp.transpose(a, (0, 4, 1, 2, 3))
    win_p = kvw_p.reshape(Bp, S, KV_WIDTH)[:, S - min(WINDOW, S):]
    win_s = jnp.concatenate([state_win_kv[l], kvw_s.reshape(Bd, Ls, 2, KVH, DH)], axis=1)[:, -w_rows:]
    pool_state_p = up.reshape(Bp, S, POOL_WIDTH)[:, S - POOL_STATE:]
    pool_state_s = jnp.concatenate([state_pool[l], us.reshape(Bd, Ls, POOL_WIDTH)], axis=1)[:, -POOL_STATE:]
    return (y_prompt, y_sample,
            row_major(kvc_t)[None], row_major(kvs_t)[None], kv6(win_p, Bp, min(WINDOW, S)),
            pool_state_p[None], mem_kv_p.reshape(1, Bp, MEM_LEN, 2, MEM_HEADS, MEM_HEAD_DIM),
            kv6(kvc_s, Bd, Ls), kv6(kvs_s, Bd, Ls), win_s[None], pool_state_s[None])
```

```python
import functools

import jax
import jax.numpy as jnp
from jax import lax
from jax.experimental import pallas as pl
from jax.experimental.pallas import tpu as pltpu

F32 = jnp.float32
BF16 = jnp.bfloat16
I32 = jnp.int32

D_MODEL = 1024
POOL_WIDTH = 512
POOL_WINDOWS = (2, 4, 8, 16)
POOL_GROUP = 128
POOL_STATE = 15
NSA_WIDTH = 512
DH = 64
NSA_HEADS = 8
KVH = 2
HPG = 4
CMP_LEN = 32
CMP_STRIDE = 16
CMP_HIDDEN = 256
SEL_LEN = 64
SEL_TOP = 16
WINDOW = 512
Q_BLOCK = 128
KV_WIDTH = 256
IN_WIDTH = 1816
ATTN_SCALE = DH ** -0.5
FORCED_SCORE = 1e4
NEG = -1e30
MEM_LEN = 256
MEM_HEADS = 4
MEM_HEAD_DIM = 256
N_EXPERTS = 32
TOP_K = 4
D_FF = 1024
SWIGLU_LIMIT = 7.0
SWIGLU_ALPHA = 1.702
DN_ALPHA = 2.0 ** 0.25
LN_EPS = 1e-5
PAST_LEN = 2048
PAGE_SIZE = 128

LANES = 128
SEL_PAD = 128
KEY_TILE = 512
ROW_TILE = 512
MOE_ROWS = 512
BF16_ROWS = 16
VMEM_LIMIT = 56 * 1024 * 1024

HIGHEST = lax.Precision.HIGHEST


def _dot(a, b):
    return jnp.dot(a, b, preferred_element_type=F32)


def _dot_nt(a, b, precision=None):
    return lax.dot_general(a, b, (((1,), (1,)), ((), ())), preferred_element_type=F32,
                           precision=precision)


def _layer_norm(x, g, b):
    mu = jnp.mean(x, axis=-1, keepdims=True)
    xc = x - mu
    var = jnp.mean(xc * xc, axis=-1, keepdims=True)
    return xc * lax.rsqrt(var + LN_EPS) * g + b


def _params(sem, vmem=VMEM_LIMIT):
    return pltpu.CompilerParams(dimension_semantics=sem, vmem_limit_bytes=vmem)


def _split_store(u, up_ref, q_ref, kvc_ref, kvs_ref, kvw_ref, gate_ref):
    o1 = POOL_WIDTH
    o2 = o1 + NSA_WIDTH
    o3 = o2 + KV_WIDTH
    o4 = o3 + KV_WIDTH
    o5 = o4 + KV_WIDTH
    up_ref[...] = u[:, :o1]
    q_ref[...] = u[:, o1:o2]
    kvc_ref[...] = u[:, o2:o3]
    kvs_ref[...] = u[:, o3:o4]
    kvw_ref[...] = u[:, o4:o5]
    gate_ref[...] = 1.0 / (1.0 + jnp.exp(-u[:, o5:]))


def _inproj_prompt_body(x_ref, w_ref, pw_ref, ps_ref,
                        up_ref, q_ref, kvc_ref, kvs_ref, kvw_ref, gate_ref, pool_ref, kvct_ref, kvst_ref,
                        ext_ref, *, tm, tiles_per_seq):
    halo = POOL_STATE + 1
    t_in_seq = pl.program_id(0) % tiles_per_seq
    u = _dot(x_ref[...].astype(BF16), w_ref[...])
    _split_store(u, up_ref, q_ref, kvc_ref, kvs_ref, kvw_ref, gate_ref)
    o2 = POOL_WIDTH + NSA_WIDTH
    kvct_ref[0] = u[:, o2:o2 + KV_WIDTH].T.reshape(2, KVH, DH, tm)
    kvst_ref[0] = u[:, o2 + KV_WIDTH:o2 + 2 * KV_WIDTH].T.reshape(2, KVH, DH, tm)

    @pl.when(t_in_seq == 0)
    def _():
        ext_ref[0:halo, :] = jnp.zeros((halo, POOL_WIDTH), F32)

    ext_ref[halo:halo + tm, :] = u[:, :POOL_WIDTH]
    pos = t_in_seq * tm + lax.broadcasted_iota(I32, (tm, 1), 0)
    for gi, w in enumerate(POOL_WINDOWS):
        cols = slice(gi * POOL_GROUP, (gi + 1) * POOL_GROUP)
        acc = ext_ref[halo:halo + tm, cols]
        for k in range(1, w):
            acc = acc + ext_ref[halo - k:halo - k + tm, cols]
        cnt = jnp.minimum(pos + 1, w).astype(F32)
        d = acc / cnt - ext_ref[halo:halo + tm, cols]
        o = _dot(d.astype(BF16), pw_ref[gi])
        pool_ref[:, cols] = (o * ps_ref[:, cols]).astype(pool_ref.dtype)
    ext_ref[0:halo, :] = ext_ref[tm:tm + halo, :]


def _inproj_prompt(x2d, w_in_bf, pool_w_bf, pool_scale, seq_len):
    T = x2d.shape[0]
    tm = ROW_TILE
    outs = [POOL_WIDTH, NSA_WIDTH, KV_WIDTH, KV_WIDTH, KV_WIDTH, 3 * NSA_HEADS, POOL_WIDTH]
    row = lambda n: pl.BlockSpec((tm, n), lambda i: (i, 0))
    full = lambda a: pl.BlockSpec(a.shape, lambda i: (0,) * a.ndim)
    tps = seq_len // tm
    kvt_spec = pl.BlockSpec((1, 2, KVH, DH, tm), lambda i: (i // tps, 0, 0, 0, i % tps))
    kvt_shape = jax.ShapeDtypeStruct((T // seq_len, 2, KVH, DH, seq_len), F32)
    return pl.pallas_call(
        functools.partial(_inproj_prompt_body, tm=tm, tiles_per_seq=tps),
        grid=(T // tm,),
        in_specs=[row(D_MODEL), full(w_in_bf), full(pool_w_bf), full(pool_scale)],
        out_specs=[row(n) for n in outs] + [kvt_spec] * 2,
        out_shape=[jax.ShapeDtypeStruct((T, n), F32) for n in outs[:-1]]
                  + [jax.ShapeDtypeStruct((T, outs[-1]), BF16)] + [kvt_shape] * 2,
        scratch_shapes=[pltpu.VMEM((tm + POOL_STATE + 1, POOL_WIDTH), F32)],
        compiler_params=_params(("arbitrary",)),
        name="inproj_prompt",
    )(x2d, w_in_bf, pool_w_bf, pool_scale)


def _inproj_sample_body(x_ref, w_ref, pw_ref, ps_ref, st_ref,
                        up_ref, q_ref, kvc_ref, kvs_ref, kvw_ref, gate_ref, pool_ref,
                        ext_ref, *, nb, ls, pos0):
    halo = POOL_STATE + 1
    tm = nb * ls
    u = _dot(x_ref[...].astype(BF16), w_ref[...])
    _split_store(u, up_ref, q_ref, kvc_ref, kvs_ref, kvw_ref, gate_ref)
    ext_ref[:, 0:halo, :] = st_ref[...]
    ext_ref[:, halo:halo + ls, :] = u[:, :POOL_WIDTH].reshape(nb, ls, POOL_WIDTH)
    pos = pos0 + lax.broadcasted_iota(I32, (1, ls, 1), 1)
    for gi, w in enumerate(POOL_WINDOWS):
        cols = slice(gi * POOL_GROUP, (gi + 1) * POOL_GROUP)
        acc = ext_ref[:, halo:halo + ls, cols]
        for k in range(1, w):
            acc = acc + ext_ref[:, halo - k:halo - k + ls, cols]
        cnt = jnp.minimum(pos + 1, w).astype(F32)
        d = acc / cnt - ext_ref[:, halo:halo + ls, cols]
        o = _dot(d.reshape(tm, POOL_GROUP).astype(BF16), pw_ref[gi])
        pool_ref[:, cols] = (o * ps_ref[:, cols]).astype(pool_ref.dtype)


def _inproj_sample(x2d, w_in_bf, pool_w_bf, pool_scale, state_pad, ls, pos0):
    T = x2d.shape[0]
    nb = ROW_TILE // ls
    tm = nb * ls
    outs = [POOL_WIDTH, NSA_WIDTH, KV_WIDTH, KV_WIDTH, KV_WIDTH, 3 * NSA_HEADS, POOL_WIDTH]
    row = lambda n: pl.BlockSpec((tm, n), lambda i: (i, 0))
    full = lambda a: pl.BlockSpec(a.shape, lambda i: (0,) * a.ndim)
    return pl.pallas_call(
        functools.partial(_inproj_sample_body, nb=nb, ls=ls, pos0=pos0),
        grid=(T // tm,),
        in_specs=[row(D_MODEL), full(w_in_bf), full(pool_w_bf), full(pool_scale),
                  pl.BlockSpec((nb, POOL_STATE + 1, POOL_WIDTH), lambda i: (i, 0, 0))],
        out_specs=[row(n) for n in outs],
        out_shape=[jax.ShapeDtypeStruct((T, n), F32) for n in outs],
        scratch_shapes=[pltpu.VMEM((nb, POOL_STATE + 1 + ls, POOL_WIDTH), F32)],
        compiler_params=_params(("arbitrary",)),
        name="inproj_sample",
    )(x2d, w_in_bf, pool_w_bf, pool_scale, state_pad)


def _matmul_body(x_ref, w_ref, o_ref):
    o_ref[...] = _dot(x_ref[...].astype(BF16), w_ref[...].astype(BF16))


def _matmul(x, w, tn=512):
    M, K = x.shape
    N = w.shape[1]
    return pl.pallas_call(
        _matmul_body,
        grid=(N // tn,),
        in_specs=[pl.BlockSpec((M, K), lambda j: (0, 0)), pl.BlockSpec((K, tn), lambda j: (0, j))],
        out_specs=pl.BlockSpec((M, tn), lambda j: (0, j)),
        out_shape=jax.ShapeDtypeStruct((M, N), F32),
        compiler_params=_params(("arbitrary",)),
        name="mem_kv_proj",
    )(x, w)


def _gelu_tanh(x):
    c = 0.7978845608028654
    return 0.5 * x * (1.0 + jnp.tanh(c * (x + 0.044715 * (x * x * x))))


def _compress(kv_refs, n_chunks, w1_ref, pe_ref, b1_ref, w2_ref, b2_ref):
    lo = _lane_iota(n_chunks) < DH
    quads = CMP_STRIDE // 4
    outs = []
    for c in range(2):
        acc_a = jnp.zeros((2 * n_chunks, CMP_HIDDEN), F32)
        acc_b = jnp.zeros((2 * n_chunks, CMP_HIDDEN), F32)
        for i in range(quads):
            x = [kv_refs[c][pl.ds(4 * i + m, n_chunks, stride=CMP_STRIDE), :] for m in range(4)]
            r = [pltpu.roll(v, DH, 1) for v in x]
            x_g0 = jnp.concatenate([jnp.where(lo, x[0], r[1]), jnp.where(lo, x[2], r[3])], axis=1)
            x_g1 = jnp.concatenate([jnp.where(lo, r[0], x[1]), jnp.where(lo, r[2], x[3])], axis=1)
            xq = jnp.concatenate([x_g0, x_g1], axis=0)
            acc_a = acc_a + _dot((xq + pe_ref[c, i:i + 1, :]).astype(BF16), w1_ref[c, i])
            acc_b = acc_b + _dot((xq + pe_ref[c, quads + i:quads + i + 1, :]).astype(BF16), w1_ref[c, quads + i])
        hid = acc_a + pltpu.roll(acc_b, 2 * n_chunks - 1, 0) + b1_ref[c]
        outs.append(_dot(_gelu_tanh(hid).astype(BF16), w2_ref[c]) + b2_ref[c])
    return outs


def _compress_prompt_body(kv_ref, w1_ref, pe_ref, b1_ref, w2_ref, b2_ref, o_ref, k_ref, v_ref, *, n_chunks):
    k_ref[...] = kv_ref[0, :, :LANES]
    v_ref[...] = kv_ref[0, :, LANES:]
    keys, values = _compress((k_ref, v_ref), n_chunks, w1_ref, pe_ref, b1_ref, w2_ref, b2_ref)
    o_ref[0, 0] = keys
    o_ref[0, 1] = values


def _compress_prompt(kvc, cw):
    B, S, _ = kvc.shape
    n_chunks = S // CMP_STRIDE
    full = lambda a: pl.BlockSpec(a.shape, lambda b: (0,) * a.ndim)
    return pl.pallas_call(
        functools.partial(_compress_prompt_body, n_chunks=n_chunks),
        grid=(B,),
        in_specs=[pl.BlockSpec((1, S, KV_WIDTH), lambda b: (b, 0, 0))] + [full(a) for a in cw],
        out_specs=pl.BlockSpec((1, 2, KVH * n_chunks, LANES), lambda b: (b, 0, 0, 0)),
        out_shape=jax.ShapeDtypeStruct((B, 2, KVH * n_chunks, LANES), F32),
        scratch_shapes=[pltpu.VMEM((S, LANES), F32)] * 2,
        compiler_params=_params(("arbitrary",)),
        name="compress_prompt",
    )(kvc, *cw)


def _compress_weights(cmp_pe, cmp_w1, cmp_b1, cmp_w2, cmp_b2):
    nq = CMP_LEN // 4
    w1 = cmp_w1.reshape(2, nq, 4 * DH, CMP_HIDDEN).astype(BF16)
    pe = cmp_pe.reshape(2, nq, 4 * DH)
    b1 = cmp_b1[:, None, :]
    w2 = jnp.stack([jnp.concatenate([cmp_w2[0], jnp.zeros_like(cmp_w2[0])], axis=1),
                    jnp.concatenate([cmp_w2[1], cmp_w2[1]], axis=1)]).astype(BF16)
    b2 = jnp.stack([jnp.concatenate([cmp_b2[0], jnp.zeros_like(cmp_b2[0])]),
                    jnp.concatenate([cmp_b2[1], cmp_b2[1]])])[:, None, :]
    return w1, pe, b1, w2, b2


def _softmax_rows(s):
    e = jnp.exp(s - jnp.max(s, axis=-1, keepdims=True))
    return e * (1.0 / jnp.sum(e, axis=-1, keepdims=True))


def _lane_iota(n):
    return lax.broadcasted_iota(I32, (n, LANES), 1)


def _key_alibi_cols(pos, lane):
    hi = jnp.left_shift(jnp.right_shift(pos, 6), 6).astype(F32)
    lo = jnp.bitwise_and(pos, SEL_LEN - 1).astype(F32)
    return jnp.where(lane == DH, hi,
                     jnp.where(lane == DH + 1, lo,
                               jnp.where((lane == DH + 2) | (lane == DH + 3), 1.0, 0.0)))


def _key_alibi_rows(pos, sub):
    hi = jnp.left_shift(jnp.right_shift(pos, 6), 6).astype(F32)
    lo = jnp.bitwise_and(pos, SEL_LEN - 1).astype(F32)
    return jnp.where(sub == 0, hi, jnp.where(sub == 1, lo, jnp.where((sub == 2) | (sub == 3), 1.0, 0.0)))


def _halves(x, zero_hi=False):
    lo = _lane_iota(x.shape[0]) < DH
    r = pltpu.roll(x, DH, 1)
    if zero_hi:
        return jnp.where(lo, x, 0.0), jnp.where(lo, r, 0.0)
    return jnp.where(lo, x, r), jnp.where(lo, r, x)


def _fill_queries(qa_ref, qs, q_pos, lq):
    lane = _lane_iota(lq)
    lo_half = lane < DH
    q_hi = jnp.left_shift(jnp.right_shift(q_pos, 7), 7).astype(F32)
    q_lo = jnp.bitwise_and(q_pos, LANES - 1).astype(F32)
    for h in range(NSA_HEADS):
        g, hl = divmod(h, HPG)
        slope = 2.0 ** (-(h + 1))
        slab = qs[:, (h // 2) * LANES:(h // 2 + 1) * LANES]
        if h % 2:
            slab = pltpu.roll(slab, DH, 1)
        ex = jnp.where((lane == DH) | (lane == DH + 1), slope,
                       jnp.where(lane == DH + 2, -slope * q_hi,
                                 jnp.where(lane == DH + 3, -slope * q_lo,
                                           jnp.where(lane == DH + 4, NEG, 0.0))))
        qa_ref[g, hl * lq:(hl + 1) * lq, 0:LANES] = jnp.where(lo_half, slab, ex).astype(BF16)


def _stack4(x):
    return jnp.concatenate([x] * HPG, axis=0)


def _cmp_branch(qa_ref, kc_k, kc_v, q_pos4, lq, n_cmp):
    c_end = lax.broadcasted_iota(I32, (1, n_cmp), 1) * CMP_STRIDE + (CMP_LEN - 1)
    m_c = c_end <= q_pos4
    any_c = (q_pos4 >= CMP_LEN - 1).astype(F32)
    outs, psums = [], []
    for g in range(KVH):
        s = jnp.where(m_c, _dot_nt(qa_ref[g, :, 0:LANES], kc_k[g][...]), NEG)
        p = _softmax_rows(s) * any_c
        outs.append(_dot(p.astype(BF16), kc_v[g][...]))
        psums.append(p[0:lq] + p[lq:2 * lq] + p[2 * lq:3 * lq] + p[3 * lq:4 * lq])
    return outs, psums


def _split3(x):
    hi = x.astype(BF16)
    r1 = x - hi.astype(F32)
    mid = r1.astype(BF16)
    lo = (r1 - mid.astype(F32)).astype(BF16)
    return hi, mid, lo


def _top_blocks_t(imp_ts, pos0, n_blk=SEL_PAD):
    blk = lax.broadcasted_iota(I32, (n_blk, LANES), 0)
    qp_t = pos0 + lax.broadcasted_iota(I32, (n_blk, LANES), 1)
    cur = jnp.right_shift(qp_t, 6)
    forced = (blk == 0) | (blk == cur) | (blk == cur - 1)
    valid = jnp.left_shift(blk, 6) <= qp_t
    v = jnp.concatenate([jnp.where(valid, jnp.where(forced, FORCED_SCORE, t), -1.0) for t in imp_ts], axis=1)
    blk_f = lax.broadcasted_iota(I32, (n_blk, KVH * LANES), 0).astype(F32)
    sel = jnp.zeros((n_blk, KVH * LANES), F32)
    for _ in range(SEL_TOP):
        m = jnp.max(v, axis=0, keepdims=True)
        idx = jnp.min(jnp.where(v == m, blk_f, float(n_blk)), axis=0, keepdims=True)
        hit = blk_f == idx
        sel = jnp.where(hit, 1.0, sel)
        v = jnp.where(hit, -jnp.inf, v)
    out = [jnp.where((sel[:, g * LANES:(g + 1) * LANES] > 0.5) & valid, 0.0, NEG) for g in range(KVH)]
    if n_blk < SEL_PAD:
        out = [jnp.concatenate([b, jnp.full((SEL_PAD - n_blk, LANES), NEG, F32)], axis=0) for b in out]
    return out


def _select_blocks(psums, ovt_ref, pos0, lq, n_blk):
    imp_ts = []
    for g in range(KVH):
        ps = psums[g]
        if lq < LANES:
            ps = jnp.concatenate([ps, jnp.zeros((LANES - lq, ps.shape[1]), F32)], axis=0)
        imp_ts.append(_dot_nt(ovt_ref[0:n_blk, :], ps, precision=HIGHEST))
    return [b.T[:lq].astype(BF16) for b in _top_blocks_t(imp_ts, pos0, n_blk)]


def _store_selbias(qa_ref, selbias, lq):
    for g in range(KVH):
        for hl in range(HPG):
            qa_ref[g, hl * lq:(hl + 1) * lq, LANES:2 * LANES] = selbias[g]


def _combine(gates, eg_ref, o_c, o_s, o_w, lq):
    lo_half = _lane_iota(lq) < DH

    def assemble(per_group):
        slabs = []
        for k in range(NSA_HEADS // 2):
            g, hl = divmod(2 * k, HPG)
            a = per_group[g][hl * lq:(hl + 1) * lq]
            b = per_group[g][(hl + 1) * lq:(hl + 2) * lq]
            slabs.append(jnp.where(lo_half, a, b))
        return jnp.concatenate(slabs, axis=1)

    return (_dot(gates, eg_ref[0]) * assemble(o_c)
            + _dot(gates, eg_ref[1]) * assemble(o_s)
            + _dot(gates, eg_ref[2]) * assemble(o_w))


def _gate_expand():
    r = jnp.arange(3 * NSA_HEADS)
    c = jnp.arange(NSA_WIDTH)
    return jnp.stack([(r[:, None] == 3 * (c[None, :] // DH) + br).astype(F32) for br in range(3)])


def _overlap_t(n_cmp):
    n = jnp.arange(n_cmp)
    s = jnp.arange(SEL_PAD)
    c_first = n * CMP_STRIDE
    c_end = c_first + CMP_LEN - 1
    b_first = s * SEL_LEN
    return ((c_first[None, :] < b_first[:, None] + SEL_LEN) & (c_end[None, :] >= b_first[:, None])).astype(F32)


def _key_rows(kv_f32, pos, invalid=None):
    n = kv_f32.shape[0]
    lane = _lane_iota(n)
    ex = _key_alibi_cols(pos, lane)
    if invalid is not None:
        ex = jnp.where((lane == DH + 4) & invalid, 1.0, ex)
    lo = lane < DH
    return jnp.where(lo, kv_f32, ex), jnp.where(lo, pltpu.roll(kv_f32, DH, 1), ex)


def _block_onehot(pos, n):
    return (lax.broadcasted_iota(I32, (n, SEL_PAD), 1) == jnp.right_shift(pos, 6)).astype(BF16)


def _tile4(x):
    return jnp.concatenate([x] * HPG, axis=1)


def _nsa_prompt_body(q_ref, gate_ref, kc_ref, kvs_ref, w0_ref, w1_ref, w2_ref, w3_ref, w4_ref,
                     ovt_ref, egt_ref, wband_ref, o_ref,
                     kck0, kck1, kcv0, kcv1, ka0, ka1, vt0, vt1, qa0, qa1,
                     sa_ref, sb_ref, m_ref, l_ref, acc_ref, *, seq_len):
    j = pl.program_id(1)
    kc_k, kc_vt, kaug, v_t, qa_t = (kck0, kck1), (kcv0, kcv1), (ka0, ka1), (vt0, vt1), (qa0, qa1)
    n_cmp = kc_ref.shape[2] // KVH
    lq = Q_BLOCK
    cols = HPG * lq

    @pl.when(j == 0)
    def _():
        for g in range(KVH):
            kc_k[g][...] = kc_ref[0, 0, g * n_cmp:(g + 1) * n_cmp, :].astype(BF16)
            kc_vt[g][...] = kc_ref[0, 1, g * n_cmp:(g + 1) * n_cmp, :].T[0:DH].astype(BF16)

        def build(i, _):
            r0 = pl.multiple_of(i * KEY_TILE, KEY_TILE)
            pos = r0 + lax.broadcasted_iota(I32, (KEY_TILE, 1), 0)
            k0, k1 = _key_rows(kvs_ref[0, pl.ds(r0, KEY_TILE), :LANES], pos)
            onehot = _block_onehot(pos, KEY_TILE)
            ka0[pl.ds(r0, KEY_TILE), :] = jnp.concatenate([k0.astype(BF16), onehot], axis=1)
            ka1[pl.ds(r0, KEY_TILE), :] = jnp.concatenate([k1.astype(BF16), onehot], axis=1)
            vt = kvs_ref[0, pl.ds(r0, KEY_TILE), LANES:].T.astype(BF16)
            vt0[:, pl.ds(r0, KEY_TILE)] = vt[0:DH]
            vt1[:, pl.ds(r0, KEY_TILE)] = vt[DH:]
            return 0

        lax.fori_loop(0, seq_len // KEY_TILE, build, 0)

    st = j * Q_BLOCK
    q_pos = st + lax.broadcasted_iota(I32, (1, lq), 1)
    q_pos4 = _tile4(q_pos)

    q_t = (q_ref[...] * ATTN_SCALE).T
    sub = lax.broadcasted_iota(I32, (DH, lq), 0)
    q_hi = jnp.left_shift(jnp.right_shift(q_pos, 7), 7).astype(F32)
    q_lo = jnp.bitwise_and(q_pos, LANES - 1).astype(F32)
    for h in range(NSA_HEADS):
        g, hl = divmod(h, HPG)
        slope = 2.0 ** (-(h + 1))
        ex = jnp.where(sub <= 1, slope,
                       jnp.where(sub == 2, -slope * q_hi,
                                 jnp.where(sub == 3, -slope * q_lo, jnp.where(sub == 4, NEG, 0.0))))
        qa_t[g][0:DH, hl * lq:(hl + 1) * lq] = q_t[h * DH:(h + 1) * DH].astype(BF16)
        qa_t[g][DH:2 * DH, hl * lq:(hl + 1) * lq] = ex.astype(BF16)

    c_end = lax.broadcasted_iota(I32, (n_cmp, 1), 0) * CMP_STRIDE + (CMP_LEN - 1)
    m_c = c_end <= q_pos4
    any_c = (q_pos4 >= CMP_LEN - 1).astype(F32)
    ovt_bf = ovt_ref[...].astype(BF16)
    o_c, imp_ts = [], []
    for g in range(KVH):
        s = jnp.where(m_c, _dot(kc_k[g][...], qa_t[g][0:LANES, :]), NEG)
        e = jnp.exp(s - jnp.max(s, axis=0, keepdims=True))
        p = e * (any_c / jnp.sum(e, axis=0, keepdims=True))
        o_c.append(_dot(kc_vt[g][...], p.astype(BF16)))
        psum = p[:, 0:lq] + p[:, lq:2 * lq] + p[:, 2 * lq:3 * lq] + p[:, 3 * lq:4 * lq]
        imp_ts.append(sum(_dot(ovt_bf, t) for t in _split3(psum)))
    bias = _top_blocks_t(imp_ts, st)
    for g in range(KVH):
        for hl in range(HPG):
            qa_t[g][2 * DH:, hl * lq:(hl + 1) * lq] = bias[g].astype(BF16)

    n_tiles = (st + Q_BLOCK + KEY_TILE - 1) // KEY_TILE
    for g in range(KVH):
        m_ref[g] = jnp.full((1, cols), NEG, F32)
        l_ref[g] = jnp.zeros((1, cols), F32)
        acc_ref[g] = jnp.zeros((DH, cols), F32)

    def scores(t, s_ref):
        r0 = pl.multiple_of(t * KEY_TILE, KEY_TILE)
        for g in range(KVH):
            s_ref[g] = _dot(kaug[g][pl.ds(r0, KEY_TILE), :], qa_t[g][...])

    def consume(t, s_ref, masked):
        r0 = pl.multiple_of(t * KEY_TILE, KEY_TILE)
        for g in range(KVH):
            s = s_ref[g]
            if masked:
                k_pos = r0 + lax.broadcasted_iota(I32, (KEY_TILE, 1), 0)
                s = jnp.where(k_pos <= q_pos4, s, NEG)
            m = m_ref[g]
            m_new = jnp.maximum(m, jnp.max(s, axis=0, keepdims=True))
            a = jnp.exp(m - m_new)
            e = jnp.exp(s - m_new)
            m_ref[g] = m_new
            l_ref[g] = a * l_ref[g] + jnp.sum(e, axis=0, keepdims=True)
            acc_ref[g] = a * acc_ref[g] + _dot(v_t[g][:, pl.ds(r0, KEY_TILE)], e.astype(BF16))

    scores(0, sa_ref)
    n_pairs = (n_tiles - 1) // 2

    def pair(u, _):
        scores(2 * u + 1, sb_ref)
        consume(2 * u, sa_ref, False)
        scores(2 * u + 2, sa_ref)
        consume(2 * u + 1, sb_ref, False)
        return 0

    lax.fori_loop(0, n_pairs, pair, 0)
    odd_tail = (n_tiles - 1) - 2 * n_pairs == 1

    @pl.when(odd_tail)
    def _():
        scores(n_tiles - 1, sb_ref)
        consume(n_tiles - 2, sa_ref, False)
        consume(n_tiles - 1, sb_ref, True)

    @pl.when(jnp.logical_not(odd_tail))
    def _():
        consume(n_tiles - 1, sa_ref, True)

    o_s = [acc_ref[g] * (1.0 / l_ref[g]) for g in range(KVH)]

    band = jnp.concatenate([w0_ref[0], w1_ref[0], w2_ref[0], w3_ref[0], w4_ref[0]], axis=0)
    n_win = band.shape[0]
    w_pos_col = st - WINDOW + lax.broadcasted_iota(I32, (n_win, 1), 0)
    kw_k = _key_rows(band[:, :LANES], jnp.maximum(w_pos_col, 0), invalid=w_pos_col < 0)
    vw_t = band[:, LANES:].T.astype(BF16)
    band_bias = _tile4(wband_ref[...])
    o_w = []
    for g in range(KVH):
        s = _dot(kw_k[g].astype(BF16), qa_t[g][0:LANES, :]) + band_bias
        e = jnp.exp(s - jnp.max(s, axis=0, keepdims=True))
        o_w.append(_dot(vw_t[g * DH:(g + 1) * DH], e.astype(BF16)) * (1.0 / jnp.sum(e, axis=0, keepdims=True)))

    def heads(per_group):
        return jnp.concatenate([per_group[h // HPG][:, (h % HPG) * lq:(h % HPG + 1) * lq]
                                for h in range(NSA_HEADS)], axis=0)

    gates = gate_ref[...]
    out_t = (_dot_nt(egt_ref[0], gates) * heads(o_c)
             + _dot_nt(egt_ref[1], gates) * heads(o_s)
             + _dot_nt(egt_ref[2], gates) * heads(o_w))
    o_ref[...] = out_t.T.astype(o_ref.dtype)


def _gate_expand_t():
    r = jnp.arange(3 * NSA_HEADS)
    c = jnp.arange(NSA_WIDTH)
    return jnp.stack([(3 * (c[:, None] // DH) + br == r[None, :]).astype(F32) for br in range(3)])


def _nsa_prompt(q, gates, kc, kvs, kvw):
    B, S, _ = kvs.shape
    nqb = S // Q_BLOCK
    n_cmp = kc.shape[2] // KVH
    ovt = _overlap_t(n_cmp)
    egt = _gate_expand_t()
    n_band = WINDOW // Q_BLOCK + 1
    d_band = jnp.arange(Q_BLOCK)[None, :] + WINDOW - jnp.arange(n_band * Q_BLOCK)[:, None]
    wband = jnp.where((d_band >= 0) & (d_band < WINDOW), 0.0, NEG).astype(F32)

    def band_spec(i):
        return pl.BlockSpec((1, Q_BLOCK, KV_WIDTH),
                            lambda b, j, i=i: (b, jnp.maximum(j - (n_band - 1) + i, 0), 0))

    full = lambda a: pl.BlockSpec(a.shape, lambda b, j: (0,) * a.ndim)
    return pl.pallas_call(
        functools.partial(_nsa_prompt_body, seq_len=S),
        grid=(B, nqb),
        in_specs=[pl.BlockSpec((Q_BLOCK, NSA_WIDTH), lambda b, j: (b * nqb + j, 0)),
                  pl.BlockSpec((Q_BLOCK, 3 * NSA_HEADS), lambda b, j: (b * nqb + j, 0)),
                  pl.BlockSpec((1,) + kc.shape[1:], lambda b, j: (b, 0, 0, 0)),
                  pl.BlockSpec((1, S, KV_WIDTH), lambda b, j: (b, 0, 0))]
                 + [band_spec(i) for i in range(n_band)] + [full(ovt), full(egt), full(wband)],
        out_specs=pl.BlockSpec((Q_BLOCK, NSA_WIDTH), lambda b, j: (b * nqb + j, 0)),
        out_shape=jax.ShapeDtypeStruct((B * S, NSA_WIDTH), BF16),
        scratch_shapes=[pltpu.VMEM((n_cmp, LANES), BF16)] * 2
                       + [pltpu.VMEM((DH, n_cmp), BF16)] * 2
                       + [pltpu.VMEM((S, 2 * LANES), BF16)] * 2
                       + [pltpu.VMEM((DH, S), BF16)] * 2
                       + [pltpu.VMEM((2 * LANES, HPG * Q_BLOCK), BF16)] * 2
                       + [pltpu.VMEM((KVH, KEY_TILE, HPG * Q_BLOCK), F32)] * 2
                       + [pltpu.VMEM((KVH, 1, HPG * Q_BLOCK), F32)] * 2
                       + [pltpu.VMEM((KVH, DH, HPG * Q_BLOCK), F32)],
        compiler_params=_params(("arbitrary", "arbitrary")),
        name="nsa_prompt",
    )(q, gates, kc, kvs, *([kvw] * n_band), ovt, egt, wband)


def _nsa_sample_body(pt_ref, q_ref, gate_ref, kvs_new_ref, win_ref, kvw_new_ref, *rest,
                     n_pages, ls, past_len):
    cmp_pages = rest[:n_pages]
    slc_pages = rest[n_pages:2 * n_pages]
    (w1_ref, pe_ref, b1_ref, w2_ref, b2_ref, ovt_ref, eg_ref, o_ref,
     full_k, full_v, kck0, kck1, kcv0, kcv1, kt0, kt1, vt0, vt1, wkt0, wkt1, qa_ref) = rest[2 * n_pages:]
    del pt_ref
    kc_k, kc_v, kaug_t, v_t, wk_t = (kck0, kck1), (kcv0, kcv1), (kt0, kt1), (vt0, vt1), (wkt0, wkt1)
    n_cmp = past_len // CMP_STRIDE
    w_rows = win_ref.shape[4]
    lq = BF16_ROWS
    rows = HPG * lq
    w_start = past_len - w_rows

    @pl.when(pl.program_id(0) == 0)
    def _():
        sub = lax.broadcasted_iota(I32, (DH, past_len), 0)
        pos = lax.broadcasted_iota(I32, (1, past_len), 1)
        ex = _key_alibi_rows(pos, sub).astype(BF16)
        onehot = (lax.broadcasted_iota(I32, (SEL_PAD, past_len), 0) == jnp.right_shift(pos, 6)).astype(BF16)
        subw = lax.broadcasted_iota(I32, (DH, w_rows), 0)
        exw = _key_alibi_rows(w_start + lax.broadcasted_iota(I32, (1, w_rows), 1), subw).astype(BF16)
        for g in range(KVH):
            kaug_t[g][DH:2 * DH, :] = ex
            kaug_t[g][2 * DH:, :] = onehot
            wk_t[g][DH:, :] = exw

    for p in range(n_pages):
        cols = slice(p * PAGE_SIZE, (p + 1) * PAGE_SIZE)
        full_k[cols, :] = cmp_pages[p][0, 0].reshape(2 * DH, PAGE_SIZE).T
        full_v[cols, :] = cmp_pages[p][0, 1].reshape(2 * DH, PAGE_SIZE).T
        for g in range(KVH):
            kaug_t[g][0:DH, cols] = slc_pages[p][0, 0, g].astype(BF16)
            vt = slc_pages[p][0, 1, g].astype(BF16)
            v_t[g][0:DH, cols] = vt
            v_t[g][DH:, cols] = vt
    for g in range(KVH):
        wk_t[g][0:DH, :] = win_ref[0, 0, g].astype(BF16)

    keys_c, values_c = _compress((full_k, full_v), n_cmp, w1_ref, pe_ref, b1_ref, w2_ref, b2_ref)
    for g in range(KVH):
        kc_k[g][...] = keys_c[g * n_cmp:(g + 1) * n_cmp].astype(BF16)
        kc_v[g][...] = values_c[g * n_cmp:(g + 1) * n_cmp].astype(BF16)

    pad_q = jnp.zeros((lq - ls, NSA_WIDTH), F32)
    q_pos = past_len + lax.broadcasted_iota(I32, (lq, 1), 0)
    q_pos4 = _stack4(q_pos)
    _fill_queries(qa_ref, jnp.concatenate([q_ref[0] * ATTN_SCALE, pad_q], axis=0), q_pos, lq)
    gates = jnp.concatenate([gate_ref[0], jnp.zeros((lq - ls, 3 * NSA_HEADS), F32)], axis=0)

    o_c, psums = _cmp_branch(qa_ref, kc_k, kc_v, q_pos4, lq, n_cmp)
    n_sel = -(-(past_len + lq) // SEL_LEN)
    n_blk = -(-n_sel // 8) * 8
    _store_selbias(qa_ref, _select_blocks(psums, ovt_ref, past_len, lq, n_blk), lq)

    pad_k = jnp.zeros((LANES - ls, KV_WIDTH), F32)
    new_pos_col = past_len + lax.broadcasted_iota(I32, (LANES, 1), 0)
    new_pos = past_len + lax.broadcasted_iota(I32, (1, LANES), 1)
    new_s = jnp.concatenate([kvs_new_ref[0], pad_k], axis=0)
    new_w = jnp.concatenate([kvw_new_ref[0], pad_k], axis=0)
    ks_new = _key_rows(new_s[:, :LANES], new_pos_col)
    vs_new = _halves(new_s[:, LANES:])
    kw_new = _key_rows(new_w[:, :LANES], new_pos_col)
    vw_new = _halves(new_w[:, LANES:])
    onehot_new = _block_onehot(new_pos_col, LANES)
    causal_new = q_pos4 >= new_pos

    o_s, o_w = [], []
    d_past = q_pos4 - (w_start + lax.broadcasted_iota(I32, (1, w_rows), 1))
    m_past = (d_past >= 0) & (d_past < WINDOW)
    d_new = q_pos4 - new_pos
    m_new = (d_new >= 0) & (d_new < WINDOW)
    for g in range(KVH):
        s_past = _dot(qa_ref[g], kaug_t[g][...])
        k_new = jnp.concatenate([ks_new[g].astype(BF16), onehot_new], axis=1)
        s_new = jnp.where(causal_new, _dot_nt(qa_ref[g], k_new), NEG)
        m = jnp.maximum(jnp.max(s_past, axis=-1, keepdims=True), jnp.max(s_new, axis=-1, keepdims=True))
        e_past = jnp.exp(s_past - m)
        e_new = jnp.exp(s_new - m)
        den = jnp.sum(e_past, axis=-1, keepdims=True) + jnp.sum(e_new, axis=-1, keepdims=True)
        acc = _dot_nt(e_past.astype(BF16), v_t[g][...]) + _dot(e_new.astype(BF16), vs_new[g].astype(BF16))
        o_s.append(acc / den)

        sw_past = jnp.where(m_past, _dot(qa_ref[g, :, 0:LANES], wk_t[g][...]), NEG)
        sw_new = jnp.where(m_new, _dot_nt(qa_ref[g, :, 0:LANES], kw_new[g].astype(BF16)), NEG)
        m = jnp.maximum(jnp.max(sw_past, axis=-1, keepdims=True), jnp.max(sw_new, axis=-1, keepdims=True))
        e_past = jnp.exp(sw_past - m)
        e_new = jnp.exp(sw_new - m)
        den = jnp.sum(e_past, axis=-1, keepdims=True) + jnp.sum(e_new, axis=-1, keepdims=True)
        vw = win_ref[0, 1, g].astype(BF16)
        vw2 = jnp.concatenate([vw, vw], axis=0)
        acc = _dot_nt(e_past.astype(BF16), vw2) + _dot(e_new.astype(BF16), vw_new[g].astype(BF16))
        o_w.append(acc / den)

    o_ref[0] = _combine(gates, eg_ref, o_c, o_s, o_w, lq)[:ls]


def _nsa_sample(q, gates, kvs_new, kvw_new, win_t, cmp_t, slc_t, page_table, cw, past_len):
    Bd, ls, _ = q.shape
    n_pages = page_table.shape[1]
    n_cmp = past_len // CMP_STRIDE
    ovt = _overlap_t(n_cmp)
    eg = _gate_expand()
    w_rows = win_t.shape[4]

    per_b = lambda a: pl.BlockSpec((1,) + a.shape[1:], lambda b, pt: (b,) + (0,) * (a.ndim - 1))
    full = lambda a: pl.BlockSpec(a.shape, lambda b, pt: (0,) * a.ndim)
    page = lambda p: pl.BlockSpec((1, 2, KVH, DH, PAGE_SIZE), lambda b, pt, p=p: (pt[b, p], 0, 0, 0, 0))
    grid_spec = pltpu.PrefetchScalarGridSpec(
        num_scalar_prefetch=1,
        grid=(Bd,),
        in_specs=[per_b(q), per_b(gates), per_b(kvs_new), per_b(win_t), per_b(kvw_new)]
                 + [page(p) for p in range(n_pages)] * 2
                 + [full(a) for a in cw] + [full(ovt), full(eg)],
        out_specs=pl.BlockSpec((1, ls, NSA_WIDTH), lambda b, pt: (b, 0, 0)),
        scratch_shapes=[pltpu.VMEM((past_len, LANES), F32)] * 2
                       + [pltpu.VMEM((n_cmp, LANES), BF16)] * 4
                       + [pltpu.VMEM((2 * LANES, past_len), BF16)] * 2
                       + [pltpu.VMEM((LANES, past_len), BF16)] * 2
                       + [pltpu.VMEM((LANES, w_rows), BF16)] * 2
                       + [pltpu.VMEM((KVH, HPG * BF16_ROWS, 2 * LANES), BF16)],
    )
    return pl.pallas_call(
        functools.partial(_nsa_sample_body, n_pages=n_pages, ls=ls, past_len=past_len),
        grid_spec=grid_spec,
        out_shape=jax.ShapeDtypeStruct((Bd, ls, NSA_WIDTH), F32),
        compiler_params=_params(("arbitrary",)),
        name="nsa_sample",
    )(page_table, q, gates, kvs_new, win_t, kvw_new,
      *([cmp_t] * n_pages), *([slc_t] * n_pages), *cw, ovt, eg)


def _fin1_body(h_ref, pool_ref, nsa_ref, wo_ref, g_ref, b_ref, wq_ref, h1_ref, qm_ref):
    mix = (_dot(pool_ref[...].astype(BF16), wo_ref[0:POOL_WIDTH, :])
           + _dot(nsa_ref[...].astype(BF16), wo_ref[POOL_WIDTH:, :]))
    h1 = _layer_norm(DN_ALPHA * h_ref[...] + mix, g_ref[...], b_ref[...])
    h1_ref[...] = h1
    qm_ref[...] = (_dot(h1.astype(BF16), wq_ref[...]) * (MEM_HEAD_DIM ** -0.5)).astype(qm_ref.dtype)


def _fin1(h, pool_o, nsa_o, w_out_bf, g, b, wq_bf, q_dtype):
    T = h.shape[0]
    tm = ROW_TILE
    row = lambda n: pl.BlockSpec((tm, n), lambda i: (i, 0))
    full = lambda a: pl.BlockSpec(a.shape, lambda i: (0,) * a.ndim)
    return pl.pallas_call(
        _fin1_body,
        grid=(T // tm,),
        in_specs=[row(D_MODEL), row(POOL_WIDTH), row(NSA_WIDTH), full(w_out_bf), full(g), full(b), full(wq_bf)],
        out_specs=[row(D_MODEL), row(D_MODEL)],
        out_shape=[jax.ShapeDtypeStruct((T, D_MODEL), F32), jax.ShapeDtypeStruct((T, D_MODEL), q_dtype)],
        compiler_params=_params(("arbitrary",)),
        name="out_proj_ln1",
    )(h, pool_o, nsa_o, w_out_bf, g, b, wq_bf)


def _memattn_body(q_ref, kv_ref, o_ref):
    width = MEM_HEADS * MEM_HEAD_DIM
    for h in range(MEM_HEADS):
        cols = slice(h * MEM_HEAD_DIM, (h + 1) * MEM_HEAD_DIM)
        qh = q_ref[0, :, cols].astype(BF16)
        kh = kv_ref[0, :, cols].astype(BF16)
        vh = kv_ref[0, :, width + h * MEM_HEAD_DIM:width + (h + 1) * MEM_HEAD_DIM].astype(BF16)
        s = _dot_nt(qh, kh)
        e = jnp.exp(s - jnp.max(s, axis=-1, keepdims=True))
        o = _dot(e.astype(BF16), vh) * (1.0 / jnp.sum(e, axis=-1, keepdims=True))
        o_ref[0, :, cols] = o.astype(o_ref.dtype)


def _memattn(qm, mem_kv, tq):
    nb, L, W = qm.shape
    return pl.pallas_call(
        _memattn_body,
        grid=(nb, L // tq),
        in_specs=[pl.BlockSpec((1, tq, W), lambda b, t: (b, t, 0)),
                  pl.BlockSpec((1, MEM_LEN, 2 * W), lambda b, t: (b, 0, 0))],
        out_specs=pl.BlockSpec((1, tq, W), lambda b, t: (b, t, 0)),
        out_shape=jax.ShapeDtypeStruct((nb, L, W), BF16),
        compiler_params=_params(("arbitrary", "arbitrary")),
        name="mem_attn",
    )(qm, mem_kv)


def _memattn_few_body(q_ref, kv_ref, o_ref):
    lq = q_ref.shape[1]
    n_keys = MEM_LEN * MEM_HEADS
    q = q_ref[0]
    qs = jnp.concatenate([q[:, h * MEM_HEAD_DIM:(h + 1) * MEM_HEAD_DIM] for h in range(MEM_HEADS)], axis=0)
    k = kv_ref[0, :, 0, :, :].reshape(n_keys, MEM_HEAD_DIM).astype(BF16)
    v = kv_ref[0, :, 1, :, :].reshape(n_keys, MEM_HEAD_DIM).astype(BF16)
    s = _dot_nt(qs.astype(BF16), k)
    assert lq & (lq - 1) == 0 and MEM_HEADS & (MEM_HEADS - 1) == 0
    col_h = jnp.bitwise_and(lax.broadcasted_iota(I32, s.shape, 1), MEM_HEADS - 1)
    row_h = jnp.right_shift(lax.broadcasted_iota(I32, s.shape, 0), lq.bit_length() - 1)
    s = jnp.where(col_h == row_h, s, NEG)
    e = jnp.exp(s - jnp.max(s, axis=-1, keepdims=True))
    o = _dot(e.astype(BF16), v) * (1.0 / jnp.sum(e, axis=-1, keepdims=True))
    for h in range(MEM_HEADS):
        o_ref[0, :, h * MEM_HEAD_DIM:(h + 1) * MEM_HEAD_DIM] = o[h * lq:(h + 1) * lq]


def _memattn_few(qm, mem_kv):
    nb, lq, W = qm.shape
    return pl.pallas_call(
        _memattn_few_body,
        grid=(nb,),
        in_specs=[pl.BlockSpec((1, lq, W), lambda b: (b, 0, 0)),
                  pl.BlockSpec((1, MEM_LEN, 2, MEM_HEADS, MEM_HEAD_DIM), lambda b: (b, 0, 0, 0, 0))],
        out_specs=pl.BlockSpec((1, lq, W), lambda b: (b, 0, 0)),
        out_shape=jax.ShapeDtypeStruct((nb, lq, W), F32),
        compiler_params=_params(("arbitrary",)),
        name="mem_attn_few",
    )(qm, mem_kv)


def _fin2_body(cnt0_ref, h1_ref, o_ref, wo_ref, g_ref, b_ref, rw_ref, rb_ref, *rest):
    h2_ref, te_ref, tg_ref, cnt_ref, run_ref = rest[-5:]
    tm = h1_ref.shape[0]

    @pl.when(pl.program_id(0) == 0)
    def _():
        run_ref[...] = cnt0_ref[...]

    a = _dot(o_ref[...].astype(BF16), wo_ref[...])
    h2 = _layer_norm(DN_ALPHA * h1_ref[...] + a, g_ref[...], b_ref[...])
    h2_ref[...] = h2
    logits = jnp.dot(h2, rw_ref[...], preferred_element_type=F32, precision=HIGHEST) + rb_ref[...]
    e_iota = lax.broadcasted_iota(I32, (tm, N_EXPERTS), 1).astype(F32)
    lane = lax.broadcasted_iota(I32, (tm, LANES), 1)
    te = jnp.zeros((tm, LANES), F32)
    tv = jnp.full((tm, LANES), NEG, F32)
    work = logits
    chosen = []
    for k in range(TOP_K):
        m = jnp.max(work, axis=-1, keepdims=True)
        idx = jnp.min(jnp.where(work == m, e_iota, float(N_EXPERTS)), axis=-1, keepdims=True)
        hit = e_iota == idx
        chosen.append(hit)
        te = jnp.where(lane == k, idx, te)
        tv = jnp.where(lane == k, m, tv)
        work = jnp.where(hit, -jnp.inf, work)
    member = sum(c.astype(F32) for c in chosen)
    earlier = (lax.broadcasted_iota(I32, (tm, tm), 0) > lax.broadcasted_iota(I32, (tm, tm), 1)).astype(BF16)
    before = _dot(earlier, member.astype(BF16)) + run_ref[...]
    for k in range(TOP_K):
        rank = jnp.sum(jnp.where(chosen[k], before, 0.0), axis=-1, keepdims=True)
        te = jnp.where(lane == TOP_K + k, rank, te)
    run_ref[...] = run_ref[...] + jnp.sum(member, axis=0, keepdims=True)
    cnt_ref[...] = run_ref[...]
    ex = jnp.exp(tv - jnp.max(tv, axis=-1, keepdims=True))
    te_ref[...] = te.astype(I32)
    tg_ref[...] = ex / jnp.sum(ex, axis=-1, keepdims=True)


def _fin2(cnt0, h1, o, wo_bf, g, b, rw, rb, total_rows, row_offset=0, into=None):
    T = h1.shape[0]
    tm = ROW_TILE
    blk0 = row_offset // tm
    row = lambda n: pl.BlockSpec((tm, n), lambda i: (i, 0))
    out_row = lambda n: pl.BlockSpec((tm, n), lambda i: (i + blk0, 0))
    full = lambda a: pl.BlockSpec(a.shape, lambda i: (0,) * a.ndim)
    ins = [cnt0, h1, o, wo_bf, g, b, rw, rb]
    in_specs = [full(cnt0), row(D_MODEL), row(D_MODEL), full(wo_bf), full(g), full(b), full(rw), full(rb)]
    aliases = {}
    if into is not None:
        aliases = {len(ins) + k: k for k in range(len(into))}
        in_specs = in_specs + [pl.BlockSpec(memory_space=pl.ANY)] * len(into)
        ins = ins + list(into)
    return pl.pallas_call(
        _fin2_body,
        grid=(T // tm,),
        in_specs=in_specs,
        out_specs=[out_row(D_MODEL), out_row(LANES), out_row(LANES), full(cnt0)],
        out_shape=[jax.ShapeDtypeStruct((total_rows, D_MODEL), F32), jax.ShapeDtypeStruct((total_rows, LANES), I32),
                   jax.ShapeDtypeStruct((total_rows, LANES), F32), jax.ShapeDtypeStruct(cnt0.shape, F32)],
        scratch_shapes=[pltpu.VMEM(cnt0.shape, F32)],
        input_output_aliases=aliases,
        compiler_params=_params(("arbitrary",)),
        name="mem_out_ln2_router",
    )(*ins)


def _moe_body(ut_ref, ue_ref, nu_ref, rs_ref, re_ref, x_ref, rw_ref, wgu_ref, bgu_ref, wdn_ref, bdn_ref,
              y_ref, wgu_bf, wdn_bf):
    u = pl.program_id(0)
    bk = x_ref.shape[0]
    e = ue_ref[u]
    tile = ut_ref[u]
    prev = jnp.maximum(u - 1, 0)

    @pl.when((u == 0) | (e != ue_ref[prev]))
    def _():
        wgu_bf[...] = wgu_ref[0].astype(BF16)
        wdn_bf[...] = wdn_ref[0].astype(BF16)

    @pl.when(u < nu_ref[0])
    def _():
        x = x_ref[...].astype(BF16)
        g = _dot(x, wgu_bf[:, :D_FF]) + bgu_ref[0, :, :D_FF]
        v = _dot(x, wgu_bf[:, D_FF:]) + bgu_ref[0, :, D_FF:]
        g = jnp.minimum(g, SWIGLU_LIMIT)
        v = jnp.clip(v, -SWIGLU_LIMIT, SWIGLU_LIMIT)
        a = g * (1.0 / (1.0 + jnp.exp(-SWIGLU_ALPHA * g))) * (v + 1.0)
        y = _dot(a.astype(BF16), wdn_bf[...]) + bdn_ref[0]
        row = tile * bk + lax.broadcasted_iota(I32, (bk, 1), 0)
        mine = (row >= rs_ref[e]) & (row < re_ref[e])
        y = jnp.where(mine, y * rw_ref[...], 0.0)

        @pl.when((u == 0) | (tile != ut_ref[prev]))
        def _():
            y_ref[...] = y

        @pl.when((u > 0) & (tile == ut_ref[prev]))
        def _():
            y_ref[...] = y_ref[...] + y


def _moe_gmm(x_rows, row_w, units, w_gu, b_gu, w_dn, b_dn):
    N = x_rows.shape[0]
    bk = MOE_ROWS
    unit_tile, unit_e, n_units, r_start, r_end = units
    grid_spec = pltpu.PrefetchScalarGridSpec(
        num_scalar_prefetch=5,
        grid=(unit_tile.shape[0],),
        in_specs=[pl.BlockSpec((bk, D_MODEL), lambda u, ut, ue, *_: (ut[u], 0)),
                  pl.BlockSpec((bk, 1), lambda u, ut, ue, *_: (ut[u], 0)),
                  pl.BlockSpec((1, D_MODEL, 2 * D_FF), lambda u, ut, ue, *_: (ue[u], 0, 0)),
                  pl.BlockSpec((1, 1, 2 * D_FF), lambda u, ut, ue, *_: (ue[u], 0, 0)),
                  pl.BlockSpec((1, D_FF, D_MODEL), lambda u, ut, ue, *_: (ue[u], 0, 0)),
                  pl.BlockSpec((1, 1, D_MODEL), lambda u, ut, ue, *_: (ue[u], 0, 0))],
        out_specs=pl.BlockSpec((bk, D_MODEL), lambda u, ut, ue, *_: (ut[u], 0)),
        scratch_shapes=[pltpu.VMEM((D_MODEL, 2 * D_FF), BF16), pltpu.VMEM((D_FF, D_MODEL), BF16)],
    )
    return pl.pallas_call(
        _moe_body,
        grid_spec=grid_spec,
        out_shape=jax.ShapeDtypeStruct((N, D_MODEL), F32),
        compiler_params=_params(("arbitrary",)),
        name="moe_experts",
    )(unit_tile, unit_e, n_units, r_start, r_end, x_rows, row_w, w_gu, b_gu, w_dn, b_dn)


FLAT_BITS = 17


def _moe_routing(te, tg, counts):
    bk = MOE_ROWS
    T = te.shape[0]
    N = T * TOP_K
    assert N % bk == 0 and N <= (1 << FLAT_BITS)
    experts = jnp.arange(N_EXPERTS, dtype=I32)
    top_e = te[:, :TOP_K]
    r_end = jnp.cumsum(counts).astype(I32)
    r_start = r_end - counts
    onehot = top_e[:, :, None] == experts[None, None, :]
    pos = jnp.sum(jnp.where(onehot, r_start[None, None, :], 0), axis=-1) + te[:, TOP_K:2 * TOP_K]
    key = jnp.left_shift(top_e.reshape(-1), FLAT_BITS) + jnp.arange(N, dtype=I32)
    key_s, gate_s = lax.sort((key, tg[:, :TOP_K].reshape(-1)), num_keys=1)
    tok_s = jnp.right_shift(jnp.bitwise_and(key_s, (1 << FLAT_BITS) - 1), 2)
    first = r_start // bk
    last = (r_end - 1) // bk
    n_e = jnp.where(counts > 0, last - first + 1, 0)
    u_end = jnp.cumsum(n_e).astype(I32)
    u_start = u_end - n_e
    n_units = u_end[-1]
    u = jnp.minimum(jnp.arange(N // bk + N_EXPERTS - 1, dtype=I32), n_units - 1)
    unit_e = jnp.sum((u[:, None] >= u_end[None, :]).astype(I32), axis=1)
    unit_tile = u + jnp.sum(jnp.where(unit_e[:, None] == experts[None, :], (first - u_start)[None, :], 0), axis=1)
    return pos, tok_s, gate_s, (unit_tile, unit_e, n_units.reshape(1), r_start, r_end)


def _fin3_body(h2_ref, y0_ref, y1_ref, y2_ref, y3_ref, g_ref, b_ref, o_ref):
    y = (y0_ref[...] + y1_ref[...]) + (y2_ref[...] + y3_ref[...])
    o_ref[...] = _layer_norm(DN_ALPHA * h2_ref[...] + y, g_ref[...], b_ref[...])


def _fin3(h2, ys, g, b, row_offset, T):
    tm = ROW_TILE
    blk0 = row_offset // tm
    row = lambda n: pl.BlockSpec((tm, n), lambda i: (i, 0))
    full = lambda a: pl.BlockSpec(a.shape, lambda i: (0,) * a.ndim)
    return pl.pallas_call(
        _fin3_body,
        grid=(T // tm,),
        in_specs=[pl.BlockSpec((tm, D_MODEL), lambda i: (i + blk0, 0))] * (1 + TOP_K) + [full(g), full(b)],
        out_specs=row(D_MODEL),
        out_shape=jax.ShapeDtypeStruct((T, D_MODEL), F32),
        compiler_params=_params(("arbitrary",)),
        name="combine_ln3",
    )(h2, *ys, g, b)


def kernel(x_prompt, x_sample, cache_cmp_kv, cache_slc_kv, state_win_kv, state_pool, cache_mem_kv, page_table,
           mem_prompt, w_in, pool_w, pool_scale, cmp_pe, cmp_w1, cmp_b1, cmp_w2, cmp_b2, w_out, ln1_g, ln1_b,
           mem_wq, mem_wkv, mem_wo, ln2_g, ln2_b, router_w, router_b, exp_w_gu, exp_b_gu, exp_w_dn, exp_b_dn,
           ln3_g, ln3_b):
    Bp, S, D = x_prompt.shape
    Bd, Ls, _ = x_sample.shape
    Tp, Ts = Bp * S, Bd * Ls
    l = 0
    w_in_bf = w_in[l].astype(BF16)
    pool_w_bf = pool_w[l].astype(BF16)
    ps = pool_scale[l][None, :]
    cw = _compress_weights(cmp_pe[l], cmp_w1[l], cmp_b1[l], cmp_w2[l], cmp_b2[l])
    w_out_bf = w_out[l].astype(BF16)
    wq_bf = mem_wq[l].astype(BF16)
    wo_bf = mem_wo[l].astype(BF16)
    vec = lambda a: a[l][None, :]

    up, qp, kvc_p, kvs_p, kvw_p, gp, pool_p, kvc_t, kvs_t = _inproj_prompt(
        x_prompt.reshape(Tp, D), w_in_bf, pool_w_bf, ps, S)
    kc_p = _compress_prompt(kvc_p.reshape(Bp, S, KV_WIDTH), cw)
    nsa_p = _nsa_prompt(qp, gp, kc_p, kvs_p.reshape(Bp, S, KV_WIDTH), kvw_p.reshape(Bp, S, KV_WIDTH))
    mem_kv_p = _matmul(mem_prompt.reshape(Bp * MEM_LEN, D), mem_wkv[l]).reshape(Bp, MEM_LEN, 2 * D)
    h1_p, qm_p = _fin1(x_prompt.reshape(Tp, D), pool_p, nsa_p, w_out_bf, vec(ln1_g), vec(ln1_b), wq_bf, BF16)
    om_p = _memattn(qm_p.reshape(Bp, S, D), mem_kv_p, ROW_TILE).reshape(Tp, D)
    T = Tp + Ts
    *routed_p, cnt_p = _fin2(jnp.zeros((1, N_EXPERTS), F32), h1_p, om_p, wo_bf, vec(ln2_g), vec(ln2_b),
                             router_w[l], vec(router_b), total_rows=T)

    state_pad = jnp.pad(state_pool[l], ((0, 0), (1, 0), (0, 0)))
    us, qs, kvc_s, kvs_s, kvw_s, gs, pool_s = _inproj_sample(
        x_sample.reshape(Ts, D), w_in_bf, pool_w_bf, ps, state_pad, Ls, PAST_LEN)
    w_rows = state_win_kv.shape[2]
    feature_major = lambda a: jnp.transpose(a, (0, 2, 3, 4, 1))
    nsa_s = _nsa_sample(qs.reshape(Bd, Ls, NSA_WIDTH), gs.reshape(Bd, Ls, 3 * NSA_HEADS),
                        kvs_s.reshape(Bd, Ls, KV_WIDTH), kvw_s.reshape(Bd, Ls, KV_WIDTH),
                        feature_major(state_win_kv[l]), feature_major(cache_cmp_kv[l]),
                        feature_major(cache_slc_kv[l]), page_table, cw, PAST_LEN)
    h1_s, qm_s = _fin1(x_sample.reshape(Ts, D), pool_s, nsa_s.reshape(Ts, NSA_WIDTH), w_out_bf,
                       vec(ln1_g), vec(ln1_b), wq_bf, F32)
    om_s = _memattn_few(qm_s.reshape(Bd, Ls, D), cache_mem_kv[l]).reshape(Ts, D)
    h2, te, tg, cnt_s = _fin2(cnt_p, h1_s, om_s, wo_bf, vec(ln2_g), vec(ln2_b), router_w[l], vec(router_b),
                              total_rows=T, row_offset=Tp, into=routed_p)

    pos, tok_s, gate_s, units = _moe_routing(te, tg, cnt_s[0].astype(I32))
    y_rows = _moe_gmm(h2[tok_s], gate_s[:, None], units, exp_w_gu[l], exp_b_gu[l][:, None, :],
                      exp_w_dn[l], exp_b_dn[l][:, None, :])
    ys = [y_rows[pos[:, k]] for k in range(TOP_K)]
    y_prompt = _fin3(h2, ys, vec(ln3_g), vec(ln3_b), 0, Tp).reshape(Bp, S, D)
    y_sample = _fin3(h2, ys, vec(ln3_g), vec(ln3_b), Tp, Ts).reshape(Bd, Ls, D)

    kv6 = lambda a, b, n: a.reshape(1, b, n, 2, KVH, DH)
    row_major = lambda a: jnp.transpose(a, (0, 4, 1, 2, 3))
    win_p = kvw_p.reshape(Bp, S, KV_WIDTH)[:, S - min(WINDOW, S):]
    win_s = jnp.concatenate([state_win_kv[l], kvw_s.reshape(Bd, Ls, 2, KVH, DH)], axis=1)[:, -w_rows:]
    pool_state_p = up.reshape(Bp, S, POOL_WIDTH)[:, S - POOL_STATE:]
    pool_state_s = jnp.concatenate([state_pool[l], us.reshape(Bd, Ls, POOL_WIDTH)], axis=1)[:, -POOL_STATE:]
    return (y_prompt, y_sample,
            row_major(kvc_t)[None], row_major(kvs_t)[None], kv6(win_p, Bp, min(WINDOW, S)),
            pool_state_p[None], mem_kv_p.reshape(1, Bp, MEM_LEN, 2, MEM_HEADS, MEM_HEAD_DIM),
            kv6(kvc_s, Bd, Ls), kv6(kvs_s, Bd, Ls), win_s[None], pool_state_s[None])
```

```python
import functools

import jax
import jax.numpy as jnp
from jax import lax
from jax.experimental import pallas as pl
from jax.experimental.pallas import tpu as pltpu

F32 = jnp.float32
BF16 = jnp.bfloat16
I32 = jnp.int32

D_MODEL = 1024
POOL_WIDTH = 512
POOL_WINDOWS = (2, 4, 8, 16)
POOL_GROUP = 128
POOL_STATE = 15
NSA_WIDTH = 512
DH = 64
NSA_HEADS = 8
KVH = 2
HPG = 4
CMP_LEN = 32
CMP_STRIDE = 16
CMP_HIDDEN = 256
SEL_LEN = 64
SEL_TOP = 16
WINDOW = 512
Q_BLOCK = 128
KV_WIDTH = 256
IN_WIDTH = 1816
ATTN_SCALE = DH ** -0.5
FORCED_SCORE = 1e4
NEG = -1e30
MEM_LEN = 256
MEM_HEADS = 4
MEM_HEAD_DIM = 256
N_EXPERTS = 32
TOP_K = 4
D_FF = 1024
SWIGLU_LIMIT = 7.0
SWIGLU_ALPHA = 1.702
DN_ALPHA = 2.0 ** 0.25
LN_EPS = 1e-5
PAST_LEN = 2048
PAGE_SIZE = 128

LANES = 128
SEL_PAD = 128
KEY_TILE = 512
ROW_TILE = 512
MOE_ROWS = 512
BF16_ROWS = 16
VMEM_LIMIT = 56 * 1024 * 1024

HIGHEST = lax.Precision.HIGHEST


def _dot(a, b):
    return jnp.dot(a, b, preferred_element_type=F32)


def _dot_nt(a, b, precision=None):
    return lax.dot_general(a, b, (((1,), (1,)), ((), ())), preferred_element_type=F32,
                           precision=precision)


def _layer_norm(x, g, b):
    mu = jnp.mean(x, axis=-1, keepdims=True)
    xc = x - mu
    var = jnp.mean(xc * xc, axis=-1, keepdims=True)
    return xc * lax.rsqrt(var + LN_EPS) * g + b


def _params(sem, vmem=VMEM_LIMIT):
    return pltpu.CompilerParams(dimension_semantics=sem, vmem_limit_bytes=vmem)


def _split_store(u, up_ref, q_ref, kvc_ref, kvs_ref, kvw_ref, gate_ref):
    o1 = POOL_WIDTH
    o2 = o1 + NSA_WIDTH
    o3 = o2 + KV_WIDTH
    o4 = o3 + KV_WIDTH
    o5 = o4 + KV_WIDTH
    up_ref[...] = u[:, :o1]
    q_ref[...] = u[:, o1:o2]
    kvc_ref[...] = u[:, o2:o3]
    kvs_ref[...] = u[:, o3:o4]
    kvw_ref[...] = u[:, o4:o5]
    gate_ref[...] = 1.0 / (1.0 + jnp.exp(-u[:, o5:]))


def _inproj_prompt_body(x_ref, w_ref, pw_ref, ps_ref,
                        up_ref, q_ref, kvc_ref, kvs_ref, kvw_ref, gate_ref, pool_ref, kvct_ref, kvst_ref,
                        ext_ref, *, tm, tiles_per_seq):
    halo = POOL_STATE + 1
    t_in_seq = pl.program_id(0) % tiles_per_seq
    u = _dot(x_ref[...].astype(BF16), w_ref[...])
    _split_store(u, up_ref, q_ref, kvc_ref, kvs_ref, kvw_ref, gate_ref)
    o2 = POOL_WIDTH + NSA_WIDTH
    kvct_ref[0] = u[:, o2:o2 + KV_WIDTH].T.reshape(2, KVH, DH, tm)
    kvst_ref[0] = u[:, o2 + KV_WIDTH:o2 + 2 * KV_WIDTH].T.reshape(2, KVH, DH, tm)

    @pl.when(t_in_seq == 0)
    def _():
        ext_ref[0:halo, :] = jnp.zeros((halo, POOL_WIDTH), F32)

    ext_ref[halo:halo + tm, :] = u[:, :POOL_WIDTH]
    pos = t_in_seq * tm + lax.broadcasted_iota(I32, (tm, 1), 0)
    for gi, w in enumerate(POOL_WINDOWS):
        cols = slice(gi * POOL_GROUP, (gi + 1) * POOL_GROUP)
        acc = ext_ref[halo:halo + tm, cols]
        for k in range(1, w):
            acc = acc + ext_ref[halo - k:halo - k + tm, cols]
        cnt = jnp.minimum(pos + 1, w).astype(F32)
        d = acc / cnt - ext_ref[halo:halo + tm, cols]
        o = _dot(d.astype(BF16), pw_ref[gi])
        pool_ref[:, cols] = (o * ps_ref[:, cols]).astype(pool_ref.dtype)
    ext_ref[0:halo, :] = ext_ref[tm:tm + halo, :]


def _inproj_prompt(x2d, w_in_bf, pool_w_bf, pool_scale, seq_len):
    T = x2d.shape[0]
    tm = ROW_TILE
    outs = [POOL_WIDTH, NSA_WIDTH, KV_WIDTH, KV_WIDTH, KV_WIDTH, 3 * NSA_HEADS, POOL_WIDTH]
    row = lambda n: pl.BlockSpec((tm, n), lambda i: (i, 0))
    full = lambda a: pl.BlockSpec(a.shape, lambda i: (0,) * a.ndim)
    tps = seq_len // tm
    kvt_spec = pl.BlockSpec((1, 2, KVH, DH, tm), lambda i: (i // tps, 0, 0, 0, i % tps))
    kvt_shape = jax.ShapeDtypeStruct((T // seq_len, 2, KVH, DH, seq_len), F32)
    return pl.pallas_call(
        functools.partial(_inproj_prompt_body, tm=tm, tiles_per_seq=tps),
        grid=(T // tm,),
        in_specs=[row(D_MODEL), full(w_in_bf), full(pool_w_bf), full(pool_scale)],
        out_specs=[row(n) for n in outs] + [kvt_spec] * 2,
        out_shape=[jax.ShapeDtypeStruct((T, n), F32) for n in outs[:-1]]
                  + [jax.ShapeDtypeStruct((T, outs[-1]), BF16)] + [kvt_shape] * 2,
        scratch_shapes=[pltpu.VMEM((tm + POOL_STATE + 1, POOL_WIDTH), F32)],
        compiler_params=_params(("arbitrary",)),
        name="inproj_prompt",
    )(x2d, w_in_bf, pool_w_bf, pool_scale)


def _inproj_sample_body(x_ref, w_ref, pw_ref, ps_ref, st_ref,
                        up_ref, q_ref, kvc_ref, kvs_ref, kvw_ref, gate_ref, pool_ref,
                        ext_ref, *, nb, ls, pos0):
    halo = POOL_STATE + 1
    tm = nb * ls
    u = _dot(x_ref[...].astype(BF16), w_ref[...])
    _split_store(u, up_ref, q_ref, kvc_ref, kvs_ref, kvw_ref, gate_ref)
    ext_ref[:, 0:halo, :] = st_ref[...]
    ext_ref[:, halo:halo + ls, :] = u[:, :POOL_WIDTH].reshape(nb, ls, POOL_WIDTH)
    pos = pos0 + lax.broadcasted_iota(I32, (1, ls, 1), 1)
    for gi, w in enumerate(POOL_WINDOWS):
        cols = slice(gi * POOL_GROUP, (gi + 1) * POOL_GROUP)
        acc = ext_ref[:, halo:halo + ls, cols]
        for k in range(1, w):
            acc = acc + ext_ref[:, halo - k:halo - k + ls, cols]
        cnt = jnp.minimum(pos + 1, w).astype(F32)
        d = acc / cnt - ext_ref[:, halo:halo + ls, cols]
        o = _dot(d.reshape(tm, POOL_GROUP).astype(BF16), pw_ref[gi])
        pool_ref[:, cols] = (o * ps_ref[:, cols]).astype(pool_ref.dtype)


def _inproj_sample(x2d, w_in_bf, pool_w_bf, pool_scale, state_pad, ls, pos0):
    T = x2d.shape[0]
    nb = ROW_TILE // ls
    tm = nb * ls
    outs = [POOL_WIDTH, NSA_WIDTH, KV_WIDTH, KV_WIDTH, KV_WIDTH, 3 * NSA_HEADS, POOL_WIDTH]
    row = lambda n: pl.BlockSpec((tm, n), lambda i: (i, 0))
    full = lambda a: pl.BlockSpec(a.shape, lambda i: (0,) * a.ndim)
    return pl.pallas_call(
        functools.partial(_inproj_sample_body, nb=nb, ls=ls, pos0=pos0),
        grid=(T // tm,),
        in_specs=[row(D_MODEL), full(w_in_bf), full(pool_w_bf), full(pool_scale),
                  pl.BlockSpec((nb, POOL_STATE + 1, POOL_WIDTH), lambda i: (i, 0, 0))],
        out_specs=[row(n) for n in outs],
        out_shape=[jax.ShapeDtypeStruct((T, n), F32) for n in outs],
        scratch_shapes=[pltpu.VMEM((nb, POOL_STATE + 1 + ls, POOL_WIDTH), F32)],
        compiler_params=_params(("arbitrary",)),
        name="inproj_sample",
    )(x2d, w_in_bf, pool_w_bf, pool_scale, state_pad)


def _matmul_body(x_ref, w_ref, o_ref):
    o_ref[...] = _dot(x_ref[...].astype(BF16), w_ref[...].astype(BF16))


def _matmul(x, w, tn=512):
    M, K = x.shape
    N = w.shape[1]
    return pl.pallas_call(
        _matmul_body,
        grid=(N // tn,),
        in_specs=[pl.BlockSpec((M, K), lambda j: (0, 0)), pl.BlockSpec((K, tn), lambda j: (0, j))],
        out_specs=pl.BlockSpec((M, tn), lambda j: (0, j)),
        out_shape=jax.ShapeDtypeStruct((M, N), F32),
        compiler_params=_params(("arbitrary",)),
        name="mem_kv_proj",
    )(x, w)


def _gelu_tanh(x):
    c = 0.7978845608028654
    return 0.5 * x * (1.0 + jnp.tanh(c * (x + 0.044715 * (x * x * x))))


def _compress(kv_refs, n_chunks, w1_ref, pe_ref, b1_ref, w2_ref, b2_ref):
    lo = _lane_iota(n_chunks) < DH
    quads = CMP_STRIDE // 4
    outs = []
    for c in range(2):
        acc_a = jnp.zeros((2 * n_chunks, CMP_HIDDEN), F32)
        acc_b = jnp.zeros((2 * n_chunks, CMP_HIDDEN), F32)
        for i in range(quads):
            x = [kv_refs[c][pl.ds(4 * i + m, n_chunks, stride=CMP_STRIDE), :] for m in range(4)]
            r = [pltpu.roll(v, DH, 1) for v in x]
            x_g0 = jnp.concatenate([jnp.where(lo, x[0], r[1]), jnp.where(lo, x[2], r[3])], axis=1)
            x_g1 = jnp.concatenate([jnp.where(lo, r[0], x[1]), jnp.where(lo, r[2], x[3])], axis=1)
            xq = jnp.concatenate([x_g0, x_g1], axis=0)
            acc_a = acc_a + _dot((xq + pe_ref[c, i:i + 1, :]).astype(BF16), w1_ref[c, i])
            acc_b = acc_b + _dot((xq + pe_ref[c, quads + i:quads + i + 1, :]).astype(BF16), w1_ref[c, quads + i])
        hid = acc_a + pltpu.roll(acc_b, 2 * n_chunks - 1, 0) + b1_ref[c]
        outs.append(_dot(_gelu_tanh(hid).astype(BF16), w2_ref[c]) + b2_ref[c])
    return outs


def _compress_prompt_body(kv_ref, w1_ref, pe_ref, b1_ref, w2_ref, b2_ref, o_ref, k_ref, v_ref, *, n_chunks):
    k_ref[...] = kv_ref[0, :, :LANES]
    v_ref[...] = kv_ref[0, :, LANES:]
    keys, values = _compress((k_ref, v_ref), n_chunks, w1_ref, pe_ref, b1_ref, w2_ref, b2_ref)
    o_ref[0, 0] = keys
    o_ref[0, 1] = values


def _compress_prompt(kvc, cw):
    B, S, _ = kvc.shape
    n_chunks = S // CMP_STRIDE
    full = lambda a: pl.BlockSpec(a.shape, lambda b: (0,) * a.ndim)
    return pl.pallas_call(
        functools.partial(_compress_prompt_body, n_chunks=n_chunks),
        grid=(B,),
        in_specs=[pl.BlockSpec((1, S, KV_WIDTH), lambda b: (b, 0, 0))] + [full(a) for a in cw],
        out_specs=pl.BlockSpec((1, 2, KVH * n_chunks, LANES), lambda b: (b, 0, 0, 0)),
        out_shape=jax.ShapeDtypeStruct((B, 2, KVH * n_chunks, LANES), F32),
        scratch_shapes=[pltpu.VMEM((S, LANES), F32)] * 2,
        compiler_params=_params(("arbitrary",)),
        name="compress_prompt",
    )(kvc, *cw)


def _compress_weights(cmp_pe, cmp_w1, cmp_b1, cmp_w2, cmp_b2):
    nq = CMP_LEN // 4
    w1 = cmp_w1.reshape(2, nq, 4 * DH, CMP_HIDDEN).astype(BF16)
    pe = cmp_pe.reshape(2, nq, 4 * DH)
    b1 = cmp_b1[:, None, :]
    w2 = jnp.stack([jnp.concatenate([cmp_w2[0], jnp.zeros_like(cmp_w2[0])], axis=1),
                    jnp.concatenate([cmp_w2[1], cmp_w2[1]], axis=1)]).astype(BF16)
    b2 = jnp.stack([jnp.concatenate([cmp_b2[0], jnp.zeros_like(cmp_b2[0])]),
                    jnp.concatenate([cmp_b2[1], cmp_b2[1]])])[:, None, :]
    return w1, pe, b1, w2, b2


def _softmax_rows(s):
    e = jnp.exp(s - jnp.max(s, axis=-1, keepdims=True))
    return e * (1.0 / jnp.sum(e, axis=-1, keepdims=True))


def _lane_iota(n):
    return lax.broadcasted_iota(I32, (n, LANES), 1)


def _key_alibi_cols(pos, lane):
    hi = jnp.left_shift(jnp.right_shift(pos, 6), 6).astype(F32)
    lo = jnp.bitwise_and(pos, SEL_LEN - 1).astype(F32)
    return jnp.where(lane == DH, hi,
                     jnp.where(lane == DH + 1, lo,
                               jnp.where((lane == DH + 2) | (lane == DH + 3), 1.0, 0.0)))


def _key_alibi_rows(pos, sub):
    hi = jnp.left_shift(jnp.right_shift(pos, 6), 6).astype(F32)
    lo = jnp.bitwise_and(pos, SEL_LEN - 1).astype(F32)
    return jnp.where(sub == 0, hi, jnp.where(sub == 1, lo, jnp.where((sub == 2) | (sub == 3), 1.0, 0.0)))


def _halves(x, zero_hi=False):
    lo = _lane_iota(x.shape[0]) < DH
    r = pltpu.roll(x, DH, 1)
    if zero_hi:
        return jnp.where(lo, x, 0.0), jnp.where(lo, r, 0.0)
    return jnp.where(lo, x, r), jnp.where(lo, r, x)


def _fill_queries(qa_ref, qs, q_pos, lq):
    lane = _lane_iota(lq)
    lo_half = lane < DH
    q_hi = jnp.left_shift(jnp.right_shift(q_pos, 7), 7).astype(F32)
    q_lo = jnp.bitwise_and(q_pos, LANES - 1).astype(F32)
    for h in range(NSA_HEADS):
        g, hl = divmod(h, HPG)
        slope = 2.0 ** (-(h + 1))
        slab = qs[:, (h // 2) * LANES:(h // 2 + 1) * LANES]
        if h % 2:
            slab = pltpu.roll(slab, DH, 1)
        ex = jnp.where((lane == DH) | (lane == DH + 1), slope,
                       jnp.where(lane == DH + 2, -slope * q_hi,
                                 jnp.where(lane == DH + 3, -slope * q_lo,
                                           jnp.where(lane == DH + 4, NEG, 0.0))))
        qa_ref[g, hl * lq:(hl + 1) * lq, 0:LANES] = jnp.where(lo_half, slab, ex).astype(BF16)


def _stack4(x):
    return jnp.concatenate([x] * HPG, axis=0)


def _cmp_branch(qa_ref, kc_k, kc_v, q_pos4, lq, n_cmp):
    c_end = lax.broadcasted_iota(I32, (1, n_cmp), 1) * CMP_STRIDE + (CMP_LEN - 1)
    m_c = c_end <= q_pos4
    any_c = (q_pos4 >= CMP_LEN - 1).astype(F32)
    outs, psums = [], []
    for g in range(KVH):
        s = jnp.where(m_c, _dot_nt(qa_ref[g, :, 0:LANES], kc_k[g][...]), NEG)
        p = _softmax_rows(s) * any_c
        outs.append(_dot(p.astype(BF16), kc_v[g][...]))
        psums.append(p[0:lq] + p[lq:2 * lq] + p[2 * lq:3 * lq] + p[3 * lq:4 * lq])
    return outs, psums


def _split3(x):
    hi = x.astype(BF16)
    r1 = x - hi.astype(F32)
    mid = r1.astype(BF16)
    lo = (r1 - mid.astype(F32)).astype(BF16)
    return hi, mid, lo


def _top_blocks_t(imp_ts, pos0, n_blk=SEL_PAD):
    blk = lax.broadcasted_iota(I32, (n_blk, LANES), 0)
    qp_t = pos0 + lax.broadcasted_iota(I32, (n_blk, LANES), 1)
    cur = jnp.right_shift(qp_t, 6)
    forced = (blk == 0) | (blk == cur) | (blk == cur - 1)
    valid = jnp.left_shift(blk, 6) <= qp_t
    v = jnp.concatenate([jnp.where(valid, jnp.where(forced, FORCED_SCORE, t), -1.0) for t in imp_ts], axis=1)
    blk_f = lax.broadcasted_iota(I32, (n_blk, KVH * LANES), 0).astype(F32)
    sel = jnp.zeros((n_blk, KVH * LANES), F32)
    for _ in range(SEL_TOP):
        m = jnp.max(v, axis=0, keepdims=True)
        idx = jnp.min(jnp.where(v == m, blk_f, float(n_blk)), axis=0, keepdims=True)
        hit = blk_f == idx
        sel = jnp.where(hit, 1.0, sel)
        v = jnp.where(hit, -jnp.inf, v)
    out = [jnp.where((sel[:, g * LANES:(g + 1) * LANES] > 0.5) & valid, 0.0, NEG) for g in range(KVH)]
    if n_blk < SEL_PAD:
        out = [jnp.concatenate([b, jnp.full((SEL_PAD - n_blk, LANES), NEG, F32)], axis=0) for b in out]
    return out


def _select_blocks(psums, ovt_ref, pos0, lq, n_blk):
    imp_ts = []
    for g in range(KVH):
        ps = psums[g]
        if lq < LANES:
            ps = jnp.concatenate([ps, jnp.zeros((LANES - lq, ps.shape[1]), F32)], axis=0)
        imp_ts.append(_dot_nt(ovt_ref[0:n_blk, :], ps, precision=HIGHEST))
    return [b.T[:lq].astype(BF16) for b in _top_blocks_t(imp_ts, pos0, n_blk)]


def _store_selbias(qa_ref, selbias, lq):
    for g in range(KVH):
        for hl in range(HPG):
            qa_ref[g, hl * lq:(hl + 1) * lq, LANES:2 * LANES] = selbias[g]


def _combine(gates, eg_ref, o_c, o_s, o_w, lq):
    lo_half = _lane_iota(lq) < DH

    def assemble(per_group):
        slabs = []
        for k in range(NSA_HEADS // 2):
            g, hl = divmod(2 * k, HPG)
            a = per_group[g][hl * lq:(hl + 1) * lq]
            b = per_group[g][(hl + 1) * lq:(hl + 2) * lq]
            slabs.append(jnp.where(lo_half, a, b))
        return jnp.concatenate(slabs, axis=1)

    return (_dot(gates, eg_ref[0]) * assemble(o_c)
            + _dot(gates, eg_ref[1]) * assemble(o_s)
            + _dot(gates, eg_ref[2]) * assemble(o_w))


def _gate_expand():
    r = jnp.arange(3 * NSA_HEADS)
    c = jnp.arange(NSA_WIDTH)
    return jnp.stack([(r[:, None] == 3 * (c[None, :] // DH) + br).astype(F32) for br in range(3)])


def _overlap_t(n_cmp):
    n = jnp.arange(n_cmp)
    s = jnp.arange(SEL_PAD)
    c_first = n * CMP_STRIDE
    c_end = c_first + CMP_LEN - 1
    b_first = s * SEL_LEN
    return ((c_first[None, :] < b_first[:, None] + SEL_LEN) & (c_end[None, :] >= b_first[:, None])).astype(F32)


def _key_rows(kv_f32, pos, invalid=None):
    n = kv_f32.shape[0]
    lane = _lane_iota(n)
    ex = _key_alibi_cols(pos, lane)
    if invalid is not None:
        ex = jnp.where((lane == DH + 4) & invalid, 1.0, ex)
    lo = lane < DH
    return jnp.where(lo, kv_f32, ex), jnp.where(lo, pltpu.roll(kv_f32, DH, 1), ex)


def _block_onehot(pos, n):
    return (lax.broadcasted_iota(I32, (n, SEL_PAD), 1) == jnp.right_shift(pos, 6)).astype(BF16)


def _tile4(x):
    return jnp.concatenate([x] * HPG, axis=1)


def _nsa_prompt_body(q_ref, gate_ref, kc_ref, kvs_ref, w0_ref, w1_ref, w2_ref, w3_ref, w4_ref,
                     ovt_ref, egt_ref, wband_ref, o_ref,
                     kck0, kck1, kcv0, kcv1, ka0, ka1, vt0, vt1, qa0, qa1,
                     sa_ref, sb_ref, m_ref, l_ref, acc_ref, oc_ref, *, seq_len):
    j = pl.program_id(1)
    kc_k, kc_vt, kaug, v_t, qa_t = (kck0, kck1), (kcv0, kcv1), (ka0, ka1), (vt0, vt1), (qa0, qa1)
    n_cmp = kc_ref.shape[2] // KVH
    lq = Q_BLOCK
    cols = HPG * lq

    @pl.when(j == 0)
    def _():
        for g in range(KVH):
            kc_k[g][...] = kc_ref[0, 0, g * n_cmp:(g + 1) * n_cmp, :].astype(BF16)
            kc_vt[g][...] = kc_ref[0, 1, g * n_cmp:(g + 1) * n_cmp, :].T[0:DH].astype(BF16)

        def build(i, _):
            r0 = pl.multiple_of(i * KEY_TILE, KEY_TILE)
            pos = r0 + lax.broadcasted_iota(I32, (KEY_TILE, 1), 0)
            k0, k1 = _key_rows(kvs_ref[0, pl.ds(r0, KEY_TILE), :LANES], pos)
            onehot = _block_onehot(pos, KEY_TILE)
            ka0[pl.ds(r0, KEY_TILE), :] = jnp.concatenate([k0.astype(BF16), onehot], axis=1)
            ka1[pl.ds(r0, KEY_TILE), :] = jnp.concatenate([k1.astype(BF16), onehot], axis=1)
            vt = kvs_ref[0, pl.ds(r0, KEY_TILE), LANES:].T.astype(BF16)
            vt0[:, pl.ds(r0, KEY_TILE)] = vt[0:DH]
            vt1[:, pl.ds(r0, KEY_TILE)] = vt[DH:]
            return 0

        lax.fori_loop(0, seq_len // KEY_TILE, build, 0)

    st = j * Q_BLOCK
    q_pos = st + lax.broadcasted_iota(I32, (1, lq), 1)
    q_pos4 = _tile4(q_pos)

    q_t = (q_ref[...] * ATTN_SCALE).T
    sub = lax.broadcasted_iota(I32, (DH, lq), 0)
    q_hi = jnp.left_shift(jnp.right_shift(q_pos, 7), 7).astype(F32)
    q_lo = jnp.bitwise_and(q_pos, LANES - 1).astype(F32)
    for h in range(NSA_HEADS):
        g, hl = divmod(h, HPG)
        slope = 2.0 ** (-(h + 1))
        ex = jnp.where(sub <= 1, slope,
                       jnp.where(sub == 2, -slope * q_hi,
                                 jnp.where(sub == 3, -slope * q_lo, jnp.where(sub == 4, NEG, 0.0))))
        qa_t[g][0:DH, hl * lq:(hl + 1) * lq] = q_t[h * DH:(h + 1) * DH].astype(BF16)
        qa_t[g][DH:2 * DH, hl * lq:(hl + 1) * lq] = ex.astype(BF16)

    any_c = (q_pos4 >= CMP_LEN - 1).astype(F32)
    blocks_per_sel = SEL_LEN // CMP_STRIDE
    n_variants = -(-n_cmp // LANES)
    last_visible = (st + Q_BLOCK - CMP_LEN) // CMP_STRIDE

    def cmp_and_select(n_c):
        n_b = n_c // blocks_per_sel
        c_end = lax.broadcasted_iota(I32, (n_c, 1), 0) * CMP_STRIDE + (CMP_LEN - 1)
        m_c = c_end <= q_pos4
        ovt_bf = ovt_ref[0:n_b, 0:n_c].astype(BF16)
        imp_ts = []
        for g in range(KVH):
            s = jnp.where(m_c, _dot(kc_k[g][0:n_c, :], qa_t[g][0:LANES, :]), NEG)
            e = jnp.exp(s - jnp.max(s, axis=0, keepdims=True))
            p = e * (any_c / jnp.sum(e, axis=0, keepdims=True))
            oc_ref[g] = _dot(kc_vt[g][:, 0:n_c], p.astype(BF16))
            psum = p[:, 0:lq] + p[:, lq:2 * lq] + p[:, 2 * lq:3 * lq] + p[:, 3 * lq:4 * lq]
            imp_ts.append(sum(_dot(ovt_bf, t) for t in _split3(psum)))
        bias = _top_blocks_t(imp_ts, st, n_b)
        for g in range(KVH):
            for hl in range(HPG):
                qa_t[g][2 * DH:, hl * lq:(hl + 1) * lq] = bias[g].astype(BF16)

    variant = jnp.minimum(jnp.maximum(last_visible, 0) // LANES, n_variants - 1)
    for k in range(n_variants):
        pl.when(variant == k)(functools.partial(cmp_and_select, min(n_cmp, (k + 1) * LANES)))
    o_c = [oc_ref[g] for g in range(KVH)]

    n_tiles = (st + Q_BLOCK + KEY_TILE - 1) // KEY_TILE
    for g in range(KVH):
        m_ref[g] = jnp.full((1, cols), NEG, F32)
        l_ref[g] = jnp.zeros((1, cols), F32)
        acc_ref[g] = jnp.zeros((DH, cols), F32)

    def scores(t, s_ref):
        r0 = pl.multiple_of(t * KEY_TILE, KEY_TILE)
        for g in range(KVH):
            s_ref[g] = _dot(kaug[g][pl.ds(r0, KEY_TILE), :], qa_t[g][...])

    def consume(t, s_ref, masked):
        r0 = pl.multiple_of(t * KEY_TILE, KEY_TILE)
        for g in range(KVH):
            s = s_ref[g]
            if masked:
                k_pos = r0 + lax.broadcasted_iota(I32, (KEY_TILE, 1), 0)
                s = jnp.where(k_pos <= q_pos4, s, NEG)
            m = m_ref[g]
            m_new = jnp.maximum(m, jnp.max(s, axis=0, keepdims=True))
            a = jnp.exp(m - m_new)
            e = jnp.exp(s - m_new)
            m_ref[g] = m_new
            l_ref[g] = a * l_ref[g] + jnp.sum(e, axis=0, keepdims=True)
            acc_ref[g] = a * acc_ref[g] + _dot(v_t[g][:, pl.ds(r0, KEY_TILE)], e.astype(BF16))

    scores(0, sa_ref)
    n_pairs = (n_tiles - 1) // 2

    def pair(u, _):
        scores(2 * u + 1, sb_ref)
        consume(2 * u, sa_ref, False)
        scores(2 * u + 2, sa_ref)
        consume(2 * u + 1, sb_ref, False)
        return 0

    lax.fori_loop(0, n_pairs, pair, 0)
    odd_tail = (n_tiles - 1) - 2 * n_pairs == 1

    @pl.when(odd_tail)
    def _():
        scores(n_tiles - 1, sb_ref)
        consume(n_tiles - 2, sa_ref, False)
        consume(n_tiles - 1, sb_ref, True)

    @pl.when(jnp.logical_not(odd_tail))
    def _():
        consume(n_tiles - 1, sa_ref, True)

    o_s = [acc_ref[g] * (1.0 / l_ref[g]) for g in range(KVH)]

    band = jnp.concatenate([w0_ref[0], w1_ref[0], w2_ref[0], w3_ref[0], w4_ref[0]], axis=0)
    n_win = band.shape[0]
    w_pos_col = st - WINDOW + lax.broadcasted_iota(I32, (n_win, 1), 0)
    kw_k = _key_rows(band[:, :LANES], jnp.maximum(w_pos_col, 0), invalid=w_pos_col < 0)
    vw_t = band[:, LANES:].T.astype(BF16)
    band_bias = _tile4(wband_ref[...])
    o_w = []
    for g in range(KVH):
        s = _dot(kw_k[g].astype(BF16), qa_t[g][0:LANES, :]) + band_bias
        e = jnp.exp(s - jnp.max(s, axis=0, keepdims=True))
        o_w.append(_dot(vw_t[g * DH:(g + 1) * DH], e.astype(BF16)) * (1.0 / jnp.sum(e, axis=0, keepdims=True)))

    def heads(per_group):
        return jnp.concatenate([per_group[h // HPG][:, (h % HPG) * lq:(h % HPG + 1) * lq]
                                for h in range(NSA_HEADS)], axis=0)

    gates = gate_ref[...]
    out_t = (_dot_nt(egt_ref[0], gates) * heads(o_c)
             + _dot_nt(egt_ref[1], gates) * heads(o_s)
             + _dot_nt(egt_ref[2], gates) * heads(o_w))
    o_ref[...] = out_t.T.astype(o_ref.dtype)


def _gate_expand_t():
    r = jnp.arange(3 * NSA_HEADS)
    c = jnp.arange(NSA_WIDTH)
    return jnp.stack([(3 * (c[:, None] // DH) + br == r[None, :]).astype(F32) for br in range(3)])


def _nsa_prompt(q, gates, kc, kvs, kvw):
    B, S, _ = kvs.shape
    nqb = S // Q_BLOCK
    n_cmp = kc.shape[2] // KVH
    ovt = _overlap_t(n_cmp)
    egt = _gate_expand_t()
    n_band = WINDOW // Q_BLOCK + 1
    d_band = jnp.arange(Q_BLOCK)[None, :] + WINDOW - jnp.arange(n_band * Q_BLOCK)[:, None]
    wband = jnp.where((d_band >= 0) & (d_band < WINDOW), 0.0, NEG).astype(F32)

    def band_spec(i):
        return pl.BlockSpec((1, Q_BLOCK, KV_WIDTH),
                            lambda b, j, i=i: (b, jnp.maximum(j - (n_band - 1) + i, 0), 0))

    full = lambda a: pl.BlockSpec(a.shape, lambda b, j: (0,) * a.ndim)
    return pl.pallas_call(
        functools.partial(_nsa_prompt_body, seq_len=S),
        grid=(B, nqb),
        in_specs=[pl.BlockSpec((Q_BLOCK, NSA_WIDTH), lambda b, j: (b * nqb + j, 0)),
                  pl.BlockSpec((Q_BLOCK, 3 * NSA_HEADS), lambda b, j: (b * nqb + j, 0)),
                  pl.BlockSpec((1,) + kc.shape[1:], lambda b, j: (b, 0, 0, 0)),
                  pl.BlockSpec((1, S, KV_WIDTH), lambda b, j: (b, 0, 0))]
                 + [band_spec(i) for i in range(n_band)] + [full(ovt), full(egt), full(wband)],
        out_specs=pl.BlockSpec((Q_BLOCK, NSA_WIDTH), lambda b, j: (b * nqb + j, 0)),
        out_shape=jax.ShapeDtypeStruct((B * S, NSA_WIDTH), BF16),
        scratch_shapes=[pltpu.VMEM((n_cmp, LANES), BF16)] * 2
                       + [pltpu.VMEM((DH, n_cmp), BF16)] * 2
                       + [pltpu.VMEM((S, 2 * LANES), BF16)] * 2
                       + [pltpu.VMEM((DH, S), BF16)] * 2
                       + [pltpu.VMEM((2 * LANES, HPG * Q_BLOCK), BF16)] * 2
                       + [pltpu.VMEM((KVH, KEY_TILE, HPG * Q_BLOCK), F32)] * 2
                       + [pltpu.VMEM((KVH, 1, HPG * Q_BLOCK), F32)] * 2
                       + [pltpu.VMEM((KVH, DH, HPG * Q_BLOCK), F32)] * 2,
        compiler_params=_params(("arbitrary", "arbitrary")),
        name="nsa_prompt",
    )(q, gates, kc, kvs, *([kvw] * n_band), ovt, egt, wband)


def _nsa_sample_body(pt_ref, q_ref, gate_ref, kvs_new_ref, win_ref, kvw_new_ref, *rest,
                     n_pages, ls, past_len):
    cmp_pages = rest[:n_pages]
    slc_pages = rest[n_pages:2 * n_pages]
    (w1_ref, pe_ref, b1_ref, w2_ref, b2_ref, ovt_ref, eg_ref, o_ref, wout_ref,
     full_k, full_v, kck0, kck1, kcv0, kcv1, kt0, kt1, vt0, vt1, wkt0, wkt1, qa_ref) = rest[2 * n_pages:]
    del pt_ref
    kc_k, kc_v, kaug_t, v_t, wk_t = (kck0, kck1), (kcv0, kcv1), (kt0, kt1), (vt0, vt1), (wkt0, wkt1)
    n_cmp = past_len // CMP_STRIDE
    w_rows = win_ref.shape[4]
    lq = BF16_ROWS
    rows = HPG * lq
    w_start = past_len - w_rows

    @pl.when(pl.program_id(0) == 0)
    def _():
        sub = lax.broadcasted_iota(I32, (DH, past_len), 0)
        pos = lax.broadcasted_iota(I32, (1, past_len), 1)
        ex = _key_alibi_rows(pos, sub).astype(BF16)
        onehot = (lax.broadcasted_iota(I32, (SEL_PAD, past_len), 0) == jnp.right_shift(pos, 6)).astype(BF16)
        subw = lax.broadcasted_iota(I32, (DH, w_rows), 0)
        exw = _key_alibi_rows(w_start + lax.broadcasted_iota(I32, (1, w_rows), 1), subw).astype(BF16)
        for g in range(KVH):
            kaug_t[g][DH:2 * DH, :] = ex
            kaug_t[g][2 * DH:, :] = onehot
            wk_t[g][DH:, :] = exw

    for p in range(n_pages):
        cols = slice(p * PAGE_SIZE, (p + 1) * PAGE_SIZE)
        full_k[cols, :] = cmp_pages[p][0, 0].reshape(2 * DH, PAGE_SIZE).T
        full_v[cols, :] = cmp_pages[p][0, 1].reshape(2 * DH, PAGE_SIZE).T
        for g in range(KVH):
            kaug_t[g][0:DH, cols] = slc_pages[p][0, 0, g].astype(BF16)
            vt = slc_pages[p][0, 1, g].astype(BF16)
            v_t[g][0:DH, cols] = vt
            v_t[g][DH:, cols] = vt
    for g in range(KVH):
        wk_t[g][0:DH, :] = win_ref[0, 0, g].astype(BF16)

    keys_c, values_c = _compress((full_k, full_v), n_cmp, w1_ref, pe_ref, b1_ref, w2_ref, b2_ref)
    for g in range(KVH):
        kc_k[g][...] = keys_c[g * n_cmp:(g + 1) * n_cmp].astype(BF16)
        kc_v[g][...] = values_c[g * n_cmp:(g + 1) * n_cmp].astype(BF16)

    pad_q = jnp.zeros((lq - ls, NSA_WIDTH), F32)
    q_pos = past_len + lax.broadcasted_iota(I32, (lq, 1), 0)
    q_pos4 = _stack4(q_pos)
    _fill_queries(qa_ref, jnp.concatenate([q_ref[0] * ATTN_SCALE, pad_q], axis=0), q_pos, lq)
    gates = jnp.concatenate([gate_ref[0], jnp.zeros((lq - ls, 3 * NSA_HEADS), F32)], axis=0)

    o_c, psums = _cmp_branch(qa_ref, kc_k, kc_v, q_pos4, lq, n_cmp)
    n_sel = -(-(past_len + lq) // SEL_LEN)
    n_blk = -(-n_sel // 8) * 8
    _store_selbias(qa_ref, _select_blocks(psums, ovt_ref, past_len, lq, n_blk), lq)

    pad_k = jnp.zeros((LANES - ls, KV_WIDTH), F32)
    new_pos_col = past_len + lax.broadcasted_iota(I32, (LANES, 1), 0)
    new_pos = past_len + lax.broadcasted_iota(I32, (1, LANES), 1)
    new_s = jnp.concatenate([kvs_new_ref[0], pad_k], axis=0)
    new_w = jnp.concatenate([kvw_new_ref[0], pad_k], axis=0)
    ks_new = _key_rows(new_s[:, :LANES], new_pos_col)
    vs_new = _halves(new_s[:, LANES:])
    kw_new = _key_rows(new_w[:, :LANES], new_pos_col)
    vw_new = _halves(new_w[:, LANES:])
    onehot_new = _block_onehot(new_pos_col, LANES)
    causal_new = q_pos4 >= new_pos

    o_s, o_w = [], []
    d_past = q_pos4 - (w_start + lax.broadcasted_iota(I32, (1, w_rows), 1))
    m_past = (d_past >= 0) & (d_past < WINDOW)
    d_new = q_pos4 - new_pos
    m_new = (d_new >= 0) & (d_new < WINDOW)
    for g in range(KVH):
        s_past = _dot(qa_ref[g], kaug_t[g][...])
        k_new = jnp.concatenate([ks_new[g].astype(BF16), onehot_new], axis=1)
        s_new = jnp.where(causal_new, _dot_nt(qa_ref[g], k_new), NEG)
        m = jnp.maximum(jnp.max(s_past, axis=-1, keepdims=True), jnp.max(s_new, axis=-1, keepdims=True))
        e_past = jnp.exp(s_past - m)
        e_new = jnp.exp(s_new - m)
        den = jnp.sum(e_past, axis=-1, keepdims=True) + jnp.sum(e_new, axis=-1, keepdims=True)
        acc = _dot_nt(e_past.astype(BF16), v_t[g][...]) + _dot(e_new.astype(BF16), vs_new[g].astype(BF16))
        o_s.append(acc / den)

        sw_past = jnp.where(m_past, _dot(qa_ref[g, :, 0:LANES], wk_t[g][...]), NEG)
        sw_new = jnp.where(m_new, _dot_nt(qa_ref[g, :, 0:LANES], kw_new[g].astype(BF16)), NEG)
        m = jnp.maximum(jnp.max(sw_past, axis=-1, keepdims=True), jnp.max(sw_new, axis=-1, keepdims=True))
        e_past = jnp.exp(sw_past - m)
        e_new = jnp.exp(sw_new - m)
        den = jnp.sum(e_past, axis=-1, keepdims=True) + jnp.sum(e_new, axis=-1, keepdims=True)
        vw = win_ref[0, 1, g].astype(BF16)
        vw2 = jnp.concatenate([vw, vw], axis=0)
        acc = _dot_nt(e_past.astype(BF16), vw2) + _dot(e_new.astype(BF16), vw_new[g].astype(BF16))
        o_w.append(acc / den)

    o_ref[0] = _combine(gates, eg_ref, o_c, o_s, o_w, lq)[:ls]

    keep = w_rows - ls
    tail_lane = lax.broadcasted_iota(I32, (DH, LANES), 1) >= keep % LANES
    for c in range(2):
        new_t = pltpu.roll(new_w[:, c * LANES:(c + 1) * LANES].T, keep % LANES, 1)
        for g in range(KVH):
            shifted = pltpu.roll(win_ref[0, c, g], keep, 1)
            last = jnp.where(tail_lane, new_t[g * DH:(g + 1) * DH], shifted[:, w_rows - LANES:])
            wout_ref[0, c, g] = jnp.concatenate([shifted[:, :w_rows - LANES], last], axis=1)


def _nsa_sample(q, gates, kvs_new, kvw_new, win_t, cmp_t, slc_t, page_table, cw, past_len):
    Bd, ls, _ = q.shape
    n_pages = page_table.shape[1]
    n_cmp = past_len // CMP_STRIDE
    ovt = _overlap_t(n_cmp)
    eg = _gate_expand()
    w_rows = win_t.shape[4]

    per_b = lambda a: pl.BlockSpec((1,) + a.shape[1:], lambda b, pt: (b,) + (0,) * (a.ndim - 1))
    full = lambda a: pl.BlockSpec(a.shape, lambda b, pt: (0,) * a.ndim)
    page = lambda p: pl.BlockSpec((1, 2, KVH, DH, PAGE_SIZE), lambda b, pt, p=p: (pt[b, p], 0, 0, 0, 0))
    grid_spec = pltpu.PrefetchScalarGridSpec(
        num_scalar_prefetch=1,
        grid=(Bd,),
        in_specs=[per_b(q), per_b(gates), per_b(kvs_new), per_b(win_t), per_b(kvw_new)]
                 + [page(p) for p in range(n_pages)] * 2
                 + [full(a) for a in cw] + [full(ovt), full(eg)],
        out_specs=[pl.BlockSpec((1, ls, NSA_WIDTH), lambda b, pt: (b, 0, 0)), per_b(win_t)],
        scratch_shapes=[pltpu.VMEM((past_len, LANES), F32)] * 2
                       + [pltpu.VMEM((n_cmp, LANES), BF16)] * 4
                       + [pltpu.VMEM((2 * LANES, past_len), BF16)] * 2
                       + [pltpu.VMEM((LANES, past_len), BF16)] * 2
                       + [pltpu.VMEM((LANES, w_rows), BF16)] * 2
                       + [pltpu.VMEM((KVH, HPG * BF16_ROWS, 2 * LANES), BF16)],
    )
    return pl.pallas_call(
        functools.partial(_nsa_sample_body, n_pages=n_pages, ls=ls, past_len=past_len),
        grid_spec=grid_spec,
        out_shape=[jax.ShapeDtypeStruct((Bd, ls, NSA_WIDTH), F32), jax.ShapeDtypeStruct(win_t.shape, F32)],
        compiler_params=_params(("arbitrary",)),
        name="nsa_sample",
    )(page_table, q, gates, kvs_new, win_t, kvw_new,
      *([cmp_t] * n_pages), *([slc_t] * n_pages), *cw, ovt, eg)


def _fin1_body(h_ref, pool_ref, nsa_ref, wo_ref, g_ref, b_ref, wq_ref, h1_ref, qm_ref):
    mix = (_dot(pool_ref[...].astype(BF16), wo_ref[0:POOL_WIDTH, :])
           + _dot(nsa_ref[...].astype(BF16), wo_ref[POOL_WIDTH:, :]))
    h1 = _layer_norm(DN_ALPHA * h_ref[...] + mix, g_ref[...], b_ref[...])
    h1_ref[...] = h1
    qm_ref[...] = (_dot(h1.astype(BF16), wq_ref[...]) * (MEM_HEAD_DIM ** -0.5)).astype(qm_ref.dtype)


def _fin1(h, pool_o, nsa_o, w_out_bf, g, b, wq_bf, q_dtype):
    T = h.shape[0]
    tm = ROW_TILE
    row = lambda n: pl.BlockSpec((tm, n), lambda i: (i, 0))
    full = lambda a: pl.BlockSpec(a.shape, lambda i: (0,) * a.ndim)
    return pl.pallas_call(
        _fin1_body,
        grid=(T // tm,),
        in_specs=[row(D_MODEL), row(POOL_WIDTH), row(NSA_WIDTH), full(w_out_bf), full(g), full(b), full(wq_bf)],
        out_specs=[row(D_MODEL), row(D_MODEL)],
        out_shape=[jax.ShapeDtypeStruct((T, D_MODEL), F32), jax.ShapeDtypeStruct((T, D_MODEL), q_dtype)],
        compiler_params=_params(("arbitrary",)),
        name="out_proj_ln1",
    )(h, pool_o, nsa_o, w_out_bf, g, b, wq_bf)


def _memattn_body(q_ref, kv_ref, o_ref):
    width = MEM_HEADS * MEM_HEAD_DIM
    for h in range(MEM_HEADS):
        cols = slice(h * MEM_HEAD_DIM, (h + 1) * MEM_HEAD_DIM)
        qh = q_ref[0, :, cols].astype(BF16)
        kh = kv_ref[0, :, cols].astype(BF16)
        vh = kv_ref[0, :, width + h * MEM_HEAD_DIM:width + (h + 1) * MEM_HEAD_DIM].astype(BF16)
        s = _dot_nt(qh, kh)
        e = jnp.exp(s - jnp.max(s, axis=-1, keepdims=True))
        o = _dot(e.astype(BF16), vh) * (1.0 / jnp.sum(e, axis=-1, keepdims=True))
        o_ref[0, :, cols] = o.astype(o_ref.dtype)


def _memattn(qm, mem_kv, tq):
    nb, L, W = qm.shape
    return pl.pallas_call(
        _memattn_body,
        grid=(nb, L // tq),
        in_specs=[pl.BlockSpec((1, tq, W), lambda b, t: (b, t, 0)),
                  pl.BlockSpec((1, MEM_LEN, 2 * W), lambda b, t: (b, 0, 0))],
        out_specs=pl.BlockSpec((1, tq, W), lambda b, t: (b, t, 0)),
        out_shape=jax.ShapeDtypeStruct((nb, L, W), BF16),
        compiler_params=_params(("arbitrary", "arbitrary")),
        name="mem_attn",
    )(qm, mem_kv)


def _memattn_few_body(q_ref, kv_ref, o_ref):
    lq = q_ref.shape[1]
    n_keys = MEM_LEN * MEM_HEADS
    q = q_ref[0]
    qs = jnp.concatenate([q[:, h * MEM_HEAD_DIM:(h + 1) * MEM_HEAD_DIM] for h in range(MEM_HEADS)], axis=0)
    k = kv_ref[0, :, 0, :, :].reshape(n_keys, MEM_HEAD_DIM).astype(BF16)
    v = kv_ref[0, :, 1, :, :].reshape(n_keys, MEM_HEAD_DIM).astype(BF16)
    s = _dot_nt(qs.astype(BF16), k)
    assert lq & (lq - 1) == 0 and MEM_HEADS & (MEM_HEADS - 1) == 0
    col_h = jnp.bitwise_and(lax.broadcasted_iota(I32, s.shape, 1), MEM_HEADS - 1)
    row_h = jnp.right_shift(lax.broadcasted_iota(I32, s.shape, 0), lq.bit_length() - 1)
    s = jnp.where(col_h == row_h, s, NEG)
    e = jnp.exp(s - jnp.max(s, axis=-1, keepdims=True))
    o = _dot(e.astype(BF16), v) * (1.0 / jnp.sum(e, axis=-1, keepdims=True))
    for h in range(MEM_HEADS):
        o_ref[0, :, h * MEM_HEAD_DIM:(h + 1) * MEM_HEAD_DIM] = o[h * lq:(h + 1) * lq]


def _memattn_few(qm, mem_kv):
    nb, lq, W = qm.shape
    return pl.pallas_call(
        _memattn_few_body,
        grid=(nb,),
        in_specs=[pl.BlockSpec((1, lq, W), lambda b: (b, 0, 0)),
                  pl.BlockSpec((1, MEM_LEN, 2, MEM_HEADS, MEM_HEAD_DIM), lambda b: (b, 0, 0, 0, 0))],
        out_specs=pl.BlockSpec((1, lq, W), lambda b: (b, 0, 0)),
        out_shape=jax.ShapeDtypeStruct((nb, lq, W), F32),
        compiler_params=_params(("arbitrary",)),
        name="mem_attn_few",
    )(qm, mem_kv)


def _fin2_body(cnt0_ref, h1_ref, o_ref, wo_ref, g_ref, b_ref, rw_ref, rb_ref, *rest):
    h2_ref, te_ref, tg_ref, cnt_ref, run_ref = rest[-5:]
    tm = h1_ref.shape[0]

    @pl.when(pl.program_id(0) == 0)
    def _():
        run_ref[...] = cnt0_ref[...]

    a = _dot(o_ref[...].astype(BF16), wo_ref[...])
    h2 = _layer_norm(DN_ALPHA * h1_ref[...] + a, g_ref[...], b_ref[...])
    h2_ref[...] = h2
    a_hi, a_lo, _ = _split3(h2)
    w_hi, w_lo, _ = _split3(rw_ref[...])
    logits = _dot(a_hi, w_hi) + (_dot(a_hi, w_lo) + _dot(a_lo, w_hi)) + rb_ref[...]
    e_iota = lax.broadcasted_iota(I32, (tm, N_EXPERTS), 1).astype(F32)
    lane = lax.broadcasted_iota(I32, (tm, LANES), 1)
    te = jnp.zeros((tm, LANES), F32)
    tv = jnp.full((tm, LANES), NEG, F32)
    work = logits
    chosen = []
    for k in range(TOP_K):
        m = jnp.max(work, axis=-1, keepdims=True)
        idx = jnp.min(jnp.where(work == m, e_iota, float(N_EXPERTS)), axis=-1, keepdims=True)
        hit = e_iota == idx
        chosen.append(hit)
        te = jnp.where(lane == k, idx, te)
        tv = jnp.where(lane == k, m, tv)
        work = jnp.where(hit, -jnp.inf, work)
    member = sum(c.astype(F32) for c in chosen)
    earlier = (lax.broadcasted_iota(I32, (tm, tm), 0) > lax.broadcasted_iota(I32, (tm, tm), 1)).astype(BF16)
    before = _dot(earlier, member.astype(BF16)) + run_ref[...]
    for k in range(TOP_K):
        rank = jnp.sum(jnp.where(chosen[k], before, 0.0), axis=-1, keepdims=True)
        te = jnp.where(lane == TOP_K + k, rank, te)
    run_ref[...] = run_ref[...] + jnp.sum(member, axis=0, keepdims=True)
    cnt_ref[...] = run_ref[...]
    ex = jnp.exp(tv - jnp.max(tv, axis=-1, keepdims=True))
    te_ref[...] = te.astype(I32)
    tg_ref[...] = ex / jnp.sum(ex, axis=-1, keepdims=True)


def _fin2(cnt0, h1, o, wo_bf, g, b, rw, rb, total_rows, row_offset=0, into=None):
    T = h1.shape[0]
    tm = ROW_TILE
    blk0 = row_offset // tm
    row = lambda n: pl.BlockSpec((tm, n), lambda i: (i, 0))
    out_row = lambda n: pl.BlockSpec((tm, n), lambda i: (i + blk0, 0))
    full = lambda a: pl.BlockSpec(a.shape, lambda i: (0,) * a.ndim)
    ins = [cnt0, h1, o, wo_bf, g, b, rw, rb]
    in_specs = [full(cnt0), row(D_MODEL), row(D_MODEL), full(wo_bf), full(g), full(b), full(rw), full(rb)]
    aliases = {}
    if into is not None:
        aliases = {len(ins) + k: k for k in range(len(into))}
        in_specs = in_specs + [pl.BlockSpec(memory_space=pl.ANY)] * len(into)
        ins = ins + list(into)
    return pl.pallas_call(
        _fin2_body,
        grid=(T // tm,),
        in_specs=in_specs,
        out_specs=[out_row(D_MODEL), out_row(LANES), out_row(LANES), full(cnt0)],
        out_shape=[jax.ShapeDtypeStruct((total_rows, D_MODEL), F32), jax.ShapeDtypeStruct((total_rows, LANES), I32),
                   jax.ShapeDtypeStruct((total_rows, LANES), F32), jax.ShapeDtypeStruct(cnt0.shape, F32)],
        scratch_shapes=[pltpu.VMEM(cnt0.shape, F32)],
        input_output_aliases=aliases,
        compiler_params=_params(("arbitrary",)),
        name="mem_out_ln2_router",
    )(*ins)


def _moe_body(ut_ref, ue_ref, nu_ref, rs_ref, re_ref, x_ref, rw_ref, wgu_ref, bgu_ref, wdn_ref, bdn_ref,
              y_ref, wgu_bf, wdn_bf):
    u = pl.program_id(0)
    bk = x_ref.shape[0]
    e = ue_ref[u]
    tile = ut_ref[u]
    prev = jnp.maximum(u - 1, 0)

    @pl.when((u == 0) | (e != ue_ref[prev]))
    def _():
        wgu_bf[...] = wgu_ref[0].astype(BF16)
        wdn_bf[...] = wdn_ref[0].astype(BF16)

    @pl.when(u < nu_ref[0])
    def _():
        x = x_ref[...].astype(BF16)
        g = _dot(x, wgu_bf[:, :D_FF]) + bgu_ref[0, :, :D_FF]
        v = _dot(x, wgu_bf[:, D_FF:]) + bgu_ref[0, :, D_FF:]
        g = jnp.minimum(g, SWIGLU_LIMIT)
        v = jnp.clip(v, -SWIGLU_LIMIT, SWIGLU_LIMIT)
        a = g * (1.0 / (1.0 + jnp.exp(-SWIGLU_ALPHA * g))) * (v + 1.0)
        y = _dot(a.astype(BF16), wdn_bf[...]) + bdn_ref[0]
        row = tile * bk + lax.broadcasted_iota(I32, (bk, 1), 0)
        mine = (row >= rs_ref[e]) & (row < re_ref[e])
        y = jnp.where(mine, y * rw_ref[...], 0.0)

        @pl.when((u == 0) | (tile != ut_ref[prev]))
        def _():
            y_ref[...] = y

        @pl.when((u > 0) & (tile == ut_ref[prev]))
        def _():
            y_ref[...] = y_ref[...] + y


def _moe_gmm(x_rows, row_w, units, w_gu, b_gu, w_dn, b_dn):
    N = x_rows.shape[0]
    bk = MOE_ROWS
    unit_tile, unit_e, n_units, r_start, r_end = units
    grid_spec = pltpu.PrefetchScalarGridSpec(
        num_scalar_prefetch=5,
        grid=(unit_tile.shape[0],),
        in_specs=[pl.BlockSpec((bk, D_MODEL), lambda u, ut, ue, *_: (ut[u], 0)),
                  pl.BlockSpec((bk, 1), lambda u, ut, ue, *_: (ut[u], 0)),
                  pl.BlockSpec((1, D_MODEL, 2 * D_FF), lambda u, ut, ue, *_: (ue[u], 0, 0)),
                  pl.BlockSpec((1, 1, 2 * D_FF), lambda u, ut, ue, *_: (ue[u], 0, 0)),
                  pl.BlockSpec((1, D_FF, D_MODEL), lambda u, ut, ue, *_: (ue[u], 0, 0)),
                  pl.BlockSpec((1, 1, D_MODEL), lambda u, ut, ue, *_: (ue[u], 0, 0))],
        out_specs=pl.BlockSpec((bk, D_MODEL), lambda u, ut, ue, *_: (ut[u], 0)),
        scratch_shapes=[pltpu.VMEM((D_MODEL, 2 * D_FF), BF16), pltpu.VMEM((D_FF, D_MODEL), BF16)],
    )
    return pl.pallas_call(
        _moe_body,
        grid_spec=grid_spec,
        out_shape=jax.ShapeDtypeStruct((N, D_MODEL), F32),
        compiler_params=_params(("arbitrary",)),
        name="moe_experts",
    )(unit_tile, unit_e, n_units, r_start, r_end, x_rows, row_w, w_gu, b_gu, w_dn, b_dn)


FLAT_BITS = 17


def _moe_routing(te, tg, counts):
    bk = MOE_ROWS
    T = te.shape[0]
    N = T * TOP_K
    assert N % bk == 0 and N <= (1 << FLAT_BITS)
    experts = jnp.arange(N_EXPERTS, dtype=I32)
    top_e = te[:, :TOP_K]
    r_end = jnp.cumsum(counts).astype(I32)
    r_start = r_end - counts
    onehot = top_e[:, :, None] == experts[None, None, :]
    pos = jnp.sum(jnp.where(onehot, r_start[None, None, :], 0), axis=-1) + te[:, TOP_K:2 * TOP_K]
    key = jnp.left_shift(top_e.reshape(-1), FLAT_BITS) + jnp.arange(N, dtype=I32)
    key_s, gate_s = lax.sort((key, tg[:, :TOP_K].reshape(-1)), num_keys=1)
    tok_s = jnp.right_shift(jnp.bitwise_and(key_s, (1 << FLAT_BITS) - 1), 2)
    first = r_start // bk
    last = (r_end - 1) // bk
    n_e = jnp.where(counts > 0, last - first + 1, 0)
    u_end = jnp.cumsum(n_e).astype(I32)
    u_start = u_end - n_e
    n_units = u_end[-1]
    u = jnp.minimum(jnp.arange(N // bk + N_EXPERTS - 1, dtype=I32), n_units - 1)
    unit_e = jnp.sum((u[:, None] >= u_end[None, :]).astype(I32), axis=1)
    unit_tile = u + jnp.sum(jnp.where(unit_e[:, None] == experts[None, :], (first - u_start)[None, :], 0), axis=1)
    return pos, tok_s, gate_s, (unit_tile, unit_e, n_units.reshape(1), r_start, r_end)


def _fin3_body(h2_ref, y0_ref, y1_ref, y2_ref, y3_ref, g_ref, b_ref, o_ref):
    y = (y0_ref[...] + y1_ref[...]) + (y2_ref[...] + y3_ref[...])
    o_ref[...] = _layer_norm(DN_ALPHA * h2_ref[...] + y, g_ref[...], b_ref[...])


def _fin3(h2, ys, g, b, row_offset, T):
    tm = ROW_TILE
    blk0 = row_offset // tm
    row = lambda n: pl.BlockSpec((tm, n), lambda i: (i, 0))
    full = lambda a: pl.BlockSpec(a.shape, lambda i: (0,) * a.ndim)
    return pl.pallas_call(
        _fin3_body,
        grid=(T // tm,),
        in_specs=[pl.BlockSpec((tm, D_MODEL), lambda i: (i + blk0, 0))] * (1 + TOP_K) + [full(g), full(b)],
        out_specs=row(D_MODEL),
        out_shape=jax.ShapeDtypeStruct((T, D_MODEL), F32),
        compiler_params=_params(("arbitrary",)),
        name="combine_ln3",
    )(h2, *ys, g, b)


def kernel(x_prompt, x_sample, cache_cmp_kv, cache_slc_kv, state_win_kv, state_pool, cache_mem_kv, page_table,
           mem_prompt, w_in, pool_w, pool_scale, cmp_pe, cmp_w1, cmp_b1, cmp_w2, cmp_b2, w_out, ln1_g, ln1_b,
           mem_wq, mem_wkv, mem_wo, ln2_g, ln2_b, router_w, router_b, exp_w_gu, exp_b_gu, exp_w_dn, exp_b_dn,
           ln3_g, ln3_b):
    Bp, S, D = x_prompt.shape
    Bd, Ls, _ = x_sample.shape
    Tp, Ts = Bp * S, Bd * Ls
    l = 0
    w_in_bf = w_in[l].astype(BF16)
    pool_w_bf = pool_w[l].astype(BF16)
    ps = pool_scale[l][None, :]
    cw = _compress_weights(cmp_pe[l], cmp_w1[l], cmp_b1[l], cmp_w2[l], cmp_b2[l])
    w_out_bf = w_out[l].astype(BF16)
    wq_bf = mem_wq[l].astype(BF16)
    wo_bf = mem_wo[l].astype(BF16)
    vec = lambda a: a[l][None, :]

    up, qp, kvc_p, kvs_p, kvw_p, gp, pool_p, kvc_t, kvs_t = _inproj_prompt(
        x_prompt.reshape(Tp, D), w_in_bf, pool_w_bf, ps, S)
    kc_p = _compress_prompt(kvc_p.reshape(Bp, S, KV_WIDTH), cw)
    nsa_p = _nsa_prompt(qp, gp, kc_p, kvs_p.reshape(Bp, S, KV_WIDTH), kvw_p.reshape(Bp, S, KV_WIDTH))
    mem_kv_p = _matmul(mem_prompt.reshape(Bp * MEM_LEN, D), mem_wkv[l]).reshape(Bp, MEM_LEN, 2 * D)
    h1_p, qm_p = _fin1(x_prompt.reshape(Tp, D), pool_p, nsa_p, w_out_bf, vec(ln1_g), vec(ln1_b), wq_bf, BF16)
    om_p = _memattn(qm_p.reshape(Bp, S, D), mem_kv_p, ROW_TILE).reshape(Tp, D)
    T = Tp + Ts
    *routed_p, cnt_p = _fin2(jnp.zeros((1, N_EXPERTS), F32), h1_p, om_p, wo_bf, vec(ln2_g), vec(ln2_b),
                             router_w[l], vec(router_b), total_rows=T)

    state_pad = jnp.pad(state_pool[l], ((0, 0), (1, 0), (0, 0)))
    us, qs, kvc_s, kvs_s, kvw_s, gs, pool_s = _inproj_sample(
        x_sample.reshape(Ts, D), w_in_bf, pool_w_bf, ps, state_pad, Ls, PAST_LEN)
    w_rows = state_win_kv.shape[2]
    feature_major = lambda a: jnp.transpose(a, (0, 2, 3, 4, 1))
    nsa_s, win_next = _nsa_sample(qs.reshape(Bd, Ls, NSA_WIDTH), gs.reshape(Bd, Ls, 3 * NSA_HEADS),
                        kvs_s.reshape(Bd, Ls, KV_WIDTH), kvw_s.reshape(Bd, Ls, KV_WIDTH),
                        feature_major(state_win_kv[l]), feature_major(cache_cmp_kv[l]),
                        feature_major(cache_slc_kv[l]), page_table, cw, PAST_LEN)
    h1_s, qm_s = _fin1(x_sample.reshape(Ts, D), pool_s, nsa_s.reshape(Ts, NSA_WIDTH), w_out_bf,
                       vec(ln1_g), vec(ln1_b), wq_bf, F32)
    om_s = _memattn_few(qm_s.reshape(Bd, Ls, D), cache_mem_kv[l]).reshape(Ts, D)
    h2, te, tg, cnt_s = _fin2(cnt_p, h1_s, om_s, wo_bf, vec(ln2_g), vec(ln2_b), router_w[l], vec(router_b),
                              total_rows=T, row_offset=Tp, into=routed_p)

    pos, tok_s, gate_s, units = _moe_routing(te, tg, cnt_s[0].astype(I32))
    y_rows = _moe_gmm(h2[tok_s], gate_s[:, None], units, exp_w_gu[l], exp_b_gu[l][:, None, :],
                      exp_w_dn[l], exp_b_dn[l][:, None, :])
    ys = [y_rows[pos[:, k]] for k in range(TOP_K)]
    y_prompt = _fin3(h2, ys, vec(ln3_g), vec(ln3_b), 0, Tp).reshape(Bp, S, D)
    y_sample = _fin3(h2, ys, vec(ln3_g), vec(ln3_b), Tp, Ts).reshape(Bd, Ls, D)

    kv6 = lambda a, b, n: a.reshape(1, b, n, 2, KVH, DH)
    row_major = lambda a: jnp.transpose(a, (0, 4, 1, 2, 3))
    win_p = kvw_p.reshape(Bp, S, KV_WIDTH)[:, S - min(WINDOW, S):]
    pool_state_p = up.reshape(Bp, S, POOL_WIDTH)[:, S - POOL_STATE:]
    pool_state_s = jnp.concatenate([state_pool[l], us.reshape(Bd, Ls, POOL_WIDTH)], axis=1)[:, -POOL_STATE:]
    return (y_prompt, y_sample,
            row_major(kvc_t)[None], row_major(kvs_t)[None], kv6(win_p, Bp, min(WINDOW, S)),
            pool_state_p[None], mem_kv_p.reshape(1, Bp, MEM_LEN, 2, MEM_HEADS, MEM_HEAD_DIM),
            kv6(kvc_s, Bd, Ls), kv6(kvs_s, Bd, Ls), row_major(win_next)[None], pool_state_s[None])
```

```python
import functools

import jax
import jax.numpy as jnp
from jax import lax
from jax.experimental import pallas as pl
from jax.experimental.pallas import tpu as pltpu

F32 = jnp.float32
BF16 = jnp.bfloat16
I32 = jnp.int32

D_MODEL = 1024
POOL_WIDTH = 512
POOL_WINDOWS = (2, 4, 8, 16)
POOL_GROUP = 128
POOL_STATE = 15
NSA_WIDTH = 512
DH = 64
NSA_HEADS = 8
KVH = 2
HPG = 4
CMP_LEN = 32
CMP_STRIDE = 16
CMP_HIDDEN = 256
SEL_LEN = 64
SEL_TOP = 16
WINDOW = 512
Q_BLOCK = 128
KV_WIDTH = 256
ATTN_SCALE = DH ** -0.5
FORCED_SCORE = 1e4
NEG = -1e30
MEM_LEN = 256
MEM_HEADS = 4
MEM_HEAD_DIM = 256
N_EXPERTS = 32
TOP_K = 4
D_FF = 1024
SWIGLU_LIMIT = 7.0
SWIGLU_ALPHA = 1.702
DN_ALPHA = 2.0 ** 0.25
LN_EPS = 1e-5
PAST_LEN = 2048
PAGE_SIZE = 128

LANES = 128
SEL_PAD = 128
KEY_TILE = 512
ROW_TILE = 512
MOE_ROWS = 512
BF16_ROWS = 16
VMEM_LIMIT = 56 * 1024 * 1024

HIGHEST = lax.Precision.HIGHEST


def _dot(a, b):
    return jnp.dot(a, b, preferred_element_type=F32)


def _dot_nt(a, b, precision=None):
    return lax.dot_general(a, b, (((1,), (1,)), ((), ())), preferred_element_type=F32,
                           precision=precision)


def _layer_norm(x, g, b):
    mu = jnp.mean(x, axis=-1, keepdims=True)
    xc = x - mu
    var = jnp.mean(xc * xc, axis=-1, keepdims=True)
    return xc * lax.rsqrt(var + LN_EPS) * g + b


def _params(sem, vmem=VMEM_LIMIT):
    return pltpu.CompilerParams(dimension_semantics=sem, vmem_limit_bytes=vmem)


def _split_store(u, up_ref, q_ref, kvc_ref, kvs_ref, kvw_ref, gate_ref):
    o1 = POOL_WIDTH
    o2 = o1 + NSA_WIDTH
    o3 = o2 + KV_WIDTH
    o4 = o3 + KV_WIDTH
    o5 = o4 + KV_WIDTH
    up_ref[...] = u[:, :o1]
    q_ref[...] = u[:, o1:o2]
    kvc_ref[...] = u[:, o2:o3]
    kvs_ref[...] = u[:, o3:o4]
    kvw_ref[...] = u[:, o4:o5]
    gate_ref[...] = 1.0 / (1.0 + jnp.exp(-u[:, o5:]))


def _inproj_prompt_body(x_ref, w_ref, pw_ref, ps_ref,
                        up_ref, q_ref, kvc_ref, kvs_ref, kvw_ref, gate_ref, pool_ref, kvct_ref, kvst_ref,
                        ext_ref, *, tm, tiles_per_seq):
    halo = POOL_STATE + 1
    t_in_seq = pl.program_id(0) % tiles_per_seq
    u = _dot(x_ref[...].astype(BF16), w_ref[...])
    _split_store(u, up_ref, q_ref, kvc_ref, kvs_ref, kvw_ref, gate_ref)
    o2 = POOL_WIDTH + NSA_WIDTH
    kvct_ref[0] = u[:, o2:o2 + KV_WIDTH].T.reshape(2, KVH, DH, tm)
    kvst_ref[0] = u[:, o2 + KV_WIDTH:o2 + 2 * KV_WIDTH].T.reshape(2, KVH, DH, tm)

    @pl.when(t_in_seq == 0)
    def _():
        ext_ref[0:halo, :] = jnp.zeros((halo, POOL_WIDTH), F32)

    ext_ref[halo:halo + tm, :] = u[:, :POOL_WIDTH]
    pos = t_in_seq * tm + lax.broadcasted_iota(I32, (tm, 1), 0)
    for gi, w in enumerate(POOL_WINDOWS):
        cols = slice(gi * POOL_GROUP, (gi + 1) * POOL_GROUP)
        acc = ext_ref[halo:halo + tm, cols]
        for k in range(1, w):
            acc = acc + ext_ref[halo - k:halo - k + tm, cols]
        cnt = jnp.minimum(pos + 1, w).astype(F32)
        d = acc / cnt - ext_ref[halo:halo + tm, cols]
        o = _dot(d.astype(BF16), pw_ref[gi])
        pool_ref[:, cols] = (o * ps_ref[:, cols]).astype(pool_ref.dtype)
    ext_ref[0:halo, :] = ext_ref[tm:tm + halo, :]


def _inproj_prompt(x2d, w_in_bf, pool_w_bf, pool_scale, seq_len):
    T = x2d.shape[0]
    tm = ROW_TILE
    outs = [POOL_WIDTH, NSA_WIDTH, KV_WIDTH, KV_WIDTH, KV_WIDTH, 3 * NSA_HEADS, POOL_WIDTH]
    row = lambda n: pl.BlockSpec((tm, n), lambda i: (i, 0))
    full = lambda a: pl.BlockSpec(a.shape, lambda i: (0,) * a.ndim)
    tps = seq_len // tm
    kvt_spec = pl.BlockSpec((1, 2, KVH, DH, tm), lambda i: (i // tps, 0, 0, 0, i % tps))
    kvt_shape = jax.ShapeDtypeStruct((T // seq_len, 2, KVH, DH, seq_len), F32)
    return pl.pallas_call(
        functools.partial(_inproj_prompt_body, tm=tm, tiles_per_seq=tps),
        grid=(T // tm,),
        in_specs=[row(D_MODEL), full(w_in_bf), full(pool_w_bf), full(pool_scale)],
        out_specs=[row(n) for n in outs] + [kvt_spec] * 2,
        out_shape=[jax.ShapeDtypeStruct((T, n), F32) for n in outs[:-1]]
                  + [jax.ShapeDtypeStruct((T, outs[-1]), BF16)] + [kvt_shape] * 2,
        scratch_shapes=[pltpu.VMEM((tm + POOL_STATE + 1, POOL_WIDTH), F32)],
        compiler_params=_params(("arbitrary",)),
        name="inproj_prompt",
    )(x2d, w_in_bf, pool_w_bf, pool_scale)


def _inproj_sample_body(x_ref, w_ref, pw_ref, ps_ref, st_ref,
                        up_ref, q_ref, kvc_ref, kvs_ref, kvw_ref, gate_ref, pool_ref,
                        ext_ref, *, nb, ls, pos0):
    halo = POOL_STATE + 1
    tm = nb * ls
    u = _dot(x_ref[...].astype(BF16), w_ref[...])
    _split_store(u, up_ref, q_ref, kvc_ref, kvs_ref, kvw_ref, gate_ref)
    ext_ref[:, 0:halo, :] = st_ref[...]
    ext_ref[:, halo:halo + ls, :] = u[:, :POOL_WIDTH].reshape(nb, ls, POOL_WIDTH)
    pos = pos0 + lax.broadcasted_iota(I32, (1, ls, 1), 1)
    for gi, w in enumerate(POOL_WINDOWS):
        cols = slice(gi * POOL_GROUP, (gi + 1) * POOL_GROUP)
        acc = ext_ref[:, halo:halo + ls, cols]
        for k in range(1, w):
            acc = acc + ext_ref[:, halo - k:halo - k + ls, cols]
        cnt = jnp.minimum(pos + 1, w).astype(F32)
        d = acc / cnt - ext_ref[:, halo:halo + ls, cols]
        o = _dot(d.reshape(tm, POOL_GROUP).astype(BF16), pw_ref[gi])
        pool_ref[:, cols] = (o * ps_ref[:, cols]).astype(pool_ref.dtype)


def _inproj_sample(x2d, w_in_bf, pool_w_bf, pool_scale, state_pad, ls, pos0):
    T = x2d.shape[0]
    nb = ROW_TILE // ls
    tm = nb * ls
    outs = [POOL_WIDTH, NSA_WIDTH, KV_WIDTH, KV_WIDTH, KV_WIDTH, 3 * NSA_HEADS, POOL_WIDTH]
    row = lambda n: pl.BlockSpec((tm, n), lambda i: (i, 0))
    full = lambda a: pl.BlockSpec(a.shape, lambda i: (0,) * a.ndim)
    return pl.pallas_call(
        functools.partial(_inproj_sample_body, nb=nb, ls=ls, pos0=pos0),
        grid=(T // tm,),
        in_specs=[row(D_MODEL), full(w_in_bf), full(pool_w_bf), full(pool_scale),
                  pl.BlockSpec((nb, POOL_STATE + 1, POOL_WIDTH), lambda i: (i, 0, 0))],
        out_specs=[row(n) for n in outs],
        out_shape=[jax.ShapeDtypeStruct((T, n), F32) for n in outs],
        scratch_shapes=[pltpu.VMEM((nb, POOL_STATE + 1 + ls, POOL_WIDTH), F32)],
        compiler_params=_params(("arbitrary",)),
        name="inproj_sample",
    )(x2d, w_in_bf, pool_w_bf, pool_scale, state_pad)


def _matmul_body(x_ref, w_ref, o_ref):
    o_ref[...] = _dot(x_ref[...].astype(BF16), w_ref[...].astype(BF16))


def _matmul(x, w, tn=512):
    M, K = x.shape
    N = w.shape[1]
    return pl.pallas_call(
        _matmul_body,
        grid=(N // tn,),
        in_specs=[pl.BlockSpec((M, K), lambda j: (0, 0)), pl.BlockSpec((K, tn), lambda j: (0, j))],
        out_specs=pl.BlockSpec((M, tn), lambda j: (0, j)),
        out_shape=jax.ShapeDtypeStruct((M, N), F32),
        compiler_params=_params(("arbitrary",)),
        name="mem_kv_proj",
    )(x, w)


def _gelu_tanh(x):
    c = 0.7978845608028654
    return 0.5 * x * (1.0 + jnp.tanh(c * (x + 0.044715 * (x * x * x))))


def _compress(kv_refs, n_chunks, w1_ref, pe_ref, b1_ref, w2_ref, b2_ref):
    lo = _lane_iota(n_chunks) < DH
    quads = CMP_STRIDE // 4
    outs = []
    for c in range(2):
        acc_a = jnp.zeros((2 * n_chunks, CMP_HIDDEN), F32)
        acc_b = jnp.zeros((2 * n_chunks, CMP_HIDDEN), F32)
        for i in range(quads):
            x = [kv_refs[c][pl.ds(4 * i + m, n_chunks, stride=CMP_STRIDE), :] for m in range(4)]
            r = [pltpu.roll(v, DH, 1) for v in x]
            x_g0 = jnp.concatenate([jnp.where(lo, x[0], r[1]), jnp.where(lo, x[2], r[3])], axis=1)
            x_g1 = jnp.concatenate([jnp.where(lo, r[0], x[1]), jnp.where(lo, r[2], x[3])], axis=1)
            xq = jnp.concatenate([x_g0, x_g1], axis=0)
            acc_a = acc_a + _dot((xq + pe_ref[c, i:i + 1, :]).astype(BF16), w1_ref[c, i])
            acc_b = acc_b + _dot((xq + pe_ref[c, quads + i:quads + i + 1, :]).astype(BF16), w1_ref[c, quads + i])
        hid = acc_a + pltpu.roll(acc_b, 2 * n_chunks - 1, 0) + b1_ref[c]
        outs.append(_dot(_gelu_tanh(hid).astype(BF16), w2_ref[c]) + b2_ref[c])
    return outs


def _compress_prompt_body(kv_ref, w1_ref, pe_ref, b1_ref, w2_ref, b2_ref, o_ref, k_ref, v_ref, *, n_chunks):
    k_ref[...] = kv_ref[0, :, :LANES]
    v_ref[...] = kv_ref[0, :, LANES:]
    keys, values = _compress((k_ref, v_ref), n_chunks, w1_ref, pe_ref, b1_ref, w2_ref, b2_ref)
    o_ref[0, 0] = keys
    o_ref[0, 1] = values


def _compress_prompt(kvc, cw):
    B, S, _ = kvc.shape
    n_chunks = S // CMP_STRIDE
    full = lambda a: pl.BlockSpec(a.shape, lambda b: (0,) * a.ndim)
    return pl.pallas_call(
        functools.partial(_compress_prompt_body, n_chunks=n_chunks),
        grid=(B,),
        in_specs=[pl.BlockSpec((1, S, KV_WIDTH), lambda b: (b, 0, 0))] + [full(a) for a in cw],
        out_specs=pl.BlockSpec((1, 2, KVH * n_chunks, LANES), lambda b: (b, 0, 0, 0)),
        out_shape=jax.ShapeDtypeStruct((B, 2, KVH * n_chunks, LANES), F32),
        scratch_shapes=[pltpu.VMEM((S, LANES), F32)] * 2,
        compiler_params=_params(("arbitrary",)),
        name="compress_prompt",
    )(kvc, *cw)


def _compress_weights(cmp_pe, cmp_w1, cmp_b1, cmp_w2, cmp_b2):
    nq = CMP_LEN // 4
    w1 = cmp_w1.reshape(2, nq, 4 * DH, CMP_HIDDEN).astype(BF16)
    pe = cmp_pe.reshape(2, nq, 4 * DH)
    b1 = cmp_b1[:, None, :]
    w2 = jnp.stack([jnp.concatenate([cmp_w2[0], jnp.zeros_like(cmp_w2[0])], axis=1),
                    jnp.concatenate([cmp_w2[1], cmp_w2[1]], axis=1)]).astype(BF16)
    b2 = jnp.stack([jnp.concatenate([cmp_b2[0], jnp.zeros_like(cmp_b2[0])]),
                    jnp.concatenate([cmp_b2[1], cmp_b2[1]])])[:, None, :]
    return w1, pe, b1, w2, b2


def _softmax_rows(s):
    e = jnp.exp(s - jnp.max(s, axis=-1, keepdims=True))
    return e * (1.0 / jnp.sum(e, axis=-1, keepdims=True))


def _lane_iota(n):
    return lax.broadcasted_iota(I32, (n, LANES), 1)


def _key_alibi_cols(pos, lane):
    hi = jnp.left_shift(jnp.right_shift(pos, 6), 6).astype(F32)
    lo = jnp.bitwise_and(pos, SEL_LEN - 1).astype(F32)
    return jnp.where(lane == DH, hi,
                     jnp.where(lane == DH + 1, lo,
                               jnp.where((lane == DH + 2) | (lane == DH + 3), 1.0, 0.0)))


def _key_alibi_rows(pos, sub):
    hi = jnp.left_shift(jnp.right_shift(pos, 6), 6).astype(F32)
    lo = jnp.bitwise_and(pos, SEL_LEN - 1).astype(F32)
    return jnp.where(sub == 0, hi, jnp.where(sub == 1, lo, jnp.where((sub == 2) | (sub == 3), 1.0, 0.0)))


def _halves(x, zero_hi=False):
    lo = _lane_iota(x.shape[0]) < DH
    r = pltpu.roll(x, DH, 1)
    if zero_hi:
        return jnp.where(lo, x, 0.0), jnp.where(lo, r, 0.0)
    return jnp.where(lo, x, r), jnp.where(lo, r, x)


def _fill_queries(qa_ref, qs, q_pos, lq):
    lane = _lane_iota(lq)
    lo_half = lane < DH
    q_hi = jnp.left_shift(jnp.right_shift(q_pos, 7), 7).astype(F32)
    q_lo = jnp.bitwise_and(q_pos, LANES - 1).astype(F32)
    for h in range(NSA_HEADS):
        g, hl = divmod(h, HPG)
        slope = 2.0 ** (-(h + 1))
        slab = qs[:, (h // 2) * LANES:(h // 2 + 1) * LANES]
        if h % 2:
            slab = pltpu.roll(slab, DH, 1)
        ex = jnp.where((lane == DH) | (lane == DH + 1), slope,
                       jnp.where(lane == DH + 2, -slope * q_hi,
                                 jnp.where(lane == DH + 3, -slope * q_lo,
                                           jnp.where(lane == DH + 4, NEG, 0.0))))
        qa_ref[g, hl * lq:(hl + 1) * lq, 0:LANES] = jnp.where(lo_half, slab, ex).astype(BF16)


def _stack4(x):
    return jnp.concatenate([x] * HPG, axis=0)


def _cmp_branch(qa_ref, kc_k, kc_v, q_pos4, lq, n_cmp):
    c_end = lax.broadcasted_iota(I32, (1, n_cmp), 1) * CMP_STRIDE + (CMP_LEN - 1)
    m_c = c_end <= q_pos4
    any_c = (q_pos4 >= CMP_LEN - 1).astype(F32)
    outs, psums = [], []
    for g in range(KVH):
        s = jnp.where(m_c, _dot_nt(qa_ref[g, :, 0:LANES], kc_k[g][...]), NEG)
        p = _softmax_rows(s) * any_c
        outs.append(_dot(p.astype(BF16), kc_v[g][...]))
        psums.append(p[0:lq] + p[lq:2 * lq] + p[2 * lq:3 * lq] + p[3 * lq:4 * lq])
    return outs, psums


def _split3(x):
    hi = x.astype(BF16)
    r1 = x - hi.astype(F32)
    mid = r1.astype(BF16)
    lo = (r1 - mid.astype(F32)).astype(BF16)
    return hi, mid, lo


def _top_blocks_t(imp_ts, pos0, n_blk=SEL_PAD):
    blk = lax.broadcasted_iota(I32, (n_blk, LANES), 0)
    qp_t = pos0 + lax.broadcasted_iota(I32, (n_blk, LANES), 1)
    cur = jnp.right_shift(qp_t, 6)
    forced = (blk == 0) | (blk == cur) | (blk == cur - 1)
    valid = jnp.left_shift(blk, 6) <= qp_t
    v = jnp.concatenate([jnp.where(valid, jnp.where(forced, FORCED_SCORE, t), -1.0) for t in imp_ts], axis=1)
    blk_f = lax.broadcasted_iota(I32, (n_blk, KVH * LANES), 0).astype(F32)
    sel = jnp.zeros((n_blk, KVH * LANES), F32)
    for _ in range(SEL_TOP):
        m = jnp.max(v, axis=0, keepdims=True)
        idx = jnp.min(jnp.where(v == m, blk_f, float(n_blk)), axis=0, keepdims=True)
        hit = blk_f == idx
        sel = jnp.where(hit, 1.0, sel)
        v = jnp.where(hit, -jnp.inf, v)
    out = [jnp.where((sel[:, g * LANES:(g + 1) * LANES] > 0.5) & valid, 0.0, NEG) for g in range(KVH)]
    if n_blk < SEL_PAD:
        out = [jnp.concatenate([b, jnp.full((SEL_PAD - n_blk, LANES), NEG, F32)], axis=0) for b in out]
    return out


def _select_blocks(psums, ovt_ref, pos0, lq, n_blk):
    imp_ts = []
    for g in range(KVH):
        ps = psums[g]
        if lq < LANES:
            ps = jnp.concatenate([ps, jnp.zeros((LANES - lq, ps.shape[1]), F32)], axis=0)
        imp_ts.append(_dot_nt(ovt_ref[0:n_blk, :], ps, precision=HIGHEST))
    return [b.T[:lq].astype(BF16) for b in _top_blocks_t(imp_ts, pos0, n_blk)]


def _store_selbias(qa_ref, selbias, lq):
    for g in range(KVH):
        for hl in range(HPG):
            qa_ref[g, hl * lq:(hl + 1) * lq, LANES:2 * LANES] = selbias[g]


def _combine(gates, eg_ref, o_c, o_s, o_w, lq):
    lo_half = _lane_iota(lq) < DH

    def assemble(per_group):
        slabs = []
        for k in range(NSA_HEADS // 2):
            g, hl = divmod(2 * k, HPG)
            a = per_group[g][hl * lq:(hl + 1) * lq]
            b = per_group[g][(hl + 1) * lq:(hl + 2) * lq]
            slabs.append(jnp.where(lo_half, a, b))
        return jnp.concatenate(slabs, axis=1)

    return (_dot(gates, eg_ref[0]) * assemble(o_c)
            + _dot(gates, eg_ref[1]) * assemble(o_s)
            + _dot(gates, eg_ref[2]) * assemble(o_w))


def _gate_expand():
    r = jnp.arange(3 * NSA_HEADS)
    c = jnp.arange(NSA_WIDTH)
    return jnp.stack([(r[:, None] == 3 * (c[None, :] // DH) + br).astype(F32) for br in range(3)])


def _overlap_t(n_cmp):
    n = jnp.arange(n_cmp)
    s = jnp.arange(SEL_PAD)
    c_first = n * CMP_STRIDE
    c_end = c_first + CMP_LEN - 1
    b_first = s * SEL_LEN
    return ((c_first[None, :] < b_first[:, None] + SEL_LEN) & (c_end[None, :] >= b_first[:, None])).astype(F32)


def _key_rows(kv_f32, pos, invalid=None):
    n = kv_f32.shape[0]
    lane = _lane_iota(n)
    ex = _key_alibi_cols(pos, lane)
    if invalid is not None:
        ex = jnp.where((lane == DH + 4) & invalid, 1.0, ex)
    lo = lane < DH
    return jnp.where(lo, kv_f32, ex), jnp.where(lo, pltpu.roll(kv_f32, DH, 1), ex)


def _block_onehot(pos, n):
    return (lax.broadcasted_iota(I32, (n, SEL_PAD), 1) == jnp.right_shift(pos, 6)).astype(BF16)


def _tile4(x):
    return jnp.concatenate([x] * HPG, axis=1)


def _nsa_prompt_body(q_ref, gate_ref, kc_ref, kvs_ref, w0_ref, w1_ref, w2_ref, w3_ref, w4_ref,
                     ovt_ref, egt_ref, wband_ref, o_ref,
                     kck0, kck1, kcv0, kcv1, ka0, ka1, vt0, vt1, qa0, qa1,
                     sa_ref, sb_ref, m_ref, l_ref, acc_ref, oc_ref, *, seq_len):
    j = pl.program_id(1)
    kc_k, kc_vt, kaug, v_t, qa_t = (kck0, kck1), (kcv0, kcv1), (ka0, ka1), (vt0, vt1), (qa0, qa1)
    n_cmp = kc_ref.shape[2] // KVH
    lq = Q_BLOCK
    cols = HPG * lq

    @pl.when(j == 0)
    def _():
        for g in range(KVH):
            kc_k[g][...] = kc_ref[0, 0, g * n_cmp:(g + 1) * n_cmp, :].astype(BF16)
            kc_vt[g][...] = kc_ref[0, 1, g * n_cmp:(g + 1) * n_cmp, :].T[0:DH].astype(BF16)

        def build(i, _):
            r0 = pl.multiple_of(i * KEY_TILE, KEY_TILE)
            pos = r0 + lax.broadcasted_iota(I32, (KEY_TILE, 1), 0)
            k0, k1 = _key_rows(kvs_ref[0, pl.ds(r0, KEY_TILE), :LANES], pos)
            onehot = _block_onehot(pos, KEY_TILE)
            ka0[pl.ds(r0, KEY_TILE), :] = jnp.concatenate([k0.astype(BF16), onehot], axis=1)
            ka1[pl.ds(r0, KEY_TILE), :] = jnp.concatenate([k1.astype(BF16), onehot], axis=1)
            vt = kvs_ref[0, pl.ds(r0, KEY_TILE), LANES:].T.astype(BF16)
            vt0[:, pl.ds(r0, KEY_TILE)] = vt[0:DH]
            vt1[:, pl.ds(r0, KEY_TILE)] = vt[DH:]
            return 0

        lax.fori_loop(0, seq_len // KEY_TILE, build, 0)

    st = j * Q_BLOCK
    q_pos = st + lax.broadcasted_iota(I32, (1, lq), 1)
    q_pos4 = _tile4(q_pos)

    q_t = (q_ref[...] * ATTN_SCALE).T
    sub = lax.broadcasted_iota(I32, (DH, lq), 0)
    q_hi = jnp.left_shift(jnp.right_shift(q_pos, 7), 7).astype(F32)
    q_lo = jnp.bitwise_and(q_pos, LANES - 1).astype(F32)
    for h in range(NSA_HEADS):
        g, hl = divmod(h, HPG)
        slope = 2.0 ** (-(h + 1))
        ex = jnp.where(sub <= 1, slope,
                       jnp.where(sub == 2, -slope * q_hi,
                                 jnp.where(sub == 3, -slope * q_lo, jnp.where(sub == 4, NEG, 0.0))))
        qa_t[g][0:DH, hl * lq:(hl + 1) * lq] = q_t[h * DH:(h + 1) * DH].astype(BF16)
        qa_t[g][DH:2 * DH, hl * lq:(hl + 1) * lq] = ex.astype(BF16)

    any_c = (q_pos4 >= CMP_LEN - 1).astype(F32)
    blocks_per_sel = SEL_LEN // CMP_STRIDE
    n_variants = -(-n_cmp // LANES)
    last_visible = (st + Q_BLOCK - CMP_LEN) // CMP_STRIDE

    def cmp_and_select(n_c):
        n_b = n_c // blocks_per_sel
        c_end = lax.broadcasted_iota(I32, (n_c, 1), 0) * CMP_STRIDE + (CMP_LEN - 1)
        m_c = c_end <= q_pos4
        ovt_bf = ovt_ref[0:n_b, 0:n_c].astype(BF16)
        imp_ts = []
        for g in range(KVH):
            s = jnp.where(m_c, _dot(kc_k[g][0:n_c, :], qa_t[g][0:LANES, :]), NEG)
            e = jnp.exp(s - jnp.max(s, axis=0, keepdims=True))
            p = e * (any_c / jnp.sum(e, axis=0, keepdims=True))
            oc_ref[g] = _dot(kc_vt[g][:, 0:n_c], p.astype(BF16))
            psum = p[:, 0:lq] + p[:, lq:2 * lq] + p[:, 2 * lq:3 * lq] + p[:, 3 * lq:4 * lq]
            imp_ts.append(sum(_dot(ovt_bf, t) for t in _split3(psum)))
        bias = _top_blocks_t(imp_ts, st, n_b)
        for g in range(KVH):
            for hl in range(HPG):
                qa_t[g][2 * DH:, hl * lq:(hl + 1) * lq] = bias[g].astype(BF16)

    variant = jnp.minimum(jnp.maximum(last_visible, 0) // LANES, n_variants - 1)
    for k in range(n_variants):
        pl.when(variant == k)(functools.partial(cmp_and_select, min(n_cmp, (k + 1) * LANES)))
    o_c = [oc_ref[g] for g in range(KVH)]

    n_tiles = (st + Q_BLOCK + KEY_TILE - 1) // KEY_TILE
    for g in range(KVH):
        m_ref[g] = jnp.full((1, cols), NEG, F32)
        l_ref[g] = jnp.zeros((1, cols), F32)
        acc_ref[g] = jnp.zeros((DH, cols), F32)

    def scores(t, s_ref):
        r0 = pl.multiple_of(t * KEY_TILE, KEY_TILE)
        for g in range(KVH):
            s_ref[g] = _dot(kaug[g][pl.ds(r0, KEY_TILE), :], qa_t[g][...])

    def consume(t, s_ref, masked):
        r0 = pl.multiple_of(t * KEY_TILE, KEY_TILE)
        for g in range(KVH):
            s = s_ref[g]
            if masked:
                k_pos = r0 + lax.broadcasted_iota(I32, (KEY_TILE, 1), 0)
                s = jnp.where(k_pos <= q_pos4, s, NEG)
            m = m_ref[g]
            m_new = jnp.maximum(m, jnp.max(s, axis=0, keepdims=True))
            a = jnp.exp(m - m_new)
            e = jnp.exp(s - m_new)
            m_ref[g] = m_new
            l_ref[g] = a * l_ref[g] + jnp.sum(e, axis=0, keepdims=True)
            acc_ref[g] = a * acc_ref[g] + _dot(v_t[g][:, pl.ds(r0, KEY_TILE)], e.astype(BF16))

    scores(0, sa_ref)
    n_pairs = (n_tiles - 1) // 2

    def pair(u, _):
        scores(2 * u + 1, sb_ref)
        consume(2 * u, sa_ref, False)
        scores(2 * u + 2, sa_ref)
        consume(2 * u + 1, sb_ref, False)
        return 0

    lax.fori_loop(0, n_pairs, pair, 0)
    odd_tail = (n_tiles - 1) - 2 * n_pairs == 1

    @pl.when(odd_tail)
    def _():
        scores(n_tiles - 1, sb_ref)
        consume(n_tiles - 2, sa_ref, False)
        consume(n_tiles - 1, sb_ref, True)

    @pl.when(jnp.logical_not(odd_tail))
    def _():
        consume(n_tiles - 1, sa_ref, True)

    o_s = [acc_ref[g] * (1.0 / l_ref[g]) for g in range(KVH)]

    band = jnp.concatenate([w0_ref[0], w1_ref[0], w2_ref[0], w3_ref[0], w4_ref[0]], axis=0)
    n_win = band.shape[0]
    w_pos_col = st - WINDOW + lax.broadcasted_iota(I32, (n_win, 1), 0)
    kw_k = _key_rows(band[:, :LANES], jnp.maximum(w_pos_col, 0), invalid=w_pos_col < 0)
    vw_t = band[:, LANES:].T.astype(BF16)
    band_bias = _tile4(wband_ref[...])
    o_w = []
    for g in range(KVH):
        s = _dot(kw_k[g].astype(BF16), qa_t[g][0:LANES, :]) + band_bias
        e = jnp.exp(s - jnp.max(s, axis=0, keepdims=True))
        o_w.append(_dot(vw_t[g * DH:(g + 1) * DH], e.astype(BF16)) * (1.0 / jnp.sum(e, axis=0, keepdims=True)))

    def heads(per_group):
        return jnp.concatenate([per_group[h // HPG][:, (h % HPG) * lq:(h % HPG + 1) * lq]
                                for h in range(NSA_HEADS)], axis=0)

    gates = gate_ref[...]
    out_t = (_dot_nt(egt_ref[0], gates) * heads(o_c)
             + _dot_nt(egt_ref[1], gates) * heads(o_s)
             + _dot_nt(egt_ref[2], gates) * heads(o_w))
    o_ref[...] = out_t.T.astype(o_ref.dtype)


def _gate_expand_t():
    r = jnp.arange(3 * NSA_HEADS)
    c = jnp.arange(NSA_WIDTH)
    return jnp.stack([(3 * (c[:, None] // DH) + br == r[None, :]).astype(F32) for br in range(3)])


def _nsa_prompt(q, gates, kc, kvs, kvw):
    B, S, _ = kvs.shape
    nqb = S // Q_BLOCK
    n_cmp = kc.shape[2] // KVH
    ovt = _overlap_t(n_cmp)
    egt = _gate_expand_t()
    n_band = WINDOW // Q_BLOCK + 1
    d_band = jnp.arange(Q_BLOCK)[None, :] + WINDOW - jnp.arange(n_band * Q_BLOCK)[:, None]
    wband = jnp.where((d_band >= 0) & (d_band < WINDOW), 0.0, NEG).astype(F32)

    def band_spec(i):
        return pl.BlockSpec((1, Q_BLOCK, KV_WIDTH),
                            lambda b, j, i=i: (b, jnp.maximum(j - (n_band - 1) + i, 0), 0))

    full = lambda a: pl.BlockSpec(a.shape, lambda b, j: (0,) * a.ndim)
    return pl.pallas_call(
        functools.partial(_nsa_prompt_body, seq_len=S),
        grid=(B, nqb),
        in_specs=[pl.BlockSpec((Q_BLOCK, NSA_WIDTH), lambda b, j: (b * nqb + j, 0)),
                  pl.BlockSpec((Q_BLOCK, 3 * NSA_HEADS), lambda b, j: (b * nqb + j, 0)),
                  pl.BlockSpec((1,) + kc.shape[1:], lambda b, j: (b, 0, 0, 0)),
                  pl.BlockSpec((1, S, KV_WIDTH), lambda b, j: (b, 0, 0))]
                 + [band_spec(i) for i in range(n_band)] + [full(ovt), full(egt), full(wband)],
        out_specs=pl.BlockSpec((Q_BLOCK, NSA_WIDTH), lambda b, j: (b * nqb + j, 0)),
        out_shape=jax.ShapeDtypeStruct((B * S, NSA_WIDTH), BF16),
        scratch_shapes=[pltpu.VMEM((n_cmp, LANES), BF16)] * 2
                       + [pltpu.VMEM((DH, n_cmp), BF16)] * 2
                       + [pltpu.VMEM((S, 2 * LANES), BF16)] * 2
                       + [pltpu.VMEM((DH, S), BF16)] * 2
                       + [pltpu.VMEM((2 * LANES, HPG * Q_BLOCK), BF16)] * 2
                       + [pltpu.VMEM((KVH, KEY_TILE, HPG * Q_BLOCK), F32)] * 2
                       + [pltpu.VMEM((KVH, 1, HPG * Q_BLOCK), F32)] * 2
                       + [pltpu.VMEM((KVH, DH, HPG * Q_BLOCK), F32)] * 2,
        compiler_params=_params(("arbitrary", "arbitrary")),
        name="nsa_prompt",
    )(q, gates, kc, kvs, *([kvw] * n_band), ovt, egt, wband)


def _nsa_sample_body(pt_ref, q_ref, gate_ref, kvs_new_ref, win_ref, kvw_new_ref, *rest,
                     n_pages, ls, past_len):
    cmp_pages = rest[:n_pages]
    slc_pages = rest[n_pages:2 * n_pages]
    (w1_ref, pe_ref, b1_ref, w2_ref, b2_ref, ovt_ref, eg_ref, o_ref, wout_ref,
     full_k, full_v, kck0, kck1, kcv0, kcv1, kt0, kt1, vt0, vt1, wkt0, wkt1, qa_ref) = rest[2 * n_pages:]
    del pt_ref
    kc_k, kc_v, kaug_t, v_t, wk_t = (kck0, kck1), (kcv0, kcv1), (kt0, kt1), (vt0, vt1), (wkt0, wkt1)
    n_cmp = past_len // CMP_STRIDE
    w_rows = win_ref.shape[4]
    lq = BF16_ROWS
    rows = HPG * lq
    w_start = past_len - w_rows

    @pl.when(pl.program_id(0) == 0)
    def _():
        sub = lax.broadcasted_iota(I32, (DH, past_len), 0)
        pos = lax.broadcasted_iota(I32, (1, past_len), 1)
        ex = _key_alibi_rows(pos, sub).astype(BF16)
        onehot = (lax.broadcasted_iota(I32, (SEL_PAD, past_len), 0) == jnp.right_shift(pos, 6)).astype(BF16)
        subw = lax.broadcasted_iota(I32, (DH, w_rows), 0)
        exw = _key_alibi_rows(w_start + lax.broadcasted_iota(I32, (1, w_rows), 1), subw).astype(BF16)
        for g in range(KVH):
            kaug_t[g][DH:2 * DH, :] = ex
            kaug_t[g][2 * DH:, :] = onehot
            wk_t[g][DH:, :] = exw

    for p in range(n_pages):
        cols = slice(p * PAGE_SIZE, (p + 1) * PAGE_SIZE)
        full_k[cols, :] = cmp_pages[p][0, 0].reshape(2 * DH, PAGE_SIZE).T
        full_v[cols, :] = cmp_pages[p][0, 1].reshape(2 * DH, PAGE_SIZE).T
        for g in range(KVH):
            kaug_t[g][0:DH, cols] = slc_pages[p][0, 0, g].astype(BF16)
            vt = slc_pages[p][0, 1, g].astype(BF16)
            v_t[g][0:DH, cols] = vt
            v_t[g][DH:, cols] = vt
    for g in range(KVH):
        wk_t[g][0:DH, :] = win_ref[0, 0, g].astype(BF16)

    keys_c, values_c = _compress((full_k, full_v), n_cmp, w1_ref, pe_ref, b1_ref, w2_ref, b2_ref)
    for g in range(KVH):
        kc_k[g][...] = keys_c[g * n_cmp:(g + 1) * n_cmp].astype(BF16)
        kc_v[g][...] = values_c[g * n_cmp:(g + 1) * n_cmp].astype(BF16)

    pad_q = jnp.zeros((lq - ls, NSA_WIDTH), F32)
    q_pos = past_len + lax.broadcasted_iota(I32, (lq, 1), 0)
    q_pos4 = _stack4(q_pos)
    _fill_queries(qa_ref, jnp.concatenate([q_ref[0] * ATTN_SCALE, pad_q], axis=0), q_pos, lq)
    gates = jnp.concatenate([gate_ref[0], jnp.zeros((lq - ls, 3 * NSA_HEADS), F32)], axis=0)

    o_c, psums = _cmp_branch(qa_ref, kc_k, kc_v, q_pos4, lq, n_cmp)
    n_sel = -(-(past_len + lq) // SEL_LEN)
    n_blk = -(-n_sel // 8) * 8
    _store_selbias(qa_ref, _select_blocks(psums, ovt_ref, past_len, lq, n_blk), lq)

    pad_k = jnp.zeros((LANES - ls, KV_WIDTH), F32)
    new_pos_col = past_len + lax.broadcasted_iota(I32, (LANES, 1), 0)
    new_pos = past_len + lax.broadcasted_iota(I32, (1, LANES), 1)
    new_s = jnp.concatenate([kvs_new_ref[0], pad_k], axis=0)
    new_w = jnp.concatenate([kvw_new_ref[0], pad_k], axis=0)
    ks_new = _key_rows(new_s[:, :LANES], new_pos_col)
    vs_new = _halves(new_s[:, LANES:])
    kw_new = _key_rows(new_w[:, :LANES], new_pos_col)
    vw_new = _halves(new_w[:, LANES:])
    onehot_new = _block_onehot(new_pos_col, LANES)
    causal_new = q_pos4 >= new_pos

    o_s, o_w = [], []
    d_past = q_pos4 - (w_start + lax.broadcasted_iota(I32, (1, w_rows), 1))
    m_past = (d_past >= 0) & (d_past < WINDOW)
    d_new = q_pos4 - new_pos
    m_new = (d_new >= 0) & (d_new < WINDOW)
    for g in range(KVH):
        s_past = _dot(qa_ref[g], kaug_t[g][...])
        k_new = jnp.concatenate([ks_new[g].astype(BF16), onehot_new], axis=1)
        s_new = jnp.where(causal_new, _dot_nt(qa_ref[g], k_new), NEG)
        m = jnp.maximum(jnp.max(s_past, axis=-1, keepdims=True), jnp.max(s_new, axis=-1, keepdims=True))
        e_past = jnp.exp(s_past - m)
        e_new = jnp.exp(s_new - m)
        den = jnp.sum(e_past, axis=-1, keepdims=True) + jnp.sum(e_new, axis=-1, keepdims=True)
        acc = _dot_nt(e_past.astype(BF16), v_t[g][...]) + _dot(e_new.astype(BF16), vs_new[g].astype(BF16))
        o_s.append(acc / den)

        sw_past = jnp.where(m_past, _dot(qa_ref[g, :, 0:LANES], wk_t[g][...]), NEG)
        sw_new = jnp.where(m_new, _dot_nt(qa_ref[g, :, 0:LANES], kw_new[g].astype(BF16)), NEG)
        m = jnp.maximum(jnp.max(sw_past, axis=-1, keepdims=True), jnp.max(sw_new, axis=-1, keepdims=True))
        e_past = jnp.exp(sw_past - m)
        e_new = jnp.exp(sw_new - m)
        den = jnp.sum(e_past, axis=-1, keepdims=True) + jnp.sum(e_new, axis=-1, keepdims=True)
        vw = win_ref[0, 1, g].astype(BF16)
        vw2 = jnp.concatenate([vw, vw], axis=0)
        acc = _dot_nt(e_past.astype(BF16), vw2) + _dot(e_new.astype(BF16), vw_new[g].astype(BF16))
        o_w.append(acc / den)

    o_ref[0] = _combine(gates, eg_ref, o_c, o_s, o_w, lq)[:ls]

    keep = w_rows - ls
    tail_lane = lax.broadcasted_iota(I32, (DH, LANES), 1) >= keep % LANES
    for c in range(2):
        new_t = pltpu.roll(new_w[:, c * LANES:(c + 1) * LANES].T, keep % LANES, 1)
        for g in range(KVH):
            shifted = pltpu.roll(win_ref[0, c, g], keep, 1)
            last = jnp.where(tail_lane, new_t[g * DH:(g + 1) * DH], shifted[:, w_rows - LANES:])
            wout_ref[0, c, g] = jnp.concatenate([shifted[:, :w_rows - LANES], last], axis=1)


def _nsa_sample(q, gates, kvs_new, kvw_new, win_t, cmp_t, slc_t, page_table, cw, past_len):
    Bd, ls, _ = q.shape
    n_pages = page_table.shape[1]
    n_cmp = past_len // CMP_STRIDE
    ovt = _overlap_t(n_cmp)
    eg = _gate_expand()
    w_rows = win_t.shape[4]

    per_b = lambda a: pl.BlockSpec((1,) + a.shape[1:], lambda b, pt: (b,) + (0,) * (a.ndim - 1))
    full = lambda a: pl.BlockSpec(a.shape, lambda b, pt: (0,) * a.ndim)
    page = lambda p: pl.BlockSpec((1, 2, KVH, DH, PAGE_SIZE), lambda b, pt, p=p: (pt[b, p], 0, 0, 0, 0))
    grid_spec = pltpu.PrefetchScalarGridSpec(
        num_scalar_prefetch=1,
        grid=(Bd,),
        in_specs=[per_b(q), per_b(gates), per_b(kvs_new), per_b(win_t), per_b(kvw_new)]
                 + [page(p) for p in range(n_pages)] * 2
                 + [full(a) for a in cw] + [full(ovt), full(eg)],
        out_specs=[pl.BlockSpec((1, ls, NSA_WIDTH), lambda b, pt: (b, 0, 0)), per_b(win_t)],
        scratch_shapes=[pltpu.VMEM((past_len, LANES), F32)] * 2
                       + [pltpu.VMEM((n_cmp, LANES), BF16)] * 4
                       + [pltpu.VMEM((2 * LANES, past_len), BF16)] * 2
                       + [pltpu.VMEM((LANES, past_len), BF16)] * 2
                       + [pltpu.VMEM((LANES, w_rows), BF16)] * 2
                       + [pltpu.VMEM((KVH, HPG * BF16_ROWS, 2 * LANES), BF16)],
    )
    return pl.pallas_call(
        functools.partial(_nsa_sample_body, n_pages=n_pages, ls=ls, past_len=past_len),
        grid_spec=grid_spec,
        out_shape=[jax.ShapeDtypeStruct((Bd, ls, NSA_WIDTH), F32), jax.ShapeDtypeStruct(win_t.shape, F32)],
        compiler_params=_params(("arbitrary",)),
        name="nsa_sample",
    )(page_table, q, gates, kvs_new, win_t, kvw_new,
      *([cmp_t] * n_pages), *([slc_t] * n_pages), *cw, ovt, eg)


def _fin1_body(h_ref, pool_ref, nsa_ref, wo_ref, g_ref, b_ref, wq_ref, h1_ref, qm_ref):
    mix = (_dot(pool_ref[...].astype(BF16), wo_ref[0:POOL_WIDTH, :])
           + _dot(nsa_ref[...].astype(BF16), wo_ref[POOL_WIDTH:, :]))
    h1 = _layer_norm(DN_ALPHA * h_ref[...] + mix, g_ref[...], b_ref[...])
    h1_ref[...] = h1
    qm_ref[...] = (_dot(h1.astype(BF16), wq_ref[...]) * (MEM_HEAD_DIM ** -0.5)).astype(qm_ref.dtype)


def _fin1(h, pool_o, nsa_o, w_out_bf, g, b, wq_bf, q_dtype):
    T = h.shape[0]
    tm = ROW_TILE
    row = lambda n: pl.BlockSpec((tm, n), lambda i: (i, 0))
    full = lambda a: pl.BlockSpec(a.shape, lambda i: (0,) * a.ndim)
    return pl.pallas_call(
        _fin1_body,
        grid=(T // tm,),
        in_specs=[row(D_MODEL), row(POOL_WIDTH), row(NSA_WIDTH), full(w_out_bf), full(g), full(b), full(wq_bf)],
        out_specs=[row(D_MODEL), row(D_MODEL)],
        out_shape=[jax.ShapeDtypeStruct((T, D_MODEL), F32), jax.ShapeDtypeStruct((T, D_MODEL), q_dtype)],
        compiler_params=_params(("arbitrary",)),
        name="out_proj_ln1",
    )(h, pool_o, nsa_o, w_out_bf, g, b, wq_bf)


def _memattn_body(q_ref, kv_ref, o_ref):
    width = MEM_HEADS * MEM_HEAD_DIM
    for h in range(MEM_HEADS):
        cols = slice(h * MEM_HEAD_DIM, (h + 1) * MEM_HEAD_DIM)
        qh = q_ref[0, :, cols].astype(BF16)
        kh = kv_ref[0, :, cols].astype(BF16)
        vh = kv_ref[0, :, width + h * MEM_HEAD_DIM:width + (h + 1) * MEM_HEAD_DIM].astype(BF16)
        s = _dot_nt(qh, kh)
        e = jnp.exp(s - jnp.max(s, axis=-1, keepdims=True))
        o = _dot(e.astype(BF16), vh) * (1.0 / jnp.sum(e, axis=-1, keepdims=True))
        o_ref[0, :, cols] = o.astype(o_ref.dtype)


def _memattn(qm, mem_kv, tq):
    nb, L, W = qm.shape
    return pl.pallas_call(
        _memattn_body,
        grid=(nb, L // tq),
        in_specs=[pl.BlockSpec((1, tq, W), lambda b, t: (b, t, 0)),
                  pl.BlockSpec((1, MEM_LEN, 2 * W), lambda b, t: (b, 0, 0))],
        out_specs=pl.BlockSpec((1, tq, W), lambda b, t: (b, t, 0)),
        out_shape=jax.ShapeDtypeStruct((nb, L, W), BF16),
        compiler_params=_params(("arbitrary", "arbitrary")),
        name="mem_attn",
    )(qm, mem_kv)


def _memattn_few_body(q_ref, kv_ref, o_ref):
    lq = q_ref.shape[1]
    n_keys = MEM_LEN * MEM_HEADS
    q = q_ref[0]
    qs = jnp.concatenate([q[:, h * MEM_HEAD_DIM:(h + 1) * MEM_HEAD_DIM] for h in range(MEM_HEADS)], axis=0)
    k = kv_ref[0, :, 0, :, :].reshape(n_keys, MEM_HEAD_DIM).astype(BF16)
    v = kv_ref[0, :, 1, :, :].reshape(n_keys, MEM_HEAD_DIM).astype(BF16)
    s = _dot_nt(qs.astype(BF16), k)
    assert lq & (lq - 1) == 0 and MEM_HEADS & (MEM_HEADS - 1) == 0
    col_h = jnp.bitwise_and(lax.broadcasted_iota(I32, s.shape, 1), MEM_HEADS - 1)
    row_h = jnp.right_shift(lax.broadcasted_iota(I32, s.shape, 0), lq.bit_length() - 1)
    s = jnp.where(col_h == row_h, s, NEG)
    e = jnp.exp(s - jnp.max(s, axis=-1, keepdims=True))
    o = _dot(e.astype(BF16), v) * (1.0 / jnp.sum(e, axis=-1, keepdims=True))
    for h in range(MEM_HEADS):
        o_ref[0, :, h * MEM_HEAD_DIM:(h + 1) * MEM_HEAD_DIM] = o[h * lq:(h + 1) * lq]


def _memattn_few(qm, mem_kv):
    nb, lq, W = qm.shape
    return pl.pallas_call(
        _memattn_few_body,
        grid=(nb,),
        in_specs=[pl.BlockSpec((1, lq, W), lambda b: (b, 0, 0)),
                  pl.BlockSpec((1, MEM_LEN, 2, MEM_HEADS, MEM_HEAD_DIM), lambda b: (b, 0, 0, 0, 0))],
        out_specs=pl.BlockSpec((1, lq, W), lambda b: (b, 0, 0)),
        out_shape=jax.ShapeDtypeStruct((nb, lq, W), F32),
        compiler_params=_params(("arbitrary",)),
        name="mem_attn_few",
    )(qm, mem_kv)


def _fin2_body(cnt0_ref, h1_ref, o_ref, wo_ref, g_ref, b_ref, rw_ref, rb_ref, *rest):
    h2_ref, te_ref, tg_ref, cnt_ref, run_ref = rest[-5:]
    tm = h1_ref.shape[0]

    @pl.when(pl.program_id(0) == 0)
    def _():
        run_ref[...] = cnt0_ref[...]

    a = _dot(o_ref[...].astype(BF16), wo_ref[...])
    h2 = _layer_norm(DN_ALPHA * h1_ref[...] + a, g_ref[...], b_ref[...])
    h2_ref[...] = h2
    a_hi, a_lo, _ = _split3(h2)
    w_hi, w_lo, _ = _split3(rw_ref[...])
    logits = _dot(a_hi, w_hi) + (_dot(a_hi, w_lo) + _dot(a_lo, w_hi)) + rb_ref[...]
    e_iota = lax.broadcasted_iota(I32, (tm, N_EXPERTS), 1).astype(F32)
    lane = lax.broadcasted_iota(I32, (tm, LANES), 1)
    te = jnp.zeros((tm, LANES), F32)
    tv = jnp.full((tm, LANES), NEG, F32)
    work = logits
    chosen = []
    for k in range(TOP_K):
        m = jnp.max(work, axis=-1, keepdims=True)
        idx = jnp.min(jnp.where(work == m, e_iota, float(N_EXPERTS)), axis=-1, keepdims=True)
        hit = e_iota == idx
        chosen.append(hit)
        te = jnp.where(lane == k, idx, te)
        tv = jnp.where(lane == k, m, tv)
        work = jnp.where(hit, -jnp.inf, work)
    member = sum(c.astype(F32) for c in chosen)
    earlier = (lax.broadcasted_iota(I32, (tm, tm), 0) > lax.broadcasted_iota(I32, (tm, tm), 1)).astype(BF16)
    before = _dot(earlier, member.astype(BF16)) + run_ref[...]
    for k in range(TOP_K):
        rank = jnp.sum(jnp.where(chosen[k], before, 0.0), axis=-1, keepdims=True)
        te = jnp.where(lane == TOP_K + k, rank, te)
    run_ref[...] = run_ref[...] + jnp.sum(member, axis=0, keepdims=True)
    cnt_ref[...] = run_ref[...]
    ex = jnp.exp(tv - jnp.max(tv, axis=-1, keepdims=True))
    te_ref[...] = te.astype(I32)
    tg_ref[...] = ex / jnp.sum(ex, axis=-1, keepdims=True)


def _fin2(cnt0, h1, o, wo_bf, g, b, rw, rb, total_rows, row_offset=0, into=None):
    T = h1.shape[0]
    tm = ROW_TILE
    blk0 = row_offset // tm
    row = lambda n: pl.BlockSpec((tm, n), lambda i: (i, 0))
    out_row = lambda n: pl.BlockSpec((tm, n), lambda i: (i + blk0, 0))
    full = lambda a: pl.BlockSpec(a.shape, lambda i: (0,) * a.ndim)
    ins = [cnt0, h1, o, wo_bf, g, b, rw, rb]
    in_specs = [full(cnt0), row(D_MODEL), row(D_MODEL), full(wo_bf), full(g), full(b), full(rw), full(rb)]
    aliases = {}
    if into is not None:
        aliases = {len(ins) + k: k for k in range(len(into))}
        in_specs = in_specs + [pl.BlockSpec(memory_space=pl.ANY)] * len(into)
        ins = ins + list(into)
    return pl.pallas_call(
        _fin2_body,
        grid=(T // tm,),
        in_specs=in_specs,
        out_specs=[out_row(D_MODEL), out_row(LANES), out_row(LANES), full(cnt0)],
        out_shape=[jax.ShapeDtypeStruct((total_rows, D_MODEL), F32), jax.ShapeDtypeStruct((total_rows, LANES), I32),
                   jax.ShapeDtypeStruct((total_rows, LANES), F32), jax.ShapeDtypeStruct(cnt0.shape, F32)],
        scratch_shapes=[pltpu.VMEM(cnt0.shape, F32)],
        input_output_aliases=aliases,
        compiler_params=_params(("arbitrary",)),
        name="mem_out_ln2_router",
    )(*ins)


def _moe_body(ut_ref, ue_ref, nu_ref, rs_ref, re_ref, x_ref, wgu_ref, bgu_ref, wdn_ref, bdn_ref,
              y_ref, wgu_bf, wdn_bf):
    u = pl.program_id(0)
    bk = x_ref.shape[0]
    e = ue_ref[u]
    tile = ut_ref[u]
    prev = jnp.maximum(u - 1, 0)

    @pl.when((u == 0) | (e != ue_ref[prev]))
    def _():
        wgu_bf[...] = wgu_ref[0].astype(BF16)
        wdn_bf[...] = wdn_ref[0].astype(BF16)

    @pl.when(u < nu_ref[0])
    def _():
        x = x_ref[...].astype(BF16)
        g = _dot(x, wgu_bf[:, :D_FF]) + bgu_ref[0, :, :D_FF]
        v = _dot(x, wgu_bf[:, D_FF:]) + bgu_ref[0, :, D_FF:]
        g = jnp.minimum(g, SWIGLU_LIMIT)
        v = jnp.clip(v, -SWIGLU_LIMIT, SWIGLU_LIMIT)
        a = g * (1.0 / (1.0 + jnp.exp(-SWIGLU_ALPHA * g))) * (v + 1.0)
        y = _dot(a.astype(BF16), wdn_bf[...]) + bdn_ref[0]
        row = tile * bk + lax.broadcasted_iota(I32, (bk, 1), 0)
        mine = (row >= rs_ref[e]) & (row < re_ref[e])
        y = jnp.where(mine, y, 0.0)

        @pl.when((u == 0) | (tile != ut_ref[prev]))
        def _():
            y_ref[...] = y

        @pl.when((u > 0) & (tile == ut_ref[prev]))
        def _():
            y_ref[...] = y_ref[...] + y


def _moe_gmm(x_rows, units, w_gu, b_gu, w_dn, b_dn):
    N = x_rows.shape[0]
    bk = MOE_ROWS
    unit_tile, unit_e, n_units, r_start, r_end = units
    grid_spec = pltpu.PrefetchScalarGridSpec(
        num_scalar_prefetch=5,
        grid=(unit_tile.shape[0],),
        in_specs=[pl.BlockSpec((bk, D_MODEL), lambda u, ut, ue, *_: (ut[u], 0)),
                  pl.BlockSpec((1, D_MODEL, 2 * D_FF), lambda u, ut, ue, *_: (ue[u], 0, 0)),
                  pl.BlockSpec((1, 1, 2 * D_FF), lambda u, ut, ue, *_: (ue[u], 0, 0)),
                  pl.BlockSpec((1, D_FF, D_MODEL), lambda u, ut, ue, *_: (ue[u], 0, 0)),
                  pl.BlockSpec((1, 1, D_MODEL), lambda u, ut, ue, *_: (ue[u], 0, 0))],
        out_specs=pl.BlockSpec((bk, D_MODEL), lambda u, ut, ue, *_: (ut[u], 0)),
        scratch_shapes=[pltpu.VMEM((D_MODEL, 2 * D_FF), BF16), pltpu.VMEM((D_FF, D_MODEL), BF16)],
    )
    return pl.pallas_call(
        _moe_body,
        grid_spec=grid_spec,
        out_shape=jax.ShapeDtypeStruct((N, D_MODEL), F32),
        compiler_params=_params(("arbitrary",)),
        name="moe_experts",
    )(unit_tile, unit_e, n_units, r_start, r_end, x_rows, w_gu, b_gu, w_dn, b_dn)


FLAT_BITS = 17


def _moe_routing(te, counts):
    bk = MOE_ROWS
    T = te.shape[0]
    N = T * TOP_K
    assert N % bk == 0 and N <= (1 << FLAT_BITS)
    experts = jnp.arange(N_EXPERTS, dtype=I32)
    top_e = te[:, :TOP_K]
    r_end = jnp.cumsum(counts).astype(I32)
    r_start = r_end - counts
    onehot = top_e[:, :, None] == experts[None, None, :]
    pos = jnp.sum(jnp.where(onehot, r_start[None, None, :], 0), axis=-1) + te[:, TOP_K:2 * TOP_K]
    key = jnp.left_shift(top_e.reshape(-1), FLAT_BITS) + jnp.arange(N, dtype=I32)
    key_s = lax.sort(key)
    tok_s = jnp.right_shift(jnp.bitwise_and(key_s, (1 << FLAT_BITS) - 1), 2)
    first = r_start // bk
    last = (r_end - 1) // bk
    n_e = jnp.where(counts > 0, last - first + 1, 0)
    u_end = jnp.cumsum(n_e).astype(I32)
    u_start = u_end - n_e
    n_units = u_end[-1]
    u = jnp.minimum(jnp.arange(N // bk + N_EXPERTS - 1, dtype=I32), n_units - 1)
    unit_e = jnp.sum((u[:, None] >= u_end[None, :]).astype(I32), axis=1)
    unit_tile = u + jnp.sum(jnp.where(unit_e[:, None] == experts[None, :], (first - u_start)[None, :], 0), axis=1)
    return pos, tok_s, (unit_tile, unit_e, n_units.reshape(1), r_start, r_end)


def _fin3_body(h2_ref, tg_ref, y0_ref, y1_ref, y2_ref, y3_ref, g_ref, b_ref, o_ref):
    gate = lambda k: tg_ref[:, k:k + 1]
    y = ((gate(0) * y0_ref[...] + gate(1) * y1_ref[...])
         + (gate(2) * y2_ref[...] + gate(3) * y3_ref[...]))
    o_ref[...] = _layer_norm(DN_ALPHA * h2_ref[...] + y, g_ref[...], b_ref[...])


def _fin3(h2, tg, ys, g, b, row_offset, T):
    tm = ROW_TILE
    blk0 = row_offset // tm
    row = lambda n: pl.BlockSpec((tm, n), lambda i: (i, 0))
    full = lambda a: pl.BlockSpec(a.shape, lambda i: (0,) * a.ndim)
    return pl.pallas_call(
        _fin3_body,
        grid=(T // tm,),
        in_specs=[pl.BlockSpec((tm, D_MODEL), lambda i: (i + blk0, 0)), pl.BlockSpec((tm, LANES), lambda i: (i + blk0, 0))]
                 + [pl.BlockSpec((tm, D_MODEL), lambda i: (i + blk0, 0))] * TOP_K + [full(g), full(b)],
        out_specs=row(D_MODEL),
        out_shape=jax.ShapeDtypeStruct((T, D_MODEL), F32),
        compiler_params=_params(("arbitrary",)),
        name="combine_ln3",
    )(h2, tg, *ys, g, b)


def kernel(x_prompt, x_sample, cache_cmp_kv, cache_slc_kv, state_win_kv, state_pool, cache_mem_kv, page_table,
           mem_prompt, w_in, pool_w, pool_scale, cmp_pe, cmp_w1, cmp_b1, cmp_w2, cmp_b2, w_out, ln1_g, ln1_b,
           mem_wq, mem_wkv, mem_wo, ln2_g, ln2_b, router_w, router_b, exp_w_gu, exp_b_gu, exp_w_dn, exp_b_dn,
           ln3_g, ln3_b):
    Bp, S, D = x_prompt.shape
    Bd, Ls, _ = x_sample.shape
    Tp, Ts = Bp * S, Bd * Ls
    l = 0
    w_in_bf = w_in[l].astype(BF16)
    pool_w_bf = pool_w[l].astype(BF16)
    ps = pool_scale[l][None, :]
    cw = _compress_weights(cmp_pe[l], cmp_w1[l], cmp_b1[l], cmp_w2[l], cmp_b2[l])
    w_out_bf = w_out[l].astype(BF16)
    wq_bf = mem_wq[l].astype(BF16)
    wo_bf = mem_wo[l].astype(BF16)
    vec = lambda a: a[l][None, :]

    up, qp, kvc_p, kvs_p, kvw_p, gp, pool_p, kvc_t, kvs_t = _inproj_prompt(
        x_prompt.reshape(Tp, D), w_in_bf, pool_w_bf, ps, S)
    kc_p = _compress_prompt(kvc_p.reshape(Bp, S, KV_WIDTH), cw)
    nsa_p = _nsa_prompt(qp, gp, kc_p, kvs_p.reshape(Bp, S, KV_WIDTH), kvw_p.reshape(Bp, S, KV_WIDTH))
    mem_kv_p = _matmul(mem_prompt.reshape(Bp * MEM_LEN, D), mem_wkv[l]).reshape(Bp, MEM_LEN, 2 * D)
    h1_p, qm_p = _fin1(x_prompt.reshape(Tp, D), pool_p, nsa_p, w_out_bf, vec(ln1_g), vec(ln1_b), wq_bf, BF16)
    om_p = _memattn(qm_p.reshape(Bp, S, D), mem_kv_p, ROW_TILE).reshape(Tp, D)
    T = Tp + Ts
    *routed_p, cnt_p = _fin2(jnp.zeros((1, N_EXPERTS), F32), h1_p, om_p, wo_bf, vec(ln2_g), vec(ln2_b),
                             router_w[l], vec(router_b), total_rows=T)

    state_pad = jnp.pad(state_pool[l], ((0, 0), (1, 0), (0, 0)))
    us, qs, kvc_s, kvs_s, kvw_s, gs, pool_s = _inproj_sample(
        x_sample.reshape(Ts, D), w_in_bf, pool_w_bf, ps, state_pad, Ls, PAST_LEN)
    feature_major = lambda a: jnp.transpose(a, (0, 2, 3, 4, 1))
    nsa_s, win_next = _nsa_sample(qs.reshape(Bd, Ls, NSA_WIDTH), gs.reshape(Bd, Ls, 3 * NSA_HEADS),
                        kvs_s.reshape(Bd, Ls, KV_WIDTH), kvw_s.reshape(Bd, Ls, KV_WIDTH),
                        feature_major(state_win_kv[l]), feature_major(cache_cmp_kv[l]),
                        feature_major(cache_slc_kv[l]), page_table, cw, PAST_LEN)
    h1_s, qm_s = _fin1(x_sample.reshape(Ts, D), pool_s, nsa_s.reshape(Ts, NSA_WIDTH), w_out_bf,
                       vec(ln1_g), vec(ln1_b), wq_bf, F32)
    om_s = _memattn_few(qm_s.reshape(Bd, Ls, D), cache_mem_kv[l]).reshape(Ts, D)
    h2, te, tg, cnt_s = _fin2(cnt_p, h1_s, om_s, wo_bf, vec(ln2_g), vec(ln2_b), router_w[l], vec(router_b),
                              total_rows=T, row_offset=Tp, into=routed_p)

    pos, tok_s, units = _moe_routing(te, cnt_s[0].astype(I32))
    y_rows = _moe_gmm(h2[tok_s], units, exp_w_gu[l], exp_b_gu[l][:, None, :],
                      exp_w_dn[l], exp_b_dn[l][:, None, :])
    ys = [y_rows[pos[:, k]] for k in range(TOP_K)]
    y_prompt = _fin3(h2, tg, ys, vec(ln3_g), vec(ln3_b), 0, Tp).reshape(Bp, S, D)
    y_sample = _fin3(h2, tg, ys, vec(ln3_g), vec(ln3_b), Tp, Ts).reshape(Bd, Ls, D)

    kv6 = lambda a, b, n: a.reshape(1, b, n, 2, KVH, DH)
    row_major = lambda a: jnp.transpose(a, (0, 4, 1, 2, 3))
    win_p = kvw_p.reshape(Bp, S, KV_WIDTH)[:, S - min(WINDOW, S):]
    pool_state_p = up.reshape(Bp, S, POOL_WIDTH)[:, S - POOL_STATE:]
    pool_state_s = jnp.concatenate([state_pool[l], us.reshape(Bd, Ls, POOL_WIDTH)], axis=1)[:, -POOL_STATE:]
    return (y_prompt, y_sample,
            row_major(kvc_t)[None], row_major(kvs_t)[None], kv6(win_p, Bp, min(WINDOW, S)),
            pool_state_p[None], mem_kv_p.reshape(1, Bp, MEM_LEN, 2, MEM_HEADS, MEM_HEAD_DIM),
            kv6(kvc_s, Bd, Ls), kv6(kvs_s, Bd, Ls), row_major(win_next)[None], pool_state_s[None])
```

```python
import functools

import jax
import jax.numpy as jnp
from jax import lax
from jax.experimental import pallas as pl
from jax.experimental.pallas import tpu as pltpu

F32 = jnp.float32
BF16 = jnp.bfloat16
I32 = jnp.int32

D_MODEL = 1024
POOL_WIDTH = 512
POOL_WINDOWS = (2, 4, 8, 16)
POOL_GROUP = 128
POOL_STATE = 15
NSA_WIDTH = 512
DH = 64
NSA_HEADS = 8
KVH = 2
HPG = 4
CMP_LEN = 32
CMP_STRIDE = 16
CMP_HIDDEN = 256
SEL_LEN = 64
SEL_TOP = 16
WINDOW = 512
Q_BLOCK = 128
KV_WIDTH = 256
ATTN_SCALE = DH ** -0.5
FORCED_SCORE = 1e4
NEG = -1e30
MEM_LEN = 256
MEM_HEADS = 4
MEM_HEAD_DIM = 256
N_EXPERTS = 32
TOP_K = 4
D_FF = 1024
SWIGLU_LIMIT = 7.0
SWIGLU_ALPHA = 1.702
DN_ALPHA = 2.0 ** 0.25
LN_EPS = 1e-5
PAST_LEN = 2048
PAGE_SIZE = 128

LANES = 128
SEL_PAD = 128
KEY_TILE = 512
ROW_TILE = 512
MOE_ROWS = 512
BF16_ROWS = 16
VMEM_LIMIT = 56 * 1024 * 1024

HIGHEST = lax.Precision.HIGHEST


def _dot(a, b):
    return jnp.dot(a, b, preferred_element_type=F32)


def _dot_nt(a, b, precision=None):
    return lax.dot_general(a, b, (((1,), (1,)), ((), ())), preferred_element_type=F32,
                           precision=precision)


def _layer_norm(x, g, b):
    mu = jnp.mean(x, axis=-1, keepdims=True)
    xc = x - mu
    var = jnp.mean(xc * xc, axis=-1, keepdims=True)
    return xc * lax.rsqrt(var + LN_EPS) * g + b


def _params(sem, vmem=VMEM_LIMIT):
    return pltpu.CompilerParams(dimension_semantics=sem, vmem_limit_bytes=vmem)


def _split_store(u, up_ref, q_ref, kvc_ref, kvs_ref, kvw_ref, gate_ref):
    o1 = POOL_WIDTH
    o2 = o1 + NSA_WIDTH
    o3 = o2 + KV_WIDTH
    o4 = o3 + KV_WIDTH
    o5 = o4 + KV_WIDTH
    up_ref[...] = u[:, :o1]
    q_ref[...] = u[:, o1:o2]
    kvc_ref[...] = u[:, o2:o3]
    kvs_ref[...] = u[:, o3:o4]
    kvw_ref[...] = u[:, o4:o5]
    gate_ref[...] = 1.0 / (1.0 + jnp.exp(-u[:, o5:]))


def _inproj_prompt_body(x_ref, w_ref, pw_ref, ps_ref,
                        up_ref, q_ref, kvc_ref, kvs_ref, kvw_ref, gate_ref, pool_ref, kvct_ref, kvst_ref,
                        ext_ref, *, tm, tiles_per_seq):
    halo = POOL_STATE + 1
    t_in_seq = pl.program_id(0) % tiles_per_seq
    u = _dot(x_ref[...].astype(BF16), w_ref[...])
    _split_store(u, up_ref, q_ref, kvc_ref, kvs_ref, kvw_ref, gate_ref)
    o2 = POOL_WIDTH + NSA_WIDTH
    kvct_ref[0] = u[:, o2:o2 + KV_WIDTH].T.reshape(2, KVH, DH, tm)
    kvst_ref[0] = u[:, o2 + KV_WIDTH:o2 + 2 * KV_WIDTH].T.reshape(2, KVH, DH, tm)

    @pl.when(t_in_seq == 0)
    def _():
        ext_ref[0:halo, :] = jnp.zeros((halo, POOL_WIDTH), F32)

    ext_ref[halo:halo + tm, :] = u[:, :POOL_WIDTH]
    pos = t_in_seq * tm + lax.broadcasted_iota(I32, (tm, 1), 0)
    for gi, w in enumerate(POOL_WINDOWS):
        cols = slice(gi * POOL_GROUP, (gi + 1) * POOL_GROUP)
        acc = ext_ref[halo:halo + tm, cols]
        for k in range(1, w):
            acc = acc + ext_ref[halo - k:halo - k + tm, cols]
        cnt = jnp.minimum(pos + 1, w).astype(F32)
        d = acc / cnt - ext_ref[halo:halo + tm, cols]
        o = _dot(d.astype(BF16), pw_ref[gi])
        pool_ref[:, cols] = (o * ps_ref[:, cols]).astype(pool_ref.dtype)
    ext_ref[0:halo, :] = ext_ref[tm:tm + halo, :]


def _inproj_prompt(x2d, w_in_bf, pool_w_bf, pool_scale, seq_len):
    T = x2d.shape[0]
    tm = ROW_TILE
    outs = [POOL_WIDTH, NSA_WIDTH, KV_WIDTH, KV_WIDTH, KV_WIDTH, 3 * NSA_HEADS, POOL_WIDTH]
    row = lambda n: pl.BlockSpec((tm, n), lambda i: (i, 0))
    full = lambda a: pl.BlockSpec(a.shape, lambda i: (0,) * a.ndim)
    tps = seq_len // tm
    kvt_spec = pl.BlockSpec((1, 2, KVH, DH, tm), lambda i: (i // tps, 0, 0, 0, i % tps))
    kvt_shape = jax.ShapeDtypeStruct((T // seq_len, 2, KVH, DH, seq_len), F32)
    return pl.pallas_call(
        functools.partial(_inproj_prompt_body, tm=tm, tiles_per_seq=tps),
        grid=(T // tm,),
        in_specs=[row(D_MODEL), full(w_in_bf), full(pool_w_bf), full(pool_scale)],
        out_specs=[row(n) for n in outs] + [kvt_spec] * 2,
        out_shape=[jax.ShapeDtypeStruct((T, n), F32) for n in outs[:-1]]
                  + [jax.ShapeDtypeStruct((T, outs[-1]), BF16)] + [kvt_shape] * 2,
        scratch_shapes=[pltpu.VMEM((tm + POOL_STATE + 1, POOL_WIDTH), F32)],
        compiler_params=_params(("arbitrary",)),
        name="inproj_prompt",
    )(x2d, w_in_bf, pool_w_bf, pool_scale)


def _inproj_sample_body(x_ref, w_ref, pw_ref, ps_ref, st_ref,
                        up_ref, q_ref, kvc_ref, kvs_ref, kvw_ref, gate_ref, pool_ref,
                        ext_ref, *, nb, ls, pos0):
    halo = POOL_STATE + 1
    tm = nb * ls
    u = _dot(x_ref[...].astype(BF16), w_ref[...])
    _split_store(u, up_ref, q_ref, kvc_ref, kvs_ref, kvw_ref, gate_ref)
    ext_ref[:, 0:halo, :] = st_ref[...]
    ext_ref[:, halo:halo + ls, :] = u[:, :POOL_WIDTH].reshape(nb, ls, POOL_WIDTH)
    pos = pos0 + lax.broadcasted_iota(I32, (1, ls, 1), 1)
    for gi, w in enumerate(POOL_WINDOWS):
        cols = slice(gi * POOL_GROUP, (gi + 1) * POOL_GROUP)
        acc = ext_ref[:, halo:halo + ls, cols]
        for k in range(1, w):
            acc = acc + ext_ref[:, halo - k:halo - k + ls, cols]
        cnt = jnp.minimum(pos + 1, w).astype(F32)
        d = acc / cnt - ext_ref[:, halo:halo + ls, cols]
        o = _dot(d.reshape(tm, POOL_GROUP).astype(BF16), pw_ref[gi])
        pool_ref[:, cols] = (o * ps_ref[:, cols]).astype(pool_ref.dtype)


def _inproj_sample(x2d, w_in_bf, pool_w_bf, pool_scale, state_pad, ls, pos0):
    T = x2d.shape[0]
    nb = ROW_TILE // ls
    tm = nb * ls
    outs = [POOL_WIDTH, NSA_WIDTH, KV_WIDTH, KV_WIDTH, KV_WIDTH, 3 * NSA_HEADS, POOL_WIDTH]
    row = lambda n: pl.BlockSpec((tm, n), lambda i: (i, 0))
    full = lambda a: pl.BlockSpec(a.shape, lambda i: (0,) * a.ndim)
    return pl.pallas_call(
        functools.partial(_inproj_sample_body, nb=nb, ls=ls, pos0=pos0),
        grid=(T // tm,),
        in_specs=[row(D_MODEL), full(w_in_bf), full(pool_w_bf), full(pool_scale),
                  pl.BlockSpec((nb, POOL_STATE + 1, POOL_WIDTH), lambda i: (i, 0, 0))],
        out_specs=[row(n) for n in outs],
        out_shape=[jax.ShapeDtypeStruct((T, n), F32) for n in outs],
        scratch_shapes=[pltpu.VMEM((nb, POOL_STATE + 1 + ls, POOL_WIDTH), F32)],
        compiler_params=_params(("arbitrary",)),
        name="inproj_sample",
    )(x2d, w_in_bf, pool_w_bf, pool_scale, state_pad)


def _matmul_body(x_ref, w_ref, o_ref):
    o_ref[...] = _dot(x_ref[...].astype(BF16), w_ref[...].astype(BF16))


def _matmul(x, w, tn=512):
    M, K = x.shape
    N = w.shape[1]
    return pl.pallas_call(
        _matmul_body,
        grid=(N // tn,),
        in_specs=[pl.BlockSpec((M, K), lambda j: (0, 0)), pl.BlockSpec((K, tn), lambda j: (0, j))],
        out_specs=pl.BlockSpec((M, tn), lambda j: (0, j)),
        out_shape=jax.ShapeDtypeStruct((M, N), F32),
        compiler_params=_params(("arbitrary",)),
        name="mem_kv_proj",
    )(x, w)


def _gelu_tanh(x):
    c = 0.7978845608028654
    return 0.5 * x * (1.0 + jnp.tanh(c * (x + 0.044715 * (x * x * x))))


def _compress(kv_refs, n_chunks, w1_ref, pe_ref, b1_ref, w2_ref, b2_ref):
    lo = _lane_iota(n_chunks) < DH
    quads = CMP_STRIDE // 4
    outs = []
    for c in range(2):
        acc_a = jnp.zeros((2 * n_chunks, CMP_HIDDEN), F32)
        acc_b = jnp.zeros((2 * n_chunks, CMP_HIDDEN), F32)
        for i in range(quads):
            x = [kv_refs[c][pl.ds(4 * i + m, n_chunks, stride=CMP_STRIDE), :] for m in range(4)]
            r = [pltpu.roll(v, DH, 1) for v in x]
            x_g0 = jnp.concatenate([jnp.where(lo, x[0], r[1]), jnp.where(lo, x[2], r[3])], axis=1)
            x_g1 = jnp.concatenate([jnp.where(lo, r[0], x[1]), jnp.where(lo, r[2], x[3])], axis=1)
            xq = jnp.concatenate([x_g0, x_g1], axis=0)
            acc_a = acc_a + _dot((xq + pe_ref[c, i:i + 1, :]).astype(BF16), w1_ref[c, i])
            acc_b = acc_b + _dot((xq + pe_ref[c, quads + i:quads + i + 1, :]).astype(BF16), w1_ref[c, quads + i])
        hid = acc_a + pltpu.roll(acc_b, 2 * n_chunks - 1, 0) + b1_ref[c]
        outs.append(_dot(_gelu_tanh(hid).astype(BF16), w2_ref[c]) + b2_ref[c])
    return outs


def _compress_prompt_body(kv_ref, w1_ref, pe_ref, b1_ref, w2_ref, b2_ref, o_ref, k_ref, v_ref, *, n_chunks):
    k_ref[...] = kv_ref[0, :, :LANES]
    v_ref[...] = kv_ref[0, :, LANES:]
    keys, values = _compress((k_ref, v_ref), n_chunks, w1_ref, pe_ref, b1_ref, w2_ref, b2_ref)
    o_ref[0, 0] = keys
    o_ref[0, 1] = values


def _compress_prompt(kvc, cw):
    B, S, _ = kvc.shape
    n_chunks = S // CMP_STRIDE
    full = lambda a: pl.BlockSpec(a.shape, lambda b: (0,) * a.ndim)
    return pl.pallas_call(
        functools.partial(_compress_prompt_body, n_chunks=n_chunks),
        grid=(B,),
        in_specs=[pl.BlockSpec((1, S, KV_WIDTH), lambda b: (b, 0, 0))] + [full(a) for a in cw],
        out_specs=pl.BlockSpec((1, 2, KVH * n_chunks, LANES), lambda b: (b, 0, 0, 0)),
        out_shape=jax.ShapeDtypeStruct((B, 2, KVH * n_chunks, LANES), F32),
        scratch_shapes=[pltpu.VMEM((S, LANES), F32)] * 2,
        compiler_params=_params(("arbitrary",)),
        name="compress_prompt",
    )(kvc, *cw)


def _compress_weights(cmp_pe, cmp_w1, cmp_b1, cmp_w2, cmp_b2):
    nq = CMP_LEN // 4
    w1 = cmp_w1.reshape(2, nq, 4 * DH, CMP_HIDDEN).astype(BF16)
    pe = cmp_pe.reshape(2, nq, 4 * DH)
    b1 = cmp_b1[:, None, :]
    w2 = jnp.stack([jnp.concatenate([cmp_w2[0], jnp.zeros_like(cmp_w2[0])], axis=1),
                    jnp.concatenate([cmp_w2[1], cmp_w2[1]], axis=1)]).astype(BF16)
    b2 = jnp.stack([jnp.concatenate([cmp_b2[0], jnp.zeros_like(cmp_b2[0])]),
                    jnp.concatenate([cmp_b2[1], cmp_b2[1]])])[:, None, :]
    return w1, pe, b1, w2, b2


def _softmax_rows(s):
    e = jnp.exp(s - jnp.max(s, axis=-1, keepdims=True))
    return e * (1.0 / jnp.sum(e, axis=-1, keepdims=True))


def _lane_iota(n):
    return lax.broadcasted_iota(I32, (n, LANES), 1)


def _key_alibi_cols(pos, lane):
    hi = jnp.left_shift(jnp.right_shift(pos, 6), 6).astype(F32)
    lo = jnp.bitwise_and(pos, SEL_LEN - 1).astype(F32)
    return jnp.where(lane == DH, hi,
                     jnp.where(lane == DH + 1, lo,
                               jnp.where((lane == DH + 2) | (lane == DH + 3), 1.0, 0.0)))


def _key_alibi_rows(pos, sub):
    hi = jnp.left_shift(jnp.right_shift(pos, 6), 6).astype(F32)
    lo = jnp.bitwise_and(pos, SEL_LEN - 1).astype(F32)
    return jnp.where(sub == 0, hi, jnp.where(sub == 1, lo, jnp.where((sub == 2) | (sub == 3), 1.0, 0.0)))


def _halves(x, zero_hi=False):
    lo = _lane_iota(x.shape[0]) < DH
    r = pltpu.roll(x, DH, 1)
    if zero_hi:
        return jnp.where(lo, x, 0.0), jnp.where(lo, r, 0.0)
    return jnp.where(lo, x, r), jnp.where(lo, r, x)


def _fill_queries(qa_ref, qs, q_pos, lq):
    lane = _lane_iota(lq)
    lo_half = lane < DH
    q_hi = jnp.left_shift(jnp.right_shift(q_pos, 7), 7).astype(F32)
    q_lo = jnp.bitwise_and(q_pos, LANES - 1).astype(F32)
    for h in range(NSA_HEADS):
        g, hl = divmod(h, HPG)
        slope = 2.0 ** (-(h + 1))
        slab = qs[:, (h // 2) * LANES:(h // 2 + 1) * LANES]
        if h % 2:
            slab = pltpu.roll(slab, DH, 1)
        ex = jnp.where((lane == DH) | (lane == DH + 1), slope,
                       jnp.where(lane == DH + 2, -slope * q_hi,
                                 jnp.where(lane == DH + 3, -slope * q_lo,
                                           jnp.where(lane == DH + 4, NEG, 0.0))))
        qa_ref[g, hl * lq:(hl + 1) * lq, 0:LANES] = jnp.where(lo_half, slab, ex).astype(BF16)


def _stack4(x):
    return jnp.concatenate([x] * HPG, axis=0)


def _cmp_branch(qa_ref, kc_k, kc_v, q_pos4, lq, n_cmp):
    c_end = lax.broadcasted_iota(I32, (1, n_cmp), 1) * CMP_STRIDE + (CMP_LEN - 1)
    m_c = c_end <= q_pos4
    any_c = (q_pos4 >= CMP_LEN - 1).astype(F32)
    outs, psums = [], []
    for g in range(KVH):
        s = jnp.where(m_c, _dot_nt(qa_ref[g, :, 0:LANES], kc_k[g][...]), NEG)
        p = _softmax_rows(s) * any_c
        outs.append(_dot(p.astype(BF16), kc_v[g][...]))
        psums.append(p[0:lq] + p[lq:2 * lq] + p[2 * lq:3 * lq] + p[3 * lq:4 * lq])
    return outs, psums


def _split3(x):
    hi = x.astype(BF16)
    r1 = x - hi.astype(F32)
    mid = r1.astype(BF16)
    lo = (r1 - mid.astype(F32)).astype(BF16)
    return hi, mid, lo


def _top_blocks_t(imp_ts, pos0, n_blk=SEL_PAD):
    blk = lax.broadcasted_iota(I32, (n_blk, LANES), 0)
    qp_t = pos0 + lax.broadcasted_iota(I32, (n_blk, LANES), 1)
    cur = jnp.right_shift(qp_t, 6)
    forced = (blk == 0) | (blk == cur) | (blk == cur - 1)
    valid = jnp.left_shift(blk, 6) <= qp_t
    v = jnp.concatenate([jnp.where(valid, jnp.where(forced, FORCED_SCORE, t), -1.0) for t in imp_ts], axis=1)
    blk_f = lax.broadcasted_iota(I32, (n_blk, KVH * LANES), 0).astype(F32)
    sel = jnp.zeros((n_blk, KVH * LANES), F32)
    for _ in range(SEL_TOP):
        m = jnp.max(v, axis=0, keepdims=True)
        idx = jnp.min(jnp.where(v == m, blk_f, float(n_blk)), axis=0, keepdims=True)
        hit = blk_f == idx
        sel = jnp.where(hit, 1.0, sel)
        v = jnp.where(hit, -jnp.inf, v)
    out = [jnp.where((sel[:, g * LANES:(g + 1) * LANES] > 0.5) & valid, 0.0, NEG) for g in range(KVH)]
    if n_blk < SEL_PAD:
        out = [jnp.concatenate([b, jnp.full((SEL_PAD - n_blk, LANES), NEG, F32)], axis=0) for b in out]
    return out


def _select_blocks(psums, ovt_ref, pos0, lq, n_blk):
    imp_ts = []
    for g in range(KVH):
        ps = psums[g]
        if lq < LANES:
            ps = jnp.concatenate([ps, jnp.zeros((LANES - lq, ps.shape[1]), F32)], axis=0)
        imp_ts.append(_dot_nt(ovt_ref[0:n_blk, :], ps, precision=HIGHEST))
    return [b.T[:lq].astype(BF16) for b in _top_blocks_t(imp_ts, pos0, n_blk)]


def _store_selbias(qa_ref, selbias, lq):
    for g in range(KVH):
        for hl in range(HPG):
            qa_ref[g, hl * lq:(hl + 1) * lq, LANES:2 * LANES] = selbias[g]


def _combine(gates, eg_ref, o_c, o_s, o_w, lq):
    lo_half = _lane_iota(lq) < DH

    def assemble(per_group):
        slabs = []
        for k in range(NSA_HEADS // 2):
            g, hl = divmod(2 * k, HPG)
            a = per_group[g][hl * lq:(hl + 1) * lq]
            b = per_group[g][(hl + 1) * lq:(hl + 2) * lq]
            slabs.append(jnp.where(lo_half, a, b))
        return jnp.concatenate(slabs, axis=1)

    return (_dot(gates, eg_ref[0]) * assemble(o_c)
            + _dot(gates, eg_ref[1]) * assemble(o_s)
            + _dot(gates, eg_ref[2]) * assemble(o_w))


def _gate_expand():
    r = jnp.arange(3 * NSA_HEADS)
    c = jnp.arange(NSA_WIDTH)
    return jnp.stack([(r[:, None] == 3 * (c[None, :] // DH) + br).astype(F32) for br in range(3)])


def _overlap_t(n_cmp):
    n = jnp.arange(n_cmp)
    s = jnp.arange(SEL_PAD)
    c_first = n * CMP_STRIDE
    c_end = c_first + CMP_LEN - 1
    b_first = s * SEL_LEN
    return ((c_first[None, :] < b_first[:, None] + SEL_LEN) & (c_end[None, :] >= b_first[:, None])).astype(F32)


def _key_rows(kv_f32, pos, invalid=None):
    n = kv_f32.shape[0]
    lane = _lane_iota(n)
    ex = _key_alibi_cols(pos, lane)
    if invalid is not None:
        ex = jnp.where((lane == DH + 4) & invalid, 1.0, ex)
    lo = lane < DH
    return jnp.where(lo, kv_f32, ex), jnp.where(lo, pltpu.roll(kv_f32, DH, 1), ex)


def _block_onehot(pos, n):
    return (lax.broadcasted_iota(I32, (n, SEL_PAD), 1) == jnp.right_shift(pos, 6)).astype(BF16)


def _tile4(x):
    return jnp.concatenate([x] * HPG, axis=1)


def _nsa_prompt_body(q_ref, gate_ref, kc_ref, kvs_ref, w0_ref, w1_ref, w2_ref, w3_ref, w4_ref,
                     ovt_ref, egt_ref, wband_ref, o_ref,
                     kck0, kck1, kcv0, kcv1, ka0, ka1, vt0, vt1, qa0, qa1,
                     sa_ref, sb_ref, m_ref, l_ref, acc_ref, oc_ref, *, seq_len):
    j = pl.program_id(1)
    kc_k, kc_vt, kaug, v_t, qa_t = (kck0, kck1), (kcv0, kcv1), (ka0, ka1), (vt0, vt1), (qa0, qa1)
    n_cmp = kc_ref.shape[2] // KVH
    lq = Q_BLOCK
    cols = HPG * lq

    @pl.when(j == 0)
    def _():
        for g in range(KVH):
            kc_k[g][...] = kc_ref[0, 0, g * n_cmp:(g + 1) * n_cmp, :].astype(BF16)
            kc_vt[g][...] = kc_ref[0, 1, g * n_cmp:(g + 1) * n_cmp, :].T[0:DH].astype(BF16)

        def build(i, _):
            r0 = pl.multiple_of(i * KEY_TILE, KEY_TILE)
            pos = r0 + lax.broadcasted_iota(I32, (KEY_TILE, 1), 0)
            k0, k1 = _key_rows(kvs_ref[0, pl.ds(r0, KEY_TILE), :LANES], pos)
            onehot = _block_onehot(pos, KEY_TILE)
            ka0[pl.ds(r0, KEY_TILE), :] = jnp.concatenate([k0.astype(BF16), onehot], axis=1)
            ka1[pl.ds(r0, KEY_TILE), :] = jnp.concatenate([k1.astype(BF16), onehot], axis=1)
            vt = kvs_ref[0, pl.ds(r0, KEY_TILE), LANES:].T.astype(BF16)
            vt0[:, pl.ds(r0, KEY_TILE)] = vt[0:DH]
            vt1[:, pl.ds(r0, KEY_TILE)] = vt[DH:]
            return 0

        lax.fori_loop(0, seq_len // KEY_TILE, build, 0)

    st = j * Q_BLOCK
    q_pos = st + lax.broadcasted_iota(I32, (1, lq), 1)
    q_pos4 = _tile4(q_pos)

    q_t = (q_ref[...] * ATTN_SCALE).T
    sub = lax.broadcasted_iota(I32, (DH, lq), 0)
    q_hi = jnp.left_shift(jnp.right_shift(q_pos, 7), 7).astype(F32)
    q_lo = jnp.bitwise_and(q_pos, LANES - 1).astype(F32)
    for h in range(NSA_HEADS):
        g, hl = divmod(h, HPG)
        slope = 2.0 ** (-(h + 1))
        ex = jnp.where(sub <= 1, slope,
                       jnp.where(sub == 2, -slope * q_hi,
                                 jnp.where(sub == 3, -slope * q_lo, jnp.where(sub == 4, NEG, 0.0))))
        qa_t[g][0:DH, hl * lq:(hl + 1) * lq] = q_t[h * DH:(h + 1) * DH].astype(BF16)
        qa_t[g][DH:2 * DH, hl * lq:(hl + 1) * lq] = ex.astype(BF16)

    any_c = (q_pos4 >= CMP_LEN - 1).astype(F32)
    blocks_per_sel = SEL_LEN // CMP_STRIDE
    n_variants = -(-n_cmp // LANES)
    last_visible = (st + Q_BLOCK - CMP_LEN) // CMP_STRIDE

    def cmp_and_select(n_c):
        n_b = n_c // blocks_per_sel
        c_end = lax.broadcasted_iota(I32, (n_c, 1), 0) * CMP_STRIDE + (CMP_LEN - 1)
        m_c = c_end <= q_pos4
        ovt_bf = ovt_ref[0:n_b, 0:n_c].astype(BF16)
        imp_ts = []
        for g in range(KVH):
            s = jnp.where(m_c, _dot(kc_k[g][0:n_c, :], qa_t[g][0:LANES, :]), NEG)
            e = jnp.exp(s - jnp.max(s, axis=0, keepdims=True))
            p = e * (any_c / jnp.sum(e, axis=0, keepdims=True))
            oc_ref[g] = _dot(kc_vt[g][:, 0:n_c], p.astype(BF16))
            psum = p[:, 0:lq] + p[:, lq:2 * lq] + p[:, 2 * lq:3 * lq] + p[:, 3 * lq:4 * lq]
            imp_ts.append(sum(_dot(ovt_bf, t) for t in _split3(psum)))
        bias = _top_blocks_t(imp_ts, st, n_b)
        for g in range(KVH):
            for hl in range(HPG):
                qa_t[g][2 * DH:, hl * lq:(hl + 1) * lq] = bias[g].astype(BF16)

    variant = jnp.minimum(jnp.maximum(last_visible, 0) // LANES, n_variants - 1)
    for k in range(n_variants):
        pl.when(variant == k)(functools.partial(cmp_and_select, min(n_cmp, (k + 1) * LANES)))
    o_c = [oc_ref[g] for g in range(KVH)]

    n_tiles = (st + Q_BLOCK + KEY_TILE - 1) // KEY_TILE
    for g in range(KVH):
        m_ref[g] = jnp.full((1, cols), NEG, F32)
        l_ref[g] = jnp.zeros((1, cols), F32)
        acc_ref[g] = jnp.zeros((DH, cols), F32)

    def scores(t, s_ref):
        r0 = pl.multiple_of(t * KEY_TILE, KEY_TILE)
        for g in range(KVH):
            s_ref[g] = _dot(kaug[g][pl.ds(r0, KEY_TILE), :], qa_t[g][...])

    def consume(t, s_ref, masked):
        r0 = pl.multiple_of(t * KEY_TILE, KEY_TILE)
        for g in range(KVH):
            s = s_ref[g]
            if masked:
                k_pos = r0 + lax.broadcasted_iota(I32, (KEY_TILE, 1), 0)
                s = jnp.where(k_pos <= q_pos4, s, NEG)
            m = m_ref[g]
            m_new = jnp.maximum(m, jnp.max(s, axis=0, keepdims=True))
            a = jnp.exp(m - m_new)
            e = jnp.exp(s - m_new)
            m_ref[g] = m_new
            l_ref[g] = a * l_ref[g] + jnp.sum(e, axis=0, keepdims=True)
            acc_ref[g] = a * acc_ref[g] + _dot(v_t[g][:, pl.ds(r0, KEY_TILE)], e.astype(BF16))

    scores(0, sa_ref)
    n_pairs = (n_tiles - 1) // 2

    def pair(u, _):
        scores(2 * u + 1, sb_ref)
        consume(2 * u, sa_ref, False)
        scores(2 * u + 2, sa_ref)
        consume(2 * u + 1, sb_ref, False)
        return 0

    lax.fori_loop(0, n_pairs, pair, 0)
    odd_tail = (n_tiles - 1) - 2 * n_pairs == 1

    @pl.when(odd_tail)
    def _():
        scores(n_tiles - 1, sb_ref)
        consume(n_tiles - 2, sa_ref, False)
        consume(n_tiles - 1, sb_ref, True)

    @pl.when(jnp.logical_not(odd_tail))
    def _():
        consume(n_tiles - 1, sa_ref, True)

    o_s = [acc_ref[g] * (1.0 / l_ref[g]) for g in range(KVH)]

    band = jnp.concatenate([w0_ref[0], w1_ref[0], w2_ref[0], w3_ref[0], w4_ref[0]], axis=0)
    n_win = band.shape[0]
    w_pos_col = st - WINDOW + lax.broadcasted_iota(I32, (n_win, 1), 0)
    kw_k = _key_rows(band[:, :LANES], jnp.maximum(w_pos_col, 0), invalid=w_pos_col < 0)
    vw_t = band[:, LANES:].T.astype(BF16)
    band_bias = _tile4(wband_ref[...])
    o_w = []
    for g in range(KVH):
        s = _dot(kw_k[g].astype(BF16), qa_t[g][0:LANES, :]) + band_bias
        e = jnp.exp(s - jnp.max(s, axis=0, keepdims=True))
        o_w.append(_dot(vw_t[g * DH:(g + 1) * DH], e.astype(BF16)) * (1.0 / jnp.sum(e, axis=0, keepdims=True)))

    def heads(per_group):
        return jnp.concatenate([per_group[h // HPG][:, (h % HPG) * lq:(h % HPG + 1) * lq]
                                for h in range(NSA_HEADS)], axis=0)

    gates = gate_ref[...]
    out_t = (_dot_nt(egt_ref[0], gates) * heads(o_c)
             + _dot_nt(egt_ref[1], gates) * heads(o_s)
             + _dot_nt(egt_ref[2], gates) * heads(o_w))
    o_ref[...] = out_t.T.astype(o_ref.dtype)


def _gate_expand_t():
    r = jnp.arange(3 * NSA_HEADS)
    c = jnp.arange(NSA_WIDTH)
    return jnp.stack([(3 * (c[:, None] // DH) + br == r[None, :]).astype(F32) for br in range(3)])


def _nsa_prompt(q, gates, kc, kvs, kvw):
    B, S, _ = kvs.shape
    nqb = S // Q_BLOCK
    n_cmp = kc.shape[2] // KVH
    ovt = _overlap_t(n_cmp)
    egt = _gate_expand_t()
    n_band = WINDOW // Q_BLOCK + 1
    d_band = jnp.arange(Q_BLOCK)[None, :] + WINDOW - jnp.arange(n_band * Q_BLOCK)[:, None]
    wband = jnp.where((d_band >= 0) & (d_band < WINDOW), 0.0, NEG).astype(F32)

    def band_spec(i):
        return pl.BlockSpec((1, Q_BLOCK, KV_WIDTH),
                            lambda b, j, i=i: (b, jnp.maximum(j - (n_band - 1) + i, 0), 0))

    full = lambda a: pl.BlockSpec(a.shape, lambda b, j: (0,) * a.ndim)
    return pl.pallas_call(
        functools.partial(_nsa_prompt_body, seq_len=S),
        grid=(B, nqb),
        in_specs=[pl.BlockSpec((Q_BLOCK, NSA_WIDTH), lambda b, j: (b * nqb + j, 0)),
                  pl.BlockSpec((Q_BLOCK, 3 * NSA_HEADS), lambda b, j: (b * nqb + j, 0)),
                  pl.BlockSpec((1,) + kc.shape[1:], lambda b, j: (b, 0, 0, 0)),
                  pl.BlockSpec((1, S, KV_WIDTH), lambda b, j: (b, 0, 0))]
                 + [band_spec(i) for i in range(n_band)] + [full(ovt), full(egt), full(wband)],
        out_specs=pl.BlockSpec((Q_BLOCK, NSA_WIDTH), lambda b, j: (b * nqb + j, 0)),
        out_shape=jax.ShapeDtypeStruct((B * S, NSA_WIDTH), BF16),
        scratch_shapes=[pltpu.VMEM((n_cmp, LANES), BF16)] * 2
                       + [pltpu.VMEM((DH, n_cmp), BF16)] * 2
                       + [pltpu.VMEM((S, 2 * LANES), BF16)] * 2
                       + [pltpu.VMEM((DH, S), BF16)] * 2
                       + [pltpu.VMEM((2 * LANES, HPG * Q_BLOCK), BF16)] * 2
                       + [pltpu.VMEM((KVH, KEY_TILE, HPG * Q_BLOCK), F32)] * 2
                       + [pltpu.VMEM((KVH, 1, HPG * Q_BLOCK), F32)] * 2
                       + [pltpu.VMEM((KVH, DH, HPG * Q_BLOCK), F32)] * 2,
        compiler_params=_params(("arbitrary", "arbitrary")),
        name="nsa_prompt",
    )(q, gates, kc, kvs, *([kvw] * n_band), ovt, egt, wband)


def _nsa_sample_body(pt_ref, q_ref, gate_ref, kvs_new_ref, win_ref, kvw_new_ref, *rest,
                     n_pages, ls, past_len):
    cmp_pages = rest[:n_pages]
    slc_pages = rest[n_pages:2 * n_pages]
    (w1_ref, pe_ref, b1_ref, w2_ref, b2_ref, ovt_ref, eg_ref, o_ref, wout_ref,
     full_k, full_v, kck0, kck1, kcv0, kcv1, kt0, kt1, vt0, vt1, wkt0, wkt1, qa_ref) = rest[2 * n_pages:]
    del pt_ref
    kc_k, kc_v, kaug_t, v_t, wk_t = (kck0, kck1), (kcv0, kcv1), (kt0, kt1), (vt0, vt1), (wkt0, wkt1)
    n_cmp = past_len // CMP_STRIDE
    w_rows = win_ref.shape[4]
    lq = BF16_ROWS
    rows = HPG * lq
    w_start = past_len - w_rows

    @pl.when(pl.program_id(0) == 0)
    def _():
        sub = lax.broadcasted_iota(I32, (DH, past_len), 0)
        pos = lax.broadcasted_iota(I32, (1, past_len), 1)
        ex = _key_alibi_rows(pos, sub).astype(BF16)
        onehot = (lax.broadcasted_iota(I32, (SEL_PAD, past_len), 0) == jnp.right_shift(pos, 6)).astype(BF16)
        subw = lax.broadcasted_iota(I32, (DH, w_rows), 0)
        exw = _key_alibi_rows(w_start + lax.broadcasted_iota(I32, (1, w_rows), 1), subw).astype(BF16)
        for g in range(KVH):
            kaug_t[g][DH:2 * DH, :] = ex
            kaug_t[g][2 * DH:, :] = onehot
            wk_t[g][DH:, :] = exw

    for p in range(n_pages):
        cols = slice(p * PAGE_SIZE, (p + 1) * PAGE_SIZE)
        full_k[cols, :] = cmp_pages[p][0, 0].reshape(2 * DH, PAGE_SIZE).T
        full_v[cols, :] = cmp_pages[p][0, 1].reshape(2 * DH, PAGE_SIZE).T
        for g in range(KVH):
            kaug_t[g][0:DH, cols] = slc_pages[p][0, 0, g].astype(BF16)
            vt = slc_pages[p][0, 1, g].astype(BF16)
            v_t[g][0:DH, cols] = vt
            v_t[g][DH:, cols] = vt
    for g in range(KVH):
        wk_t[g][0:DH, :] = win_ref[0, 0, g].astype(BF16)

    keys_c, values_c = _compress((full_k, full_v), n_cmp, w1_ref, pe_ref, b1_ref, w2_ref, b2_ref)
    for g in range(KVH):
        kc_k[g][...] = keys_c[g * n_cmp:(g + 1) * n_cmp].astype(BF16)
        kc_v[g][...] = values_c[g * n_cmp:(g + 1) * n_cmp].astype(BF16)

    pad_q = jnp.zeros((lq - ls, NSA_WIDTH), F32)
    q_pos = past_len + lax.broadcasted_iota(I32, (lq, 1), 0)
    q_pos4 = _stack4(q_pos)
    _fill_queries(qa_ref, jnp.concatenate([q_ref[0] * ATTN_SCALE, pad_q], axis=0), q_pos, lq)
    gates = jnp.concatenate([gate_ref[0], jnp.zeros((lq - ls, 3 * NSA_HEADS), F32)], axis=0)

    o_c, psums = _cmp_branch(qa_ref, kc_k, kc_v, q_pos4, lq, n_cmp)
    n_sel = -(-(past_len + lq) // SEL_LEN)
    n_blk = -(-n_sel // 8) * 8
    _store_selbias(qa_ref, _select_blocks(psums, ovt_ref, past_len, lq, n_blk), lq)

    pad_k = jnp.zeros((LANES - ls, KV_WIDTH), F32)
    new_pos_col = past_len + lax.broadcasted_iota(I32, (LANES, 1), 0)
    new_pos = past_len + lax.broadcasted_iota(I32, (1, LANES), 1)
    new_s = jnp.concatenate([kvs_new_ref[0], pad_k], axis=0)
    new_w = jnp.concatenate([kvw_new_ref[0], pad_k], axis=0)
    ks_new = _key_rows(new_s[:, :LANES], new_pos_col)
    vs_new = _halves(new_s[:, LANES:])
    kw_new = _key_rows(new_w[:, :LANES], new_pos_col)
    vw_new = _halves(new_w[:, LANES:])
    onehot_new = _block_onehot(new_pos_col, LANES)
    causal_new = q_pos4 >= new_pos

    o_s, o_w = [], []
    d_past = q_pos4 - (w_start + lax.broadcasted_iota(I32, (1, w_rows), 1))
    m_past = (d_past >= 0) & (d_past < WINDOW)
    d_new = q_pos4 - new_pos
    m_new = (d_new >= 0) & (d_new < WINDOW)
    for g in range(KVH):
        s_past = _dot(qa_ref[g], kaug_t[g][...])
        k_new = jnp.concatenate([ks_new[g].astype(BF16), onehot_new], axis=1)
        s_new = jnp.where(causal_new, _dot_nt(qa_ref[g], k_new), NEG)
        m = jnp.maximum(jnp.max(s_past, axis=-1, keepdims=True), jnp.max(s_new, axis=-1, keepdims=True))
        e_past = jnp.exp(s_past - m)
        e_new = jnp.exp(s_new - m)
        den = jnp.sum(e_past, axis=-1, keepdims=True) + jnp.sum(e_new, axis=-1, keepdims=True)
        acc = _dot_nt(e_past.astype(BF16), v_t[g][...]) + _dot(e_new.astype(BF16), vs_new[g].astype(BF16))
        o_s.append(acc / den)

        sw_past = jnp.where(m_past, _dot(qa_ref[g, :, 0:LANES], wk_t[g][...]), NEG)
        sw_new = jnp.where(m_new, _dot_nt(qa_ref[g, :, 0:LANES], kw_new[g].astype(BF16)), NEG)
        m = jnp.maximum(jnp.max(sw_past, axis=-1, keepdims=True), jnp.max(sw_new, axis=-1, keepdims=True))
        e_past = jnp.exp(sw_past - m)
        e_new = jnp.exp(sw_new - m)
        den = jnp.sum(e_past, axis=-1, keepdims=True) + jnp.sum(e_new, axis=-1, keepdims=True)
        vw = win_ref[0, 1, g].astype(BF16)
        vw2 = jnp.concatenate([vw, vw], axis=0)
        acc = _dot_nt(e_past.astype(BF16), vw2) + _dot(e_new.astype(BF16), vw_new[g].astype(BF16))
        o_w.append(acc / den)

    o_ref[0] = _combine(gates, eg_ref, o_c, o_s, o_w, lq)[:ls]

    keep = w_rows - ls
    tail_lane = lax.broadcasted_iota(I32, (DH, LANES), 1) >= keep % LANES
    for c in range(2):
        new_t = pltpu.roll(new_w[:, c * LANES:(c + 1) * LANES].T, keep % LANES, 1)
        for g in range(KVH):
            shifted = pltpu.roll(win_ref[0, c, g], keep, 1)
            last = jnp.where(tail_lane, new_t[g * DH:(g + 1) * DH], shifted[:, w_rows - LANES:])
            wout_ref[0, c, g] = jnp.concatenate([shifted[:, :w_rows - LANES], last], axis=1)


def _nsa_sample(q, gates, kvs_new, kvw_new, win_t, cmp_t, slc_t, page_table, cw, past_len):
    Bd, ls, _ = q.shape
    n_pages = page_table.shape[1]
    n_cmp = past_len // CMP_STRIDE
    ovt = _overlap_t(n_cmp)
    eg = _gate_expand()
    w_rows = win_t.shape[4]

    per_b = lambda a: pl.BlockSpec((1,) + a.shape[1:], lambda b, pt: (b,) + (0,) * (a.ndim - 1))
    full = lambda a: pl.BlockSpec(a.shape, lambda b, pt: (0,) * a.ndim)
    page = lambda p: pl.BlockSpec((1, 2, KVH, DH, PAGE_SIZE), lambda b, pt, p=p: (pt[b, p], 0, 0, 0, 0))
    grid_spec = pltpu.PrefetchScalarGridSpec(
        num_scalar_prefetch=1,
        grid=(Bd,),
        in_specs=[per_b(q), per_b(gates), per_b(kvs_new), per_b(win_t), per_b(kvw_new)]
                 + [page(p) for p in range(n_pages)] * 2
                 + [full(a) for a in cw] + [full(ovt), full(eg)],
        out_specs=[pl.BlockSpec((1, ls, NSA_WIDTH), lambda b, pt: (b, 0, 0)), per_b(win_t)],
        scratch_shapes=[pltpu.VMEM((past_len, LANES), F32)] * 2
                       + [pltpu.VMEM((n_cmp, LANES), BF16)] * 4
                       + [pltpu.VMEM((2 * LANES, past_len), BF16)] * 2
                       + [pltpu.VMEM((LANES, past_len), BF16)] * 2
                       + [pltpu.VMEM((LANES, w_rows), BF16)] * 2
                       + [pltpu.VMEM((KVH, HPG * BF16_ROWS, 2 * LANES), BF16)],
    )
    return pl.pallas_call(
        functools.partial(_nsa_sample_body, n_pages=n_pages, ls=ls, past_len=past_len),
        grid_spec=grid_spec,
        out_shape=[jax.ShapeDtypeStruct((Bd, ls, NSA_WIDTH), F32), jax.ShapeDtypeStruct(win_t.shape, F32)],
        compiler_params=_params(("arbitrary",)),
        name="nsa_sample",
    )(page_table, q, gates, kvs_new, win_t, kvw_new,
      *([cmp_t] * n_pages), *([slc_t] * n_pages), *cw, ovt, eg)


def _fin1_body(h_ref, pool_ref, nsa_ref, wo_ref, g_ref, b_ref, wq_ref, h1_ref, qm_ref):
    mix = (_dot(pool_ref[...].astype(BF16), wo_ref[0:POOL_WIDTH, :])
           + _dot(nsa_ref[...].astype(BF16), wo_ref[POOL_WIDTH:, :]))
    h1 = _layer_norm(DN_ALPHA * h_ref[...] + mix, g_ref[...], b_ref[...])
    h1_ref[...] = h1
    qm_ref[...] = (_dot(h1.astype(BF16), wq_ref[...]) * (MEM_HEAD_DIM ** -0.5)).astype(qm_ref.dtype)


def _fin1(h, pool_o, nsa_o, w_out_bf, g, b, wq_bf, q_dtype):
    T = h.shape[0]
    tm = ROW_TILE
    row = lambda n: pl.BlockSpec((tm, n), lambda i: (i, 0))
    full = lambda a: pl.BlockSpec(a.shape, lambda i: (0,) * a.ndim)
    return pl.pallas_call(
        _fin1_body,
        grid=(T // tm,),
        in_specs=[row(D_MODEL), row(POOL_WIDTH), row(NSA_WIDTH), full(w_out_bf), full(g), full(b), full(wq_bf)],
        out_specs=[row(D_MODEL), row(D_MODEL)],
        out_shape=[jax.ShapeDtypeStruct((T, D_MODEL), F32), jax.ShapeDtypeStruct((T, D_MODEL), q_dtype)],
        compiler_params=_params(("arbitrary",)),
        name="out_proj_ln1",
    )(h, pool_o, nsa_o, w_out_bf, g, b, wq_bf)


def _memattn_body(q_ref, kv_ref, o_ref):
    width = MEM_HEADS * MEM_HEAD_DIM
    for h in range(MEM_HEADS):
        cols = slice(h * MEM_HEAD_DIM, (h + 1) * MEM_HEAD_DIM)
        qh = q_ref[0, :, cols].astype(BF16)
        kh = kv_ref[0, :, cols].astype(BF16)
        vh = kv_ref[0, :, width + h * MEM_HEAD_DIM:width + (h + 1) * MEM_HEAD_DIM].astype(BF16)
        s = _dot_nt(qh, kh)
        e = jnp.exp(s - jnp.max(s, axis=-1, keepdims=True))
        o = _dot(e.astype(BF16), vh) * (1.0 / jnp.sum(e, axis=-1, keepdims=True))
        o_ref[0, :, cols] = o.astype(o_ref.dtype)


def _memattn(qm, mem_kv, tq):
    nb, L, W = qm.shape
    return pl.pallas_call(
        _memattn_body,
        grid=(nb, L // tq),
        in_specs=[pl.BlockSpec((1, tq, W), lambda b, t: (b, t, 0)),
                  pl.BlockSpec((1, MEM_LEN, 2 * W), lambda b, t: (b, 0, 0))],
        out_specs=pl.BlockSpec((1, tq, W), lambda b, t: (b, t, 0)),
        out_shape=jax.ShapeDtypeStruct((nb, L, W), BF16),
        compiler_params=_params(("arbitrary", "arbitrary")),
        name="mem_attn",
    )(qm, mem_kv)


def _memattn_few_body(q_ref, kv_ref, o_ref):
    lq = q_ref.shape[1]
    n_keys = MEM_LEN * MEM_HEADS
    q = q_ref[0]
    qs = jnp.concatenate([q[:, h * MEM_HEAD_DIM:(h + 1) * MEM_HEAD_DIM] for h in range(MEM_HEADS)], axis=0)
    k = kv_ref[0, :, 0, :, :].reshape(n_keys, MEM_HEAD_DIM).astype(BF16)
    v = kv_ref[0, :, 1, :, :].reshape(n_keys, MEM_HEAD_DIM).astype(BF16)
    s = _dot_nt(qs.astype(BF16), k)
    assert lq & (lq - 1) == 0 and MEM_HEADS & (MEM_HEADS - 1) == 0
    col_h = jnp.bitwise_and(lax.broadcasted_iota(I32, s.shape, 1), MEM_HEADS - 1)
    row_h = jnp.right_shift(lax.broadcasted_iota(I32, s.shape, 0), lq.bit_length() - 1)
    s = jnp.where(col_h == row_h, s, NEG)
    e = jnp.exp(s - jnp.max(s, axis=-1, keepdims=True))
    o = _dot(e.astype(BF16), v) * (1.0 / jnp.sum(e, axis=-1, keepdims=True))
    for h in range(MEM_HEADS):
        o_ref[0, :, h * MEM_HEAD_DIM:(h + 1) * MEM_HEAD_DIM] = o[h * lq:(h + 1) * lq]


def _memattn_few(qm, mem_kv):
    nb, lq, W = qm.shape
    return pl.pallas_call(
        _memattn_few_body,
        grid=(nb,),
        in_specs=[pl.BlockSpec((1, lq, W), lambda b: (b, 0, 0)),
                  pl.BlockSpec((1, MEM_LEN, 2, MEM_HEADS, MEM_HEAD_DIM), lambda b: (b, 0, 0, 0, 0))],
        out_specs=pl.BlockSpec((1, lq, W), lambda b: (b, 0, 0)),
        out_shape=jax.ShapeDtypeStruct((nb, lq, W), F32),
        compiler_params=_params(("arbitrary",)),
        name="mem_attn_few",
    )(qm, mem_kv)


def _fin2_body(cnt0_ref, h1_ref, o_ref, wo_ref, g_ref, b_ref, rw_ref, rb_ref, *rest, n_own):
    h2_ref, te_ref, tg_ref, cnt_ref, run_ref = rest[-5:]
    step = pl.program_id(0)

    @pl.when(step >= n_own)
    def _():
        h2_ref[...] = jnp.zeros(h2_ref.shape, F32)
        te_ref[...] = jnp.zeros(te_ref.shape, I32)
        tg_ref[...] = jnp.zeros(tg_ref.shape, F32)

    @pl.when(step < n_own)
    def _():
        _fin2_rows(cnt0_ref, h1_ref, o_ref, wo_ref, g_ref, b_ref, rw_ref, rb_ref,
                   h2_ref, te_ref, tg_ref, cnt_ref, run_ref)


def _fin2_rows(cnt0_ref, h1_ref, o_ref, wo_ref, g_ref, b_ref, rw_ref, rb_ref,
               h2_ref, te_ref, tg_ref, cnt_ref, run_ref):
    tm = h1_ref.shape[0]

    @pl.when(pl.program_id(0) == 0)
    def _():
        run_ref[...] = cnt0_ref[...]

    a = _dot(o_ref[...].astype(BF16), wo_ref[...])
    h2 = _layer_norm(DN_ALPHA * h1_ref[...] + a, g_ref[...], b_ref[...])
    h2_ref[...] = h2
    a_hi, a_lo, _ = _split3(h2)
    w_hi, w_lo, _ = _split3(rw_ref[...])
    logits = _dot(a_hi, w_hi) + (_dot(a_hi, w_lo) + _dot(a_lo, w_hi)) + rb_ref[...]
    e_iota = lax.broadcasted_iota(I32, (tm, N_EXPERTS), 1).astype(F32)
    lane = lax.broadcasted_iota(I32, (tm, LANES), 1)
    te = jnp.zeros((tm, LANES), F32)
    tv = jnp.full((tm, LANES), NEG, F32)
    work = logits
    chosen = []
    for k in range(TOP_K):
        m = jnp.max(work, axis=-1, keepdims=True)
        idx = jnp.min(jnp.where(work == m, e_iota, float(N_EXPERTS)), axis=-1, keepdims=True)
        hit = e_iota == idx
        chosen.append(hit)
        te = jnp.where(lane == k, idx, te)
        tv = jnp.where(lane == k, m, tv)
        work = jnp.where(hit, -jnp.inf, work)
    member = sum(c.astype(F32) for c in chosen)
    earlier = (lax.broadcasted_iota(I32, (tm, tm), 0) > lax.broadcasted_iota(I32, (tm, tm), 1)).astype(BF16)
    before = _dot(earlier, member.astype(BF16)) + run_ref[...]
    for k in range(TOP_K):
        rank = jnp.sum(jnp.where(chosen[k], before, 0.0), axis=-1, keepdims=True)
        te = jnp.where(lane == TOP_K + k, rank, te)
    run_ref[...] = run_ref[...] + jnp.sum(member, axis=0, keepdims=True)
    cnt_ref[...] = run_ref[...]
    ex = jnp.exp(tv - jnp.max(tv, axis=-1, keepdims=True))
    te_ref[...] = te.astype(I32)
    tg_ref[...] = ex / jnp.sum(ex, axis=-1, keepdims=True)


def _fin2(cnt0, h1, o, wo_bf, g, b, rw, rb, total_rows, row_offset=0, into=None):
    T = h1.shape[0]
    tm = ROW_TILE
    blk0 = row_offset // tm
    n_own = T // tm
    n_steps = n_own if into is not None else (total_rows - row_offset) // tm
    row = lambda n: pl.BlockSpec((tm, n), lambda i: (jnp.minimum(i, n_own - 1), 0))
    out_row = lambda n: pl.BlockSpec((tm, n), lambda i: (i + blk0, 0))
    full = lambda a: pl.BlockSpec(a.shape, lambda i: (0,) * a.ndim)
    ins = [cnt0, h1, o, wo_bf, g, b, rw, rb]
    in_specs = [full(cnt0), row(D_MODEL), row(D_MODEL), full(wo_bf), full(g), full(b), full(rw), full(rb)]
    aliases = {}
    if into is not None:
        aliases = {len(ins) + k: k for k in range(len(into))}
        in_specs = in_specs + [pl.BlockSpec(memory_space=pl.ANY)] * len(into)
        ins = ins + list(into)
    return pl.pallas_call(
        functools.partial(_fin2_body, n_own=n_own),
        grid=(n_steps,),
        in_specs=in_specs,
        out_specs=[out_row(D_MODEL), out_row(LANES), out_row(LANES), full(cnt0)],
        out_shape=[jax.ShapeDtypeStruct((total_rows, D_MODEL), F32), jax.ShapeDtypeStruct((total_rows, LANES), I32),
                   jax.ShapeDtypeStruct((total_rows, LANES), F32), jax.ShapeDtypeStruct(cnt0.shape, F32)],
        scratch_shapes=[pltpu.VMEM(cnt0.shape, F32)],
        input_output_aliases=aliases,
        compiler_params=_params(("arbitrary",)),
        name="mem_out_ln2_router",
    )(*ins)


def _moe_body(ut_ref, ue_ref, nu_ref, rs_ref, re_ref, x_ref, wgu_ref, bgu_ref, wdn_ref, bdn_ref,
              y_ref, wgu_bf, wdn_bf):
    u = pl.program_id(0)
    bk = x_ref.shape[0]
    e = ue_ref[u]
    tile = ut_ref[u]
    prev = jnp.maximum(u - 1, 0)

    @pl.when((u == 0) | (e != ue_ref[prev]))
    def _():
        wgu_bf[...] = wgu_ref[0].astype(BF16)
        wdn_bf[...] = wdn_ref[0].astype(BF16)

    @pl.when(u < nu_ref[0])
    def _():
        x = x_ref[...].astype(BF16)
        g = _dot(x, wgu_bf[:, :D_FF]) + bgu_ref[0, :, :D_FF]
        v = _dot(x, wgu_bf[:, D_FF:]) + bgu_ref[0, :, D_FF:]
        g = jnp.minimum(g, SWIGLU_LIMIT)
        v = jnp.clip(v, -SWIGLU_LIMIT, SWIGLU_LIMIT)
        a = g * (1.0 / (1.0 + jnp.exp(-SWIGLU_ALPHA * g))) * (v + 1.0)
        y = _dot(a.astype(BF16), wdn_bf[...]) + bdn_ref[0]
        row = tile * bk + lax.broadcasted_iota(I32, (bk, 1), 0)
        mine = (row >= rs_ref[e]) & (row < re_ref[e])
        y = jnp.where(mine, y, 0.0)

        @pl.when((u == 0) | (tile != ut_ref[prev]))
        def _():
            y_ref[...] = y

        @pl.when((u > 0) & (tile == ut_ref[prev]))
        def _():
            y_ref[...] = y_ref[...] + y


def _moe_gmm(x_rows, units, w_gu, b_gu, w_dn, b_dn):
    N = x_rows.shape[0]
    bk = MOE_ROWS
    unit_tile, unit_e, n_units, r_start, r_end = units
    grid_spec = pltpu.PrefetchScalarGridSpec(
        num_scalar_prefetch=5,
        grid=(unit_tile.shape[0],),
        in_specs=[pl.BlockSpec((bk, D_MODEL), lambda u, ut, ue, *_: (ut[u], 0)),
                  pl.BlockSpec((1, D_MODEL, 2 * D_FF), lambda u, ut, ue, *_: (ue[u], 0, 0)),
                  pl.BlockSpec((1, 1, 2 * D_FF), lambda u, ut, ue, *_: (ue[u], 0, 0)),
                  pl.BlockSpec((1, D_FF, D_MODEL), lambda u, ut, ue, *_: (ue[u], 0, 0)),
                  pl.BlockSpec((1, 1, D_MODEL), lambda u, ut, ue, *_: (ue[u], 0, 0))],
        out_specs=pl.BlockSpec((bk, D_MODEL), lambda u, ut, ue, *_: (ut[u], 0)),
        scratch_shapes=[pltpu.VMEM((D_MODEL, 2 * D_FF), BF16), pltpu.VMEM((D_FF, D_MODEL), BF16)],
    )
    return pl.pallas_call(
        _moe_body,
        grid_spec=grid_spec,
        out_shape=jax.ShapeDtypeStruct((N, D_MODEL), F32),
        compiler_params=_params(("arbitrary",)),
        name="moe_experts",
    )(unit_tile, unit_e, n_units, r_start, r_end, x_rows, w_gu, b_gu, w_dn, b_dn)


FLAT_BITS = 17


def _moe_routing(te, counts):
    bk = MOE_ROWS
    T = te.shape[0]
    N = T * TOP_K
    assert N % bk == 0 and N <= (1 << FLAT_BITS)
    experts = jnp.arange(N_EXPERTS, dtype=I32)
    top_e = te[:, :TOP_K]
    r_end = jnp.cumsum(counts).astype(I32)
    r_start = r_end - counts
    onehot = top_e[:, :, None] == experts[None, None, :]
    pos = jnp.sum(jnp.where(onehot, r_start[None, None, :], 0), axis=-1) + te[:, TOP_K:2 * TOP_K]
    key = jnp.left_shift(top_e.reshape(-1), FLAT_BITS) + jnp.arange(N, dtype=I32)
    key_s = lax.sort(key)
    tok_s = jnp.right_shift(jnp.bitwise_and(key_s, (1 << FLAT_BITS) - 1), 2)
    first = r_start // bk
    last = (r_end - 1) // bk
    n_e = jnp.where(counts > 0, last - first + 1, 0)
    u_end = jnp.cumsum(n_e).astype(I32)
    u_start = u_end - n_e
    n_units = u_end[-1]
    u = jnp.minimum(jnp.arange(N // bk + N_EXPERTS - 1, dtype=I32), n_units - 1)
    unit_e = jnp.sum((u[:, None] >= u_end[None, :]).astype(I32), axis=1)
    unit_tile = u + jnp.sum(jnp.where(unit_e[:, None] == experts[None, :], (first - u_start)[None, :], 0), axis=1)
    return pos, tok_s, (unit_tile, unit_e, n_units.reshape(1), r_start, r_end)


def _fin3_body(h2_ref, tg_ref, y0_ref, y1_ref, y2_ref, y3_ref, g_ref, b_ref, o_ref):
    gate = lambda k: tg_ref[:, k:k + 1]
    y = ((gate(0) * y0_ref[...] + gate(1) * y1_ref[...])
         + (gate(2) * y2_ref[...] + gate(3) * y3_ref[...]))
    o_ref[...] = _layer_norm(DN_ALPHA * h2_ref[...] + y, g_ref[...], b_ref[...])


def _fin3(h2, tg, ys, g, b, row_offset, T):
    tm = ROW_TILE
    blk0 = row_offset // tm
    row = lambda n: pl.BlockSpec((tm, n), lambda i: (i, 0))
    full = lambda a: pl.BlockSpec(a.shape, lambda i: (0,) * a.ndim)
    return pl.pallas_call(
        _fin3_body,
        grid=(T // tm,),
        in_specs=[pl.BlockSpec((tm, D_MODEL), lambda i: (i + blk0, 0)), pl.BlockSpec((tm, LANES), lambda i: (i + blk0, 0))]
                 + [pl.BlockSpec((tm, D_MODEL), lambda i: (i + blk0, 0))] * TOP_K + [full(g), full(b)],
        out_specs=row(D_MODEL),
        out_shape=jax.ShapeDtypeStruct((T, D_MODEL), F32),
        compiler_params=_params(("arbitrary",)),
        name="combine_ln3",
    )(h2, tg, *ys, g, b)


def kernel(x_prompt, x_sample, cache_cmp_kv, cache_slc_kv, state_win_kv, state_pool, cache_mem_kv, page_table,
           mem_prompt, w_in, pool_w, pool_scale, cmp_pe, cmp_w1, cmp_b1, cmp_w2, cmp_b2, w_out, ln1_g, ln1_b,
           mem_wq, mem_wkv, mem_wo, ln2_g, ln2_b, router_w, router_b, exp_w_gu, exp_b_gu, exp_w_dn, exp_b_dn,
           ln3_g, ln3_b):
    Bp, S, D = x_prompt.shape
    Bd, Ls, _ = x_sample.shape
    Tp, Ts = Bp * S, Bd * Ls
    l = 0
    w_in_bf = w_in[l].astype(BF16)
    pool_w_bf = pool_w[l].astype(BF16)
    ps = pool_scale[l][None, :]
    cw = _compress_weights(cmp_pe[l], cmp_w1[l], cmp_b1[l], cmp_w2[l], cmp_b2[l])
    w_out_bf = w_out[l].astype(BF16)
    wq_bf = mem_wq[l].astype(BF16)
    wo_bf = mem_wo[l].astype(BF16)
    vec = lambda a: a[l][None, :]

    up, qp, kvc_p, kvs_p, kvw_p, gp, pool_p, kvc_t, kvs_t = _inproj_prompt(
        x_prompt.reshape(Tp, D), w_in_bf, pool_w_bf, ps, S)
    kc_p = _compress_prompt(kvc_p.reshape(Bp, S, KV_WIDTH), cw)
    nsa_p = _nsa_prompt(qp, gp, kc_p, kvs_p.reshape(Bp, S, KV_WIDTH), kvw_p.reshape(Bp, S, KV_WIDTH))
    mem_kv_p = _matmul(mem_prompt.reshape(Bp * MEM_LEN, D), mem_wkv[l]).reshape(Bp, MEM_LEN, 2 * D)
    h1_p, qm_p = _fin1(x_prompt.reshape(Tp, D), pool_p, nsa_p, w_out_bf, vec(ln1_g), vec(ln1_b), wq_bf, BF16)
    om_p = _memattn(qm_p.reshape(Bp, S, D), mem_kv_p, ROW_TILE).reshape(Tp, D)
    T = Tp + Ts
    *routed_p, cnt_p = _fin2(jnp.zeros((1, N_EXPERTS), F32), h1_p, om_p, wo_bf, vec(ln2_g), vec(ln2_b),
                             router_w[l], vec(router_b), total_rows=T)

    state_pad = jnp.pad(state_pool[l], ((0, 0), (1, 0), (0, 0)))
    us, qs, kvc_s, kvs_s, kvw_s, gs, pool_s = _inproj_sample(
        x_sample.reshape(Ts, D), w_in_bf, pool_w_bf, ps, state_pad, Ls, PAST_LEN)
    feature_major = lambda a: jnp.transpose(a, (0, 2, 3, 4, 1))
    nsa_s, win_next = _nsa_sample(qs.reshape(Bd, Ls, NSA_WIDTH), gs.reshape(Bd, Ls, 3 * NSA_HEADS),
                        kvs_s.reshape(Bd, Ls, KV_WIDTH), kvw_s.reshape(Bd, Ls, KV_WIDTH),
                        feature_major(state_win_kv[l]), feature_major(cache_cmp_kv[l]),
                        feature_major(cache_slc_kv[l]), page_table, cw, PAST_LEN)
    h1_s, qm_s = _fin1(x_sample.reshape(Ts, D), pool_s, nsa_s.reshape(Ts, NSA_WIDTH), w_out_bf,
                       vec(ln1_g), vec(ln1_b), wq_bf, F32)
    om_s = _memattn_few(qm_s.reshape(Bd, Ls, D), cache_mem_kv[l]).reshape(Ts, D)
    h2, te, tg, cnt_s = _fin2(cnt_p, h1_s, om_s, wo_bf, vec(ln2_g), vec(ln2_b), router_w[l], vec(router_b),
                              total_rows=T, row_offset=Tp, into=routed_p)

    pos, tok_s, units = _moe_routing(te, cnt_s[0].astype(I32))
    y_rows = _moe_gmm(h2[tok_s], units, exp_w_gu[l], exp_b_gu[l][:, None, :],
                      exp_w_dn[l], exp_b_dn[l][:, None, :])
    ys = [y_rows[pos[:, k]] for k in range(TOP_K)]
    y_prompt = _fin3(h2, tg, ys, vec(ln3_g), vec(ln3_b), 0, Tp).reshape(Bp, S, D)
    y_sample = _fin3(h2, tg, ys, vec(ln3_g), vec(ln3_b), Tp, Ts).reshape(Bd, Ls, D)

    kv6 = lambda a, b, n: a.reshape(1, b, n, 2, KVH, DH)
    row_major = lambda a: jnp.transpose(a, (0, 4, 1, 2, 3))
    win_p = kvw_p.reshape(Bp, S, KV_WIDTH)[:, S - min(WINDOW, S):]
    pool_state_p = up.reshape(Bp, S, POOL_WIDTH)[:, S - POOL_STATE:]
    pool_state_s = jnp.concatenate([state_pool[l], us.reshape(Bd, Ls, POOL_WIDTH)], axis=1)[:, -POOL_STATE:]
    return (y_prompt, y_sample,
            row_major(kvc_t)[None], row_major(kvs_t)[None], kv6(win_p, Bp, min(WINDOW, S)),
            pool_state_p[None], mem_kv_p.reshape(1, Bp, MEM_LEN, 2, MEM_HEADS, MEM_HEAD_DIM),
            kv6(kvc_s, Bd, Ls), kv6(kvs_s, Bd, Ls), row_major(win_next)[None], pool_state_s[None])
```

```python
import functools

import jax
import jax.numpy as jnp
from jax import lax
from jax.experimental import pallas as pl
from jax.experimental.pallas import tpu as pltpu

F32 = jnp.float32
BF16 = jnp.bfloat16
I32 = jnp.int32

D_MODEL = 1024
POOL_WIDTH = 512
POOL_WINDOWS = (2, 4, 8, 16)
POOL_GROUP = 128
POOL_STATE = 15
NSA_WIDTH = 512
DH = 64
NSA_HEADS = 8
KVH = 2
HPG = 4
CMP_LEN = 32
CMP_STRIDE = 16
CMP_HIDDEN = 256
SEL_LEN = 64
SEL_TOP = 16
WINDOW = 512
Q_BLOCK = 128
KV_WIDTH = 256
ATTN_SCALE = DH ** -0.5
FORCED_SCORE = 1e4
NEG = -1e30
MEM_LEN = 256
MEM_HEADS = 4
MEM_HEAD_DIM = 256
N_EXPERTS = 32
TOP_K = 4
D_FF = 1024
SWIGLU_LIMIT = 7.0
SWIGLU_ALPHA = 1.702
DN_ALPHA = 2.0 ** 0.25
LN_EPS = 1e-5
PAST_LEN = 2048
PAGE_SIZE = 128

LANES = 128
SEL_PAD = 128
KEY_TILE = 512
ROW_TILE = 512
MOE_ROWS = 512
BF16_ROWS = 16
VMEM_LIMIT = 56 * 1024 * 1024

HIGHEST = lax.Precision.HIGHEST


def _dot(a, b):
    return jnp.dot(a, b, preferred_element_type=F32)


def _dot_nt(a, b, precision=None):
    return lax.dot_general(a, b, (((1,), (1,)), ((), ())), preferred_element_type=F32,
                           precision=precision)


def _layer_norm(x, g, b):
    mu = jnp.mean(x, axis=-1, keepdims=True)
    xc = x - mu
    var = jnp.mean(xc * xc, axis=-1, keepdims=True)
    return xc * lax.rsqrt(var + LN_EPS) * g + b


def _params(sem, vmem=VMEM_LIMIT):
    return pltpu.CompilerParams(dimension_semantics=sem, vmem_limit_bytes=vmem)


def _split_store(u, up_ref, q_ref, kvc_ref, kvs_ref, kvw_ref, gate_ref):
    o1 = POOL_WIDTH
    o2 = o1 + NSA_WIDTH
    o3 = o2 + KV_WIDTH
    o4 = o3 + KV_WIDTH
    o5 = o4 + KV_WIDTH
    up_ref[...] = u[:, :o1]
    q_ref[...] = u[:, o1:o2]
    kvc_ref[...] = u[:, o2:o3]
    kvs_ref[...] = u[:, o3:o4]
    kvw_ref[...] = u[:, o4:o5]
    gate_ref[...] = 1.0 / (1.0 + jnp.exp(-u[:, o5:]))


def _inproj_prompt_body(x_ref, w_ref, pw_ref, ps_ref,
                        up_ref, q_ref, kvc_ref, kvs_ref, kvw_ref, gate_ref, pool_ref, kvct_ref, kvst_ref,
                        ext_ref, *, tm, tiles_per_seq):
    halo = POOL_STATE + 1
    t_in_seq = pl.program_id(0) % tiles_per_seq
    u = _dot(x_ref[...].astype(BF16), w_ref[...])
    _split_store(u, up_ref, q_ref, kvc_ref, kvs_ref, kvw_ref, gate_ref)
    o2 = POOL_WIDTH + NSA_WIDTH
    kvct_ref[0] = u[:, o2:o2 + KV_WIDTH].T.reshape(2, KVH, DH, tm)
    kvst_ref[0] = u[:, o2 + KV_WIDTH:o2 + 2 * KV_WIDTH].T.reshape(2, KVH, DH, tm)

    @pl.when(t_in_seq == 0)
    def _():
        ext_ref[0:halo, :] = jnp.zeros((halo, POOL_WIDTH), F32)

    ext_ref[halo:halo + tm, :] = u[:, :POOL_WIDTH]
    pos = t_in_seq * tm + lax.broadcasted_iota(I32, (tm, 1), 0)
    for gi, w in enumerate(POOL_WINDOWS):
        cols = slice(gi * POOL_GROUP, (gi + 1) * POOL_GROUP)
        acc = ext_ref[halo:halo + tm, cols]
        for k in range(1, w):
            acc = acc + ext_ref[halo - k:halo - k + tm, cols]
        cnt = jnp.minimum(pos + 1, w).astype(F32)
        d = acc / cnt - ext_ref[halo:halo + tm, cols]
        o = _dot(d.astype(BF16), pw_ref[gi])
        pool_ref[:, cols] = (o * ps_ref[:, cols]).astype(pool_ref.dtype)
    ext_ref[0:halo, :] = ext_ref[tm:tm + halo, :]


def _inproj_prompt(x2d, w_in_bf, pool_w_bf, pool_scale, seq_len):
    T = x2d.shape[0]
    tm = ROW_TILE
    outs = [POOL_WIDTH, NSA_WIDTH, KV_WIDTH, KV_WIDTH, KV_WIDTH, 3 * NSA_HEADS, POOL_WIDTH]
    row = lambda n: pl.BlockSpec((tm, n), lambda i: (i, 0))
    full = lambda a: pl.BlockSpec(a.shape, lambda i: (0,) * a.ndim)
    tps = seq_len // tm
    kvt_spec = pl.BlockSpec((1, 2, KVH, DH, tm), lambda i: (i // tps, 0, 0, 0, i % tps))
    kvt_shape = jax.ShapeDtypeStruct((T // seq_len, 2, KVH, DH, seq_len), F32)
    return pl.pallas_call(
        functools.partial(_inproj_prompt_body, tm=tm, tiles_per_seq=tps),
        grid=(T // tm,),
        in_specs=[row(D_MODEL), full(w_in_bf), full(pool_w_bf), full(pool_scale)],
        out_specs=[row(n) for n in outs] + [kvt_spec] * 2,
        out_shape=[jax.ShapeDtypeStruct((T, n), F32) for n in outs[:-1]]
                  + [jax.ShapeDtypeStruct((T, outs[-1]), BF16)] + [kvt_shape] * 2,
        scratch_shapes=[pltpu.VMEM((tm + POOL_STATE + 1, POOL_WIDTH), F32)],
        compiler_params=_params(("arbitrary",)),
        name="inproj_prompt",
    )(x2d, w_in_bf, pool_w_bf, pool_scale)


def _inproj_sample_body(x_ref, w_ref, pw_ref, ps_ref, st_ref,
                        up_ref, q_ref, kvc_ref, kvs_ref, kvw_ref, gate_ref, pool_ref,
                        ext_ref, *, nb, ls, pos0):
    halo = POOL_STATE + 1
    tm = nb * ls
    u = _dot(x_ref[...].astype(BF16), w_ref[...])
    _split_store(u, up_ref, q_ref, kvc_ref, kvs_ref, kvw_ref, gate_ref)
    ext_ref[:, 0:halo, :] = st_ref[...]
    ext_ref[:, halo:halo + ls, :] = u[:, :POOL_WIDTH].reshape(nb, ls, POOL_WIDTH)
    pos = pos0 + lax.broadcasted_iota(I32, (1, ls, 1), 1)
    for gi, w in enumerate(POOL_WINDOWS):
        cols = slice(gi * POOL_GROUP, (gi + 1) * POOL_GROUP)
        acc = ext_ref[:, halo:halo + ls, cols]
        for k in range(1, w):
            acc = acc + ext_ref[:, halo - k:halo - k + ls, cols]
        cnt = jnp.minimum(pos + 1, w).astype(F32)
        d = acc / cnt - ext_ref[:, halo:halo + ls, cols]
        o = _dot(d.reshape(tm, POOL_GROUP).astype(BF16), pw_ref[gi])
        pool_ref[:, cols] = (o * ps_ref[:, cols]).astype(pool_ref.dtype)


def _inproj_sample(x2d, w_in_bf, pool_w_bf, pool_scale, state_pad, ls, pos0):
    T = x2d.shape[0]
    nb = ROW_TILE // ls
    tm = nb * ls
    outs = [POOL_WIDTH, NSA_WIDTH, KV_WIDTH, KV_WIDTH, KV_WIDTH, 3 * NSA_HEADS, POOL_WIDTH]
    row = lambda n: pl.BlockSpec((tm, n), lambda i: (i, 0))
    full = lambda a: pl.BlockSpec(a.shape, lambda i: (0,) * a.ndim)
    return pl.pallas_call(
        functools.partial(_inproj_sample_body, nb=nb, ls=ls, pos0=pos0),
        grid=(T // tm,),
        in_specs=[row(D_MODEL), full(w_in_bf), full(pool_w_bf), full(pool_scale),
                  pl.BlockSpec((nb, POOL_STATE + 1, POOL_WIDTH), lambda i: (i, 0, 0))],
        out_specs=[row(n) for n in outs],
        out_shape=[jax.ShapeDtypeStruct((T, n), F32) for n in outs],
        scratch_shapes=[pltpu.VMEM((nb, POOL_STATE + 1 + ls, POOL_WIDTH), F32)],
        compiler_params=_params(("arbitrary",)),
        name="inproj_sample",
    )(x2d, w_in_bf, pool_w_bf, pool_scale, state_pad)


def _matmul_body(x_ref, w_ref, o_ref):
    o_ref[...] = _dot(x_ref[...].astype(BF16), w_ref[...].astype(BF16))


def _matmul(x, w, tn=512):
    M, K = x.shape
    N = w.shape[1]
    return pl.pallas_call(
        _matmul_body,
        grid=(N // tn,),
        in_specs=[pl.BlockSpec((M, K), lambda j: (0, 0)), pl.BlockSpec((K, tn), lambda j: (0, j))],
        out_specs=pl.BlockSpec((M, tn), lambda j: (0, j)),
        out_shape=jax.ShapeDtypeStruct((M, N), F32),
        compiler_params=_params(("arbitrary",)),
        name="mem_kv_proj",
    )(x, w)


def _gelu_tanh(x):
    c = 0.7978845608028654
    return 0.5 * x * (1.0 + jnp.tanh(c * (x + 0.044715 * (x * x * x))))


def _compress(kv_refs, n_chunks, w1_ref, pe_ref, b1_ref, w2_ref, b2_ref):
    lo = _lane_iota(n_chunks) < DH
    quads = CMP_STRIDE // 4
    outs = []
    for c in range(2):
        acc_a = jnp.zeros((2 * n_chunks, CMP_HIDDEN), F32)
        acc_b = jnp.zeros((2 * n_chunks, CMP_HIDDEN), F32)
        for i in range(quads):
            x = [kv_refs[c][pl.ds(4 * i + m, n_chunks, stride=CMP_STRIDE), :] for m in range(4)]
            r = [pltpu.roll(v, DH, 1) for v in x]
            x_g0 = jnp.concatenate([jnp.where(lo, x[0], r[1]), jnp.where(lo, x[2], r[3])], axis=1)
            x_g1 = jnp.concatenate([jnp.where(lo, r[0], x[1]), jnp.where(lo, r[2], x[3])], axis=1)
            xq = jnp.concatenate([x_g0, x_g1], axis=0)
            acc_a = acc_a + _dot((xq + pe_ref[c, i:i + 1, :]).astype(BF16), w1_ref[c, i])
            acc_b = acc_b + _dot((xq + pe_ref[c, quads + i:quads + i + 1, :]).astype(BF16), w1_ref[c, quads + i])
        hid = acc_a + pltpu.roll(acc_b, 2 * n_chunks - 1, 0) + b1_ref[c]
        outs.append(_dot(_gelu_tanh(hid).astype(BF16), w2_ref[c]) + b2_ref[c])
    return outs


def _compress_prompt_body(kv_ref, w1_ref, pe_ref, b1_ref, w2_ref, b2_ref, o_ref, k_ref, v_ref, *, n_chunks):
    k_ref[...] = kv_ref[0, :, :LANES]
    v_ref[...] = kv_ref[0, :, LANES:]
    keys, values = _compress((k_ref, v_ref), n_chunks, w1_ref, pe_ref, b1_ref, w2_ref, b2_ref)
    o_ref[0, 0] = keys
    o_ref[0, 1] = values


def _compress_prompt(kvc, cw):
    B, S, _ = kvc.shape
    n_chunks = S // CMP_STRIDE
    full = lambda a: pl.BlockSpec(a.shape, lambda b: (0,) * a.ndim)
    return pl.pallas_call(
        functools.partial(_compress_prompt_body, n_chunks=n_chunks),
        grid=(B,),
        in_specs=[pl.BlockSpec((1, S, KV_WIDTH), lambda b: (b, 0, 0))] + [full(a) for a in cw],
        out_specs=pl.BlockSpec((1, 2, KVH * n_chunks, LANES), lambda b: (b, 0, 0, 0)),
        out_shape=jax.ShapeDtypeStruct((B, 2, KVH * n_chunks, LANES), F32),
        scratch_shapes=[pltpu.VMEM((S, LANES), F32)] * 2,
        compiler_params=_params(("arbitrary",)),
        name="compress_prompt",
    )(kvc, *cw)


def _compress_weights(cmp_pe, cmp_w1, cmp_b1, cmp_w2, cmp_b2):
    nq = CMP_LEN // 4
    w1 = cmp_w1.reshape(2, nq, 4 * DH, CMP_HIDDEN).astype(BF16)
    pe = cmp_pe.reshape(2, nq, 4 * DH)
    b1 = cmp_b1[:, None, :]
    w2 = jnp.stack([jnp.concatenate([cmp_w2[0], jnp.zeros_like(cmp_w2[0])], axis=1),
                    jnp.concatenate([cmp_w2[1], cmp_w2[1]], axis=1)]).astype(BF16)
    b2 = jnp.stack([jnp.concatenate([cmp_b2[0], jnp.zeros_like(cmp_b2[0])]),
                    jnp.concatenate([cmp_b2[1], cmp_b2[1]])])[:, None, :]
    return w1, pe, b1, w2, b2


def _softmax_rows(s):
    e = jnp.exp(s - jnp.max(s, axis=-1, keepdims=True))
    return e * (1.0 / jnp.sum(e, axis=-1, keepdims=True))


def _lane_iota(n):
    return lax.broadcasted_iota(I32, (n, LANES), 1)


def _key_alibi_cols(pos, lane):
    hi = jnp.left_shift(jnp.right_shift(pos, 6), 6).astype(F32)
    lo = jnp.bitwise_and(pos, SEL_LEN - 1).astype(F32)
    return jnp.where(lane == DH, hi,
                     jnp.where(lane == DH + 1, lo,
                               jnp.where((lane == DH + 2) | (lane == DH + 3), 1.0, 0.0)))


def _key_alibi_rows(pos, sub):
    hi = jnp.left_shift(jnp.right_shift(pos, 6), 6).astype(F32)
    lo = jnp.bitwise_and(pos, SEL_LEN - 1).astype(F32)
    return jnp.where(sub == 0, hi, jnp.where(sub == 1, lo, jnp.where((sub == 2) | (sub == 3), 1.0, 0.0)))


def _halves(x, zero_hi=False):
    lo = _lane_iota(x.shape[0]) < DH
    r = pltpu.roll(x, DH, 1)
    if zero_hi:
        return jnp.where(lo, x, 0.0), jnp.where(lo, r, 0.0)
    return jnp.where(lo, x, r), jnp.where(lo, r, x)


def _fill_queries(qa_ref, qs, q_pos, lq):
    lane = _lane_iota(lq)
    lo_half = lane < DH
    q_hi = jnp.left_shift(jnp.right_shift(q_pos, 7), 7).astype(F32)
    q_lo = jnp.bitwise_and(q_pos, LANES - 1).astype(F32)
    for h in range(NSA_HEADS):
        g, hl = divmod(h, HPG)
        slope = 2.0 ** (-(h + 1))
        slab = qs[:, (h // 2) * LANES:(h // 2 + 1) * LANES]
        if h % 2:
            slab = pltpu.roll(slab, DH, 1)
        ex = jnp.where((lane == DH) | (lane == DH + 1), slope,
                       jnp.where(lane == DH + 2, -slope * q_hi,
                                 jnp.where(lane == DH + 3, -slope * q_lo,
                                           jnp.where(lane == DH + 4, NEG, 0.0))))
        qa_ref[g, hl * lq:(hl + 1) * lq, 0:LANES] = jnp.where(lo_half, slab, ex).astype(BF16)


def _stack4(x):
    return jnp.concatenate([x] * HPG, axis=0)


def _cmp_branch(qa_ref, kc_k, kc_v, q_pos4, lq, n_cmp):
    c_end = lax.broadcasted_iota(I32, (1, n_cmp), 1) * CMP_STRIDE + (CMP_LEN - 1)
    m_c = c_end <= q_pos4
    any_c = (q_pos4 >= CMP_LEN - 1).astype(F32)
    outs, psums = [], []
    for g in range(KVH):
        s = jnp.where(m_c, _dot_nt(qa_ref[g, :, 0:LANES], kc_k[g][...]), NEG)
        p = _softmax_rows(s) * any_c
        outs.append(_dot(p.astype(BF16), kc_v[g][...]))
        psums.append(p[0:lq] + p[lq:2 * lq] + p[2 * lq:3 * lq] + p[3 * lq:4 * lq])
    return outs, psums


def _split3(x):
    hi = x.astype(BF16)
    r1 = x - hi.astype(F32)
    mid = r1.astype(BF16)
    lo = (r1 - mid.astype(F32)).astype(BF16)
    return hi, mid, lo


def _top_blocks_t(imp_ts, pos0, n_blk=SEL_PAD):
    blk = lax.broadcasted_iota(I32, (n_blk, LANES), 0)
    qp_t = pos0 + lax.broadcasted_iota(I32, (n_blk, LANES), 1)
    cur = jnp.right_shift(qp_t, 6)
    forced = (blk == 0) | (blk == cur) | (blk == cur - 1)
    valid = jnp.left_shift(blk, 6) <= qp_t
    v = jnp.concatenate([jnp.where(valid, jnp.where(forced, FORCED_SCORE, t), -1.0) for t in imp_ts], axis=1)
    blk_f = lax.broadcasted_iota(I32, (n_blk, KVH * LANES), 0).astype(F32)
    sel = jnp.zeros((n_blk, KVH * LANES), F32)
    for _ in range(SEL_TOP):
        m = jnp.max(v, axis=0, keepdims=True)
        idx = jnp.min(jnp.where(v == m, blk_f, float(n_blk)), axis=0, keepdims=True)
        hit = blk_f == idx
        sel = jnp.where(hit, 1.0, sel)
        v = jnp.where(hit, -jnp.inf, v)
    out = [jnp.where((sel[:, g * LANES:(g + 1) * LANES] > 0.5) & valid, 0.0, NEG) for g in range(KVH)]
    if n_blk < SEL_PAD:
        out = [jnp.concatenate([b, jnp.full((SEL_PAD - n_blk, LANES), NEG, F32)], axis=0) for b in out]
    return out


def _select_blocks(psums, ovt_ref, pos0, lq, n_blk):
    imp_ts = []
    for g in range(KVH):
        ps = psums[g]
        if lq < LANES:
            ps = jnp.concatenate([ps, jnp.zeros((LANES - lq, ps.shape[1]), F32)], axis=0)
        imp_ts.append(_dot_nt(ovt_ref[0:n_blk, :], ps, precision=HIGHEST))
    return [b.T[:lq].astype(BF16) for b in _top_blocks_t(imp_ts, pos0, n_blk)]


def _store_selbias(qa_ref, selbias, lq):
    for g in range(KVH):
        for hl in range(HPG):
            qa_ref[g, hl * lq:(hl + 1) * lq, LANES:2 * LANES] = selbias[g]


def _combine(gates, eg_ref, o_c, o_s, o_w, lq):
    lo_half = _lane_iota(lq) < DH

    def assemble(per_group):
        slabs = []
        for k in range(NSA_HEADS // 2):
            g, hl = divmod(2 * k, HPG)
            a = per_group[g][hl * lq:(hl + 1) * lq]
            b = per_group[g][(hl + 1) * lq:(hl + 2) * lq]
            slabs.append(jnp.where(lo_half, a, b))
        return jnp.concatenate(slabs, axis=1)

    return (_dot(gates, eg_ref[0]) * assemble(o_c)
            + _dot(gates, eg_ref[1]) * assemble(o_s)
            + _dot(gates, eg_ref[2]) * assemble(o_w))


def _gate_expand():
    r = jnp.arange(3 * NSA_HEADS)
    c = jnp.arange(NSA_WIDTH)
    return jnp.stack([(r[:, None] == 3 * (c[None, :] // DH) + br).astype(F32) for br in range(3)])


def _overlap_t(n_cmp):
    n = jnp.arange(n_cmp)
    s = jnp.arange(SEL_PAD)
    c_first = n * CMP_STRIDE
    c_end = c_first + CMP_LEN - 1
    b_first = s * SEL_LEN
    return ((c_first[None, :] < b_first[:, None] + SEL_LEN) & (c_end[None, :] >= b_first[:, None])).astype(F32)


def _key_rows(kv_f32, pos, invalid=None):
    n = kv_f32.shape[0]
    lane = _lane_iota(n)
    ex = _key_alibi_cols(pos, lane)
    if invalid is not None:
        ex = jnp.where((lane == DH + 4) & invalid, 1.0, ex)
    lo = lane < DH
    return jnp.where(lo, kv_f32, ex), jnp.where(lo, pltpu.roll(kv_f32, DH, 1), ex)


def _block_onehot(pos, n):
    return (lax.broadcasted_iota(I32, (n, SEL_PAD), 1) == jnp.right_shift(pos, 6)).astype(BF16)


def _tile4(x):
    return jnp.concatenate([x] * HPG, axis=1)


def _nsa_prompt_body(q_ref, gate_ref, kc_ref, kvs_ref, w0_ref, w1_ref, w2_ref, w3_ref, w4_ref,
                     ovt_ref, egt_ref, wband_ref, o_ref,
                     kck0, kck1, kcv0, kcv1, ka0, ka1, vt0, vt1, qa0, qa1,
                     sa_ref, sb_ref, m_ref, l_ref, acc_ref, oc_ref, *, seq_len):
    j = pl.program_id(1)
    kc_k, kc_vt, kaug, v_t, qa_t = (kck0, kck1), (kcv0, kcv1), (ka0, ka1), (vt0, vt1), (qa0, qa1)
    n_cmp = kc_ref.shape[2] // KVH
    lq = Q_BLOCK
    cols = HPG * lq

    @pl.when(j == 0)
    def _():
        for g in range(KVH):
            kc_k[g][...] = kc_ref[0, 0, g * n_cmp:(g + 1) * n_cmp, :].astype(BF16)
            kc_vt[g][...] = kc_ref[0, 1, g * n_cmp:(g + 1) * n_cmp, :].T[0:DH].astype(BF16)

        def build(i, _):
            r0 = pl.multiple_of(i * KEY_TILE, KEY_TILE)
            pos = r0 + lax.broadcasted_iota(I32, (KEY_TILE, 1), 0)
            k0, k1 = _key_rows(kvs_ref[0, pl.ds(r0, KEY_TILE), :LANES], pos)
            onehot = _block_onehot(pos, KEY_TILE)
            ka0[pl.ds(r0, KEY_TILE), :] = jnp.concatenate([k0.astype(BF16), onehot], axis=1)
            ka1[pl.ds(r0, KEY_TILE), :] = jnp.concatenate([k1.astype(BF16), onehot], axis=1)
            vt = kvs_ref[0, pl.ds(r0, KEY_TILE), LANES:].T.astype(BF16)
            vt0[:, pl.ds(r0, KEY_TILE)] = vt[0:DH]
            vt1[:, pl.ds(r0, KEY_TILE)] = vt[DH:]
            return 0

        lax.fori_loop(0, seq_len // KEY_TILE, build, 0)

    st = j * Q_BLOCK
    q_pos = st + lax.broadcasted_iota(I32, (1, lq), 1)
    q_pos4 = _tile4(q_pos)

    q_t = (q_ref[...] * ATTN_SCALE).T
    sub = lax.broadcasted_iota(I32, (DH, lq), 0)
    q_hi = jnp.left_shift(jnp.right_shift(q_pos, 7), 7).astype(F32)
    q_lo = jnp.bitwise_and(q_pos, LANES - 1).astype(F32)
    for h in range(NSA_HEADS):
        g, hl = divmod(h, HPG)
        slope = 2.0 ** (-(h + 1))
        ex = jnp.where(sub <= 1, slope,
                       jnp.where(sub == 2, -slope * q_hi,
                                 jnp.where(sub == 3, -slope * q_lo, jnp.where(sub == 4, NEG, 0.0))))
        qa_t[g][0:DH, hl * lq:(hl + 1) * lq] = q_t[h * DH:(h + 1) * DH].astype(BF16)
        qa_t[g][DH:2 * DH, hl * lq:(hl + 1) * lq] = ex.astype(BF16)

    any_c = (q_pos4 >= CMP_LEN - 1).astype(F32)
    blocks_per_sel = SEL_LEN // CMP_STRIDE
    n_variants = -(-n_cmp // LANES)
    last_visible = (st + Q_BLOCK - CMP_LEN) // CMP_STRIDE

    def cmp_and_select(n_c):
        n_b = n_c // blocks_per_sel
        c_end = lax.broadcasted_iota(I32, (n_c, 1), 0) * CMP_STRIDE + (CMP_LEN - 1)
        m_c = c_end <= q_pos4
        ovt_bf = ovt_ref[0:n_b, 0:n_c].astype(BF16)
        imp_ts = []
        for g in range(KVH):
            s = jnp.where(m_c, _dot(kc_k[g][0:n_c, :], qa_t[g][0:LANES, :]), NEG)
            e = jnp.exp(s - jnp.max(s, axis=0, keepdims=True))
            p = e * (any_c / jnp.sum(e, axis=0, keepdims=True))
            oc_ref[g] = _dot(kc_vt[g][:, 0:n_c], p.astype(BF16))
            psum = p[:, 0:lq] + p[:, lq:2 * lq] + p[:, 2 * lq:3 * lq] + p[:, 3 * lq:4 * lq]
            imp_ts.append(sum(_dot(ovt_bf, t) for t in _split3(psum)))
        bias = _top_blocks_t(imp_ts, st, n_b)
        for g in range(KVH):
            for hl in range(HPG):
                qa_t[g][2 * DH:, hl * lq:(hl + 1) * lq] = bias[g].astype(BF16)

    variant = jnp.minimum(jnp.maximum(last_visible, 0) // LANES, n_variants - 1)
    for k in range(n_variants):
        pl.when(variant == k)(functools.partial(cmp_and_select, min(n_cmp, (k + 1) * LANES)))
    o_c = [oc_ref[g] for g in range(KVH)]

    n_tiles = (st + Q_BLOCK + KEY_TILE - 1) // KEY_TILE
    for g in range(KVH):
        m_ref[g] = jnp.full((1, cols), NEG, F32)
        l_ref[g] = jnp.zeros((1, cols), F32)
        acc_ref[g] = jnp.zeros((DH, cols), F32)

    def scores(t, s_ref):
        r0 = pl.multiple_of(t * KEY_TILE, KEY_TILE)
        for g in range(KVH):
            s_ref[g] = _dot(kaug[g][pl.ds(r0, KEY_TILE), :], qa_t[g][...])

    def consume(t, s_ref, masked):
        r0 = pl.multiple_of(t * KEY_TILE, KEY_TILE)
        for g in range(KVH):
            s = s_ref[g]
            if masked:
                k_pos = r0 + lax.broadcasted_iota(I32, (KEY_TILE, 1), 0)
                s = jnp.where(k_pos <= q_pos4, s, NEG)
            m = m_ref[g]
            m_new = jnp.maximum(m, jnp.max(s, axis=0, keepdims=True))
            a = jnp.exp(m - m_new)
            e = jnp.exp(s - m_new)
            m_ref[g] = m_new
            l_ref[g] = a * l_ref[g] + jnp.sum(e, axis=0, keepdims=True)
            acc_ref[g] = a * acc_ref[g] + _dot(v_t[g][:, pl.ds(r0, KEY_TILE)], e.astype(BF16))

    scores(0, sa_ref)
    n_pairs = (n_tiles - 1) // 2

    def pair(u, _):
        scores(2 * u + 1, sb_ref)
        consume(2 * u, sa_ref, False)
        scores(2 * u + 2, sa_ref)
        consume(2 * u + 1, sb_ref, False)
        return 0

    lax.fori_loop(0, n_pairs, pair, 0)
    odd_tail = (n_tiles - 1) - 2 * n_pairs == 1

    @pl.when(odd_tail)
    def _():
        scores(n_tiles - 1, sb_ref)
        consume(n_tiles - 2, sa_ref, False)
        consume(n_tiles - 1, sb_ref, True)

    @pl.when(jnp.logical_not(odd_tail))
    def _():
        consume(n_tiles - 1, sa_ref, True)

    o_s = [acc_ref[g] * (1.0 / l_ref[g]) for g in range(KVH)]

    band = jnp.concatenate([w0_ref[0], w1_ref[0], w2_ref[0], w3_ref[0], w4_ref[0]], axis=0)
    n_win = band.shape[0]
    w_pos_col = st - WINDOW + lax.broadcasted_iota(I32, (n_win, 1), 0)
    kw_k = _key_rows(band[:, :LANES], jnp.maximum(w_pos_col, 0), invalid=w_pos_col < 0)
    vw_t = band[:, LANES:].T.astype(BF16)
    band_bias = _tile4(wband_ref[...])
    o_w = []
    for g in range(KVH):
        s = _dot(kw_k[g].astype(BF16), qa_t[g][0:LANES, :]) + band_bias
        e = jnp.exp(s - jnp.max(s, axis=0, keepdims=True))
        o_w.append(_dot(vw_t[g * DH:(g + 1) * DH], e.astype(BF16)) * (1.0 / jnp.sum(e, axis=0, keepdims=True)))

    def heads(per_group):
        return jnp.concatenate([per_group[h // HPG][:, (h % HPG) * lq:(h % HPG + 1) * lq]
                                for h in range(NSA_HEADS)], axis=0)

    gates = gate_ref[...]
    out_t = (_dot_nt(egt_ref[0], gates) * heads(o_c)
             + _dot_nt(egt_ref[1], gates) * heads(o_s)
             + _dot_nt(egt_ref[2], gates) * heads(o_w))
    o_ref[...] = out_t.T.astype(o_ref.dtype)


def _gate_expand_t():
    r = jnp.arange(3 * NSA_HEADS)
    c = jnp.arange(NSA_WIDTH)
    return jnp.stack([(3 * (c[:, None] // DH) + br == r[None, :]).astype(F32) for br in range(3)])


def _nsa_prompt(q, gates, kc, kvs, kvw):
    B, S, _ = kvs.shape
    nqb = S // Q_BLOCK
    n_cmp = kc.shape[2] // KVH
    ovt = _overlap_t(n_cmp)
    egt = _gate_expand_t()
    n_band = WINDOW // Q_BLOCK + 1
    d_band = jnp.arange(Q_BLOCK)[None, :] + WINDOW - jnp.arange(n_band * Q_BLOCK)[:, None]
    wband = jnp.where((d_band >= 0) & (d_band < WINDOW), 0.0, NEG).astype(F32)

    def band_spec(i):
        return pl.BlockSpec((1, Q_BLOCK, KV_WIDTH),
                            lambda b, j, i=i: (b, jnp.maximum(j - (n_band - 1) + i, 0), 0))

    full = lambda a: pl.BlockSpec(a.shape, lambda b, j: (0,) * a.ndim)
    return pl.pallas_call(
        functools.partial(_nsa_prompt_body, seq_len=S),
        grid=(B, nqb),
        in_specs=[pl.BlockSpec((Q_BLOCK, NSA_WIDTH), lambda b, j: (b * nqb + j, 0)),
                  pl.BlockSpec((Q_BLOCK, 3 * NSA_HEADS), lambda b, j: (b * nqb + j, 0)),
                  pl.BlockSpec((1,) + kc.shape[1:], lambda b, j: (b, 0, 0, 0)),
                  pl.BlockSpec((1, S, KV_WIDTH), lambda b, j: (b, 0, 0))]
                 + [band_spec(i) for i in range(n_band)] + [full(ovt), full(egt), full(wband)],
        out_specs=pl.BlockSpec((Q_BLOCK, NSA_WIDTH), lambda b, j: (b * nqb + j, 0)),
        out_shape=jax.ShapeDtypeStruct((B * S, NSA_WIDTH), BF16),
        scratch_shapes=[pltpu.VMEM((n_cmp, LANES), BF16)] * 2
                       + [pltpu.VMEM((DH, n_cmp), BF16)] * 2
                       + [pltpu.VMEM((S, 2 * LANES), BF16)] * 2
                       + [pltpu.VMEM((DH, S), BF16)] * 2
                       + [pltpu.VMEM((2 * LANES, HPG * Q_BLOCK), BF16)] * 2
                       + [pltpu.VMEM((KVH, KEY_TILE, HPG * Q_BLOCK), F32)] * 2
                       + [pltpu.VMEM((KVH, 1, HPG * Q_BLOCK), F32)] * 2
                       + [pltpu.VMEM((KVH, DH, HPG * Q_BLOCK), F32)] * 2,
        compiler_params=_params(("arbitrary", "arbitrary")),
        name="nsa_prompt",
    )(q, gates, kc, kvs, *([kvw] * n_band), ovt, egt, wband)


def _nsa_sample_body(pt_ref, q_ref, gate_ref, kvs_new_ref, win_ref, kvw_new_ref, *rest,
                     n_pages, ls, past_len):
    cmp_pages = rest[:n_pages]
    slc_pages = rest[n_pages:2 * n_pages]
    (w1_ref, pe_ref, b1_ref, w2_ref, b2_ref, ovt_ref, eg_ref, o_ref, wout_ref,
     full_k, full_v, kck0, kck1, kcv0, kcv1, kt0, kt1, vt0, vt1, wkt0, wkt1, qa_ref) = rest[2 * n_pages:]
    del pt_ref
    kc_k, kc_v, kaug_t, v_t, wk_t = (kck0, kck1), (kcv0, kcv1), (kt0, kt1), (vt0, vt1), (wkt0, wkt1)
    n_cmp = past_len // CMP_STRIDE
    w_rows = win_ref.shape[4]
    lq = BF16_ROWS
    rows = HPG * lq
    w_start = past_len - w_rows

    @pl.when(pl.program_id(0) == 0)
    def _():
        sub = lax.broadcasted_iota(I32, (DH, past_len), 0)
        pos = lax.broadcasted_iota(I32, (1, past_len), 1)
        ex = _key_alibi_rows(pos, sub).astype(BF16)
        onehot = (lax.broadcasted_iota(I32, (SEL_PAD, past_len), 0) == jnp.right_shift(pos, 6)).astype(BF16)
        subw = lax.broadcasted_iota(I32, (DH, w_rows), 0)
        exw = _key_alibi_rows(w_start + lax.broadcasted_iota(I32, (1, w_rows), 1), subw).astype(BF16)
        for g in range(KVH):
            kaug_t[g][DH:2 * DH, :] = ex
            kaug_t[g][2 * DH:, :] = onehot
            wk_t[g][DH:, :] = exw

    for p in range(n_pages):
        cols = slice(p * PAGE_SIZE, (p + 1) * PAGE_SIZE)
        full_k[cols, :] = cmp_pages[p][0, 0].reshape(2 * DH, PAGE_SIZE).T
        full_v[cols, :] = cmp_pages[p][0, 1].reshape(2 * DH, PAGE_SIZE).T
        for g in range(KVH):
            kaug_t[g][0:DH, cols] = slc_pages[p][0, 0, g].astype(BF16)
            vt = slc_pages[p][0, 1, g].astype(BF16)
            v_t[g][0:DH, cols] = vt
            v_t[g][DH:, cols] = vt
    for g in range(KVH):
        wk_t[g][0:DH, :] = win_ref[0, 0, g].astype(BF16)

    keys_c, values_c = _compress((full_k, full_v), n_cmp, w1_ref, pe_ref, b1_ref, w2_ref, b2_ref)
    for g in range(KVH):
        kc_k[g][...] = keys_c[g * n_cmp:(g + 1) * n_cmp].astype(BF16)
        kc_v[g][...] = values_c[g * n_cmp:(g + 1) * n_cmp].astype(BF16)

    pad_q = jnp.zeros((lq - ls, NSA_WIDTH), F32)
    q_pos = past_len + lax.broadcasted_iota(I32, (lq, 1), 0)
    q_pos4 = _stack4(q_pos)
    _fill_queries(qa_ref, jnp.concatenate([q_ref[0] * ATTN_SCALE, pad_q], axis=0), q_pos, lq)
    gates = jnp.concatenate([gate_ref[0], jnp.zeros((lq - ls, 3 * NSA_HEADS), F32)], axis=0)

    o_c, psums = _cmp_branch(qa_ref, kc_k, kc_v, q_pos4, lq, n_cmp)
    n_sel = -(-(past_len + lq) // SEL_LEN)
    n_blk = -(-n_sel // 8) * 8
    _store_selbias(qa_ref, _select_blocks(psums, ovt_ref, past_len, lq, n_blk), lq)

    pad_k = jnp.zeros((LANES - ls, KV_WIDTH), F32)
    new_pos_col = past_len + lax.broadcasted_iota(I32, (LANES, 1), 0)
    new_pos = past_len + lax.broadcasted_iota(I32, (1, LANES), 1)
    new_s = jnp.concatenate([kvs_new_ref[0], pad_k], axis=0)
    new_w = jnp.concatenate([kvw_new_ref[0], pad_k], axis=0)
    ks_new = _key_rows(new_s[:, :LANES], new_pos_col)
    vs_new = _halves(new_s[:, LANES:])
    kw_new = _key_rows(new_w[:, :LANES], new_pos_col)
    vw_new = _halves(new_w[:, LANES:])
    onehot_new = _block_onehot(new_pos_col, LANES)
    causal_new = q_pos4 >= new_pos

    o_s, o_w = [], []
    d_past = q_pos4 - (w_start + lax.broadcasted_iota(I32, (1, w_rows), 1))
    m_past = (d_past >= 0) & (d_past < WINDOW)
    d_new = q_pos4 - new_pos
    m_new = (d_new >= 0) & (d_new < WINDOW)
    for g in range(KVH):
        s_past = _dot(qa_ref[g], kaug_t[g][...])
        k_new = jnp.concatenate([ks_new[g].astype(BF16), onehot_new], axis=1)
        s_new = jnp.where(causal_new, _dot_nt(qa_ref[g], k_new), NEG)
        m = jnp.maximum(jnp.max(s_past, axis=-1, keepdims=True), jnp.max(s_new, axis=-1, keepdims=True))
        e_past = jnp.exp(s_past - m)
        e_new = jnp.exp(s_new - m)
        den = jnp.sum(e_past, axis=-1, keepdims=True) + jnp.sum(e_new, axis=-1, keepdims=True)
        acc = _dot_nt(e_past.astype(BF16), v_t[g][...]) + _dot(e_new.astype(BF16), vs_new[g].astype(BF16))
        o_s.append(acc / den)

        sw_past = jnp.where(m_past, _dot(qa_ref[g, :, 0:LANES], wk_t[g][...]), NEG)
        sw_new = jnp.where(m_new, _dot_nt(qa_ref[g, :, 0:LANES], kw_new[g].astype(BF16)), NEG)
        m = jnp.maximum(jnp.max(sw_past, axis=-1, keepdims=True), jnp.max(sw_new, axis=-1, keepdims=True))
        e_past = jnp.exp(sw_past - m)
        e_new = jnp.exp(sw_new - m)
        den = jnp.sum(e_past, axis=-1, keepdims=True) + jnp.sum(e_new, axis=-1, keepdims=True)
        vw = win_ref[0, 1, g].astype(BF16)
        vw2 = jnp.concatenate([vw, vw], axis=0)
        acc = _dot_nt(e_past.astype(BF16), vw2) + _dot(e_new.astype(BF16), vw_new[g].astype(BF16))
        o_w.append(acc / den)

    o_ref[0] = _combine(gates, eg_ref, o_c, o_s, o_w, lq)[:ls]

    keep = w_rows - ls
    tail_lane = lax.broadcasted_iota(I32, (DH, LANES), 1) >= keep % LANES
    for c in range(2):
        new_t = pltpu.roll(new_w[:, c * LANES:(c + 1) * LANES].T, keep % LANES, 1)
        for g in range(KVH):
            shifted = pltpu.roll(win_ref[0, c, g], keep, 1)
            last = jnp.where(tail_lane, new_t[g * DH:(g + 1) * DH], shifted[:, w_rows - LANES:])
            wout_ref[0, c, g] = jnp.concatenate([shifted[:, :w_rows - LANES], last], axis=1)


def _nsa_sample(q, gates, kvs_new, kvw_new, win_t, cmp_t, slc_t, page_table, cw, past_len):
    Bd, ls, _ = q.shape
    n_pages = page_table.shape[1]
    n_cmp = past_len // CMP_STRIDE
    ovt = _overlap_t(n_cmp)
    eg = _gate_expand()
    w_rows = win_t.shape[4]

    per_b = lambda a: pl.BlockSpec((1,) + a.shape[1:], lambda b, pt: (b,) + (0,) * (a.ndim - 1))
    full = lambda a: pl.BlockSpec(a.shape, lambda b, pt: (0,) * a.ndim)
    page = lambda p: pl.BlockSpec((1, 2, KVH, DH, PAGE_SIZE), lambda b, pt, p=p: (pt[b, p], 0, 0, 0, 0))
    grid_spec = pltpu.PrefetchScalarGridSpec(
        num_scalar_prefetch=1,
        grid=(Bd,),
        in_specs=[per_b(q), per_b(gates), per_b(kvs_new), per_b(win_t), per_b(kvw_new)]
                 + [page(p) for p in range(n_pages)] * 2
                 + [full(a) for a in cw] + [full(ovt), full(eg)],
        out_specs=[pl.BlockSpec((1, ls, NSA_WIDTH), lambda b, pt: (b, 0, 0)), per_b(win_t)],
        scratch_shapes=[pltpu.VMEM((past_len, LANES), F32)] * 2
                       + [pltpu.VMEM((n_cmp, LANES), BF16)] * 4
                       + [pltpu.VMEM((2 * LANES, past_len), BF16)] * 2
                       + [pltpu.VMEM((LANES, past_len), BF16)] * 2
                       + [pltpu.VMEM((LANES, w_rows), BF16)] * 2
                       + [pltpu.VMEM((KVH, HPG * BF16_ROWS, 2 * LANES), BF16)],
    )
    return pl.pallas_call(
        functools.partial(_nsa_sample_body, n_pages=n_pages, ls=ls, past_len=past_len),
        grid_spec=grid_spec,
        out_shape=[jax.ShapeDtypeStruct((Bd, ls, NSA_WIDTH), F32), jax.ShapeDtypeStruct(win_t.shape, F32)],
        compiler_params=_params(("arbitrary",)),
        name="nsa_sample",
    )(page_table, q, gates, kvs_new, win_t, kvw_new,
      *([cmp_t] * n_pages), *([slc_t] * n_pages), *cw, ovt, eg)


def _fin1_body(h_ref, pool_ref, nsa_ref, wo_ref, g_ref, b_ref, wq_ref, h1_ref, qm_ref):
    mix = (_dot(pool_ref[...].astype(BF16), wo_ref[0:POOL_WIDTH, :])
           + _dot(nsa_ref[...].astype(BF16), wo_ref[POOL_WIDTH:, :]))
    h1 = _layer_norm(DN_ALPHA * h_ref[...] + mix, g_ref[...], b_ref[...])
    h1_ref[...] = h1
    qm_ref[...] = (_dot(h1.astype(BF16), wq_ref[...]) * (MEM_HEAD_DIM ** -0.5)).astype(qm_ref.dtype)


def _fin1(h, pool_o, nsa_o, w_out_bf, g, b, wq_bf, q_dtype):
    T = h.shape[0]
    tm = ROW_TILE
    row = lambda n: pl.BlockSpec((tm, n), lambda i: (i, 0))
    full = lambda a: pl.BlockSpec(a.shape, lambda i: (0,) * a.ndim)
    return pl.pallas_call(
        _fin1_body,
        grid=(T // tm,),
        in_specs=[row(D_MODEL), row(POOL_WIDTH), row(NSA_WIDTH), full(w_out_bf), full(g), full(b), full(wq_bf)],
        out_specs=[row(D_MODEL), row(D_MODEL)],
        out_shape=[jax.ShapeDtypeStruct((T, D_MODEL), F32), jax.ShapeDtypeStruct((T, D_MODEL), q_dtype)],
        compiler_params=_params(("arbitrary",)),
        name="out_proj_ln1",
    )(h, pool_o, nsa_o, w_out_bf, g, b, wq_bf)


def _memattn_body(q_ref, kv_ref, o_ref):
    width = MEM_HEADS * MEM_HEAD_DIM
    for h in range(MEM_HEADS):
        cols = slice(h * MEM_HEAD_DIM, (h + 1) * MEM_HEAD_DIM)
        qh = q_ref[0, :, cols].astype(BF16)
        kh = kv_ref[0, :, cols].astype(BF16)
        vh = kv_ref[0, :, width + h * MEM_HEAD_DIM:width + (h + 1) * MEM_HEAD_DIM].astype(BF16)
        s = _dot_nt(qh, kh)
        e = jnp.exp(s - jnp.max(s, axis=-1, keepdims=True))
        o = _dot(e.astype(BF16), vh) * (1.0 / jnp.sum(e, axis=-1, keepdims=True))
        o_ref[0, :, cols] = o.astype(o_ref.dtype)


def _memattn(qm, mem_kv, tq):
    nb, L, W = qm.shape
    return pl.pallas_call(
        _memattn_body,
        grid=(nb, L // tq),
        in_specs=[pl.BlockSpec((1, tq, W), lambda b, t: (b, t, 0)),
                  pl.BlockSpec((1, MEM_LEN, 2 * W), lambda b, t: (b, 0, 0))],
        out_specs=pl.BlockSpec((1, tq, W), lambda b, t: (b, t, 0)),
        out_shape=jax.ShapeDtypeStruct((nb, L, W), BF16),
        compiler_params=_params(("arbitrary", "arbitrary")),
        name="mem_attn",
    )(qm, mem_kv)


def _memattn_few_body(q_ref, kv_ref, o_ref):
    lq = q_ref.shape[1]
    n_keys = MEM_LEN * MEM_HEADS
    q = q_ref[0]
    qs = jnp.concatenate([q[:, h * MEM_HEAD_DIM:(h + 1) * MEM_HEAD_DIM] for h in range(MEM_HEADS)], axis=0)
    k = kv_ref[0, :, 0, :, :].reshape(n_keys, MEM_HEAD_DIM).astype(BF16)
    v = kv_ref[0, :, 1, :, :].reshape(n_keys, MEM_HEAD_DIM).astype(BF16)
    s = _dot_nt(qs.astype(BF16), k)
    assert lq & (lq - 1) == 0 and MEM_HEADS & (MEM_HEADS - 1) == 0
    col_h = jnp.bitwise_and(lax.broadcasted_iota(I32, s.shape, 1), MEM_HEADS - 1)
    row_h = jnp.right_shift(lax.broadcasted_iota(I32, s.shape, 0), lq.bit_length() - 1)
    s = jnp.where(col_h == row_h, s, NEG)
    e = jnp.exp(s - jnp.max(s, axis=-1, keepdims=True))
    o = _dot(e.astype(BF16), v) * (1.0 / jnp.sum(e, axis=-1, keepdims=True))
    for h in range(MEM_HEADS):
        o_ref[0, :, h * MEM_HEAD_DIM:(h + 1) * MEM_HEAD_DIM] = o[h * lq:(h + 1) * lq]


def _memattn_few(qm, mem_kv):
    nb, lq, W = qm.shape
    return pl.pallas_call(
        _memattn_few_body,
        grid=(nb,),
        in_specs=[pl.BlockSpec((1, lq, W), lambda b: (b, 0, 0)),
                  pl.BlockSpec((1, MEM_LEN, 2, MEM_HEADS, MEM_HEAD_DIM), lambda b: (b, 0, 0, 0, 0))],
        out_specs=pl.BlockSpec((1, lq, W), lambda b: (b, 0, 0)),
        out_shape=jax.ShapeDtypeStruct((nb, lq, W), F32),
        compiler_params=_params(("arbitrary",)),
        name="mem_attn_few",
    )(qm, mem_kv)


def _fin2_body(cnt0_ref, h1_ref, o_ref, wo_ref, g_ref, b_ref, rw_ref, rb_ref, *rest, n_own):
    h2_ref, te_ref, tg_ref, cnt_ref, run_ref = rest[-5:]
    step = pl.program_id(0)

    @pl.when(step >= n_own)
    def _():
        h2_ref[...] = jnp.zeros(h2_ref.shape, F32)
        te_ref[...] = jnp.zeros(te_ref.shape, I32)
        tg_ref[...] = jnp.zeros(tg_ref.shape, F32)

    @pl.when(step < n_own)
    def _():
        _fin2_rows(cnt0_ref, h1_ref, o_ref, wo_ref, g_ref, b_ref, rw_ref, rb_ref,
                   h2_ref, te_ref, tg_ref, cnt_ref, run_ref)


def _fin2_rows(cnt0_ref, h1_ref, o_ref, wo_ref, g_ref, b_ref, rw_ref, rb_ref,
               h2_ref, te_ref, tg_ref, cnt_ref, run_ref):
    tm = h1_ref.shape[0]

    @pl.when(pl.program_id(0) == 0)
    def _():
        run_ref[...] = cnt0_ref[...]

    a = _dot(o_ref[...].astype(BF16), wo_ref[...])
    h2 = _layer_norm(DN_ALPHA * h1_ref[...] + a, g_ref[...], b_ref[...])
    h2_ref[...] = h2
    a_hi, a_lo, _ = _split3(h2)
    w_hi, w_lo, _ = _split3(rw_ref[...])
    logits = _dot(a_hi, w_hi) + (_dot(a_hi, w_lo) + _dot(a_lo, w_hi)) + rb_ref[...]
    e_iota = lax.broadcasted_iota(I32, (tm, N_EXPERTS), 1).astype(F32)
    lane = lax.broadcasted_iota(I32, (tm, LANES), 1)
    te = jnp.zeros((tm, LANES), F32)
    tv = jnp.full((tm, LANES), NEG, F32)
    work = logits
    chosen = []
    for k in range(TOP_K):
        m = jnp.max(work, axis=-1, keepdims=True)
        idx = jnp.min(jnp.where(work == m, e_iota, float(N_EXPERTS)), axis=-1, keepdims=True)
        hit = e_iota == idx
        chosen.append(hit)
        te = jnp.where(lane == k, idx, te)
        tv = jnp.where(lane == k, m, tv)
        work = jnp.where(hit, -jnp.inf, work)
    member = sum(c.astype(F32) for c in chosen)
    earlier = (lax.broadcasted_iota(I32, (tm, tm), 0) > lax.broadcasted_iota(I32, (tm, tm), 1)).astype(BF16)
    before = _dot(earlier, member.astype(BF16)) + run_ref[...]
    for k in range(TOP_K):
        rank = jnp.sum(jnp.where(chosen[k], before, 0.0), axis=-1, keepdims=True)
        te = jnp.where(lane == TOP_K + k, rank, te)
    run_ref[...] = run_ref[...] + jnp.sum(member, axis=0, keepdims=True)
    cnt_ref[...] = run_ref[...]
    ex = jnp.exp(tv - jnp.max(tv, axis=-1, keepdims=True))
    te_ref[...] = te.astype(I32)
    tg_ref[...] = ex / jnp.sum(ex, axis=-1, keepdims=True)


def _fin2(cnt0, h1, o, wo_bf, g, b, rw, rb, total_rows, row_offset=0, into=None):
    T = h1.shape[0]
    tm = ROW_TILE
    blk0 = row_offset // tm
    n_own = T // tm
    n_steps = n_own if into is not None else (total_rows - row_offset) // tm
    row = lambda n: pl.BlockSpec((tm, n), lambda i: (jnp.minimum(i, n_own - 1), 0))
    out_row = lambda n: pl.BlockSpec((tm, n), lambda i: (i + blk0, 0))
    full = lambda a: pl.BlockSpec(a.shape, lambda i: (0,) * a.ndim)
    ins = [cnt0, h1, o, wo_bf, g, b, rw, rb]
    in_specs = [full(cnt0), row(D_MODEL), row(D_MODEL), full(wo_bf), full(g), full(b), full(rw), full(rb)]
    aliases = {}
    if into is not None:
        aliases = {len(ins) + k: k for k in range(len(into))}
        in_specs = in_specs + [pl.BlockSpec(memory_space=pl.ANY)] * len(into)
        ins = ins + list(into)
    return pl.pallas_call(
        functools.partial(_fin2_body, n_own=n_own),
        grid=(n_steps,),
        in_specs=in_specs,
        out_specs=[out_row(D_MODEL), out_row(LANES), out_row(LANES), full(cnt0)],
        out_shape=[jax.ShapeDtypeStruct((total_rows, D_MODEL), F32), jax.ShapeDtypeStruct((total_rows, LANES), I32),
                   jax.ShapeDtypeStruct((total_rows, LANES), F32), jax.ShapeDtypeStruct(cnt0.shape, F32)],
        scratch_shapes=[pltpu.VMEM(cnt0.shape, F32)],
        input_output_aliases=aliases,
        compiler_params=_params(("arbitrary",)),
        name="mem_out_ln2_router",
    )(*ins)


def _moe_body(ut_ref, ue_ref, nu_ref, rs_ref, re_ref, x_ref, wgu_ref, bgu_ref, wdn_ref, bdn_ref,
              y_ref, wgu_bf, wdn_bf):
    u = pl.program_id(0)
    bk = x_ref.shape[0]
    e = ue_ref[u]
    tile = ut_ref[u]
    prev = jnp.maximum(u - 1, 0)

    @pl.when((u == 0) | (e != ue_ref[prev]))
    def _():
        wgu_bf[...] = wgu_ref[0].astype(BF16)
        wdn_bf[...] = wdn_ref[0].astype(BF16)

    @pl.when(u < nu_ref[0])
    def _():
        x = x_ref[...].astype(BF16)
        g = _dot(x, wgu_bf[:, :D_FF]) + bgu_ref[0, :, :D_FF]
        v = _dot(x, wgu_bf[:, D_FF:]) + bgu_ref[0, :, D_FF:]
        g = jnp.minimum(g, SWIGLU_LIMIT)
        v = jnp.clip(v, -SWIGLU_LIMIT, SWIGLU_LIMIT)
        a = g * (1.0 / (1.0 + jnp.exp(-SWIGLU_ALPHA * g))) * (v + 1.0)
        y = _dot(a.astype(BF16), wdn_bf[...]) + bdn_ref[0]
        row = tile * bk + lax.broadcasted_iota(I32, (bk, 1), 0)
        mine = (row >= rs_ref[e]) & (row < re_ref[e])
        y = jnp.where(mine, y, 0.0)

        @pl.when((u == 0) | (tile != ut_ref[prev]))
        def _():
            y_ref[...] = y

        @pl.when((u > 0) & (tile == ut_ref[prev]))
        def _():
            y_ref[...] = y_ref[...] + y


def _moe_gmm(x_rows, units, w_gu, b_gu, w_dn, b_dn):
    N = x_rows.shape[0]
    bk = MOE_ROWS
    unit_tile, unit_e, n_units, r_start, r_end = units
    grid_spec = pltpu.PrefetchScalarGridSpec(
        num_scalar_prefetch=5,
        grid=(unit_tile.shape[0],),
        in_specs=[pl.BlockSpec((bk, D_MODEL), lambda u, ut, ue, *_: (ut[u], 0)),
                  pl.BlockSpec((1, D_MODEL, 2 * D_FF), lambda u, ut, ue, *_: (ue[u], 0, 0)),
                  pl.BlockSpec((1, 1, 2 * D_FF), lambda u, ut, ue, *_: (ue[u], 0, 0)),
                  pl.BlockSpec((1, D_FF, D_MODEL), lambda u, ut, ue, *_: (ue[u], 0, 0)),
                  pl.BlockSpec((1, 1, D_MODEL), lambda u, ut, ue, *_: (ue[u], 0, 0))],
        out_specs=pl.BlockSpec((bk, D_MODEL), lambda u, ut, ue, *_: (ut[u], 0)),
        scratch_shapes=[pltpu.VMEM((D_MODEL, 2 * D_FF), BF16), pltpu.VMEM((D_FF, D_MODEL), BF16)],
    )
    return pl.pallas_call(
        _moe_body,
        grid_spec=grid_spec,
        out_shape=jax.ShapeDtypeStruct((N, D_MODEL), F32),
        compiler_params=_params(("arbitrary",)),
        name="moe_experts",
    )(unit_tile, unit_e, n_units, r_start, r_end, x_rows, w_gu, b_gu, w_dn, b_dn)


FLAT_BITS = 17


def _moe_routing(te, counts):
    bk = MOE_ROWS
    T = te.shape[0]
    N = T * TOP_K
    assert N % bk == 0 and N <= (1 << FLAT_BITS)
    experts = jnp.arange(N_EXPERTS, dtype=I32)
    top_e = te[:, :TOP_K]
    r_end = jnp.cumsum(counts).astype(I32)
    r_start = r_end - counts
    onehot = top_e[:, :, None] == experts[None, None, :]
    pos = jnp.sum(jnp.where(onehot, r_start[None, None, :], 0), axis=-1) + te[:, TOP_K:2 * TOP_K]
    key = jnp.left_shift(top_e.reshape(-1), FLAT_BITS) + jnp.arange(N, dtype=I32)
    key_s = lax.sort(key)
    tok_s = jnp.right_shift(jnp.bitwise_and(key_s, (1 << FLAT_BITS) - 1), 2)
    first = r_start // bk
    last = (r_end - 1) // bk
    n_e = jnp.where(counts > 0, last - first + 1, 0)
    u_end = jnp.cumsum(n_e).astype(I32)
    u_start = u_end - n_e
    n_units = u_end[-1]
    u = jnp.minimum(jnp.arange(N // bk + N_EXPERTS - 1, dtype=I32), n_units - 1)
    unit_e = jnp.sum((u[:, None] >= u_end[None, :]).astype(I32), axis=1)
    unit_tile = u + jnp.sum(jnp.where(unit_e[:, None] == experts[None, :], (first - u_start)[None, :], 0), axis=1)
    return pos, tok_s, (unit_tile, unit_e, n_units.reshape(1), r_start, r_end)


def _fin3_body(h2_ref, tg_ref, y0_ref, y1_ref, y2_ref, y3_ref, g_ref, b_ref, o_ref):
    gate = lambda k: tg_ref[:, k:k + 1]
    y = ((gate(0) * y0_ref[...] + gate(1) * y1_ref[...])
         + (gate(2) * y2_ref[...] + gate(3) * y3_ref[...]))
    o_ref[...] = _layer_norm(DN_ALPHA * h2_ref[...] + y, g_ref[...], b_ref[...])


def _fin3(h2, tg, ys, g, b, row_offset, T):
    tm = ROW_TILE
    blk0 = row_offset // tm
    row = lambda n: pl.BlockSpec((tm, n), lambda i: (i, 0))
    full = lambda a: pl.BlockSpec(a.shape, lambda i: (0,) * a.ndim)
    return pl.pallas_call(
        _fin3_body,
        grid=(T // tm,),
        in_specs=[pl.BlockSpec((tm, D_MODEL), lambda i: (i + blk0, 0)), pl.BlockSpec((tm, LANES), lambda i: (i + blk0, 0))]
                 + [pl.BlockSpec((tm, D_MODEL), lambda i: (i + blk0, 0))] * TOP_K + [full(g), full(b)],
        out_specs=row(D_MODEL),
        out_shape=jax.ShapeDtypeStruct((T, D_MODEL), F32),
        compiler_params=_params(("arbitrary",)),
        name="combine_ln3",
    )(h2, tg, *ys, g, b)


def kernel(x_prompt, x_sample, cache_cmp_kv, cache_slc_kv, state_win_kv, state_pool, cache_mem_kv, page_table,
           mem_prompt, w_in, pool_w, pool_scale, cmp_pe, cmp_w1, cmp_b1, cmp_w2, cmp_b2, w_out, ln1_g, ln1_b,
           mem_wq, mem_wkv, mem_wo, ln2_g, ln2_b, router_w, router_b, exp_w_gu, exp_b_gu, exp_w_dn, exp_b_dn,
           ln3_g, ln3_b):
    Bp, S, D = x_prompt.shape
    Bd, Ls, _ = x_sample.shape
    Tp, Ts = Bp * S, Bd * Ls
    l = 0
    w_in_bf = w_in[l].astype(BF16)
    pool_w_bf = pool_w[l].astype(BF16)
    ps = pool_scale[l][None, :]
    cw = _compress_weights(cmp_pe[l], cmp_w1[l], cmp_b1[l], cmp_w2[l], cmp_b2[l])
    w_out_bf = w_out[l].astype(BF16)
    wq_bf = mem_wq[l].astype(BF16)
    wo_bf = mem_wo[l].astype(BF16)
    vec = lambda a: a[l][None, :]

    up, qp, kvc_p, kvs_p, kvw_p, gp, pool_p, kvc_t, kvs_t = _inproj_prompt(
        x_prompt.reshape(Tp, D), w_in_bf, pool_w_bf, ps, S)
    kc_p = _compress_prompt(kvc_p.reshape(Bp, S, KV_WIDTH), cw)
    nsa_p = _nsa_prompt(qp, gp, kc_p, kvs_p.reshape(Bp, S, KV_WIDTH), kvw_p.reshape(Bp, S, KV_WIDTH))
    mem_kv_p = _matmul(mem_prompt.reshape(Bp * MEM_LEN, D), mem_wkv[l]).reshape(Bp, MEM_LEN, 2 * D)
    h1_p, qm_p = _fin1(x_prompt.reshape(Tp, D), pool_p, nsa_p, w_out_bf, vec(ln1_g), vec(ln1_b), wq_bf, BF16)
    om_p = _memattn(qm_p.reshape(Bp, S, D), mem_kv_p, ROW_TILE).reshape(Tp, D)
    T = Tp + Ts
    *routed_p, cnt_p = _fin2(jnp.zeros((1, N_EXPERTS), F32), h1_p, om_p, wo_bf, vec(ln2_g), vec(ln2_b),
                             router_w[l], vec(router_b), total_rows=T)

    state_pad = jnp.pad(state_pool[l], ((0, 0), (1, 0), (0, 0)))
    us, qs, kvc_s, kvs_s, kvw_s, gs, pool_s = _inproj_sample(
        x_sample.reshape(Ts, D), w_in_bf, pool_w_bf, ps, state_pad, Ls, PAST_LEN)
    feature_major = lambda a: jnp.transpose(a, (0, 2, 3, 4, 1))
    nsa_s, win_next = _nsa_sample(qs.reshape(Bd, Ls, NSA_WIDTH), gs.reshape(Bd, Ls, 3 * NSA_HEADS),
                        kvs_s.reshape(Bd, Ls, KV_WIDTH), kvw_s.reshape(Bd, Ls, KV_WIDTH),
                        feature_major(state_win_kv[l]), feature_major(cache_cmp_kv[l]),
                        feature_major(cache_slc_kv[l]), page_table, cw, PAST_LEN)
    h1_s, qm_s = _fin1(x_sample.reshape(Ts, D), pool_s, nsa_s.reshape(Ts, NSA_WIDTH), w_out_bf,
                       vec(ln1_g), vec(ln1_b), wq_bf, F32)
    om_s = _memattn_few(qm_s.reshape(Bd, Ls, D), cache_mem_kv[l]).reshape(Ts, D)
    h2, te, tg, cnt_s = _fin2(cnt_p, h1_s, om_s, wo_bf, vec(ln2_g), vec(ln2_b), router_w[l], vec(router_b),
                              total_rows=T, row_offset=Tp, into=routed_p)

    pos, tok_s, units = _moe_routing(te, cnt_s[0].astype(I32))
    x_rows = jnp.zeros((T * TOP_K, D), F32)
    for k in range(TOP_K):
        x_rows = x_rows.at[pos[:, k]].set(h2, unique_indices=True)
    y_rows = _moe_gmm(x_rows, units, exp_w_gu[l], exp_b_gu[l][:, None, :],
                      exp_w_dn[l], exp_b_dn[l][:, None, :])
    ys = [y_rows[pos[:, k]] for k in range(TOP_K)]
    y_prompt = _fin3(h2, tg, ys, vec(ln3_g), vec(ln3_b), 0, Tp).reshape(Bp, S, D)
    y_sample = _fin3(h2, tg, ys, vec(ln3_g), vec(ln3_b), Tp, Ts).reshape(Bd, Ls, D)

    kv6 = lambda a, b, n: a.reshape(1, b, n, 2, KVH, DH)
    row_major = lambda a: jnp.transpose(a, (0, 4, 1, 2, 3))
    win_p = kvw_p.reshape(Bp, S, KV_WIDTH)[:, S - min(WINDOW, S):]
    pool_state_p = up.reshape(Bp, S, POOL_WIDTH)[:, S - POOL_STATE:]
    pool_state_s = jnp.concatenate([state_pool[l], us.reshape(Bd, Ls, POOL_WIDTH)], axis=1)[:, -POOL_STATE:]
    return (y_prompt, y_sample,
            row_major(kvc_t)[None], row_major(kvs_t)[None], kv6(win_p, Bp, min(WINDOW, S)),
            pool_state_p[None], mem_kv_p.reshape(1, Bp, MEM_LEN, 2, MEM_HEADS, MEM_HEAD_DIM),
            kv6(kvc_s, Bd, Ls), kv6(kvs_s, Bd, Ls), row_major(win_next)[None], pool_state_s[None])
```

```python
import functools

import jax
import jax.numpy as jnp
from jax import lax
from jax.experimental import pallas as pl
from jax.experimental.pallas import tpu as pltpu

F32 = jnp.float32
BF16 = jnp.bfloat16
I32 = jnp.int32

D_MODEL = 1024
POOL_WIDTH = 512
POOL_WINDOWS = (2, 4, 8, 16)
POOL_GROUP = 128
POOL_STATE = 15
NSA_WIDTH = 512
DH = 64
NSA_HEADS = 8
KVH = 2
HPG = 4
CMP_LEN = 32
CMP_STRIDE = 16
CMP_HIDDEN = 256
SEL_LEN = 64
SEL_TOP = 16
WINDOW = 512
Q_BLOCK = 128
KV_WIDTH = 256
ATTN_SCALE = DH ** -0.5
FORCED_SCORE = 1e4
NEG = -1e30
MEM_LEN = 256
MEM_HEADS = 4
MEM_HEAD_DIM = 256
N_EXPERTS = 32
TOP_K = 4
D_FF = 1024
SWIGLU_LIMIT = 7.0
SWIGLU_ALPHA = 1.702
DN_ALPHA = 2.0 ** 0.25
LN_EPS = 1e-5
PAST_LEN = 2048
PAGE_SIZE = 128

LANES = 128
SEL_PAD = 128
KEY_TILE = 512
ROW_TILE = 512
MOE_ROWS = 256
MEM_PER_STEP = 4
BF16_ROWS = 16
VMEM_LIMIT = 56 * 1024 * 1024

HIGHEST = lax.Precision.HIGHEST


def _dot(a, b):
    return jnp.dot(a, b, preferred_element_type=F32)


def _dot_nt(a, b, precision=None):
    return lax.dot_general(a, b, (((1,), (1,)), ((), ())), preferred_element_type=F32,
                           precision=precision)


def _layer_norm(x, g, b):
    mu = jnp.mean(x, axis=-1, keepdims=True)
    xc = x - mu
    var = jnp.mean(xc * xc, axis=-1, keepdims=True)
    return xc * lax.rsqrt(var + LN_EPS) * g + b


def _params(sem, vmem=VMEM_LIMIT):
    return pltpu.CompilerParams(dimension_semantics=sem, vmem_limit_bytes=vmem)


def _split_store(u, up_ref, q_ref, kvc_ref, kvs_ref, kvw_ref, gate_ref):
    o1 = POOL_WIDTH
    o2 = o1 + NSA_WIDTH
    o3 = o2 + KV_WIDTH
    o4 = o3 + KV_WIDTH
    o5 = o4 + KV_WIDTH
    up_ref[...] = u[:, :o1]
    q_ref[...] = u[:, o1:o2]
    kvc_ref[...] = u[:, o2:o3]
    kvs_ref[...] = u[:, o3:o4]
    kvw_ref[...] = u[:, o4:o5]
    gate_ref[...] = 1.0 / (1.0 + jnp.exp(-u[:, o5:]))


def _inproj_prompt_body(x_ref, w_ref, pw_ref, ps_ref,
                        up_ref, q_ref, kvc_ref, kvs_ref, kvw_ref, gate_ref, pool_ref, kvct_ref, kvst_ref,
                        ext_ref, *, tm, tiles_per_seq):
    halo = POOL_STATE + 1
    t_in_seq = pl.program_id(0) % tiles_per_seq
    u = _dot(x_ref[...].astype(BF16), w_ref[...])
    _split_store(u, up_ref, q_ref, kvc_ref, kvs_ref, kvw_ref, gate_ref)
    o2 = POOL_WIDTH + NSA_WIDTH
    kvct_ref[0] = u[:, o2:o2 + KV_WIDTH].T.reshape(2, KVH, DH, tm)
    kvst_ref[0] = u[:, o2 + KV_WIDTH:o2 + 2 * KV_WIDTH].T.reshape(2, KVH, DH, tm)

    @pl.when(t_in_seq == 0)
    def _():
        ext_ref[0:halo, :] = jnp.zeros((halo, POOL_WIDTH), F32)

    ext_ref[halo:halo + tm, :] = u[:, :POOL_WIDTH]
    pos = t_in_seq * tm + lax.broadcasted_iota(I32, (tm, 1), 0)
    for gi, w in enumerate(POOL_WINDOWS):
        cols = slice(gi * POOL_GROUP, (gi + 1) * POOL_GROUP)
        acc = ext_ref[halo:halo + tm, cols]
        for k in range(1, w):
            acc = acc + ext_ref[halo - k:halo - k + tm, cols]
        cnt = jnp.minimum(pos + 1, w).astype(F32)
        d = acc / cnt - ext_ref[halo:halo + tm, cols]
        o = _dot(d.astype(BF16), pw_ref[gi])
        pool_ref[:, cols] = (o * ps_ref[:, cols]).astype(pool_ref.dtype)
    ext_ref[0:halo, :] = ext_ref[tm:tm + halo, :]


def _inproj_prompt(x2d, w_in_bf, pool_w_bf, pool_scale, seq_len):
    T = x2d.shape[0]
    tm = ROW_TILE
    outs = [POOL_WIDTH, NSA_WIDTH, KV_WIDTH, KV_WIDTH, KV_WIDTH, 3 * NSA_HEADS, POOL_WIDTH]
    row = lambda n: pl.BlockSpec((tm, n), lambda i: (i, 0))
    full = lambda a: pl.BlockSpec(a.shape, lambda i: (0,) * a.ndim)
    tps = seq_len // tm
    kvt_spec = pl.BlockSpec((1, 2, KVH, DH, tm), lambda i: (i // tps, 0, 0, 0, i % tps))
    kvt_shape = jax.ShapeDtypeStruct((T // seq_len, 2, KVH, DH, seq_len), F32)
    return pl.pallas_call(
        functools.partial(_inproj_prompt_body, tm=tm, tiles_per_seq=tps),
        grid=(T // tm,),
        in_specs=[row(D_MODEL), full(w_in_bf), full(pool_w_bf), full(pool_scale)],
        out_specs=[row(n) for n in outs] + [kvt_spec] * 2,
        out_shape=[jax.ShapeDtypeStruct((T, n), F32) for n in outs[:-1]]
                  + [jax.ShapeDtypeStruct((T, outs[-1]), BF16)] + [kvt_shape] * 2,
        scratch_shapes=[pltpu.VMEM((tm + POOL_STATE + 1, POOL_WIDTH), F32)],
        compiler_params=_params(("arbitrary",)),
        name="inproj_prompt",
    )(x2d, w_in_bf, pool_w_bf, pool_scale)


def _inproj_sample_body(x_ref, w_ref, pw_ref, ps_ref, st_ref,
                        up_ref, q_ref, kvc_ref, kvs_ref, kvw_ref, gate_ref, pool_ref,
                        ext_ref, *, nb, ls, pos0):
    halo = POOL_STATE + 1
    tm = nb * ls
    u = _dot(x_ref[...].astype(BF16), w_ref[...])
    _split_store(u, up_ref, q_ref, kvc_ref, kvs_ref, kvw_ref, gate_ref)
    ext_ref[:, 0:halo, :] = st_ref[...]
    ext_ref[:, halo:halo + ls, :] = u[:, :POOL_WIDTH].reshape(nb, ls, POOL_WIDTH)
    pos = pos0 + lax.broadcasted_iota(I32, (1, ls, 1), 1)
    for gi, w in enumerate(POOL_WINDOWS):
        cols = slice(gi * POOL_GROUP, (gi + 1) * POOL_GROUP)
        acc = ext_ref[:, halo:halo + ls, cols]
        for k in range(1, w):
            acc = acc + ext_ref[:, halo - k:halo - k + ls, cols]
        cnt = jnp.minimum(pos + 1, w).astype(F32)
        d = acc / cnt - ext_ref[:, halo:halo + ls, cols]
        o = _dot(d.reshape(tm, POOL_GROUP).astype(BF16), pw_ref[gi])
        pool_ref[:, cols] = (o * ps_ref[:, cols]).astype(pool_ref.dtype)


def _inproj_sample(x2d, w_in_bf, pool_w_bf, pool_scale, state_pad, ls, pos0):
    T = x2d.shape[0]
    nb = ROW_TILE // ls
    tm = nb * ls
    outs = [POOL_WIDTH, NSA_WIDTH, KV_WIDTH, KV_WIDTH, KV_WIDTH, 3 * NSA_HEADS, POOL_WIDTH]
    row = lambda n: pl.BlockSpec((tm, n), lambda i: (i, 0))
    full = lambda a: pl.BlockSpec(a.shape, lambda i: (0,) * a.ndim)
    return pl.pallas_call(
        functools.partial(_inproj_sample_body, nb=nb, ls=ls, pos0=pos0),
        grid=(T // tm,),
        in_specs=[row(D_MODEL), full(w_in_bf), full(pool_w_bf), full(pool_scale),
                  pl.BlockSpec((nb, POOL_STATE + 1, POOL_WIDTH), lambda i: (i, 0, 0))],
        out_specs=[row(n) for n in outs],
        out_shape=[jax.ShapeDtypeStruct((T, n), F32) for n in outs],
        scratch_shapes=[pltpu.VMEM((nb, POOL_STATE + 1 + ls, POOL_WIDTH), F32)],
        compiler_params=_params(("arbitrary",)),
        name="inproj_sample",
    )(x2d, w_in_bf, pool_w_bf, pool_scale, state_pad)


def _matmul_body(x_ref, w_ref, o_ref):
    o_ref[...] = _dot(x_ref[...].astype(BF16), w_ref[...].astype(BF16))


def _matmul(x, w, tn=512):
    M, K = x.shape
    N = w.shape[1]
    return pl.pallas_call(
        _matmul_body,
        grid=(N // tn,),
        in_specs=[pl.BlockSpec((M, K), lambda j: (0, 0)), pl.BlockSpec((K, tn), lambda j: (0, j))],
        out_specs=pl.BlockSpec((M, tn), lambda j: (0, j)),
        out_shape=jax.ShapeDtypeStruct((M, N), F32),
        compiler_params=_params(("arbitrary",)),
        name="mem_kv_proj",
    )(x, w)


def _gelu_tanh(x):
    c = 0.7978845608028654
    return 0.5 * x * (1.0 + jnp.tanh(c * (x + 0.044715 * (x * x * x))))


def _compress(kv_refs, n_chunks, w1_ref, pe_ref, b1_ref, w2_ref, b2_ref):
    lo = _lane_iota(n_chunks) < DH
    quads = CMP_STRIDE // 4
    outs = []
    for c in range(2):
        acc_a = jnp.zeros((2 * n_chunks, CMP_HIDDEN), F32)
        acc_b = jnp.zeros((2 * n_chunks, CMP_HIDDEN), F32)
        for i in range(quads):
            x = [kv_refs[c][pl.ds(4 * i + m, n_chunks, stride=CMP_STRIDE), :] for m in range(4)]
            r = [pltpu.roll(v, DH, 1) for v in x]
            x_g0 = jnp.concatenate([jnp.where(lo, x[0], r[1]), jnp.where(lo, x[2], r[3])], axis=1)
            x_g1 = jnp.concatenate([jnp.where(lo, r[0], x[1]), jnp.where(lo, r[2], x[3])], axis=1)
            xq = jnp.concatenate([x_g0, x_g1], axis=0)
            acc_a = acc_a + _dot((xq + pe_ref[c, i:i + 1, :]).astype(BF16), w1_ref[c, i])
            acc_b = acc_b + _dot((xq + pe_ref[c, quads + i:quads + i + 1, :]).astype(BF16), w1_ref[c, quads + i])
        hid = acc_a + pltpu.roll(acc_b, 2 * n_chunks - 1, 0) + b1_ref[c]
        outs.append(_dot(_gelu_tanh(hid).astype(BF16), w2_ref[c]) + b2_ref[c])
    return outs


def _compress_prompt_body(kv_ref, w1_ref, pe_ref, b1_ref, w2_ref, b2_ref, o_ref, k_ref, v_ref, *, n_chunks):
    k_ref[...] = kv_ref[0, :, :LANES]
    v_ref[...] = kv_ref[0, :, LANES:]
    keys, values = _compress((k_ref, v_ref), n_chunks, w1_ref, pe_ref, b1_ref, w2_ref, b2_ref)
    o_ref[0, 0] = keys
    o_ref[0, 1] = values


def _compress_prompt(kvc, cw):
    B, S, _ = kvc.shape
    n_chunks = S // CMP_STRIDE
    full = lambda a: pl.BlockSpec(a.shape, lambda b: (0,) * a.ndim)
    return pl.pallas_call(
        functools.partial(_compress_prompt_body, n_chunks=n_chunks),
        grid=(B,),
        in_specs=[pl.BlockSpec((1, S, KV_WIDTH), lambda b: (b, 0, 0))] + [full(a) for a in cw],
        out_specs=pl.BlockSpec((1, 2, KVH * n_chunks, LANES), lambda b: (b, 0, 0, 0)),
        out_shape=jax.ShapeDtypeStruct((B, 2, KVH * n_chunks, LANES), F32),
        scratch_shapes=[pltpu.VMEM((S, LANES), F32)] * 2,
        compiler_params=_params(("arbitrary",)),
        name="compress_prompt",
    )(kvc, *cw)


def _compress_weights(cmp_pe, cmp_w1, cmp_b1, cmp_w2, cmp_b2):
    nq = CMP_LEN // 4
    w1 = cmp_w1.reshape(2, nq, 4 * DH, CMP_HIDDEN).astype(BF16)
    pe = cmp_pe.reshape(2, nq, 4 * DH)
    b1 = cmp_b1[:, None, :]
    w2 = jnp.stack([jnp.concatenate([cmp_w2[0], jnp.zeros_like(cmp_w2[0])], axis=1),
                    jnp.concatenate([cmp_w2[1], cmp_w2[1]], axis=1)]).astype(BF16)
    b2 = jnp.stack([jnp.concatenate([cmp_b2[0], jnp.zeros_like(cmp_b2[0])]),
                    jnp.concatenate([cmp_b2[1], cmp_b2[1]])])[:, None, :]
    return w1, pe, b1, w2, b2


def _softmax_rows(s):
    e = jnp.exp(s - jnp.max(s, axis=-1, keepdims=True))
    return e * (1.0 / jnp.sum(e, axis=-1, keepdims=True))


def _lane_iota(n):
    return lax.broadcasted_iota(I32, (n, LANES), 1)


def _key_alibi_cols(pos, lane):
    hi = jnp.left_shift(jnp.right_shift(pos, 6), 6).astype(F32)
    lo = jnp.bitwise_and(pos, SEL_LEN - 1).astype(F32)
    return jnp.where(lane == DH, hi,
                     jnp.where(lane == DH + 1, lo,
                               jnp.where((lane == DH + 2) | (lane == DH + 3), 1.0, 0.0)))


def _key_alibi_rows(pos, sub):
    hi = jnp.left_shift(jnp.right_shift(pos, 6), 6).astype(F32)
    lo = jnp.bitwise_and(pos, SEL_LEN - 1).astype(F32)
    return jnp.where(sub == 0, hi, jnp.where(sub == 1, lo, jnp.where((sub == 2) | (sub == 3), 1.0, 0.0)))


def _halves(x, zero_hi=False):
    lo = _lane_iota(x.shape[0]) < DH
    r = pltpu.roll(x, DH, 1)
    if zero_hi:
        return jnp.where(lo, x, 0.0), jnp.where(lo, r, 0.0)
    return jnp.where(lo, x, r), jnp.where(lo, r, x)


def _fill_queries(qa_ref, qs, q_pos, lq):
    lane = _lane_iota(lq)
    lo_half = lane < DH
    q_hi = jnp.left_shift(jnp.right_shift(q_pos, 7), 7).astype(F32)
    q_lo = jnp.bitwise_and(q_pos, LANES - 1).astype(F32)
    for h in range(NSA_HEADS):
        g, hl = divmod(h, HPG)
        slope = 2.0 ** (-(h + 1))
        slab = qs[:, (h // 2) * LANES:(h // 2 + 1) * LANES]
        if h % 2:
            slab = pltpu.roll(slab, DH, 1)
        ex = jnp.where((lane == DH) | (lane == DH + 1), slope,
                       jnp.where(lane == DH + 2, -slope * q_hi,
                                 jnp.where(lane == DH + 3, -slope * q_lo,
                                           jnp.where(lane == DH + 4, NEG, 0.0))))
        qa_ref[g, hl * lq:(hl + 1) * lq, 0:LANES] = jnp.where(lo_half, slab, ex).astype(BF16)


def _stack4(x):
    return jnp.concatenate([x] * HPG, axis=0)


def _cmp_branch(qa_ref, kc_k, kc_v, q_pos4, lq, n_cmp):
    c_end = lax.broadcasted_iota(I32, (1, n_cmp), 1) * CMP_STRIDE + (CMP_LEN - 1)
    m_c = c_end <= q_pos4
    any_c = (q_pos4 >= CMP_LEN - 1).astype(F32)
    outs, psums = [], []
    for g in range(KVH):
        s = jnp.where(m_c, _dot_nt(qa_ref[g, :, 0:LANES], kc_k[g][...]), NEG)
        p = _softmax_rows(s) * any_c
        outs.append(_dot(p.astype(BF16), kc_v[g][...]))
        psums.append(p[0:lq] + p[lq:2 * lq] + p[2 * lq:3 * lq] + p[3 * lq:4 * lq])
    return outs, psums


def _split3(x):
    hi = x.astype(BF16)
    r1 = x - hi.astype(F32)
    mid = r1.astype(BF16)
    lo = (r1 - mid.astype(F32)).astype(BF16)
    return hi, mid, lo


def _top_blocks_t(imp_ts, pos0, n_blk=SEL_PAD):
    blk = lax.broadcasted_iota(I32, (n_blk, LANES), 0)
    qp_t = pos0 + lax.broadcasted_iota(I32, (n_blk, LANES), 1)
    cur = jnp.right_shift(qp_t, 6)
    forced = (blk == 0) | (blk == cur) | (blk == cur - 1)
    valid = jnp.left_shift(blk, 6) <= qp_t
    v = jnp.concatenate([jnp.where(valid, jnp.where(forced, FORCED_SCORE, t), -1.0) for t in imp_ts], axis=1)
    blk_f = lax.broadcasted_iota(I32, (n_blk, KVH * LANES), 0).astype(F32)
    sel = jnp.zeros((n_blk, KVH * LANES), F32)
    for _ in range(SEL_TOP):
        m = jnp.max(v, axis=0, keepdims=True)
        idx = jnp.min(jnp.where(v == m, blk_f, float(n_blk)), axis=0, keepdims=True)
        hit = blk_f == idx
        sel = jnp.where(hit, 1.0, sel)
        v = jnp.where(hit, -jnp.inf, v)
    out = [jnp.where((sel[:, g * LANES:(g + 1) * LANES] > 0.5) & valid, 0.0, NEG) for g in range(KVH)]
    if n_blk < SEL_PAD:
        out = [jnp.concatenate([b, jnp.full((SEL_PAD - n_blk, LANES), NEG, F32)], axis=0) for b in out]
    return out


def _select_blocks(psums, ovt_ref, pos0, lq, n_blk):
    imp_ts = []
    for g in range(KVH):
        ps = psums[g]
        if lq < LANES:
            ps = jnp.concatenate([ps, jnp.zeros((LANES - lq, ps.shape[1]), F32)], axis=0)
        imp_ts.append(_dot_nt(ovt_ref[0:n_blk, :], ps, precision=HIGHEST))
    return [b.T[:lq].astype(BF16) for b in _top_blocks_t(imp_ts, pos0, n_blk)]


def _store_selbias(qa_ref, selbias, lq):
    for g in range(KVH):
        for hl in range(HPG):
            qa_ref[g, hl * lq:(hl + 1) * lq, LANES:2 * LANES] = selbias[g]


def _combine(gates, eg_ref, o_c, o_s, o_w, lq):
    lo_half = _lane_iota(lq) < DH

    def assemble(per_group):
        slabs = []
        for k in range(NSA_HEADS // 2):
            g, hl = divmod(2 * k, HPG)
            a = per_group[g][hl * lq:(hl + 1) * lq]
            b = per_group[g][(hl + 1) * lq:(hl + 2) * lq]
            slabs.append(jnp.where(lo_half, a, b))
        return jnp.concatenate(slabs, axis=1)

    return (_dot(gates, eg_ref[0]) * assemble(o_c)
            + _dot(gates, eg_ref[1]) * assemble(o_s)
            + _dot(gates, eg_ref[2]) * assemble(o_w))


def _gate_expand():
    r = jnp.arange(3 * NSA_HEADS)
    c = jnp.arange(NSA_WIDTH)
    return jnp.stack([(r[:, None] == 3 * (c[None, :] // DH) + br).astype(F32) for br in range(3)])


def _overlap_t(n_cmp):
    n = jnp.arange(n_cmp)
    s = jnp.arange(SEL_PAD)
    c_first = n * CMP_STRIDE
    c_end = c_first + CMP_LEN - 1
    b_first = s * SEL_LEN
    return ((c_first[None, :] < b_first[:, None] + SEL_LEN) & (c_end[None, :] >= b_first[:, None])).astype(F32)


def _key_rows(kv_f32, pos, invalid=None):
    n = kv_f32.shape[0]
    lane = _lane_iota(n)
    ex = _key_alibi_cols(pos, lane)
    if invalid is not None:
        ex = jnp.where((lane == DH + 4) & invalid, 1.0, ex)
    lo = lane < DH
    return jnp.where(lo, kv_f32, ex), jnp.where(lo, pltpu.roll(kv_f32, DH, 1), ex)


def _block_onehot(pos, n):
    return (lax.broadcasted_iota(I32, (n, SEL_PAD), 1) == jnp.right_shift(pos, 6)).astype(BF16)


def _tile4(x):
    return jnp.concatenate([x] * HPG, axis=1)


def _nsa_prompt_body(q_ref, gate_ref, kc_ref, kvs_ref, w0_ref, w1_ref, w2_ref, w3_ref, w4_ref,
                     ovt_ref, egt_ref, wband_ref, o_ref,
                     kck0, kck1, kcv0, kcv1, ka0, ka1, vt0, vt1, qa0, qa1,
                     sa_ref, sb_ref, m_ref, l_ref, acc_ref, oc_ref, *, seq_len):
    j = pl.program_id(1)
    kc_k, kc_vt, kaug, v_t, qa_t = (kck0, kck1), (kcv0, kcv1), (ka0, ka1), (vt0, vt1), (qa0, qa1)
    n_cmp = kc_ref.shape[2] // KVH
    lq = Q_BLOCK
    cols = HPG * lq

    @pl.when(j == 0)
    def _():
        for g in range(KVH):
            kc_k[g][...] = kc_ref[0, 0, g * n_cmp:(g + 1) * n_cmp, :].astype(BF16)
            kc_vt[g][...] = kc_ref[0, 1, g * n_cmp:(g + 1) * n_cmp, :].T[0:DH].astype(BF16)

        def build(i, _):
            r0 = pl.multiple_of(i * KEY_TILE, KEY_TILE)
            pos = r0 + lax.broadcasted_iota(I32, (KEY_TILE, 1), 0)
            k0, k1 = _key_rows(kvs_ref[0, pl.ds(r0, KEY_TILE), :LANES], pos)
            onehot = _block_onehot(pos, KEY_TILE)
            ka0[pl.ds(r0, KEY_TILE), :] = jnp.concatenate([k0.astype(BF16), onehot], axis=1)
            ka1[pl.ds(r0, KEY_TILE), :] = jnp.concatenate([k1.astype(BF16), onehot], axis=1)
            vt = kvs_ref[0, pl.ds(r0, KEY_TILE), LANES:].T.astype(BF16)
            vt0[:, pl.ds(r0, KEY_TILE)] = vt[0:DH]
            vt1[:, pl.ds(r0, KEY_TILE)] = vt[DH:]
            return 0

        lax.fori_loop(0, seq_len // KEY_TILE, build, 0)

    st = j * Q_BLOCK
    q_pos = st + lax.broadcasted_iota(I32, (1, lq), 1)
    q_pos4 = _tile4(q_pos)

    q_t = (q_ref[...] * ATTN_SCALE).T
    sub = lax.broadcasted_iota(I32, (DH, lq), 0)
    q_hi = jnp.left_shift(jnp.right_shift(q_pos, 7), 7).astype(F32)
    q_lo = jnp.bitwise_and(q_pos, LANES - 1).astype(F32)
    for h in range(NSA_HEADS):
        g, hl = divmod(h, HPG)
        slope = 2.0 ** (-(h + 1))
        ex = jnp.where(sub <= 1, slope,
                       jnp.where(sub == 2, -slope * q_hi,
                                 jnp.where(sub == 3, -slope * q_lo, jnp.where(sub == 4, NEG, 0.0))))
        qa_t[g][0:DH, hl * lq:(hl + 1) * lq] = q_t[h * DH:(h + 1) * DH].astype(BF16)
        qa_t[g][DH:2 * DH, hl * lq:(hl + 1) * lq] = ex.astype(BF16)

    any_c = (q_pos4 >= CMP_LEN - 1).astype(F32)
    blocks_per_sel = SEL_LEN // CMP_STRIDE
    n_variants = -(-n_cmp // LANES)
    last_visible = (st + Q_BLOCK - CMP_LEN) // CMP_STRIDE

    def cmp_and_select(n_c):
        n_b = n_c // blocks_per_sel
        c_end = lax.broadcasted_iota(I32, (n_c, 1), 0) * CMP_STRIDE + (CMP_LEN - 1)
        m_c = c_end <= q_pos4
        ovt_bf = ovt_ref[0:n_b, 0:n_c].astype(BF16)
        imp_ts = []
        for g in range(KVH):
            s = jnp.where(m_c, _dot(kc_k[g][0:n_c, :], qa_t[g][0:LANES, :]), NEG)
            e = jnp.exp(s - jnp.max(s, axis=0, keepdims=True))
            p = e * (any_c / jnp.sum(e, axis=0, keepdims=True))
            oc_ref[g] = _dot(kc_vt[g][:, 0:n_c], p.astype(BF16))
            psum = p[:, 0:lq] + p[:, lq:2 * lq] + p[:, 2 * lq:3 * lq] + p[:, 3 * lq:4 * lq]
            imp_ts.append(sum(_dot(ovt_bf, t) for t in _split3(psum)))
        bias = _top_blocks_t(imp_ts, st, n_b)
        for g in range(KVH):
            for hl in range(HPG):
                qa_t[g][2 * DH:, hl * lq:(hl + 1) * lq] = bias[g].astype(BF16)

    variant = jnp.minimum(jnp.maximum(last_visible, 0) // LANES, n_variants - 1)
    for k in range(n_variants):
        pl.when(variant == k)(functools.partial(cmp_and_select, min(n_cmp, (k + 1) * LANES)))
    o_c = [oc_ref[g] for g in range(KVH)]

    n_tiles = (st + Q_BLOCK + KEY_TILE - 1) // KEY_TILE
    for g in range(KVH):
        m_ref[g] = jnp.full((1, cols), NEG, F32)
        l_ref[g] = jnp.zeros((1, cols), F32)
        acc_ref[g] = jnp.zeros((DH, cols), F32)

    def scores(t, s_ref):
        r0 = pl.multiple_of(t * KEY_TILE, KEY_TILE)
        for g in range(KVH):
            s_ref[g] = _dot(kaug[g][pl.ds(r0, KEY_TILE), :], qa_t[g][...])

    def consume(t, s_ref, masked):
        r0 = pl.multiple_of(t * KEY_TILE, KEY_TILE)
        for g in range(KVH):
            s = s_ref[g]
            if masked:
                k_pos = r0 + lax.broadcasted_iota(I32, (KEY_TILE, 1), 0)
                s = jnp.where(k_pos <= q_pos4, s, NEG)
            m = m_ref[g]
            m_new = jnp.maximum(m, jnp.max(s, axis=0, keepdims=True))
            a = jnp.exp(m - m_new)
            e = jnp.exp(s - m_new)
            m_ref[g] = m_new
            l_ref[g] = a * l_ref[g] + jnp.sum(e, axis=0, keepdims=True)
            acc_ref[g] = a * acc_ref[g] + _dot(v_t[g][:, pl.ds(r0, KEY_TILE)], e.astype(BF16))

    scores(0, sa_ref)
    n_pairs = (n_tiles - 1) // 2

    def pair(u, _):
        scores(2 * u + 1, sb_ref)
        consume(2 * u, sa_ref, False)
        scores(2 * u + 2, sa_ref)
        consume(2 * u + 1, sb_ref, False)
        return 0

    lax.fori_loop(0, n_pairs, pair, 0)
    odd_tail = (n_tiles - 1) - 2 * n_pairs == 1

    @pl.when(odd_tail)
    def _():
        scores(n_tiles - 1, sb_ref)
        consume(n_tiles - 2, sa_ref, False)
        consume(n_tiles - 1, sb_ref, True)

    @pl.when(jnp.logical_not(odd_tail))
    def _():
        consume(n_tiles - 1, sa_ref, True)

    o_s = [acc_ref[g] * (1.0 / l_ref[g]) for g in range(KVH)]

    band = jnp.concatenate([w0_ref[0], w1_ref[0], w2_ref[0], w3_ref[0], w4_ref[0]], axis=0)
    n_win = band.shape[0]
    w_pos_col = st - WINDOW + lax.broadcasted_iota(I32, (n_win, 1), 0)
    kw_k = _key_rows(band[:, :LANES], jnp.maximum(w_pos_col, 0), invalid=w_pos_col < 0)
    vw_t = band[:, LANES:].T.astype(BF16)
    band_bias = _tile4(wband_ref[...])
    o_w = []
    for g in range(KVH):
        s = _dot(kw_k[g].astype(BF16), qa_t[g][0:LANES, :]) + band_bias
        e = jnp.exp(s - jnp.max(s, axis=0, keepdims=True))
        o_w.append(_dot(vw_t[g * DH:(g + 1) * DH], e.astype(BF16)) * (1.0 / jnp.sum(e, axis=0, keepdims=True)))

    def heads(per_group):
        return jnp.concatenate([per_group[h // HPG][:, (h % HPG) * lq:(h % HPG + 1) * lq]
                                for h in range(NSA_HEADS)], axis=0)

    gates = gate_ref[...]
    out_t = (_dot_nt(egt_ref[0], gates) * heads(o_c)
             + _dot_nt(egt_ref[1], gates) * heads(o_s)
             + _dot_nt(egt_ref[2], gates) * heads(o_w))
    o_ref[...] = out_t.T.astype(o_ref.dtype)


def _gate_expand_t():
    r = jnp.arange(3 * NSA_HEADS)
    c = jnp.arange(NSA_WIDTH)
    return jnp.stack([(3 * (c[:, None] // DH) + br == r[None, :]).astype(F32) for br in range(3)])


def _nsa_prompt(q, gates, kc, kvs, kvw):
    B, S, _ = kvs.shape
    nqb = S // Q_BLOCK
    n_cmp = kc.shape[2] // KVH
    ovt = _overlap_t(n_cmp)
    egt = _gate_expand_t()
    n_band = WINDOW // Q_BLOCK + 1
    d_band = jnp.arange(Q_BLOCK)[None, :] + WINDOW - jnp.arange(n_band * Q_BLOCK)[:, None]
    wband = jnp.where((d_band >= 0) & (d_band < WINDOW), 0.0, NEG).astype(F32)

    def band_spec(i):
        return pl.BlockSpec((1, Q_BLOCK, KV_WIDTH),
                            lambda b, j, i=i: (b, jnp.maximum(j - (n_band - 1) + i, 0), 0))

    full = lambda a: pl.BlockSpec(a.shape, lambda b, j: (0,) * a.ndim)
    return pl.pallas_call(
        functools.partial(_nsa_prompt_body, seq_len=S),
        grid=(B, nqb),
        in_specs=[pl.BlockSpec((Q_BLOCK, NSA_WIDTH), lambda b, j: (b * nqb + j, 0)),
                  pl.BlockSpec((Q_BLOCK, 3 * NSA_HEADS), lambda b, j: (b * nqb + j, 0)),
                  pl.BlockSpec((1,) + kc.shape[1:], lambda b, j: (b, 0, 0, 0)),
                  pl.BlockSpec((1, S, KV_WIDTH), lambda b, j: (b, 0, 0))]
                 + [band_spec(i) for i in range(n_band)] + [full(ovt), full(egt), full(wband)],
        out_specs=pl.BlockSpec((Q_BLOCK, NSA_WIDTH), lambda b, j: (b * nqb + j, 0)),
        out_shape=jax.ShapeDtypeStruct((B * S, NSA_WIDTH), BF16),
        scratch_shapes=[pltpu.VMEM((n_cmp, LANES), BF16)] * 2
                       + [pltpu.VMEM((DH, n_cmp), BF16)] * 2
                       + [pltpu.VMEM((S, 2 * LANES), BF16)] * 2
                       + [pltpu.VMEM((DH, S), BF16)] * 2
                       + [pltpu.VMEM((2 * LANES, HPG * Q_BLOCK), BF16)] * 2
                       + [pltpu.VMEM((KVH, KEY_TILE, HPG * Q_BLOCK), F32)] * 2
                       + [pltpu.VMEM((KVH, 1, HPG * Q_BLOCK), F32)] * 2
                       + [pltpu.VMEM((KVH, DH, HPG * Q_BLOCK), F32)] * 2,
        compiler_params=_params(("arbitrary", "arbitrary")),
        name="nsa_prompt",
    )(q, gates, kc, kvs, *([kvw] * n_band), ovt, egt, wband)


def _nsa_sample_body(pt_ref, q_ref, gate_ref, kvs_new_ref, win_ref, kvw_new_ref, *rest,
                     n_pages, ls, past_len):
    cmp_pages = rest[:n_pages]
    slc_pages = rest[n_pages:2 * n_pages]
    (w1_ref, pe_ref, b1_ref, w2_ref, b2_ref, ovt_ref, eg_ref, o_ref, wout_ref,
     full_k, full_v, kck0, kck1, kcv0, kcv1, kt0, kt1, vt0, vt1, wkt0, wkt1, qa_ref) = rest[2 * n_pages:]
    del pt_ref
    kc_k, kc_v, kaug_t, v_t, wk_t = (kck0, kck1), (kcv0, kcv1), (kt0, kt1), (vt0, vt1), (wkt0, wkt1)
    n_cmp = past_len // CMP_STRIDE
    w_rows = win_ref.shape[4]
    lq = BF16_ROWS
    rows = HPG * lq
    w_start = past_len - w_rows

    @pl.when(pl.program_id(0) == 0)
    def _():
        sub = lax.broadcasted_iota(I32, (DH, past_len), 0)
        pos = lax.broadcasted_iota(I32, (1, past_len), 1)
        ex = _key_alibi_rows(pos, sub).astype(BF16)
        onehot = (lax.broadcasted_iota(I32, (SEL_PAD, past_len), 0) == jnp.right_shift(pos, 6)).astype(BF16)
        subw = lax.broadcasted_iota(I32, (DH, w_rows), 0)
        exw = _key_alibi_rows(w_start + lax.broadcasted_iota(I32, (1, w_rows), 1), subw).astype(BF16)
        for g in range(KVH):
            kaug_t[g][DH:2 * DH, :] = ex
            kaug_t[g][2 * DH:, :] = onehot
            wk_t[g][DH:, :] = exw

    for p in range(n_pages):
        cols = slice(p * PAGE_SIZE, (p + 1) * PAGE_SIZE)
        full_k[cols, :] = cmp_pages[p][0, 0].reshape(2 * DH, PAGE_SIZE).T
        full_v[cols, :] = cmp_pages[p][0, 1].reshape(2 * DH, PAGE_SIZE).T
        for g in range(KVH):
            kaug_t[g][0:DH, cols] = slc_pages[p][0, 0, g].astype(BF16)
            vt = slc_pages[p][0, 1, g].astype(BF16)
            v_t[g][0:DH, cols] = vt
            v_t[g][DH:, cols] = vt
    for g in range(KVH):
        wk_t[g][0:DH, :] = win_ref[0, 0, g].astype(BF16)

    keys_c, values_c = _compress((full_k, full_v), n_cmp, w1_ref, pe_ref, b1_ref, w2_ref, b2_ref)
    for g in range(KVH):
        kc_k[g][...] = keys_c[g * n_cmp:(g + 1) * n_cmp].astype(BF16)
        kc_v[g][...] = values_c[g * n_cmp:(g + 1) * n_cmp].astype(BF16)

    pad_q = jnp.zeros((lq - ls, NSA_WIDTH), F32)
    q_pos = past_len + lax.broadcasted_iota(I32, (lq, 1), 0)
    q_pos4 = _stack4(q_pos)
    _fill_queries(qa_ref, jnp.concatenate([q_ref[0] * ATTN_SCALE, pad_q], axis=0), q_pos, lq)
    gates = jnp.concatenate([gate_ref[0], jnp.zeros((lq - ls, 3 * NSA_HEADS), F32)], axis=0)

    o_c, psums = _cmp_branch(qa_ref, kc_k, kc_v, q_pos4, lq, n_cmp)
    n_sel = -(-(past_len + lq) // SEL_LEN)
    n_blk = -(-n_sel // 8) * 8
    _store_selbias(qa_ref, _select_blocks(psums, ovt_ref, past_len, lq, n_blk), lq)

    pad_k = jnp.zeros((LANES - ls, KV_WIDTH), F32)
    new_pos_col = past_len + lax.broadcasted_iota(I32, (LANES, 1), 0)
    new_pos = past_len + lax.broadcasted_iota(I32, (1, LANES), 1)
    new_s = jnp.concatenate([kvs_new_ref[0], pad_k], axis=0)
    new_w = jnp.concatenate([kvw_new_ref[0], pad_k], axis=0)
    ks_new = _key_rows(new_s[:, :LANES], new_pos_col)
    vs_new = _halves(new_s[:, LANES:])
    kw_new = _key_rows(new_w[:, :LANES], new_pos_col)
    vw_new = _halves(new_w[:, LANES:])
    onehot_new = _block_onehot(new_pos_col, LANES)
    causal_new = q_pos4 >= new_pos

    o_s, o_w = [], []
    d_past = q_pos4 - (w_start + lax.broadcasted_iota(I32, (1, w_rows), 1))
    m_past = (d_past >= 0) & (d_past < WINDOW)
    d_new = q_pos4 - new_pos
    m_new = (d_new >= 0) & (d_new < WINDOW)
    for g in range(KVH):
        s_past = _dot(qa_ref[g], kaug_t[g][...])
        k_new = jnp.concatenate([ks_new[g].astype(BF16), onehot_new], axis=1)
        s_new = jnp.where(causal_new, _dot_nt(qa_ref[g], k_new), NEG)
        m = jnp.maximum(jnp.max(s_past, axis=-1, keepdims=True), jnp.max(s_new, axis=-1, keepdims=True))
        e_past = jnp.exp(s_past - m)
        e_new = jnp.exp(s_new - m)
        den = jnp.sum(e_past, axis=-1, keepdims=True) + jnp.sum(e_new, axis=-1, keepdims=True)
        acc = _dot_nt(e_past.astype(BF16), v_t[g][...]) + _dot(e_new.astype(BF16), vs_new[g].astype(BF16))
        o_s.append(acc / den)

        sw_past = jnp.where(m_past, _dot(qa_ref[g, :, 0:LANES], wk_t[g][...]), NEG)
        sw_new = jnp.where(m_new, _dot_nt(qa_ref[g, :, 0:LANES], kw_new[g].astype(BF16)), NEG)
        m = jnp.maximum(jnp.max(sw_past, axis=-1, keepdims=True), jnp.max(sw_new, axis=-1, keepdims=True))
        e_past = jnp.exp(sw_past - m)
        e_new = jnp.exp(sw_new - m)
        den = jnp.sum(e_past, axis=-1, keepdims=True) + jnp.sum(e_new, axis=-1, keepdims=True)
        vw = win_ref[0, 1, g].astype(BF16)
        vw2 = jnp.concatenate([vw, vw], axis=0)
        acc = _dot_nt(e_past.astype(BF16), vw2) + _dot(e_new.astype(BF16), vw_new[g].astype(BF16))
        o_w.append(acc / den)

    o_ref[0] = _combine(gates, eg_ref, o_c, o_s, o_w, lq)[:ls]

    keep = w_rows - ls
    tail_lane = lax.broadcasted_iota(I32, (DH, LANES), 1) >= keep % LANES
    for c in range(2):
        new_t = pltpu.roll(new_w[:, c * LANES:(c + 1) * LANES].T, keep % LANES, 1)
        for g in range(KVH):
            shifted = pltpu.roll(win_ref[0, c, g], keep, 1)
            last = jnp.where(tail_lane, new_t[g * DH:(g + 1) * DH], shifted[:, w_rows - LANES:])
            wout_ref[0, c, g] = jnp.concatenate([shifted[:, :w_rows - LANES], last], axis=1)


def _nsa_sample(q, gates, kvs_new, kvw_new, win_t, cmp_t, slc_t, page_table, cw, past_len):
    Bd, ls, _ = q.shape
    n_pages = page_table.shape[1]
    n_cmp = past_len // CMP_STRIDE
    ovt = _overlap_t(n_cmp)
    eg = _gate_expand()
    w_rows = win_t.shape[4]

    per_b = lambda a: pl.BlockSpec((1,) + a.shape[1:], lambda b, pt: (b,) + (0,) * (a.ndim - 1))
    full = lambda a: pl.BlockSpec(a.shape, lambda b, pt: (0,) * a.ndim)
    page = lambda p: pl.BlockSpec((1, 2, KVH, DH, PAGE_SIZE), lambda b, pt, p=p: (pt[b, p], 0, 0, 0, 0))
    grid_spec = pltpu.PrefetchScalarGridSpec(
        num_scalar_prefetch=1,
        grid=(Bd,),
        in_specs=[per_b(q), per_b(gates), per_b(kvs_new), per_b(win_t), per_b(kvw_new)]
                 + [page(p) for p in range(n_pages)] * 2
                 + [full(a) for a in cw] + [full(ovt), full(eg)],
        out_specs=[pl.BlockSpec((1, ls, NSA_WIDTH), lambda b, pt: (b, 0, 0)), per_b(win_t)],
        scratch_shapes=[pltpu.VMEM((past_len, LANES), F32)] * 2
                       + [pltpu.VMEM((n_cmp, LANES), BF16)] * 4
                       + [pltpu.VMEM((2 * LANES, past_len), BF16)] * 2
                       + [pltpu.VMEM((LANES, past_len), BF16)] * 2
                       + [pltpu.VMEM((LANES, w_rows), BF16)] * 2
                       + [pltpu.VMEM((KVH, HPG * BF16_ROWS, 2 * LANES), BF16)],
    )
    return pl.pallas_call(
        functools.partial(_nsa_sample_body, n_pages=n_pages, ls=ls, past_len=past_len),
        grid_spec=grid_spec,
        out_shape=[jax.ShapeDtypeStruct((Bd, ls, NSA_WIDTH), F32), jax.ShapeDtypeStruct(win_t.shape, F32)],
        compiler_params=_params(("arbitrary",)),
        name="nsa_sample",
    )(page_table, q, gates, kvs_new, win_t, kvw_new,
      *([cmp_t] * n_pages), *([slc_t] * n_pages), *cw, ovt, eg)


def _fin1_body(h_ref, pool_ref, nsa_ref, wo_ref, g_ref, b_ref, wq_ref, h1_ref, qm_ref):
    mix = (_dot(pool_ref[...].astype(BF16), wo_ref[0:POOL_WIDTH, :])
           + _dot(nsa_ref[...].astype(BF16), wo_ref[POOL_WIDTH:, :]))
    h1 = _layer_norm(DN_ALPHA * h_ref[...] + mix, g_ref[...], b_ref[...])
    h1_ref[...] = h1
    qm_ref[...] = (_dot(h1.astype(BF16), wq_ref[...]) * (MEM_HEAD_DIM ** -0.5)).astype(qm_ref.dtype)


def _fin1(h, pool_o, nsa_o, w_out_bf, g, b, wq_bf, q_dtype):
    T = h.shape[0]
    tm = ROW_TILE
    row = lambda n: pl.BlockSpec((tm, n), lambda i: (i, 0))
    full = lambda a: pl.BlockSpec(a.shape, lambda i: (0,) * a.ndim)
    return pl.pallas_call(
        _fin1_body,
        grid=(T // tm,),
        in_specs=[row(D_MODEL), row(POOL_WIDTH), row(NSA_WIDTH), full(w_out_bf), full(g), full(b), full(wq_bf)],
        out_specs=[row(D_MODEL), row(D_MODEL)],
        out_shape=[jax.ShapeDtypeStruct((T, D_MODEL), F32), jax.ShapeDtypeStruct((T, D_MODEL), q_dtype)],
        compiler_params=_params(("arbitrary",)),
        name="out_proj_ln1",
    )(h, pool_o, nsa_o, w_out_bf, g, b, wq_bf)


def _memattn_body(q_ref, kv_ref, o_ref):
    width = MEM_HEADS * MEM_HEAD_DIM
    for h in range(MEM_HEADS):
        cols = slice(h * MEM_HEAD_DIM, (h + 1) * MEM_HEAD_DIM)
        qh = q_ref[0, :, cols].astype(BF16)
        kh = kv_ref[0, :, cols].astype(BF16)
        vh = kv_ref[0, :, width + h * MEM_HEAD_DIM:width + (h + 1) * MEM_HEAD_DIM].astype(BF16)
        s = _dot_nt(qh, kh)
        e = jnp.exp(s - jnp.max(s, axis=-1, keepdims=True))
        o = _dot(e.astype(BF16), vh) * (1.0 / jnp.sum(e, axis=-1, keepdims=True))
        o_ref[0, :, cols] = o.astype(o_ref.dtype)


def _memattn(qm, mem_kv, tq):
    nb, L, W = qm.shape
    return pl.pallas_call(
        _memattn_body,
        grid=(nb, L // tq),
        in_specs=[pl.BlockSpec((1, tq, W), lambda b, t: (b, t, 0)),
                  pl.BlockSpec((1, MEM_LEN, 2 * W), lambda b, t: (b, 0, 0))],
        out_specs=pl.BlockSpec((1, tq, W), lambda b, t: (b, t, 0)),
        out_shape=jax.ShapeDtypeStruct((nb, L, W), BF16),
        compiler_params=_params(("arbitrary", "arbitrary")),
        name="mem_attn",
    )(qm, mem_kv)


def _memattn_few_body(q_ref, kv_ref, o_ref):
    lq = q_ref.shape[1]
    n_keys = MEM_LEN * MEM_HEADS
    assert lq & (lq - 1) == 0 and MEM_HEADS & (MEM_HEADS - 1) == 0
    shape = (MEM_HEADS * lq, n_keys)
    own_head = (jnp.bitwise_and(lax.broadcasted_iota(I32, shape, 1), MEM_HEADS - 1)
                == jnp.right_shift(lax.broadcasted_iota(I32, shape, 0), lq.bit_length() - 1))
    for i in range(q_ref.shape[0]):
        q = q_ref[i]
        qs = jnp.concatenate([q[:, h * MEM_HEAD_DIM:(h + 1) * MEM_HEAD_DIM] for h in range(MEM_HEADS)], axis=0)
        k = kv_ref[i, :, 0, :, :].reshape(n_keys, MEM_HEAD_DIM).astype(BF16)
        v = kv_ref[i, :, 1, :, :].reshape(n_keys, MEM_HEAD_DIM).astype(BF16)
        s = jnp.where(own_head, _dot_nt(qs.astype(BF16), k), NEG)
        e = jnp.exp(s - jnp.max(s, axis=-1, keepdims=True))
        o = _dot(e.astype(BF16), v) * (1.0 / jnp.sum(e, axis=-1, keepdims=True))
        for h in range(MEM_HEADS):
            o_ref[i, :, h * MEM_HEAD_DIM:(h + 1) * MEM_HEAD_DIM] = o[h * lq:(h + 1) * lq]


def _memattn_few(qm, mem_kv):
    nb, lq, W = qm.shape
    per_step = MEM_PER_STEP if nb % MEM_PER_STEP == 0 else 1
    return pl.pallas_call(
        _memattn_few_body,
        grid=(nb // per_step,),
        in_specs=[pl.BlockSpec((per_step, lq, W), lambda b: (b, 0, 0)),
                  pl.BlockSpec((per_step, MEM_LEN, 2, MEM_HEADS, MEM_HEAD_DIM), lambda b: (b, 0, 0, 0, 0))],
        out_specs=pl.BlockSpec((per_step, lq, W), lambda b: (b, 0, 0)),
        out_shape=jax.ShapeDtypeStruct((nb, lq, W), F32),
        compiler_params=_params(("arbitrary",)),
        name="mem_attn_few",
    )(qm, mem_kv)


def _fin2_body(cnt0_ref, h1_ref, o_ref, wo_ref, g_ref, b_ref, rw_ref, rb_ref, *rest, n_own):
    h2_ref, te_ref, tg_ref, cnt_ref, run_ref = rest[-5:]
    step = pl.program_id(0)

    @pl.when(step >= n_own)
    def _():
        h2_ref[...] = jnp.zeros(h2_ref.shape, F32)
        te_ref[...] = jnp.zeros(te_ref.shape, I32)
        tg_ref[...] = jnp.zeros(tg_ref.shape, F32)

    @pl.when(step < n_own)
    def _():
        _fin2_rows(cnt0_ref, h1_ref, o_ref, wo_ref, g_ref, b_ref, rw_ref, rb_ref,
                   h2_ref, te_ref, tg_ref, cnt_ref, run_ref)


def _fin2_rows(cnt0_ref, h1_ref, o_ref, wo_ref, g_ref, b_ref, rw_ref, rb_ref,
               h2_ref, te_ref, tg_ref, cnt_ref, run_ref):
    tm = h1_ref.shape[0]

    @pl.when(pl.program_id(0) == 0)
    def _():
        run_ref[...] = cnt0_ref[...]

    a = _dot(o_ref[...].astype(BF16), wo_ref[...])
    h2 = _layer_norm(DN_ALPHA * h1_ref[...] + a, g_ref[...], b_ref[...])
    h2_ref[...] = h2
    a_hi, a_lo, _ = _split3(h2)
    w_hi, w_lo, _ = _split3(rw_ref[...])
    logits = _dot(a_hi, w_hi) + (_dot(a_hi, w_lo) + _dot(a_lo, w_hi)) + rb_ref[...]
    e_iota = lax.broadcasted_iota(I32, (tm, N_EXPERTS), 1).astype(F32)
    lane = lax.broadcasted_iota(I32, (tm, LANES), 1)
    te = jnp.zeros((tm, LANES), F32)
    tv = jnp.full((tm, LANES), NEG, F32)
    work = logits
    chosen = []
    for k in range(TOP_K):
        m = jnp.max(work, axis=-1, keepdims=True)
        idx = jnp.min(jnp.where(work == m, e_iota, float(N_EXPERTS)), axis=-1, keepdims=True)
        hit = e_iota == idx
        chosen.append(hit)
        te = jnp.where(lane == k, idx, te)
        tv = jnp.where(lane == k, m, tv)
        work = jnp.where(hit, -jnp.inf, work)
    member = sum(c.astype(F32) for c in chosen)
    earlier = (lax.broadcasted_iota(I32, (tm, tm), 0) > lax.broadcasted_iota(I32, (tm, tm), 1)).astype(BF16)
    before = _dot(earlier, member.astype(BF16)) + run_ref[...]
    for k in range(TOP_K):
        rank = jnp.sum(jnp.where(chosen[k], before, 0.0), axis=-1, keepdims=True)
        te = jnp.where(lane == TOP_K + k, rank, te)
    run_ref[...] = run_ref[...] + jnp.sum(member, axis=0, keepdims=True)
    cnt_ref[...] = run_ref[...]
    ex = jnp.exp(tv - jnp.max(tv, axis=-1, keepdims=True))
    te_ref[...] = te.astype(I32)
    tg_ref[...] = ex / jnp.sum(ex, axis=-1, keepdims=True)


def _fin2(cnt0, h1, o, wo_bf, g, b, rw, rb, total_rows, row_offset=0, into=None):
    T = h1.shape[0]
    tm = ROW_TILE
    blk0 = row_offset // tm
    n_own = T // tm
    n_steps = n_own if into is not None else (total_rows - row_offset) // tm
    row = lambda n: pl.BlockSpec((tm, n), lambda i: (jnp.minimum(i, n_own - 1), 0))
    out_row = lambda n: pl.BlockSpec((tm, n), lambda i: (i + blk0, 0))
    full = lambda a: pl.BlockSpec(a.shape, lambda i: (0,) * a.ndim)
    ins = [cnt0, h1, o, wo_bf, g, b, rw, rb]
    in_specs = [full(cnt0), row(D_MODEL), row(D_MODEL), full(wo_bf), full(g), full(b), full(rw), full(rb)]
    aliases = {}
    if into is not None:
        aliases = {len(ins) + k: k for k in range(len(into))}
        in_specs = in_specs + [pl.BlockSpec(memory_space=pl.ANY)] * len(into)
        ins = ins + list(into)
    return pl.pallas_call(
        functools.partial(_fin2_body, n_own=n_own),
        grid=(n_steps,),
        in_specs=in_specs,
        out_specs=[out_row(D_MODEL), out_row(LANES), out_row(LANES), full(cnt0)],
        out_shape=[jax.ShapeDtypeStruct((total_rows, D_MODEL), F32), jax.ShapeDtypeStruct((total_rows, LANES), I32),
                   jax.ShapeDtypeStruct((total_rows, LANES), F32), jax.ShapeDtypeStruct(cnt0.shape, F32)],
        scratch_shapes=[pltpu.VMEM(cnt0.shape, F32)],
        input_output_aliases=aliases,
        compiler_params=_params(("arbitrary",)),
        name="mem_out_ln2_router",
    )(*ins)


def _moe_body(ut_ref, ue_ref, nu_ref, rs_ref, re_ref, x_ref, wgu_ref, bgu_ref, wdn_ref, bdn_ref,
              y_ref, wgu_bf, wdn_bf):
    u = pl.program_id(0)
    bk = x_ref.shape[0]
    e = ue_ref[u]
    tile = ut_ref[u]
    prev = jnp.maximum(u - 1, 0)

    @pl.when((u == 0) | (e != ue_ref[prev]))
    def _():
        wgu_bf[...] = wgu_ref[0].astype(BF16)
        wdn_bf[...] = wdn_ref[0].astype(BF16)

    @pl.when(u < nu_ref[0])
    def _():
        x = x_ref[...].astype(BF16)
        g = _dot(x, wgu_bf[:, :D_FF]) + bgu_ref[0, :, :D_FF]
        v = _dot(x, wgu_bf[:, D_FF:]) + bgu_ref[0, :, D_FF:]
        g = jnp.minimum(g, SWIGLU_LIMIT)
        v = jnp.clip(v, -SWIGLU_LIMIT, SWIGLU_LIMIT)
        a = g * (1.0 / (1.0 + jnp.exp(-SWIGLU_ALPHA * g))) * (v + 1.0)
        y = _dot(a.astype(BF16), wdn_bf[...]) + bdn_ref[0]
        row = tile * bk + lax.broadcasted_iota(I32, (bk, 1), 0)
        mine = (row >= rs_ref[e]) & (row < re_ref[e])
        y = jnp.where(mine, y, 0.0)

        @pl.when((u == 0) | (tile != ut_ref[prev]))
        def _():
            y_ref[...] = y

        @pl.when((u > 0) & (tile == ut_ref[prev]))
        def _():
            y_ref[...] = y_ref[...] + y


def _moe_gmm(x_rows, units, w_gu, b_gu, w_dn, b_dn):
    N = x_rows.shape[0]
    bk = MOE_ROWS
    unit_tile, unit_e, n_units, r_start, r_end = units
    grid_spec = pltpu.PrefetchScalarGridSpec(
        num_scalar_prefetch=5,
        grid=(unit_tile.shape[0],),
        in_specs=[pl.BlockSpec((bk, D_MODEL), lambda u, ut, ue, *_: (ut[u], 0)),
                  pl.BlockSpec((1, D_MODEL, 2 * D_FF), lambda u, ut, ue, *_: (ue[u], 0, 0)),
                  pl.BlockSpec((1, 1, 2 * D_FF), lambda u, ut, ue, *_: (ue[u], 0, 0)),
                  pl.BlockSpec((1, D_FF, D_MODEL), lambda u, ut, ue, *_: (ue[u], 0, 0)),
                  pl.BlockSpec((1, 1, D_MODEL), lambda u, ut, ue, *_: (ue[u], 0, 0))],
        out_specs=pl.BlockSpec((bk, D_MODEL), lambda u, ut, ue, *_: (ut[u], 0)),
        scratch_shapes=[pltpu.VMEM((D_MODEL, 2 * D_FF), BF16), pltpu.VMEM((D_FF, D_MODEL), BF16)],
    )
    return pl.pallas_call(
        _moe_body,
        grid_spec=grid_spec,
        out_shape=jax.ShapeDtypeStruct((N, D_MODEL), F32),
        compiler_params=_params(("arbitrary",)),
        name="moe_experts",
    )(unit_tile, unit_e, n_units, r_start, r_end, x_rows, w_gu, b_gu, w_dn, b_dn)


FLAT_BITS = 17


def _moe_routing(te, counts):
    bk = MOE_ROWS
    T = te.shape[0]
    N = T * TOP_K
    assert N % bk == 0 and N <= (1 << FLAT_BITS)
    experts = jnp.arange(N_EXPERTS, dtype=I32)
    top_e = te[:, :TOP_K]
    r_end = jnp.cumsum(counts).astype(I32)
    r_start = r_end - counts
    onehot = top_e[:, :, None] == experts[None, None, :]
    pos = jnp.sum(jnp.where(onehot, r_start[None, None, :], 0), axis=-1) + te[:, TOP_K:2 * TOP_K]
    key = jnp.left_shift(top_e.reshape(-1), FLAT_BITS) + jnp.arange(N, dtype=I32)
    key_s = lax.sort(key)
    tok_s = jnp.right_shift(jnp.bitwise_and(key_s, (1 << FLAT_BITS) - 1), 2)
    first = r_start // bk
    last = (r_end - 1) // bk
    n_e = jnp.where(counts > 0, last - first + 1, 0)
    u_end = jnp.cumsum(n_e).astype(I32)
    u_start = u_end - n_e
    n_units = u_end[-1]
    u = jnp.minimum(jnp.arange(N // bk + N_EXPERTS - 1, dtype=I32), n_units - 1)
    unit_e = jnp.sum((u[:, None] >= u_end[None, :]).astype(I32), axis=1)
    unit_tile = u + jnp.sum(jnp.where(unit_e[:, None] == experts[None, :], (first - u_start)[None, :], 0), axis=1)
    return pos, tok_s, (unit_tile, unit_e, n_units.reshape(1), r_start, r_end)


def _fin3_body(h2_ref, tg_ref, y0_ref, y1_ref, y2_ref, y3_ref, g_ref, b_ref, o_ref):
    gate = lambda k: tg_ref[:, k:k + 1]
    y = ((gate(0) * y0_ref[...] + gate(1) * y1_ref[...])
         + (gate(2) * y2_ref[...] + gate(3) * y3_ref[...]))
    o_ref[...] = _layer_norm(DN_ALPHA * h2_ref[...] + y, g_ref[...], b_ref[...])


def _fin3(h2, tg, ys, g, b, row_offset, T):
    tm = ROW_TILE
    blk0 = row_offset // tm
    row = lambda n: pl.BlockSpec((tm, n), lambda i: (i, 0))
    full = lambda a: pl.BlockSpec(a.shape, lambda i: (0,) * a.ndim)
    return pl.pallas_call(
        _fin3_body,
        grid=(T // tm,),
        in_specs=[pl.BlockSpec((tm, D_MODEL), lambda i: (i + blk0, 0)), pl.BlockSpec((tm, LANES), lambda i: (i + blk0, 0))]
                 + [pl.BlockSpec((tm, D_MODEL), lambda i: (i + blk0, 0))] * TOP_K + [full(g), full(b)],
        out_specs=row(D_MODEL),
        out_shape=jax.ShapeDtypeStruct((T, D_MODEL), F32),
        compiler_params=_params(("arbitrary",)),
        name="combine_ln3",
    )(h2, tg, *ys, g, b)


def kernel(x_prompt, x_sample, cache_cmp_kv, cache_slc_kv, state_win_kv, state_pool, cache_mem_kv, page_table,
           mem_prompt, w_in, pool_w, pool_scale, cmp_pe, cmp_w1, cmp_b1, cmp_w2, cmp_b2, w_out, ln1_g, ln1_b,
           mem_wq, mem_wkv, mem_wo, ln2_g, ln2_b, router_w, router_b, exp_w_gu, exp_b_gu, exp_w_dn, exp_b_dn,
           ln3_g, ln3_b):
    Bp, S, D = x_prompt.shape
    Bd, Ls, _ = x_sample.shape
    Tp, Ts = Bp * S, Bd * Ls
    l = 0
    w_in_bf = w_in[l].astype(BF16)
    pool_w_bf = pool_w[l].astype(BF16)
    ps = pool_scale[l][None, :]
    cw = _compress_weights(cmp_pe[l], cmp_w1[l], cmp_b1[l], cmp_w2[l], cmp_b2[l])
    w_out_bf = w_out[l].astype(BF16)
    wq_bf = mem_wq[l].astype(BF16)
    wo_bf = mem_wo[l].astype(BF16)
    vec = lambda a: a[l][None, :]

    up, qp, kvc_p, kvs_p, kvw_p, gp, pool_p, kvc_t, kvs_t = _inproj_prompt(
        x_prompt.reshape(Tp, D), w_in_bf, pool_w_bf, ps, S)
    kc_p = _compress_prompt(kvc_p.reshape(Bp, S, KV_WIDTH), cw)
    nsa_p = _nsa_prompt(qp, gp, kc_p, kvs_p.reshape(Bp, S, KV_WIDTH), kvw_p.reshape(Bp, S, KV_WIDTH))
    mem_kv_p = _matmul(mem_prompt.reshape(Bp * MEM_LEN, D), mem_wkv[l]).reshape(Bp, MEM_LEN, 2 * D)
    h1_p, qm_p = _fin1(x_prompt.reshape(Tp, D), pool_p, nsa_p, w_out_bf, vec(ln1_g), vec(ln1_b), wq_bf, BF16)
    om_p = _memattn(qm_p.reshape(Bp, S, D), mem_kv_p, ROW_TILE).reshape(Tp, D)
    T = Tp + Ts
    *routed_p, cnt_p = _fin2(jnp.zeros((1, N_EXPERTS), F32), h1_p, om_p, wo_bf, vec(ln2_g), vec(ln2_b),
                             router_w[l], vec(router_b), total_rows=T)

    state_pad = jnp.pad(state_pool[l], ((0, 0), (1, 0), (0, 0)))
    us, qs, kvc_s, kvs_s, kvw_s, gs, pool_s = _inproj_sample(
        x_sample.reshape(Ts, D), w_in_bf, pool_w_bf, ps, state_pad, Ls, PAST_LEN)
    feature_major = lambda a: jnp.transpose(a, (0, 2, 3, 4, 1))
    nsa_s, win_next = _nsa_sample(qs.reshape(Bd, Ls, NSA_WIDTH), gs.reshape(Bd, Ls, 3 * NSA_HEADS),
                        kvs_s.reshape(Bd, Ls, KV_WIDTH), kvw_s.reshape(Bd, Ls, KV_WIDTH),
                        feature_major(state_win_kv[l]), feature_major(cache_cmp_kv[l]),
                        feature_major(cache_slc_kv[l]), page_table, cw, PAST_LEN)
    h1_s, qm_s = _fin1(x_sample.reshape(Ts, D), pool_s, nsa_s.reshape(Ts, NSA_WIDTH), w_out_bf,
                       vec(ln1_g), vec(ln1_b), wq_bf, F32)
    om_s = _memattn_few(qm_s.reshape(Bd, Ls, D), cache_mem_kv[l]).reshape(Ts, D)
    h2, te, tg, cnt_s = _fin2(cnt_p, h1_s, om_s, wo_bf, vec(ln2_g), vec(ln2_b), router_w[l], vec(router_b),
                              total_rows=T, row_offset=Tp, into=routed_p)

    pos, tok_s, units = _moe_routing(te, cnt_s[0].astype(I32))
    y_rows = _moe_gmm(h2[tok_s], units, exp_w_gu[l], exp_b_gu[l][:, None, :],
                      exp_w_dn[l], exp_b_dn[l][:, None, :])
    ys = [y_rows[pos[:, k]] for k in range(TOP_K)]
    y_prompt = _fin3(h2, tg, ys, vec(ln3_g), vec(ln3_b), 0, Tp).reshape(Bp, S, D)
    y_sample = _fin3(h2, tg, ys, vec(ln3_g), vec(ln3_b), Tp, Ts).reshape(Bd, Ls, D)

    kv6 = lambda a, b, n: a.reshape(1, b, n, 2, KVH, DH)
    row_major = lambda a: jnp.transpose(a, (0, 4, 1, 2, 3))
    win_p = kvw_p.reshape(Bp, S, KV_WIDTH)[:, S - min(WINDOW, S):]
    pool_state_p = up.reshape(Bp, S, POOL_WIDTH)[:, S - POOL_STATE:]
    pool_state_s = jnp.concatenate([state_pool[l], us.reshape(Bd, Ls, POOL_WIDTH)], axis=1)[:, -POOL_STATE:]
    return (y_prompt, y_sample,
            row_major(kvc_t)[None], row_major(kvs_t)[None], kv6(win_p, Bp, min(WINDOW, S)),
            pool_state_p[None], mem_kv_p.reshape(1, Bp, MEM_LEN, 2, MEM_HEADS, MEM_HEAD_DIM),
            kv6(kvc_s, Bd, Ls), kv6(kvs_s, Bd, Ls), row_major(win_next)[None], pool_state_s[None])
```

```python
import functools

import jax
import jax.numpy as jnp
from jax import lax
from jax.experimental import pallas as pl
from jax.experimental.pallas import tpu as pltpu

F32 = jnp.float32
BF16 = jnp.bfloat16
I32 = jnp.int32

D_MODEL = 1024
POOL_WIDTH = 512
POOL_WINDOWS = (2, 4, 8, 16)
POOL_GROUP = 128
POOL_STATE = 15
NSA_WIDTH = 512
DH = 64
NSA_HEADS = 8
KVH = 2
HPG = 4
CMP_LEN = 32
CMP_STRIDE = 16
CMP_HIDDEN = 256
SEL_LEN = 64
SEL_TOP = 16
WINDOW = 512
Q_BLOCK = 128
KV_WIDTH = 256
ATTN_SCALE = DH ** -0.5
FORCED_SCORE = 1e4
NEG = -1e30
MEM_LEN = 256
MEM_HEADS = 4
MEM_HEAD_DIM = 256
N_EXPERTS = 32
TOP_K = 4
D_FF = 1024
SWIGLU_LIMIT = 7.0
SWIGLU_ALPHA = 1.702
DN_ALPHA = 2.0 ** 0.25
LN_EPS = 1e-5
PAST_LEN = 2048
PAGE_SIZE = 128

LANES = 128
SEL_PAD = 128
KEY_TILE = 512
ROW_TILE = 512
MOE_ROWS = 512
MEM_PER_STEP = 8
BF16_ROWS = 16
VMEM_LIMIT = 56 * 1024 * 1024

HIGHEST = lax.Precision.HIGHEST


def _dot(a, b):
    return jnp.dot(a, b, preferred_element_type=F32)


def _dot_nt(a, b, precision=None):
    return lax.dot_general(a, b, (((1,), (1,)), ((), ())), preferred_element_type=F32,
                           precision=precision)


def _layer_norm(x, g, b):
    mu = jnp.mean(x, axis=-1, keepdims=True)
    xc = x - mu
    var = jnp.mean(xc * xc, axis=-1, keepdims=True)
    return xc * lax.rsqrt(var + LN_EPS) * g + b


def _params(sem, vmem=VMEM_LIMIT):
    return pltpu.CompilerParams(dimension_semantics=sem, vmem_limit_bytes=vmem)


def _split_store(u, up_ref, q_ref, kvc_ref, kvs_ref, kvw_ref, gate_ref):
    o1 = POOL_WIDTH
    o2 = o1 + NSA_WIDTH
    o3 = o2 + KV_WIDTH
    o4 = o3 + KV_WIDTH
    o5 = o4 + KV_WIDTH
    up_ref[...] = u[:, :o1]
    q_ref[...] = u[:, o1:o2]
    kvc_ref[...] = u[:, o2:o3]
    kvs_ref[...] = u[:, o3:o4]
    kvw_ref[...] = u[:, o4:o5]
    gate_ref[...] = 1.0 / (1.0 + jnp.exp(-u[:, o5:]))


def _inproj_prompt_body(x_ref, w_ref, pw_ref, ps_ref,
                        up_ref, q_ref, kvc_ref, kvs_ref, kvw_ref, gate_ref, pool_ref, kvct_ref, kvst_ref,
                        ext_ref, *, tm, tiles_per_seq):
    halo = POOL_STATE + 1
    t_in_seq = pl.program_id(0) % tiles_per_seq
    u = _dot(x_ref[...].astype(BF16), w_ref[...])
    _split_store(u, up_ref, q_ref, kvc_ref, kvs_ref, kvw_ref, gate_ref)
    o2 = POOL_WIDTH + NSA_WIDTH
    kvct_ref[0] = u[:, o2:o2 + KV_WIDTH].T.reshape(2, KVH, DH, tm)
    kvst_ref[0] = u[:, o2 + KV_WIDTH:o2 + 2 * KV_WIDTH].T.reshape(2, KVH, DH, tm)

    @pl.when(t_in_seq == 0)
    def _():
        ext_ref[0:halo, :] = jnp.zeros((halo, POOL_WIDTH), F32)

    ext_ref[halo:halo + tm, :] = u[:, :POOL_WIDTH]
    pos = t_in_seq * tm + lax.broadcasted_iota(I32, (tm, 1), 0)
    for gi, w in enumerate(POOL_WINDOWS):
        cols = slice(gi * POOL_GROUP, (gi + 1) * POOL_GROUP)
        acc = ext_ref[halo:halo + tm, cols]
        for k in range(1, w):
            acc = acc + ext_ref[halo - k:halo - k + tm, cols]
        cnt = jnp.minimum(pos + 1, w).astype(F32)
        d = acc / cnt - ext_ref[halo:halo + tm, cols]
        o = _dot(d.astype(BF16), pw_ref[gi])
        pool_ref[:, cols] = (o * ps_ref[:, cols]).astype(pool_ref.dtype)
    ext_ref[0:halo, :] = ext_ref[tm:tm + halo, :]


def _inproj_prompt(x2d, w_in_bf, pool_w_bf, pool_scale, seq_len):
    T = x2d.shape[0]
    tm = ROW_TILE
    outs = [POOL_WIDTH, NSA_WIDTH, KV_WIDTH, KV_WIDTH, KV_WIDTH, 3 * NSA_HEADS, POOL_WIDTH]
    row = lambda n: pl.BlockSpec((tm, n), lambda i: (i, 0))
    full = lambda a: pl.BlockSpec(a.shape, lambda i: (0,) * a.ndim)
    tps = seq_len // tm
    kvt_spec = pl.BlockSpec((1, 2, KVH, DH, tm), lambda i: (i // tps, 0, 0, 0, i % tps))
    kvt_shape = jax.ShapeDtypeStruct((T // seq_len, 2, KVH, DH, seq_len), F32)
    return pl.pallas_call(
        functools.partial(_inproj_prompt_body, tm=tm, tiles_per_seq=tps),
        grid=(T // tm,),
        in_specs=[row(D_MODEL), full(w_in_bf), full(pool_w_bf), full(pool_scale)],
        out_specs=[row(n) for n in outs] + [kvt_spec] * 2,
        out_shape=[jax.ShapeDtypeStruct((T, n), F32) for n in outs[:-1]]
                  + [jax.ShapeDtypeStruct((T, outs[-1]), BF16)] + [kvt_shape] * 2,
        scratch_shapes=[pltpu.VMEM((tm + POOL_STATE + 1, POOL_WIDTH), F32)],
        compiler_params=_params(("arbitrary",)),
        name="inproj_prompt",
    )(x2d, w_in_bf, pool_w_bf, pool_scale)


def _inproj_sample_body(x_ref, w_ref, pw_ref, ps_ref, st_ref,
                        up_ref, q_ref, kvc_ref, kvs_ref, kvw_ref, gate_ref, pool_ref,
                        ext_ref, *, nb, ls, pos0):
    halo = POOL_STATE + 1
    tm = nb * ls
    u = _dot(x_ref[...].astype(BF16), w_ref[...])
    _split_store(u, up_ref, q_ref, kvc_ref, kvs_ref, kvw_ref, gate_ref)
    ext_ref[:, 0:halo, :] = st_ref[...]
    ext_ref[:, halo:halo + ls, :] = u[:, :POOL_WIDTH].reshape(nb, ls, POOL_WIDTH)
    pos = pos0 + lax.broadcasted_iota(I32, (1, ls, 1), 1)
    for gi, w in enumerate(POOL_WINDOWS):
        cols = slice(gi * POOL_GROUP, (gi + 1) * POOL_GROUP)
        acc = ext_ref[:, halo:halo + ls, cols]
        for k in range(1, w):
            acc = acc + ext_ref[:, halo - k:halo - k + ls, cols]
        cnt = jnp.minimum(pos + 1, w).astype(F32)
        d = acc / cnt - ext_ref[:, halo:halo + ls, cols]
        o = _dot(d.reshape(tm, POOL_GROUP).astype(BF16), pw_ref[gi])
        pool_ref[:, cols] = (o * ps_ref[:, cols]).astype(pool_ref.dtype)


def _inproj_sample(x2d, w_in_bf, pool_w_bf, pool_scale, state_pad, ls, pos0):
    T = x2d.shape[0]
    nb = ROW_TILE // ls
    tm = nb * ls
    outs = [POOL_WIDTH, NSA_WIDTH, KV_WIDTH, KV_WIDTH, KV_WIDTH, 3 * NSA_HEADS, POOL_WIDTH]
    row = lambda n: pl.BlockSpec((tm, n), lambda i: (i, 0))
    full = lambda a: pl.BlockSpec(a.shape, lambda i: (0,) * a.ndim)
    return pl.pallas_call(
        functools.partial(_inproj_sample_body, nb=nb, ls=ls, pos0=pos0),
        grid=(T // tm,),
        in_specs=[row(D_MODEL), full(w_in_bf), full(pool_w_bf), full(pool_scale),
                  pl.BlockSpec((nb, POOL_STATE + 1, POOL_WIDTH), lambda i: (i, 0, 0))],
        out_specs=[row(n) for n in outs],
        out_shape=[jax.ShapeDtypeStruct((T, n), F32) for n in outs],
        scratch_shapes=[pltpu.VMEM((nb, POOL_STATE + 1 + ls, POOL_WIDTH), F32)],
        compiler_params=_params(("arbitrary",)),
        name="inproj_sample",
    )(x2d, w_in_bf, pool_w_bf, pool_scale, state_pad)


def _matmul_body(x_ref, w_ref, o_ref):
    o_ref[...] = _dot(x_ref[...].astype(BF16), w_ref[...].astype(BF16))


def _matmul(x, w, tn=512):
    M, K = x.shape
    N = w.shape[1]
    return pl.pallas_call(
        _matmul_body,
        grid=(N // tn,),
        in_specs=[pl.BlockSpec((M, K), lambda j: (0, 0)), pl.BlockSpec((K, tn), lambda j: (0, j))],
        out_specs=pl.BlockSpec((M, tn), lambda j: (0, j)),
        out_shape=jax.ShapeDtypeStruct((M, N), F32),
        compiler_params=_params(("arbitrary",)),
        name="mem_kv_proj",
    )(x, w)


def _gelu_tanh(x):
    c = 0.7978845608028654
    return 0.5 * x * (1.0 + jnp.tanh(c * (x + 0.044715 * (x * x * x))))


def _compress(kv_refs, n_chunks, w1_ref, pe_ref, b1_ref, w2_ref, b2_ref):
    lo = _lane_iota(n_chunks) < DH
    quads = CMP_STRIDE // 4
    outs = []
    for c in range(2):
        acc_a = jnp.zeros((2 * n_chunks, CMP_HIDDEN), F32)
        acc_b = jnp.zeros((2 * n_chunks, CMP_HIDDEN), F32)
        for i in range(quads):
            x = [kv_refs[c][pl.ds(4 * i + m, n_chunks, stride=CMP_STRIDE), :] for m in range(4)]
            r = [pltpu.roll(v, DH, 1) for v in x]
            x_g0 = jnp.concatenate([jnp.where(lo, x[0], r[1]), jnp.where(lo, x[2], r[3])], axis=1)
            x_g1 = jnp.concatenate([jnp.where(lo, r[0], x[1]), jnp.where(lo, r[2], x[3])], axis=1)
            xq = jnp.concatenate([x_g0, x_g1], axis=0)
            acc_a = acc_a + _dot((xq + pe_ref[c, i:i + 1, :]).astype(BF16), w1_ref[c, i])
            acc_b = acc_b + _dot((xq + pe_ref[c, quads + i:quads + i + 1, :]).astype(BF16), w1_ref[c, quads + i])
        hid = acc_a + pltpu.roll(acc_b, 2 * n_chunks - 1, 0) + b1_ref[c]
        outs.append(_dot(_gelu_tanh(hid).astype(BF16), w2_ref[c]) + b2_ref[c])
    return outs


def _compress_prompt_body(kv_ref, w1_ref, pe_ref, b1_ref, w2_ref, b2_ref, o_ref, k_ref, v_ref, *, n_chunks):
    k_ref[...] = kv_ref[0, :, :LANES]
    v_ref[...] = kv_ref[0, :, LANES:]
    keys, values = _compress((k_ref, v_ref), n_chunks, w1_ref, pe_ref, b1_ref, w2_ref, b2_ref)
    o_ref[0, 0] = keys
    o_ref[0, 1] = values


def _compress_prompt(kvc, cw):
    B, S, _ = kvc.shape
    n_chunks = S // CMP_STRIDE
    full = lambda a: pl.BlockSpec(a.shape, lambda b: (0,) * a.ndim)
    return pl.pallas_call(
        functools.partial(_compress_prompt_body, n_chunks=n_chunks),
        grid=(B,),
        in_specs=[pl.BlockSpec((1, S, KV_WIDTH), lambda b: (b, 0, 0))] + [full(a) for a in cw],
        out_specs=pl.BlockSpec((1, 2, KVH * n_chunks, LANES), lambda b: (b, 0, 0, 0)),
        out_shape=jax.ShapeDtypeStruct((B, 2, KVH * n_chunks, LANES), F32),
        scratch_shapes=[pltpu.VMEM((S, LANES), F32)] * 2,
        compiler_params=_params(("arbitrary",)),
        name="compress_prompt",
    )(kvc, *cw)


def _compress_weights(cmp_pe, cmp_w1, cmp_b1, cmp_w2, cmp_b2):
    nq = CMP_LEN // 4
    w1 = cmp_w1.reshape(2, nq, 4 * DH, CMP_HIDDEN).astype(BF16)
    pe = cmp_pe.reshape(2, nq, 4 * DH)
    b1 = cmp_b1[:, None, :]
    w2 = jnp.stack([jnp.concatenate([cmp_w2[0], jnp.zeros_like(cmp_w2[0])], axis=1),
                    jnp.concatenate([cmp_w2[1], cmp_w2[1]], axis=1)]).astype(BF16)
    b2 = jnp.stack([jnp.concatenate([cmp_b2[0], jnp.zeros_like(cmp_b2[0])]),
                    jnp.concatenate([cmp_b2[1], cmp_b2[1]])])[:, None, :]
    return w1, pe, b1, w2, b2


def _softmax_rows(s):
    e = jnp.exp(s - jnp.max(s, axis=-1, keepdims=True))
    return e * (1.0 / jnp.sum(e, axis=-1, keepdims=True))


def _lane_iota(n):
    return lax.broadcasted_iota(I32, (n, LANES), 1)


def _key_alibi_cols(pos, lane):
    hi = jnp.left_shift(jnp.right_shift(pos, 6), 6).astype(F32)
    lo = jnp.bitwise_and(pos, SEL_LEN - 1).astype(F32)
    return jnp.where(lane == DH, hi,
                     jnp.where(lane == DH + 1, lo,
                               jnp.where((lane == DH + 2) | (lane == DH + 3), 1.0, 0.0)))


def _key_alibi_rows(pos, sub):
    hi = jnp.left_shift(jnp.right_shift(pos, 6), 6).astype(F32)
    lo = jnp.bitwise_and(pos, SEL_LEN - 1).astype(F32)
    return jnp.where(sub == 0, hi, jnp.where(sub == 1, lo, jnp.where((sub == 2) | (sub == 3), 1.0, 0.0)))


def _halves(x, zero_hi=False):
    lo = _lane_iota(x.shape[0]) < DH
    r = pltpu.roll(x, DH, 1)
    if zero_hi:
        return jnp.where(lo, x, 0.0), jnp.where(lo, r, 0.0)
    return jnp.where(lo, x, r), jnp.where(lo, r, x)


def _fill_queries(qa_ref, qs, q_pos, lq):
    lane = _lane_iota(lq)
    lo_half = lane < DH
    q_hi = jnp.left_shift(jnp.right_shift(q_pos, 7), 7).astype(F32)
    q_lo = jnp.bitwise_and(q_pos, LANES - 1).astype(F32)
    for h in range(NSA_HEADS):
        g, hl = divmod(h, HPG)
        slope = 2.0 ** (-(h + 1))
        slab = qs[:, (h // 2) * LANES:(h // 2 + 1) * LANES]
        if h % 2:
            slab = pltpu.roll(slab, DH, 1)
        ex = jnp.where((lane == DH) | (lane == DH + 1), slope,
                       jnp.where(lane == DH + 2, -slope * q_hi,
                                 jnp.where(lane == DH + 3, -slope * q_lo,
                                           jnp.where(lane == DH + 4, NEG, 0.0))))
        qa_ref[g, hl * lq:(hl + 1) * lq, 0:LANES] = jnp.where(lo_half, slab, ex).astype(BF16)


def _stack4(x):
    return jnp.concatenate([x] * HPG, axis=0)


def _cmp_branch(qa_ref, kc_k, kc_v, q_pos4, lq, n_cmp):
    c_end = lax.broadcasted_iota(I32, (1, n_cmp), 1) * CMP_STRIDE + (CMP_LEN - 1)
    m_c = c_end <= q_pos4
    any_c = (q_pos4 >= CMP_LEN - 1).astype(F32)
    outs, psums = [], []
    for g in range(KVH):
        s = jnp.where(m_c, _dot_nt(qa_ref[g, :, 0:LANES], kc_k[g][...]), NEG)
        p = _softmax_rows(s) * any_c
        outs.append(_dot(p.astype(BF16), kc_v[g][...]))
        psums.append(p[0:lq] + p[lq:2 * lq] + p[2 * lq:3 * lq] + p[3 * lq:4 * lq])
    return outs, psums


def _split3(x):
    hi = x.astype(BF16)
    r1 = x - hi.astype(F32)
    mid = r1.astype(BF16)
    lo = (r1 - mid.astype(F32)).astype(BF16)
    return hi, mid, lo


def _top_blocks_t(imp_ts, pos0, n_blk=SEL_PAD):
    blk = lax.broadcasted_iota(I32, (n_blk, LANES), 0)
    qp_t = pos0 + lax.broadcasted_iota(I32, (n_blk, LANES), 1)
    cur = jnp.right_shift(qp_t, 6)
    forced = (blk == 0) | (blk == cur) | (blk == cur - 1)
    valid = jnp.left_shift(blk, 6) <= qp_t
    v = jnp.concatenate([jnp.where(valid, jnp.where(forced, FORCED_SCORE, t), -1.0) for t in imp_ts], axis=1)
    blk_f = lax.broadcasted_iota(I32, (n_blk, KVH * LANES), 0).astype(F32)
    sel = jnp.zeros((n_blk, KVH * LANES), F32)
    for _ in range(SEL_TOP):
        m = jnp.max(v, axis=0, keepdims=True)
        idx = jnp.min(jnp.where(v == m, blk_f, float(n_blk)), axis=0, keepdims=True)
        hit = blk_f == idx
        sel = jnp.where(hit, 1.0, sel)
        v = jnp.where(hit, -jnp.inf, v)
    out = [jnp.where((sel[:, g * LANES:(g + 1) * LANES] > 0.5) & valid, 0.0, NEG) for g in range(KVH)]
    if n_blk < SEL_PAD:
        out = [jnp.concatenate([b, jnp.full((SEL_PAD - n_blk, LANES), NEG, F32)], axis=0) for b in out]
    return out


def _select_blocks(psums, ovt_ref, pos0, lq, n_blk):
    imp_ts = []
    for g in range(KVH):
        ps = psums[g]
        if lq < LANES:
            ps = jnp.concatenate([ps, jnp.zeros((LANES - lq, ps.shape[1]), F32)], axis=0)
        imp_ts.append(_dot_nt(ovt_ref[0:n_blk, :], ps, precision=HIGHEST))
    return [b.T[:lq].astype(BF16) for b in _top_blocks_t(imp_ts, pos0, n_blk)]


def _store_selbias(qa_ref, selbias, lq):
    for g in range(KVH):
        for hl in range(HPG):
            qa_ref[g, hl * lq:(hl + 1) * lq, LANES:2 * LANES] = selbias[g]


def _combine(gates, eg_ref, o_c, o_s, o_w, lq):
    lo_half = _lane_iota(lq) < DH

    def assemble(per_group):
        slabs = []
        for k in range(NSA_HEADS // 2):
            g, hl = divmod(2 * k, HPG)
            a = per_group[g][hl * lq:(hl + 1) * lq]
            b = per_group[g][(hl + 1) * lq:(hl + 2) * lq]
            slabs.append(jnp.where(lo_half, a, b))
        return jnp.concatenate(slabs, axis=1)

    return (_dot(gates, eg_ref[0]) * assemble(o_c)
            + _dot(gates, eg_ref[1]) * assemble(o_s)
            + _dot(gates, eg_ref[2]) * assemble(o_w))


def _gate_expand():
    r = jnp.arange(3 * NSA_HEADS)
    c = jnp.arange(NSA_WIDTH)
    return jnp.stack([(r[:, None] == 3 * (c[None, :] // DH) + br).astype(F32) for br in range(3)])


def _overlap_t(n_cmp):
    n = jnp.arange(n_cmp)
    s = jnp.arange(SEL_PAD)
    c_first = n * CMP_STRIDE
    c_end = c_first + CMP_LEN - 1
    b_first = s * SEL_LEN
    return ((c_first[None, :] < b_first[:, None] + SEL_LEN) & (c_end[None, :] >= b_first[:, None])).astype(F32)


def _key_rows(kv_f32, pos, invalid=None):
    n = kv_f32.shape[0]
    lane = _lane_iota(n)
    ex = _key_alibi_cols(pos, lane)
    if invalid is not None:
        ex = jnp.where((lane == DH + 4) & invalid, 1.0, ex)
    lo = lane < DH
    return jnp.where(lo, kv_f32, ex), jnp.where(lo, pltpu.roll(kv_f32, DH, 1), ex)


def _block_onehot(pos, n):
    return (lax.broadcasted_iota(I32, (n, SEL_PAD), 1) == jnp.right_shift(pos, 6)).astype(BF16)


def _tile4(x):
    return jnp.concatenate([x] * HPG, axis=1)


def _nsa_prompt_body(q_ref, gate_ref, kc_ref, kvs_ref, w0_ref, w1_ref, w2_ref, w3_ref, w4_ref,
                     ovt_ref, egt_ref, wband_ref, o_ref,
                     kck0, kck1, kcv0, kcv1, ka0, ka1, vt0, vt1, qa0, qa1,
                     sa_ref, sb_ref, m_ref, l_ref, acc_ref, oc_ref, *, seq_len):
    j = pl.program_id(1)
    kc_k, kc_vt, kaug, v_t, qa_t = (kck0, kck1), (kcv0, kcv1), (ka0, ka1), (vt0, vt1), (qa0, qa1)
    n_cmp = kc_ref.shape[2] // KVH
    lq = Q_BLOCK
    cols = HPG * lq

    @pl.when(j == 0)
    def _():
        for g in range(KVH):
            kc_k[g][...] = kc_ref[0, 0, g * n_cmp:(g + 1) * n_cmp, :].astype(BF16)
            kc_vt[g][...] = kc_ref[0, 1, g * n_cmp:(g + 1) * n_cmp, :].T[0:DH].astype(BF16)

        def build(i, _):
            r0 = pl.multiple_of(i * KEY_TILE, KEY_TILE)
            pos = r0 + lax.broadcasted_iota(I32, (KEY_TILE, 1), 0)
            k0, k1 = _key_rows(kvs_ref[0, pl.ds(r0, KEY_TILE), :LANES], pos)
            onehot = _block_onehot(pos, KEY_TILE)
            ka0[pl.ds(r0, KEY_TILE), :] = jnp.concatenate([k0.astype(BF16), onehot], axis=1)
            ka1[pl.ds(r0, KEY_TILE), :] = jnp.concatenate([k1.astype(BF16), onehot], axis=1)
            vt = kvs_ref[0, pl.ds(r0, KEY_TILE), LANES:].T.astype(BF16)
            vt0[:, pl.ds(r0, KEY_TILE)] = vt[0:DH]
            vt1[:, pl.ds(r0, KEY_TILE)] = vt[DH:]
            return 0

        lax.fori_loop(0, seq_len // KEY_TILE, build, 0)

    st = j * Q_BLOCK
    q_pos = st + lax.broadcasted_iota(I32, (1, lq), 1)
    q_pos4 = _tile4(q_pos)

    q_t = (q_ref[...] * ATTN_SCALE).T
    sub = lax.broadcasted_iota(I32, (DH, lq), 0)
    q_hi = jnp.left_shift(jnp.right_shift(q_pos, 7), 7).astype(F32)
    q_lo = jnp.bitwise_and(q_pos, LANES - 1).astype(F32)
    for h in range(NSA_HEADS):
        g, hl = divmod(h, HPG)
        slope = 2.0 ** (-(h + 1))
        ex = jnp.where(sub <= 1, slope,
                       jnp.where(sub == 2, -slope * q_hi,
                                 jnp.where(sub == 3, -slope * q_lo, jnp.where(sub == 4, NEG, 0.0))))
        qa_t[g][0:DH, hl * lq:(hl + 1) * lq] = q_t[h * DH:(h + 1) * DH].astype(BF16)
        qa_t[g][DH:2 * DH, hl * lq:(hl + 1) * lq] = ex.astype(BF16)

    any_c = (q_pos4 >= CMP_LEN - 1).astype(F32)
    blocks_per_sel = SEL_LEN // CMP_STRIDE
    n_variants = -(-n_cmp // LANES)
    last_visible = (st + Q_BLOCK - CMP_LEN) // CMP_STRIDE

    def cmp_and_select(n_c):
        n_b = n_c // blocks_per_sel
        c_end = lax.broadcasted_iota(I32, (n_c, 1), 0) * CMP_STRIDE + (CMP_LEN - 1)
        m_c = c_end <= q_pos4
        ovt_bf = ovt_ref[0:n_b, 0:n_c].astype(BF16)
        imp_ts = []
        for g in range(KVH):
            s = jnp.where(m_c, _dot(kc_k[g][0:n_c, :], qa_t[g][0:LANES, :]), NEG)
            e = jnp.exp(s - jnp.max(s, axis=0, keepdims=True))
            p = e * (any_c / jnp.sum(e, axis=0, keepdims=True))
            oc_ref[g] = _dot(kc_vt[g][:, 0:n_c], p.astype(BF16))
            psum = p[:, 0:lq] + p[:, lq:2 * lq] + p[:, 2 * lq:3 * lq] + p[:, 3 * lq:4 * lq]
            imp_ts.append(sum(_dot(ovt_bf, t) for t in _split3(psum)))
        bias = _top_blocks_t(imp_ts, st, n_b)
        for g in range(KVH):
            for hl in range(HPG):
                qa_t[g][2 * DH:, hl * lq:(hl + 1) * lq] = bias[g].astype(BF16)

    variant = jnp.minimum(jnp.maximum(last_visible, 0) // LANES, n_variants - 1)
    for k in range(n_variants):
        pl.when(variant == k)(functools.partial(cmp_and_select, min(n_cmp, (k + 1) * LANES)))
    o_c = [oc_ref[g] for g in range(KVH)]

    n_tiles = (st + Q_BLOCK + KEY_TILE - 1) // KEY_TILE
    for g in range(KVH):
        m_ref[g] = jnp.full((1, cols), NEG, F32)
        l_ref[g] = jnp.zeros((1, cols), F32)
        acc_ref[g] = jnp.zeros((DH, cols), F32)

    def scores(t, s_ref):
        r0 = pl.multiple_of(t * KEY_TILE, KEY_TILE)
        for g in range(KVH):
            s_ref[g] = _dot(kaug[g][pl.ds(r0, KEY_TILE), :], qa_t[g][...])

    def consume(t, s_ref, masked):
        r0 = pl.multiple_of(t * KEY_TILE, KEY_TILE)
        for g in range(KVH):
            s = s_ref[g]
            if masked:
                k_pos = r0 + lax.broadcasted_iota(I32, (KEY_TILE, 1), 0)
                s = jnp.where(k_pos <= q_pos4, s, NEG)
            m = m_ref[g]
            m_new = jnp.maximum(m, jnp.max(s, axis=0, keepdims=True))
            a = jnp.exp(m - m_new)
            e = jnp.exp(s - m_new)
            m_ref[g] = m_new
            l_ref[g] = a * l_ref[g] + jnp.sum(e, axis=0, keepdims=True)
            acc_ref[g] = a * acc_ref[g] + _dot(v_t[g][:, pl.ds(r0, KEY_TILE)], e.astype(BF16))

    scores(0, sa_ref)
    n_pairs = (n_tiles - 1) // 2

    def pair(u, _):
        scores(2 * u + 1, sb_ref)
        consume(2 * u, sa_ref, False)
        scores(2 * u + 2, sa_ref)
        consume(2 * u + 1, sb_ref, False)
        return 0

    lax.fori_loop(0, n_pairs, pair, 0)
    odd_tail = (n_tiles - 1) - 2 * n_pairs == 1

    @pl.when(odd_tail)
    def _():
        scores(n_tiles - 1, sb_ref)
        consume(n_tiles - 2, sa_ref, False)
        consume(n_tiles - 1, sb_ref, True)

    @pl.when(jnp.logical_not(odd_tail))
    def _():
        consume(n_tiles - 1, sa_ref, True)

    o_s = [acc_ref[g] * (1.0 / l_ref[g]) for g in range(KVH)]

    band = jnp.concatenate([w0_ref[0], w1_ref[0], w2_ref[0], w3_ref[0], w4_ref[0]], axis=0)
    n_win = band.shape[0]
    w_pos_col = st - WINDOW + lax.broadcasted_iota(I32, (n_win, 1), 0)
    kw_k = _key_rows(band[:, :LANES], jnp.maximum(w_pos_col, 0), invalid=w_pos_col < 0)
    vw_t = band[:, LANES:].T.astype(BF16)
    band_bias = _tile4(wband_ref[...])
    o_w = []
    for g in range(KVH):
        s = _dot(kw_k[g].astype(BF16), qa_t[g][0:LANES, :]) + band_bias
        e = jnp.exp(s - jnp.max(s, axis=0, keepdims=True))
        o_w.append(_dot(vw_t[g * DH:(g + 1) * DH], e.astype(BF16)) * (1.0 / jnp.sum(e, axis=0, keepdims=True)))

    def heads(per_group):
        return jnp.concatenate([per_group[h // HPG][:, (h % HPG) * lq:(h % HPG + 1) * lq]
                                for h in range(NSA_HEADS)], axis=0)

    gates = gate_ref[...]
    out_t = (_dot_nt(egt_ref[0], gates) * heads(o_c)
             + _dot_nt(egt_ref[1], gates) * heads(o_s)
             + _dot_nt(egt_ref[2], gates) * heads(o_w))
    o_ref[...] = out_t.T.astype(o_ref.dtype)


def _gate_expand_t():
    r = jnp.arange(3 * NSA_HEADS)
    c = jnp.arange(NSA_WIDTH)
    return jnp.stack([(3 * (c[:, None] // DH) + br == r[None, :]).astype(F32) for br in range(3)])


def _nsa_prompt(q, gates, kc, kvs, kvw):
    B, S, _ = kvs.shape
    nqb = S // Q_BLOCK
    n_cmp = kc.shape[2] // KVH
    ovt = _overlap_t(n_cmp)
    egt = _gate_expand_t()
    n_band = WINDOW // Q_BLOCK + 1
    d_band = jnp.arange(Q_BLOCK)[None, :] + WINDOW - jnp.arange(n_band * Q_BLOCK)[:, None]
    wband = jnp.where((d_band >= 0) & (d_band < WINDOW), 0.0, NEG).astype(F32)

    def band_spec(i):
        return pl.BlockSpec((1, Q_BLOCK, KV_WIDTH),
                            lambda b, j, i=i: (b, jnp.maximum(j - (n_band - 1) + i, 0), 0))

    full = lambda a: pl.BlockSpec(a.shape, lambda b, j: (0,) * a.ndim)
    return pl.pallas_call(
        functools.partial(_nsa_prompt_body, seq_len=S),
        grid=(B, nqb),
        in_specs=[pl.BlockSpec((Q_BLOCK, NSA_WIDTH), lambda b, j: (b * nqb + j, 0)),
                  pl.BlockSpec((Q_BLOCK, 3 * NSA_HEADS), lambda b, j: (b * nqb + j, 0)),
                  pl.BlockSpec((1,) + kc.shape[1:], lambda b, j: (b, 0, 0, 0)),
                  pl.BlockSpec((1, S, KV_WIDTH), lambda b, j: (b, 0, 0))]
                 + [band_spec(i) for i in range(n_band)] + [full(ovt), full(egt), full(wband)],
        out_specs=pl.BlockSpec((Q_BLOCK, NSA_WIDTH), lambda b, j: (b * nqb + j, 0)),
        out_shape=jax.ShapeDtypeStruct((B * S, NSA_WIDTH), BF16),
        scratch_shapes=[pltpu.VMEM((n_cmp, LANES), BF16)] * 2
                       + [pltpu.VMEM((DH, n_cmp), BF16)] * 2
                       + [pltpu.VMEM((S, 2 * LANES), BF16)] * 2
                       + [pltpu.VMEM((DH, S), BF16)] * 2
                       + [pltpu.VMEM((2 * LANES, HPG * Q_BLOCK), BF16)] * 2
                       + [pltpu.VMEM((KVH, KEY_TILE, HPG * Q_BLOCK), F32)] * 2
                       + [pltpu.VMEM((KVH, 1, HPG * Q_BLOCK), F32)] * 2
                       + [pltpu.VMEM((KVH, DH, HPG * Q_BLOCK), F32)] * 2,
        compiler_params=_params(("arbitrary", "arbitrary")),
        name="nsa_prompt",
    )(q, gates, kc, kvs, *([kvw] * n_band), ovt, egt, wband)


def _nsa_sample_body(pt_ref, q_ref, gate_ref, kvs_new_ref, win_ref, kvw_new_ref, *rest,
                     n_pages, ls, past_len):
    cmp_pages = rest[:n_pages]
    slc_pages = rest[n_pages:2 * n_pages]
    (w1_ref, pe_ref, b1_ref, w2_ref, b2_ref, ovt_ref, eg_ref, o_ref, wout_ref,
     full_k, full_v, kck0, kck1, kcv0, kcv1, kt0, kt1, vt0, vt1, wkt0, wkt1, qa_ref) = rest[2 * n_pages:]
    del pt_ref
    kc_k, kc_v, kaug_t, v_t, wk_t = (kck0, kck1), (kcv0, kcv1), (kt0, kt1), (vt0, vt1), (wkt0, wkt1)
    n_cmp = past_len // CMP_STRIDE
    w_rows = win_ref.shape[4]
    lq = BF16_ROWS
    rows = HPG * lq
    w_start = past_len - w_rows

    @pl.when(pl.program_id(0) == 0)
    def _():
        sub = lax.broadcasted_iota(I32, (DH, past_len), 0)
        pos = lax.broadcasted_iota(I32, (1, past_len), 1)
        ex = _key_alibi_rows(pos, sub).astype(BF16)
        onehot = (lax.broadcasted_iota(I32, (SEL_PAD, past_len), 0) == jnp.right_shift(pos, 6)).astype(BF16)
        subw = lax.broadcasted_iota(I32, (DH, w_rows), 0)
        exw = _key_alibi_rows(w_start + lax.broadcasted_iota(I32, (1, w_rows), 1), subw).astype(BF16)
        for g in range(KVH):
            kaug_t[g][DH:2 * DH, :] = ex
            kaug_t[g][2 * DH:, :] = onehot
            wk_t[g][DH:, :] = exw

    for p in range(n_pages):
        cols = slice(p * PAGE_SIZE, (p + 1) * PAGE_SIZE)
        full_k[cols, :] = cmp_pages[p][0, 0].reshape(2 * DH, PAGE_SIZE).T
        full_v[cols, :] = cmp_pages[p][0, 1].reshape(2 * DH, PAGE_SIZE).T
        for g in range(KVH):
            kaug_t[g][0:DH, cols] = slc_pages[p][0, 0, g].astype(BF16)
            vt = slc_pages[p][0, 1, g].astype(BF16)
            v_t[g][0:DH, cols] = vt
            v_t[g][DH:, cols] = vt
    for g in range(KVH):
        wk_t[g][0:DH, :] = win_ref[0, 0, g].astype(BF16)

    keys_c, values_c = _compress((full_k, full_v), n_cmp, w1_ref, pe_ref, b1_ref, w2_ref, b2_ref)
    for g in range(KVH):
        kc_k[g][...] = keys_c[g * n_cmp:(g + 1) * n_cmp].astype(BF16)
        kc_v[g][...] = values_c[g * n_cmp:(g + 1) * n_cmp].astype(BF16)

    pad_q = jnp.zeros((lq - ls, NSA_WIDTH), F32)
    q_pos = past_len + lax.broadcasted_iota(I32, (lq, 1), 0)
    q_pos4 = _stack4(q_pos)
    _fill_queries(qa_ref, jnp.concatenate([q_ref[0] * ATTN_SCALE, pad_q], axis=0), q_pos, lq)
    gates = jnp.concatenate([gate_ref[0], jnp.zeros((lq - ls, 3 * NSA_HEADS), F32)], axis=0)

    o_c, psums = _cmp_branch(qa_ref, kc_k, kc_v, q_pos4, lq, n_cmp)
    n_sel = -(-(past_len + lq) // SEL_LEN)
    n_blk = -(-n_sel // 8) * 8
    _store_selbias(qa_ref, _select_blocks(psums, ovt_ref, past_len, lq, n_blk), lq)

    pad_k = jnp.zeros((LANES - ls, KV_WIDTH), F32)
    new_pos_col = past_len + lax.broadcasted_iota(I32, (LANES, 1), 0)
    new_pos = past_len + lax.broadcasted_iota(I32, (1, LANES), 1)
    new_s = jnp.concatenate([kvs_new_ref[0], pad_k], axis=0)
    new_w = jnp.concatenate([kvw_new_ref[0], pad_k], axis=0)
    ks_new = _key_rows(new_s[:, :LANES], new_pos_col)
    vs_new = _halves(new_s[:, LANES:])
    kw_new = _key_rows(new_w[:, :LANES], new_pos_col)
    vw_new = _halves(new_w[:, LANES:])
    onehot_new = _block_onehot(new_pos_col, LANES)
    causal_new = q_pos4 >= new_pos

    o_s, o_w = [], []
    d_past = q_pos4 - (w_start + lax.broadcasted_iota(I32, (1, w_rows), 1))
    m_past = (d_past >= 0) & (d_past < WINDOW)
    d_new = q_pos4 - new_pos
    m_new = (d_new >= 0) & (d_new < WINDOW)
    for g in range(KVH):
        s_past = _dot(qa_ref[g], kaug_t[g][...])
        k_new = jnp.concatenate([ks_new[g].astype(BF16), onehot_new], axis=1)
        s_new = jnp.where(causal_new, _dot_nt(qa_ref[g], k_new), NEG)
        m = jnp.maximum(jnp.max(s_past, axis=-1, keepdims=True), jnp.max(s_new, axis=-1, keepdims=True))
        e_past = jnp.exp(s_past - m)
        e_new = jnp.exp(s_new - m)
        den = jnp.sum(e_past, axis=-1, keepdims=True) + jnp.sum(e_new, axis=-1, keepdims=True)
        acc = _dot_nt(e_past.astype(BF16), v_t[g][...]) + _dot(e_new.astype(BF16), vs_new[g].astype(BF16))
        o_s.append(acc / den)

        sw_past = jnp.where(m_past, _dot(qa_ref[g, :, 0:LANES], wk_t[g][...]), NEG)
        sw_new = jnp.where(m_new, _dot_nt(qa_ref[g, :, 0:LANES], kw_new[g].astype(BF16)), NEG)
        m = jnp.maximum(jnp.max(sw_past, axis=-1, keepdims=True), jnp.max(sw_new, axis=-1, keepdims=True))
        e_past = jnp.exp(sw_past - m)
        e_new = jnp.exp(sw_new - m)
        den = jnp.sum(e_past, axis=-1, keepdims=True) + jnp.sum(e_new, axis=-1, keepdims=True)
        vw = win_ref[0, 1, g].astype(BF16)
        vw2 = jnp.concatenate([vw, vw], axis=0)
        acc = _dot_nt(e_past.astype(BF16), vw2) + _dot(e_new.astype(BF16), vw_new[g].astype(BF16))
        o_w.append(acc / den)

    o_ref[0] = _combine(gates, eg_ref, o_c, o_s, o_w, lq)[:ls]

    keep = w_rows - ls
    tail_lane = lax.broadcasted_iota(I32, (DH, LANES), 1) >= keep % LANES
    for c in range(2):
        new_t = pltpu.roll(new_w[:, c * LANES:(c + 1) * LANES].T, keep % LANES, 1)
        for g in range(KVH):
            shifted = pltpu.roll(win_ref[0, c, g], keep, 1)
            last = jnp.where(tail_lane, new_t[g * DH:(g + 1) * DH], shifted[:, w_rows - LANES:])
            wout_ref[0, c, g] = jnp.concatenate([shifted[:, :w_rows - LANES], last], axis=1)


def _nsa_sample(q, gates, kvs_new, kvw_new, win_t, cmp_t, slc_t, page_table, cw, past_len):
    Bd, ls, _ = q.shape
    n_pages = page_table.shape[1]
    n_cmp = past_len // CMP_STRIDE
    ovt = _overlap_t(n_cmp)
    eg = _gate_expand()
    w_rows = win_t.shape[4]

    per_b = lambda a: pl.BlockSpec((1,) + a.shape[1:], lambda b, pt: (b,) + (0,) * (a.ndim - 1))
    full = lambda a: pl.BlockSpec(a.shape, lambda b, pt: (0,) * a.ndim)
    page = lambda p: pl.BlockSpec((1, 2, KVH, DH, PAGE_SIZE), lambda b, pt, p=p: (pt[b, p], 0, 0, 0, 0))
    grid_spec = pltpu.PrefetchScalarGridSpec(
        num_scalar_prefetch=1,
        grid=(Bd,),
        in_specs=[per_b(q), per_b(gates), per_b(kvs_new), per_b(win_t), per_b(kvw_new)]
                 + [page(p) for p in range(n_pages)] * 2
                 + [full(a) for a in cw] + [full(ovt), full(eg)],
        out_specs=[pl.BlockSpec((1, ls, NSA_WIDTH), lambda b, pt: (b, 0, 0)), per_b(win_t)],
        scratch_shapes=[pltpu.VMEM((past_len, LANES), F32)] * 2
                       + [pltpu.VMEM((n_cmp, LANES), BF16)] * 4
                       + [pltpu.VMEM((2 * LANES, past_len), BF16)] * 2
                       + [pltpu.VMEM((LANES, past_len), BF16)] * 2
                       + [pltpu.VMEM((LANES, w_rows), BF16)] * 2
                       + [pltpu.VMEM((KVH, HPG * BF16_ROWS, 2 * LANES), BF16)],
    )
    return pl.pallas_call(
        functools.partial(_nsa_sample_body, n_pages=n_pages, ls=ls, past_len=past_len),
        grid_spec=grid_spec,
        out_shape=[jax.ShapeDtypeStruct((Bd, ls, NSA_WIDTH), F32), jax.ShapeDtypeStruct(win_t.shape, F32)],
        compiler_params=_params(("arbitrary",)),
        name="nsa_sample",
    )(page_table, q, gates, kvs_new, win_t, kvw_new,
      *([cmp_t] * n_pages), *([slc_t] * n_pages), *cw, ovt, eg)


def _fin1_body(h_ref, pool_ref, nsa_ref, wo_ref, g_ref, b_ref, wq_ref, h1_ref, qm_ref):
    mix = (_dot(pool_ref[...].astype(BF16), wo_ref[0:POOL_WIDTH, :])
           + _dot(nsa_ref[...].astype(BF16), wo_ref[POOL_WIDTH:, :]))
    h1 = _layer_norm(DN_ALPHA * h_ref[...] + mix, g_ref[...], b_ref[...])
    h1_ref[...] = h1
    qm_ref[...] = (_dot(h1.astype(BF16), wq_ref[...]) * (MEM_HEAD_DIM ** -0.5)).astype(qm_ref.dtype)


def _fin1(h, pool_o, nsa_o, w_out_bf, g, b, wq_bf, q_dtype):
    T = h.shape[0]
    tm = ROW_TILE
    row = lambda n: pl.BlockSpec((tm, n), lambda i: (i, 0))
    full = lambda a: pl.BlockSpec(a.shape, lambda i: (0,) * a.ndim)
    return pl.pallas_call(
        _fin1_body,
        grid=(T // tm,),
        in_specs=[row(D_MODEL), row(POOL_WIDTH), row(NSA_WIDTH), full(w_out_bf), full(g), full(b), full(wq_bf)],
        out_specs=[row(D_MODEL), row(D_MODEL)],
        out_shape=[jax.ShapeDtypeStruct((T, D_MODEL), F32), jax.ShapeDtypeStruct((T, D_MODEL), q_dtype)],
        compiler_params=_params(("arbitrary",)),
        name="out_proj_ln1",
    )(h, pool_o, nsa_o, w_out_bf, g, b, wq_bf)


def _memattn_body(q_ref, kv_ref, o_ref):
    width = MEM_HEADS * MEM_HEAD_DIM
    for h in range(MEM_HEADS):
        cols = slice(h * MEM_HEAD_DIM, (h + 1) * MEM_HEAD_DIM)
        qh = q_ref[0, :, cols].astype(BF16)
        kh = kv_ref[0, :, cols].astype(BF16)
        vh = kv_ref[0, :, width + h * MEM_HEAD_DIM:width + (h + 1) * MEM_HEAD_DIM].astype(BF16)
        s = _dot_nt(qh, kh)
        e = jnp.exp(s - jnp.max(s, axis=-1, keepdims=True))
        o = _dot(e.astype(BF16), vh) * (1.0 / jnp.sum(e, axis=-1, keepdims=True))
        o_ref[0, :, cols] = o.astype(o_ref.dtype)


def _memattn(qm, mem_kv, tq):
    nb, L, W = qm.shape
    return pl.pallas_call(
        _memattn_body,
        grid=(nb, L // tq),
        in_specs=[pl.BlockSpec((1, tq, W), lambda b, t: (b, t, 0)),
                  pl.BlockSpec((1, MEM_LEN, 2 * W), lambda b, t: (b, 0, 0))],
        out_specs=pl.BlockSpec((1, tq, W), lambda b, t: (b, t, 0)),
        out_shape=jax.ShapeDtypeStruct((nb, L, W), BF16),
        compiler_params=_params(("arbitrary", "arbitrary")),
        name="mem_attn",
    )(qm, mem_kv)


def _memattn_few_body(q_ref, kv_ref, o_ref):
    lq = q_ref.shape[1]
    n_keys = MEM_LEN * MEM_HEADS
    assert lq & (lq - 1) == 0 and MEM_HEADS & (MEM_HEADS - 1) == 0
    shape = (MEM_HEADS * lq, n_keys)
    own_head = (jnp.bitwise_and(lax.broadcasted_iota(I32, shape, 1), MEM_HEADS - 1)
                == jnp.right_shift(lax.broadcasted_iota(I32, shape, 0), lq.bit_length() - 1))
    for i in range(q_ref.shape[0]):
        q = q_ref[i]
        qs = jnp.concatenate([q[:, h * MEM_HEAD_DIM:(h + 1) * MEM_HEAD_DIM] for h in range(MEM_HEADS)], axis=0)
        k = kv_ref[i, :, 0, :, :].reshape(n_keys, MEM_HEAD_DIM).astype(BF16)
        v = kv_ref[i, :, 1, :, :].reshape(n_keys, MEM_HEAD_DIM).astype(BF16)
        s = jnp.where(own_head, _dot_nt(qs.astype(BF16), k), NEG)
        e = jnp.exp(s - jnp.max(s, axis=-1, keepdims=True))
        o = _dot(e.astype(BF16), v) * (1.0 / jnp.sum(e, axis=-1, keepdims=True))
        for h in range(MEM_HEADS):
            o_ref[i, :, h * MEM_HEAD_DIM:(h + 1) * MEM_HEAD_DIM] = o[h * lq:(h + 1) * lq]


def _memattn_few(qm, mem_kv):
    nb, lq, W = qm.shape
    per_step = MEM_PER_STEP if nb % MEM_PER_STEP == 0 else 1
    return pl.pallas_call(
        _memattn_few_body,
        grid=(nb // per_step,),
        in_specs=[pl.BlockSpec((per_step, lq, W), lambda b: (b, 0, 0)),
                  pl.BlockSpec((per_step, MEM_LEN, 2, MEM_HEADS, MEM_HEAD_DIM), lambda b: (b, 0, 0, 0, 0))],
        out_specs=pl.BlockSpec((per_step, lq, W), lambda b: (b, 0, 0)),
        out_shape=jax.ShapeDtypeStruct((nb, lq, W), F32),
        compiler_params=_params(("arbitrary",)),
        name="mem_attn_few",
    )(qm, mem_kv)


def _fin2_body(cnt0_ref, h1_ref, o_ref, wo_ref, g_ref, b_ref, rw_ref, rb_ref, *rest, n_own):
    h2_ref, te_ref, tg_ref, cnt_ref, run_ref = rest[-5:]
    step = pl.program_id(0)

    @pl.when(step >= n_own)
    def _():
        h2_ref[...] = jnp.zeros(h2_ref.shape, F32)
        te_ref[...] = jnp.zeros(te_ref.shape, I32)
        tg_ref[...] = jnp.zeros(tg_ref.shape, F32)

    @pl.when(step < n_own)
    def _():
        _fin2_rows(cnt0_ref, h1_ref, o_ref, wo_ref, g_ref, b_ref, rw_ref, rb_ref,
                   h2_ref, te_ref, tg_ref, cnt_ref, run_ref)


def _fin2_rows(cnt0_ref, h1_ref, o_ref, wo_ref, g_ref, b_ref, rw_ref, rb_ref,
               h2_ref, te_ref, tg_ref, cnt_ref, run_ref):
    tm = h1_ref.shape[0]

    @pl.when(pl.program_id(0) == 0)
    def _():
        run_ref[...] = cnt0_ref[...]

    a = _dot(o_ref[...].astype(BF16), wo_ref[...])
    h2 = _layer_norm(DN_ALPHA * h1_ref[...] + a, g_ref[...], b_ref[...])
    h2_ref[...] = h2
    a_hi, a_lo, _ = _split3(h2)
    w_hi, w_lo, _ = _split3(rw_ref[...])
    logits = _dot(a_hi, w_hi) + (_dot(a_hi, w_lo) + _dot(a_lo, w_hi)) + rb_ref[...]
    e_iota = lax.broadcasted_iota(I32, (tm, N_EXPERTS), 1).astype(F32)
    lane = lax.broadcasted_iota(I32, (tm, LANES), 1)
    te = jnp.zeros((tm, LANES), F32)
    tv = jnp.full((tm, LANES), NEG, F32)
    work = logits
    chosen = []
    for k in range(TOP_K):
        m = jnp.max(work, axis=-1, keepdims=True)
        idx = jnp.min(jnp.where(work == m, e_iota, float(N_EXPERTS)), axis=-1, keepdims=True)
        hit = e_iota == idx
        chosen.append(hit)
        te = jnp.where(lane == k, idx, te)
        tv = jnp.where(lane == k, m, tv)
        work = jnp.where(hit, -jnp.inf, work)
    member = sum(c.astype(F32) for c in chosen)
    earlier = (lax.broadcasted_iota(I32, (tm, tm), 0) > lax.broadcasted_iota(I32, (tm, tm), 1)).astype(BF16)
    before = _dot(earlier, member.astype(BF16)) + run_ref[...]
    for k in range(TOP_K):
        rank = jnp.sum(jnp.where(chosen[k], before, 0.0), axis=-1, keepdims=True)
        te = jnp.where(lane == TOP_K + k, rank, te)
    run_ref[...] = run_ref[...] + jnp.sum(member, axis=0, keepdims=True)
    cnt_ref[...] = run_ref[...]
    ex = jnp.exp(tv - jnp.max(tv, axis=-1, keepdims=True))
    te_ref[...] = te.astype(I32)
    tg_ref[...] = ex / jnp.sum(ex, axis=-1, keepdims=True)


def _fin2(cnt0, h1, o, wo_bf, g, b, rw, rb, total_rows, row_offset=0, into=None):
    T = h1.shape[0]
    tm = ROW_TILE
    blk0 = row_offset // tm
    n_own = T // tm
    n_steps = n_own if into is not None else (total_rows - row_offset) // tm
    row = lambda n: pl.BlockSpec((tm, n), lambda i: (jnp.minimum(i, n_own - 1), 0))
    out_row = lambda n: pl.BlockSpec((tm, n), lambda i: (i + blk0, 0))
    full = lambda a: pl.BlockSpec(a.shape, lambda i: (0,) * a.ndim)
    ins = [cnt0, h1, o, wo_bf, g, b, rw, rb]
    in_specs = [full(cnt0), row(D_MODEL), row(D_MODEL), full(wo_bf), full(g), full(b), full(rw), full(rb)]
    aliases = {}
    if into is not None:
        aliases = {len(ins) + k: k for k in range(len(into))}
        in_specs = in_specs + [pl.BlockSpec(memory_space=pl.ANY)] * len(into)
        ins = ins + list(into)
    return pl.pallas_call(
        functools.partial(_fin2_body, n_own=n_own),
        grid=(n_steps,),
        in_specs=in_specs,
        out_specs=[out_row(D_MODEL), out_row(LANES), out_row(LANES), full(cnt0)],
        out_shape=[jax.ShapeDtypeStruct((total_rows, D_MODEL), F32), jax.ShapeDtypeStruct((total_rows, LANES), I32),
                   jax.ShapeDtypeStruct((total_rows, LANES), F32), jax.ShapeDtypeStruct(cnt0.shape, F32)],
        scratch_shapes=[pltpu.VMEM(cnt0.shape, F32)],
        input_output_aliases=aliases,
        compiler_params=_params(("arbitrary",)),
        name="mem_out_ln2_router",
    )(*ins)


def _moe_body(ut_ref, ue_ref, nu_ref, rs_ref, re_ref, x_ref, wgu_ref, bgu_ref, wdn_ref, bdn_ref,
              y_ref, wgu_bf, wdn_bf):
    u = pl.program_id(0)
    bk = x_ref.shape[0]
    e = ue_ref[u]
    tile = ut_ref[u]
    prev = jnp.maximum(u - 1, 0)

    @pl.when((u == 0) | (e != ue_ref[prev]))
    def _():
        wgu_bf[...] = wgu_ref[0].astype(BF16)
        wdn_bf[...] = wdn_ref[0].astype(BF16)

    @pl.when(u < nu_ref[0])
    def _():
        x = x_ref[...].astype(BF16)
        g = _dot(x, wgu_bf[:, :D_FF]) + bgu_ref[0, :, :D_FF]
        v = _dot(x, wgu_bf[:, D_FF:]) + bgu_ref[0, :, D_FF:]
        g = jnp.minimum(g, SWIGLU_LIMIT)
        v = jnp.clip(v, -SWIGLU_LIMIT, SWIGLU_LIMIT)
        a = g * (1.0 / (1.0 + jnp.exp(-SWIGLU_ALPHA * g))) * (v + 1.0)
        y = _dot(a.astype(BF16), wdn_bf[...]) + bdn_ref[0]
        row = tile * bk + lax.broadcasted_iota(I32, (bk, 1), 0)
        mine = (row >= rs_ref[e]) & (row < re_ref[e])
        y = jnp.where(mine, y, 0.0)

        @pl.when((u == 0) | (tile != ut_ref[prev]))
        def _():
            y_ref[...] = y

        @pl.when((u > 0) & (tile == ut_ref[prev]))
        def _():
            y_ref[...] = y_ref[...] + y


def _moe_gmm(x_rows, units, w_gu, b_gu, w_dn, b_dn):
    N = x_rows.shape[0]
    bk = MOE_ROWS
    unit_tile, unit_e, n_units, r_start, r_end = units
    grid_spec = pltpu.PrefetchScalarGridSpec(
        num_scalar_prefetch=5,
        grid=(unit_tile.shape[0],),
        in_specs=[pl.BlockSpec((bk, D_MODEL), lambda u, ut, ue, *_: (ut[u], 0)),
                  pl.BlockSpec((1, D_MODEL, 2 * D_FF), lambda u, ut, ue, *_: (ue[u], 0, 0)),
                  pl.BlockSpec((1, 1, 2 * D_FF), lambda u, ut, ue, *_: (ue[u], 0, 0)),
                  pl.BlockSpec((1, D_FF, D_MODEL), lambda u, ut, ue, *_: (ue[u], 0, 0)),
                  pl.BlockSpec((1, 1, D_MODEL), lambda u, ut, ue, *_: (ue[u], 0, 0))],
        out_specs=pl.BlockSpec((bk, D_MODEL), lambda u, ut, ue, *_: (ut[u], 0)),
        scratch_shapes=[pltpu.VMEM((D_MODEL, 2 * D_FF), BF16), pltpu.VMEM((D_FF, D_MODEL), BF16)],
    )
    return pl.pallas_call(
        _moe_body,
        grid_spec=grid_spec,
        out_shape=jax.ShapeDtypeStruct((N, D_MODEL), F32),
        compiler_params=_params(("arbitrary",)),
        name="moe_experts",
    )(unit_tile, unit_e, n_units, r_start, r_end, x_rows, w_gu, b_gu, w_dn, b_dn)


FLAT_BITS = 17


def _moe_routing(te, counts):
    bk = MOE_ROWS
    T = te.shape[0]
    N = T * TOP_K
    assert N % bk == 0 and N <= (1 << FLAT_BITS)
    experts = jnp.arange(N_EXPERTS, dtype=I32)
    top_e = te[:, :TOP_K]
    r_end = jnp.cumsum(counts).astype(I32)
    r_start = r_end - counts
    onehot = top_e[:, :, None] == experts[None, None, :]
    pos = jnp.sum(jnp.where(onehot, r_start[None, None, :], 0), axis=-1) + te[:, TOP_K:2 * TOP_K]
    key = jnp.left_shift(top_e.reshape(-1), FLAT_BITS) + jnp.arange(N, dtype=I32)
    key_s = lax.sort(key)
    tok_s = jnp.right_shift(jnp.bitwise_and(key_s, (1 << FLAT_BITS) - 1), 2)
    first = r_start // bk
    last = (r_end - 1) // bk
    n_e = jnp.where(counts > 0, last - first + 1, 0)
    u_end = jnp.cumsum(n_e).astype(I32)
    u_start = u_end - n_e
    n_units = u_end[-1]
    u = jnp.minimum(jnp.arange(N // bk + N_EXPERTS - 1, dtype=I32), n_units - 1)
    unit_e = jnp.sum((u[:, None] >= u_end[None, :]).astype(I32), axis=1)
    unit_tile = u + jnp.sum(jnp.where(unit_e[:, None] == experts[None, :], (first - u_start)[None, :], 0), axis=1)
    return pos, tok_s, (unit_tile, unit_e, n_units.reshape(1), r_start, r_end)


def _fin3_body(h2_ref, tg_ref, y0_ref, y1_ref, y2_ref, y3_ref, g_ref, b_ref, o_ref):
    gate = lambda k: tg_ref[:, k:k + 1]
    y = ((gate(0) * y0_ref[...] + gate(1) * y1_ref[...])
         + (gate(2) * y2_ref[...] + gate(3) * y3_ref[...]))
    o_ref[...] = _layer_norm(DN_ALPHA * h2_ref[...] + y, g_ref[...], b_ref[...])


def _fin3(h2, tg, ys, g, b, row_offset, T):
    tm = ROW_TILE
    blk0 = row_offset // tm
    row = lambda n: pl.BlockSpec((tm, n), lambda i: (i, 0))
    full = lambda a: pl.BlockSpec(a.shape, lambda i: (0,) * a.ndim)
    return pl.pallas_call(
        _fin3_body,
        grid=(T // tm,),
        in_specs=[pl.BlockSpec((tm, D_MODEL), lambda i: (i + blk0, 0)), pl.BlockSpec((tm, LANES), lambda i: (i + blk0, 0))]
                 + [pl.BlockSpec((tm, D_MODEL), lambda i: (i + blk0, 0))] * TOP_K + [full(g), full(b)],
        out_specs=row(D_MODEL),
        out_shape=jax.ShapeDtypeStruct((T, D_MODEL), F32),
        compiler_params=_params(("arbitrary",)),
        name="combine_ln3",
    )(h2, tg, *ys, g, b)


def kernel(x_prompt, x_sample, cache_cmp_kv, cache_slc_kv, state_win_kv, state_pool, cache_mem_kv, page_table,
           mem_prompt, w_in, pool_w, pool_scale, cmp_pe, cmp_w1, cmp_b1, cmp_w2, cmp_b2, w_out, ln1_g, ln1_b,
           mem_wq, mem_wkv, mem_wo, ln2_g, ln2_b, router_w, router_b, exp_w_gu, exp_b_gu, exp_w_dn, exp_b_dn,
           ln3_g, ln3_b):
    Bp, S, D = x_prompt.shape
    Bd, Ls, _ = x_sample.shape
    Tp, Ts = Bp * S, Bd * Ls
    l = 0
    w_in_bf = w_in[l].astype(BF16)
    pool_w_bf = pool_w[l].astype(BF16)
    ps = pool_scale[l][None, :]
    cw = _compress_weights(cmp_pe[l], cmp_w1[l], cmp_b1[l], cmp_w2[l], cmp_b2[l])
    w_out_bf = w_out[l].astype(BF16)
    wq_bf = mem_wq[l].astype(BF16)
    wo_bf = mem_wo[l].astype(BF16)
    vec = lambda a: a[l][None, :]

    up, qp, kvc_p, kvs_p, kvw_p, gp, pool_p, kvc_t, kvs_t = _inproj_prompt(
        x_prompt.reshape(Tp, D), w_in_bf, pool_w_bf, ps, S)
    kc_p = _compress_prompt(kvc_p.reshape(Bp, S, KV_WIDTH), cw)
    nsa_p = _nsa_prompt(qp, gp, kc_p, kvs_p.reshape(Bp, S, KV_WIDTH), kvw_p.reshape(Bp, S, KV_WIDTH))
    mem_kv_p = _matmul(mem_prompt.reshape(Bp * MEM_LEN, D), mem_wkv[l]).reshape(Bp, MEM_LEN, 2 * D)
    h1_p, qm_p = _fin1(x_prompt.reshape(Tp, D), pool_p, nsa_p, w_out_bf, vec(ln1_g), vec(ln1_b), wq_bf, BF16)
    om_p = _memattn(qm_p.reshape(Bp, S, D), mem_kv_p, ROW_TILE).reshape(Tp, D)
    T = Tp + Ts
    *routed_p, cnt_p = _fin2(jnp.zeros((1, N_EXPERTS), F32), h1_p, om_p, wo_bf, vec(ln2_g), vec(ln2_b),
                             router_w[l], vec(router_b), total_rows=T)

    state_pad = jnp.pad(state_pool[l], ((0, 0), (1, 0), (0, 0)))
    us, qs, kvc_s, kvs_s, kvw_s, gs, pool_s = _inproj_sample(
        x_sample.reshape(Ts, D), w_in_bf, pool_w_bf, ps, state_pad, Ls, PAST_LEN)
    feature_major = lambda a: jnp.transpose(a, (0, 2, 3, 4, 1))
    nsa_s, win_next = _nsa_sample(qs.reshape(Bd, Ls, NSA_WIDTH), gs.reshape(Bd, Ls, 3 * NSA_HEADS),
                        kvs_s.reshape(Bd, Ls, KV_WIDTH), kvw_s.reshape(Bd, Ls, KV_WIDTH),
                        feature_major(state_win_kv[l]), feature_major(cache_cmp_kv[l]),
                        feature_major(cache_slc_kv[l]), page_table, cw, PAST_LEN)
    h1_s, qm_s = _fin1(x_sample.reshape(Ts, D), pool_s, nsa_s.reshape(Ts, NSA_WIDTH), w_out_bf,
                       vec(ln1_g), vec(ln1_b), wq_bf, F32)
    om_s = _memattn_few(qm_s.reshape(Bd, Ls, D), cache_mem_kv[l]).reshape(Ts, D)
    h2, te, tg, cnt_s = _fin2(cnt_p, h1_s, om_s, wo_bf, vec(ln2_g), vec(ln2_b), router_w[l], vec(router_b),
                              total_rows=T, row_offset=Tp, into=routed_p)

    pos, tok_s, units = _moe_routing(te, cnt_s[0].astype(I32))
    y_rows = _moe_gmm(h2[tok_s], units, exp_w_gu[l], exp_b_gu[l][:, None, :],
                      exp_w_dn[l], exp_b_dn[l][:, None, :])
    ys = [y_rows[pos[:, k]] for k in range(TOP_K)]
    y_prompt = _fin3(h2, tg, ys, vec(ln3_g), vec(ln3_b), 0, Tp).reshape(Bp, S, D)
    y_sample = _fin3(h2, tg, ys, vec(ln3_g), vec(ln3_b), Tp, Ts).reshape(Bd, Ls, D)

    kv6 = lambda a, b, n: a.reshape(1, b, n, 2, KVH, DH)
    row_major = lambda a: jnp.transpose(a, (0, 4, 1, 2, 3))
    win_p = kvw_p.reshape(Bp, S, KV_WIDTH)[:, S - min(WINDOW, S):]
    pool_state_p = up.reshape(Bp, S, POOL_WIDTH)[:, S - POOL_STATE:]
    pool_state_s = jnp.concatenate([state_pool[l], us.reshape(Bd, Ls, POOL_WIDTH)], axis=1)[:, -POOL_STATE:]
    return (y_prompt, y_sample,
            row_major(kvc_t)[None], row_major(kvs_t)[None], kv6(win_p, Bp, min(WINDOW, S)),
            pool_state_p[None], mem_kv_p.reshape(1, Bp, MEM_LEN, 2, MEM_HEADS, MEM_HEAD_DIM),
            kv6(kvc_s, Bd, Ls), kv6(kvs_s, Bd, Ls), row_major(win_next)[None], pool_state_s[None])
```

```python
import functools

import jax
import jax.numpy as jnp
from jax import lax
from jax.experimental import pallas as pl
from jax.experimental.pallas import tpu as pltpu

F32 = jnp.float32
BF16 = jnp.bfloat16
I32 = jnp.int32

D_MODEL = 1024
POOL_WIDTH = 512
POOL_WINDOWS = (2, 4, 8, 16)
POOL_GROUP = 128
POOL_STATE = 15
NSA_WIDTH = 512
DH = 64
NSA_HEADS = 8
KVH = 2
HPG = 4
CMP_LEN = 32
CMP_STRIDE = 16
CMP_HIDDEN = 256
SEL_LEN = 64
SEL_TOP = 16
WINDOW = 512
Q_BLOCK = 128
KV_WIDTH = 256
ATTN_SCALE = DH ** -0.5
FORCED_SCORE = 1e4
NEG = -1e30
MEM_LEN = 256
MEM_HEADS = 4
MEM_HEAD_DIM = 256
N_EXPERTS = 32
TOP_K = 4
D_FF = 1024
SWIGLU_LIMIT = 7.0
SWIGLU_ALPHA = 1.702
DN_ALPHA = 2.0 ** 0.25
LN_EPS = 1e-5
PAST_LEN = 2048
PAGE_SIZE = 128

LANES = 128
SEL_PAD = 128
KEY_TILE = 512
ROW_TILE = 512
MOE_ROWS = 512
MEM_PER_STEP = 8
BF16_ROWS = 16
VMEM_LIMIT = 56 * 1024 * 1024

HIGHEST = lax.Precision.HIGHEST


def _dot(a, b):
    return jnp.dot(a, b, preferred_element_type=F32)


def _dot_nt(a, b, precision=None):
    return lax.dot_general(a, b, (((1,), (1,)), ((), ())), preferred_element_type=F32,
                           precision=precision)


def _layer_norm(x, g, b):
    mu = jnp.mean(x, axis=-1, keepdims=True)
    xc = x - mu
    var = jnp.mean(xc * xc, axis=-1, keepdims=True)
    return xc * lax.rsqrt(var + LN_EPS) * g + b


def _params(sem, vmem=VMEM_LIMIT):
    return pltpu.CompilerParams(dimension_semantics=sem, vmem_limit_bytes=vmem)


def _split_store(u, up_ref, q_ref, kvc_ref, kvs_ref, kvw_ref, gate_ref):
    o1 = POOL_WIDTH
    o2 = o1 + NSA_WIDTH
    o3 = o2 + KV_WIDTH
    o4 = o3 + KV_WIDTH
    o5 = o4 + KV_WIDTH
    up_ref[...] = u[:, :o1]
    q_ref[...] = u[:, o1:o2]
    kvc_ref[...] = u[:, o2:o3]
    kvs_ref[...] = u[:, o3:o4]
    kvw_ref[...] = u[:, o4:o5]
    gate_ref[...] = 1.0 / (1.0 + jnp.exp(-u[:, o5:]))


def _inproj_prompt_body(x_ref, w_ref, pw_ref, ps_ref,
                        up_ref, q_ref, kvc_ref, kvs_ref, kvw_ref, gate_ref, pool_ref, kvct_ref, kvst_ref,
                        ext_ref, *, tm, tiles_per_seq):
    halo = POOL_STATE + 1
    t_in_seq = pl.program_id(0) % tiles_per_seq
    u = _dot(x_ref[...].astype(BF16), w_ref[...])
    _split_store(u, up_ref, q_ref, kvc_ref, kvs_ref, kvw_ref, gate_ref)
    o2 = POOL_WIDTH + NSA_WIDTH
    kvct_ref[0] = u[:, o2:o2 + KV_WIDTH].T.reshape(2, KVH, DH, tm)
    kvst_ref[0] = u[:, o2 + KV_WIDTH:o2 + 2 * KV_WIDTH].T.reshape(2, KVH, DH, tm)

    @pl.when(t_in_seq == 0)
    def _():
        ext_ref[0:halo, :] = jnp.zeros((halo, POOL_WIDTH), F32)

    ext_ref[halo:halo + tm, :] = u[:, :POOL_WIDTH]
    pos = t_in_seq * tm + lax.broadcasted_iota(I32, (tm, 1), 0)
    for gi, w in enumerate(POOL_WINDOWS):
        cols = slice(gi * POOL_GROUP, (gi + 1) * POOL_GROUP)
        acc = ext_ref[halo:halo + tm, cols]
        for k in range(1, w):
            acc = acc + ext_ref[halo - k:halo - k + tm, cols]
        cnt = jnp.minimum(pos + 1, w).astype(F32)
        d = acc / cnt - ext_ref[halo:halo + tm, cols]
        o = _dot(d.astype(BF16), pw_ref[gi])
        pool_ref[:, cols] = (o * ps_ref[:, cols]).astype(pool_ref.dtype)
    ext_ref[0:halo, :] = ext_ref[tm:tm + halo, :]


def _inproj_prompt(x2d, w_in_bf, pool_w_bf, pool_scale, seq_len):
    T = x2d.shape[0]
    tm = ROW_TILE
    outs = [POOL_WIDTH, NSA_WIDTH, KV_WIDTH, KV_WIDTH, KV_WIDTH, 3 * NSA_HEADS, POOL_WIDTH]
    row = lambda n: pl.BlockSpec((tm, n), lambda i: (i, 0))
    full = lambda a: pl.BlockSpec(a.shape, lambda i: (0,) * a.ndim)
    tps = seq_len // tm
    kvt_spec = pl.BlockSpec((1, 2, KVH, DH, tm), lambda i: (i // tps, 0, 0, 0, i % tps))
    kvt_shape = jax.ShapeDtypeStruct((T // seq_len, 2, KVH, DH, seq_len), F32)
    return pl.pallas_call(
        functools.partial(_inproj_prompt_body, tm=tm, tiles_per_seq=tps),
        grid=(T // tm,),
        in_specs=[row(D_MODEL), full(w_in_bf), full(pool_w_bf), full(pool_scale)],
        out_specs=[row(n) for n in outs] + [kvt_spec] * 2,
        out_shape=[jax.ShapeDtypeStruct((T, n), F32) for n in outs[:-1]]
                  + [jax.ShapeDtypeStruct((T, outs[-1]), BF16)] + [kvt_shape] * 2,
        scratch_shapes=[pltpu.VMEM((tm + POOL_STATE + 1, POOL_WIDTH), F32)],
        compiler_params=_params(("arbitrary",)),
        name="inproj_prompt",
    )(x2d, w_in_bf, pool_w_bf, pool_scale)


def _inproj_sample_body(x_ref, w_ref, pw_ref, ps_ref, st_ref,
                        up_ref, q_ref, kvc_ref, kvs_ref, kvw_ref, gate_ref, pool_ref,
                        ext_ref, *, nb, ls, pos0):
    halo = POOL_STATE + 1
    tm = nb * ls
    u = _dot(x_ref[...].astype(BF16), w_ref[...])
    _split_store(u, up_ref, q_ref, kvc_ref, kvs_ref, kvw_ref, gate_ref)
    ext_ref[:, 0:halo, :] = st_ref[...]
    ext_ref[:, halo:halo + ls, :] = u[:, :POOL_WIDTH].reshape(nb, ls, POOL_WIDTH)
    pos = pos0 + lax.broadcasted_iota(I32, (1, ls, 1), 1)
    for gi, w in enumerate(POOL_WINDOWS):
        cols = slice(gi * POOL_GROUP, (gi + 1) * POOL_GROUP)
        acc = ext_ref[:, halo:halo + ls, cols]
        for k in range(1, w):
            acc = acc + ext_ref[:, halo - k:halo - k + ls, cols]
        cnt = jnp.minimum(pos + 1, w).astype(F32)
        d = acc / cnt - ext_ref[:, halo:halo + ls, cols]
        o = _dot(d.reshape(tm, POOL_GROUP).astype(BF16), pw_ref[gi])
        pool_ref[:, cols] = (o * ps_ref[:, cols]).astype(pool_ref.dtype)


def _inproj_sample(x2d, w_in_bf, pool_w_bf, pool_scale, state_pad, ls, pos0):
    T = x2d.shape[0]
    nb = ROW_TILE // ls
    tm = nb * ls
    outs = [POOL_WIDTH, NSA_WIDTH, KV_WIDTH, KV_WIDTH, KV_WIDTH, 3 * NSA_HEADS, POOL_WIDTH]
    row = lambda n: pl.BlockSpec((tm, n), lambda i: (i, 0))
    full = lambda a: pl.BlockSpec(a.shape, lambda i: (0,) * a.ndim)
    return pl.pallas_call(
        functools.partial(_inproj_sample_body, nb=nb, ls=ls, pos0=pos0),
        grid=(T // tm,),
        in_specs=[row(D_MODEL), full(w_in_bf), full(pool_w_bf), full(pool_scale),
                  pl.BlockSpec((nb, POOL_STATE + 1, POOL_WIDTH), lambda i: (i, 0, 0))],
        out_specs=[row(n) for n in outs],
        out_shape=[jax.ShapeDtypeStruct((T, n), F32) for n in outs],
        scratch_shapes=[pltpu.VMEM((nb, POOL_STATE + 1 + ls, POOL_WIDTH), F32)],
        compiler_params=_params(("arbitrary",)),
        name="inproj_sample",
    )(x2d, w_in_bf, pool_w_bf, pool_scale, state_pad)


def _matmul_body(x_ref, w_ref, o_ref):
    o_ref[...] = _dot(x_ref[...].astype(BF16), w_ref[...].astype(BF16))


def _matmul(x, w, tn=512):
    M, K = x.shape
    N = w.shape[1]
    return pl.pallas_call(
        _matmul_body,
        grid=(N // tn,),
        in_specs=[pl.BlockSpec((M, K), lambda j: (0, 0)), pl.BlockSpec((K, tn), lambda j: (0, j))],
        out_specs=pl.BlockSpec((M, tn), lambda j: (0, j)),
        out_shape=jax.ShapeDtypeStruct((M, N), F32),
        compiler_params=_params(("arbitrary",)),
        name="mem_kv_proj",
    )(x, w)


def _gelu_tanh(x):
    c = 0.7978845608028654
    return 0.5 * x * (1.0 + jnp.tanh(c * (x + 0.044715 * (x * x * x))))


def _compress(kv_refs, n_chunks, w1_ref, pe_ref, b1_ref, w2_ref, b2_ref):
    lo = _lane_iota(n_chunks) < DH
    quads = CMP_STRIDE // 4
    outs = []
    for c in range(2):
        acc_a = jnp.zeros((2 * n_chunks, CMP_HIDDEN), F32)
        acc_b = jnp.zeros((2 * n_chunks, CMP_HIDDEN), F32)
        for i in range(quads):
            x = [kv_refs[c][pl.ds(4 * i + m, n_chunks, stride=CMP_STRIDE), :] for m in range(4)]
            r = [pltpu.roll(v, DH, 1) for v in x]
            x_g0 = jnp.concatenate([jnp.where(lo, x[0], r[1]), jnp.where(lo, x[2], r[3])], axis=1)
            x_g1 = jnp.concatenate([jnp.where(lo, r[0], x[1]), jnp.where(lo, r[2], x[3])], axis=1)
            xq = jnp.concatenate([x_g0, x_g1], axis=0)
            acc_a = acc_a + _dot((xq + pe_ref[c, i:i + 1, :]).astype(BF16), w1_ref[c, i])
            acc_b = acc_b + _dot((xq + pe_ref[c, quads + i:quads + i + 1, :]).astype(BF16), w1_ref[c, quads + i])
        hid = acc_a + pltpu.roll(acc_b, 2 * n_chunks - 1, 0) + b1_ref[c]
        outs.append(_dot(_gelu_tanh(hid).astype(BF16), w2_ref[c]) + b2_ref[c])
    return outs


def _compress_prompt_body(kv_ref, w1_ref, pe_ref, b1_ref, w2_ref, b2_ref, o_ref, k_ref, v_ref, *, n_chunks):
    k_ref[...] = kv_ref[0, :, :LANES]
    v_ref[...] = kv_ref[0, :, LANES:]
    keys, values = _compress((k_ref, v_ref), n_chunks, w1_ref, pe_ref, b1_ref, w2_ref, b2_ref)
    o_ref[0, 0] = keys
    o_ref[0, 1] = values


def _compress_prompt(kvc, cw):
    B, S, _ = kvc.shape
    n_chunks = S // CMP_STRIDE
    full = lambda a: pl.BlockSpec(a.shape, lambda b: (0,) * a.ndim)
    return pl.pallas_call(
        functools.partial(_compress_prompt_body, n_chunks=n_chunks),
        grid=(B,),
        in_specs=[pl.BlockSpec((1, S, KV_WIDTH), lambda b: (b, 0, 0))] + [full(a) for a in cw],
        out_specs=pl.BlockSpec((1, 2, KVH * n_chunks, LANES), lambda b: (b, 0, 0, 0)),
        out_shape=jax.ShapeDtypeStruct((B, 2, KVH * n_chunks, LANES), F32),
        scratch_shapes=[pltpu.VMEM((S, LANES), F32)] * 2,
        compiler_params=_params(("arbitrary",)),
        name="compress_prompt",
    )(kvc, *cw)


def _compress_weights(cmp_pe, cmp_w1, cmp_b1, cmp_w2, cmp_b2):
    nq = CMP_LEN // 4
    w1 = cmp_w1.reshape(2, nq, 4 * DH, CMP_HIDDEN).astype(BF16)
    pe = cmp_pe.reshape(2, nq, 4 * DH)
    b1 = cmp_b1[:, None, :]
    w2 = jnp.stack([jnp.concatenate([cmp_w2[0], jnp.zeros_like(cmp_w2[0])], axis=1),
                    jnp.concatenate([cmp_w2[1], cmp_w2[1]], axis=1)]).astype(BF16)
    b2 = jnp.stack([jnp.concatenate([cmp_b2[0], jnp.zeros_like(cmp_b2[0])]),
                    jnp.concatenate([cmp_b2[1], cmp_b2[1]])])[:, None, :]
    return w1, pe, b1, w2, b2


def _softmax_rows(s):
    e = jnp.exp(s - jnp.max(s, axis=-1, keepdims=True))
    return e * (1.0 / jnp.sum(e, axis=-1, keepdims=True))


def _lane_iota(n):
    return lax.broadcasted_iota(I32, (n, LANES), 1)


def _key_alibi_cols(pos, lane):
    hi = jnp.left_shift(jnp.right_shift(pos, 6), 6).astype(F32)
    lo = jnp.bitwise_and(pos, SEL_LEN - 1).astype(F32)
    return jnp.where(lane == DH, hi,
                     jnp.where(lane == DH + 1, lo,
                               jnp.where((lane == DH + 2) | (lane == DH + 3), 1.0, 0.0)))


def _key_alibi_rows(pos, sub):
    hi = jnp.left_shift(jnp.right_shift(pos, 6), 6).astype(F32)
    lo = jnp.bitwise_and(pos, SEL_LEN - 1).astype(F32)
    return jnp.where(sub == 0, hi, jnp.where(sub == 1, lo, jnp.where((sub == 2) | (sub == 3), 1.0, 0.0)))


def _halves(x, zero_hi=False):
    lo = _lane_iota(x.shape[0]) < DH
    r = pltpu.roll(x, DH, 1)
    if zero_hi:
        return jnp.where(lo, x, 0.0), jnp.where(lo, r, 0.0)
    return jnp.where(lo, x, r), jnp.where(lo, r, x)


def _fill_queries(qa_ref, qs, q_pos, lq):
    lane = _lane_iota(lq)
    lo_half = lane < DH
    q_hi = jnp.left_shift(jnp.right_shift(q_pos, 7), 7).astype(F32)
    q_lo = jnp.bitwise_and(q_pos, LANES - 1).astype(F32)
    for h in range(NSA_HEADS):
        g, hl = divmod(h, HPG)
        slope = 2.0 ** (-(h + 1))
        slab = qs[:, (h // 2) * LANES:(h // 2 + 1) * LANES]
        if h % 2:
            slab = pltpu.roll(slab, DH, 1)
        ex = jnp.where((lane == DH) | (lane == DH + 1), slope,
                       jnp.where(lane == DH + 2, -slope * q_hi,
                                 jnp.where(lane == DH + 3, -slope * q_lo,
                                           jnp.where(lane == DH + 4, NEG, 0.0))))
        qa_ref[g, hl * lq:(hl + 1) * lq, 0:LANES] = jnp.where(lo_half, slab, ex).astype(BF16)


def _stack4(x):
    return jnp.concatenate([x] * HPG, axis=0)


def _cmp_branch(qa_ref, kc_k, kc_v, q_pos4, lq, n_cmp):
    c_end = lax.broadcasted_iota(I32, (1, n_cmp), 1) * CMP_STRIDE + (CMP_LEN - 1)
    m_c = c_end <= q_pos4
    any_c = (q_pos4 >= CMP_LEN - 1).astype(F32)
    outs, psums = [], []
    for g in range(KVH):
        s = jnp.where(m_c, _dot_nt(qa_ref[g, :, 0:LANES], kc_k[g][...]), NEG)
        p = _softmax_rows(s) * any_c
        outs.append(_dot(p.astype(BF16), kc_v[g][...]))
        psums.append(p[0:lq] + p[lq:2 * lq] + p[2 * lq:3 * lq] + p[3 * lq:4 * lq])
    return outs, psums


def _split3(x):
    hi = x.astype(BF16)
    r1 = x - hi.astype(F32)
    mid = r1.astype(BF16)
    lo = (r1 - mid.astype(F32)).astype(BF16)
    return hi, mid, lo


def _top_blocks_t(imp_ts, pos0, n_blk=SEL_PAD):
    blk = lax.broadcasted_iota(I32, (n_blk, LANES), 0)
    qp_t = pos0 + lax.broadcasted_iota(I32, (n_blk, LANES), 1)
    cur = jnp.right_shift(qp_t, 6)
    forced = (blk == 0) | (blk == cur) | (blk == cur - 1)
    valid = jnp.left_shift(blk, 6) <= qp_t
    v = jnp.concatenate([jnp.where(valid, jnp.where(forced, FORCED_SCORE, t), -1.0) for t in imp_ts], axis=1)
    blk_f = lax.broadcasted_iota(I32, (n_blk, KVH * LANES), 0).astype(F32)
    sel = jnp.zeros((n_blk, KVH * LANES), F32)
    for _ in range(SEL_TOP):
        m = jnp.max(v, axis=0, keepdims=True)
        idx = jnp.min(jnp.where(v == m, blk_f, float(n_blk)), axis=0, keepdims=True)
        hit = blk_f == idx
        sel = jnp.where(hit, 1.0, sel)
        v = jnp.where(hit, -jnp.inf, v)
    out = [jnp.where((sel[:, g * LANES:(g + 1) * LANES] > 0.5) & valid, 0.0, NEG) for g in range(KVH)]
    if n_blk < SEL_PAD:
        out = [jnp.concatenate([b, jnp.full((SEL_PAD - n_blk, LANES), NEG, F32)], axis=0) for b in out]
    return out


def _select_blocks(psums, ovt_ref, pos0, lq, n_blk):
    imp_ts = []
    for g in range(KVH):
        ps = psums[g]
        if lq < LANES:
            ps = jnp.concatenate([ps, jnp.zeros((LANES - lq, ps.shape[1]), F32)], axis=0)
        imp_ts.append(_dot_nt(ovt_ref[0:n_blk, :], ps, precision=HIGHEST))
    return [b.T[:lq].astype(BF16) for b in _top_blocks_t(imp_ts, pos0, n_blk)]


def _store_selbias(qa_ref, selbias, lq):
    for g in range(KVH):
        for hl in range(HPG):
            qa_ref[g, hl * lq:(hl + 1) * lq, LANES:2 * LANES] = selbias[g]


def _combine(gates, eg_ref, o_c, o_s, o_w, lq):
    lo_half = _lane_iota(lq) < DH

    def assemble(per_group):
        slabs = []
        for k in range(NSA_HEADS // 2):
            g, hl = divmod(2 * k, HPG)
            a = per_group[g][hl * lq:(hl + 1) * lq]
            b = per_group[g][(hl + 1) * lq:(hl + 2) * lq]
            slabs.append(jnp.where(lo_half, a, b))
        return jnp.concatenate(slabs, axis=1)

    return (_dot(gates, eg_ref[0]) * assemble(o_c)
            + _dot(gates, eg_ref[1]) * assemble(o_s)
            + _dot(gates, eg_ref[2]) * assemble(o_w))


def _gate_expand():
    r = jnp.arange(3 * NSA_HEADS)
    c = jnp.arange(NSA_WIDTH)
    return jnp.stack([(r[:, None] == 3 * (c[None, :] // DH) + br).astype(F32) for br in range(3)])


def _overlap_t(n_cmp):
    n = jnp.arange(n_cmp)
    s = jnp.arange(SEL_PAD)
    c_first = n * CMP_STRIDE
    c_end = c_first + CMP_LEN - 1
    b_first = s * SEL_LEN
    return ((c_first[None, :] < b_first[:, None] + SEL_LEN) & (c_end[None, :] >= b_first[:, None])).astype(F32)


def _key_rows(kv_f32, pos, invalid=None):
    n = kv_f32.shape[0]
    lane = _lane_iota(n)
    ex = _key_alibi_cols(pos, lane)
    if invalid is not None:
        ex = jnp.where((lane == DH + 4) & invalid, 1.0, ex)
    lo = lane < DH
    return jnp.where(lo, kv_f32, ex), jnp.where(lo, pltpu.roll(kv_f32, DH, 1), ex)


def _block_onehot(pos, n):
    return (lax.broadcasted_iota(I32, (n, SEL_PAD), 1) == jnp.right_shift(pos, 6)).astype(BF16)


def _tile4(x):
    return jnp.concatenate([x] * HPG, axis=1)


def _nsa_prompt_body(q_ref, gate_ref, kc_ref, kvs_ref, w0_ref, w1_ref, w2_ref, w3_ref, w4_ref,
                     ovt_ref, egt_ref, wband_ref, o_ref,
                     kck0, kck1, kcv0, kcv1, ka0, ka1, vt0, vt1, qa0, qa1,
                     sa_ref, sb_ref, m_ref, l_ref, acc_ref, oc_ref, *, seq_len):
    j = pl.program_id(1)
    kc_k, kc_vt, kaug, v_t, qa_t = (kck0, kck1), (kcv0, kcv1), (ka0, ka1), (vt0, vt1), (qa0, qa1)
    n_cmp = kc_ref.shape[2] // KVH
    lq = Q_BLOCK
    cols = HPG * lq

    @pl.when(j == 0)
    def _():
        for g in range(KVH):
            kc_k[g][...] = kc_ref[0, 0, g * n_cmp:(g + 1) * n_cmp, :].astype(BF16)
            kc_vt[g][...] = kc_ref[0, 1, g * n_cmp:(g + 1) * n_cmp, :].T[0:DH].astype(BF16)

        def build(i, _):
            r0 = pl.multiple_of(i * KEY_TILE, KEY_TILE)
            pos = r0 + lax.broadcasted_iota(I32, (KEY_TILE, 1), 0)
            k0, k1 = _key_rows(kvs_ref[0, pl.ds(r0, KEY_TILE), :LANES], pos)
            onehot = _block_onehot(pos, KEY_TILE)
            ka0[pl.ds(r0, KEY_TILE), :] = jnp.concatenate([k0.astype(BF16), onehot], axis=1)
            ka1[pl.ds(r0, KEY_TILE), :] = jnp.concatenate([k1.astype(BF16), onehot], axis=1)
            vt = kvs_ref[0, pl.ds(r0, KEY_TILE), LANES:].T.astype(BF16)
            vt0[:, pl.ds(r0, KEY_TILE)] = vt[0:DH]
            vt1[:, pl.ds(r0, KEY_TILE)] = vt[DH:]
            return 0

        lax.fori_loop(0, seq_len // KEY_TILE, build, 0)

    st = j * Q_BLOCK
    q_pos = st + lax.broadcasted_iota(I32, (1, lq), 1)
    q_pos4 = _tile4(q_pos)

    q_t = (q_ref[...] * ATTN_SCALE).T
    sub = lax.broadcasted_iota(I32, (DH, lq), 0)
    q_hi = jnp.left_shift(jnp.right_shift(q_pos, 7), 7).astype(F32)
    q_lo = jnp.bitwise_and(q_pos, LANES - 1).astype(F32)
    for h in range(NSA_HEADS):
        g, hl = divmod(h, HPG)
        slope = 2.0 ** (-(h + 1))
        ex = jnp.where(sub <= 1, slope,
                       jnp.where(sub == 2, -slope * q_hi,
                                 jnp.where(sub == 3, -slope * q_lo, jnp.where(sub == 4, NEG, 0.0))))
        qa_t[g][0:DH, hl * lq:(hl + 1) * lq] = q_t[h * DH:(h + 1) * DH].astype(BF16)
        qa_t[g][DH:2 * DH, hl * lq:(hl + 1) * lq] = ex.astype(BF16)

    any_c = (q_pos4 >= CMP_LEN - 1).astype(F32)
    blocks_per_sel = SEL_LEN // CMP_STRIDE
    n_variants = -(-n_cmp // LANES)
    last_visible = (st + Q_BLOCK - CMP_LEN) // CMP_STRIDE

    def cmp_and_select(n_c):
        n_b = n_c // blocks_per_sel
        c_end = lax.broadcasted_iota(I32, (n_c, 1), 0) * CMP_STRIDE + (CMP_LEN - 1)
        m_c = c_end <= q_pos4
        ovt_bf = ovt_ref[0:n_b, 0:n_c].astype(BF16)
        imp_ts = []
        for g in range(KVH):
            s = jnp.where(m_c, _dot(kc_k[g][0:n_c, :], qa_t[g][0:LANES, :]), NEG)
            e = jnp.exp(s - jnp.max(s, axis=0, keepdims=True))
            p = e * (any_c / jnp.sum(e, axis=0, keepdims=True))
            oc_ref[g] = _dot(kc_vt[g][:, 0:n_c], p.astype(BF16))
            psum = p[:, 0:lq] + p[:, lq:2 * lq] + p[:, 2 * lq:3 * lq] + p[:, 3 * lq:4 * lq]
            imp_ts.append(sum(_dot(ovt_bf, t) for t in _split3(psum)))
        bias = _top_blocks_t(imp_ts, st, n_b)
        for g in range(KVH):
            for hl in range(HPG):
                qa_t[g][2 * DH:, hl * lq:(hl + 1) * lq] = bias[g].astype(BF16)

    variant = jnp.minimum(jnp.maximum(last_visible, 0) // LANES, n_variants - 1)
    for k in range(n_variants):
        pl.when(variant == k)(functools.partial(cmp_and_select, min(n_cmp, (k + 1) * LANES)))
    o_c = [oc_ref[g] for g in range(KVH)]

    n_tiles = (st + Q_BLOCK + KEY_TILE - 1) // KEY_TILE
    for g in range(KVH):
        m_ref[g] = jnp.full((1, cols), NEG, F32)
        l_ref[g] = jnp.zeros((1, cols), F32)
        acc_ref[g] = jnp.zeros((DH, cols), F32)

    def scores(t, s_ref):
        r0 = pl.multiple_of(t * KEY_TILE, KEY_TILE)
        for g in range(KVH):
            s_ref[g] = _dot(kaug[g][pl.ds(r0, KEY_TILE), :], qa_t[g][...])

    def consume(t, s_ref, masked):
        r0 = pl.multiple_of(t * KEY_TILE, KEY_TILE)
        for g in range(KVH):
            s = s_ref[g]
            if masked:
                k_pos = r0 + lax.broadcasted_iota(I32, (KEY_TILE, 1), 0)
                s = jnp.where(k_pos <= q_pos4, s, NEG)
            m = m_ref[g]
            m_new = jnp.maximum(m, jnp.max(s, axis=0, keepdims=True))
            a = jnp.exp(m - m_new)
            e = jnp.exp(s - m_new)
            m_ref[g] = m_new
            l_ref[g] = a * l_ref[g] + jnp.sum(e, axis=0, keepdims=True)
            acc_ref[g] = a * acc_ref[g] + _dot(v_t[g][:, pl.ds(r0, KEY_TILE)], e.astype(BF16))

    scores(0, sa_ref)
    n_pairs = (n_tiles - 1) // 2

    def pair(u, _):
        scores(2 * u + 1, sb_ref)
        consume(2 * u, sa_ref, False)
        scores(2 * u + 2, sa_ref)
        consume(2 * u + 1, sb_ref, False)
        return 0

    lax.fori_loop(0, n_pairs, pair, 0)
    odd_tail = (n_tiles - 1) - 2 * n_pairs == 1

    @pl.when(odd_tail)
    def _():
        scores(n_tiles - 1, sb_ref)
        consume(n_tiles - 2, sa_ref, False)
        consume(n_tiles - 1, sb_ref, True)

    @pl.when(jnp.logical_not(odd_tail))
    def _():
        consume(n_tiles - 1, sa_ref, True)

    o_s = [acc_ref[g] * (1.0 / l_ref[g]) for g in range(KVH)]

    band = jnp.concatenate([w0_ref[0], w1_ref[0], w2_ref[0], w3_ref[0], w4_ref[0]], axis=0)
    n_win = band.shape[0]
    w_pos_col = st - WINDOW + lax.broadcasted_iota(I32, (n_win, 1), 0)
    kw_k = _key_rows(band[:, :LANES], jnp.maximum(w_pos_col, 0), invalid=w_pos_col < 0)
    vw_t = band[:, LANES:].T.astype(BF16)
    band_bias = _tile4(wband_ref[...])
    o_w = []
    for g in range(KVH):
        s = _dot(kw_k[g].astype(BF16), qa_t[g][0:LANES, :]) + band_bias
        e = jnp.exp(s - jnp.max(s, axis=0, keepdims=True))
        o_w.append(_dot(vw_t[g * DH:(g + 1) * DH], e.astype(BF16)) * (1.0 / jnp.sum(e, axis=0, keepdims=True)))

    def heads(per_group):
        return jnp.concatenate([per_group[h // HPG][:, (h % HPG) * lq:(h % HPG + 1) * lq]
                                for h in range(NSA_HEADS)], axis=0)

    gates = gate_ref[...]
    out_t = (_dot_nt(egt_ref[0], gates) * heads(o_c)
             + _dot_nt(egt_ref[1], gates) * heads(o_s)
             + _dot_nt(egt_ref[2], gates) * heads(o_w))
    o_ref[...] = out_t.T.astype(o_ref.dtype)


def _gate_expand_t():
    r = jnp.arange(3 * NSA_HEADS)
    c = jnp.arange(NSA_WIDTH)
    return jnp.stack([(3 * (c[:, None] // DH) + br == r[None, :]).astype(F32) for br in range(3)])


def _nsa_prompt(q, gates, kc, kvs, kvw):
    B, S, _ = kvs.shape
    nqb = S // Q_BLOCK
    n_cmp = kc.shape[2] // KVH
    ovt = _overlap_t(n_cmp)
    egt = _gate_expand_t()
    n_band = WINDOW // Q_BLOCK + 1
    d_band = jnp.arange(Q_BLOCK)[None, :] + WINDOW - jnp.arange(n_band * Q_BLOCK)[:, None]
    wband = jnp.where((d_band >= 0) & (d_band < WINDOW), 0.0, NEG).astype(F32)

    def band_spec(i):
        return pl.BlockSpec((1, Q_BLOCK, KV_WIDTH),
                            lambda b, j, i=i: (b, jnp.maximum(j - (n_band - 1) + i, 0), 0))

    full = lambda a: pl.BlockSpec(a.shape, lambda b, j: (0,) * a.ndim)
    return pl.pallas_call(
        functools.partial(_nsa_prompt_body, seq_len=S),
        grid=(B, nqb),
        in_specs=[pl.BlockSpec((Q_BLOCK, NSA_WIDTH), lambda b, j: (b * nqb + j, 0)),
                  pl.BlockSpec((Q_BLOCK, 3 * NSA_HEADS), lambda b, j: (b * nqb + j, 0)),
                  pl.BlockSpec((1,) + kc.shape[1:], lambda b, j: (b, 0, 0, 0)),
                  pl.BlockSpec((1, S, KV_WIDTH), lambda b, j: (b, 0, 0))]
                 + [band_spec(i) for i in range(n_band)] + [full(ovt), full(egt), full(wband)],
        out_specs=pl.BlockSpec((Q_BLOCK, NSA_WIDTH), lambda b, j: (b * nqb + j, 0)),
        out_shape=jax.ShapeDtypeStruct((B * S, NSA_WIDTH), BF16),
        scratch_shapes=[pltpu.VMEM((n_cmp, LANES), BF16)] * 2
                       + [pltpu.VMEM((DH, n_cmp), BF16)] * 2
                       + [pltpu.VMEM((S, 2 * LANES), BF16)] * 2
                       + [pltpu.VMEM((DH, S), BF16)] * 2
                       + [pltpu.VMEM((2 * LANES, HPG * Q_BLOCK), BF16)] * 2
                       + [pltpu.VMEM((KVH, KEY_TILE, HPG * Q_BLOCK), F32)] * 2
                       + [pltpu.VMEM((KVH, 1, HPG * Q_BLOCK), F32)] * 2
                       + [pltpu.VMEM((KVH, DH, HPG * Q_BLOCK), F32)] * 2,
        compiler_params=_params(("arbitrary", "arbitrary")),
        name="nsa_prompt",
    )(q, gates, kc, kvs, *([kvw] * n_band), ovt, egt, wband)


def _nsa_sample_body(pt_ref, q_ref, gate_ref, kvs_new_ref, win_ref, kvw_new_ref, *rest,
                     n_pages, ls, past_len):
    cmp_pages = rest[:n_pages]
    slc_pages = rest[n_pages:2 * n_pages]
    (w1_ref, pe_ref, b1_ref, w2_ref, b2_ref, ovt_ref, eg_ref, o_ref, wout_ref,
     full_k, full_v, kck0, kck1, kcv0, kcv1, kt0, kt1, vt0, vt1, wkt0, wkt1, qa_ref) = rest[2 * n_pages:]
    del pt_ref
    kc_k, kc_v, kaug_t, v_t, wk_t = (kck0, kck1), (kcv0, kcv1), (kt0, kt1), (vt0, vt1), (wkt0, wkt1)
    n_cmp = past_len // CMP_STRIDE
    w_rows = win_ref.shape[4]
    lq = BF16_ROWS
    rows = HPG * lq
    w_start = past_len - w_rows

    @pl.when(pl.program_id(0) == 0)
    def _():
        sub = lax.broadcasted_iota(I32, (DH, past_len), 0)
        pos = lax.broadcasted_iota(I32, (1, past_len), 1)
        ex = _key_alibi_rows(pos, sub).astype(BF16)
        onehot = (lax.broadcasted_iota(I32, (SEL_PAD, past_len), 0) == jnp.right_shift(pos, 6)).astype(BF16)
        subw = lax.broadcasted_iota(I32, (DH, w_rows), 0)
        exw = _key_alibi_rows(w_start + lax.broadcasted_iota(I32, (1, w_rows), 1), subw).astype(BF16)
        for g in range(KVH):
            kaug_t[g][DH:2 * DH, :] = ex
            kaug_t[g][2 * DH:, :] = onehot
            wk_t[g][DH:, :] = exw

    for p in range(n_pages):
        cols = slice(p * PAGE_SIZE, (p + 1) * PAGE_SIZE)
        full_k[cols, :] = cmp_pages[p][0, 0].reshape(2 * DH, PAGE_SIZE).T
        full_v[cols, :] = cmp_pages[p][0, 1].reshape(2 * DH, PAGE_SIZE).T
        for g in range(KVH):
            kaug_t[g][0:DH, cols] = slc_pages[p][0, 0, g].astype(BF16)
            vt = slc_pages[p][0, 1, g].astype(BF16)
            v_t[g][0:DH, cols] = vt
            v_t[g][DH:, cols] = vt
    for g in range(KVH):
        wk_t[g][0:DH, :] = win_ref[0, 0, g].astype(BF16)

    keys_c, values_c = _compress((full_k, full_v), n_cmp, w1_ref, pe_ref, b1_ref, w2_ref, b2_ref)
    for g in range(KVH):
        kc_k[g][...] = keys_c[g * n_cmp:(g + 1) * n_cmp].astype(BF16)
        kc_v[g][...] = values_c[g * n_cmp:(g + 1) * n_cmp].astype(BF16)

    pad_q = jnp.zeros((lq - ls, NSA_WIDTH), F32)
    q_pos = past_len + lax.broadcasted_iota(I32, (lq, 1), 0)
    q_pos4 = _stack4(q_pos)
    _fill_queries(qa_ref, jnp.concatenate([q_ref[0] * ATTN_SCALE, pad_q], axis=0), q_pos, lq)
    gates = jnp.concatenate([gate_ref[0], jnp.zeros((lq - ls, 3 * NSA_HEADS), F32)], axis=0)

    o_c, psums = _cmp_branch(qa_ref, kc_k, kc_v, q_pos4, lq, n_cmp)
    n_sel = -(-(past_len + lq) // SEL_LEN)
    n_blk = -(-n_sel // 8) * 8
    _store_selbias(qa_ref, _select_blocks(psums, ovt_ref, past_len, lq, n_blk), lq)

    pad_k = jnp.zeros((LANES - ls, KV_WIDTH), F32)
    new_pos_col = past_len + lax.broadcasted_iota(I32, (LANES, 1), 0)
    new_pos = past_len + lax.broadcasted_iota(I32, (1, LANES), 1)
    new_s = jnp.concatenate([kvs_new_ref[0], pad_k], axis=0)
    new_w = jnp.concatenate([kvw_new_ref[0], pad_k], axis=0)
    ks_new = _key_rows(new_s[:, :LANES], new_pos_col)
    vs_new = _halves(new_s[:, LANES:])
    kw_new = _key_rows(new_w[:, :LANES], new_pos_col)
    vw_new = _halves(new_w[:, LANES:])
    onehot_new = _block_onehot(new_pos_col, LANES)
    causal_new = q_pos4 >= new_pos

    o_s, o_w = [], []
    d_past = q_pos4 - (w_start + lax.broadcasted_iota(I32, (1, w_rows), 1))
    m_past = (d_past >= 0) & (d_past < WINDOW)
    d_new = q_pos4 - new_pos
    m_new = (d_new >= 0) & (d_new < WINDOW)
    for g in range(KVH):
        s_past = _dot(qa_ref[g], kaug_t[g][...])
        k_new = jnp.concatenate([ks_new[g].astype(BF16), onehot_new], axis=1)
        s_new = jnp.where(causal_new, _dot_nt(qa_ref[g], k_new), NEG)
        m = jnp.maximum(jnp.max(s_past, axis=-1, keepdims=True), jnp.max(s_new, axis=-1, keepdims=True))
        e_past = jnp.exp(s_past - m)
        e_new = jnp.exp(s_new - m)
        den = jnp.sum(e_past, axis=-1, keepdims=True) + jnp.sum(e_new, axis=-1, keepdims=True)
        acc = _dot_nt(e_past.astype(BF16), v_t[g][...]) + _dot(e_new.astype(BF16), vs_new[g].astype(BF16))
        o_s.append(acc / den)

        sw_past = jnp.where(m_past, _dot(qa_ref[g, :, 0:LANES], wk_t[g][...]), NEG)
        sw_new = jnp.where(m_new, _dot_nt(qa_ref[g, :, 0:LANES], kw_new[g].astype(BF16)), NEG)
        m = jnp.maximum(jnp.max(sw_past, axis=-1, keepdims=True), jnp.max(sw_new, axis=-1, keepdims=True))
        e_past = jnp.exp(sw_past - m)
        e_new = jnp.exp(sw_new - m)
        den = jnp.sum(e_past, axis=-1, keepdims=True) + jnp.sum(e_new, axis=-1, keepdims=True)
        vw = win_ref[0, 1, g].astype(BF16)
        vw2 = jnp.concatenate([vw, vw], axis=0)
        acc = _dot_nt(e_past.astype(BF16), vw2) + _dot(e_new.astype(BF16), vw_new[g].astype(BF16))
        o_w.append(acc / den)

    o_ref[0] = _combine(gates, eg_ref, o_c, o_s, o_w, lq)[:ls]

    keep = w_rows - ls
    tail_lane = lax.broadcasted_iota(I32, (DH, LANES), 1) >= keep % LANES
    for c in range(2):
        new_t = pltpu.roll(new_w[:, c * LANES:(c + 1) * LANES].T, keep % LANES, 1)
        for g in range(KVH):
            shifted = pltpu.roll(win_ref[0, c, g], keep, 1)
            last = jnp.where(tail_lane, new_t[g * DH:(g + 1) * DH], shifted[:, w_rows - LANES:])
            wout_ref[0, c, g] = jnp.concatenate([shifted[:, :w_rows - LANES], last], axis=1)


def _nsa_sample(q, gates, kvs_new, kvw_new, win_t, cmp_t, slc_t, page_table, cw, past_len):
    Bd, ls, _ = q.shape
    n_pages = page_table.shape[1]
    n_cmp = past_len // CMP_STRIDE
    ovt = _overlap_t(n_cmp)
    eg = _gate_expand()
    w_rows = win_t.shape[4]

    per_b = lambda a: pl.BlockSpec((1,) + a.shape[1:], lambda b, pt: (b,) + (0,) * (a.ndim - 1))
    full = lambda a: pl.BlockSpec(a.shape, lambda b, pt: (0,) * a.ndim)
    page = lambda p: pl.BlockSpec((1, 2, KVH, DH, PAGE_SIZE), lambda b, pt, p=p: (pt[b, p], 0, 0, 0, 0))
    grid_spec = pltpu.PrefetchScalarGridSpec(
        num_scalar_prefetch=1,
        grid=(Bd,),
        in_specs=[per_b(q), per_b(gates), per_b(kvs_new), per_b(win_t), per_b(kvw_new)]
                 + [page(p) for p in range(n_pages)] * 2
                 + [full(a) for a in cw] + [full(ovt), full(eg)],
        out_specs=[pl.BlockSpec((1, ls, NSA_WIDTH), lambda b, pt: (b, 0, 0)), per_b(win_t)],
        scratch_shapes=[pltpu.VMEM((past_len, LANES), F32)] * 2
                       + [pltpu.VMEM((n_cmp, LANES), BF16)] * 4
                       + [pltpu.VMEM((2 * LANES, past_len), BF16)] * 2
                       + [pltpu.VMEM((LANES, past_len), BF16)] * 2
                       + [pltpu.VMEM((LANES, w_rows), BF16)] * 2
                       + [pltpu.VMEM((KVH, HPG * BF16_ROWS, 2 * LANES), BF16)],
    )
    return pl.pallas_call(
        functools.partial(_nsa_sample_body, n_pages=n_pages, ls=ls, past_len=past_len),
        grid_spec=grid_spec,
        out_shape=[jax.ShapeDtypeStruct((Bd, ls, NSA_WIDTH), F32), jax.ShapeDtypeStruct(win_t.shape, F32)],
        compiler_params=_params(("arbitrary",)),
        name="nsa_sample",
    )(page_table, q, gates, kvs_new, win_t, kvw_new,
      *([cmp_t] * n_pages), *([slc_t] * n_pages), *cw, ovt, eg)


def _fin1_body(h_ref, pool_ref, nsa_ref, wo_ref, g_ref, b_ref, wq_ref, h1_ref, qm_ref):
    mix = (_dot(pool_ref[...].astype(BF16), wo_ref[0:POOL_WIDTH, :])
           + _dot(nsa_ref[...].astype(BF16), wo_ref[POOL_WIDTH:, :]))
    h1 = _layer_norm(DN_ALPHA * h_ref[...] + mix, g_ref[...], b_ref[...])
    h1_ref[...] = h1
    qm_ref[...] = (_dot(h1.astype(BF16), wq_ref[...]) * (MEM_HEAD_DIM ** -0.5)).astype(qm_ref.dtype)


def _fin1(h, pool_o, nsa_o, w_out_bf, g, b, wq_bf, q_dtype):
    T = h.shape[0]
    tm = ROW_TILE
    row = lambda n: pl.BlockSpec((tm, n), lambda i: (i, 0))
    full = lambda a: pl.BlockSpec(a.shape, lambda i: (0,) * a.ndim)
    return pl.pallas_call(
        _fin1_body,
        grid=(T // tm,),
        in_specs=[row(D_MODEL), row(POOL_WIDTH), row(NSA_WIDTH), full(w_out_bf), full(g), full(b), full(wq_bf)],
        out_specs=[row(D_MODEL), row(D_MODEL)],
        out_shape=[jax.ShapeDtypeStruct((T, D_MODEL), F32), jax.ShapeDtypeStruct((T, D_MODEL), q_dtype)],
        compiler_params=_params(("arbitrary",)),
        name="out_proj_ln1",
    )(h, pool_o, nsa_o, w_out_bf, g, b, wq_bf)


def _memattn_body(q_ref, kv_ref, o_ref):
    width = MEM_HEADS * MEM_HEAD_DIM
    for h in range(MEM_HEADS):
        cols = slice(h * MEM_HEAD_DIM, (h + 1) * MEM_HEAD_DIM)
        qh = q_ref[0, :, cols].astype(BF16)
        kh = kv_ref[0, :, cols].astype(BF16)
        vh = kv_ref[0, :, width + h * MEM_HEAD_DIM:width + (h + 1) * MEM_HEAD_DIM].astype(BF16)
        s = _dot_nt(qh, kh)
        e = jnp.exp(s - jnp.max(s, axis=-1, keepdims=True))
        o = _dot(e.astype(BF16), vh) * (1.0 / jnp.sum(e, axis=-1, keepdims=True))
        o_ref[0, :, cols] = o.astype(o_ref.dtype)


def _memattn(qm, mem_kv, tq):
    nb, L, W = qm.shape
    return pl.pallas_call(
        _memattn_body,
        grid=(nb, L // tq),
        in_specs=[pl.BlockSpec((1, tq, W), lambda b, t: (b, t, 0)),
                  pl.BlockSpec((1, MEM_LEN, 2 * W), lambda b, t: (b, 0, 0))],
        out_specs=pl.BlockSpec((1, tq, W), lambda b, t: (b, t, 0)),
        out_shape=jax.ShapeDtypeStruct((nb, L, W), BF16),
        compiler_params=_params(("arbitrary", "arbitrary")),
        name="mem_attn",
    )(qm, mem_kv)


def _memattn_few_body(q_ref, kv_ref, o_ref):
    lq = q_ref.shape[1]
    n_keys = MEM_LEN * MEM_HEADS
    assert lq & (lq - 1) == 0 and MEM_HEADS & (MEM_HEADS - 1) == 0
    shape = (MEM_HEADS * lq, n_keys)
    own_head = (jnp.bitwise_and(lax.broadcasted_iota(I32, shape, 1), MEM_HEADS - 1)
                == jnp.right_shift(lax.broadcasted_iota(I32, shape, 0), lq.bit_length() - 1))
    for i in range(q_ref.shape[0]):
        q = q_ref[i]
        qs = jnp.concatenate([q[:, h * MEM_HEAD_DIM:(h + 1) * MEM_HEAD_DIM] for h in range(MEM_HEADS)], axis=0)
        k = kv_ref[i, :, 0, :, :].reshape(n_keys, MEM_HEAD_DIM).astype(BF16)
        v = kv_ref[i, :, 1, :, :].reshape(n_keys, MEM_HEAD_DIM).astype(BF16)
        s = jnp.where(own_head, _dot_nt(qs.astype(BF16), k), NEG)
        e = jnp.exp(s - jnp.max(s, axis=-1, keepdims=True))
        o = _dot(e.astype(BF16), v) * (1.0 / jnp.sum(e, axis=-1, keepdims=True))
        for h in range(MEM_HEADS):
            o_ref[i, :, h * MEM_HEAD_DIM:(h + 1) * MEM_HEAD_DIM] = o[h * lq:(h + 1) * lq]


def _memattn_few(qm, mem_kv):
    nb, lq, W = qm.shape
    per_step = MEM_PER_STEP if nb % MEM_PER_STEP == 0 else 1
    return pl.pallas_call(
        _memattn_few_body,
        grid=(nb // per_step,),
        in_specs=[pl.BlockSpec((per_step, lq, W), lambda b: (b, 0, 0)),
                  pl.BlockSpec((per_step, MEM_LEN, 2, MEM_HEADS, MEM_HEAD_DIM), lambda b: (b, 0, 0, 0, 0))],
        out_specs=pl.BlockSpec((per_step, lq, W), lambda b: (b, 0, 0)),
        out_shape=jax.ShapeDtypeStruct((nb, lq, W), F32),
        compiler_params=_params(("arbitrary",)),
        name="mem_attn_few",
    )(qm, mem_kv)


def _fin2_body(cnt0_ref, h1_ref, o_ref, wo_ref, g_ref, b_ref, rw_ref, rb_ref, *rest, n_own):
    h2_ref, te_ref, tg_ref, cnt_ref, run_ref = rest[-5:]
    step = pl.program_id(0)

    @pl.when(step >= n_own)
    def _():
        h2_ref[...] = jnp.zeros(h2_ref.shape, F32)
        te_ref[...] = jnp.zeros(te_ref.shape, I32)
        tg_ref[...] = jnp.zeros(tg_ref.shape, F32)

    @pl.when(step < n_own)
    def _():
        _fin2_rows(cnt0_ref, h1_ref, o_ref, wo_ref, g_ref, b_ref, rw_ref, rb_ref,
                   h2_ref, te_ref, tg_ref, cnt_ref, run_ref)


def _fin2_rows(cnt0_ref, h1_ref, o_ref, wo_ref, g_ref, b_ref, rw_ref, rb_ref,
               h2_ref, te_ref, tg_ref, cnt_ref, run_ref):
    tm = h1_ref.shape[0]

    @pl.when(pl.program_id(0) == 0)
    def _():
        run_ref[...] = cnt0_ref[...]

    a = _dot(o_ref[...].astype(BF16), wo_ref[...])
    h2 = _layer_norm(DN_ALPHA * h1_ref[...] + a, g_ref[...], b_ref[...])
    h2_ref[...] = h2
    a_hi, a_lo, _ = _split3(h2)
    w_hi, w_lo, _ = _split3(rw_ref[...])
    logits = _dot(a_hi, w_hi) + (_dot(a_hi, w_lo) + _dot(a_lo, w_hi)) + rb_ref[...]
    e_iota = lax.broadcasted_iota(I32, (tm, N_EXPERTS), 1).astype(F32)
    lane = lax.broadcasted_iota(I32, (tm, LANES), 1)
    te = jnp.zeros((tm, LANES), F32)
    tv = jnp.full((tm, LANES), NEG, F32)
    work = logits
    chosen = []
    for k in range(TOP_K):
        m = jnp.max(work, axis=-1, keepdims=True)
        idx = jnp.min(jnp.where(work == m, e_iota, float(N_EXPERTS)), axis=-1, keepdims=True)
        hit = e_iota == idx
        chosen.append(hit)
        te = jnp.where(lane == k, idx, te)
        tv = jnp.where(lane == k, m, tv)
        work = jnp.where(hit, -jnp.inf, work)
    member = sum(c.astype(F32) for c in chosen)
    earlier = (lax.broadcasted_iota(I32, (tm, tm), 0) > lax.broadcasted_iota(I32, (tm, tm), 1)).astype(BF16)
    before = _dot(earlier, member.astype(BF16)) + run_ref[...]
    for k in range(TOP_K):
        rank = jnp.sum(jnp.where(chosen[k], before, 0.0), axis=-1, keepdims=True)
        te = jnp.where(lane == TOP_K + k, rank, te)
    run_ref[...] = run_ref[...] + jnp.sum(member, axis=0, keepdims=True)
    cnt_ref[...] = run_ref[...]
    ex = jnp.exp(tv - jnp.max(tv, axis=-1, keepdims=True))
    te_ref[...] = te.astype(I32)
    tg_ref[...] = ex / jnp.sum(ex, axis=-1, keepdims=True)


def _fin2(cnt0, h1, o, wo_bf, g, b, rw, rb, total_rows, row_offset=0, into=None):
    T = h1.shape[0]
    tm = ROW_TILE
    blk0 = row_offset // tm
    n_own = T // tm
    n_steps = n_own if into is not None else (total_rows - row_offset) // tm
    row = lambda n: pl.BlockSpec((tm, n), lambda i: (jnp.minimum(i, n_own - 1), 0))
    out_row = lambda n: pl.BlockSpec((tm, n), lambda i: (i + blk0, 0))
    full = lambda a: pl.BlockSpec(a.shape, lambda i: (0,) * a.ndim)
    ins = [cnt0, h1, o, wo_bf, g, b, rw, rb]
    in_specs = [full(cnt0), row(D_MODEL), row(D_MODEL), full(wo_bf), full(g), full(b), full(rw), full(rb)]
    aliases = {}
    if into is not None:
        aliases = {len(ins) + k: k for k in range(len(into))}
        in_specs = in_specs + [pl.BlockSpec(memory_space=pl.ANY)] * len(into)
        ins = ins + list(into)
    return pl.pallas_call(
        functools.partial(_fin2_body, n_own=n_own),
        grid=(n_steps,),
        in_specs=in_specs,
        out_specs=[out_row(D_MODEL), out_row(LANES), out_row(LANES), full(cnt0)],
        out_shape=[jax.ShapeDtypeStruct((total_rows, D_MODEL), F32), jax.ShapeDtypeStruct((total_rows, LANES), I32),
                   jax.ShapeDtypeStruct((total_rows, LANES), F32), jax.ShapeDtypeStruct(cnt0.shape, F32)],
        scratch_shapes=[pltpu.VMEM(cnt0.shape, F32)],
        input_output_aliases=aliases,
        compiler_params=_params(("arbitrary",)),
        name="mem_out_ln2_router",
    )(*ins)


def _row_chain_body(cnt0_ref, h_ref, pool_ref, nsa_ref, wout_ref, g1_ref, b1_ref, wq_ref, kv_ref,
                    wo_ref, g2_ref, b2_ref, rw_ref, rb_ref,
                    h2_ref, te_ref, tg_ref, cnt_ref, h1_s, q_s, o_s, run_ref, *, n_own):
    step = pl.program_id(0)

    @pl.when(step >= n_own)
    def _():
        h2_ref[...] = jnp.zeros(h2_ref.shape, F32)
        te_ref[...] = jnp.zeros(te_ref.shape, I32)
        tg_ref[...] = jnp.zeros(tg_ref.shape, F32)

    @pl.when(step < n_own)
    def _():
        _fin1_body(h_ref, pool_ref, nsa_ref, wout_ref, g1_ref, b1_ref, wq_ref, h1_s, q_s.at[0])
        _memattn_body(q_s, kv_ref, o_s)
        _fin2_rows(cnt0_ref, h1_s, o_s.at[0], wo_ref, g2_ref, b2_ref, rw_ref, rb_ref,
                   h2_ref, te_ref, tg_ref, cnt_ref, run_ref)


def _row_chain(cnt0, h, pool_o, nsa_o, w_out_bf, g1, b1, wq_bf, mem_kv, wo_bf, g2, b2, rw, rb, total_rows):
    T = h.shape[0]
    tm = ROW_TILE
    n_own = T // tm
    tiles_per_seq = n_own // mem_kv.shape[0]
    own = lambda i: jnp.minimum(i, n_own - 1)
    row = lambda n: pl.BlockSpec((tm, n), lambda i: (own(i), 0))
    out_row = lambda n: pl.BlockSpec((tm, n), lambda i: (i, 0))
    full = lambda a: pl.BlockSpec(a.shape, lambda i: (0,) * a.ndim)
    return pl.pallas_call(
        functools.partial(_row_chain_body, n_own=n_own),
        grid=(total_rows // tm,),
        in_specs=[full(cnt0), row(D_MODEL), row(POOL_WIDTH), row(NSA_WIDTH), full(w_out_bf), full(g1), full(b1),
                  full(wq_bf), pl.BlockSpec((1,) + mem_kv.shape[1:], lambda i: (own(i) // tiles_per_seq, 0, 0)),
                  full(wo_bf), full(g2), full(b2), full(rw), full(rb)],
        out_specs=[out_row(D_MODEL), out_row(LANES), out_row(LANES), full(cnt0)],
        out_shape=[jax.ShapeDtypeStruct((total_rows, D_MODEL), F32), jax.ShapeDtypeStruct((total_rows, LANES), I32),
                   jax.ShapeDtypeStruct((total_rows, LANES), F32), jax.ShapeDtypeStruct(cnt0.shape, F32)],
        scratch_shapes=[pltpu.VMEM((tm, D_MODEL), F32), pltpu.VMEM((1, tm, D_MODEL), BF16),
                        pltpu.VMEM((1, tm, D_MODEL), BF16), pltpu.VMEM(cnt0.shape, F32)],
        compiler_params=_params(("arbitrary",)),
        name="out_proj_mem_attn_router",
    )(cnt0, h, pool_o, nsa_o, w_out_bf, g1, b1, wq_bf, mem_kv, wo_bf, g2, b2, rw, rb)


def _moe_body(ut_ref, ue_ref, nu_ref, rs_ref, re_ref, x_ref, wgu_ref, bgu_ref, wdn_ref, bdn_ref,
              y_ref, wgu_bf, wdn_bf):
    u = pl.program_id(0)
    bk = x_ref.shape[0]
    e = ue_ref[u]
    tile = ut_ref[u]
    prev = jnp.maximum(u - 1, 0)

    @pl.when((u == 0) | (e != ue_ref[prev]))
    def _():
        wgu_bf[...] = wgu_ref[0].astype(BF16)
        wdn_bf[...] = wdn_ref[0].astype(BF16)

    @pl.when(u < nu_ref[0])
    def _():
        x = x_ref[...].astype(BF16)
        g = _dot(x, wgu_bf[:, :D_FF]) + bgu_ref[0, :, :D_FF]
        v = _dot(x, wgu_bf[:, D_FF:]) + bgu_ref[0, :, D_FF:]
        g = jnp.minimum(g, SWIGLU_LIMIT)
        v = jnp.clip(v, -SWIGLU_LIMIT, SWIGLU_LIMIT)
        a = g * (1.0 / (1.0 + jnp.exp(-SWIGLU_ALPHA * g))) * (v + 1.0)
        y = _dot(a.astype(BF16), wdn_bf[...]) + bdn_ref[0]
        row = tile * bk + lax.broadcasted_iota(I32, (bk, 1), 0)
        mine = (row >= rs_ref[e]) & (row < re_ref[e])
        y = jnp.where(mine, y, 0.0)

        @pl.when((u == 0) | (tile != ut_ref[prev]))
        def _():
            y_ref[...] = y

        @pl.when((u > 0) & (tile == ut_ref[prev]))
        def _():
            y_ref[...] = y_ref[...] + y


def _moe_gmm(x_rows, units, w_gu, b_gu, w_dn, b_dn):
    N = x_rows.shape[0]
    bk = MOE_ROWS
    unit_tile, unit_e, n_units, r_start, r_end = units
    grid_spec = pltpu.PrefetchScalarGridSpec(
        num_scalar_prefetch=5,
        grid=(unit_tile.shape[0],),
        in_specs=[pl.BlockSpec((bk, D_MODEL), lambda u, ut, ue, *_: (ut[u], 0)),
                  pl.BlockSpec((1, D_MODEL, 2 * D_FF), lambda u, ut, ue, *_: (ue[u], 0, 0)),
                  pl.BlockSpec((1, 1, 2 * D_FF), lambda u, ut, ue, *_: (ue[u], 0, 0)),
                  pl.BlockSpec((1, D_FF, D_MODEL), lambda u, ut, ue, *_: (ue[u], 0, 0)),
                  pl.BlockSpec((1, 1, D_MODEL), lambda u, ut, ue, *_: (ue[u], 0, 0))],
        out_specs=pl.BlockSpec((bk, D_MODEL), lambda u, ut, ue, *_: (ut[u], 0)),
        scratch_shapes=[pltpu.VMEM((D_MODEL, 2 * D_FF), BF16), pltpu.VMEM((D_FF, D_MODEL), BF16)],
    )
    return pl.pallas_call(
        _moe_body,
        grid_spec=grid_spec,
        out_shape=jax.ShapeDtypeStruct((N, D_MODEL), F32),
        compiler_params=_params(("arbitrary",)),
        name="moe_experts",
    )(unit_tile, unit_e, n_units, r_start, r_end, x_rows, w_gu, b_gu, w_dn, b_dn)


FLAT_BITS = 17


def _moe_routing(te, counts):
    bk = MOE_ROWS
    T = te.shape[0]
    N = T * TOP_K
    assert N % bk == 0 and N <= (1 << FLAT_BITS)
    experts = jnp.arange(N_EXPERTS, dtype=I32)
    top_e = te[:, :TOP_K]
    r_end = jnp.cumsum(counts).astype(I32)
    r_start = r_end - counts
    onehot = top_e[:, :, None] == experts[None, None, :]
    pos = jnp.sum(jnp.where(onehot, r_start[None, None, :], 0), axis=-1) + te[:, TOP_K:2 * TOP_K]
    key = jnp.left_shift(top_e.reshape(-1), FLAT_BITS) + jnp.arange(N, dtype=I32)
    key_s = lax.sort(key)
    tok_s = jnp.right_shift(jnp.bitwise_and(key_s, (1 << FLAT_BITS) - 1), 2)
    first = r_start // bk
    last = (r_end - 1) // bk
    n_e = jnp.where(counts > 0, last - first + 1, 0)
    u_end = jnp.cumsum(n_e).astype(I32)
    u_start = u_end - n_e
    n_units = u_end[-1]
    u = jnp.minimum(jnp.arange(N // bk + N_EXPERTS - 1, dtype=I32), n_units - 1)
    unit_e = jnp.sum((u[:, None] >= u_end[None, :]).astype(I32), axis=1)
    unit_tile = u + jnp.sum(jnp.where(unit_e[:, None] == experts[None, :], (first - u_start)[None, :], 0), axis=1)
    return pos, tok_s, (unit_tile, unit_e, n_units.reshape(1), r_start, r_end)


def _fin3_body(h2_ref, tg_ref, y0_ref, y1_ref, y2_ref, y3_ref, g_ref, b_ref, o_ref):
    gate = lambda k: tg_ref[:, k:k + 1]
    y = ((gate(0) * y0_ref[...] + gate(1) * y1_ref[...])
         + (gate(2) * y2_ref[...] + gate(3) * y3_ref[...]))
    o_ref[...] = _layer_norm(DN_ALPHA * h2_ref[...] + y, g_ref[...], b_ref[...])


def _fin3(h2, tg, ys, g, b, row_offset, T):
    tm = ROW_TILE
    blk0 = row_offset // tm
    row = lambda n: pl.BlockSpec((tm, n), lambda i: (i, 0))
    full = lambda a: pl.BlockSpec(a.shape, lambda i: (0,) * a.ndim)
    return pl.pallas_call(
        _fin3_body,
        grid=(T // tm,),
        in_specs=[pl.BlockSpec((tm, D_MODEL), lambda i: (i + blk0, 0)), pl.BlockSpec((tm, LANES), lambda i: (i + blk0, 0))]
                 + [pl.BlockSpec((tm, D_MODEL), lambda i: (i + blk0, 0))] * TOP_K + [full(g), full(b)],
        out_specs=row(D_MODEL),
        out_shape=jax.ShapeDtypeStruct((T, D_MODEL), F32),
        compiler_params=_params(("arbitrary",)),
        name="combine_ln3",
    )(h2, tg, *ys, g, b)


def kernel(x_prompt, x_sample, cache_cmp_kv, cache_slc_kv, state_win_kv, state_pool, cache_mem_kv, page_table,
           mem_prompt, w_in, pool_w, pool_scale, cmp_pe, cmp_w1, cmp_b1, cmp_w2, cmp_b2, w_out, ln1_g, ln1_b,
           mem_wq, mem_wkv, mem_wo, ln2_g, ln2_b, router_w, router_b, exp_w_gu, exp_b_gu, exp_w_dn, exp_b_dn,
           ln3_g, ln3_b):
    Bp, S, D = x_prompt.shape
    Bd, Ls, _ = x_sample.shape
    Tp, Ts = Bp * S, Bd * Ls
    l = 0
    w_in_bf = w_in[l].astype(BF16)
    pool_w_bf = pool_w[l].astype(BF16)
    ps = pool_scale[l][None, :]
    cw = _compress_weights(cmp_pe[l], cmp_w1[l], cmp_b1[l], cmp_w2[l], cmp_b2[l])
    w_out_bf = w_out[l].astype(BF16)
    wq_bf = mem_wq[l].astype(BF16)
    wo_bf = mem_wo[l].astype(BF16)
    vec = lambda a: a[l][None, :]

    up, qp, kvc_p, kvs_p, kvw_p, gp, pool_p, kvc_t, kvs_t = _inproj_prompt(
        x_prompt.reshape(Tp, D), w_in_bf, pool_w_bf, ps, S)
    kc_p = _compress_prompt(kvc_p.reshape(Bp, S, KV_WIDTH), cw)
    nsa_p = _nsa_prompt(qp, gp, kc_p, kvs_p.reshape(Bp, S, KV_WIDTH), kvw_p.reshape(Bp, S, KV_WIDTH))
    mem_kv_p = _matmul(mem_prompt.reshape(Bp * MEM_LEN, D), mem_wkv[l]).reshape(Bp, MEM_LEN, 2 * D)
    T = Tp + Ts
    *routed_p, cnt_p = _row_chain(jnp.zeros((1, N_EXPERTS), F32), x_prompt.reshape(Tp, D), pool_p, nsa_p, w_out_bf,
                                  vec(ln1_g), vec(ln1_b), wq_bf, mem_kv_p, wo_bf, vec(ln2_g), vec(ln2_b),
                                  router_w[l], vec(router_b), total_rows=T)

    state_pad = jnp.pad(state_pool[l], ((0, 0), (1, 0), (0, 0)))
    us, qs, kvc_s, kvs_s, kvw_s, gs, pool_s = _inproj_sample(
        x_sample.reshape(Ts, D), w_in_bf, pool_w_bf, ps, state_pad, Ls, PAST_LEN)
    feature_major = lambda a: jnp.transpose(a, (0, 2, 3, 4, 1))
    nsa_s, win_next = _nsa_sample(qs.reshape(Bd, Ls, NSA_WIDTH), gs.reshape(Bd, Ls, 3 * NSA_HEADS),
                        kvs_s.reshape(Bd, Ls, KV_WIDTH), kvw_s.reshape(Bd, Ls, KV_WIDTH),
                        feature_major(state_win_kv[l]), feature_major(cache_cmp_kv[l]),
                        feature_major(cache_slc_kv[l]), page_table, cw, PAST_LEN)
    h1_s, qm_s = _fin1(x_sample.reshape(Ts, D), pool_s, nsa_s.reshape(Ts, NSA_WIDTH), w_out_bf,
                       vec(ln1_g), vec(ln1_b), wq_bf, F32)
    om_s = _memattn_few(qm_s.reshape(Bd, Ls, D), cache_mem_kv[l]).reshape(Ts, D)
    h2, te, tg, cnt_s = _fin2(cnt_p, h1_s, om_s, wo_bf, vec(ln2_g), vec(ln2_b), router_w[l], vec(router_b),
                              total_rows=T, row_offset=Tp, into=routed_p)

    pos, tok_s, units = _moe_routing(te, cnt_s[0].astype(I32))
    y_rows = _moe_gmm(h2[tok_s], units, exp_w_gu[l], exp_b_gu[l][:, None, :],
                      exp_w_dn[l], exp_b_dn[l][:, None, :])
    ys = [y_rows[pos[:, k]] for k in range(TOP_K)]
    y_prompt = _fin3(h2, tg, ys, vec(ln3_g), vec(ln3_b), 0, Tp).reshape(Bp, S, D)
    y_sample = _fin3(h2, tg, ys, vec(ln3_g), vec(ln3_b), Tp, Ts).reshape(Bd, Ls, D)

    kv6 = lambda a, b, n: a.reshape(1, b, n, 2, KVH, DH)
    row_major = lambda a: jnp.transpose(a, (0, 4, 1, 2, 3))
    win_p = kvw_p.reshape(Bp, S, KV_WIDTH)[:, S - min(WINDOW, S):]
    pool_state_p = up.reshape(Bp, S, POOL_WIDTH)[:, S - POOL_STATE:]
    pool_state_s = jnp.concatenate([state_pool[l], us.reshape(Bd, Ls, POOL_WIDTH)], axis=1)[:, -POOL_STATE:]
    return (y_prompt, y_sample,
            row_major(kvc_t)[None], row_major(kvs_t)[None], kv6(win_p, Bp, min(WINDOW, S)),
            pool_state_p[None], mem_kv_p.reshape(1, Bp, MEM_LEN, 2, MEM_HEADS, MEM_HEAD_DIM),
            kv6(kvc_s, Bd, Ls), kv6(kvs_s, Bd, Ls), row_major(win_next)[None], pool_state_s[None])
```

```python
import functools

import jax
import jax.numpy as jnp
from jax import lax
from jax.experimental import pallas as pl
from jax.experimental.pallas import tpu as pltpu

F32 = jnp.float32
BF16 = jnp.bfloat16
I32 = jnp.int32

D_MODEL = 1024
POOL_WIDTH = 512
POOL_WINDOWS = (2, 4, 8, 16)
POOL_GROUP = 128
POOL_STATE = 15
NSA_WIDTH = 512
DH = 64
NSA_HEADS = 8
KVH = 2
HPG = 4
CMP_LEN = 32
CMP_STRIDE = 16
CMP_HIDDEN = 256
SEL_LEN = 64
SEL_TOP = 16
WINDOW = 512
Q_BLOCK = 128
KV_WIDTH = 256
ATTN_SCALE = DH ** -0.5
FORCED_SCORE = 1e4
NEG = -1e30
MEM_LEN = 256
MEM_HEADS = 4
MEM_HEAD_DIM = 256
N_EXPERTS = 32
TOP_K = 4
D_FF = 1024
SWIGLU_LIMIT = 7.0
SWIGLU_ALPHA = 1.702
DN_ALPHA = 2.0 ** 0.25
LN_EPS = 1e-5
PAST_LEN = 2048
PAGE_SIZE = 128

LANES = 128
SEL_PAD = 128
KEY_TILE = 512
ROW_TILE = 512
MOE_ROWS = 512
MEM_PER_STEP = 8
BF16_ROWS = 16
VMEM_LIMIT = 56 * 1024 * 1024

HIGHEST = lax.Precision.HIGHEST


def _dot(a, b):
    return jnp.dot(a, b, preferred_element_type=F32)


def _dot_nt(a, b, precision=None):
    return lax.dot_general(a, b, (((1,), (1,)), ((), ())), preferred_element_type=F32,
                           precision=precision)


def _layer_norm(x, g, b):
    mu = jnp.mean(x, axis=-1, keepdims=True)
    xc = x - mu
    var = jnp.mean(xc * xc, axis=-1, keepdims=True)
    return xc * lax.rsqrt(var + LN_EPS) * g + b


def _params(sem, vmem=VMEM_LIMIT):
    return pltpu.CompilerParams(dimension_semantics=sem, vmem_limit_bytes=vmem)


def _split_store(u, up_ref, q_ref, kvc_ref, kvs_ref, kvw_ref, gate_ref):
    o1 = POOL_WIDTH
    o2 = o1 + NSA_WIDTH
    o3 = o2 + KV_WIDTH
    o4 = o3 + KV_WIDTH
    o5 = o4 + KV_WIDTH
    up_ref[...] = u[:, :o1]
    q_ref[...] = u[:, o1:o2]
    kvc_ref[...] = u[:, o2:o3]
    kvs_ref[...] = u[:, o3:o4]
    kvw_ref[...] = u[:, o4:o5]
    gate_ref[...] = 1.0 / (1.0 + jnp.exp(-u[:, o5:]))


def _inproj_prompt_body(x_ref, w_ref, pw_ref, ps_ref,
                        up_ref, q_ref, kvc_ref, kvs_ref, kvw_ref, gate_ref, pool_ref, kvct_ref, kvst_ref,
                        ext_ref, *, tm, tiles_per_seq):
    halo = POOL_STATE + 1
    t_in_seq = pl.program_id(0) % tiles_per_seq
    u = _dot(x_ref[...].astype(BF16), w_ref[...])
    _split_store(u, up_ref, q_ref, kvc_ref, kvs_ref, kvw_ref, gate_ref)
    o2 = POOL_WIDTH + NSA_WIDTH
    kvct_ref[0] = u[:, o2:o2 + KV_WIDTH].T.reshape(2, KVH, DH, tm)
    kvst_ref[0] = u[:, o2 + KV_WIDTH:o2 + 2 * KV_WIDTH].T.reshape(2, KVH, DH, tm)

    @pl.when(t_in_seq == 0)
    def _():
        ext_ref[0:halo, :] = jnp.zeros((halo, POOL_WIDTH), F32)

    ext_ref[halo:halo + tm, :] = u[:, :POOL_WIDTH]
    pos = t_in_seq * tm + lax.broadcasted_iota(I32, (tm, 1), 0)
    for gi, w in enumerate(POOL_WINDOWS):
        cols = slice(gi * POOL_GROUP, (gi + 1) * POOL_GROUP)
        acc = ext_ref[halo:halo + tm, cols]
        for k in range(1, w):
            acc = acc + ext_ref[halo - k:halo - k + tm, cols]
        cnt = jnp.minimum(pos + 1, w).astype(F32)
        d = acc / cnt - ext_ref[halo:halo + tm, cols]
        o = _dot(d.astype(BF16), pw_ref[gi])
        pool_ref[:, cols] = (o * ps_ref[:, cols]).astype(pool_ref.dtype)
    ext_ref[0:halo, :] = ext_ref[tm:tm + halo, :]


def _inproj_prompt(x2d, w_in_bf, pool_w_bf, pool_scale, seq_len):
    T = x2d.shape[0]
    tm = ROW_TILE
    outs = [POOL_WIDTH, NSA_WIDTH, KV_WIDTH, KV_WIDTH, KV_WIDTH, 3 * NSA_HEADS, POOL_WIDTH]
    row = lambda n: pl.BlockSpec((tm, n), lambda i: (i, 0))
    full = lambda a: pl.BlockSpec(a.shape, lambda i: (0,) * a.ndim)
    tps = seq_len // tm
    kvt_spec = pl.BlockSpec((1, 2, KVH, DH, tm), lambda i: (i // tps, 0, 0, 0, i % tps))
    kvt_shape = jax.ShapeDtypeStruct((T // seq_len, 2, KVH, DH, seq_len), F32)
    return pl.pallas_call(
        functools.partial(_inproj_prompt_body, tm=tm, tiles_per_seq=tps),
        grid=(T // tm,),
        in_specs=[row(D_MODEL), full(w_in_bf), full(pool_w_bf), full(pool_scale)],
        out_specs=[row(n) for n in outs] + [kvt_spec] * 2,
        out_shape=[jax.ShapeDtypeStruct((T, n), F32) for n in outs[:-1]]
                  + [jax.ShapeDtypeStruct((T, outs[-1]), BF16)] + [kvt_shape] * 2,
        scratch_shapes=[pltpu.VMEM((tm + POOL_STATE + 1, POOL_WIDTH), F32)],
        compiler_params=_params(("arbitrary",)),
        name="inproj_prompt",
    )(x2d, w_in_bf, pool_w_bf, pool_scale)


def _inproj_sample_body(x_ref, w_ref, pw_ref, ps_ref, st_ref,
                        up_ref, q_ref, kvc_ref, kvs_ref, kvw_ref, gate_ref, pool_ref,
                        ext_ref, *, nb, ls, pos0):
    halo = POOL_STATE + 1
    tm = nb * ls
    u = _dot(x_ref[...].astype(BF16), w_ref[...])
    _split_store(u, up_ref, q_ref, kvc_ref, kvs_ref, kvw_ref, gate_ref)
    ext_ref[:, 0:halo, :] = st_ref[...]
    ext_ref[:, halo:halo + ls, :] = u[:, :POOL_WIDTH].reshape(nb, ls, POOL_WIDTH)
    pos = pos0 + lax.broadcasted_iota(I32, (1, ls, 1), 1)
    for gi, w in enumerate(POOL_WINDOWS):
        cols = slice(gi * POOL_GROUP, (gi + 1) * POOL_GROUP)
        acc = ext_ref[:, halo:halo + ls, cols]
        for k in range(1, w):
            acc = acc + ext_ref[:, halo - k:halo - k + ls, cols]
        cnt = jnp.minimum(pos + 1, w).astype(F32)
        d = acc / cnt - ext_ref[:, halo:halo + ls, cols]
        o = _dot(d.reshape(tm, POOL_GROUP).astype(BF16), pw_ref[gi])
        pool_ref[:, cols] = (o * ps_ref[:, cols]).astype(pool_ref.dtype)


def _inproj_sample(x2d, w_in_bf, pool_w_bf, pool_scale, state_pad, ls, pos0):
    T = x2d.shape[0]
    nb = ROW_TILE // ls
    tm = nb * ls
    outs = [POOL_WIDTH, NSA_WIDTH, KV_WIDTH, KV_WIDTH, KV_WIDTH, 3 * NSA_HEADS, POOL_WIDTH]
    row = lambda n: pl.BlockSpec((tm, n), lambda i: (i, 0))
    full = lambda a: pl.BlockSpec(a.shape, lambda i: (0,) * a.ndim)
    return pl.pallas_call(
        functools.partial(_inproj_sample_body, nb=nb, ls=ls, pos0=pos0),
        grid=(T // tm,),
        in_specs=[row(D_MODEL), full(w_in_bf), full(pool_w_bf), full(pool_scale),
                  pl.BlockSpec((nb, POOL_STATE + 1, POOL_WIDTH), lambda i: (i, 0, 0))],
        out_specs=[row(n) for n in outs],
        out_shape=[jax.ShapeDtypeStruct((T, n), F32) for n in outs],
        scratch_shapes=[pltpu.VMEM((nb, POOL_STATE + 1 + ls, POOL_WIDTH), F32)],
        compiler_params=_params(("arbitrary",)),
        name="inproj_sample",
    )(x2d, w_in_bf, pool_w_bf, pool_scale, state_pad)


def _matmul_body(x_ref, w_ref, o_ref):
    o_ref[...] = _dot(x_ref[...].astype(BF16), w_ref[...].astype(BF16))


def _matmul(x, w, tn=512):
    M, K = x.shape
    N = w.shape[1]
    return pl.pallas_call(
        _matmul_body,
        grid=(N // tn,),
        in_specs=[pl.BlockSpec((M, K), lambda j: (0, 0)), pl.BlockSpec((K, tn), lambda j: (0, j))],
        out_specs=pl.BlockSpec((M, tn), lambda j: (0, j)),
        out_shape=jax.ShapeDtypeStruct((M, N), F32),
        compiler_params=_params(("arbitrary",)),
        name="mem_kv_proj",
    )(x, w)


def _gelu_tanh(x):
    c = 0.7978845608028654
    return 0.5 * x * (1.0 + jnp.tanh(c * (x + 0.044715 * (x * x * x))))


def _compress(kv_refs, n_chunks, w1_ref, pe_ref, b1_ref, w2_ref, b2_ref):
    lo = _lane_iota(n_chunks) < DH
    quads = CMP_STRIDE // 4
    outs = []
    for c in range(2):
        acc_a = jnp.zeros((2 * n_chunks, CMP_HIDDEN), F32)
        acc_b = jnp.zeros((2 * n_chunks, CMP_HIDDEN), F32)
        for i in range(quads):
            x = [kv_refs[c][pl.ds(4 * i + m, n_chunks, stride=CMP_STRIDE), :] for m in range(4)]
            r = [pltpu.roll(v, DH, 1) for v in x]
            x_g0 = jnp.concatenate([jnp.where(lo, x[0], r[1]), jnp.where(lo, x[2], r[3])], axis=1)
            x_g1 = jnp.concatenate([jnp.where(lo, r[0], x[1]), jnp.where(lo, r[2], x[3])], axis=1)
            xq = jnp.concatenate([x_g0, x_g1], axis=0)
            acc_a = acc_a + _dot((xq + pe_ref[c, i:i + 1, :]).astype(BF16), w1_ref[c, i])
            acc_b = acc_b + _dot((xq + pe_ref[c, quads + i:quads + i + 1, :]).astype(BF16), w1_ref[c, quads + i])
        hid = acc_a + pltpu.roll(acc_b, 2 * n_chunks - 1, 0) + b1_ref[c]
        outs.append(_dot(_gelu_tanh(hid).astype(BF16), w2_ref[c]) + b2_ref[c])
    return outs


def _compress_prompt_body(kv_ref, w1_ref, pe_ref, b1_ref, w2_ref, b2_ref, o_ref, k_ref, v_ref, *, n_chunks):
    k_ref[...] = kv_ref[0, :, :LANES]
    v_ref[...] = kv_ref[0, :, LANES:]
    keys, values = _compress((k_ref, v_ref), n_chunks, w1_ref, pe_ref, b1_ref, w2_ref, b2_ref)
    o_ref[0, 0] = keys
    o_ref[0, 1] = values


def _compress_prompt(kvc, cw):
    B, S, _ = kvc.shape
    n_chunks = S // CMP_STRIDE
    full = lambda a: pl.BlockSpec(a.shape, lambda b: (0,) * a.ndim)
    return pl.pallas_call(
        functools.partial(_compress_prompt_body, n_chunks=n_chunks),
        grid=(B,),
        in_specs=[pl.BlockSpec((1, S, KV_WIDTH), lambda b: (b, 0, 0))] + [full(a) for a in cw],
        out_specs=pl.BlockSpec((1, 2, KVH * n_chunks, LANES), lambda b: (b, 0, 0, 0)),
        out_shape=jax.ShapeDtypeStruct((B, 2, KVH * n_chunks, LANES), F32),
        scratch_shapes=[pltpu.VMEM((S, LANES), F32)] * 2,
        compiler_params=_params(("arbitrary",)),
        name="compress_prompt",
    )(kvc, *cw)


def _compress_weights(cmp_pe, cmp_w1, cmp_b1, cmp_w2, cmp_b2):
    nq = CMP_LEN // 4
    w1 = cmp_w1.reshape(2, nq, 4 * DH, CMP_HIDDEN).astype(BF16)
    pe = cmp_pe.reshape(2, nq, 4 * DH)
    b1 = cmp_b1[:, None, :]
    w2 = jnp.stack([jnp.concatenate([cmp_w2[0], jnp.zeros_like(cmp_w2[0])], axis=1),
                    jnp.concatenate([cmp_w2[1], cmp_w2[1]], axis=1)]).astype(BF16)
    b2 = jnp.stack([jnp.concatenate([cmp_b2[0], jnp.zeros_like(cmp_b2[0])]),
                    jnp.concatenate([cmp_b2[1], cmp_b2[1]])])[:, None, :]
    return w1, pe, b1, w2, b2


def _softmax_rows(s):
    e = jnp.exp(s - jnp.max(s, axis=-1, keepdims=True))
    return e * (1.0 / jnp.sum(e, axis=-1, keepdims=True))


def _lane_iota(n):
    return lax.broadcasted_iota(I32, (n, LANES), 1)


def _key_alibi_cols(pos, lane):
    hi = jnp.left_shift(jnp.right_shift(pos, 6), 6).astype(F32)
    lo = jnp.bitwise_and(pos, SEL_LEN - 1).astype(F32)
    return jnp.where(lane == DH, hi,
                     jnp.where(lane == DH + 1, lo,
                               jnp.where((lane == DH + 2) | (lane == DH + 3), 1.0, 0.0)))


def _key_alibi_rows(pos, sub):
    hi = jnp.left_shift(jnp.right_shift(pos, 6), 6).astype(F32)
    lo = jnp.bitwise_and(pos, SEL_LEN - 1).astype(F32)
    return jnp.where(sub == 0, hi, jnp.where(sub == 1, lo, jnp.where((sub == 2) | (sub == 3), 1.0, 0.0)))


def _halves(x, zero_hi=False):
    lo = _lane_iota(x.shape[0]) < DH
    r = pltpu.roll(x, DH, 1)
    if zero_hi:
        return jnp.where(lo, x, 0.0), jnp.where(lo, r, 0.0)
    return jnp.where(lo, x, r), jnp.where(lo, r, x)


def _fill_queries(qa_ref, qs, q_pos, lq):
    lane = _lane_iota(lq)
    lo_half = lane < DH
    q_hi = jnp.left_shift(jnp.right_shift(q_pos, 7), 7).astype(F32)
    q_lo = jnp.bitwise_and(q_pos, LANES - 1).astype(F32)
    for h in range(NSA_HEADS):
        g, hl = divmod(h, HPG)
        slope = 2.0 ** (-(h + 1))
        slab = qs[:, (h // 2) * LANES:(h // 2 + 1) * LANES]
        if h % 2:
            slab = pltpu.roll(slab, DH, 1)
        ex = jnp.where((lane == DH) | (lane == DH + 1), slope,
                       jnp.where(lane == DH + 2, -slope * q_hi,
                                 jnp.where(lane == DH + 3, -slope * q_lo,
                                           jnp.where(lane == DH + 4, NEG, 0.0))))
        qa_ref[g, hl * lq:(hl + 1) * lq, 0:LANES] = jnp.where(lo_half, slab, ex).astype(BF16)


def _stack4(x):
    return jnp.concatenate([x] * HPG, axis=0)


def _cmp_branch(qa_ref, kc_k, kc_v, q_pos4, lq, n_cmp):
    c_end = lax.broadcasted_iota(I32, (1, n_cmp), 1) * CMP_STRIDE + (CMP_LEN - 1)
    m_c = c_end <= q_pos4
    any_c = (q_pos4 >= CMP_LEN - 1).astype(F32)
    outs, psums = [], []
    for g in range(KVH):
        s = jnp.where(m_c, _dot_nt(qa_ref[g, :, 0:LANES], kc_k[g][...]), NEG)
        p = _softmax_rows(s) * any_c
        outs.append(_dot(p.astype(BF16), kc_v[g][...]))
        psums.append(p[0:lq] + p[lq:2 * lq] + p[2 * lq:3 * lq] + p[3 * lq:4 * lq])
    return outs, psums


def _split3(x):
    hi = x.astype(BF16)
    r1 = x - hi.astype(F32)
    mid = r1.astype(BF16)
    lo = (r1 - mid.astype(F32)).astype(BF16)
    return hi, mid, lo


def _top_blocks_t(imp_ts, pos0, n_blk=SEL_PAD):
    blk = lax.broadcasted_iota(I32, (n_blk, LANES), 0)
    qp_t = pos0 + lax.broadcasted_iota(I32, (n_blk, LANES), 1)
    cur = jnp.right_shift(qp_t, 6)
    forced = (blk == 0) | (blk == cur) | (blk == cur - 1)
    valid = jnp.left_shift(blk, 6) <= qp_t
    v = jnp.concatenate([jnp.where(valid, jnp.where(forced, FORCED_SCORE, t), -1.0) for t in imp_ts], axis=1)
    blk_f = lax.broadcasted_iota(I32, (n_blk, KVH * LANES), 0).astype(F32)
    sel = jnp.zeros((n_blk, KVH * LANES), F32)
    for _ in range(SEL_TOP):
        m = jnp.max(v, axis=0, keepdims=True)
        idx = jnp.min(jnp.where(v == m, blk_f, float(n_blk)), axis=0, keepdims=True)
        hit = blk_f == idx
        sel = jnp.where(hit, 1.0, sel)
        v = jnp.where(hit, -jnp.inf, v)
    out = [jnp.where((sel[:, g * LANES:(g + 1) * LANES] > 0.5) & valid, 0.0, NEG) for g in range(KVH)]
    if n_blk < SEL_PAD:
        out = [jnp.concatenate([b, jnp.full((SEL_PAD - n_blk, LANES), NEG, F32)], axis=0) for b in out]
    return out


def _select_blocks(psums, ovt_ref, pos0, lq, n_blk):
    imp_ts = []
    for g in range(KVH):
        ps = psums[g]
        if lq < LANES:
            ps = jnp.concatenate([ps, jnp.zeros((LANES - lq, ps.shape[1]), F32)], axis=0)
        imp_ts.append(_dot_nt(ovt_ref[0:n_blk, :], ps, precision=HIGHEST))
    return [b.T[:lq].astype(BF16) for b in _top_blocks_t(imp_ts, pos0, n_blk)]


def _store_selbias(qa_ref, selbias, lq):
    for g in range(KVH):
        for hl in range(HPG):
            qa_ref[g, hl * lq:(hl + 1) * lq, LANES:2 * LANES] = selbias[g]


def _combine(gates, eg_ref, o_c, o_s, o_w, lq):
    lo_half = _lane_iota(lq) < DH

    def assemble(per_group):
        slabs = []
        for k in range(NSA_HEADS // 2):
            g, hl = divmod(2 * k, HPG)
            a = per_group[g][hl * lq:(hl + 1) * lq]
            b = per_group[g][(hl + 1) * lq:(hl + 2) * lq]
            slabs.append(jnp.where(lo_half, a, b))
        return jnp.concatenate(slabs, axis=1)

    return (_dot(gates, eg_ref[0]) * assemble(o_c)
            + _dot(gates, eg_ref[1]) * assemble(o_s)
            + _dot(gates, eg_ref[2]) * assemble(o_w))


def _gate_expand():
    r = jnp.arange(3 * NSA_HEADS)
    c = jnp.arange(NSA_WIDTH)
    return jnp.stack([(r[:, None] == 3 * (c[None, :] // DH) + br).astype(F32) for br in range(3)])


def _overlap_t(n_cmp):
    n = jnp.arange(n_cmp)
    s = jnp.arange(SEL_PAD)
    c_first = n * CMP_STRIDE
    c_end = c_first + CMP_LEN - 1
    b_first = s * SEL_LEN
    return ((c_first[None, :] < b_first[:, None] + SEL_LEN) & (c_end[None, :] >= b_first[:, None])).astype(F32)


def _key_rows(kv_f32, pos, invalid=None):
    n = kv_f32.shape[0]
    lane = _lane_iota(n)
    ex = _key_alibi_cols(pos, lane)
    if invalid is not None:
        ex = jnp.where((lane == DH + 4) & invalid, 1.0, ex)
    lo = lane < DH
    return jnp.where(lo, kv_f32, ex), jnp.where(lo, pltpu.roll(kv_f32, DH, 1), ex)


def _block_onehot(pos, n):
    return (lax.broadcasted_iota(I32, (n, SEL_PAD), 1) == jnp.right_shift(pos, 6)).astype(BF16)


def _tile4(x):
    return jnp.concatenate([x] * HPG, axis=1)


def _nsa_prompt_body(q_ref, gate_ref, kc_ref, kvs_ref, w0_ref, w1_ref, w2_ref, w3_ref, w4_ref,
                     ovt_ref, egt_ref, wband_ref, o_ref,
                     kck0, kck1, kcv0, kcv1, ka0, ka1, vt0, vt1, qa0, qa1,
                     sa_ref, sb_ref, m_ref, l_ref, acc_ref, oc_ref, *, seq_len):
    j = pl.program_id(1)
    kc_k, kc_vt, kaug, v_t, qa_t = (kck0, kck1), (kcv0, kcv1), (ka0, ka1), (vt0, vt1), (qa0, qa1)
    n_cmp = kc_ref.shape[2] // KVH
    lq = Q_BLOCK
    cols = HPG * lq

    @pl.when(j == 0)
    def _():
        for g in range(KVH):
            kc_k[g][...] = kc_ref[0, 0, g * n_cmp:(g + 1) * n_cmp, :].astype(BF16)
            kc_vt[g][...] = kc_ref[0, 1, g * n_cmp:(g + 1) * n_cmp, :].T[0:DH].astype(BF16)

        def build(i, _):
            r0 = pl.multiple_of(i * KEY_TILE, KEY_TILE)
            pos = r0 + lax.broadcasted_iota(I32, (KEY_TILE, 1), 0)
            k0, k1 = _key_rows(kvs_ref[0, pl.ds(r0, KEY_TILE), :LANES], pos)
            onehot = _block_onehot(pos, KEY_TILE)
            ka0[pl.ds(r0, KEY_TILE), :] = jnp.concatenate([k0.astype(BF16), onehot], axis=1)
            ka1[pl.ds(r0, KEY_TILE), :] = jnp.concatenate([k1.astype(BF16), onehot], axis=1)
            vt = kvs_ref[0, pl.ds(r0, KEY_TILE), LANES:].T.astype(BF16)
            vt0[:, pl.ds(r0, KEY_TILE)] = vt[0:DH]
            vt1[:, pl.ds(r0, KEY_TILE)] = vt[DH:]
            return 0

        lax.fori_loop(0, seq_len // KEY_TILE, build, 0)

    st = j * Q_BLOCK
    q_pos = st + lax.broadcasted_iota(I32, (1, lq), 1)
    q_pos4 = _tile4(q_pos)

    q_t = (q_ref[...] * ATTN_SCALE).T
    sub = lax.broadcasted_iota(I32, (DH, lq), 0)
    q_hi = jnp.left_shift(jnp.right_shift(q_pos, 7), 7).astype(F32)
    q_lo = jnp.bitwise_and(q_pos, LANES - 1).astype(F32)
    for h in range(NSA_HEADS):
        g, hl = divmod(h, HPG)
        slope = 2.0 ** (-(h + 1))
        ex = jnp.where(sub <= 1, slope,
                       jnp.where(sub == 2, -slope * q_hi,
                                 jnp.where(sub == 3, -slope * q_lo, jnp.where(sub == 4, NEG, 0.0))))
        qa_t[g][0:DH, hl * lq:(hl + 1) * lq] = q_t[h * DH:(h + 1) * DH].astype(BF16)
        qa_t[g][DH:2 * DH, hl * lq:(hl + 1) * lq] = ex.astype(BF16)

    any_c = (q_pos4 >= CMP_LEN - 1).astype(F32)
    blocks_per_sel = SEL_LEN // CMP_STRIDE
    n_variants = -(-n_cmp // LANES)
    last_visible = (st + Q_BLOCK - CMP_LEN) // CMP_STRIDE

    def cmp_and_select(n_c):
        n_b = n_c // blocks_per_sel
        c_end = lax.broadcasted_iota(I32, (n_c, 1), 0) * CMP_STRIDE + (CMP_LEN - 1)
        m_c = c_end <= q_pos4
        ovt_bf = ovt_ref[0:n_b, 0:n_c].astype(BF16)
        imp_ts = []
        for g in range(KVH):
            s = jnp.where(m_c, _dot(kc_k[g][0:n_c, :], qa_t[g][0:LANES, :]), NEG)
            e = jnp.exp(s - jnp.max(s, axis=0, keepdims=True))
            p = e * (any_c / jnp.sum(e, axis=0, keepdims=True))
            oc_ref[g] = _dot(kc_vt[g][:, 0:n_c], p.astype(BF16))
            psum = p[:, 0:lq] + p[:, lq:2 * lq] + p[:, 2 * lq:3 * lq] + p[:, 3 * lq:4 * lq]
            imp_ts.append(sum(_dot(ovt_bf, t) for t in _split3(psum)))
        bias = _top_blocks_t(imp_ts, st, n_b)
        for g in range(KVH):
            for hl in range(HPG):
                qa_t[g][2 * DH:, hl * lq:(hl + 1) * lq] = bias[g].astype(BF16)

    variant = jnp.minimum(jnp.maximum(last_visible, 0) // LANES, n_variants - 1)
    for k in range(n_variants):
        pl.when(variant == k)(functools.partial(cmp_and_select, min(n_cmp, (k + 1) * LANES)))
    o_c = [oc_ref[g] for g in range(KVH)]

    n_tiles = (st + Q_BLOCK + KEY_TILE - 1) // KEY_TILE
    for g in range(KVH):
        m_ref[g] = jnp.full((1, cols), NEG, F32)
        l_ref[g] = jnp.zeros((1, cols), F32)
        acc_ref[g] = jnp.zeros((DH, cols), F32)

    def scores(t, s_ref):
        r0 = pl.multiple_of(t * KEY_TILE, KEY_TILE)
        for g in range(KVH):
            s_ref[g] = _dot(kaug[g][pl.ds(r0, KEY_TILE), :], qa_t[g][...])

    def consume(t, s_ref, masked):
        r0 = pl.multiple_of(t * KEY_TILE, KEY_TILE)
        for g in range(KVH):
            s = s_ref[g]
            if masked:
                k_pos = r0 + lax.broadcasted_iota(I32, (KEY_TILE, 1), 0)
                s = jnp.where(k_pos <= q_pos4, s, NEG)
            m = m_ref[g]
            m_new = jnp.maximum(m, jnp.max(s, axis=0, keepdims=True))
            a = jnp.exp(m - m_new)
            e = jnp.exp(s - m_new)
            m_ref[g] = m_new
            l_ref[g] = a * l_ref[g] + jnp.sum(e, axis=0, keepdims=True)
            acc_ref[g] = a * acc_ref[g] + _dot(v_t[g][:, pl.ds(r0, KEY_TILE)], e.astype(BF16))

    scores(0, sa_ref)
    n_pairs = (n_tiles - 1) // 2

    def pair(u, _):
        scores(2 * u + 1, sb_ref)
        consume(2 * u, sa_ref, False)
        scores(2 * u + 2, sa_ref)
        consume(2 * u + 1, sb_ref, False)
        return 0

    lax.fori_loop(0, n_pairs, pair, 0)
    odd_tail = (n_tiles - 1) - 2 * n_pairs == 1

    @pl.when(odd_tail)
    def _():
        scores(n_tiles - 1, sb_ref)
        consume(n_tiles - 2, sa_ref, False)
        consume(n_tiles - 1, sb_ref, True)

    @pl.when(jnp.logical_not(odd_tail))
    def _():
        consume(n_tiles - 1, sa_ref, True)

    o_s = [acc_ref[g] * (1.0 / l_ref[g]) for g in range(KVH)]

    band = jnp.concatenate([w0_ref[0], w1_ref[0], w2_ref[0], w3_ref[0], w4_ref[0]], axis=0)
    n_win = band.shape[0]
    w_pos_col = st - WINDOW + lax.broadcasted_iota(I32, (n_win, 1), 0)
    kw_k = _key_rows(band[:, :LANES], jnp.maximum(w_pos_col, 0), invalid=w_pos_col < 0)
    vw_t = band[:, LANES:].T.astype(BF16)
    band_bias = _tile4(wband_ref[...])
    o_w = []
    for g in range(KVH):
        s = _dot(kw_k[g].astype(BF16), qa_t[g][0:LANES, :]) + band_bias
        e = jnp.exp(s - jnp.max(s, axis=0, keepdims=True))
        o_w.append(_dot(vw_t[g * DH:(g + 1) * DH], e.astype(BF16)) * (1.0 / jnp.sum(e, axis=0, keepdims=True)))

    def heads(per_group):
        return jnp.concatenate([per_group[h // HPG][:, (h % HPG) * lq:(h % HPG + 1) * lq]
                                for h in range(NSA_HEADS)], axis=0)

    gates = gate_ref[...]
    out_t = (_dot_nt(egt_ref[0], gates) * heads(o_c)
             + _dot_nt(egt_ref[1], gates) * heads(o_s)
             + _dot_nt(egt_ref[2], gates) * heads(o_w))
    o_ref[...] = out_t.T.astype(o_ref.dtype)


def _gate_expand_t():
    r = jnp.arange(3 * NSA_HEADS)
    c = jnp.arange(NSA_WIDTH)
    return jnp.stack([(3 * (c[:, None] // DH) + br == r[None, :]).astype(F32) for br in range(3)])


def _nsa_prompt(q, gates, kc, kvs, kvw):
    B, S, _ = kvs.shape
    nqb = S // Q_BLOCK
    n_cmp = kc.shape[2] // KVH
    ovt = _overlap_t(n_cmp)
    egt = _gate_expand_t()
    n_band = WINDOW // Q_BLOCK + 1
    d_band = jnp.arange(Q_BLOCK)[None, :] + WINDOW - jnp.arange(n_band * Q_BLOCK)[:, None]
    wband = jnp.where((d_band >= 0) & (d_band < WINDOW), 0.0, NEG).astype(F32)

    def band_spec(i):
        return pl.BlockSpec((1, Q_BLOCK, KV_WIDTH),
                            lambda b, j, i=i: (b, jnp.maximum(j - (n_band - 1) + i, 0), 0))

    full = lambda a: pl.BlockSpec(a.shape, lambda b, j: (0,) * a.ndim)
    return pl.pallas_call(
        functools.partial(_nsa_prompt_body, seq_len=S),
        grid=(B, nqb),
        in_specs=[pl.BlockSpec((Q_BLOCK, NSA_WIDTH), lambda b, j: (b * nqb + j, 0)),
                  pl.BlockSpec((Q_BLOCK, 3 * NSA_HEADS), lambda b, j: (b * nqb + j, 0)),
                  pl.BlockSpec((1,) + kc.shape[1:], lambda b, j: (b, 0, 0, 0)),
                  pl.BlockSpec((1, S, KV_WIDTH), lambda b, j: (b, 0, 0))]
                 + [band_spec(i) for i in range(n_band)] + [full(ovt), full(egt), full(wband)],
        out_specs=pl.BlockSpec((Q_BLOCK, NSA_WIDTH), lambda b, j: (b * nqb + j, 0)),
        out_shape=jax.ShapeDtypeStruct((B * S, NSA_WIDTH), BF16),
        scratch_shapes=[pltpu.VMEM((n_cmp, LANES), BF16)] * 2
                       + [pltpu.VMEM((DH, n_cmp), BF16)] * 2
                       + [pltpu.VMEM((S, 2 * LANES), BF16)] * 2
                       + [pltpu.VMEM((DH, S), BF16)] * 2
                       + [pltpu.VMEM((2 * LANES, HPG * Q_BLOCK), BF16)] * 2
                       + [pltpu.VMEM((KVH, KEY_TILE, HPG * Q_BLOCK), F32)] * 2
                       + [pltpu.VMEM((KVH, 1, HPG * Q_BLOCK), F32)] * 2
                       + [pltpu.VMEM((KVH, DH, HPG * Q_BLOCK), F32)] * 2,
        compiler_params=_params(("arbitrary", "arbitrary")),
        name="nsa_prompt",
    )(q, gates, kc, kvs, *([kvw] * n_band), ovt, egt, wband)


def _nsa_sample_body(pt_ref, q_ref, gate_ref, kvs_new_ref, win_ref, kvw_new_ref, *rest,
                     n_pages, ls, past_len):
    cmp_pages = rest[:n_pages]
    slc_pages = rest[n_pages:2 * n_pages]
    (w1_ref, pe_ref, b1_ref, w2_ref, b2_ref, ovt_ref, eg_ref, o_ref, wout_ref,
     full_k, full_v, kck0, kck1, kcv0, kcv1, kt0, kt1, vt0, vt1, wkt0, wkt1, qa_ref) = rest[2 * n_pages:]
    del pt_ref
    kc_k, kc_v, kaug_t, v_t, wk_t = (kck0, kck1), (kcv0, kcv1), (kt0, kt1), (vt0, vt1), (wkt0, wkt1)
    n_cmp = past_len // CMP_STRIDE
    w_rows = win_ref.shape[4]
    lq = BF16_ROWS
    rows = HPG * lq
    w_start = past_len - w_rows

    @pl.when(pl.program_id(0) == 0)
    def _():
        sub = lax.broadcasted_iota(I32, (DH, past_len), 0)
        pos = lax.broadcasted_iota(I32, (1, past_len), 1)
        ex = _key_alibi_rows(pos, sub).astype(BF16)
        onehot = (lax.broadcasted_iota(I32, (SEL_PAD, past_len), 0) == jnp.right_shift(pos, 6)).astype(BF16)
        subw = lax.broadcasted_iota(I32, (DH, w_rows), 0)
        exw = _key_alibi_rows(w_start + lax.broadcasted_iota(I32, (1, w_rows), 1), subw).astype(BF16)
        for g in range(KVH):
            kaug_t[g][DH:2 * DH, :] = ex
            kaug_t[g][2 * DH:, :] = onehot
            wk_t[g][DH:, :] = exw

    for p in range(n_pages):
        cols = slice(p * PAGE_SIZE, (p + 1) * PAGE_SIZE)
        full_k[cols, :] = cmp_pages[p][0, 0].reshape(2 * DH, PAGE_SIZE).T
        full_v[cols, :] = cmp_pages[p][0, 1].reshape(2 * DH, PAGE_SIZE).T
        for g in range(KVH):
            kaug_t[g][0:DH, cols] = slc_pages[p][0, 0, g].astype(BF16)
            vt = slc_pages[p][0, 1, g].astype(BF16)
            v_t[g][0:DH, cols] = vt
            v_t[g][DH:, cols] = vt
    for g in range(KVH):
        wk_t[g][0:DH, :] = win_ref[0, 0, g].astype(BF16)

    keys_c, values_c = _compress((full_k, full_v), n_cmp, w1_ref, pe_ref, b1_ref, w2_ref, b2_ref)
    for g in range(KVH):
        kc_k[g][...] = keys_c[g * n_cmp:(g + 1) * n_cmp].astype(BF16)
        kc_v[g][...] = values_c[g * n_cmp:(g + 1) * n_cmp].astype(BF16)

    pad_q = jnp.zeros((lq - ls, NSA_WIDTH), F32)
    q_pos = past_len + lax.broadcasted_iota(I32, (lq, 1), 0)
    q_pos4 = _stack4(q_pos)
    _fill_queries(qa_ref, jnp.concatenate([q_ref[0] * ATTN_SCALE, pad_q], axis=0), q_pos, lq)
    gates = jnp.concatenate([gate_ref[0], jnp.zeros((lq - ls, 3 * NSA_HEADS), F32)], axis=0)

    o_c, psums = _cmp_branch(qa_ref, kc_k, kc_v, q_pos4, lq, n_cmp)
    n_sel = -(-(past_len + lq) // SEL_LEN)
    n_blk = -(-n_sel // 8) * 8
    _store_selbias(qa_ref, _select_blocks(psums, ovt_ref, past_len, lq, n_blk), lq)

    pad_k = jnp.zeros((LANES - ls, KV_WIDTH), F32)
    new_pos_col = past_len + lax.broadcasted_iota(I32, (LANES, 1), 0)
    new_pos = past_len + lax.broadcasted_iota(I32, (1, LANES), 1)
    new_s = jnp.concatenate([kvs_new_ref[0], pad_k], axis=0)
    new_w = jnp.concatenate([kvw_new_ref[0], pad_k], axis=0)
    ks_new = _key_rows(new_s[:, :LANES], new_pos_col)
    vs_new = _halves(new_s[:, LANES:])
    kw_new = _key_rows(new_w[:, :LANES], new_pos_col)
    vw_new = _halves(new_w[:, LANES:])
    onehot_new = _block_onehot(new_pos_col, LANES)
    causal_new = q_pos4 >= new_pos

    o_s, o_w = [], []
    d_past = q_pos4 - (w_start + lax.broadcasted_iota(I32, (1, w_rows), 1))
    m_past = (d_past >= 0) & (d_past < WINDOW)
    d_new = q_pos4 - new_pos
    m_new = (d_new >= 0) & (d_new < WINDOW)
    for g in range(KVH):
        s_past = _dot(qa_ref[g], kaug_t[g][...])
        k_new = jnp.concatenate([ks_new[g].astype(BF16), onehot_new], axis=1)
        s_new = jnp.where(causal_new, _dot_nt(qa_ref[g], k_new), NEG)
        m = jnp.maximum(jnp.max(s_past, axis=-1, keepdims=True), jnp.max(s_new, axis=-1, keepdims=True))
        e_past = jnp.exp(s_past - m)
        e_new = jnp.exp(s_new - m)
        den = jnp.sum(e_past, axis=-1, keepdims=True) + jnp.sum(e_new, axis=-1, keepdims=True)
        acc = _dot_nt(e_past.astype(BF16), v_t[g][...]) + _dot(e_new.astype(BF16), vs_new[g].astype(BF16))
        o_s.append(acc / den)

        sw_past = jnp.where(m_past, _dot(qa_ref[g, :, 0:LANES], wk_t[g][...]), NEG)
        sw_new = jnp.where(m_new, _dot_nt(qa_ref[g, :, 0:LANES], kw_new[g].astype(BF16)), NEG)
        m = jnp.maximum(jnp.max(sw_past, axis=-1, keepdims=True), jnp.max(sw_new, axis=-1, keepdims=True))
        e_past = jnp.exp(sw_past - m)
        e_new = jnp.exp(sw_new - m)
        den = jnp.sum(e_past, axis=-1, keepdims=True) + jnp.sum(e_new, axis=-1, keepdims=True)
        vw = win_ref[0, 1, g].astype(BF16)
        vw2 = jnp.concatenate([vw, vw], axis=0)
        acc = _dot_nt(e_past.astype(BF16), vw2) + _dot(e_new.astype(BF16), vw_new[g].astype(BF16))
        o_w.append(acc / den)

    o_ref[0] = _combine(gates, eg_ref, o_c, o_s, o_w, lq)[:ls]

    keep = w_rows - ls
    tail_lane = lax.broadcasted_iota(I32, (DH, LANES), 1) >= keep % LANES
    for c in range(2):
        new_t = pltpu.roll(new_w[:, c * LANES:(c + 1) * LANES].T, keep % LANES, 1)
        for g in range(KVH):
            shifted = pltpu.roll(win_ref[0, c, g], keep, 1)
            last = jnp.where(tail_lane, new_t[g * DH:(g + 1) * DH], shifted[:, w_rows - LANES:])
            wout_ref[0, c, g] = jnp.concatenate([shifted[:, :w_rows - LANES], last], axis=1)


def _nsa_sample(q, gates, kvs_new, kvw_new, win_t, cmp_t, slc_t, page_table, cw, past_len):
    Bd, ls, _ = q.shape
    n_pages = page_table.shape[1]
    n_cmp = past_len // CMP_STRIDE
    ovt = _overlap_t(n_cmp)
    eg = _gate_expand()
    w_rows = win_t.shape[4]

    per_b = lambda a: pl.BlockSpec((1,) + a.shape[1:], lambda b, pt: (b,) + (0,) * (a.ndim - 1))
    full = lambda a: pl.BlockSpec(a.shape, lambda b, pt: (0,) * a.ndim)
    page = lambda p: pl.BlockSpec((1, 2, KVH, DH, PAGE_SIZE), lambda b, pt, p=p: (pt[b, p], 0, 0, 0, 0))
    grid_spec = pltpu.PrefetchScalarGridSpec(
        num_scalar_prefetch=1,
        grid=(Bd,),
        in_specs=[per_b(q), per_b(gates), per_b(kvs_new), per_b(win_t), per_b(kvw_new)]
                 + [page(p) for p in range(n_pages)] * 2
                 + [full(a) for a in cw] + [full(ovt), full(eg)],
        out_specs=[pl.BlockSpec((1, ls, NSA_WIDTH), lambda b, pt: (b, 0, 0)), per_b(win_t)],
        scratch_shapes=[pltpu.VMEM((past_len, LANES), F32)] * 2
                       + [pltpu.VMEM((n_cmp, LANES), BF16)] * 4
                       + [pltpu.VMEM((2 * LANES, past_len), BF16)] * 2
                       + [pltpu.VMEM((LANES, past_len), BF16)] * 2
                       + [pltpu.VMEM((LANES, w_rows), BF16)] * 2
                       + [pltpu.VMEM((KVH, HPG * BF16_ROWS, 2 * LANES), BF16)],
    )
    return pl.pallas_call(
        functools.partial(_nsa_sample_body, n_pages=n_pages, ls=ls, past_len=past_len),
        grid_spec=grid_spec,
        out_shape=[jax.ShapeDtypeStruct((Bd, ls, NSA_WIDTH), F32), jax.ShapeDtypeStruct(win_t.shape, F32)],
        compiler_params=_params(("arbitrary",)),
        name="nsa_sample",
    )(page_table, q, gates, kvs_new, win_t, kvw_new,
      *([cmp_t] * n_pages), *([slc_t] * n_pages), *cw, ovt, eg)


def _fin1_body(h_ref, pool_ref, nsa_ref, wo_ref, g_ref, b_ref, wq_ref, h1_ref, qm_ref):
    mix = (_dot(pool_ref[...].astype(BF16), wo_ref[0:POOL_WIDTH, :])
           + _dot(nsa_ref[...].astype(BF16), wo_ref[POOL_WIDTH:, :]))
    h1 = _layer_norm(DN_ALPHA * h_ref[...] + mix, g_ref[...], b_ref[...])
    h1_ref[...] = h1
    qm_ref[...] = (_dot(h1.astype(BF16), wq_ref[...]) * (MEM_HEAD_DIM ** -0.5)).astype(qm_ref.dtype)


def _fin1(h, pool_o, nsa_o, w_out_bf, g, b, wq_bf, q_dtype):
    T = h.shape[0]
    tm = ROW_TILE
    row = lambda n: pl.BlockSpec((tm, n), lambda i: (i, 0))
    full = lambda a: pl.BlockSpec(a.shape, lambda i: (0,) * a.ndim)
    return pl.pallas_call(
        _fin1_body,
        grid=(T // tm,),
        in_specs=[row(D_MODEL), row(POOL_WIDTH), row(NSA_WIDTH), full(w_out_bf), full(g), full(b), full(wq_bf)],
        out_specs=[row(D_MODEL), row(D_MODEL)],
        out_shape=[jax.ShapeDtypeStruct((T, D_MODEL), F32), jax.ShapeDtypeStruct((T, D_MODEL), q_dtype)],
        compiler_params=_params(("arbitrary",)),
        name="out_proj_ln1",
    )(h, pool_o, nsa_o, w_out_bf, g, b, wq_bf)


def _memattn_body(q_ref, kv_ref, o_ref):
    width = MEM_HEADS * MEM_HEAD_DIM
    for h in range(MEM_HEADS):
        cols = slice(h * MEM_HEAD_DIM, (h + 1) * MEM_HEAD_DIM)
        qh = q_ref[0, :, cols].astype(BF16)
        kh = kv_ref[0, :, cols].astype(BF16)
        vh = kv_ref[0, :, width + h * MEM_HEAD_DIM:width + (h + 1) * MEM_HEAD_DIM].astype(BF16)
        s = _dot_nt(qh, kh)
        e = jnp.exp(s - jnp.max(s, axis=-1, keepdims=True))
        o = _dot(e.astype(BF16), vh) * (1.0 / jnp.sum(e, axis=-1, keepdims=True))
        o_ref[0, :, cols] = o.astype(o_ref.dtype)


def _memattn_few_body(q_ref, kv_ref, o_ref):
    lq = q_ref.shape[1]
    n_keys = MEM_LEN * MEM_HEADS
    assert lq & (lq - 1) == 0 and MEM_HEADS & (MEM_HEADS - 1) == 0
    shape = (MEM_HEADS * lq, n_keys)
    own_head = (jnp.bitwise_and(lax.broadcasted_iota(I32, shape, 1), MEM_HEADS - 1)
                == jnp.right_shift(lax.broadcasted_iota(I32, shape, 0), lq.bit_length() - 1))
    for i in range(q_ref.shape[0]):
        q = q_ref[i]
        qs = jnp.concatenate([q[:, h * MEM_HEAD_DIM:(h + 1) * MEM_HEAD_DIM] for h in range(MEM_HEADS)], axis=0)
        k = kv_ref[i, :, 0, :, :].reshape(n_keys, MEM_HEAD_DIM).astype(BF16)
        v = kv_ref[i, :, 1, :, :].reshape(n_keys, MEM_HEAD_DIM).astype(BF16)
        s = jnp.where(own_head, _dot_nt(qs.astype(BF16), k), NEG)
        e = jnp.exp(s - jnp.max(s, axis=-1, keepdims=True))
        o = _dot(e.astype(BF16), v) * (1.0 / jnp.sum(e, axis=-1, keepdims=True))
        for h in range(MEM_HEADS):
            o_ref[i, :, h * MEM_HEAD_DIM:(h + 1) * MEM_HEAD_DIM] = o[h * lq:(h + 1) * lq]


def _memattn_few(qm, mem_kv):
    nb, lq, W = qm.shape
    per_step = MEM_PER_STEP if nb % MEM_PER_STEP == 0 else 1
    return pl.pallas_call(
        _memattn_few_body,
        grid=(nb // per_step,),
        in_specs=[pl.BlockSpec((per_step, lq, W), lambda b: (b, 0, 0)),
                  pl.BlockSpec((per_step, MEM_LEN, 2, MEM_HEADS, MEM_HEAD_DIM), lambda b: (b, 0, 0, 0, 0))],
        out_specs=pl.BlockSpec((per_step, lq, W), lambda b: (b, 0, 0)),
        out_shape=jax.ShapeDtypeStruct((nb, lq, W), F32),
        compiler_params=_params(("arbitrary",)),
        name="mem_attn_few",
    )(qm, mem_kv)


def _fin2_body(cnt0_ref, h1_ref, o_ref, wo_ref, g_ref, b_ref, rw_ref, rb_ref, *rest, n_own):
    h2_ref, te_ref, tg_ref, cnt_ref, run_ref = rest[-5:]
    step = pl.program_id(0)

    @pl.when(step >= n_own)
    def _():
        h2_ref[...] = jnp.zeros(h2_ref.shape, F32)
        te_ref[...] = jnp.zeros(te_ref.shape, I32)
        tg_ref[...] = jnp.zeros(tg_ref.shape, F32)

    @pl.when(step < n_own)
    def _():
        _fin2_rows(cnt0_ref, h1_ref, o_ref, wo_ref, g_ref, b_ref, rw_ref, rb_ref,
                   h2_ref, te_ref, tg_ref, cnt_ref, run_ref)


def _fin2_rows(cnt0_ref, h1_ref, o_ref, wo_ref, g_ref, b_ref, rw_ref, rb_ref,
               h2_ref, te_ref, tg_ref, cnt_ref, run_ref):
    tm = h1_ref.shape[0]

    @pl.when(pl.program_id(0) == 0)
    def _():
        run_ref[...] = cnt0_ref[...]

    a = _dot(o_ref[...].astype(BF16), wo_ref[...])
    h2 = _layer_norm(DN_ALPHA * h1_ref[...] + a, g_ref[...], b_ref[...])
    h2_ref[...] = h2
    a_hi, a_lo, _ = _split3(h2)
    w_hi, w_lo, _ = _split3(rw_ref[...])
    logits = _dot(a_hi, w_hi) + (_dot(a_hi, w_lo) + _dot(a_lo, w_hi)) + rb_ref[...]
    e_iota = lax.broadcasted_iota(I32, (tm, N_EXPERTS), 1).astype(F32)
    lane = lax.broadcasted_iota(I32, (tm, LANES), 1)
    te = jnp.zeros((tm, LANES), F32)
    tv = jnp.full((tm, LANES), NEG, F32)
    work = logits
    chosen = []
    for k in range(TOP_K):
        m = jnp.max(work, axis=-1, keepdims=True)
        idx = jnp.min(jnp.where(work == m, e_iota, float(N_EXPERTS)), axis=-1, keepdims=True)
        hit = e_iota == idx
        chosen.append(hit)
        te = jnp.where(lane == k, idx, te)
        tv = jnp.where(lane == k, m, tv)
        work = jnp.where(hit, -jnp.inf, work)
    member = sum(c.astype(F32) for c in chosen)
    earlier = (lax.broadcasted_iota(I32, (tm, tm), 0) > lax.broadcasted_iota(I32, (tm, tm), 1)).astype(BF16)
    before = _dot(earlier, member.astype(BF16)) + run_ref[...]
    for k in range(TOP_K):
        rank = jnp.sum(jnp.where(chosen[k], before, 0.0), axis=-1, keepdims=True)
        te = jnp.where(lane == TOP_K + k, rank, te)
    run_ref[...] = run_ref[...] + jnp.sum(member, axis=0, keepdims=True)
    cnt_ref[...] = run_ref[...]
    ex = jnp.exp(tv - jnp.max(tv, axis=-1, keepdims=True))
    te_ref[...] = te.astype(I32)
    tg_ref[...] = ex / jnp.sum(ex, axis=-1, keepdims=True)


def _fin2(cnt0, h1, o, wo_bf, g, b, rw, rb, total_rows, row_offset=0, into=None):
    T = h1.shape[0]
    tm = ROW_TILE
    blk0 = row_offset // tm
    n_own = T // tm
    n_steps = n_own if into is not None else (total_rows - row_offset) // tm
    row = lambda n: pl.BlockSpec((tm, n), lambda i: (jnp.minimum(i, n_own - 1), 0))
    out_row = lambda n: pl.BlockSpec((tm, n), lambda i: (i + blk0, 0))
    full = lambda a: pl.BlockSpec(a.shape, lambda i: (0,) * a.ndim)
    ins = [cnt0, h1, o, wo_bf, g, b, rw, rb]
    in_specs = [full(cnt0), row(D_MODEL), row(D_MODEL), full(wo_bf), full(g), full(b), full(rw), full(rb)]
    aliases = {}
    if into is not None:
        aliases = {len(ins) + k: k for k in range(len(into))}
        in_specs = in_specs + [pl.BlockSpec(memory_space=pl.ANY)] * len(into)
        ins = ins + list(into)
    return pl.pallas_call(
        functools.partial(_fin2_body, n_own=n_own),
        grid=(n_steps,),
        in_specs=in_specs,
        out_specs=[out_row(D_MODEL), out_row(LANES), out_row(LANES), full(cnt0)],
        out_shape=[jax.ShapeDtypeStruct((total_rows, D_MODEL), F32), jax.ShapeDtypeStruct((total_rows, LANES), I32),
                   jax.ShapeDtypeStruct((total_rows, LANES), F32), jax.ShapeDtypeStruct(cnt0.shape, F32)],
        scratch_shapes=[pltpu.VMEM(cnt0.shape, F32)],
        input_output_aliases=aliases,
        compiler_params=_params(("arbitrary",)),
        name="mem_out_ln2_router",
    )(*ins)


def _row_chain_body(cnt0_ref, h_ref, pool_ref, nsa_ref, wout_ref, g1_ref, b1_ref, wq_ref, kv_ref,
                    wo_ref, g2_ref, b2_ref, rw_ref, rb_ref,
                    h2_ref, te_ref, tg_ref, cnt_ref, h1_s, q_s, o_s, run_ref, *, n_own):
    step = pl.program_id(0)

    @pl.when(step >= n_own)
    def _():
        h2_ref[...] = jnp.zeros(h2_ref.shape, F32)
        te_ref[...] = jnp.zeros(te_ref.shape, I32)
        tg_ref[...] = jnp.zeros(tg_ref.shape, F32)

    @pl.when(step < n_own)
    def _():
        _fin1_body(h_ref, pool_ref, nsa_ref, wout_ref, g1_ref, b1_ref, wq_ref, h1_s, q_s.at[0])
        _memattn_body(q_s, kv_ref, o_s)
        _fin2_rows(cnt0_ref, h1_s, o_s.at[0], wo_ref, g2_ref, b2_ref, rw_ref, rb_ref,
                   h2_ref, te_ref, tg_ref, cnt_ref, run_ref)


def _row_chain(cnt0, h, pool_o, nsa_o, w_out_bf, g1, b1, wq_bf, mem_kv, wo_bf, g2, b2, rw, rb, total_rows):
    T = h.shape[0]
    tm = ROW_TILE
    n_own = T // tm
    tiles_per_seq = n_own // mem_kv.shape[0]
    own = lambda i: jnp.minimum(i, n_own - 1)
    row = lambda n: pl.BlockSpec((tm, n), lambda i: (own(i), 0))
    out_row = lambda n: pl.BlockSpec((tm, n), lambda i: (i, 0))
    full = lambda a: pl.BlockSpec(a.shape, lambda i: (0,) * a.ndim)
    return pl.pallas_call(
        functools.partial(_row_chain_body, n_own=n_own),
        grid=(total_rows // tm,),
        in_specs=[full(cnt0), row(D_MODEL), row(POOL_WIDTH), row(NSA_WIDTH), full(w_out_bf), full(g1), full(b1),
                  full(wq_bf), pl.BlockSpec((1,) + mem_kv.shape[1:], lambda i: (own(i) // tiles_per_seq, 0, 0)),
                  full(wo_bf), full(g2), full(b2), full(rw), full(rb)],
        out_specs=[out_row(D_MODEL), out_row(LANES), out_row(LANES), full(cnt0)],
        out_shape=[jax.ShapeDtypeStruct((total_rows, D_MODEL), F32), jax.ShapeDtypeStruct((total_rows, LANES), I32),
                   jax.ShapeDtypeStruct((total_rows, LANES), F32), jax.ShapeDtypeStruct(cnt0.shape, F32)],
        scratch_shapes=[pltpu.VMEM((tm, D_MODEL), F32), pltpu.VMEM((1, tm, D_MODEL), BF16),
                        pltpu.VMEM((1, tm, D_MODEL), BF16), pltpu.VMEM(cnt0.shape, F32)],
        compiler_params=_params(("arbitrary",)),
        name="out_proj_mem_attn_router",
    )(cnt0, h, pool_o, nsa_o, w_out_bf, g1, b1, wq_bf, mem_kv, wo_bf, g2, b2, rw, rb)


def _moe_body(ut_ref, ue_ref, nu_ref, rs_ref, re_ref, x_ref, wgu_ref, bgu_ref, wdn_ref, bdn_ref,
              y_ref, wgu_bf, wdn_bf):
    u = pl.program_id(0)
    bk = x_ref.shape[0]
    e = ue_ref[u]
    tile = ut_ref[u]
    prev = jnp.maximum(u - 1, 0)

    @pl.when((u == 0) | (e != ue_ref[prev]))
    def _():
        wgu_bf[...] = wgu_ref[0].astype(BF16)
        wdn_bf[...] = wdn_ref[0].astype(BF16)

    @pl.when(u < nu_ref[0])
    def _():
        x = x_ref[...].astype(BF16)
        g = _dot(x, wgu_bf[:, :D_FF]) + bgu_ref[0, :, :D_FF]
        v = _dot(x, wgu_bf[:, D_FF:]) + bgu_ref[0, :, D_FF:]
        g = jnp.minimum(g, SWIGLU_LIMIT)
        v = jnp.clip(v, -SWIGLU_LIMIT, SWIGLU_LIMIT)
        a = g * (1.0 / (1.0 + jnp.exp(-SWIGLU_ALPHA * g))) * (v + 1.0)
        y = _dot(a.astype(BF16), wdn_bf[...]) + bdn_ref[0]
        row = tile * bk + lax.broadcasted_iota(I32, (bk, 1), 0)
        mine = (row >= rs_ref[e]) & (row < re_ref[e])
        y = jnp.where(mine, y, 0.0)

        @pl.when((u == 0) | (tile != ut_ref[prev]))
        def _():
            y_ref[...] = y

        @pl.when((u > 0) & (tile == ut_ref[prev]))
        def _():
            y_ref[...] = y_ref[...] + y


def _moe_gmm(x_rows, units, w_gu, b_gu, w_dn, b_dn):
    N = x_rows.shape[0]
    bk = MOE_ROWS
    unit_tile, unit_e, n_units, r_start, r_end = units
    grid_spec = pltpu.PrefetchScalarGridSpec(
        num_scalar_prefetch=5,
        grid=(unit_tile.shape[0],),
        in_specs=[pl.BlockSpec((bk, D_MODEL), lambda u, ut, ue, *_: (ut[u], 0)),
                  pl.BlockSpec((1, D_MODEL, 2 * D_FF), lambda u, ut, ue, *_: (ue[u], 0, 0)),
                  pl.BlockSpec((1, 1, 2 * D_FF), lambda u, ut, ue, *_: (ue[u], 0, 0)),
                  pl.BlockSpec((1, D_FF, D_MODEL), lambda u, ut, ue, *_: (ue[u], 0, 0)),
                  pl.BlockSpec((1, 1, D_MODEL), lambda u, ut, ue, *_: (ue[u], 0, 0))],
        out_specs=pl.BlockSpec((bk, D_MODEL), lambda u, ut, ue, *_: (ut[u], 0)),
        scratch_shapes=[pltpu.VMEM((D_MODEL, 2 * D_FF), BF16), pltpu.VMEM((D_FF, D_MODEL), BF16)],
    )
    return pl.pallas_call(
        _moe_body,
        grid_spec=grid_spec,
        out_shape=jax.ShapeDtypeStruct((N, D_MODEL), F32),
        compiler_params=_params(("arbitrary",)),
        name="moe_experts",
    )(unit_tile, unit_e, n_units, r_start, r_end, x_rows, w_gu, b_gu, w_dn, b_dn)


FLAT_BITS = 17


def _moe_routing(te, counts):
    bk = MOE_ROWS
    T = te.shape[0]
    N = T * TOP_K
    assert N % bk == 0 and N <= (1 << FLAT_BITS)
    experts = jnp.arange(N_EXPERTS, dtype=I32)
    top_e = te[:, :TOP_K]
    r_end = jnp.cumsum(counts).astype(I32)
    r_start = r_end - counts
    onehot = top_e[:, :, None] == experts[None, None, :]
    pos = jnp.sum(jnp.where(onehot, r_start[None, None, :], 0), axis=-1) + te[:, TOP_K:2 * TOP_K]
    key = jnp.left_shift(top_e.reshape(-1), FLAT_BITS) + jnp.arange(N, dtype=I32)
    key_s = lax.sort(key)
    tok_s = jnp.right_shift(jnp.bitwise_and(key_s, (1 << FLAT_BITS) - 1), 2)
    first = r_start // bk
    last = (r_end - 1) // bk
    n_e = jnp.where(counts > 0, last - first + 1, 0)
    u_end = jnp.cumsum(n_e).astype(I32)
    u_start = u_end - n_e
    n_units = u_end[-1]
    u = jnp.minimum(jnp.arange(N // bk + N_EXPERTS - 1, dtype=I32), n_units - 1)
    unit_e = jnp.sum((u[:, None] >= u_end[None, :]).astype(I32), axis=1)
    unit_tile = u + jnp.sum(jnp.where(unit_e[:, None] == experts[None, :], (first - u_start)[None, :], 0), axis=1)
    return pos, tok_s, (unit_tile, unit_e, n_units.reshape(1), r_start, r_end)


def _fin3_body(h2_ref, tg_ref, y0_ref, y1_ref, y2_ref, y3_ref, g_ref, b_ref, o_ref):
    gate = lambda k: tg_ref[:, k:k + 1]
    y = ((gate(0) * y0_ref[...] + gate(1) * y1_ref[...])
         + (gate(2) * y2_ref[...] + gate(3) * y3_ref[...]))
    o_ref[...] = _layer_norm(DN_ALPHA * h2_ref[...] + y, g_ref[...], b_ref[...])


def _fin3(h2, tg, ys, g, b, row_offset, T):
    tm = ROW_TILE
    blk0 = row_offset // tm
    row = lambda n: pl.BlockSpec((tm, n), lambda i: (i, 0))
    full = lambda a: pl.BlockSpec(a.shape, lambda i: (0,) * a.ndim)
    return pl.pallas_call(
        _fin3_body,
        grid=(T // tm,),
        in_specs=[pl.BlockSpec((tm, D_MODEL), lambda i: (i + blk0, 0)), pl.BlockSpec((tm, LANES), lambda i: (i + blk0, 0))]
                 + [pl.BlockSpec((tm, D_MODEL), lambda i: (i + blk0, 0))] * TOP_K + [full(g), full(b)],
        out_specs=row(D_MODEL),
        out_shape=jax.ShapeDtypeStruct((T, D_MODEL), F32),
        compiler_params=_params(("arbitrary",)),
        name="combine_ln3",
    )(h2, tg, *ys, g, b)


def kernel(x_prompt, x_sample, cache_cmp_kv, cache_slc_kv, state_win_kv, state_pool, cache_mem_kv, page_table,
           mem_prompt, w_in, pool_w, pool_scale, cmp_pe, cmp_w1, cmp_b1, cmp_w2, cmp_b2, w_out, ln1_g, ln1_b,
           mem_wq, mem_wkv, mem_wo, ln2_g, ln2_b, router_w, router_b, exp_w_gu, exp_b_gu, exp_w_dn, exp_b_dn,
           ln3_g, ln3_b):
    Bp, S, D = x_prompt.shape
    Bd, Ls, _ = x_sample.shape
    Tp, Ts = Bp * S, Bd * Ls
    l = 0
    w_in_bf = w_in[l].astype(BF16)
    pool_w_bf = pool_w[l].astype(BF16)
    ps = pool_scale[l][None, :]
    cw = _compress_weights(cmp_pe[l], cmp_w1[l], cmp_b1[l], cmp_w2[l], cmp_b2[l])
    w_out_bf = w_out[l].astype(BF16)
    wq_bf = mem_wq[l].astype(BF16)
    wo_bf = mem_wo[l].astype(BF16)
    vec = lambda a: a[l][None, :]

    up, qp, kvc_p, kvs_p, kvw_p, gp, pool_p, kvc_t, kvs_t = _inproj_prompt(
        x_prompt.reshape(Tp, D), w_in_bf, pool_w_bf, ps, S)
    kc_p = _compress_prompt(kvc_p.reshape(Bp, S, KV_WIDTH), cw)
    nsa_p = _nsa_prompt(qp, gp, kc_p, kvs_p.reshape(Bp, S, KV_WIDTH), kvw_p.reshape(Bp, S, KV_WIDTH))
    mem_kv_p = _matmul(mem_prompt.reshape(Bp * MEM_LEN, D), mem_wkv[l]).reshape(Bp, MEM_LEN, 2 * D)
    T = Tp + Ts
    *routed_p, cnt_p = _row_chain(jnp.zeros((1, N_EXPERTS), F32), x_prompt.reshape(Tp, D), pool_p, nsa_p, w_out_bf,
                                  vec(ln1_g), vec(ln1_b), wq_bf, mem_kv_p, wo_bf, vec(ln2_g), vec(ln2_b),
                                  router_w[l], vec(router_b), total_rows=T)

    state_pad = jnp.pad(state_pool[l], ((0, 0), (1, 0), (0, 0)))
    us, qs, kvc_s, kvs_s, kvw_s, gs, pool_s = _inproj_sample(
        x_sample.reshape(Ts, D), w_in_bf, pool_w_bf, ps, state_pad, Ls, PAST_LEN)
    feature_major = lambda a: jnp.transpose(a, (0, 2, 3, 4, 1))
    nsa_s, win_next = _nsa_sample(qs.reshape(Bd, Ls, NSA_WIDTH), gs.reshape(Bd, Ls, 3 * NSA_HEADS),
                        kvs_s.reshape(Bd, Ls, KV_WIDTH), kvw_s.reshape(Bd, Ls, KV_WIDTH),
                        feature_major(state_win_kv[l]), feature_major(cache_cmp_kv[l]),
                        feature_major(cache_slc_kv[l]), page_table, cw, PAST_LEN)
    h1_s, qm_s = _fin1(x_sample.reshape(Ts, D), pool_s, nsa_s.reshape(Ts, NSA_WIDTH), w_out_bf,
                       vec(ln1_g), vec(ln1_b), wq_bf, F32)
    om_s = _memattn_few(qm_s.reshape(Bd, Ls, D), cache_mem_kv[l]).reshape(Ts, D)
    h2, te, tg, cnt_s = _fin2(cnt_p, h1_s, om_s, wo_bf, vec(ln2_g), vec(ln2_b), router_w[l], vec(router_b),
                              total_rows=T, row_offset=Tp, into=routed_p)

    pos, tok_s, units = _moe_routing(te, cnt_s[0].astype(I32))
    y_rows = _moe_gmm(h2[tok_s], units, exp_w_gu[l], exp_b_gu[l][:, None, :],
                      exp_w_dn[l], exp_b_dn[l][:, None, :])
    ys = [y_rows[pos[:, k]] for k in range(TOP_K)]
    y_prompt = _fin3(h2, tg, ys, vec(ln3_g), vec(ln3_b), 0, Tp).reshape(Bp, S, D)
    y_sample = _fin3(h2, tg, ys, vec(ln3_g), vec(ln3_b), Tp, Ts).reshape(Bd, Ls, D)

    kv6 = lambda a, b, n: a.reshape(1, b, n, 2, KVH, DH)
    row_major = lambda a: jnp.transpose(a, (0, 4, 1, 2, 3))
    win_p = kvw_p.reshape(Bp, S, KV_WIDTH)[:, S - min(WINDOW, S):]
    pool_state_p = up.reshape(Bp, S, POOL_WIDTH)[:, S - POOL_STATE:]
    pool_state_s = jnp.concatenate([state_pool[l], us.reshape(Bd, Ls, POOL_WIDTH)], axis=1)[:, -POOL_STATE:]
    return (y_prompt, y_sample,
            row_major(kvc_t)[None], row_major(kvs_t)[None], kv6(win_p, Bp, min(WINDOW, S)),
            pool_state_p[None], mem_kv_p.reshape(1, Bp, MEM_LEN, 2, MEM_HEADS, MEM_HEAD_DIM),
            kv6(kvc_s, Bd, Ls), kv6(kvs_s, Bd, Ls), row_major(win_next)[None], pool_state_s[None])
```
